```python
import math
import jax, jax.numpy as jnp
from jax import lax
import numpy as np

D_MODEL = 1024
BATCH = 8
SEQ = 8192
DEPTH = 1

D_MIX = D_MODEL
HG_HEADS = 4
HG_DK = 128
HG_DV = 128
HG_WIDTH = HG_HEADS * HG_DV
HG_CHUNK = 64
AT_HEADS = 8
AT_HEAD_DIM = 64
AT_WIDTH = AT_HEADS * AT_HEAD_DIM
DILATED_PATTERNS = ((128, 1), (512, 4), (2048, 16))
ATT_BLOCK = 128
ROPE_THETA = 10000.0
NORM_EPS = 1e-6
IN_SPLITS = (HG_HEADS * HG_DK, HG_HEADS * HG_DK, HG_WIDTH, HG_WIDTH,
             AT_WIDTH, AT_WIDTH, AT_WIDTH, AT_WIDTH)
IN_COLS = sum(IN_SPLITS)

kernel_name = "hgrn2_dilated_attn_parallel_hybrid"


def rms_norm(x, w):
    xf = x.astype(jnp.float32)
    y = xf * lax.rsqrt(jnp.mean(xf * xf, axis=-1, keepdims=True) + NORM_EPS)
    return y * w.astype(jnp.float32)


def rope(x, positions):
    half = x.shape[-1] // 2
    inv_freq = 1.0 / (ROPE_THETA ** (jnp.arange(half, dtype=jnp.float32) / half))
    ang = positions.astype(jnp.float32)[:, None] * inv_freq[None, :]
    cos = jnp.cos(ang)[:, None, :]
    sin = jnp.sin(ang)[:, None, :]
    x1, x2 = x[..., :half], x[..., half:]
    return jnp.concatenate([x1 * cos - x2 * sin, x2 * cos + x1 * sin], axis=-1)


def hgrn2_mixer(q, f_logit, i, lb):
    B, S, H, DK = q.shape
    DV = i.shape[-1]
    C = HG_CHUNK
    nC = S // C
    log_f = jnp.logaddexp(jnp.log(lb), jnp.log1p(-lb) + jax.nn.log_sigmoid(f_logit))
    k = -jnp.expm1(log_f)
    q = jax.nn.silu(q)

    def chunks(t):
        return t.reshape(B, nC, C, H, t.shape[-1]).transpose(1, 0, 3, 2, 4)

    causal = (jnp.arange(C)[:, None] >= jnp.arange(C)[None, :])[:, :, None]

    def step(state, inp):
        qc, kc, vc, gc = inp
        b = jnp.cumsum(gc, axis=2)
        o_inter = jnp.einsum('bhck,bhkv->bhcv', qc * jnp.exp(b), state)
        diff = b[:, :, :, None, :] - b[:, :, None, :, :]
        decay = jnp.exp(jnp.where(causal, diff, -jnp.inf))
        scores = jnp.einsum('bhtk,bhtsk,bhsk->bhts', qc, decay, kc)
        o_intra = jnp.einsum('bhts,bhsv->bhtv', scores, vc)
        b_last = b[:, :, -1:, :]
        k_dec = kc * jnp.exp(b_last - b)
        new_state = jnp.exp(b_last[:, :, 0, :])[..., None] * state + \
            jnp.einsum('bhsk,bhsv->bhkv', k_dec, vc)
        return new_state, o_inter + o_intra

    state0 = jnp.zeros((B, H, DK, DV), jnp.float32)
    _, out = lax.scan(step, state0, (chunks(q), chunks(k), chunks(i), chunks(log_f)))
    return out.transpose(1, 0, 3, 2, 4).reshape(B, S, H, DV)


def dilated_window_attention(q, k, v, window, dilation):
    B, S, H, Dh = q.shape
    span = window // dilation
    assert span <= ATT_BLOCK
    BLK = ATT_BLOCK
    L = S // dilation
    nb = -(-L // BLK)
    Lp = nb * BLK

    def to_sub(t):
        t = t.reshape(B, L, dilation, H, Dh).transpose(0, 2, 3, 1, 4)
        return jnp.pad(t, ((0, 0), (0, 0), (0, 0), (0, Lp - L), (0, 0)))

    def band(t):
        tp = jnp.pad(t, ((0, 0), (0, 0), (0, 0), (BLK, 0), (0, 0)))
        tp = tp.reshape(B, dilation, H, nb + 1, BLK, Dh)
        return jnp.concatenate([tp[:, :, :, :-1], tp[:, :, :, 1:]], axis=-2)

    qb = to_sub(q).reshape(B, dilation, H, nb, BLK, Dh)
    kb = band(to_sub(k))
    vb = band(to_sub(v))
    s = jnp.einsum('bdhnqc,bdhnkc->bdhnqk', qb, kb) * (Dh ** -0.5)
    qi = jnp.arange(BLK)[:, None]
    kj = jnp.arange(2 * BLK)[None, :]
    dist = qi + BLK - kj
    key_idx = jnp.arange(nb)[:, None, None] * BLK + kj[None] - BLK
    valid = (dist >= 0)[None] & (dist <= span)[None] & (key_idx >= 0)
    s = jnp.where(valid, s, -jnp.inf)
    m = jnp.max(s, axis=-1, keepdims=True)
    p = jnp.exp(s - m)
    l = jnp.sum(p, axis=-1, keepdims=True)
    o = jnp.einsum('bdhnqk,bdhnkc->bdhnqc', p, vb) / l
    lse = (m + jnp.log(l))[..., 0]
    o = o.reshape(B, dilation, H, Lp, Dh)[:, :, :, :L].transpose(0, 3, 1, 2, 4).reshape(B, S, H, Dh)
    lse = lse.reshape(B, dilation, H, Lp)[..., :L].transpose(0, 3, 1, 2).reshape(B, S, H)
    return o, lse


def _fwd_setup_inputs(seed: int = 0) -> dict:
    key = jax.random.key(seed)
    ks = jax.random.split(key, 8)
    x = jax.random.normal(ks[0], (BATCH, SEQ, D_MODEL), jnp.float32)
    norm_w = 1.0 + 0.02 * jax.random.normal(ks[1], (DEPTH, D_MODEL), jnp.float32)
    w_in = jax.random.normal(ks[2], (DEPTH, D_MODEL, IN_COLS), jnp.float32) * D_MODEL ** -0.5
    hgrn_lb_logits = 0.5 * jax.random.normal(ks[3], (DEPTH + 1, HG_HEADS * HG_DK), jnp.float32)
    hg_norm_w = 1.0 + 0.02 * jax.random.normal(ks[4], (DEPTH, HG_HEADS * HG_DV), jnp.float32)
    w_out = jax.random.normal(ks[5], (DEPTH, D_MIX, D_MODEL), jnp.float32) * D_MIX ** -0.5
    final_norm_w = 1.0 + 0.02 * jax.random.normal(ks[6], (D_MODEL,), jnp.float32)
    return {"x": x, "norm_w": norm_w, "w_in": w_in, "hgrn_lb_logits": hgrn_lb_logits,
            "hg_norm_w": hg_norm_w, "w_out": w_out, "final_norm_w": final_norm_w}


def _fwd_reference(x, norm_w, w_in, hgrn_lb_logits, hg_norm_w, w_out, final_norm_w):
    B, S, _ = x.shape
    positions = jnp.arange(S, dtype=jnp.int32)
    lb_table = jnp.cumsum(jax.nn.softmax(hgrn_lb_logits.astype(jnp.float32), axis=0), axis=0)
    split_points = list(np.cumsum(IN_SPLITS)[:-1])
    h = x
    for layer in range(DEPTH):
        u = rms_norm(h, norm_w[layer])
        proj = jnp.einsum('bsd,dc->bsc', u, w_in[layer].astype(jnp.float32))
        hg_q, hg_f, hg_i, hg_z, at_q, at_k, at_v, at_z = jnp.split(proj, split_points, axis=-1)

        lb = jnp.clip(lb_table[layer], 1e-6, 1.0 - 1e-6).reshape(HG_HEADS, HG_DK)
        o_hg = hgrn2_mixer(hg_q.reshape(B, S, HG_HEADS, HG_DK),
                           hg_f.reshape(B, S, HG_HEADS, HG_DK),
                           hg_i.reshape(B, S, HG_HEADS, HG_DV), lb)
        g = hg_norm_w[layer].astype(jnp.float32).reshape(HG_HEADS, HG_DV)
        o_hg = o_hg * lax.rsqrt(jnp.mean(o_hg * o_hg, axis=-1, keepdims=True) + NORM_EPS) * g
        o_hg = o_hg.reshape(B, S, HG_WIDTH) * jax.nn.silu(hg_z)

        q = rope(at_q.reshape(B, S, AT_HEADS, AT_HEAD_DIM), positions)
        k = rope(at_k.reshape(B, S, AT_HEADS, AT_HEAD_DIM), positions)
        v = at_v.reshape(B, S, AT_HEADS, AT_HEAD_DIM)
        outs, lses = [], []
        for window, dilation in DILATED_PATTERNS:
            o_i, lse_i = dilated_window_attention(q, k, v, window, dilation)
            outs.append(o_i)
            lses.append(lse_i)
        mix_w = jax.nn.softmax(jnp.stack(lses, axis=0), axis=0)[..., None]
        o_at = jnp.sum(mix_w * jnp.stack(outs, axis=0), axis=0)
        o_at = o_at.reshape(B, S, AT_WIDTH) * jax.nn.silu(at_z)

        mixed = jnp.concatenate([o_hg, o_at], axis=-1)
        y = jnp.einsum('bsc,cd->bsd', mixed, w_out[layer].astype(jnp.float32))
        h = h + y.astype(h.dtype)
    return rms_norm(h, final_norm_w).astype(x.dtype)


import jax as _jax
import jax.numpy as _jnp

TWIN_FORMAT = 'train_step'
FWD_PARAMS = ['x', 'norm_w', 'w_in', 'hgrn_lb_logits', 'hg_norm_w', 'w_out', 'final_norm_w']
TWIN_WEIGHTS = ['norm_w', 'w_in', 'hgrn_lb_logits', 'hg_norm_w', 'w_out', 'final_norm_w']
TWIN_DIFF_INPUT = 'x'
TWIN_INPUTS = ['x', 'norm_w', 'w_in', 'hgrn_lb_logits', 'hg_norm_w', 'w_out', 'final_norm_w', 'loss_target', 'm_norm_w', 'm_w_in', 'm_hgrn_lb_logits', 'm_hg_norm_w', 'm_w_out', 'm_final_norm_w', 'v_norm_w', 'v_w_in', 'v_hgrn_lb_logits', 'v_hg_norm_w', 'v_w_out', 'v_final_norm_w']
TWIN_OUTPUTS = ['loss', 'grad_x', 'grad_norm_w', 'grad_w_in', 'grad_hgrn_lb_logits', 'grad_hg_norm_w', 'grad_w_out', 'grad_final_norm_w', 'delta_norm_w', 'delta_w_in', 'delta_hgrn_lb_logits', 'delta_hg_norm_w', 'delta_w_out', 'delta_final_norm_w', 'new_m_norm_w', 'new_m_w_in', 'new_m_hgrn_lb_logits', 'new_m_hg_norm_w', 'new_m_w_out', 'new_m_final_norm_w', 'new_v_norm_w', 'new_v_w_in', 'new_v_hgrn_lb_logits', 'new_v_hg_norm_w', 'new_v_w_out', 'new_v_final_norm_w']
TWIN_LEAF_KINDS = {'loss': 'loss', 'grad_x': 'grad_x', 'grad_norm_w': 'grad_w', 'grad_w_in': 'grad_w', 'grad_hgrn_lb_logits': 'grad_w', 'grad_hg_norm_w': 'grad_w', 'grad_w_out': 'grad_w', 'grad_final_norm_w': 'grad_w', 'delta_norm_w': 'delta_w', 'delta_w_in': 'delta_w', 'delta_hgrn_lb_logits': 'delta_w', 'delta_hg_norm_w': 'delta_w', 'delta_w_out': 'delta_w', 'delta_final_norm_w': 'delta_w', 'new_m_norm_w': 'new_m', 'new_m_w_in': 'new_m', 'new_m_hgrn_lb_logits': 'new_m', 'new_m_hg_norm_w': 'new_m', 'new_m_w_out': 'new_m', 'new_m_final_norm_w': 'new_m', 'new_v_norm_w': 'new_v', 'new_v_w_in': 'new_v', 'new_v_hgrn_lb_logits': 'new_v', 'new_v_hg_norm_w': 'new_v', 'new_v_w_out': 'new_v', 'new_v_final_norm_w': 'new_v'}


def _forward(args):
    return _fwd_reference(*[args[k] for k in FWD_PARAMS])


def _output_shape():
    def fwd():
        inp = _fwd_setup_inputs(0)
        return _fwd_reference(*[inp[k] for k in FWD_PARAMS])
    out = _jax.eval_shape(fwd)
    return out.shape, out.dtype

N_MICROBATCH = 1
ADAM_LR = 0.001
ADAM_B1 = 0.9
ADAM_B2 = 0.999
ADAM_EPS = 1e-08
ADAM_WD = 0.01
ADAM_STEP = 10
PER_EXAMPLE_BATCH_AXIS = {'x': 0, 'loss_target': 0}
SHARED_INPUTS = []
_WEIGHT_DTYPES = {'norm_w': _jnp.float32, 'w_in': _jnp.float32, 'hgrn_lb_logits': _jnp.float32, 'hg_norm_w': _jnp.float32, 'w_out': _jnp.float32, 'final_norm_w': _jnp.float32}
MOMENT_SCALE = {'norm_w': 1.450089e-01, 'w_in': 7.296852e-02, 'hgrn_lb_logits': 1.182935e-02, 'hg_norm_w': 1.480042e-01, 'w_out': 9.884687e-02, 'final_norm_w': 6.401626e+01}


def _to_microbatches(a, axis):
    t = _jnp.moveaxis(a, axis, 0)
    t = t.reshape((N_MICROBATCH, t.shape[0] // N_MICROBATCH) + t.shape[1:])
    return _jnp.moveaxis(t, 1, axis + 1)


def setup_inputs(seed: int = 0) -> dict:
    inp = _fwd_setup_inputs(seed)
    key = _jax.random.fold_in(_jax.random.key(seed), 7919)
    shape, _ = _output_shape()
    out = dict(inp)
    out["loss_target"] = _jax.random.normal(_jax.random.fold_in(key, 0), shape, _jnp.float32)
    for i, name in enumerate(TWIN_WEIGHTS):
        w = inp[name].astype(_jnp.float32)
        if MOMENT_SCALE is None:
            s = _jnp.sqrt(_jnp.mean(_jnp.square(w)) + 1e-30)
        else:
            s = MOMENT_SCALE[name]
        km, kv = _jax.random.split(_jax.random.fold_in(key, i + 1))
        out[name] = w
        out["m_" + name] = s * _jax.random.normal(km, w.shape, _jnp.float32)
        out["v_" + name] = (s * s) * _jax.random.uniform(kv, w.shape, _jnp.float32, 0.5, 1.5)
    if N_MICROBATCH > 1:
        for name, axis in PER_EXAMPLE_BATCH_AXIS.items():
            out[name] = _to_microbatches(out[name], axis)
    return {'x': out['x'], 'norm_w': out['norm_w'], 'w_in': out['w_in'], 'hgrn_lb_logits': out['hgrn_lb_logits'], 'hg_norm_w': out['hg_norm_w'], 'w_out': out['w_out'], 'final_norm_w': out['final_norm_w'], 'loss_target': out['loss_target'], 'm_norm_w': out['m_norm_w'], 'm_w_in': out['m_w_in'], 'm_hgrn_lb_logits': out['m_hgrn_lb_logits'], 'm_hg_norm_w': out['m_hg_norm_w'], 'm_w_out': out['m_w_out'], 'm_final_norm_w': out['m_final_norm_w'], 'v_norm_w': out['v_norm_w'], 'v_w_in': out['v_w_in'], 'v_hgrn_lb_logits': out['v_hgrn_lb_logits'], 'v_hg_norm_w': out['v_hg_norm_w'], 'v_w_out': out['v_w_out'], 'v_final_norm_w': out['v_final_norm_w']}


def _loss(weights, diff, rest, loss_target):
    with _jax.named_scope("forward"):
        args = {**rest, TWIN_DIFF_INPUT: diff, **{k: w.astype(_WEIGHT_DTYPES[k]) for k, w in weights.items()}}
        y = _forward(args)
    with _jax.named_scope("loss_head"):
        err = _jnp.square(y.astype(_jnp.float32) - loss_target)
        return 0.5 * _jnp.sum(_jnp.mean(err, axis=-1)) if err.ndim else 0.5 * err


def _adamw(w, g, m, v):
    m = ADAM_B1 * m + (1.0 - ADAM_B1) * g
    v = ADAM_B2 * v + (1.0 - ADAM_B2) * _jnp.square(g)
    m_hat = m / (1.0 - ADAM_B1 ** ADAM_STEP)
    v_hat = v / (1.0 - ADAM_B2 ** ADAM_STEP)
    delta = -ADAM_LR * (m_hat / (_jnp.sqrt(v_hat) + ADAM_EPS) + ADAM_WD * w)
    return delta, m, v


def reference(x, norm_w, w_in, hgrn_lb_logits, hg_norm_w, w_out, final_norm_w, loss_target, m_norm_w, m_w_in, m_hgrn_lb_logits, m_hg_norm_w, m_w_out, m_final_norm_w, v_norm_w, v_w_in, v_hgrn_lb_logits, v_hg_norm_w, v_w_out, v_final_norm_w):
    given = dict(x=x, norm_w=norm_w, w_in=w_in, hgrn_lb_logits=hgrn_lb_logits, hg_norm_w=hg_norm_w, w_out=w_out, final_norm_w=final_norm_w, loss_target=loss_target, m_norm_w=m_norm_w, m_w_in=m_w_in, m_hgrn_lb_logits=m_hgrn_lb_logits, m_hg_norm_w=m_hg_norm_w, m_w_out=m_w_out, m_final_norm_w=m_final_norm_w, v_norm_w=v_norm_w, v_w_in=v_w_in, v_hgrn_lb_logits=v_hgrn_lb_logits, v_hg_norm_w=v_hg_norm_w, v_w_out=v_w_out, v_final_norm_w=v_final_norm_w)
    weights = {n: given[n] for n in TWIN_WEIGHTS}
    shared = {n: given[n] for n in SHARED_INPUTS}
    per_example = {n: given[n] for n in ['x']}
    grad_fn = _jax.value_and_grad(_loss, argnums=(0, 1))

    def one_microbatch(ex, loss_target):
        ex = dict(ex)
        diff = ex.pop(TWIN_DIFF_INPUT)
        return grad_fn(weights, diff, {**shared, **ex}, loss_target)

    if N_MICROBATCH == 1:
        loss, (grad_w, grad_x) = one_microbatch(per_example, given["loss_target"])
    else:
        def body(carry, xs):
            loss_sum, grad_sum = carry
            l_k, (gw_k, gx_k) = one_microbatch(xs[0], xs[1])
            with _jax.named_scope("update"):
                return (loss_sum + l_k, _jax.tree.map(_jnp.add, grad_sum, gw_k)), gx_k

        init = (_jnp.zeros((), _jnp.float32), _jax.tree.map(_jnp.zeros_like, weights))
        (loss, grad_w), grad_x = _jax.lax.scan(body, init, (per_example, given["loss_target"]))
    with _jax.named_scope("update"):
        delta_w, new_m, new_v = {}, {}, {}
        for n in TWIN_WEIGHTS:
            delta_w[n], new_m[n], new_v[n] = _adamw(weights[n], grad_w[n], given["m_" + n], given["v_" + n])
    return (loss, grad_x, *[grad_w[n] for n in TWIN_WEIGHTS], *[delta_w[n] for n in TWIN_WEIGHTS],
            *[new_m[n] for n in TWIN_WEIGHTS], *[new_v[n] for n in TWIN_WEIGHTS])
```

```python
import functools

import jax
import jax.numpy as jnp
from jax import lax
from jax.experimental import pallas as pl
from jax.experimental.pallas import tpu as pltpu

F32 = jnp.float32
MM = jnp.bfloat16
NORM_EPS = 1e-6
NEG = -1e30
N_DEV = 8
D_MODEL = 1024
N_SEC = 8
SEC_W = 512
HG_HEADS = 4
HG_D = 128
AT_HEADS = 8
AT_DH = 64
ATT_BLK = 128
DILATIONS = (1, 4, 16)
ROPE_THETA = 10000.0
CH = 16
LB_LO, LB_HI = 1e-6, 1.0 - 1e-6
ADAM_LR, ADAM_B1, ADAM_B2, ADAM_EPS, ADAM_WD, ADAM_STEP = 0.001, 0.9, 0.999, 1e-08, 0.01, 10
VMEM_LIMIT = 56 * 1024 * 1024
MESH = pl.DeviceIdType.MESH


def _params(*sem):
    return pltpu.CompilerParams(dimension_semantics=sem, vmem_limit_bytes=VMEM_LIMIT)


def _sigmoid(x):
    return 1.0 / (1.0 + jnp.exp(-x))


def _dot(a, b):
    return jnp.dot(a.astype(MM), b.astype(MM), preferred_element_type=F32)


def _dot_nt(a, b):
    return lax.dot_general(a.astype(MM), b.astype(MM), (((1,), (1,)), ((), ())), preferred_element_type=F32)


def _dot_tn(a, b):
    return lax.dot_general(a.astype(MM), b.astype(MM), (((0,), (0,)), ((), ())), preferred_element_type=F32)


def _tri_dot(tri, g):
    g1 = g.astype(jnp.bfloat16)
    r1 = g - g1.astype(F32)
    g2 = r1.astype(jnp.bfloat16)
    g3 = (r1 - g2.astype(F32)).astype(jnp.bfloat16)
    t = tri.astype(jnp.bfloat16)
    d = functools.partial(jnp.dot, preferred_element_type=F32)
    return d(t, g1) + d(t, g2) + d(t, g3)


def _lower_bound(lbl):
    l0, l1 = lbl[0:1, :], lbl[1:2, :]
    m = jnp.maximum(l0, l1)
    e0, e1 = jnp.exp(l0 - m), jnp.exp(l1 - m)
    p = e0 / (e0 + e1)
    inside = (p >= LB_LO) & (p <= LB_HI)
    return jnp.clip(p, LB_LO, LB_HI), jnp.where(inside, p * (e1 / (e0 + e1)), 0.0)


def _iota2(shape, dim):
    return lax.broadcasted_iota(jnp.int32, shape, dim)


def _inproj_fwd(x, norm_w, w_all, tm=256):
    s = x.shape[0]

    def body(x_ref, nw_ref, w_ref, proj_ref):
        xv = x_ref[...]
        rstd = lax.rsqrt(jnp.mean(xv * xv, axis=-1, keepdims=True) + NORM_EPS)
        u = (xv * rstd * nw_ref[...]).astype(MM)
        for j in range(N_SEC):
            proj_ref[j] = jnp.dot(u, w_ref[j], preferred_element_type=F32)

    return pl.pallas_call(
        body, name="inproj_fwd", grid=(s // tm,),
        in_specs=[pl.BlockSpec((tm, D_MODEL), lambda i: (i, 0)),
                  pl.BlockSpec((1, D_MODEL), lambda i: (0, 0)),
                  pl.BlockSpec((N_SEC, D_MODEL, SEC_W), lambda i: (0, 0, 0))],
        out_specs=pl.BlockSpec((N_SEC, tm, SEC_W), lambda i: (0, i, 0)),
        out_shape=jax.ShapeDtypeStruct((N_SEC, s, SEC_W), F32),
        compiler_params=_params("parallel"),
    )(x, norm_w, w_all)


def _hgrn_gates(xq, xf, lb):
    sgq = _sigmoid(xq)
    sg = _sigmoid(xf)
    sn = _sigmoid(-xf)
    f = lb + (1.0 - lb) * sg
    return sgq, xq * sgq, sg, sn, f, (1.0 - lb) * sn


def _hgrn_fwd(proj, lb_logits, rb=256):
    s = proj.shape[1]
    nb, nc = s // rb, rb // CH

    def body(q_ref, f_ref, i_ref, lbl_ref, o_ref, sst_ref, st_ref, slab_ref):
        @pl.when(pl.program_id(1) == 0)
        def _():
            st_ref[...] = jnp.zeros_like(st_ref)

        sst_ref[...] = st_ref[...]
        lb, _ = _lower_bound(lbl_ref[...])
        row, col = _iota2((CH, CH), 0), _iota2((CH, CH), 1)
        tril = row >= col

        def chunk(c, carry):
            rows = pl.ds(pl.multiple_of(c * CH, CH), CH)
            v = i_ref[rows, :]
            _, q, _, _, f, kk = _hgrn_gates(q_ref[rows, :], f_ref[rows, :], lb)
            b = _tri_dot(tril, jnp.log(f))
            bl = b[CH - 1:CH, :]
            st = st_ref[...]
            o = _dot_nt(q * jnp.exp(b), st)
            for t in range(CH):
                slab_ref[t * CH:(t + 1) * CH, :] = (q * jnp.exp(jnp.minimum(b - b[t:t + 1, :], 0.0))).astype(MM)
            r = _dot_nt(slab_ref[...], kk)
            a = jnp.zeros((CH, CH), F32)
            for t in range(CH):
                a = a + jnp.where(col == t, r[t * CH:(t + 1) * CH, :], 0.0)
            a = jnp.where(tril, a, 0.0)
            o_ref[rows, :] = o + _dot(a, v)
            st_ref[...] = st * jnp.exp(bl) + _dot_tn(v, kk * jnp.exp(bl - b))
            return carry

        lax.fori_loop(0, nc, chunk, 0)

    sec = lambda j: pl.BlockSpec((None, rb, HG_D), lambda h, i, j=j: (j, i, h))
    return pl.pallas_call(
        body, name="hgrn_fwd", grid=(HG_HEADS, nb),
        in_specs=[sec(0), sec(1), sec(2), pl.BlockSpec((2, HG_D), lambda h, i: (0, h))],
        out_specs=[pl.BlockSpec((rb, HG_D), lambda h, i: (i, h)),
                   pl.BlockSpec((None, None, HG_D, HG_D), lambda h, i: (i, h, 0, 0))],
        out_shape=[jax.ShapeDtypeStruct((s, SEC_W), F32),
                   jax.ShapeDtypeStruct((nb, HG_HEADS, HG_D, HG_D), F32)],
        scratch_shapes=[pltpu.VMEM((HG_D, HG_D), F32), pltpu.VMEM((CH * CH, HG_D), MM)],
        compiler_params=_params("parallel", "arbitrary"),
    )(proj, proj, proj, lb_logits)


def _hgrn_bwd(proj, lb_logits, d_o, sst, rb=256):
    s = proj.shape[1]
    nb, nc = s // rb, rb // CH

    def body(q_ref, f_ref, i_ref, lbl_ref, do_ref, sst_ref, dxq_ref, dxf_ref, dxi_ref, dlb_ref,
             dst_ref, states_ref, lslab_ref, kslab_ref):
        @pl.when(pl.program_id(1) == 0)
        def _():
            dst_ref[...] = jnp.zeros_like(dst_ref)
            dlb_ref[...] = jnp.zeros_like(dlb_ref)

        lb, _ = _lower_bound(lbl_ref[...])
        row, col = _iota2((CH, CH), 0), _iota2((CH, CH), 1)
        tril, triu = row >= col, row <= col
        sel = (_iota2((CH, CH * CH), 1) % CH == _iota2((CH, CH * CH), 0)).astype(MM)
        blockdiag = _iota2((CH, CH * CH), 1) // CH == _iota2((CH, CH * CH), 0)
        last = _iota2((CH, HG_D), 0) == CH - 1

        def replay(c, st):
            rows = pl.ds(pl.multiple_of(c * CH, CH), CH)
            states_ref[c] = st
            _, _, _, _, f, kk = _hgrn_gates(q_ref[rows, :], f_ref[rows, :], lb)
            b = _tri_dot(tril, jnp.log(f))
            bl = b[CH - 1:CH, :]
            return st * jnp.exp(bl) + _dot_tn(i_ref[rows, :], kk * jnp.exp(bl - b))

        lax.fori_loop(0, nc, replay, sst_ref[...])

        def chunk(n, carry):
            c = nc - 1 - n
            rows = pl.ds(pl.multiple_of(c * CH, CH), CH)
            xq, v, do = q_ref[rows, :], i_ref[rows, :], do_ref[rows, :]
            sgq, q, sg, sn, f, kk = _hgrn_gates(xq, f_ref[rows, :], lb)
            b = _tri_dot(tril, jnp.log(f))
            bl = b[CH - 1:CH, :]
            eb, ebl, dec = jnp.exp(b), jnp.exp(bl), jnp.exp(bl - b)
            qe, kd = q * eb, kk * dec
            st, dstn = states_ref[c], dst_ref[...]
            dqe = _dot(do, st)
            dkd = _dot(v, dstn)
            d_a = jnp.where(tril, _dot_nt(do, v), 0.0)
            d_at = jnp.where(triu, _dot_nt(v, do), 0.0)
            for t in range(CH):
                bt = b[t:t + 1, :]
                lslab_ref[t * CH:(t + 1) * CH, :] = (q * jnp.exp(jnp.minimum(b - bt, 0.0))).astype(MM)
                kslab_ref[t * CH:(t + 1) * CH, :] = (kk * jnp.exp(jnp.minimum(bt - b, 0.0))).astype(MM)
            r = _dot_nt(kslab_ref[...], q)
            a_t = jnp.zeros((CH, CH), F32)
            for t in range(CH):
                a_t = a_t + jnp.where(col == t, r[t * CH:(t + 1) * CH, :], 0.0)
            a_t = jnp.where(triu, a_t, 0.0)
            dv = _dot_nt(kd, dstn) + _dot(a_t, do)
            dq_in = _dot(jnp.where(blockdiag, _dot(d_a, sel), 0.0), kslab_ref[...])
            dk_in = _dot(jnp.where(blockdiag, _dot(d_at, sel), 0.0), lslab_ref[...])
            dkd_kd = dkd * kd
            db = dqe * qe - dkd_kd + q * dq_in - kk * dk_in
            dbl = jnp.sum(dkd_kd, axis=0, keepdims=True) + jnp.sum(dstn * st, axis=0, keepdims=True) * ebl
            dg = _tri_dot(triu, db + jnp.where(last, dbl, 0.0))
            df = dg / f - (dkd * dec + dk_in)
            dxq_ref[rows, :] = (dqe * eb + dq_in) * (sgq * (1.0 + xq * (1.0 - sgq)))
            dxf_ref[rows, :] = df * (1.0 - lb) * sg * sn
            dxi_ref[rows, :] = dv
            dlb_ref[...] += jnp.sum(df * sn, axis=0, keepdims=True)
            dst_ref[...] = dstn * ebl + _dot_tn(do, qe)
            return carry

        lax.fori_loop(0, nc, chunk, 0)

    rev = lambda i: nb - 1 - i
    sec = lambda j: pl.BlockSpec((None, rb, HG_D), lambda h, i, j=j: (j, rev(i), h))
    blk = pl.BlockSpec((rb, HG_D), lambda h, i: (rev(i), h))
    return pl.pallas_call(
        body, name="hgrn_bwd", grid=(HG_HEADS, nb),
        in_specs=[sec(0), sec(1), sec(2), pl.BlockSpec((2, HG_D), lambda h, i: (0, h)), blk,
                  pl.BlockSpec((None, None, HG_D, HG_D), lambda h, i: (rev(i), h, 0, 0))],
        out_specs=[blk, blk, blk, pl.BlockSpec((1, HG_D), lambda h, i: (0, h))],
        out_shape=[jax.ShapeDtypeStruct((s, SEC_W), F32)] * 3 + [jax.ShapeDtypeStruct((1, SEC_W), F32)],
        scratch_shapes=[pltpu.VMEM((HG_D, HG_D), F32), pltpu.VMEM((nc, HG_D, HG_D), F32),
                        pltpu.VMEM((CH * CH, HG_D), MM), pltpu.VMEM((CH * CH, HG_D), MM)],
        compiler_params=_params("parallel", "arbitrary"),
    )(proj, proj, proj, lb_logits, d_o, sst)


def _rope_tables(s):
    half = AT_DH // 2
    inv_freq = 1.0 / (ROPE_THETA ** (jnp.arange(half, dtype=F32) / half))
    ang = jnp.arange(s, dtype=jnp.int32).astype(F32)[:, None] * inv_freq[None, :]
    cos, sin = jnp.cos(ang), jnp.sin(ang)
    return jnp.concatenate([cos] * 4, axis=-1), jnp.concatenate([-sin, sin] * 2, axis=-1)


def _rope128(x, cos, sin):
    lo = (_iota2(x.shape, 1) % AT_DH) < AT_DH // 2
    rot = jnp.where(lo, pltpu.roll(x, 128 - AT_DH // 2, 1), pltpu.roll(x, AT_DH // 2, 1))
    return x * cos + rot * sin


def _attn_prep(proj, cos, sin, tm=512):
    s = proj.shape[1]

    def body(q_ref, k_ref, v_ref, cos_ref, sin_ref, qo_ref, ko_ref, vo_ref):
        c, sn = cos_ref[...], sin_ref[...]
        for j in range(SEC_W // 128):
            sl = slice(j * 128, (j + 1) * 128)
            qo_ref[:, sl] = (_rope128(q_ref[:, sl], c, sn) * (AT_DH ** -0.5)).astype(MM)
            ko_ref[:, sl] = _rope128(k_ref[:, sl], c, sn).astype(MM)
        vo_ref[...] = v_ref[...].astype(MM)

    sec = lambda j: pl.BlockSpec((None, tm, SEC_W), lambda i, j=j: (j, i, 0))
    tab = pl.BlockSpec((tm, 128), lambda i: (i, 0))
    out = pl.BlockSpec((tm, SEC_W), lambda i: (i, 0))
    return pl.pallas_call(
        body, name="attn_prep", grid=(s // tm,),
        in_specs=[sec(4), sec(5), sec(6), tab, tab], out_specs=[out] * 3,
        out_shape=[jax.ShapeDtypeStruct((s, SEC_W), MM)] * 3,
        compiler_params=_params("parallel"),
    )(proj, proj, proj, cos, sin)


def _scores(q, k_cur, k_prev, has_prev):
    row, col = _iota2((ATT_BLK, ATT_BLK), 0), _iota2((ATT_BLK, ATT_BLK), 1)
    s_c = jnp.where(col <= row, _dot_nt(q, k_cur), NEG)
    s_p = jnp.where((col >= row) & has_prev, _dot_nt(q, k_prev), NEG)
    return s_c, s_p


def _attn_fwd(qr, kr, vr, d):
    s = qr.shape[0]
    rows, cols = s // d, d * SEC_W
    nb = rows // ATT_BLK
    view = lambda a: a.reshape(rows, cols)

    def body(q_ref, kc_ref, kp_ref, vc_ref, vp_ref, o_ref, lse_ref):
        has_prev = pl.program_id(1) > 0
        outs, lses = [], []
        for h in range(2):
            sl = slice(h * AT_DH, (h + 1) * AT_DH)
            s_c, s_p = _scores(q_ref[:, sl], kc_ref[:, sl], kp_ref[:, sl], has_prev)
            m = jnp.maximum(jnp.max(s_c, axis=-1, keepdims=True), jnp.max(s_p, axis=-1, keepdims=True))
            p_c, p_p = jnp.exp(s_c - m), jnp.exp(s_p - m)
            l = jnp.sum(p_c, axis=-1, keepdims=True) + jnp.sum(p_p, axis=-1, keepdims=True)
            outs.append((_dot(p_c, vc_ref[:, sl]) + _dot(p_p, vp_ref[:, sl])) / l)
            lses.append(jnp.broadcast_to(m + jnp.log(l), (ATT_BLK, AT_DH)))
        o_ref[...] = jnp.concatenate(outs, axis=-1)
        lse_ref[...] = jnp.concatenate(lses, axis=-1)

    cur = pl.BlockSpec((ATT_BLK, 128), lambda c, n: (n, c))
    prev = pl.BlockSpec((ATT_BLK, 128), lambda c, n: (jnp.maximum(n - 1, 0), c))
    o, lse = pl.pallas_call(
        body, name=f"attn_fwd_d{d}", grid=(cols // 128, nb),
        in_specs=[cur, cur, prev, cur, prev], out_specs=[cur, cur],
        out_shape=[jax.ShapeDtypeStruct((rows, cols), F32)] * 2,
        compiler_params=_params("parallel", "parallel"),
    )(view(qr), view(kr), view(kr), view(vr), view(vr))
    return o.reshape(s, SEC_W), lse.reshape(s, SEC_W)


def _attn_bwd_dq(qr, kr, vr, do, lse, delta, d):
    s = qr.shape[0]
    rows, cols = s // d, d * SEC_W
    nb = rows // ATT_BLK
    view = lambda a: a.reshape(rows, cols)

    def body(q_ref, kc_ref, kp_ref, vc_ref, vp_ref, do_ref, lse_ref, dl_ref, dq_ref):
        has_prev = pl.program_id(1) > 0
        outs = []
        for h in range(2):
            sl = slice(h * AT_DH, (h + 1) * AT_DH)
            s_c, s_p = _scores(q_ref[:, sl], kc_ref[:, sl], kp_ref[:, sl], has_prev)
            lse_h, dl_h, do_h = lse_ref[:, sl][:, 0:1], dl_ref[:, sl][:, 0:1], do_ref[:, sl]
            ds_c = jnp.exp(s_c - lse_h) * (_dot_nt(do_h, vc_ref[:, sl]) - dl_h)
            ds_p = jnp.exp(s_p - lse_h) * (_dot_nt(do_h, vp_ref[:, sl]) - dl_h)
            outs.append((_dot(ds_c, kc_ref[:, sl]) + _dot(ds_p, kp_ref[:, sl])) * (AT_DH ** -0.5))
        dq_ref[...] = jnp.concatenate(outs, axis=-1)

    cur = pl.BlockSpec((ATT_BLK, 128), lambda c, n: (n, c))
    prev = pl.BlockSpec((ATT_BLK, 128), lambda c, n: (jnp.maximum(n - 1, 0), c))
    dq = pl.pallas_call(
        body, name=f"attn_bwd_dq_d{d}", grid=(cols // 128, nb),
        in_specs=[cur, cur, prev, cur, prev, cur, cur, cur], out_specs=cur,
        out_shape=jax.ShapeDtypeStruct((rows, cols), F32),
        compiler_params=_params("parallel", "parallel"),
    )(view(qr), view(kr), view(kr), view(vr), view(vr), view(do), view(lse), view(delta))
    return dq.reshape(s, SEC_W)


def _attn_bwd_dkv(qr, kr, vr, do, lse, delta, d):
    s = qr.shape[0]
    rows, cols = s // d, d * SEC_W
    nb = rows // ATT_BLK
    view = lambda a: a.reshape(rows, cols)

    def body(k_ref, v_ref, qc_ref, qn_ref, doc_ref, don_ref, lsec_ref, lsen_ref, dlc_ref, dln_ref,
             dk_ref, dv_ref):
        has_next = pl.program_id(1) < nb - 1
        row, col = _iota2((ATT_BLK, ATT_BLK), 0), _iota2((ATT_BLK, ATT_BLK), 1)
        dks, dvs = [], []
        for h in range(2):
            sl = slice(h * AT_DH, (h + 1) * AT_DH)
            k, v = k_ref[:, sl], v_ref[:, sl]
            dk, dv = 0.0, 0.0
            for q_ref, do_ref, lse_ref, dl_ref, mask in (
                    (qc_ref, doc_ref, lsec_ref, dlc_ref, col <= row),
                    (qn_ref, don_ref, lsen_ref, dln_ref, (col >= row) & has_next)):
                q, do_h = q_ref[:, sl], do_ref[:, sl]
                p = jnp.exp(jnp.where(mask, _dot_nt(q, k), NEG) - lse_ref[:, sl][:, 0:1])
                ds = p * (_dot_nt(do_h, v) - dl_ref[:, sl][:, 0:1])
                dv = dv + _dot_tn(p, do_h)
                dk = dk + _dot_tn(ds, q)
            dks.append(dk)
            dvs.append(dv)
        dk_ref[...] = jnp.concatenate(dks, axis=-1)
        dv_ref[...] = jnp.concatenate(dvs, axis=-1)

    cur = pl.BlockSpec((ATT_BLK, 128), lambda c, n: (n, c))
    nxt = pl.BlockSpec((ATT_BLK, 128), lambda c, n: (jnp.minimum(n + 1, nb - 1), c))
    dk, dv = pl.pallas_call(
        body, name=f"attn_bwd_dkv_d{d}", grid=(cols // 128, nb),
        in_specs=[cur, cur, cur, nxt, cur, nxt, cur, nxt, cur, nxt], out_specs=[cur, cur],
        out_shape=[jax.ShapeDtypeStruct((rows, cols), F32)] * 2,
        compiler_params=_params("parallel", "parallel"),
    )(view(kr), view(vr), view(qr), view(qr), view(do), view(do), view(lse), view(lse), view(delta), view(delta))
    return dk.reshape(s, SEC_W), dv.reshape(s, SEC_W)


def _head_sum(a, width):
    parts = []
    for j in range(a.shape[1] // width):
        sm = jnp.sum(a[:, j * width:(j + 1) * width], axis=-1, keepdims=True)
        parts.append(jnp.broadcast_to(sm, (a.shape[0], width)))
    return jnp.concatenate(parts, axis=-1)


def _mid(x, tgt, proj, o_hg, o_at, lse_at, hg_norm_w, final_norm_w, wo_all, tm=256):
    s = x.shape[0]
    nb = s // tm

    def body(x_ref, t_ref, hgz_ref, atz_ref, ohg_ref, o1_ref, o2_ref, o3_ref, l1_ref, l2_ref, l3_ref,
             g_ref, fw_ref, wo_ref,
             dh_ref, dohg_ref, dhgz_ref, datz_ref, do1_ref, do2_ref, do3_ref, dl1_ref, dl2_ref, dl3_ref,
             gwo_ref, gfw_ref, ghg_ref, loss_ref):
        @pl.when(pl.program_id(0) == 0)
        def _():
            gwo_ref[...] = jnp.zeros_like(gwo_ref)
            gfw_ref[...] = jnp.zeros_like(gfw_ref)
            ghg_ref[...] = jnp.zeros_like(ghg_ref)
            loss_ref[...] = jnp.zeros_like(loss_ref)

        ohg, g = ohg_ref[...], g_ref[...]
        rs = lax.rsqrt(_head_sum(ohg * ohg, HG_D) * (1.0 / HG_D) + NORM_EPS)
        on = ohg * rs
        hgz = hgz_ref[...]
        sz = _sigmoid(hgz)
        gate_hg = hgz * sz
        lses = (l1_ref[...], l2_ref[...], l3_ref[...])
        outs = (o1_ref[...], o2_ref[...], o3_ref[...])
        mx = jnp.maximum(jnp.maximum(lses[0], lses[1]), lses[2])
        es = [jnp.exp(l - mx) for l in lses]
        den = es[0] + es[1] + es[2]
        ws = [e / den for e in es]
        oat = ws[0] * outs[0] + ws[1] * outs[1] + ws[2] * outs[2]
        atz = atz_ref[...]
        sa = _sigmoid(atz)
        gate_at = atz * sa
        mixed = jnp.concatenate([on * g * gate_hg, oat * gate_at], axis=-1).astype(MM)
        h = x_ref[...] + jnp.dot(mixed, wo_ref[...], preferred_element_type=F32)
        rstd = lax.rsqrt(jnp.mean(h * h, axis=-1, keepdims=True) + NORM_EPS)
        hn = h * rstd
        fw = fw_ref[...]
        err = hn * fw - t_ref[...]
        loss_ref[...] += 0.5 * jnp.sum(jnp.mean(err * err, axis=-1, keepdims=True), axis=0, keepdims=True)
        dout = err * (1.0 / D_MODEL)
        gfw_ref[...] += jnp.sum(dout * hn, axis=0, keepdims=True)
        dhn = dout * fw
        dh = rstd * (dhn - hn * jnp.mean(dhn * hn, axis=-1, keepdims=True))
        dh_ref[...] = dh
        dh_mm = dh.astype(MM)
        gwo_ref[...] += _dot_tn(mixed, dh_mm)
        dmixed = _dot_nt(dh_mm, wo_ref[...])
        dm_hg = dmixed[:, :SEC_W]
        d_ong = dm_hg * gate_hg
        dhgz_ref[...] = dm_hg * (on * g) * (sz * (1.0 + hgz * (1.0 - sz)))
        ghg_ref[...] += jnp.sum(d_ong * on, axis=0, keepdims=True)
        d_on = d_ong * g
        dohg_ref[...] = rs * (d_on - on * (_head_sum(d_on * on, HG_D) * (1.0 / HG_D)))
        dm_at = dmixed[:, SEC_W:]
        d_oat = dm_at * gate_at
        datz_ref[...] = dm_at * oat * (sa * (1.0 + atz * (1.0 - sa)))
        drow = _head_sum(d_oat * oat, AT_DH)
        for w, do_ref, dl_ref in zip(ws, (do1_ref, do2_ref, do3_ref), (dl1_ref, dl2_ref, dl3_ref)):
            do_ref[...] = (w * d_oat).astype(MM)
            dl_ref[...] = w * drow

    row = lambda w: pl.BlockSpec((tm, w), lambda i: (i, 0))
    sec = lambda j: pl.BlockSpec((None, tm, SEC_W), lambda i, j=j: (j, i, 0))
    const = lambda shp: pl.BlockSpec(shp, lambda i: (0,) * len(shp))
    half = row(SEC_W)
    return pl.pallas_call(
        body, name="mid", grid=(nb,),
        in_specs=[row(D_MODEL), row(D_MODEL), sec(3), sec(7)] + [half] * 7
                 + [const((1, SEC_W)), const((1, D_MODEL)), const((D_MODEL, D_MODEL))],
        out_specs=[row(D_MODEL)] + [half] * 9
                  + [const((D_MODEL, D_MODEL)), const((1, D_MODEL)), const((1, SEC_W)), const((1, 1))],
        out_shape=[jax.ShapeDtypeStruct((s, D_MODEL), F32)] + [jax.ShapeDtypeStruct((s, SEC_W), F32)] * 3
                  + [jax.ShapeDtypeStruct((s, SEC_W), MM)] * 3 + [jax.ShapeDtypeStruct((s, SEC_W), F32)] * 3
                  + [jax.ShapeDtypeStruct((D_MODEL, D_MODEL), F32), jax.ShapeDtypeStruct((1, D_MODEL), F32),
                     jax.ShapeDtypeStruct((1, SEC_W), F32), jax.ShapeDtypeStruct((1, 1), F32)],
        compiler_params=_params("arbitrary"),
    )(x, tgt, proj, proj, o_hg, *o_at, *lse_at, hg_norm_w, final_norm_w, wo_all)


def _inproj_bwd_x(x, norm_w, w_all, dh, dsec, dq_r, dk_r, dv, cos, sin, tm=256):
    s = x.shape[0]

    def body(x_ref, nw_ref, w_ref, dh_ref, s0, s1, s2, s3, s7, q1, q2, q3, k1, k2, k3, v1, v2, v3,
             cos_ref, sin_ref, gx_ref, dp_ref, gnw_ref):
        @pl.when(pl.program_id(0) == 0)
        def _():
            gnw_ref[...] = jnp.zeros_like(gnw_ref)

        c, sn = cos_ref[...], -sin_ref[...]
        dq = q1[...] + q2[...] + q3[...]
        dk = k1[...] + k2[...] + k3[...]
        unrot = lambda a: jnp.concatenate(
            [_rope128(a[:, j * 128:(j + 1) * 128], c, sn) for j in range(SEC_W // 128)], axis=-1)
        secs = (s0[...], s1[...], s2[...], s3[...], unrot(dq), unrot(dk), v1[...] + v2[...] + v3[...], s7[...])
        du = jnp.zeros((tm, D_MODEL), F32)
        for j, dsj in enumerate(secs):
            dsj = dsj.astype(MM)
            dp_ref[j] = dsj
            du = du + _dot_nt(dsj, w_ref[j])
        xv, nw = x_ref[...], nw_ref[...]
        rstd = lax.rsqrt(jnp.mean(xv * xv, axis=-1, keepdims=True) + NORM_EPS)
        xn = xv * rstd
        gnw_ref[...] += jnp.sum(du * xn, axis=0, keepdims=True)
        dxn = du * nw
        gx_ref[...] = dh_ref[...] + rstd * (dxn - xn * jnp.mean(dxn * xn, axis=-1, keepdims=True))

    row = lambda w: pl.BlockSpec((tm, w), lambda i: (i, 0))
    const = lambda shp: pl.BlockSpec(shp, lambda i: (0,) * len(shp))
    return pl.pallas_call(
        body, name="inproj_bwd_x", grid=(s // tm,),
        in_specs=[row(D_MODEL), const((1, D_MODEL)), const((N_SEC, D_MODEL, SEC_W)), row(D_MODEL)]
                 + [row(SEC_W)] * 14 + [row(128), row(128)],
        out_specs=[row(D_MODEL), pl.BlockSpec((N_SEC, tm, SEC_W), lambda i: (0, i, 0)), const((1, D_MODEL))],
        out_shape=[jax.ShapeDtypeStruct((s, D_MODEL), F32), jax.ShapeDtypeStruct((N_SEC, s, SEC_W), MM),
                   jax.ShapeDtypeStruct((1, D_MODEL), F32)],
        compiler_params=_params("arbitrary"),
    )(x, norm_w, w_all, dh, *dsec, *dq_r, *dk_r, *dv, cos, sin)


def _inproj_bwd_w(x, norm_w, dproj, tm=256):
    s = x.shape[0]

    def body(x_ref, nw_ref, dp_ref, gw_ref):
        @pl.when(pl.program_id(1) == 0)
        def _():
            gw_ref[...] = jnp.zeros_like(gw_ref)

        xv = x_ref[...]
        rstd = lax.rsqrt(jnp.mean(xv * xv, axis=-1, keepdims=True) + NORM_EPS)
        u = (xv * rstd * nw_ref[...]).astype(MM)
        gw_ref[...] += _dot_tn(u, dp_ref[...])

    return pl.pallas_call(
        body, name="inproj_bwd_w", grid=(N_SEC, s // tm),
        in_specs=[pl.BlockSpec((tm, D_MODEL), lambda j, i: (i, 0)), pl.BlockSpec((1, D_MODEL), lambda j, i: (0, 0)),
                  pl.BlockSpec((None, tm, SEC_W), lambda j, i: (j, i, 0))],
        out_specs=pl.BlockSpec((None, D_MODEL, SEC_W), lambda j, i: (j, 0, 0)),
        out_shape=jax.ShapeDtypeStruct((N_SEC, D_MODEL, SEC_W), F32),
        compiler_params=_params("parallel", "arbitrary"),
    )(x, norm_w, dproj)


def _local_step(x, tgt, norm_w, w_all, lb_logits, hg_norm_w, wo_all, final_norm_w):
    s = x.shape[0]
    cos, sin = _rope_tables(s)
    proj = _inproj_fwd(x, norm_w, w_all)
    o_hg, sst = _hgrn_fwd(proj, lb_logits)
    qr, kr, vr = _attn_prep(proj, cos, sin)
    att = [_attn_fwd(qr, kr, vr, d) for d in DILATIONS]
    (dh, d_ohg, d_hgz, d_atz, do1, do2, do3, dl1, dl2, dl3, gwo, gfw, ghg, loss) = _mid(
        x, tgt, proj, o_hg, [a[0] for a in att], [a[1] for a in att], hg_norm_w, final_norm_w[None, :], wo_all)
    dxq, dxf, dxi, dlb = _hgrn_bwd(proj, lb_logits, d_ohg, sst)
    dq_r, dk_r, dv = [], [], []
    for d, a, do, dl in zip(DILATIONS, att, (do1, do2, do3), (dl1, dl2, dl3)):
        dq_r.append(_attn_bwd_dq(qr, kr, vr, do, a[1], dl, d))
        dk_d, dv_d = _attn_bwd_dkv(qr, kr, vr, do, a[1], dl, d)
        dk_r.append(dk_d)
        dv.append(dv_d)
    gx, dproj, gnw = _inproj_bwd_x(x, norm_w, w_all, dh, (dxq, dxf, dxi, d_hgz, d_atz), dq_r, dk_r, dv, cos, sin)
    gwi = _inproj_bwd_w(x, norm_w, dproj)
    small = jnp.concatenate([gnw, jnp.concatenate([dlb, ghg], axis=-1), gfw,
                             jnp.pad(loss, ((0, 0), (0, D_MODEL - 1)))], axis=0)
    return gx, gwi, gwo, small


def _coords():
    return lax.axis_index("x"), lax.axis_index("y"), lax.axis_index("c")


def _gather_weights(w_in, w_out):
    wo_rows = w_out.shape[0]

    def body(wi_ref, wo_ref, wi_all, wo_all, send_sems, recv_sems):
        x, y, c = _coords()
        me, sibling = (x, y, c), (x, y, 1 - c)
        chips = [(1 - x, y), (x, 1 - y), (1 - x, 1 - y)]
        slot = lambda p: 4 * p[0] + 2 * p[1] + p[2]

        def copies(k, block, to):
            return [pltpu.make_async_remote_copy(
                src_ref=ref.at[slot(block)], dst_ref=ref.at[slot(block)], send_sem=send_sems.at[a, k],
                recv_sem=recv_sems.at[a, k], device_id=to, device_id_type=MESH)
                for a, ref in enumerate((wi_all, wo_all))]

        wi_all[slot(me)] = wi_ref[...].astype(MM)
        wo_all[slot(me)] = wo_ref[...].astype(MM)
        first = copies(0, me, sibling)
        for j, chip in enumerate(chips):
            first += copies(1 + j, me, (*chip, c))
        for cp in first:
            cp.start()
        passed = []
        for j, chip in enumerate(chips):
            for cp in copies(1 + j, (*chip, c), me):
                cp.wait_recv()
            fwd = copies(4 + j, (*chip, c), sibling)
            for cp in fwd:
                cp.start()
            passed += fwd
        for cp in copies(0, sibling, me):
            cp.wait_recv()
        for j, chip in enumerate(chips):
            for cp in copies(4 + j, (*chip, 1 - c), me):
                cp.wait_recv()
        for cp in first + passed:
            cp.wait_send()

    vmem = pl.BlockSpec(memory_space=pltpu.VMEM)
    return pl.pallas_call(
        body, name="gather_weights",
        in_specs=[vmem, vmem], out_specs=[vmem, vmem],
        out_shape=[jax.ShapeDtypeStruct((N_DEV, D_MODEL, SEC_W), MM),
                   jax.ShapeDtypeStruct((N_DEV, wo_rows, D_MODEL), MM)],
        scratch_shapes=[pltpu.SemaphoreType.DMA((2, 7)), pltpu.SemaphoreType.DMA((2, 7))],
        compiler_params=pltpu.CompilerParams(vmem_limit_bytes=VMEM_LIMIT),
    )(w_in, w_out)


def _exchange_grads(gwi, gwo, small):
    def body(gwi_ref, gwo_ref, sm_ref, li_ref, lo_ref, ls_ref, send_sems, recv_sems, local_sems):
        x, y, c = _coords()
        me = 4 * x + 2 * y + c
        refs = ((gwi_ref, li_ref), (gwo_ref, lo_ref))
        own = [pltpu.make_async_copy(src.at[me], dst.at[me], local_sems.at[a]) for a, (src, dst) in enumerate(refs)]
        own.append(pltpu.make_async_copy(sm_ref, ls_ref.at[me], local_sems.at[2]))
        for cp in own:
            cp.start()
        sends = []
        for k in range(1, N_DEV):
            px, py, pc = x ^ (k >> 2), y ^ ((k >> 1) & 1), c ^ (k & 1)
            peer = 4 * px + 2 * py + pc
            for a, (src, dst) in enumerate(refs):
                sends.append(pltpu.make_async_remote_copy(
                    src_ref=src.at[peer], dst_ref=dst.at[me], send_sem=send_sems.at[a, k - 1],
                    recv_sem=recv_sems.at[a, k - 1], device_id=(px, py, pc), device_id_type=MESH))
            sends.append(pltpu.make_async_remote_copy(
                src_ref=sm_ref, dst_ref=ls_ref.at[me], send_sem=send_sems.at[2, k - 1],
                recv_sem=recv_sems.at[2, k - 1], device_id=(px, py, pc), device_id_type=MESH))
        for cp in sends:
            cp.start()
        for cp in sends:
            cp.wait_recv()
        for cp in sends:
            cp.wait_send()
        for cp in own:
            cp.wait()

    hbm = pl.BlockSpec(memory_space=pl.ANY)
    return pl.pallas_call(
        body, name="exchange_grads",
        in_specs=[hbm, hbm, hbm], out_specs=[hbm, hbm, hbm],
        out_shape=[jax.ShapeDtypeStruct(gwi.shape, F32), jax.ShapeDtypeStruct(gwo.shape, F32),
                   jax.ShapeDtypeStruct((N_DEV,) + small.shape, F32)],
        scratch_shapes=[pltpu.SemaphoreType.DMA((3, 7)), pltpu.SemaphoreType.DMA((3, 7)),
                        pltpu.SemaphoreType.DMA((3,))],
    )(gwi, gwo, small)


def _adamw(w, g, m, v):
    m = ADAM_B1 * m + (1.0 - ADAM_B1) * g
    v = ADAM_B2 * v + (1.0 - ADAM_B2) * (g * g)
    m_hat = m / (1.0 - ADAM_B1 ** ADAM_STEP)
    v_hat = v / (1.0 - ADAM_B2 ** ADAM_STEP)
    return -ADAM_LR * (m_hat / (jnp.sqrt(v_hat) + ADAM_EPS) + ADAM_WD * w), m, v


def _slot_sum(ref):
    g = ref[0]
    for i in range(1, N_DEV):
        g = g + ref[i]
    return g


def _update_matrix(name, landed, w, m, v, rows):
    r, c = w.shape

    def body(l_ref, w_ref, m_ref, v_ref, g_ref, d_ref, nm_ref, nv_ref):
        g = _slot_sum(l_ref)
        g_ref[...] = g
        d_ref[...], nm_ref[...], nv_ref[...] = _adamw(w_ref[...], g, m_ref[...], v_ref[...])

    blk = pl.BlockSpec((rows, c), lambda i: (i, 0))
    return pl.pallas_call(
        body, name=name, grid=(r // rows,),
        in_specs=[pl.BlockSpec((N_DEV, rows, c), lambda i: (0, i, 0)), blk, blk, blk],
        out_specs=[blk] * 4, out_shape=[jax.ShapeDtypeStruct((r, c), F32)] * 4,
        compiler_params=_params("parallel"),
    )(landed, w, m, v)


def _update_small(landed, lb_logits, ws, ms, vs):
    def body(l_ref, lbl_ref, w_ref, m_ref, v_ref, g_ref, d_ref, nm_ref, nv_ref, loss_ref):
        tot = _slot_sum(l_ref)
        _, dlb = _lower_bound(lbl_ref[...])
        g_lb = tot[1:2, :SEC_W] * dlb
        g = jnp.concatenate([tot[0:1], jnp.concatenate([g_lb, -g_lb], axis=-1),
                             jnp.pad(tot[1:2, SEC_W:], ((0, 0), (0, SEC_W))), tot[2:3]], axis=0)
        g_ref[...] = g
        d_ref[...], nm_ref[...], nv_ref[...] = _adamw(w_ref[...], g, m_ref[...], v_ref[...])
        loss_ref[...] = tot[3:4, 0:1]

    vmem = pl.BlockSpec(memory_space=pltpu.VMEM)
    return pl.pallas_call(
        body, name="update_small", in_specs=[vmem] * 5, out_specs=[vmem] * 5,
        out_shape=[jax.ShapeDtypeStruct((4, D_MODEL), F32)] * 4 + [jax.ShapeDtypeStruct((1, 1), F32)],
    )(landed, lb_logits, ws, ms, vs)


def _pack_small(norm_w, lb_logits, hg_norm_w, final_norm_w):
    return jnp.concatenate([norm_w, lb_logits.reshape(1, D_MODEL),
                            jnp.pad(hg_norm_w, ((0, 0), (0, D_MODEL - SEC_W))), final_norm_w[None, :]], axis=0)


def _unpack_small(a):
    return a[0:1], a[1].reshape(2, SEC_W), a[2:3, :SEC_W], a[3]


def kernel(x, norm_w, w_in, hgrn_lb_logits, hg_norm_w, w_out, final_norm_w, loss_target, m_norm_w, m_w_in, m_hgrn_lb_logits, m_hg_norm_w, m_w_out, m_final_norm_w, v_norm_w, v_w_in, v_hgrn_lb_logits, v_hg_norm_w, v_w_out, v_final_norm_w):
    w_all, wo_all = _gather_weights(w_in[0], w_out[0])
    gx, gwi, gwo, small = _local_step(x[0], loss_target[0], norm_w, w_all, hgrn_lb_logits, hg_norm_w,
                                      wo_all.reshape(D_MODEL, D_MODEL), final_norm_w)
    li, lo, ls = _exchange_grads(gwi, gwo.reshape(N_DEV, D_MODEL // N_DEV, D_MODEL), small)
    g_wi, d_wi, nm_wi, nv_wi = _update_matrix("update_w_in", li, w_in[0], m_w_in[0], v_w_in[0], 256)
    g_wo, d_wo, nm_wo, nv_wo = _update_matrix("update_w_out", lo, w_out[0], m_w_out[0], v_w_out[0], 128)
    g_s, d_s, nm_s, nv_s, loss = _update_small(
        ls, hgrn_lb_logits, _pack_small(norm_w, hgrn_lb_logits, hg_norm_w, final_norm_w),
        _pack_small(m_norm_w, m_hgrn_lb_logits, m_hg_norm_w, m_final_norm_w),
        _pack_small(v_norm_w, v_hgrn_lb_logits, v_hg_norm_w, v_final_norm_w))
    outs = []
    for small_out, wi, wo in ((g_s, g_wi, g_wo), (d_s, d_wi, d_wo), (nm_s, nm_wi, nm_wo), (nv_s, nv_wi, nv_wo)):
        nw, lb, hg, fw = _unpack_small(small_out)
        outs += [nw, wi[None], lb, hg, wo[None], fw]
    return (loss[0, 0], gx[None], *outs)
```

```python
import functools

import jax
import jax.numpy as jnp
from jax import lax
from jax.experimental import pallas as pl
from jax.experimental.pallas import tpu as pltpu

F32 = jnp.float32
MM = jnp.bfloat16
NORM_EPS = 1e-6
NEG = -1e30
N_DEV = 8
D_MODEL = 1024
N_SEC = 8
SEC_W = 512
HG_HEADS = 4
HG_D = 128
AT_HEADS = 8
AT_DH = 64
ATT_BLK = 128
AT_COLS = 512
DILATIONS = (1, 4, 16)
ROPE_THETA = 10000.0
CH = 16
LB_LO, LB_HI = 1e-6, 1.0 - 1e-6
ADAM_LR, ADAM_B1, ADAM_B2, ADAM_EPS, ADAM_WD, ADAM_STEP = 0.001, 0.9, 0.999, 1e-08, 0.01, 10
VMEM_LIMIT = 56 * 1024 * 1024
MESH = pl.DeviceIdType.MESH


def _params(*sem):
    return pltpu.CompilerParams(dimension_semantics=sem, vmem_limit_bytes=VMEM_LIMIT)


def _sigmoid(x):
    return 1.0 / (1.0 + jnp.exp(-x))


def _dot(a, b):
    return jnp.dot(a.astype(MM), b.astype(MM), preferred_element_type=F32)


def _dot_nt(a, b):
    return lax.dot_general(a.astype(MM), b.astype(MM), (((1,), (1,)), ((), ())), preferred_element_type=F32)


def _dot_tn(a, b):
    return lax.dot_general(a.astype(MM), b.astype(MM), (((0,), (0,)), ((), ())), preferred_element_type=F32)


def _tri_dot(tri, g):
    g1 = g.astype(jnp.bfloat16)
    r1 = g - g1.astype(F32)
    g2 = r1.astype(jnp.bfloat16)
    g3 = (r1 - g2.astype(F32)).astype(jnp.bfloat16)
    t = tri.astype(jnp.bfloat16)
    d = functools.partial(jnp.dot, preferred_element_type=F32)
    return d(t, g1) + d(t, g2) + d(t, g3)


def _lower_bound(lbl):
    l0, l1 = lbl[0:1, :], lbl[1:2, :]
    m = jnp.maximum(l0, l1)
    e0, e1 = jnp.exp(l0 - m), jnp.exp(l1 - m)
    p = e0 / (e0 + e1)
    inside = (p >= LB_LO) & (p <= LB_HI)
    return jnp.clip(p, LB_LO, LB_HI), jnp.where(inside, p * (e1 / (e0 + e1)), 0.0)


def _iota2(shape, dim):
    return lax.broadcasted_iota(jnp.int32, shape, dim)


def _inproj_fwd(x, norm_w, w_all, tm=256):
    s = x.shape[0]

    def body(x_ref, nw_ref, w_ref, proj_ref):
        xv = x_ref[...]
        rstd = lax.rsqrt(jnp.mean(xv * xv, axis=-1, keepdims=True) + NORM_EPS)
        u = (xv * rstd * nw_ref[...]).astype(MM)
        for j in range(N_SEC):
            proj_ref[j] = jnp.dot(u, w_ref[j], preferred_element_type=F32)

    return pl.pallas_call(
        body, name="inproj_fwd", grid=(s // tm,),
        in_specs=[pl.BlockSpec((tm, D_MODEL), lambda i: (i, 0)),
                  pl.BlockSpec((1, D_MODEL), lambda i: (0, 0)),
                  pl.BlockSpec((N_SEC, D_MODEL, SEC_W), lambda i: (0, 0, 0))],
        out_specs=pl.BlockSpec((N_SEC, tm, SEC_W), lambda i: (0, i, 0)),
        out_shape=jax.ShapeDtypeStruct((N_SEC, s, SEC_W), F32),
        compiler_params=_params("parallel"),
    )(x, norm_w, w_all)


def _hgrn_gates(xq, xf, lb):
    sgq = _sigmoid(xq)
    sg = _sigmoid(xf)
    sn = _sigmoid(-xf)
    f = lb + (1.0 - lb) * sg
    return sgq, xq * sgq, sg, sn, f, (1.0 - lb) * sn


def _bdot(a, b, ca, cb):
    return lax.dot_general(a.astype(MM), b.astype(MM), (((ca,), (cb,)), ((0,), (0,))), preferred_element_type=F32)


def _chunk_masks(rb):
    row, col = _iota2((rb, rb), 0), _iota2((rb, rb), 1)
    same = (row // CH) == (col // CH)
    return same & (row >= col), same & (row <= col)


def _hgrn_fwd(proj, lb_logits, rb=256):
    s = proj.shape[1]
    nb, nc = s // rb, rb // CH

    def body(q_ref, f_ref, i_ref, lbl_ref, o_ref, sst_ref, st_ref, slab_ref, states_ref):
        @pl.when(pl.program_id(1) == 0)
        def _():
            st_ref[...] = jnp.zeros_like(st_ref)

        sst_ref[...] = st_ref[...]
        lb, _ = _lower_bound(lbl_ref[...])
        prefix, _ = _chunk_masks(rb)
        c3 = lambda a: a.reshape(nc, CH, HG_D)
        _, q, _, _, f, kk = _hgrn_gates(q_ref[...], f_ref[...], lb)
        b3 = c3(_tri_dot(prefix, jnp.log(f)))
        q3, kk3, v3 = c3(q), c3(kk), c3(i_ref[...])
        bl3 = b3[:, CH - 1:CH, :]
        for t in range(CH):
            slab_ref[:, t * CH:(t + 1) * CH, :] = (q3 * jnp.exp(jnp.minimum(b3 - b3[:, t:t + 1, :], 0.0))).astype(MM)
        r = _bdot(slab_ref[...], kk3, 2, 2)
        row, col = _iota2((nc, CH, CH), 1), _iota2((nc, CH, CH), 2)
        a = jnp.zeros((nc, CH, CH), F32)
        for t in range(CH):
            a = a + jnp.where(col == t, r[:, t * CH:(t + 1) * CH, :], 0.0)
        a = jnp.where(row >= col, a, 0.0)
        x_upd = _bdot(v3, kk3 * jnp.exp(bl3 - b3), 1, 1)
        ebl3 = jnp.exp(bl3)
        st = st_ref[...]
        for c in range(nc):
            states_ref[c] = st
            st = st * ebl3[c] + x_upd[c]
        st_ref[...] = st
        o3 = _bdot(q3 * jnp.exp(b3), states_ref[...], 2, 2) + _bdot(a, v3, 2, 1)
        o_ref[...] = o3.reshape(rb, HG_D)

    sec = lambda j: pl.BlockSpec((None, rb, HG_D), lambda h, i, j=j: (j, i, h))
    return pl.pallas_call(
        body, name="hgrn_fwd", grid=(HG_HEADS, nb),
        in_specs=[sec(0), sec(1), sec(2), pl.BlockSpec((2, HG_D), lambda h, i: (0, h))],
        out_specs=[pl.BlockSpec((rb, HG_D), lambda h, i: (i, h)),
                   pl.BlockSpec((None, None, HG_D, HG_D), lambda h, i: (i, h, 0, 0))],
        out_shape=[jax.ShapeDtypeStruct((s, SEC_W), F32),
                   jax.ShapeDtypeStruct((nb, HG_HEADS, HG_D, HG_D), F32)],
        scratch_shapes=[pltpu.VMEM((HG_D, HG_D), F32), pltpu.VMEM((nc, CH * CH, HG_D), MM),
                        pltpu.VMEM((nc, HG_D, HG_D), F32)],
        compiler_params=_params("parallel", "arbitrary"),
    )(proj, proj, proj, lb_logits)


def _hgrn_bwd(proj, lb_logits, d_o, sst, rb=256):
    s = proj.shape[1]
    nb, nc = s // rb, rb // CH

    def body(q_ref, f_ref, i_ref, lbl_ref, do_ref, sst_ref, dxq_ref, dxf_ref, dxi_ref, dlb_ref,
             dst_ref, states_ref, dstates_ref, lslab_ref, kslab_ref):
        @pl.when(pl.program_id(1) == 0)
        def _():
            dst_ref[...] = jnp.zeros_like(dst_ref)
            dlb_ref[...] = jnp.zeros_like(dlb_ref)

        lb, _ = _lower_bound(lbl_ref[...])
        prefix, suffix = _chunk_masks(rb)
        c3 = lambda a: a.reshape(nc, CH, HG_D)
        flat = lambda a: a.reshape(rb, HG_D)
        xq = q_ref[...]
        sgq, q, sg, sn, f, kk = _hgrn_gates(xq, f_ref[...], lb)
        b3 = c3(_tri_dot(prefix, jnp.log(f)))
        q3, kk3, v3, do3 = c3(q), c3(kk), c3(i_ref[...]), c3(do_ref[...])
        bl3 = b3[:, CH - 1:CH, :]
        eb3, ebl3, dec3 = jnp.exp(b3), jnp.exp(bl3), jnp.exp(bl3 - b3)
        qe3, kd3 = q3 * eb3, kk3 * dec3
        x_upd, y_upd = _bdot(v3, kd3, 1, 1), _bdot(do3, qe3, 1, 1)
        st = sst_ref[...]
        for c in range(nc):
            states_ref[c] = st
            st = st * ebl3[c] + x_upd[c]
        dst = dst_ref[...]
        for c in reversed(range(nc)):
            dstates_ref[c] = dst
            dst = dst * ebl3[c] + y_upd[c]
        dst_ref[...] = dst
        states, dstates = states_ref[...], dstates_ref[...]
        dqe = _bdot(do3, states, 2, 1)
        dkd = _bdot(v3, dstates, 2, 1)
        row, col = _iota2((nc, CH, CH), 1), _iota2((nc, CH, CH), 2)
        tril, triu = row >= col, row <= col
        d_a = jnp.where(tril, _bdot(do3, v3, 2, 2), 0.0)
        d_at = jnp.where(triu, _bdot(v3, do3, 2, 2), 0.0)
        for t in range(CH):
            bt = b3[:, t:t + 1, :]
            lslab_ref[:, t * CH:(t + 1) * CH, :] = (q3 * jnp.exp(jnp.minimum(b3 - bt, 0.0))).astype(MM)
            kslab_ref[:, t * CH:(t + 1) * CH, :] = (kk3 * jnp.exp(jnp.minimum(bt - b3, 0.0))).astype(MM)
        r = _bdot(kslab_ref[...], q3, 2, 2)
        a_t = jnp.zeros((nc, CH, CH), F32)
        for t in range(CH):
            a_t = a_t + jnp.where(col == t, r[:, t * CH:(t + 1) * CH, :], 0.0)
        a_t = jnp.where(triu, a_t, 0.0)
        dv = _bdot(kd3, dstates, 2, 2) + _bdot(a_t, do3, 2, 1)
        sel = (_iota2((CH, CH * CH), 1) % CH == _iota2((CH, CH * CH), 0)).astype(MM)
        blockdiag = _iota2((nc, CH, CH * CH), 2) // CH == _iota2((nc, CH, CH * CH), 1)
        tile = lambda m: jnp.where(blockdiag, _dot(m.reshape(rb, CH), sel).reshape(nc, CH, CH * CH), 0.0)
        dq_in = _bdot(tile(d_a), kslab_ref[...], 2, 1)
        dk_in = _bdot(tile(d_at), lslab_ref[...], 2, 1)
        dkd_kd = dkd * kd3
        db = dqe * qe3 - dkd_kd + q3 * dq_in - kk3 * dk_in
        dbl = jnp.sum(dkd_kd, axis=1, keepdims=True) + jnp.sum(dstates * states, axis=1, keepdims=True) * ebl3
        last = _iota2((nc, CH, HG_D), 1) == CH - 1
        dg = _tri_dot(suffix, flat(db + jnp.where(last, dbl, 0.0)))
        df = dg / f - flat(dkd * dec3 + dk_in)
        dxq_ref[...] = flat(dqe * eb3 + dq_in) * (sgq * (1.0 + xq * (1.0 - sgq)))
        dxf_ref[...] = df * (1.0 - lb) * sg * sn
        dxi_ref[...] = flat(dv)
        dlb_ref[...] += jnp.sum(df * sn, axis=0, keepdims=True)

    rev = lambda i: nb - 1 - i
    sec = lambda j: pl.BlockSpec((None, rb, HG_D), lambda h, i, j=j: (j, rev(i), h))
    blk = pl.BlockSpec((rb, HG_D), lambda h, i: (rev(i), h))
    return pl.pallas_call(
        body, name="hgrn_bwd", grid=(HG_HEADS, nb),
        in_specs=[sec(0), sec(1), sec(2), pl.BlockSpec((2, HG_D), lambda h, i: (0, h)), blk,
                  pl.BlockSpec((None, None, HG_D, HG_D), lambda h, i: (rev(i), h, 0, 0))],
        out_specs=[blk, blk, blk, pl.BlockSpec((1, HG_D), lambda h, i: (0, h))],
        out_shape=[jax.ShapeDtypeStruct((s, SEC_W), F32)] * 3 + [jax.ShapeDtypeStruct((1, SEC_W), F32)],
        scratch_shapes=[pltpu.VMEM((HG_D, HG_D), F32), pltpu.VMEM((nc, HG_D, HG_D), F32),
                        pltpu.VMEM((nc, HG_D, HG_D), F32),
                        pltpu.VMEM((nc, CH * CH, HG_D), MM), pltpu.VMEM((nc, CH * CH, HG_D), MM)],
        compiler_params=_params("parallel", "arbitrary"),
    )(proj, proj, proj, lb_logits, d_o, sst)


def _rope_tables(s):
    half = AT_DH // 2
    inv_freq = 1.0 / (ROPE_THETA ** (jnp.arange(half, dtype=F32) / half))
    ang = jnp.arange(s, dtype=jnp.int32).astype(F32)[:, None] * inv_freq[None, :]
    cos, sin = jnp.cos(ang), jnp.sin(ang)
    return jnp.concatenate([cos] * 4, axis=-1), jnp.concatenate([-sin, sin] * 2, axis=-1)


def _rope128(x, cos, sin):
    lo = (_iota2(x.shape, 1) % AT_DH) < AT_DH // 2
    rot = jnp.where(lo, pltpu.roll(x, 128 - AT_DH // 2, 1), pltpu.roll(x, AT_DH // 2, 1))
    return x * cos + rot * sin


def _attn_prep(proj, cos, sin, tm=512):
    s = proj.shape[1]

    def body(q_ref, k_ref, v_ref, cos_ref, sin_ref, qo_ref, ko_ref, vo_ref):
        c, sn = cos_ref[...], sin_ref[...]
        for j in range(SEC_W // 128):
            sl = slice(j * 128, (j + 1) * 128)
            qo_ref[:, sl] = (_rope128(q_ref[:, sl], c, sn) * (AT_DH ** -0.5)).astype(MM)
            ko_ref[:, sl] = _rope128(k_ref[:, sl], c, sn).astype(MM)
        vo_ref[...] = v_ref[...].astype(MM)

    sec = lambda j: pl.BlockSpec((None, tm, SEC_W), lambda i, j=j: (j, i, 0))
    tab = pl.BlockSpec((tm, 128), lambda i: (i, 0))
    out = pl.BlockSpec((tm, SEC_W), lambda i: (i, 0))
    return pl.pallas_call(
        body, name="attn_prep", grid=(s // tm,),
        in_specs=[sec(4), sec(5), sec(6), tab, tab], out_specs=[out] * 3,
        out_shape=[jax.ShapeDtypeStruct((s, SEC_W), MM)] * 3,
        compiler_params=_params("parallel"),
    )(proj, proj, proj, cos, sin)


def _scores(q, k_cur, k_prev, has_prev):
    row, col = _iota2((ATT_BLK, ATT_BLK), 0), _iota2((ATT_BLK, ATT_BLK), 1)
    s_c = jnp.where(col <= row, _dot_nt(q, k_cur), NEG)
    s_p = jnp.where((col >= row) & has_prev, _dot_nt(q, k_prev), NEG)
    return s_c, s_p


def _attn_fwd(qr, kr, vr, d):
    s = qr.shape[0]
    rows, cols = s // d, d * SEC_W
    nb = rows // ATT_BLK
    view = lambda a: a.reshape(rows, cols)

    def body(q_ref, kc_ref, kp_ref, vc_ref, vp_ref, o_ref, lse_ref):
        has_prev = pl.program_id(1) > 0
        outs, lses = [], []
        for h in range(AT_COLS // AT_DH):
            sl = slice(h * AT_DH, (h + 1) * AT_DH)
            s_c, s_p = _scores(q_ref[:, sl], kc_ref[:, sl], kp_ref[:, sl], has_prev)
            m = jnp.maximum(jnp.max(s_c, axis=-1, keepdims=True), jnp.max(s_p, axis=-1, keepdims=True))
            p_c, p_p = jnp.exp(s_c - m), jnp.exp(s_p - m)
            l = jnp.sum(p_c, axis=-1, keepdims=True) + jnp.sum(p_p, axis=-1, keepdims=True)
            outs.append((_dot(p_c, vc_ref[:, sl]) + _dot(p_p, vp_ref[:, sl])) / l)
            lses.append(jnp.broadcast_to(m + jnp.log(l), (ATT_BLK, AT_DH)))
        o_ref[...] = jnp.concatenate(outs, axis=-1)
        lse_ref[...] = jnp.concatenate(lses, axis=-1)

    cur = pl.BlockSpec((ATT_BLK, AT_COLS), lambda c, n: (n, c))
    prev = pl.BlockSpec((ATT_BLK, AT_COLS), lambda c, n: (jnp.maximum(n - 1, 0), c))
    o, lse = pl.pallas_call(
        body, name=f"attn_fwd_d{d}", grid=(cols // AT_COLS, nb),
        in_specs=[cur, cur, prev, cur, prev], out_specs=[cur, cur],
        out_shape=[jax.ShapeDtypeStruct((rows, cols), F32)] * 2,
        compiler_params=_params("parallel", "parallel"),
    )(view(qr), view(kr), view(kr), view(vr), view(vr))
    return o.reshape(s, SEC_W), lse.reshape(s, SEC_W)


def _attn_bwd_dq(qr, kr, vr, do, lse, delta, d):
    s = qr.shape[0]
    rows, cols = s // d, d * SEC_W
    nb = rows // ATT_BLK
    view = lambda a: a.reshape(rows, cols)

    def body(q_ref, kc_ref, kp_ref, vc_ref, vp_ref, do_ref, lse_ref, dl_ref, dq_ref):
        has_prev = pl.program_id(1) > 0
        outs = []
        for h in range(AT_COLS // AT_DH):
            sl = slice(h * AT_DH, (h + 1) * AT_DH)
            s_c, s_p = _scores(q_ref[:, sl], kc_ref[:, sl], kp_ref[:, sl], has_prev)
            lse_h, dl_h, do_h = lse_ref[:, sl][:, 0:1], dl_ref[:, sl][:, 0:1], do_ref[:, sl]
            ds_c = jnp.exp(s_c - lse_h) * (_dot_nt(do_h, vc_ref[:, sl]) - dl_h)
            ds_p = jnp.exp(s_p - lse_h) * (_dot_nt(do_h, vp_ref[:, sl]) - dl_h)
            outs.append((_dot(ds_c, kc_ref[:, sl]) + _dot(ds_p, kp_ref[:, sl])) * (AT_DH ** -0.5))
        dq_ref[...] = jnp.concatenate(outs, axis=-1)

    cur = pl.BlockSpec((ATT_BLK, AT_COLS), lambda c, n: (n, c))
    prev = pl.BlockSpec((ATT_BLK, AT_COLS), lambda c, n: (jnp.maximum(n - 1, 0), c))
    dq = pl.pallas_call(
        body, name=f"attn_bwd_dq_d{d}", grid=(cols // AT_COLS, nb),
        in_specs=[cur, cur, prev, cur, prev, cur, cur, cur], out_specs=cur,
        out_shape=jax.ShapeDtypeStruct((rows, cols), F32),
        compiler_params=_params("parallel", "parallel"),
    )(view(qr), view(kr), view(kr), view(vr), view(vr), view(do), view(lse), view(delta))
    return dq.reshape(s, SEC_W)


def _attn_bwd_dkv(qr, kr, vr, do, lse, delta, d):
    s = qr.shape[0]
    rows, cols = s // d, d * SEC_W
    nb = rows // ATT_BLK
    view = lambda a: a.reshape(rows, cols)

    def body(k_ref, v_ref, qc_ref, qn_ref, doc_ref, don_ref, lsec_ref, lsen_ref, dlc_ref, dln_ref,
             dk_ref, dv_ref):
        has_next = pl.program_id(1) < nb - 1
        row, col = _iota2((ATT_BLK, ATT_BLK), 0), _iota2((ATT_BLK, ATT_BLK), 1)
        dks, dvs = [], []
        for h in range(AT_COLS // AT_DH):
            sl = slice(h * AT_DH, (h + 1) * AT_DH)
            k, v = k_ref[:, sl], v_ref[:, sl]
            dk, dv = 0.0, 0.0
            for q_ref, do_ref, lse_ref, dl_ref, mask in (
                    (qc_ref, doc_ref, lsec_ref, dlc_ref, col <= row),
                    (qn_ref, don_ref, lsen_ref, dln_ref, (col >= row) & has_next)):
                q, do_h = q_ref[:, sl], do_ref[:, sl]
                p = jnp.exp(jnp.where(mask, _dot_nt(q, k), NEG) - lse_ref[:, sl][:, 0:1])
                ds = p * (_dot_nt(do_h, v) - dl_ref[:, sl][:, 0:1])
                dv = dv + _dot_tn(p, do_h)
                dk = dk + _dot_tn(ds, q)
            dks.append(dk)
            dvs.append(dv)
        dk_ref[...] = jnp.concatenate(dks, axis=-1)
        dv_ref[...] = jnp.concatenate(dvs, axis=-1)

    cur = pl.BlockSpec((ATT_BLK, AT_COLS), lambda c, n: (n, c))
    nxt = pl.BlockSpec((ATT_BLK, AT_COLS), lambda c, n: (jnp.minimum(n + 1, nb - 1), c))
    dk, dv = pl.pallas_call(
        body, name=f"attn_bwd_dkv_d{d}", grid=(cols // AT_COLS, nb),
        in_specs=[cur, cur, cur, nxt, cur, nxt, cur, nxt, cur, nxt], out_specs=[cur, cur],
        out_shape=[jax.ShapeDtypeStruct((rows, cols), F32)] * 2,
        compiler_params=_params("parallel", "parallel"),
    )(view(kr), view(vr), view(qr), view(qr), view(do), view(do), view(lse), view(lse), view(delta), view(delta))
    return dk.reshape(s, SEC_W), dv.reshape(s, SEC_W)


def _head_sum(a, width):
    parts = []
    for j in range(a.shape[1] // width):
        sm = jnp.sum(a[:, j * width:(j + 1) * width], axis=-1, keepdims=True)
        parts.append(jnp.broadcast_to(sm, (a.shape[0], width)))
    return jnp.concatenate(parts, axis=-1)


def _mid(x, tgt, proj, o_hg, o_at, lse_at, hg_norm_w, final_norm_w, wo_all, tm=256):
    s = x.shape[0]
    nb = s // tm

    def body(x_ref, t_ref, hgz_ref, atz_ref, ohg_ref, o1_ref, o2_ref, o3_ref, l1_ref, l2_ref, l3_ref,
             g_ref, fw_ref, wo_ref,
             dh_ref, dohg_ref, dhgz_ref, datz_ref, do1_ref, do2_ref, do3_ref, dl1_ref, dl2_ref, dl3_ref,
             gwo_ref, gfw_ref, ghg_ref, loss_ref):
        @pl.when(pl.program_id(0) == 0)
        def _():
            gwo_ref[...] = jnp.zeros_like(gwo_ref)
            gfw_ref[...] = jnp.zeros_like(gfw_ref)
            ghg_ref[...] = jnp.zeros_like(ghg_ref)
            loss_ref[...] = jnp.zeros_like(loss_ref)

        ohg, g = ohg_ref[...], g_ref[...]
        rs = lax.rsqrt(_head_sum(ohg * ohg, HG_D) * (1.0 / HG_D) + NORM_EPS)
        on = ohg * rs
        hgz = hgz_ref[...]
        sz = _sigmoid(hgz)
        gate_hg = hgz * sz
        lses = (l1_ref[...], l2_ref[...], l3_ref[...])
        outs = (o1_ref[...], o2_ref[...], o3_ref[...])
        mx = jnp.maximum(jnp.maximum(lses[0], lses[1]), lses[2])
        es = [jnp.exp(l - mx) for l in lses]
        den = es[0] + es[1] + es[2]
        ws = [e / den for e in es]
        oat = ws[0] * outs[0] + ws[1] * outs[1] + ws[2] * outs[2]
        atz = atz_ref[...]
        sa = _sigmoid(atz)
        gate_at = atz * sa
        mixed = jnp.concatenate([on * g * gate_hg, oat * gate_at], axis=-1).astype(MM)
        h = x_ref[...] + jnp.dot(mixed, wo_ref[...], preferred_element_type=F32)
        rstd = lax.rsqrt(jnp.mean(h * h, axis=-1, keepdims=True) + NORM_EPS)
        hn = h * rstd
        fw = fw_ref[...]
        err = hn * fw - t_ref[...]
        loss_ref[...] += 0.5 * jnp.sum(jnp.mean(err * err, axis=-1, keepdims=True), axis=0, keepdims=True)
        dout = err * (1.0 / D_MODEL)
        gfw_ref[...] += jnp.sum(dout * hn, axis=0, keepdims=True)
        dhn = dout * fw
        dh = rstd * (dhn - hn * jnp.mean(dhn * hn, axis=-1, keepdims=True))
        dh_ref[...] = dh
        dh_mm = dh.astype(MM)
        gwo_ref[...] += _dot_tn(mixed, dh_mm)
        dmixed = _dot_nt(dh_mm, wo_ref[...])
        dm_hg = dmixed[:, :SEC_W]
        d_ong = dm_hg * gate_hg
        dhgz_ref[...] = dm_hg * (on * g) * (sz * (1.0 + hgz * (1.0 - sz)))
        ghg_ref[...] += jnp.sum(d_ong * on, axis=0, keepdims=True)
        d_on = d_ong * g
        dohg_ref[...] = rs * (d_on - on * (_head_sum(d_on * on, HG_D) * (1.0 / HG_D)))
        dm_at = dmixed[:, SEC_W:]
        d_oat = dm_at * gate_at
        datz_ref[...] = dm_at * oat * (sa * (1.0 + atz * (1.0 - sa)))
        drow = _head_sum(d_oat * oat, AT_DH)
        for w, do_ref, dl_ref in zip(ws, (do1_ref, do2_ref, do3_ref), (dl1_ref, dl2_ref, dl3_ref)):
            do_ref[...] = (w * d_oat).astype(MM)
            dl_ref[...] = w * drow

    row = lambda w: pl.BlockSpec((tm, w), lambda i: (i, 0))
    sec = lambda j: pl.BlockSpec((None, tm, SEC_W), lambda i, j=j: (j, i, 0))
    const = lambda shp: pl.BlockSpec(shp, lambda i: (0,) * len(shp))
    half = row(SEC_W)
    return pl.pallas_call(
        body, name="mid", grid=(nb,),
        in_specs=[row(D_MODEL), row(D_MODEL), sec(3), sec(7)] + [half] * 7
                 + [const((1, SEC_W)), const((1, D_MODEL)), const((D_MODEL, D_MODEL))],
        out_specs=[row(D_MODEL)] + [half] * 9
                  + [const((D_MODEL, D_MODEL)), const((1, D_MODEL)), const((1, SEC_W)), const((1, 1))],
        out_shape=[jax.ShapeDtypeStruct((s, D_MODEL), F32)] + [jax.ShapeDtypeStruct((s, SEC_W), F32)] * 3
                  + [jax.ShapeDtypeStruct((s, SEC_W), MM)] * 3 + [jax.ShapeDtypeStruct((s, SEC_W), F32)] * 3
                  + [jax.ShapeDtypeStruct((D_MODEL, D_MODEL), F32), jax.ShapeDtypeStruct((1, D_MODEL), F32),
                     jax.ShapeDtypeStruct((1, SEC_W), F32), jax.ShapeDtypeStruct((1, 1), F32)],
        compiler_params=_params("arbitrary"),
    )(x, tgt, proj, proj, o_hg, *o_at, *lse_at, hg_norm_w, final_norm_w, wo_all)


def _inproj_bwd_x(x, norm_w, w_all, dh, dsec, dq_r, dk_r, dv, cos, sin, tm=256):
    s = x.shape[0]

    def body(x_ref, nw_ref, w_ref, dh_ref, s0, s1, s2, s3, s7, q1, q2, q3, k1, k2, k3, v1, v2, v3,
             cos_ref, sin_ref, gx_ref, dp_ref, gnw_ref):
        @pl.when(pl.program_id(0) == 0)
        def _():
            gnw_ref[...] = jnp.zeros_like(gnw_ref)

        c, sn = cos_ref[...], -sin_ref[...]
        dq = q1[...] + q2[...] + q3[...]
        dk = k1[...] + k2[...] + k3[...]
        unrot = lambda a: jnp.concatenate(
            [_rope128(a[:, j * 128:(j + 1) * 128], c, sn) for j in range(SEC_W // 128)], axis=-1)
        secs = (s0[...], s1[...], s2[...], s3[...], unrot(dq), unrot(dk), v1[...] + v2[...] + v3[...], s7[...])
        du = jnp.zeros((tm, D_MODEL), F32)
        for j, dsj in enumerate(secs):
            dsj = dsj.astype(MM)
            dp_ref[j] = dsj
            du = du + _dot_nt(dsj, w_ref[j])
        xv, nw = x_ref[...], nw_ref[...]
        rstd = lax.rsqrt(jnp.mean(xv * xv, axis=-1, keepdims=True) + NORM_EPS)
        xn = xv * rstd
        gnw_ref[...] += jnp.sum(du * xn, axis=0, keepdims=True)
        dxn = du * nw
        gx_ref[...] = dh_ref[...] + rstd * (dxn - xn * jnp.mean(dxn * xn, axis=-1, keepdims=True))

    row = lambda w: pl.BlockSpec((tm, w), lambda i: (i, 0))
    const = lambda shp: pl.BlockSpec(shp, lambda i: (0,) * len(shp))
    return pl.pallas_call(
        body, name="inproj_bwd_x", grid=(s // tm,),
        in_specs=[row(D_MODEL), const((1, D_MODEL)), const((N_SEC, D_MODEL, SEC_W)), row(D_MODEL)]
                 + [row(SEC_W)] * 14 + [row(128), row(128)],
        out_specs=[row(D_MODEL), pl.BlockSpec((N_SEC, tm, SEC_W), lambda i: (0, i, 0)), const((1, D_MODEL))],
        out_shape=[jax.ShapeDtypeStruct((s, D_MODEL), F32), jax.ShapeDtypeStruct((N_SEC, s, SEC_W), MM),
                   jax.ShapeDtypeStruct((1, D_MODEL), F32)],
        compiler_params=_params("arbitrary"),
    )(x, norm_w, w_all, dh, *dsec, *dq_r, *dk_r, *dv, cos, sin)


def _inproj_bwd_w(x, norm_w, dproj, tm=256):
    s = x.shape[0]

    def body(x_ref, nw_ref, dp_ref, gw_ref):
        @pl.when(pl.program_id(1) == 0)
        def _():
            gw_ref[...] = jnp.zeros_like(gw_ref)

        xv = x_ref[...]
        rstd = lax.rsqrt(jnp.mean(xv * xv, axis=-1, keepdims=True) + NORM_EPS)
        u = (xv * rstd * nw_ref[...]).astype(MM)
        gw_ref[...] += _dot_tn(u, dp_ref[...])

    return pl.pallas_call(
        body, name="inproj_bwd_w", grid=(N_SEC, s // tm),
        in_specs=[pl.BlockSpec((tm, D_MODEL), lambda j, i: (i, 0)), pl.BlockSpec((1, D_MODEL), lambda j, i: (0, 0)),
                  pl.BlockSpec((None, tm, SEC_W), lambda j, i: (j, i, 0))],
        out_specs=pl.BlockSpec((None, D_MODEL, SEC_W), lambda j, i: (j, 0, 0)),
        out_shape=jax.ShapeDtypeStruct((N_SEC, D_MODEL, SEC_W), F32),
        compiler_params=_params("parallel", "arbitrary"),
    )(x, norm_w, dproj)


def _local_step(x, tgt, norm_w, w_all, lb_logits, hg_norm_w, wo_all, final_norm_w):
    s = x.shape[0]
    cos, sin = _rope_tables(s)
    proj = _inproj_fwd(x, norm_w, w_all)
    o_hg, sst = _hgrn_fwd(proj, lb_logits)
    qr, kr, vr = _attn_prep(proj, cos, sin)
    att = [_attn_fwd(qr, kr, vr, d) for d in DILATIONS]
    (dh, d_ohg, d_hgz, d_atz, do1, do2, do3, dl1, dl2, dl3, gwo, gfw, ghg, loss) = _mid(
        x, tgt, proj, o_hg, [a[0] for a in att], [a[1] for a in att], hg_norm_w, final_norm_w[None, :], wo_all)
    dxq, dxf, dxi, dlb = _hgrn_bwd(proj, lb_logits, d_ohg, sst)
    dq_r, dk_r, dv = [], [], []
    for d, a, do, dl in zip(DILATIONS, att, (do1, do2, do3), (dl1, dl2, dl3)):
        dq_r.append(_attn_bwd_dq(qr, kr, vr, do, a[1], dl, d))
        dk_d, dv_d = _attn_bwd_dkv(qr, kr, vr, do, a[1], dl, d)
        dk_r.append(dk_d)
        dv.append(dv_d)
    gx, dproj, gnw = _inproj_bwd_x(x, norm_w, w_all, dh, (dxq, dxf, dxi, d_hgz, d_atz), dq_r, dk_r, dv, cos, sin)
    gwi = _inproj_bwd_w(x, norm_w, dproj)
    small = jnp.concatenate([gnw, jnp.concatenate([dlb, ghg], axis=-1), gfw,
                             jnp.pad(loss, ((0, 0), (0, D_MODEL - 1)))], axis=0)
    return gx, gwi, gwo, small


def _coords():
    return lax.axis_index("x"), lax.axis_index("y"), lax.axis_index("c")


def _gather_weights(w_in, w_out):
    wo_rows = w_out.shape[0]

    def body(wi_ref, wo_ref, wi_all, wo_all, send_sems, recv_sems):
        x, y, c = _coords()
        me, sibling = (x, y, c), (x, y, 1 - c)
        chips = [(1 - x, y), (x, 1 - y), (1 - x, 1 - y)]
        slot = lambda p: 4 * p[0] + 2 * p[1] + p[2]

        def copies(k, block, to):
            return [pltpu.make_async_remote_copy(
                src_ref=ref.at[slot(block)], dst_ref=ref.at[slot(block)], send_sem=send_sems.at[a, k],
                recv_sem=recv_sems.at[a, k], device_id=to, device_id_type=MESH)
                for a, ref in enumerate((wi_all, wo_all))]

        wi_all[slot(me)] = wi_ref[...].astype(MM)
        wo_all[slot(me)] = wo_ref[...].astype(MM)
        first = copies(0, me, sibling)
        for j, chip in enumerate(chips):
            first += copies(1 + j, me, (*chip, c))
        for cp in first:
            cp.start()
        passed = []
        for j, chip in enumerate(chips):
            for cp in copies(1 + j, (*chip, c), me):
                cp.wait_recv()
            fwd = copies(4 + j, (*chip, c), sibling)
            for cp in fwd:
                cp.start()
            passed += fwd
        for cp in copies(0, sibling, me):
            cp.wait_recv()
        for j, chip in enumerate(chips):
            for cp in copies(4 + j, (*chip, 1 - c), me):
                cp.wait_recv()
        for cp in first + passed:
            cp.wait_send()

    vmem = pl.BlockSpec(memory_space=pltpu.VMEM)
    return pl.pallas_call(
        body, name="gather_weights",
        in_specs=[vmem, vmem], out_specs=[vmem, vmem],
        out_shape=[jax.ShapeDtypeStruct((N_DEV, D_MODEL, SEC_W), MM),
                   jax.ShapeDtypeStruct((N_DEV, wo_rows, D_MODEL), MM)],
        scratch_shapes=[pltpu.SemaphoreType.DMA((2, 7)), pltpu.SemaphoreType.DMA((2, 7))],
        compiler_params=pltpu.CompilerParams(vmem_limit_bytes=VMEM_LIMIT),
    )(w_in, w_out)


def _exchange_grads(gwi, gwo, small):
    def body(gwi_ref, gwo_ref, sm_ref, li_ref, lo_ref, ls_ref, send_sems, recv_sems, local_sems):
        x, y, c = _coords()
        me = 4 * x + 2 * y + c
        refs = ((gwi_ref, li_ref), (gwo_ref, lo_ref))
        own = [pltpu.make_async_copy(src.at[me], dst.at[me], local_sems.at[a]) for a, (src, dst) in enumerate(refs)]
        own.append(pltpu.make_async_copy(sm_ref, ls_ref.at[me], local_sems.at[2]))
        for cp in own:
            cp.start()
        sends = []
        for k in range(1, N_DEV):
            px, py, pc = x ^ (k >> 2), y ^ ((k >> 1) & 1), c ^ (k & 1)
            peer = 4 * px + 2 * py + pc
            for a, (src, dst) in enumerate(refs):
                sends.append(pltpu.make_async_remote_copy(
                    src_ref=src.at[peer], dst_ref=dst.at[me], send_sem=send_sems.at[a, k - 1],
                    recv_sem=recv_sems.at[a, k - 1], device_id=(px, py, pc), device_id_type=MESH))
            sends.append(pltpu.make_async_remote_copy(
                src_ref=sm_ref, dst_ref=ls_ref.at[me], send_sem=send_sems.at[2, k - 1],
                recv_sem=recv_sems.at[2, k - 1], device_id=(px, py, pc), device_id_type=MESH))
        for cp in sends:
            cp.start()
        for cp in sends:
            cp.wait_recv()
        for cp in sends:
            cp.wait_send()
        for cp in own:
            cp.wait()

    hbm = pl.BlockSpec(memory_space=pl.ANY)
    return pl.pallas_call(
        body, name="exchange_grads",
        in_specs=[hbm, hbm, hbm], out_specs=[hbm, hbm, hbm],
        out_shape=[jax.ShapeDtypeStruct(gwi.shape, F32), jax.ShapeDtypeStruct(gwo.shape, F32),
                   jax.ShapeDtypeStruct((N_DEV,) + small.shape, F32)],
        scratch_shapes=[pltpu.SemaphoreType.DMA((3, 7)), pltpu.SemaphoreType.DMA((3, 7)),
                        pltpu.SemaphoreType.DMA((3,))],
    )(gwi, gwo, small)


def _adamw(w, g, m, v):
    m = ADAM_B1 * m + (1.0 - ADAM_B1) * g
    v = ADAM_B2 * v + (1.0 - ADAM_B2) * (g * g)
    m_hat = m / (1.0 - ADAM_B1 ** ADAM_STEP)
    v_hat = v / (1.0 - ADAM_B2 ** ADAM_STEP)
    return -ADAM_LR * (m_hat / (jnp.sqrt(v_hat) + ADAM_EPS) + ADAM_WD * w), m, v


def _slot_sum(ref):
    g = ref[0]
    for i in range(1, N_DEV):
        g = g + ref[i]
    return g


def _update_matrix(name, landed, w, m, v, rows):
    r, c = w.shape

    def body(l_ref, w_ref, m_ref, v_ref, g_ref, d_ref, nm_ref, nv_ref):
        g = _slot_sum(l_ref)
        g_ref[...] = g
        d_ref[...], nm_ref[...], nv_ref[...] = _adamw(w_ref[...], g, m_ref[...], v_ref[...])

    blk = pl.BlockSpec((rows, c), lambda i: (i, 0))
    return pl.pallas_call(
        body, name=name, grid=(r // rows,),
        in_specs=[pl.BlockSpec((N_DEV, rows, c), lambda i: (0, i, 0)), blk, blk, blk],
        out_specs=[blk] * 4, out_shape=[jax.ShapeDtypeStruct((r, c), F32)] * 4,
        compiler_params=_params("parallel"),
    )(landed, w, m, v)


def _update_small(landed, lb_logits, ws, ms, vs):
    def body(l_ref, lbl_ref, w_ref, m_ref, v_ref, g_ref, d_ref, nm_ref, nv_ref, loss_ref):
        tot = _slot_sum(l_ref)
        _, dlb = _lower_bound(lbl_ref[...])
        g_lb = tot[1:2, :SEC_W] * dlb
        g = jnp.concatenate([tot[0:1], jnp.concatenate([g_lb, -g_lb], axis=-1),
                             jnp.pad(tot[1:2, SEC_W:], ((0, 0), (0, SEC_W))), tot[2:3]], axis=0)
        g_ref[...] = g
        d_ref[...], nm_ref[...], nv_ref[...] = _adamw(w_ref[...], g, m_ref[...], v_ref[...])
        loss_ref[...] = tot[3:4, 0:1]

    vmem = pl.BlockSpec(memory_space=pltpu.VMEM)
    return pl.pallas_call(
        body, name="update_small", in_specs=[vmem] * 5, out_specs=[vmem] * 5,
        out_shape=[jax.ShapeDtypeStruct((4, D_MODEL), F32)] * 4 + [jax.ShapeDtypeStruct((1, 1), F32)],
    )(landed, lb_logits, ws, ms, vs)


def _pack_small(norm_w, lb_logits, hg_norm_w, final_norm_w):
    return jnp.concatenate([norm_w, lb_logits.reshape(1, D_MODEL),
                            jnp.pad(hg_norm_w, ((0, 0), (0, D_MODEL - SEC_W))), final_norm_w[None, :]], axis=0)


def _unpack_small(a):
    return a[0:1], a[1].reshape(2, SEC_W), a[2:3, :SEC_W], a[3]


def kernel(x, norm_w, w_in, hgrn_lb_logits, hg_norm_w, w_out, final_norm_w, loss_target, m_norm_w, m_w_in, m_hgrn_lb_logits, m_hg_norm_w, m_w_out, m_final_norm_w, v_norm_w, v_w_in, v_hgrn_lb_logits, v_hg_norm_w, v_w_out, v_final_norm_w):
    w_all, wo_all = _gather_weights(w_in[0], w_out[0])
    gx, gwi, gwo, small = _local_step(x[0], loss_target[0], norm_w, w_all, hgrn_lb_logits, hg_norm_w,
                                      wo_all.reshape(D_MODEL, D_MODEL), final_norm_w)
    li, lo, ls = _exchange_grads(gwi, gwo.reshape(N_DEV, D_MODEL // N_DEV, D_MODEL), small)
    g_wi, d_wi, nm_wi, nv_wi = _update_matrix("update_w_in", li, w_in[0], m_w_in[0], v_w_in[0], 256)
    g_wo, d_wo, nm_wo, nv_wo = _update_matrix("update_w_out", lo, w_out[0], m_w_out[0], v_w_out[0], 128)
    g_s, d_s, nm_s, nv_s, loss = _update_small(
        ls, hgrn_lb_logits, _pack_small(norm_w, hgrn_lb_logits, hg_norm_w, final_norm_w),
        _pack_small(m_norm_w, m_hgrn_lb_logits, m_hg_norm_w, m_final_norm_w),
        _pack_small(v_norm_w, v_hgrn_lb_logits, v_hg_norm_w, v_final_norm_w))
    outs = []
    for small_out, wi, wo in ((g_s, g_wi, g_wo), (d_s, d_wi, d_wo), (nm_s, nm_wi, nm_wo), (nv_s, nv_wi, nv_wo)):
        nw, lb, hg, fw = _unpack_small(small_out)
        outs += [nw, wi[None], lb, hg, wo[None], fw]
    return (loss[0, 0], gx[None], *outs)
```

```python
import functools

import jax
import jax.numpy as jnp
from jax import lax
from jax.experimental import pallas as pl
from jax.experimental.pallas import tpu as pltpu

F32 = jnp.float32
MM = jnp.bfloat16
NORM_EPS = 1e-6
NEG = -1e30
N_DEV = 8
D_MODEL = 1024
N_SEC = 8
SEC_W = 512
HG_HEADS = 4
HG_D = 128
AT_HEADS = 8
AT_DH = 64
ATT_BLK = 128
AT_COLS = 512
DILATIONS = (1, 4, 16)
ROPE_THETA = 10000.0
CH = 16
LB_LO, LB_HI = 1e-6, 1.0 - 1e-6
ADAM_LR, ADAM_B1, ADAM_B2, ADAM_EPS, ADAM_WD, ADAM_STEP = 0.001, 0.9, 0.999, 1e-08, 0.01, 10
VMEM_LIMIT = 56 * 1024 * 1024
MESH = pl.DeviceIdType.MESH


def _params(*sem):
    return pltpu.CompilerParams(dimension_semantics=sem, vmem_limit_bytes=VMEM_LIMIT)


def _sigmoid(x):
    return 1.0 / (1.0 + jnp.exp(-x))


def _dot(a, b):
    return jnp.dot(a.astype(MM), b.astype(MM), preferred_element_type=F32)


def _dot_nt(a, b):
    return lax.dot_general(a.astype(MM), b.astype(MM), (((1,), (1,)), ((), ())), preferred_element_type=F32)


def _dot_tn(a, b):
    return lax.dot_general(a.astype(MM), b.astype(MM), (((0,), (0,)), ((), ())), preferred_element_type=F32)


def _tri_dot(tri, g):
    g1 = g.astype(jnp.bfloat16)
    r1 = g - g1.astype(F32)
    g2 = r1.astype(jnp.bfloat16)
    g3 = (r1 - g2.astype(F32)).astype(jnp.bfloat16)
    t = tri.astype(jnp.bfloat16)
    d = functools.partial(jnp.dot, preferred_element_type=F32)
    return d(t, g1) + d(t, g2) + d(t, g3)


def _lower_bound(lbl):
    l0, l1 = lbl[0:1, :], lbl[1:2, :]
    m = jnp.maximum(l0, l1)
    e0, e1 = jnp.exp(l0 - m), jnp.exp(l1 - m)
    p = e0 / (e0 + e1)
    inside = (p >= LB_LO) & (p <= LB_HI)
    return jnp.clip(p, LB_LO, LB_HI), jnp.where(inside, p * (e1 / (e0 + e1)), 0.0)


def _iota2(shape, dim):
    return lax.broadcasted_iota(jnp.int32, shape, dim)


def _inproj_fwd(x, norm_w, w_all, tm=256):
    s = x.shape[0]

    def body(x_ref, nw_ref, w_ref, proj_ref):
        xv = x_ref[...]
        rstd = lax.rsqrt(jnp.mean(xv * xv, axis=-1, keepdims=True) + NORM_EPS)
        u = (xv * rstd * nw_ref[...]).astype(MM)
        for j in range(N_SEC):
            proj_ref[j] = jnp.dot(u, w_ref[j], preferred_element_type=F32)

    return pl.pallas_call(
        body, name="inproj_fwd", grid=(s // tm,),
        in_specs=[pl.BlockSpec((tm, D_MODEL), lambda i: (i, 0)),
                  pl.BlockSpec((1, D_MODEL), lambda i: (0, 0)),
                  pl.BlockSpec((N_SEC, D_MODEL, SEC_W), lambda i: (0, 0, 0))],
        out_specs=pl.BlockSpec((N_SEC, tm, SEC_W), lambda i: (0, i, 0)),
        out_shape=jax.ShapeDtypeStruct((N_SEC, s, SEC_W), F32),
        compiler_params=_params("parallel"),
    )(x, norm_w, w_all)


def _hgrn_gates(xq, xf, lb):
    sgq = _sigmoid(xq)
    sg = _sigmoid(xf)
    sn = _sigmoid(-xf)
    f = lb + (1.0 - lb) * sg
    return sgq, xq * sgq, sg, sn, f, (1.0 - lb) * sn


def _bdot(a, b, ca, cb):
    return lax.dot_general(a.astype(MM), b.astype(MM), (((ca,), (cb,)), ((0,), (0,))), preferred_element_type=F32)


def _chunk_masks(rb):
    row, col = _iota2((rb, rb), 0), _iota2((rb, rb), 1)
    same = (row // CH) == (col // CH)
    return same & (row >= col), same & (row <= col)


def _hgrn_fwd(proj, lb_logits, rb=256):
    s = proj.shape[1]
    nb, nc = s // rb, rb // CH

    def body(q_ref, f_ref, i_ref, lbl_ref, o_ref, sst_ref, st_ref, slab_ref, states_ref):
        @pl.when(pl.program_id(1) == 0)
        def _():
            st_ref[...] = jnp.zeros_like(st_ref)

        sst_ref[...] = st_ref[...]
        lb, _ = _lower_bound(lbl_ref[...])
        prefix, _ = _chunk_masks(rb)
        c3 = lambda a: a.reshape(nc, CH, HG_D)
        _, q, _, _, f, kk = _hgrn_gates(q_ref[...], f_ref[...], lb)
        b3 = c3(_tri_dot(prefix, jnp.log(f)))
        q3, kk3, v3 = c3(q), c3(kk), c3(i_ref[...])
        bl3 = b3[:, CH - 1:CH, :]
        for t in range(CH):
            slab_ref[:, t * CH:(t + 1) * CH, :] = (q3 * jnp.exp(jnp.minimum(b3 - b3[:, t:t + 1, :], 0.0))).astype(MM)
        r = _bdot(slab_ref[...], kk3, 2, 2)
        row, col = _iota2((nc, CH, CH), 1), _iota2((nc, CH, CH), 2)
        a = jnp.zeros((nc, CH, CH), F32)
        for t in range(CH):
            a = a + jnp.where(col == t, r[:, t * CH:(t + 1) * CH, :], 0.0)
        a = jnp.where(row >= col, a, 0.0)
        x_upd = _bdot(v3, kk3 * jnp.exp(bl3 - b3), 1, 1)
        ebl3 = jnp.exp(bl3)
        st = st_ref[...]
        for c in range(nc):
            states_ref[c] = st
            st = st * ebl3[c] + x_upd[c]
        st_ref[...] = st
        o3 = _bdot(q3 * jnp.exp(b3), states_ref[...], 2, 2) + _bdot(a, v3, 2, 1)
        o_ref[...] = o3.reshape(rb, HG_D)

    sec = lambda j: pl.BlockSpec((None, rb, HG_D), lambda h, i, j=j: (j, i, h))
    return pl.pallas_call(
        body, name="hgrn_fwd", grid=(HG_HEADS, nb),
        in_specs=[sec(0), sec(1), sec(2), pl.BlockSpec((2, HG_D), lambda h, i: (0, h))],
        out_specs=[pl.BlockSpec((rb, HG_D), lambda h, i: (i, h)),
                   pl.BlockSpec((None, None, HG_D, HG_D), lambda h, i: (i, h, 0, 0))],
        out_shape=[jax.ShapeDtypeStruct((s, SEC_W), F32),
                   jax.ShapeDtypeStruct((nb, HG_HEADS, HG_D, HG_D), F32)],
        scratch_shapes=[pltpu.VMEM((HG_D, HG_D), F32), pltpu.VMEM((nc, CH * CH, HG_D), MM),
                        pltpu.VMEM((nc, HG_D, HG_D), F32)],
        compiler_params=_params("parallel", "arbitrary"),
    )(proj, proj, proj, lb_logits)


def _hgrn_bwd(proj, lb_logits, d_o, sst, rb=256):
    s = proj.shape[1]
    nb, nc = s // rb, rb // CH

    def body(q_ref, f_ref, i_ref, lbl_ref, do_ref, sst_ref, dxq_ref, dxf_ref, dxi_ref, dlb_ref,
             dst_ref, states_ref, dstates_ref, lslab_ref, kslab_ref):
        @pl.when(pl.program_id(1) == 0)
        def _():
            dst_ref[...] = jnp.zeros_like(dst_ref)
            dlb_ref[...] = jnp.zeros_like(dlb_ref)

        lb, _ = _lower_bound(lbl_ref[...])
        prefix, suffix = _chunk_masks(rb)
        c3 = lambda a: a.reshape(nc, CH, HG_D)
        flat = lambda a: a.reshape(rb, HG_D)
        xq = q_ref[...]
        sgq, q, sg, sn, f, kk = _hgrn_gates(xq, f_ref[...], lb)
        b3 = c3(_tri_dot(prefix, jnp.log(f)))
        q3, kk3, v3, do3 = c3(q), c3(kk), c3(i_ref[...]), c3(do_ref[...])
        bl3 = b3[:, CH - 1:CH, :]
        eb3, ebl3, dec3 = jnp.exp(b3), jnp.exp(bl3), jnp.exp(bl3 - b3)
        qe3, kd3 = q3 * eb3, kk3 * dec3
        x_upd, y_upd = _bdot(v3, kd3, 1, 1), _bdot(do3, qe3, 1, 1)
        st = sst_ref[...]
        for c in range(nc):
            states_ref[c] = st
            st = st * ebl3[c] + x_upd[c]
        dst = dst_ref[...]
        for c in reversed(range(nc)):
            dstates_ref[c] = dst
            dst = dst * ebl3[c] + y_upd[c]
        dst_ref[...] = dst
        states, dstates = states_ref[...], dstates_ref[...]
        dqe = _bdot(do3, states, 2, 1)
        dkd = _bdot(v3, dstates, 2, 1)
        row, col = _iota2((nc, CH, CH), 1), _iota2((nc, CH, CH), 2)
        tril, triu = row >= col, row <= col
        d_a = jnp.where(tril, _bdot(do3, v3, 2, 2), 0.0)
        d_at = jnp.where(triu, _bdot(v3, do3, 2, 2), 0.0)
        for t in range(CH):
            bt = b3[:, t:t + 1, :]
            lslab_ref[:, t * CH:(t + 1) * CH, :] = (q3 * jnp.exp(jnp.minimum(b3 - bt, 0.0))).astype(MM)
            kslab_ref[:, t * CH:(t + 1) * CH, :] = (kk3 * jnp.exp(jnp.minimum(bt - b3, 0.0))).astype(MM)
        r = _bdot(kslab_ref[...], q3, 2, 2)
        a_t = jnp.zeros((nc, CH, CH), F32)
        for t in range(CH):
            a_t = a_t + jnp.where(col == t, r[:, t * CH:(t + 1) * CH, :], 0.0)
        a_t = jnp.where(triu, a_t, 0.0)
        dv = _bdot(kd3, dstates, 2, 2) + _bdot(a_t, do3, 2, 1)
        sel = (_iota2((CH, CH * CH), 1) % CH == _iota2((CH, CH * CH), 0)).astype(MM)
        blockdiag = _iota2((nc, CH, CH * CH), 2) // CH == _iota2((nc, CH, CH * CH), 1)
        tile = lambda m: jnp.where(blockdiag, _dot(m.reshape(rb, CH), sel).reshape(nc, CH, CH * CH), 0.0)
        dq_in = _bdot(tile(d_a), kslab_ref[...], 2, 1)
        dk_in = _bdot(tile(d_at), lslab_ref[...], 2, 1)
        dkd_kd = dkd * kd3
        db = dqe * qe3 - dkd_kd + q3 * dq_in - kk3 * dk_in
        dbl = jnp.sum(dkd_kd, axis=1, keepdims=True) + jnp.sum(dstates * states, axis=1, keepdims=True) * ebl3
        last = _iota2((nc, CH, HG_D), 1) == CH - 1
        dg = _tri_dot(suffix, flat(db + jnp.where(last, dbl, 0.0)))
        df = dg / f - flat(dkd * dec3 + dk_in)
        dxq_ref[...] = flat(dqe * eb3 + dq_in) * (sgq * (1.0 + xq * (1.0 - sgq)))
        dxf_ref[...] = df * (1.0 - lb) * sg * sn
        dxi_ref[...] = flat(dv)
        dlb_ref[...] += jnp.sum(df * sn, axis=0, keepdims=True)

    rev = lambda i: nb - 1 - i
    sec = lambda j: pl.BlockSpec((None, rb, HG_D), lambda h, i, j=j: (j, rev(i), h))
    blk = pl.BlockSpec((rb, HG_D), lambda h, i: (rev(i), h))
    return pl.pallas_call(
        body, name="hgrn_bwd", grid=(HG_HEADS, nb),
        in_specs=[sec(0), sec(1), sec(2), pl.BlockSpec((2, HG_D), lambda h, i: (0, h)), blk,
                  pl.BlockSpec((None, None, HG_D, HG_D), lambda h, i: (rev(i), h, 0, 0))],
        out_specs=[blk, blk, blk, pl.BlockSpec((1, HG_D), lambda h, i: (0, h))],
        out_shape=[jax.ShapeDtypeStruct((s, SEC_W), F32)] * 3 + [jax.ShapeDtypeStruct((1, SEC_W), F32)],
        scratch_shapes=[pltpu.VMEM((HG_D, HG_D), F32), pltpu.VMEM((nc, HG_D, HG_D), F32),
                        pltpu.VMEM((nc, HG_D, HG_D), F32),
                        pltpu.VMEM((nc, CH * CH, HG_D), MM), pltpu.VMEM((nc, CH * CH, HG_D), MM)],
        compiler_params=_params("parallel", "arbitrary"),
    )(proj, proj, proj, lb_logits, d_o, sst)


def _rope_tables(s):
    half = AT_DH // 2
    inv_freq = 1.0 / (ROPE_THETA ** (jnp.arange(half, dtype=F32) / half))
    ang = jnp.arange(s, dtype=jnp.int32).astype(F32)[:, None] * inv_freq[None, :]
    cos, sin = jnp.cos(ang), jnp.sin(ang)
    return jnp.concatenate([cos] * 4, axis=-1), jnp.concatenate([-sin, sin] * 2, axis=-1)


def _rope128(x, cos, sin):
    lo = (_iota2(x.shape, 1) % AT_DH) < AT_DH // 2
    rot = jnp.where(lo, pltpu.roll(x, 128 - AT_DH // 2, 1), pltpu.roll(x, AT_DH // 2, 1))
    return x * cos + rot * sin


def _attn_prep(proj, cos, sin, tm=512):
    s = proj.shape[1]

    def body(q_ref, k_ref, v_ref, cos_ref, sin_ref, qo_ref, ko_ref, vo_ref):
        c, sn = cos_ref[...], sin_ref[...]
        for j in range(SEC_W // 128):
            sl = slice(j * 128, (j + 1) * 128)
            qo_ref[:, sl] = (_rope128(q_ref[:, sl], c, sn) * (AT_DH ** -0.5)).astype(MM)
            ko_ref[:, sl] = _rope128(k_ref[:, sl], c, sn).astype(MM)
        vo_ref[...] = v_ref[...].astype(MM)

    sec = lambda j: pl.BlockSpec((None, tm, SEC_W), lambda i, j=j: (j, i, 0))
    tab = pl.BlockSpec((tm, 128), lambda i: (i, 0))
    out = pl.BlockSpec((tm, SEC_W), lambda i: (i, 0))
    return pl.pallas_call(
        body, name="attn_prep", grid=(s // tm,),
        in_specs=[sec(4), sec(5), sec(6), tab, tab], out_specs=[out] * 3,
        out_shape=[jax.ShapeDtypeStruct((s, SEC_W), MM)] * 3,
        compiler_params=_params("parallel"),
    )(proj, proj, proj, cos, sin)


def _band_mask(neighbour_ok, keys_major=False):
    row, col = _iota2((ATT_BLK, 2 * ATT_BLK), 0), _iota2((ATT_BLK, 2 * ATT_BLK), 1)
    near, own = col < ATT_BLK, col >= ATT_BLK
    if keys_major:
        return (near & (col <= row) & neighbour_ok) | (own & ((col - ATT_BLK) >= row))
    return (near & (col >= row) & neighbour_ok) | (own & ((col - ATT_BLK) <= row))


def _own_lanes(rows, h):
    lane = _iota2((rows, 128), 1)
    return (lane < AT_DH) if h == 0 else (lane >= AT_DH)


def _neg_pieces(rows, h):
    lane = _iota2((rows, 128), 1) - (AT_DH if h == 0 else 0)
    return jnp.where((lane >= 0) & (lane < 3), -1.0, 0.0).astype(MM)


def _attn_fwd(qr, kr, vr, d):
    s = qr.shape[0]
    rows, cols = s // d, d * SEC_W
    nb = rows // ATT_BLK
    view = lambda a: a.reshape(rows, cols)

    def body(q_ref, kc_ref, kp_ref, vc_ref, vp_ref, o_ref, lse_ref):
        valid = _band_mask(pl.program_id(1) > 0)
        ones = jnp.ones((2 * ATT_BLK, 128), MM)
        head0 = _own_lanes(ATT_BLK, 0)
        groups = [slice(g * 128, (g + 1) * 128) for g in range(AT_COLS // 128)]
        heads = [(sl, h) for sl in groups for h in range(2)]
        vbs = {sl.start: jnp.concatenate([vp_ref[:, sl], vc_ref[:, sl]], axis=0) for sl in groups}
        scs = []
        for sl, h in heads:
            q2 = q_ref[:, sl]
            kb = jnp.concatenate([kp_ref[:, sl], kc_ref[:, sl]], axis=0)
            qh = jnp.where(_own_lanes(ATT_BLK, h), q2, jnp.zeros_like(q2))
            scs.append(jnp.where(valid, _dot_nt(qh, kb), NEG))
        ms = [jnp.max(sc, axis=-1, keepdims=True) for sc in scs]
        ps = [jnp.exp(sc - m).astype(MM) for sc, m in zip(scs, ms)]
        ls = [jnp.dot(p, ones, preferred_element_type=F32) for p in ps]
        os_ = [jnp.dot(p, vbs[sl.start], preferred_element_type=F32) / l for p, l, (sl, _) in zip(ps, ls, heads)]
        lses = [m + jnp.log(l) for m, l in zip(ms, ls)]
        for g, sl in enumerate(groups):
            o_ref[:, sl] = jnp.where(head0, os_[2 * g], os_[2 * g + 1])
            lse_ref[:, sl] = jnp.where(head0, lses[2 * g], lses[2 * g + 1])

    cur = pl.BlockSpec((ATT_BLK, AT_COLS), lambda c, n: (n, c))
    prev = pl.BlockSpec((ATT_BLK, AT_COLS), lambda c, n: (jnp.maximum(n - 1, 0), c))
    o, lse = pl.pallas_call(
        body, name=f"attn_fwd_d{d}", grid=(cols // AT_COLS, nb),
        in_specs=[cur, cur, prev, cur, prev], out_specs=[cur, cur],
        out_shape=[jax.ShapeDtypeStruct((rows, cols), F32)] * 2,
        compiler_params=_params("parallel", "parallel"),
    )(view(qr), view(kr), view(kr), view(vr), view(vr))
    return o.reshape(s, SEC_W), lse.reshape(s, SEC_W)


def _attn_bwd_dq(qr, kr, vr, do, lse, delta, d):
    s = qr.shape[0]
    rows, cols = s // d, d * SEC_W
    nb = rows // ATT_BLK
    view = lambda a: a.reshape(rows, cols)

    def body(q_ref, kc_ref, kp_ref, vc_ref, vp_ref, do_ref, lse_ref, dl_ref, dq_ref):
        valid = _band_mask(pl.program_id(1) > 0)
        groups = [slice(g * 128, (g + 1) * 128) for g in range(AT_COLS // 128)]
        heads = [(sl, h) for sl in groups for h in range(2)]
        kbs = {sl.start: jnp.concatenate([kp_ref[:, sl], kc_ref[:, sl]], axis=0) for sl in groups}
        vbs = {sl.start: jnp.concatenate([vp_ref[:, sl], vc_ref[:, sl]], axis=0) for sl in groups}
        sms, dps = [], []
        for sl, h in heads:
            own, own_b, neg = _own_lanes(ATT_BLK, h), _own_lanes(2 * ATT_BLK, h), _neg_pieces(2 * ATT_BLK, h)
            sms.append(_dot_nt(jnp.where(own, q_ref[:, sl], lse_ref[:, sl]), jnp.where(own_b, kbs[sl.start], neg)))
            dps.append(_dot_nt(jnp.where(own, do_ref[:, sl], dl_ref[:, sl]), jnp.where(own_b, vbs[sl.start], neg)))
        dss = [(jnp.exp(jnp.where(valid, sm, NEG)) * dp).astype(MM) for sm, dp in zip(sms, dps)]
        dqs = [jnp.dot(ds, kbs[sl.start], preferred_element_type=F32) * (AT_DH ** -0.5)
               for ds, (sl, _) in zip(dss, heads)]
        for g, sl in enumerate(groups):
            dq_ref[:, sl] = jnp.where(_own_lanes(ATT_BLK, 0), dqs[2 * g], dqs[2 * g + 1])

    cur = pl.BlockSpec((ATT_BLK, AT_COLS), lambda c, n: (n, c))
    prev = pl.BlockSpec((ATT_BLK, AT_COLS), lambda c, n: (jnp.maximum(n - 1, 0), c))
    dq = pl.pallas_call(
        body, name=f"attn_bwd_dq_d{d}", grid=(cols // AT_COLS, nb),
        in_specs=[cur, cur, prev, cur, prev, cur, cur, cur], out_specs=cur,
        out_shape=jax.ShapeDtypeStruct((rows, cols), F32),
        compiler_params=_params("parallel", "parallel"),
    )(view(qr), view(kr), view(kr), view(vr), view(vr), view(do), view(lse), view(delta))
    return dq.reshape(s, SEC_W)


def _attn_bwd_dkv(qr, kr, vr, do, lse, delta, d):
    s = qr.shape[0]
    rows, cols = s // d, d * SEC_W
    nb = rows // ATT_BLK
    view = lambda a: a.reshape(rows, cols)

    def body(k_ref, v_ref, qc_ref, qn_ref, doc_ref, don_ref, lsec_ref, lsen_ref, dlc_ref, dln_ref,
             dk_ref, dv_ref):
        valid = _band_mask(pl.program_id(1) < nb - 1, keys_major=True)
        band = lambda nxt_ref, cur_ref, sl: jnp.concatenate([nxt_ref[:, sl], cur_ref[:, sl]], axis=0)
        groups = [slice(g * 128, (g + 1) * 128) for g in range(AT_COLS // 128)]
        heads = [(sl, h) for sl in groups for h in range(2)]
        qbs = {sl.start: band(qn_ref, qc_ref, sl) for sl in groups}
        dobs = {sl.start: band(don_ref, doc_ref, sl) for sl in groups}
        sms, dps = [], []
        for sl, h in heads:
            own, own_b, neg = _own_lanes(ATT_BLK, h), _own_lanes(2 * ATT_BLK, h), _neg_pieces(ATT_BLK, h)
            sms.append(_dot_nt(jnp.where(own, k_ref[:, sl], neg),
                               jnp.where(own_b, qbs[sl.start], band(lsen_ref, lsec_ref, sl))))
            dps.append(_dot_nt(jnp.where(own, v_ref[:, sl], neg),
                               jnp.where(own_b, dobs[sl.start], band(dln_ref, dlc_ref, sl))))
        ps = [jnp.exp(jnp.where(valid, sm, NEG)) for sm in sms]
        dss = [(p * dp).astype(MM) for p, dp in zip(ps, dps)]
        dvs = [jnp.dot(p.astype(MM), dobs[sl.start], preferred_element_type=F32) for p, (sl, _) in zip(ps, heads)]
        dks = [jnp.dot(ds, qbs[sl.start], preferred_element_type=F32) for ds, (sl, _) in zip(dss, heads)]
        head0 = _own_lanes(ATT_BLK, 0)
        for g, sl in enumerate(groups):
            dk_ref[:, sl] = jnp.where(head0, dks[2 * g], dks[2 * g + 1])
            dv_ref[:, sl] = jnp.where(head0, dvs[2 * g], dvs[2 * g + 1])

    cur = pl.BlockSpec((ATT_BLK, AT_COLS), lambda c, n: (n, c))
    nxt = pl.BlockSpec((ATT_BLK, AT_COLS), lambda c, n: (jnp.minimum(n + 1, nb - 1), c))
    dk, dv = pl.pallas_call(
        body, name=f"attn_bwd_dkv_d{d}", grid=(cols // AT_COLS, nb),
        in_specs=[cur, cur, cur, nxt, cur, nxt, cur, nxt, cur, nxt], out_specs=[cur, cur],
        out_shape=[jax.ShapeDtypeStruct((rows, cols), F32)] * 2,
        compiler_params=_params("parallel", "parallel"),
    )(view(kr), view(vr), view(qr), view(qr), view(do), view(do), view(lse), view(lse), view(delta), view(delta))
    return dk.reshape(s, SEC_W), dv.reshape(s, SEC_W)


def _head_sum(a, width):
    parts = []
    for j in range(a.shape[1] // width):
        sm = jnp.sum(a[:, j * width:(j + 1) * width], axis=-1, keepdims=True)
        parts.append(jnp.broadcast_to(sm, (a.shape[0], width)))
    return jnp.concatenate(parts, axis=-1)


def _partner_pieces(x):
    xs = jnp.concatenate([pltpu.roll(x[:, j * 128:(j + 1) * 128], AT_DH, 1) for j in range(x.shape[1] // 128)],
                         axis=-1)
    hi = xs.astype(jnp.bfloat16).astype(F32)
    mid = (xs - hi).astype(jnp.bfloat16).astype(F32)
    lo = (xs - hi - mid).astype(jnp.bfloat16).astype(F32)
    lane = _iota2(x.shape, 1) % AT_DH
    return jnp.where(lane == 0, hi, jnp.where(lane == 1, mid, jnp.where(lane == 2, lo, 0.0))).astype(MM)


def _mid(x, tgt, proj, o_hg, o_at, lse_at, hg_norm_w, final_norm_w, wo_all, tm=256):
    s = x.shape[0]
    nb = s // tm

    def body(x_ref, t_ref, hgz_ref, atz_ref, ohg_ref, o1_ref, o2_ref, o3_ref, l1_ref, l2_ref, l3_ref,
             g_ref, fw_ref, wo_ref,
             dh_ref, dohg_ref, dhgz_ref, datz_ref, do1_ref, do2_ref, do3_ref, dl1_ref, dl2_ref, dl3_ref,
             lp1_ref, lp2_ref, lp3_ref,
             gwo_ref, gfw_ref, ghg_ref, loss_ref):
        @pl.when(pl.program_id(0) == 0)
        def _():
            gwo_ref[...] = jnp.zeros_like(gwo_ref)
            gfw_ref[...] = jnp.zeros_like(gfw_ref)
            ghg_ref[...] = jnp.zeros_like(ghg_ref)
            loss_ref[...] = jnp.zeros_like(loss_ref)

        ohg, g = ohg_ref[...], g_ref[...]
        rs = lax.rsqrt(_head_sum(ohg * ohg, HG_D) * (1.0 / HG_D) + NORM_EPS)
        on = ohg * rs
        hgz = hgz_ref[...]
        sz = _sigmoid(hgz)
        gate_hg = hgz * sz
        lses = (l1_ref[...], l2_ref[...], l3_ref[...])
        outs = (o1_ref[...], o2_ref[...], o3_ref[...])
        mx = jnp.maximum(jnp.maximum(lses[0], lses[1]), lses[2])
        es = [jnp.exp(l - mx) for l in lses]
        den = es[0] + es[1] + es[2]
        ws = [e / den for e in es]
        oat = ws[0] * outs[0] + ws[1] * outs[1] + ws[2] * outs[2]
        atz = atz_ref[...]
        sa = _sigmoid(atz)
        gate_at = atz * sa
        mixed = jnp.concatenate([on * g * gate_hg, oat * gate_at], axis=-1).astype(MM)
        h = x_ref[...] + jnp.dot(mixed, wo_ref[...], preferred_element_type=F32)
        rstd = lax.rsqrt(jnp.mean(h * h, axis=-1, keepdims=True) + NORM_EPS)
        hn = h * rstd
        fw = fw_ref[...]
        err = hn * fw - t_ref[...]
        loss_ref[...] += 0.5 * jnp.sum(jnp.mean(err * err, axis=-1, keepdims=True), axis=0, keepdims=True)
        dout = err * (1.0 / D_MODEL)
        gfw_ref[...] += jnp.sum(dout * hn, axis=0, keepdims=True)
        dhn = dout * fw
        dh = rstd * (dhn - hn * jnp.mean(dhn * hn, axis=-1, keepdims=True))
        dh_ref[...] = dh
        dh_mm = dh.astype(MM)
        gwo_ref[...] += _dot_tn(mixed, dh_mm)
        dmixed = _dot_nt(dh_mm, wo_ref[...])
        dm_hg = dmixed[:, :SEC_W]
        d_ong = dm_hg * gate_hg
        dhgz_ref[...] = dm_hg * (on * g) * (sz * (1.0 + hgz * (1.0 - sz)))
        ghg_ref[...] += jnp.sum(d_ong * on, axis=0, keepdims=True)
        d_on = d_ong * g
        dohg_ref[...] = rs * (d_on - on * (_head_sum(d_on * on, HG_D) * (1.0 / HG_D)))
        dm_at = dmixed[:, SEC_W:]
        d_oat = dm_at * gate_at
        datz_ref[...] = dm_at * oat * (sa * (1.0 + atz * (1.0 - sa)))
        drow = _head_sum(d_oat * oat, AT_DH)
        for w, lse, do_ref, dl_ref, lp_ref in zip(ws, lses, (do1_ref, do2_ref, do3_ref),
                                                  (dl1_ref, dl2_ref, dl3_ref), (lp1_ref, lp2_ref, lp3_ref)):
            do_ref[...] = (w * d_oat).astype(MM)
            dl_ref[...] = _partner_pieces(w * drow)
            lp_ref[...] = _partner_pieces(lse)

    row = lambda w: pl.BlockSpec((tm, w), lambda i: (i, 0))
    sec = lambda j: pl.BlockSpec((None, tm, SEC_W), lambda i, j=j: (j, i, 0))
    const = lambda shp: pl.BlockSpec(shp, lambda i: (0,) * len(shp))
    half = row(SEC_W)
    return pl.pallas_call(
        body, name="mid", grid=(nb,),
        in_specs=[row(D_MODEL), row(D_MODEL), sec(3), sec(7)] + [half] * 7
                 + [const((1, SEC_W)), const((1, D_MODEL)), const((D_MODEL, D_MODEL))],
        out_specs=[row(D_MODEL)] + [half] * 12
                  + [const((D_MODEL, D_MODEL)), const((1, D_MODEL)), const((1, SEC_W)), const((1, 1))],
        out_shape=[jax.ShapeDtypeStruct((s, D_MODEL), F32)] + [jax.ShapeDtypeStruct((s, SEC_W), F32)] * 3
                  + [jax.ShapeDtypeStruct((s, SEC_W), MM)] * 9
                  + [jax.ShapeDtypeStruct((D_MODEL, D_MODEL), F32), jax.ShapeDtypeStruct((1, D_MODEL), F32),
                     jax.ShapeDtypeStruct((1, SEC_W), F32), jax.ShapeDtypeStruct((1, 1), F32)],
        compiler_params=_params("arbitrary"),
    )(x, tgt, proj, proj, o_hg, *o_at, *lse_at, hg_norm_w, final_norm_w, wo_all)


def _inproj_bwd_x(x, norm_w, w_all, dh, dsec, dq_r, dk_r, dv, cos, sin, tm=256):
    s = x.shape[0]

    def body(x_ref, nw_ref, w_ref, dh_ref, s0, s1, s2, s3, s7, q1, q2, q3, k1, k2, k3, v1, v2, v3,
             cos_ref, sin_ref, gx_ref, dp_ref, gnw_ref):
        @pl.when(pl.program_id(0) == 0)
        def _():
            gnw_ref[...] = jnp.zeros_like(gnw_ref)

        c, sn = cos_ref[...], -sin_ref[...]
        dq = q1[...] + q2[...] + q3[...]
        dk = k1[...] + k2[...] + k3[...]
        unrot = lambda a: jnp.concatenate(
            [_rope128(a[:, j * 128:(j + 1) * 128], c, sn) for j in range(SEC_W // 128)], axis=-1)
        secs = (s0[...], s1[...], s2[...], s3[...], unrot(dq), unrot(dk), v1[...] + v2[...] + v3[...], s7[...])
        du = jnp.zeros((tm, D_MODEL), F32)
        for j, dsj in enumerate(secs):
            dsj = dsj.astype(MM)
            dp_ref[j] = dsj
            du = du + _dot_nt(dsj, w_ref[j])
        xv, nw = x_ref[...], nw_ref[...]
        rstd = lax.rsqrt(jnp.mean(xv * xv, axis=-1, keepdims=True) + NORM_EPS)
        xn = xv * rstd
        gnw_ref[...] += jnp.sum(du * xn, axis=0, keepdims=True)
        dxn = du * nw
        gx_ref[...] = dh_ref[...] + rstd * (dxn - xn * jnp.mean(dxn * xn, axis=-1, keepdims=True))

    row = lambda w: pl.BlockSpec((tm, w), lambda i: (i, 0))
    const = lambda shp: pl.BlockSpec(shp, lambda i: (0,) * len(shp))
    return pl.pallas_call(
        body, name="inproj_bwd_x", grid=(s // tm,),
        in_specs=[row(D_MODEL), const((1, D_MODEL)), const((N_SEC, D_MODEL, SEC_W)), row(D_MODEL)]
                 + [row(SEC_W)] * 14 + [row(128), row(128)],
        out_specs=[row(D_MODEL), pl.BlockSpec((N_SEC, tm, SEC_W), lambda i: (0, i, 0)), const((1, D_MODEL))],
        out_shape=[jax.ShapeDtypeStruct((s, D_MODEL), F32), jax.ShapeDtypeStruct((N_SEC, s, SEC_W), MM),
                   jax.ShapeDtypeStruct((1, D_MODEL), F32)],
        compiler_params=_params("arbitrary"),
    )(x, norm_w, w_all, dh, *dsec, *dq_r, *dk_r, *dv, cos, sin)


def _inproj_bwd_w(x, norm_w, dproj, tm=256):
    s = x.shape[0]

    def body(x_ref, nw_ref, dp_ref, gw_ref):
        @pl.when(pl.program_id(1) == 0)
        def _():
            gw_ref[...] = jnp.zeros_like(gw_ref)

        xv = x_ref[...]
        rstd = lax.rsqrt(jnp.mean(xv * xv, axis=-1, keepdims=True) + NORM_EPS)
        u = (xv * rstd * nw_ref[...]).astype(MM)
        gw_ref[...] += _dot_tn(u, dp_ref[...])

    return pl.pallas_call(
        body, name="inproj_bwd_w", grid=(N_SEC, s // tm),
        in_specs=[pl.BlockSpec((tm, D_MODEL), lambda j, i: (i, 0)), pl.BlockSpec((1, D_MODEL), lambda j, i: (0, 0)),
                  pl.BlockSpec((None, tm, SEC_W), lambda j, i: (j, i, 0))],
        out_specs=pl.BlockSpec((None, D_MODEL, SEC_W), lambda j, i: (j, 0, 0)),
        out_shape=jax.ShapeDtypeStruct((N_SEC, D_MODEL, SEC_W), F32),
        compiler_params=_params("parallel", "arbitrary"),
    )(x, norm_w, dproj)


def _local_step(x, tgt, norm_w, w_all, lb_logits, hg_norm_w, wo_all, final_norm_w):
    s = x.shape[0]
    cos, sin = _rope_tables(s)
    proj = _inproj_fwd(x, norm_w, w_all)
    o_hg, sst = _hgrn_fwd(proj, lb_logits)
    qr, kr, vr = _attn_prep(proj, cos, sin)
    att = [_attn_fwd(qr, kr, vr, d) for d in DILATIONS]
    (dh, d_ohg, d_hgz, d_atz, do1, do2, do3, dl1, dl2, dl3, lp1, lp2, lp3, gwo, gfw, ghg, loss) = _mid(
        x, tgt, proj, o_hg, [a[0] for a in att], [a[1] for a in att], hg_norm_w, final_norm_w[None, :], wo_all)
    dxq, dxf, dxi, dlb = _hgrn_bwd(proj, lb_logits, d_ohg, sst)
    dq_r, dk_r, dv = [], [], []
    for d, do, lp, dl in zip(DILATIONS, (do1, do2, do3), (lp1, lp2, lp3), (dl1, dl2, dl3)):
        dq_r.append(_attn_bwd_dq(qr, kr, vr, do, lp, dl, d))
        dk_d, dv_d = _attn_bwd_dkv(qr, kr, vr, do, lp, dl, d)
        dk_r.append(dk_d)
        dv.append(dv_d)
    gx, dproj, gnw = _inproj_bwd_x(x, norm_w, w_all, dh, (dxq, dxf, dxi, d_hgz, d_atz), dq_r, dk_r, dv, cos, sin)
    gwi = _inproj_bwd_w(x, norm_w, dproj)
    small = jnp.concatenate([gnw, jnp.concatenate([dlb, ghg], axis=-1), gfw,
                             jnp.pad(loss, ((0, 0), (0, D_MODEL - 1)))], axis=0)
    return gx, gwi, gwo, small


def _coords():
    return lax.axis_index("x"), lax.axis_index("y"), lax.axis_index("c")


def _gather_weights(w_in, w_out):
    wo_rows = w_out.shape[0]

    def body(wi_ref, wo_ref, wi_all, wo_all, send_sems, recv_sems):
        x, y, c = _coords()
        me, sibling = (x, y, c), (x, y, 1 - c)
        chips = [(1 - x, y), (x, 1 - y), (1 - x, 1 - y)]
        slot = lambda p: 4 * p[0] + 2 * p[1] + p[2]

        def copies(k, block, to):
            return [pltpu.make_async_remote_copy(
                src_ref=ref.at[slot(block)], dst_ref=ref.at[slot(block)], send_sem=send_sems.at[a, k],
                recv_sem=recv_sems.at[a, k], device_id=to, device_id_type=MESH)
                for a, ref in enumerate((wi_all, wo_all))]

        wi_all[slot(me)] = wi_ref[...].astype(MM)
        wo_all[slot(me)] = wo_ref[...].astype(MM)
        first = copies(0, me, sibling)
        for j, chip in enumerate(chips):
            first += copies(1 + j, me, (*chip, c))
        for cp in first:
            cp.start()
        passed = []
        for j, chip in enumerate(chips):
            for cp in copies(1 + j, (*chip, c), me):
                cp.wait_recv()
            fwd = copies(4 + j, (*chip, c), sibling)
            for cp in fwd:
                cp.start()
            passed += fwd
        for cp in copies(0, sibling, me):
            cp.wait_recv()
        for j, chip in enumerate(chips):
            for cp in copies(4 + j, (*chip, 1 - c), me):
                cp.wait_recv()
        for cp in first + passed:
            cp.wait_send()

    vmem = pl.BlockSpec(memory_space=pltpu.VMEM)
    return pl.pallas_call(
        body, name="gather_weights",
        in_specs=[vmem, vmem], out_specs=[vmem, vmem],
        out_shape=[jax.ShapeDtypeStruct((N_DEV, D_MODEL, SEC_W), MM),
                   jax.ShapeDtypeStruct((N_DEV, wo_rows, D_MODEL), MM)],
        scratch_shapes=[pltpu.SemaphoreType.DMA((2, 7)), pltpu.SemaphoreType.DMA((2, 7))],
        compiler_params=pltpu.CompilerParams(vmem_limit_bytes=VMEM_LIMIT),
    )(w_in, w_out)


def _exchange_grads(gwi, gwo, small):
    def body(gwi_ref, gwo_ref, sm_ref, li_ref, lo_ref, ls_ref, send_sems, recv_sems, local_sems):
        x, y, c = _coords()
        me = 4 * x + 2 * y + c
        refs = ((gwi_ref, li_ref), (gwo_ref, lo_ref))
        own = [pltpu.make_async_copy(src.at[me], dst.at[me], local_sems.at[a]) for a, (src, dst) in enumerate(refs)]
        own.append(pltpu.make_async_copy(sm_ref, ls_ref.at[me], local_sems.at[2]))
        for cp in own:
            cp.start()
        sends = []
        for k in range(1, N_DEV):
            px, py, pc = x ^ (k >> 2), y ^ ((k >> 1) & 1), c ^ (k & 1)
            peer = 4 * px + 2 * py + pc
            for a, (src, dst) in enumerate(refs):
                sends.append(pltpu.make_async_remote_copy(
                    src_ref=src.at[peer], dst_ref=dst.at[me], send_sem=send_sems.at[a, k - 1],
                    recv_sem=recv_sems.at[a, k - 1], device_id=(px, py, pc), device_id_type=MESH))
            sends.append(pltpu.make_async_remote_copy(
                src_ref=sm_ref, dst_ref=ls_ref.at[me], send_sem=send_sems.at[2, k - 1],
                recv_sem=recv_sems.at[2, k - 1], device_id=(px, py, pc), device_id_type=MESH))
        for cp in sends:
            cp.start()
        for cp in sends:
            cp.wait_recv()
        for cp in sends:
            cp.wait_send()
        for cp in own:
            cp.wait()

    hbm = pl.BlockSpec(memory_space=pl.ANY)
    return pl.pallas_call(
        body, name="exchange_grads",
        in_specs=[hbm, hbm, hbm], out_specs=[hbm, hbm, hbm],
        out_shape=[jax.ShapeDtypeStruct(gwi.shape, F32), jax.ShapeDtypeStruct(gwo.shape, F32),
                   jax.ShapeDtypeStruct((N_DEV,) + small.shape, F32)],
        scratch_shapes=[pltpu.SemaphoreType.DMA((3, 7)), pltpu.SemaphoreType.DMA((3, 7)),
                        pltpu.SemaphoreType.DMA((3,))],
    )(gwi, gwo, small)


def _adamw(w, g, m, v):
    m = ADAM_B1 * m + (1.0 - ADAM_B1) * g
    v = ADAM_B2 * v + (1.0 - ADAM_B2) * (g * g)
    m_hat = m / (1.0 - ADAM_B1 ** ADAM_STEP)
    v_hat = v / (1.0 - ADAM_B2 ** ADAM_STEP)
    return -ADAM_LR * (m_hat / (jnp.sqrt(v_hat) + ADAM_EPS) + ADAM_WD * w), m, v


def _slot_sum(ref):
    g = ref[0]
    for i in range(1, N_DEV):
        g = g + ref[i]
    return g


def _update_matrix(name, landed, w, m, v, rows):
    r, c = w.shape

    def body(l_ref, w_ref, m_ref, v_ref, g_ref, d_ref, nm_ref, nv_ref):
        g = _slot_sum(l_ref)
        g_ref[...] = g
        d_ref[...], nm_ref[...], nv_ref[...] = _adamw(w_ref[...], g, m_ref[...], v_ref[...])

    blk = pl.BlockSpec((rows, c), lambda i: (i, 0))
    return pl.pallas_call(
        body, name=name, grid=(r // rows,),
        in_specs=[pl.BlockSpec((N_DEV, rows, c), lambda i: (0, i, 0)), blk, blk, blk],
        out_specs=[blk] * 4, out_shape=[jax.ShapeDtypeStruct((r, c), F32)] * 4,
        compiler_params=_params("parallel"),
    )(landed, w, m, v)


def _update_small(landed, lb_logits, ws, ms, vs):
    def body(l_ref, lbl_ref, w_ref, m_ref, v_ref, g_ref, d_ref, nm_ref, nv_ref, loss_ref):
        tot = _slot_sum(l_ref)
        _, dlb = _lower_bound(lbl_ref[...])
        g_lb = tot[1:2, :SEC_W] * dlb
        g = jnp.concatenate([tot[0:1], jnp.concatenate([g_lb, -g_lb], axis=-1),
                             jnp.pad(tot[1:2, SEC_W:], ((0, 0), (0, SEC_W))), tot[2:3]], axis=0)
        g_ref[...] = g
        d_ref[...], nm_ref[...], nv_ref[...] = _adamw(w_ref[...], g, m_ref[...], v_ref[...])
        loss_ref[...] = tot[3:4, 0:1]

    vmem = pl.BlockSpec(memory_space=pltpu.VMEM)
    return pl.pallas_call(
        body, name="update_small", in_specs=[vmem] * 5, out_specs=[vmem] * 5,
        out_shape=[jax.ShapeDtypeStruct((4, D_MODEL), F32)] * 4 + [jax.ShapeDtypeStruct((1, 1), F32)],
    )(landed, lb_logits, ws, ms, vs)


def _pack_small(norm_w, lb_logits, hg_norm_w, final_norm_w):
    return jnp.concatenate([norm_w, lb_logits.reshape(1, D_MODEL),
                            jnp.pad(hg_norm_w, ((0, 0), (0, D_MODEL - SEC_W))), final_norm_w[None, :]], axis=0)


def _unpack_small(a):
    return a[0:1], a[1].reshape(2, SEC_W), a[2:3, :SEC_W], a[3]


def kernel(x, norm_w, w_in, hgrn_lb_logits, hg_norm_w, w_out, final_norm_w, loss_target, m_norm_w, m_w_in, m_hgrn_lb_logits, m_hg_norm_w, m_w_out, m_final_norm_w, v_norm_w, v_w_in, v_hgrn_lb_logits, v_hg_norm_w, v_w_out, v_final_norm_w):
    w_all, wo_all = _gather_weights(w_in[0], w_out[0])
    gx, gwi, gwo, small = _local_step(x[0], loss_target[0], norm_w, w_all, hgrn_lb_logits, hg_norm_w,
                                      wo_all.reshape(D_MODEL, D_MODEL), final_norm_w)
    li, lo, ls = _exchange_grads(gwi, gwo.reshape(N_DEV, D_MODEL // N_DEV, D_MODEL), small)
    g_wi, d_wi, nm_wi, nv_wi = _update_matrix("update_w_in", li, w_in[0], m_w_in[0], v_w_in[0], 256)
    g_wo, d_wo, nm_wo, nv_wo = _update_matrix("update_w_out", lo, w_out[0], m_w_out[0], v_w_out[0], 128)
    g_s, d_s, nm_s, nv_s, loss = _update_small(
        ls, hgrn_lb_logits, _pack_small(norm_w, hgrn_lb_logits, hg_norm_w, final_norm_w),
        _pack_small(m_norm_w, m_hgrn_lb_logits, m_hg_norm_w, m_final_norm_w),
        _pack_small(v_norm_w, v_hgrn_lb_logits, v_hg_norm_w, v_final_norm_w))
    outs = []
    for small_out, wi, wo in ((g_s, g_wi, g_wo), (d_s, d_wi, d_wo), (nm_s, nm_wi, nm_wo), (nv_s, nv_wi, nv_wo)):
        nw, lb, hg, fw = _unpack_small(small_out)
        outs += [nw, wi[None], lb, hg, wo[None], fw]
    return (loss[0, 0], gx[None], *outs)
```

```python
import functools

import jax
import jax.numpy as jnp
from jax import lax
from jax.experimental import pallas as pl
from jax.experimental.pallas import tpu as pltpu

F32 = jnp.float32
MM = jnp.bfloat16
NORM_EPS = 1e-6
NEG = -1e30
N_DEV = 8
D_MODEL = 1024
N_SEC = 8
SEC_W = 512
HG_HEADS = 4
HG_D = 128
AT_HEADS = 8
AT_DH = 64
ATT_BLK = 128
AT_COLS = 512
DILATIONS = (1, 4, 16)
ROPE_THETA = 10000.0
CH = 16
LB_LO, LB_HI = 1e-6, 1.0 - 1e-6
ADAM_LR, ADAM_B1, ADAM_B2, ADAM_EPS, ADAM_WD, ADAM_STEP = 0.001, 0.9, 0.999, 1e-08, 0.01, 10
VMEM_LIMIT = 56 * 1024 * 1024
MESH = pl.DeviceIdType.MESH


def _params(*sem):
    return pltpu.CompilerParams(dimension_semantics=sem, vmem_limit_bytes=VMEM_LIMIT)


def _sigmoid(x):
    return 1.0 / (1.0 + jnp.exp(-x))


def _dot(a, b):
    return jnp.dot(a.astype(MM), b.astype(MM), preferred_element_type=F32)


def _dot_nt(a, b):
    return lax.dot_general(a.astype(MM), b.astype(MM), (((1,), (1,)), ((), ())), preferred_element_type=F32)


def _dot_tn(a, b):
    return lax.dot_general(a.astype(MM), b.astype(MM), (((0,), (0,)), ((), ())), preferred_element_type=F32)


def _tri_dot(tri, g):
    g1 = g.astype(jnp.bfloat16)
    r1 = g - g1.astype(F32)
    g2 = r1.astype(jnp.bfloat16)
    g3 = (r1 - g2.astype(F32)).astype(jnp.bfloat16)
    t = tri.astype(jnp.bfloat16)
    d = functools.partial(jnp.dot, preferred_element_type=F32)
    return d(t, g1) + d(t, g2) + d(t, g3)


def _lower_bound(lbl):
    l0, l1 = lbl[0:1, :], lbl[1:2, :]
    m = jnp.maximum(l0, l1)
    e0, e1 = jnp.exp(l0 - m), jnp.exp(l1 - m)
    p = e0 / (e0 + e1)
    inside = (p >= LB_LO) & (p <= LB_HI)
    return jnp.clip(p, LB_LO, LB_HI), jnp.where(inside, p * (e1 / (e0 + e1)), 0.0)


def _iota2(shape, dim):
    return lax.broadcasted_iota(jnp.int32, shape, dim)


def _inproj_fwd(x, norm_w, w_all, tm=256):
    s = x.shape[0]

    def body(x_ref, nw_ref, w_ref, proj_ref):
        xv = x_ref[...]
        rstd = lax.rsqrt(jnp.mean(xv * xv, axis=-1, keepdims=True) + NORM_EPS)
        u = (xv * rstd * nw_ref[...]).astype(MM)
        for j in range(N_SEC):
            proj_ref[j] = jnp.dot(u, w_ref[j], preferred_element_type=F32)

    return pl.pallas_call(
        body, name="inproj_fwd", grid=(s // tm,),
        in_specs=[pl.BlockSpec((tm, D_MODEL), lambda i: (i, 0)),
                  pl.BlockSpec((1, D_MODEL), lambda i: (0, 0)),
                  pl.BlockSpec((N_SEC, D_MODEL, SEC_W), lambda i: (0, 0, 0))],
        out_specs=pl.BlockSpec((N_SEC, tm, SEC_W), lambda i: (0, i, 0)),
        out_shape=jax.ShapeDtypeStruct((N_SEC, s, SEC_W), F32),
        compiler_params=_params("parallel"),
    )(x, norm_w, w_all)


def _hgrn_gates(xq, xf, lb):
    sgq = _sigmoid(xq)
    sg = _sigmoid(xf)
    sn = _sigmoid(-xf)
    f = lb + (1.0 - lb) * sg
    return sgq, xq * sgq, sg, sn, f, (1.0 - lb) * sn


def _bdot(a, b, ca, cb):
    return lax.dot_general(a.astype(MM), b.astype(MM), (((ca,), (cb,)), ((0,), (0,))), preferred_element_type=F32)


def _chunk_masks(rb):
    row, col = _iota2((rb, rb), 0), _iota2((rb, rb), 1)
    same = (row // CH) == (col // CH)
    return same & (row >= col), same & (row <= col)


def _hgrn_fwd(proj, lb_logits, rb=256):
    s = proj.shape[1]
    nb, nc = s // rb, rb // CH

    def body(q_ref, f_ref, i_ref, lbl_ref, o_ref, sst_ref, st_ref, slab_ref, states_ref):
        @pl.when(pl.program_id(1) == 0)
        def _():
            st_ref[...] = jnp.zeros_like(st_ref)

        sst_ref[...] = st_ref[...]
        lb, _ = _lower_bound(lbl_ref[...])
        prefix, _ = _chunk_masks(rb)
        c3 = lambda a: a.reshape(nc, CH, HG_D)
        _, q, _, _, f, kk = _hgrn_gates(q_ref[...], f_ref[...], lb)
        b3 = c3(_tri_dot(prefix, jnp.log(f)))
        q3, kk3, v3 = c3(q), c3(kk), c3(i_ref[...])
        bl3 = b3[:, CH - 1:CH, :]
        for t in range(CH):
            slab_ref[:, t * CH:(t + 1) * CH, :] = (q3 * jnp.exp(jnp.minimum(b3 - b3[:, t:t + 1, :], 0.0))).astype(MM)
        r = _bdot(slab_ref[...], kk3, 2, 2)
        row, col = _iota2((nc, CH, CH), 1), _iota2((nc, CH, CH), 2)
        a = jnp.zeros((nc, CH, CH), F32)
        for t in range(CH):
            a = a + jnp.where(col == t, r[:, t * CH:(t + 1) * CH, :], 0.0)
        a = jnp.where(row >= col, a, 0.0)
        x_upd = _bdot(v3, kk3 * jnp.exp(bl3 - b3), 1, 1)
        ebl3 = jnp.exp(bl3)
        st = st_ref[...]
        for c in range(nc):
            states_ref[c] = st
            st = st * ebl3[c] + x_upd[c]
        st_ref[...] = st
        o3 = _bdot(q3 * jnp.exp(b3), states_ref[...], 2, 2) + _bdot(a, v3, 2, 1)
        o_ref[...] = o3.reshape(rb, HG_D)

    sec = lambda j: pl.BlockSpec((None, rb, HG_D), lambda h, i, j=j: (j, i, h))
    return pl.pallas_call(
        body, name="hgrn_fwd", grid=(HG_HEADS, nb),
        in_specs=[sec(0), sec(1), sec(2), pl.BlockSpec((2, HG_D), lambda h, i: (0, h))],
        out_specs=[pl.BlockSpec((rb, HG_D), lambda h, i: (i, h)),
                   pl.BlockSpec((None, None, HG_D, HG_D), lambda h, i: (i, h, 0, 0))],
        out_shape=[jax.ShapeDtypeStruct((s, SEC_W), F32),
                   jax.ShapeDtypeStruct((nb, HG_HEADS, HG_D, HG_D), F32)],
        scratch_shapes=[pltpu.VMEM((HG_D, HG_D), F32), pltpu.VMEM((nc, CH * CH, HG_D), MM),
                        pltpu.VMEM((nc, HG_D, HG_D), F32)],
        compiler_params=_params("parallel", "arbitrary"),
    )(proj, proj, proj, lb_logits)


def _hgrn_bwd(proj, lb_logits, d_o, sst, rb=256):
    s = proj.shape[1]
    nb, nc = s // rb, rb // CH

    def body(q_ref, f_ref, i_ref, lbl_ref, do_ref, sst_ref, dxq_ref, dxf_ref, dxi_ref, dlb_ref,
             dst_ref, states_ref, dstates_ref, lslab_ref, kslab_ref):
        @pl.when(pl.program_id(1) == 0)
        def _():
            dst_ref[...] = jnp.zeros_like(dst_ref)
            dlb_ref[...] = jnp.zeros_like(dlb_ref)

        lb, _ = _lower_bound(lbl_ref[...])
        prefix, suffix = _chunk_masks(rb)
        c3 = lambda a: a.reshape(nc, CH, HG_D)
        flat = lambda a: a.reshape(rb, HG_D)
        xq = q_ref[...]
        sgq, q, sg, sn, f, kk = _hgrn_gates(xq, f_ref[...], lb)
        b3 = c3(_tri_dot(prefix, jnp.log(f)))
        q3, kk3, v3, do3 = c3(q), c3(kk), c3(i_ref[...]), c3(do_ref[...])
        bl3 = b3[:, CH - 1:CH, :]
        eb3, ebl3, dec3 = jnp.exp(b3), jnp.exp(bl3), jnp.exp(bl3 - b3)
        qe3, kd3 = q3 * eb3, kk3 * dec3
        x_upd, y_upd = _bdot(v3, kd3, 1, 1), _bdot(do3, qe3, 1, 1)
        st = sst_ref[...]
        for c in range(nc):
            states_ref[c] = st
            st = st * ebl3[c] + x_upd[c]
        dst = dst_ref[...]
        for c in reversed(range(nc)):
            dstates_ref[c] = dst
            dst = dst * ebl3[c] + y_upd[c]
        dst_ref[...] = dst
        states, dstates = states_ref[...], dstates_ref[...]
        dqe = _bdot(do3, states, 2, 1)
        dkd = _bdot(v3, dstates, 2, 1)
        row, col = _iota2((nc, CH, CH), 1), _iota2((nc, CH, CH), 2)
        tril, triu = row >= col, row <= col
        d_a = jnp.where(tril, _bdot(do3, v3, 2, 2), 0.0)
        d_at = jnp.where(triu, _bdot(v3, do3, 2, 2), 0.0)
        for t in range(CH):
            bt = b3[:, t:t + 1, :]
            lslab_ref[:, t * CH:(t + 1) * CH, :] = (q3 * jnp.exp(jnp.minimum(b3 - bt, 0.0))).astype(MM)
            kslab_ref[:, t * CH:(t + 1) * CH, :] = (kk3 * jnp.exp(jnp.minimum(bt - b3, 0.0))).astype(MM)
        r = _bdot(kslab_ref[...], q3, 2, 2)
        a_t = jnp.zeros((nc, CH, CH), F32)
        for t in range(CH):
            a_t = a_t + jnp.where(col == t, r[:, t * CH:(t + 1) * CH, :], 0.0)
        a_t = jnp.where(triu, a_t, 0.0)
        dv = _bdot(kd3, dstates, 2, 2) + _bdot(a_t, do3, 2, 1)
        sel = (_iota2((CH, CH * CH), 1) % CH == _iota2((CH, CH * CH), 0)).astype(MM)
        blockdiag = _iota2((nc, CH, CH * CH), 2) // CH == _iota2((nc, CH, CH * CH), 1)
        tile = lambda m: jnp.where(blockdiag, _dot(m.reshape(rb, CH), sel).reshape(nc, CH, CH * CH), 0.0)
        dq_in = _bdot(tile(d_a), kslab_ref[...], 2, 1)
        dk_in = _bdot(tile(d_at), lslab_ref[...], 2, 1)
        dkd_kd = dkd * kd3
        db = dqe * qe3 - dkd_kd + q3 * dq_in - kk3 * dk_in
        dbl = jnp.sum(dkd_kd, axis=1, keepdims=True) + jnp.sum(dstates * states, axis=1, keepdims=True) * ebl3
        last = _iota2((nc, CH, HG_D), 1) == CH - 1
        dg = _tri_dot(suffix, flat(db + jnp.where(last, dbl, 0.0)))
        df = dg / f - flat(dkd * dec3 + dk_in)
        dxq_ref[...] = flat(dqe * eb3 + dq_in) * (sgq * (1.0 + xq * (1.0 - sgq)))
        dxf_ref[...] = df * (1.0 - lb) * sg * sn
        dxi_ref[...] = flat(dv)
        dlb_ref[...] += jnp.sum(df * sn, axis=0, keepdims=True)

    rev = lambda i: nb - 1 - i
    sec = lambda j: pl.BlockSpec((None, rb, HG_D), lambda h, i, j=j: (j, rev(i), h))
    blk = pl.BlockSpec((rb, HG_D), lambda h, i: (rev(i), h))
    return pl.pallas_call(
        body, name="hgrn_bwd", grid=(HG_HEADS, nb),
        in_specs=[sec(0), sec(1), sec(2), pl.BlockSpec((2, HG_D), lambda h, i: (0, h)), blk,
                  pl.BlockSpec((None, None, HG_D, HG_D), lambda h, i: (rev(i), h, 0, 0))],
        out_specs=[blk, blk, blk, pl.BlockSpec((1, HG_D), lambda h, i: (0, h))],
        out_shape=[jax.ShapeDtypeStruct((s, SEC_W), F32)] * 3 + [jax.ShapeDtypeStruct((1, SEC_W), F32)],
        scratch_shapes=[pltpu.VMEM((HG_D, HG_D), F32), pltpu.VMEM((nc, HG_D, HG_D), F32),
                        pltpu.VMEM((nc, HG_D, HG_D), F32),
                        pltpu.VMEM((nc, CH * CH, HG_D), MM), pltpu.VMEM((nc, CH * CH, HG_D), MM)],
        compiler_params=_params("parallel", "arbitrary"),
    )(proj, proj, proj, lb_logits, d_o, sst)


def _rope_tables(s):
    half = AT_DH // 2
    inv_freq = 1.0 / (ROPE_THETA ** (jnp.arange(half, dtype=F32) / half))
    ang = jnp.arange(s, dtype=jnp.int32).astype(F32)[:, None] * inv_freq[None, :]
    cos, sin = jnp.cos(ang), jnp.sin(ang)
    return jnp.concatenate([cos] * 4, axis=-1), jnp.concatenate([-sin, sin] * 2, axis=-1)


def _rope128(x, cos, sin):
    lo = (_iota2(x.shape, 1) % AT_DH) < AT_DH // 2
    rot = jnp.where(lo, pltpu.roll(x, 128 - AT_DH // 2, 1), pltpu.roll(x, AT_DH // 2, 1))
    return x * cos + rot * sin


LANE_GROUPS = SEC_W // 128


def _set_lanes(ref, val):
    for j in range(LANE_GROUPS):
        ref[j] = val[:, j * 128:(j + 1) * 128]


def _get_lanes(ref):
    return jnp.concatenate([ref[j] for j in range(LANE_GROUPS)], axis=-1)


def _to_view(src_ref, dst_ref, d):
    n = src_ref.shape[1] // d
    for r in range(d):
        rows = pl.ds(r, n, stride=d) if d > 1 else slice(None)
        for j in range(LANE_GROUPS):
            c0 = r * SEC_W + j * 128
            dst_ref[:, c0:c0 + 128] = src_ref.at[j][rows, :].astype(dst_ref.dtype)


def _from_view(src_ref, dst_ref, d):
    n = dst_ref.shape[1] // d
    for r in range(d):
        for j in range(LANE_GROUPS):
            c0 = r * SEC_W + j * 128
            dst_ref.at[j][pl.ds(r, n, stride=d), :] = src_ref[:, c0:c0 + 128]


def _view_spec(tm, d):
    return pl.BlockSpec((tm // d, d * SEC_W), lambda i: (i, 0))


def _view_shape(s, d, dtype):
    return jax.ShapeDtypeStruct((s // d, d * SEC_W), dtype)


def _attn_prep(proj, cos, sin, tm=512):
    s = proj.shape[1]

    def body(q_ref, k_ref, v_ref, cos_ref, sin_ref, *refs):
        outs, (qs_ref, ks_ref, vs_ref) = refs[:-3], refs[-3:]
        c, sn = cos_ref[...], sin_ref[...]
        for j in range(LANE_GROUPS):
            sl = slice(j * 128, (j + 1) * 128)
            qs_ref[j] = _rope128(q_ref[:, sl], c, sn) * (AT_DH ** -0.5)
            ks_ref[j] = _rope128(k_ref[:, sl], c, sn)
            vs_ref[j] = v_ref[:, sl]
        for i, d in enumerate(DILATIONS):
            for src_ref, dst_ref in zip((qs_ref, ks_ref, vs_ref), outs[3 * i:3 * i + 3]):
                _to_view(src_ref, dst_ref, d)

    sec = lambda j: pl.BlockSpec((None, tm, SEC_W), lambda i, j=j: (j, i, 0))
    tab = pl.BlockSpec((tm, 128), lambda i: (i, 0))
    return pl.pallas_call(
        body, name="attn_prep", grid=(s // tm,),
        in_specs=[sec(4), sec(5), sec(6), tab, tab],
        out_specs=[_view_spec(tm, d) for d in DILATIONS for _ in range(3)],
        out_shape=[_view_shape(s, d, MM) for d in DILATIONS for _ in range(3)],
        scratch_shapes=[pltpu.VMEM((LANE_GROUPS, tm, 128), F32)] * 3,
        compiler_params=_params("parallel"),
    )(proj, proj, proj, cos, sin)


def _band_mask(neighbour_ok, keys_major=False):
    row, col = _iota2((ATT_BLK, 2 * ATT_BLK), 0), _iota2((ATT_BLK, 2 * ATT_BLK), 1)
    near, own = col < ATT_BLK, col >= ATT_BLK
    if keys_major:
        return (near & (col <= row) & neighbour_ok) | (own & ((col - ATT_BLK) >= row))
    return (near & (col >= row) & neighbour_ok) | (own & ((col - ATT_BLK) <= row))


def _own_lanes(rows, h):
    lane = _iota2((rows, 128), 1)
    return (lane < AT_DH) if h == 0 else (lane >= AT_DH)


def _neg_pieces(rows, h):
    lane = _iota2((rows, 128), 1) - (AT_DH if h == 0 else 0)
    return jnp.where((lane >= 0) & (lane < 3), -1.0, 0.0).astype(MM)


def _attn_fwd(qr, kr, vr, d):
    rows, cols = qr.shape
    nb = rows // ATT_BLK

    def body(q_ref, kc_ref, kp_ref, vc_ref, vp_ref, o_ref, lse_ref):
        valid = _band_mask(pl.program_id(1) > 0)
        ones = jnp.ones((2 * ATT_BLK, 128), MM)
        head0 = _own_lanes(ATT_BLK, 0)
        groups = [slice(g * 128, (g + 1) * 128) for g in range(AT_COLS // 128)]
        heads = [(sl, h) for sl in groups for h in range(2)]
        vbs = {sl.start: jnp.concatenate([vp_ref[:, sl], vc_ref[:, sl]], axis=0) for sl in groups}
        scs = []
        for sl, h in heads:
            q2 = q_ref[:, sl]
            kb = jnp.concatenate([kp_ref[:, sl], kc_ref[:, sl]], axis=0)
            qh = jnp.where(_own_lanes(ATT_BLK, h), q2, jnp.zeros_like(q2))
            scs.append(jnp.where(valid, _dot_nt(qh, kb), NEG))
        ms = [jnp.max(sc, axis=-1, keepdims=True) for sc in scs]
        ps = [jnp.exp(sc - m).astype(MM) for sc, m in zip(scs, ms)]
        ls = [jnp.dot(p, ones, preferred_element_type=F32) for p in ps]
        os_ = [jnp.dot(p, vbs[sl.start], preferred_element_type=F32) / l for p, l, (sl, _) in zip(ps, ls, heads)]
        lses = [m + jnp.log(l) for m, l in zip(ms, ls)]
        for g, sl in enumerate(groups):
            o_ref[:, sl] = jnp.where(head0, os_[2 * g], os_[2 * g + 1])
            lse_ref[:, sl] = jnp.where(head0, lses[2 * g], lses[2 * g + 1])

    cur = pl.BlockSpec((ATT_BLK, AT_COLS), lambda c, n: (n, c))
    prev = pl.BlockSpec((ATT_BLK, AT_COLS), lambda c, n: (jnp.maximum(n - 1, 0), c))
    o, lse = pl.pallas_call(
        body, name=f"attn_fwd_d{d}", grid=(cols // AT_COLS, nb),
        in_specs=[cur, cur, prev, cur, prev], out_specs=[cur, cur],
        out_shape=[jax.ShapeDtypeStruct((rows, cols), F32)] * 2,
        compiler_params=_params("parallel", "parallel"),
    )(qr, kr, kr, vr, vr)
    return o, lse


def _attn_bwd_dq(qr, kr, vr, do, lse, delta, d):
    rows, cols = qr.shape
    nb = rows // ATT_BLK

    def body(q_ref, kc_ref, kp_ref, vc_ref, vp_ref, do_ref, lse_ref, dl_ref, dq_ref):
        valid = _band_mask(pl.program_id(1) > 0)
        groups = [slice(g * 128, (g + 1) * 128) for g in range(AT_COLS // 128)]
        heads = [(sl, h) for sl in groups for h in range(2)]
        kbs = {sl.start: jnp.concatenate([kp_ref[:, sl], kc_ref[:, sl]], axis=0) for sl in groups}
        vbs = {sl.start: jnp.concatenate([vp_ref[:, sl], vc_ref[:, sl]], axis=0) for sl in groups}
        sms, dps = [], []
        for sl, h in heads:
            own, own_b, neg = _own_lanes(ATT_BLK, h), _own_lanes(2 * ATT_BLK, h), _neg_pieces(2 * ATT_BLK, h)
            sms.append(_dot_nt(jnp.where(own, q_ref[:, sl], lse_ref[:, sl]), jnp.where(own_b, kbs[sl.start], neg)))
            dps.append(_dot_nt(jnp.where(own, do_ref[:, sl], dl_ref[:, sl]), jnp.where(own_b, vbs[sl.start], neg)))
        dss = [(jnp.exp(jnp.where(valid, sm, NEG)) * dp).astype(MM) for sm, dp in zip(sms, dps)]
        dqs = [jnp.dot(ds, kbs[sl.start], preferred_element_type=F32) * (AT_DH ** -0.5)
               for ds, (sl, _) in zip(dss, heads)]
        for g, sl in enumerate(groups):
            dq_ref[:, sl] = jnp.where(_own_lanes(ATT_BLK, 0), dqs[2 * g], dqs[2 * g + 1])

    cur = pl.BlockSpec((ATT_BLK, AT_COLS), lambda c, n: (n, c))
    prev = pl.BlockSpec((ATT_BLK, AT_COLS), lambda c, n: (jnp.maximum(n - 1, 0), c))
    dq = pl.pallas_call(
        body, name=f"attn_bwd_dq_d{d}", grid=(cols // AT_COLS, nb),
        in_specs=[cur, cur, prev, cur, prev, cur, cur, cur], out_specs=cur,
        out_shape=jax.ShapeDtypeStruct((rows, cols), F32),
        compiler_params=_params("parallel", "parallel"),
    )(qr, kr, kr, vr, vr, do, lse, delta)
    return dq


def _attn_bwd_dkv(qr, kr, vr, do, lse, delta, d):
    rows, cols = qr.shape
    nb = rows // ATT_BLK

    def body(k_ref, v_ref, qc_ref, qn_ref, doc_ref, don_ref, lsec_ref, lsen_ref, dlc_ref, dln_ref,
             dk_ref, dv_ref):
        valid = _band_mask(pl.program_id(1) < nb - 1, keys_major=True)
        band = lambda nxt_ref, cur_ref, sl: jnp.concatenate([nxt_ref[:, sl], cur_ref[:, sl]], axis=0)
        groups = [slice(g * 128, (g + 1) * 128) for g in range(AT_COLS // 128)]
        heads = [(sl, h) for sl in groups for h in range(2)]
        qbs = {sl.start: band(qn_ref, qc_ref, sl) for sl in groups}
        dobs = {sl.start: band(don_ref, doc_ref, sl) for sl in groups}
        sms, dps = [], []
        for sl, h in heads:
            own, own_b, neg = _own_lanes(ATT_BLK, h), _own_lanes(2 * ATT_BLK, h), _neg_pieces(ATT_BLK, h)
            sms.append(_dot_nt(jnp.where(own, k_ref[:, sl], neg),
                               jnp.where(own_b, qbs[sl.start], band(lsen_ref, lsec_ref, sl))))
            dps.append(_dot_nt(jnp.where(own, v_ref[:, sl], neg),
                               jnp.where(own_b, dobs[sl.start], band(dln_ref, dlc_ref, sl))))
        ps = [jnp.exp(jnp.where(valid, sm, NEG)) for sm in sms]
        dss = [(p * dp).astype(MM) for p, dp in zip(ps, dps)]
        dvs = [jnp.dot(p.astype(MM), dobs[sl.start], preferred_element_type=F32) for p, (sl, _) in zip(ps, heads)]
        dks = [jnp.dot(ds, qbs[sl.start], preferred_element_type=F32) for ds, (sl, _) in zip(dss, heads)]
        head0 = _own_lanes(ATT_BLK, 0)
        for g, sl in enumerate(groups):
            dk_ref[:, sl] = jnp.where(head0, dks[2 * g], dks[2 * g + 1])
            dv_ref[:, sl] = jnp.where(head0, dvs[2 * g], dvs[2 * g + 1])

    cur = pl.BlockSpec((ATT_BLK, AT_COLS), lambda c, n: (n, c))
    nxt = pl.BlockSpec((ATT_BLK, AT_COLS), lambda c, n: (jnp.minimum(n + 1, nb - 1), c))
    dk, dv = pl.pallas_call(
        body, name=f"attn_bwd_dkv_d{d}", grid=(cols // AT_COLS, nb),
        in_specs=[cur, cur, cur, nxt, cur, nxt, cur, nxt, cur, nxt], out_specs=[cur, cur],
        out_shape=[jax.ShapeDtypeStruct((rows, cols), F32)] * 2,
        compiler_params=_params("parallel", "parallel"),
    )(kr, vr, qr, qr, do, do, lse, lse, delta, delta)
    return dk, dv


def _head_sum(a, width):
    parts = []
    for j in range(a.shape[1] // width):
        sm = jnp.sum(a[:, j * width:(j + 1) * width], axis=-1, keepdims=True)
        parts.append(jnp.broadcast_to(sm, (a.shape[0], width)))
    return jnp.concatenate(parts, axis=-1)


def _partner_pieces(x):
    xs = jnp.concatenate([pltpu.roll(x[:, j * 128:(j + 1) * 128], AT_DH, 1) for j in range(x.shape[1] // 128)],
                         axis=-1)
    hi = xs.astype(jnp.bfloat16).astype(F32)
    mid = (xs - hi).astype(jnp.bfloat16).astype(F32)
    lo = (xs - hi - mid).astype(jnp.bfloat16).astype(F32)
    lane = _iota2(x.shape, 1) % AT_DH
    return jnp.where(lane == 0, hi, jnp.where(lane == 1, mid, jnp.where(lane == 2, lo, 0.0)))


def _mid(x, tgt, proj, o_hg, o_at, lse_at, hg_norm_w, final_norm_w, wo_all, tm=256):
    s = x.shape[0]
    nb = s // tm

    def body(x_ref, t_ref, hgz_ref, atz_ref, ohg_ref, o1_ref, o2_ref, o3_ref, l1_ref, l2_ref, l3_ref,
             g_ref, fw_ref, wo_ref,
             dh_ref, dohg_ref, dhgz_ref, datz_ref, do1_ref, do2_ref, do3_ref, dl1_ref, dl2_ref, dl3_ref,
             lp1_ref, lp2_ref, lp3_ref,
             gwo_ref, gfw_ref, ghg_ref, loss_ref, nat_ref, stage_ref):
        @pl.when(pl.program_id(0) == 0)
        def _():
            gwo_ref[...] = jnp.zeros_like(gwo_ref)
            gfw_ref[...] = jnp.zeros_like(gfw_ref)
            ghg_ref[...] = jnp.zeros_like(ghg_ref)
            loss_ref[...] = jnp.zeros_like(loss_ref)

        ohg, g = ohg_ref[...], g_ref[...]
        rs = lax.rsqrt(_head_sum(ohg * ohg, HG_D) * (1.0 / HG_D) + NORM_EPS)
        on = ohg * rs
        hgz = hgz_ref[...]
        sz = _sigmoid(hgz)
        gate_hg = hgz * sz
        lses, outs = [l1_ref[...]], [o1_ref[...]]
        for k, (d, l_ref, o_ref) in enumerate(zip(DILATIONS[1:], (l2_ref, l3_ref), (o2_ref, o3_ref))):
            _from_view(l_ref, nat_ref.at[2 * k], d)
            _from_view(o_ref, nat_ref.at[2 * k + 1], d)
            lses.append(_get_lanes(nat_ref.at[2 * k]))
            outs.append(_get_lanes(nat_ref.at[2 * k + 1]))
        mx = jnp.maximum(jnp.maximum(lses[0], lses[1]), lses[2])
        es = [jnp.exp(l - mx) for l in lses]
        den = es[0] + es[1] + es[2]
        ws = [e / den for e in es]
        oat = ws[0] * outs[0] + ws[1] * outs[1] + ws[2] * outs[2]
        atz = atz_ref[...]
        sa = _sigmoid(atz)
        gate_at = atz * sa
        mixed = jnp.concatenate([on * g * gate_hg, oat * gate_at], axis=-1).astype(MM)
        h = x_ref[...] + jnp.dot(mixed, wo_ref[...], preferred_element_type=F32)
        rstd = lax.rsqrt(jnp.mean(h * h, axis=-1, keepdims=True) + NORM_EPS)
        hn = h * rstd
        fw = fw_ref[...]
        err = hn * fw - t_ref[...]
        loss_ref[...] += 0.5 * jnp.sum(jnp.mean(err * err, axis=-1, keepdims=True), axis=0, keepdims=True)
        dout = err * (1.0 / D_MODEL)
        gfw_ref[...] += jnp.sum(dout * hn, axis=0, keepdims=True)
        dhn = dout * fw
        dh = rstd * (dhn - hn * jnp.mean(dhn * hn, axis=-1, keepdims=True))
        dh_ref[...] = dh
        dh_mm = dh.astype(MM)
        gwo_ref[...] += _dot_tn(mixed, dh_mm)
        dmixed = _dot_nt(dh_mm, wo_ref[...])
        dm_hg = dmixed[:, :SEC_W]
        d_ong = dm_hg * gate_hg
        dhgz_ref[...] = dm_hg * (on * g) * (sz * (1.0 + hgz * (1.0 - sz)))
        ghg_ref[...] += jnp.sum(d_ong * on, axis=0, keepdims=True)
        d_on = d_ong * g
        dohg_ref[...] = rs * (d_on - on * (_head_sum(d_on * on, HG_D) * (1.0 / HG_D)))
        dm_at = dmixed[:, SEC_W:]
        d_oat = dm_at * gate_at
        datz_ref[...] = dm_at * oat * (sa * (1.0 + atz * (1.0 - sa)))
        drow = _head_sum(d_oat * oat, AT_DH)
        for d, w, lse, do_ref, dl_ref, lp_ref in zip(DILATIONS, ws, lses, (do1_ref, do2_ref, do3_ref),
                                                     (dl1_ref, dl2_ref, dl3_ref), (lp1_ref, lp2_ref, lp3_ref)):
            for val, dst_ref in ((w * d_oat, do_ref), (_partner_pieces(w * drow), dl_ref),
                                 (_partner_pieces(lse), lp_ref)):
                _set_lanes(stage_ref, val)
                _to_view(stage_ref, dst_ref, d)

    row = lambda w: pl.BlockSpec((tm, w), lambda i: (i, 0))
    sec = lambda j: pl.BlockSpec((None, tm, SEC_W), lambda i, j=j: (j, i, 0))
    const = lambda shp: pl.BlockSpec(shp, lambda i: (0,) * len(shp))
    half = row(SEC_W)
    views = [_view_spec(tm, d) for d in DILATIONS]
    return pl.pallas_call(
        body, name="mid", grid=(nb,),
        in_specs=[row(D_MODEL), row(D_MODEL), sec(3), sec(7), half] + views * 2
                 + [const((1, SEC_W)), const((1, D_MODEL)), const((D_MODEL, D_MODEL))],
        out_specs=[row(D_MODEL)] + [half] * 3 + views * 3
                  + [const((D_MODEL, D_MODEL)), const((1, D_MODEL)), const((1, SEC_W)), const((1, 1))],
        out_shape=[jax.ShapeDtypeStruct((s, D_MODEL), F32)] + [jax.ShapeDtypeStruct((s, SEC_W), F32)] * 3
                  + [_view_shape(s, d, MM) for d in DILATIONS] * 3
                  + [jax.ShapeDtypeStruct((D_MODEL, D_MODEL), F32), jax.ShapeDtypeStruct((1, D_MODEL), F32),
                     jax.ShapeDtypeStruct((1, SEC_W), F32), jax.ShapeDtypeStruct((1, 1), F32)],
        scratch_shapes=[pltpu.VMEM((4, LANE_GROUPS, tm, 128), F32), pltpu.VMEM((LANE_GROUPS, tm, 128), F32)],
        compiler_params=_params("arbitrary"),
    )(x, tgt, proj, proj, o_hg, *o_at, *lse_at, hg_norm_w, final_norm_w, wo_all)


def _inproj_bwd_x(x, norm_w, w_all, dh, dsec, dq_r, dk_r, dv, cos, sin, tm=256):
    s = x.shape[0]

    def body(x_ref, nw_ref, w_ref, dh_ref, s0, s1, s2, s3, s7, q1, q2, q3, k1, k2, k3, v1, v2, v3,
             cos_ref, sin_ref, gx_ref, dp_ref, gnw_ref, nat_ref):
        @pl.when(pl.program_id(0) == 0)
        def _():
            gnw_ref[...] = jnp.zeros_like(gnw_ref)

        def total(refs):
            acc = refs[0][...]
            for d, ref in zip(DILATIONS[1:], refs[1:]):
                _from_view(ref, nat_ref, d)
                acc = acc + _get_lanes(nat_ref)
            return acc

        c, sn = cos_ref[...], -sin_ref[...]
        dq, dk = total((q1, q2, q3)), total((k1, k2, k3))
        unrot = lambda a: jnp.concatenate(
            [_rope128(a[:, j * 128:(j + 1) * 128], c, sn) for j in range(SEC_W // 128)], axis=-1)
        secs = (s0[...], s1[...], s2[...], s3[...], unrot(dq), unrot(dk), total((v1, v2, v3)), s7[...])
        du = jnp.zeros((tm, D_MODEL), F32)
        for j, dsj in enumerate(secs):
            dsj = dsj.astype(MM)
            dp_ref[j] = dsj
            du = du + _dot_nt(dsj, w_ref[j])
        xv, nw = x_ref[...], nw_ref[...]
        rstd = lax.rsqrt(jnp.mean(xv * xv, axis=-1, keepdims=True) + NORM_EPS)
        xn = xv * rstd
        gnw_ref[...] += jnp.sum(du * xn, axis=0, keepdims=True)
        dxn = du * nw
        gx_ref[...] = dh_ref[...] + rstd * (dxn - xn * jnp.mean(dxn * xn, axis=-1, keepdims=True))

    row = lambda w: pl.BlockSpec((tm, w), lambda i: (i, 0))
    const = lambda shp: pl.BlockSpec(shp, lambda i: (0,) * len(shp))
    return pl.pallas_call(
        body, name="inproj_bwd_x", grid=(s // tm,),
        in_specs=[row(D_MODEL), const((1, D_MODEL)), const((N_SEC, D_MODEL, SEC_W)), row(D_MODEL)]
                 + [row(SEC_W)] * 5 + [_view_spec(tm, d) for d in DILATIONS] * 3 + [row(128), row(128)],
        out_specs=[row(D_MODEL), pl.BlockSpec((N_SEC, tm, SEC_W), lambda i: (0, i, 0)), const((1, D_MODEL))],
        out_shape=[jax.ShapeDtypeStruct((s, D_MODEL), F32), jax.ShapeDtypeStruct((N_SEC, s, SEC_W), MM),
                   jax.ShapeDtypeStruct((1, D_MODEL), F32)],
        scratch_shapes=[pltpu.VMEM((LANE_GROUPS, tm, 128), F32)],
        compiler_params=_params("arbitrary"),
    )(x, norm_w, w_all, dh, *dsec, *dq_r, *dk_r, *dv, cos, sin)


def _inproj_bwd_w(x, norm_w, dproj, tm=256):
    s = x.shape[0]

    def body(x_ref, nw_ref, dp_ref, gw_ref):
        @pl.when(pl.program_id(1) == 0)
        def _():
            gw_ref[...] = jnp.zeros_like(gw_ref)

        xv = x_ref[...]
        rstd = lax.rsqrt(jnp.mean(xv * xv, axis=-1, keepdims=True) + NORM_EPS)
        u = (xv * rstd * nw_ref[...]).astype(MM)
        gw_ref[...] += _dot_tn(u, dp_ref[...])

    return pl.pallas_call(
        body, name="inproj_bwd_w", grid=(N_SEC, s // tm),
        in_specs=[pl.BlockSpec((tm, D_MODEL), lambda j, i: (i, 0)), pl.BlockSpec((1, D_MODEL), lambda j, i: (0, 0)),
                  pl.BlockSpec((None, tm, SEC_W), lambda j, i: (j, i, 0))],
        out_specs=pl.BlockSpec((None, D_MODEL, SEC_W), lambda j, i: (j, 0, 0)),
        out_shape=jax.ShapeDtypeStruct((N_SEC, D_MODEL, SEC_W), F32),
        compiler_params=_params("parallel", "arbitrary"),
    )(x, norm_w, dproj)


def _local_step(x, tgt, norm_w, w_all, lb_logits, hg_norm_w, wo_all, final_norm_w):
    s = x.shape[0]
    cos, sin = _rope_tables(s)
    proj = _inproj_fwd(x, norm_w, w_all)
    o_hg, sst = _hgrn_fwd(proj, lb_logits)
    qkv = _attn_prep(proj, cos, sin)
    qkv = [qkv[3 * i:3 * i + 3] for i in range(len(DILATIONS))]
    att = [_attn_fwd(*qkv_d, d) for qkv_d, d in zip(qkv, DILATIONS)]
    (dh, d_ohg, d_hgz, d_atz, do1, do2, do3, dl1, dl2, dl3, lp1, lp2, lp3, gwo, gfw, ghg, loss) = _mid(
        x, tgt, proj, o_hg, [a[0] for a in att], [a[1] for a in att], hg_norm_w, final_norm_w[None, :], wo_all)
    dxq, dxf, dxi, dlb = _hgrn_bwd(proj, lb_logits, d_ohg, sst)
    dq_r, dk_r, dv = [], [], []
    for d, qkv_d, do, lp, dl in zip(DILATIONS, qkv, (do1, do2, do3), (lp1, lp2, lp3), (dl1, dl2, dl3)):
        dq_r.append(_attn_bwd_dq(*qkv_d, do, lp, dl, d))
        dk_d, dv_d = _attn_bwd_dkv(*qkv_d, do, lp, dl, d)
        dk_r.append(dk_d)
        dv.append(dv_d)
    gx, dproj, gnw = _inproj_bwd_x(x, norm_w, w_all, dh, (dxq, dxf, dxi, d_hgz, d_atz), dq_r, dk_r, dv, cos, sin)
    gwi = _inproj_bwd_w(x, norm_w, dproj)
    small = jnp.concatenate([gnw, jnp.concatenate([dlb, ghg], axis=-1), gfw,
                             jnp.pad(loss, ((0, 0), (0, D_MODEL - 1)))], axis=0)
    return gx, gwi, gwo, small


def _coords():
    return lax.axis_index("x"), lax.axis_index("y"), lax.axis_index("c")


def _gather_weights(w_in, w_out):
    wo_rows = w_out.shape[0]

    def body(wi_ref, wo_ref, wi_all, wo_all, send_sems, recv_sems):
        x, y, c = _coords()
        me, sibling = (x, y, c), (x, y, 1 - c)
        chips = [(1 - x, y), (x, 1 - y), (1 - x, 1 - y)]
        slot = lambda p: 4 * p[0] + 2 * p[1] + p[2]

        def copies(k, block, to):
            return [pltpu.make_async_remote_copy(
                src_ref=ref.at[slot(block)], dst_ref=ref.at[slot(block)], send_sem=send_sems.at[a, k],
                recv_sem=recv_sems.at[a, k], device_id=to, device_id_type=MESH)
                for a, ref in enumerate((wi_all, wo_all))]

        wi_all[slot(me)] = wi_ref[...].astype(MM)
        wo_all[slot(me)] = wo_ref[...].astype(MM)
        first = copies(0, me, sibling)
        for j, chip in enumerate(chips):
            first += copies(1 + j, me, (*chip, c))
        for cp in first:
            cp.start()
        passed = []
        for j, chip in enumerate(chips):
            for cp in copies(1 + j, (*chip, c), me):
                cp.wait_recv()
            fwd = copies(4 + j, (*chip, c), sibling)
            for cp in fwd:
                cp.start()
            passed += fwd
        for cp in copies(0, sibling, me):
            cp.wait_recv()
        for j, chip in enumerate(chips):
            for cp in copies(4 + j, (*chip, 1 - c), me):
                cp.wait_recv()
        for cp in first + passed:
            cp.wait_send()

    vmem = pl.BlockSpec(memory_space=pltpu.VMEM)
    return pl.pallas_call(
        body, name="gather_weights",
        in_specs=[vmem, vmem], out_specs=[vmem, vmem],
        out_shape=[jax.ShapeDtypeStruct((N_DEV, D_MODEL, SEC_W), MM),
                   jax.ShapeDtypeStruct((N_DEV, wo_rows, D_MODEL), MM)],
        scratch_shapes=[pltpu.SemaphoreType.DMA((2, 7)), pltpu.SemaphoreType.DMA((2, 7))],
        compiler_params=pltpu.CompilerParams(vmem_limit_bytes=VMEM_LIMIT),
    )(w_in, w_out)


def _exchange_grads(gwi, gwo, small):
    def body(gwi_ref, gwo_ref, sm_ref, li_ref, lo_ref, ls_ref, send_sems, recv_sems, local_sems):
        x, y, c = _coords()
        me = 4 * x + 2 * y + c
        refs = ((gwi_ref, li_ref), (gwo_ref, lo_ref))
        own = [pltpu.make_async_copy(src.at[me], dst.at[me], local_sems.at[a]) for a, (src, dst) in enumerate(refs)]
        own.append(pltpu.make_async_copy(sm_ref, ls_ref.at[me], local_sems.at[2]))
        for cp in own:
            cp.start()
        sends = []
        for k in range(1, N_DEV):
            px, py, pc = x ^ (k >> 2), y ^ ((k >> 1) & 1), c ^ (k & 1)
            peer = 4 * px + 2 * py + pc
            for a, (src, dst) in enumerate(refs):
                sends.append(pltpu.make_async_remote_copy(
                    src_ref=src.at[peer], dst_ref=dst.at[me], send_sem=send_sems.at[a, k - 1],
                    recv_sem=recv_sems.at[a, k - 1], device_id=(px, py, pc), device_id_type=MESH))
            sends.append(pltpu.make_async_remote_copy(
                src_ref=sm_ref, dst_ref=ls_ref.at[me], send_sem=send_sems.at[2, k - 1],
                recv_sem=recv_sems.at[2, k - 1], device_id=(px, py, pc), device_id_type=MESH))
        for cp in sends:
            cp.start()
        for cp in sends:
            cp.wait_recv()
        for cp in sends:
            cp.wait_send()
        for cp in own:
            cp.wait()

    hbm = pl.BlockSpec(memory_space=pl.ANY)
    return pl.pallas_call(
        body, name="exchange_grads",
        in_specs=[hbm, hbm, hbm], out_specs=[hbm, hbm, hbm],
        out_shape=[jax.ShapeDtypeStruct(gwi.shape, F32), jax.ShapeDtypeStruct(gwo.shape, F32),
                   jax.ShapeDtypeStruct((N_DEV,) + small.shape, F32)],
        scratch_shapes=[pltpu.SemaphoreType.DMA((3, 7)), pltpu.SemaphoreType.DMA((3, 7)),
                        pltpu.SemaphoreType.DMA((3,))],
    )(gwi, gwo, small)


def _adamw(w, g, m, v):
    m = ADAM_B1 * m + (1.0 - ADAM_B1) * g
    v = ADAM_B2 * v + (1.0 - ADAM_B2) * (g * g)
    m_hat = m / (1.0 - ADAM_B1 ** ADAM_STEP)
    v_hat = v / (1.0 - ADAM_B2 ** ADAM_STEP)
    return -ADAM_LR * (m_hat / (jnp.sqrt(v_hat) + ADAM_EPS) + ADAM_WD * w), m, v


def _slot_sum(ref):
    g = ref[0]
    for i in range(1, N_DEV):
        g = g + ref[i]
    return g


def _update_matrix(name, landed, w, m, v, rows):
    r, c = w.shape

    def body(l_ref, w_ref, m_ref, v_ref, g_ref, d_ref, nm_ref, nv_ref):
        g = _slot_sum(l_ref)
        g_ref[...] = g
        d_ref[...], nm_ref[...], nv_ref[...] = _adamw(w_ref[...], g, m_ref[...], v_ref[...])

    blk = pl.BlockSpec((rows, c), lambda i: (i, 0))
    return pl.pallas_call(
        body, name=name, grid=(r // rows,),
        in_specs=[pl.BlockSpec((N_DEV, rows, c), lambda i: (0, i, 0)), blk, blk, blk],
        out_specs=[blk] * 4, out_shape=[jax.ShapeDtypeStruct((r, c), F32)] * 4,
        compiler_params=_params("parallel"),
    )(landed, w, m, v)


def _update_small(landed, lb_logits, ws, ms, vs):
    def body(l_ref, lbl_ref, w_ref, m_ref, v_ref, g_ref, d_ref, nm_ref, nv_ref, loss_ref):
        tot = _slot_sum(l_ref)
        _, dlb = _lower_bound(lbl_ref[...])
        g_lb = tot[1:2, :SEC_W] * dlb
        g = jnp.concatenate([tot[0:1], jnp.concatenate([g_lb, -g_lb], axis=-1),
                             jnp.pad(tot[1:2, SEC_W:], ((0, 0), (0, SEC_W))), tot[2:3]], axis=0)
        g_ref[...] = g
        d_ref[...], nm_ref[...], nv_ref[...] = _adamw(w_ref[...], g, m_ref[...], v_ref[...])
        loss_ref[...] = tot[3:4, 0:1]

    vmem = pl.BlockSpec(memory_space=pltpu.VMEM)
    return pl.pallas_call(
        body, name="update_small", in_specs=[vmem] * 5, out_specs=[vmem] * 5,
        out_shape=[jax.ShapeDtypeStruct((4, D_MODEL), F32)] * 4 + [jax.ShapeDtypeStruct((1, 1), F32)],
    )(landed, lb_logits, ws, ms, vs)


def _pack_small(norm_w, lb_logits, hg_norm_w, final_norm_w):
    return jnp.concatenate([norm_w, lb_logits.reshape(1, D_MODEL),
                            jnp.pad(hg_norm_w, ((0, 0), (0, D_MODEL - SEC_W))), final_norm_w[None, :]], axis=0)


def _unpack_small(a):
    return a[0:1], a[1].reshape(2, SEC_W), a[2:3, :SEC_W], a[3]


def kernel(x, norm_w, w_in, hgrn_lb_logits, hg_norm_w, w_out, final_norm_w, loss_target, m_norm_w, m_w_in, m_hgrn_lb_logits, m_hg_norm_w, m_w_out, m_final_norm_w, v_norm_w, v_w_in, v_hgrn_lb_logits, v_hg_norm_w, v_w_out, v_final_norm_w):
    w_all, wo_all = _gather_weights(w_in[0], w_out[0])
    gx, gwi, gwo, small = _local_step(x[0], loss_target[0], norm_w, w_all, hgrn_lb_logits, hg_norm_w,
                                      wo_all.reshape(D_MODEL, D_MODEL), final_norm_w)
    li, lo, ls = _exchange_grads(gwi, gwo.reshape(N_DEV, D_MODEL // N_DEV, D_MODEL), small)
    g_wi, d_wi, nm_wi, nv_wi = _update_matrix("update_w_in", li, w_in[0], m_w_in[0], v_w_in[0], 256)
    g_wo, d_wo, nm_wo, nv_wo = _update_matrix("update_w_out", lo, w_out[0], m_w_out[0], v_w_out[0], 128)
    g_s, d_s, nm_s, nv_s, loss = _update_small(
        ls, hgrn_lb_logits, _pack_small(norm_w, hgrn_lb_logits, hg_norm_w, final_norm_w),
        _pack_small(m_norm_w, m_hgrn_lb_logits, m_hg_norm_w, m_final_norm_w),
        _pack_small(v_norm_w, v_hgrn_lb_logits, v_hg_norm_w, v_final_norm_w))
    outs = []
    for small_out, wi, wo in ((g_s, g_wi, g_wo), (d_s, d_wi, d_wo), (nm_s, nm_wi, nm_wo), (nv_s, nv_wi, nv_wo)):
        nw, lb, hg, fw = _unpack_small(small_out)
        outs += [nw, wi[None], lb, hg, wo[None], fw]
    return (loss[0, 0], gx[None], *outs)
```

```python
import functools

import jax
import jax.numpy as jnp
from jax import lax
from jax.experimental import pallas as pl
from jax.experimental.pallas import tpu as pltpu

F32 = jnp.float32
MM = jnp.bfloat16
XCH = jnp.bfloat16
NORM_EPS = 1e-6
NEG = -1e30
N_DEV = 8
D_MODEL = 1024
N_SEC = 8
SEC_W = 512
HG_HEADS = 4
HG_D = 128
AT_HEADS = 8
AT_DH = 64
ATT_BLK = 128
AT_COLS = 512
DILATIONS = (1, 4, 16)
ROPE_THETA = 10000.0
CH = 16
LB_LO, LB_HI = 1e-6, 1.0 - 1e-6
ADAM_LR, ADAM_B1, ADAM_B2, ADAM_EPS, ADAM_WD, ADAM_STEP = 0.001, 0.9, 0.999, 1e-08, 0.01, 10
VMEM_LIMIT = 56 * 1024 * 1024
MESH = pl.DeviceIdType.MESH


def _params(*sem):
    return pltpu.CompilerParams(dimension_semantics=sem, vmem_limit_bytes=VMEM_LIMIT)


def _sigmoid(x):
    return 1.0 / (1.0 + jnp.exp(-x))


def _dot(a, b):
    return jnp.dot(a.astype(MM), b.astype(MM), preferred_element_type=F32)


def _dot_nt(a, b):
    return lax.dot_general(a.astype(MM), b.astype(MM), (((1,), (1,)), ((), ())), preferred_element_type=F32)


def _dot_tn(a, b):
    return lax.dot_general(a.astype(MM), b.astype(MM), (((0,), (0,)), ((), ())), preferred_element_type=F32)


def _tri_dot(tri, g):
    g1 = g.astype(jnp.bfloat16)
    r1 = g - g1.astype(F32)
    g2 = r1.astype(jnp.bfloat16)
    g3 = (r1 - g2.astype(F32)).astype(jnp.bfloat16)
    t = tri.astype(jnp.bfloat16)
    d = functools.partial(jnp.dot, preferred_element_type=F32)
    return d(t, g1) + d(t, g2) + d(t, g3)


def _lower_bound(lbl):
    l0, l1 = lbl[0:1, :], lbl[1:2, :]
    m = jnp.maximum(l0, l1)
    e0, e1 = jnp.exp(l0 - m), jnp.exp(l1 - m)
    p = e0 / (e0 + e1)
    inside = (p >= LB_LO) & (p <= LB_HI)
    return jnp.clip(p, LB_LO, LB_HI), jnp.where(inside, p * (e1 / (e0 + e1)), 0.0)


def _iota2(shape, dim):
    return lax.broadcasted_iota(jnp.int32, shape, dim)


def _inproj_fwd(x, norm_w, w_all, tm=256):
    s = x.shape[0]

    def body(x_ref, nw_ref, w_ref, proj_ref):
        xv = x_ref[...]
        rstd = lax.rsqrt(jnp.mean(xv * xv, axis=-1, keepdims=True) + NORM_EPS)
        u = (xv * rstd * nw_ref[...]).astype(MM)
        for j in range(N_SEC):
            proj_ref[j] = jnp.dot(u, w_ref[j], preferred_element_type=F32)

    return pl.pallas_call(
        body, name="inproj_fwd", grid=(s // tm,),
        in_specs=[pl.BlockSpec((tm, D_MODEL), lambda i: (i, 0)),
                  pl.BlockSpec((1, D_MODEL), lambda i: (0, 0)),
                  pl.BlockSpec((N_SEC, D_MODEL, SEC_W), lambda i: (0, 0, 0))],
        out_specs=pl.BlockSpec((N_SEC, tm, SEC_W), lambda i: (0, i, 0)),
        out_shape=jax.ShapeDtypeStruct((N_SEC, s, SEC_W), F32),
        compiler_params=_params("parallel"),
    )(x, norm_w, w_all)


def _hgrn_gates(xq, xf, lb):
    sgq = _sigmoid(xq)
    sg = _sigmoid(xf)
    sn = _sigmoid(-xf)
    f = lb + (1.0 - lb) * sg
    return sgq, xq * sgq, sg, sn, f, (1.0 - lb) * sn


def _bdot(a, b, ca, cb):
    return lax.dot_general(a.astype(MM), b.astype(MM), (((ca,), (cb,)), ((0,), (0,))), preferred_element_type=F32)


def _chunk_masks(rb):
    row, col = _iota2((rb, rb), 0), _iota2((rb, rb), 1)
    same = (row // CH) == (col // CH)
    return same & (row >= col), same & (row <= col)


def _hgrn_fwd(proj, lb_logits, rb=256):
    s = proj.shape[1]
    nb, nc = s // rb, rb // CH

    def body(q_ref, f_ref, i_ref, lbl_ref, o_ref, sst_ref, st_ref, slab_ref, states_ref):
        @pl.when(pl.program_id(1) == 0)
        def _():
            st_ref[...] = jnp.zeros_like(st_ref)

        sst_ref[...] = st_ref[...]
        lb, _ = _lower_bound(lbl_ref[...])
        prefix, _ = _chunk_masks(rb)
        c3 = lambda a: a.reshape(nc, CH, HG_D)
        _, q, _, _, f, kk = _hgrn_gates(q_ref[...], f_ref[...], lb)
        b3 = c3(_tri_dot(prefix, jnp.log(f)))
        q3, kk3, v3 = c3(q), c3(kk), c3(i_ref[...])
        bl3 = b3[:, CH - 1:CH, :]
        for t in range(CH):
            slab_ref[:, t * CH:(t + 1) * CH, :] = (q3 * jnp.exp(jnp.minimum(b3 - b3[:, t:t + 1, :], 0.0))).astype(MM)
        r = _bdot(slab_ref[...], kk3, 2, 2)
        row, col = _iota2((nc, CH, CH), 1), _iota2((nc, CH, CH), 2)
        a = jnp.zeros((nc, CH, CH), F32)
        for t in range(CH):
            a = a + jnp.where(col == t, r[:, t * CH:(t + 1) * CH, :], 0.0)
        a = jnp.where(row >= col, a, 0.0)
        x_upd = _bdot(v3, kk3 * jnp.exp(bl3 - b3), 1, 1)
        ebl3 = jnp.exp(bl3)
        st = st_ref[...]
        for c in range(nc):
            states_ref[c] = st
            st = st * ebl3[c] + x_upd[c]
        st_ref[...] = st
        o3 = _bdot(q3 * jnp.exp(b3), states_ref[...], 2, 2) + _bdot(a, v3, 2, 1)
        o_ref[...] = o3.reshape(rb, HG_D)

    sec = lambda j: pl.BlockSpec((None, rb, HG_D), lambda h, i, j=j: (j, i, h))
    return pl.pallas_call(
        body, name="hgrn_fwd", grid=(HG_HEADS, nb),
        in_specs=[sec(0), sec(1), sec(2), pl.BlockSpec((2, HG_D), lambda h, i: (0, h))],
        out_specs=[pl.BlockSpec((rb, HG_D), lambda h, i: (i, h)),
                   pl.BlockSpec((None, None, HG_D, HG_D), lambda h, i: (i, h, 0, 0))],
        out_shape=[jax.ShapeDtypeStruct((s, SEC_W), F32),
                   jax.ShapeDtypeStruct((nb, HG_HEADS, HG_D, HG_D), F32)],
        scratch_shapes=[pltpu.VMEM((HG_D, HG_D), F32), pltpu.VMEM((nc, CH * CH, HG_D), MM),
                        pltpu.VMEM((nc, HG_D, HG_D), F32)],
        compiler_params=_params("parallel", "arbitrary"),
    )(proj, proj, proj, lb_logits)


def _hgrn_bwd(proj, lb_logits, d_o, sst, rb=256):
    s = proj.shape[1]
    nb, nc = s // rb, rb // CH

    def body(q_ref, f_ref, i_ref, lbl_ref, do_ref, sst_ref, dxq_ref, dxf_ref, dxi_ref, dlb_ref,
             dst_ref, states_ref, dstates_ref, lslab_ref, kslab_ref):
        @pl.when(pl.program_id(1) == 0)
        def _():
            dst_ref[...] = jnp.zeros_like(dst_ref)
            dlb_ref[...] = jnp.zeros_like(dlb_ref)

        lb, _ = _lower_bound(lbl_ref[...])
        prefix, suffix = _chunk_masks(rb)
        c3 = lambda a: a.reshape(nc, CH, HG_D)
        flat = lambda a: a.reshape(rb, HG_D)
        xq = q_ref[...]
        sgq, q, sg, sn, f, kk = _hgrn_gates(xq, f_ref[...], lb)
        b3 = c3(_tri_dot(prefix, jnp.log(f)))
        q3, kk3, v3, do3 = c3(q), c3(kk), c3(i_ref[...]), c3(do_ref[...])
        bl3 = b3[:, CH - 1:CH, :]
        eb3, ebl3, dec3 = jnp.exp(b3), jnp.exp(bl3), jnp.exp(bl3 - b3)
        qe3, kd3 = q3 * eb3, kk3 * dec3
        x_upd, y_upd = _bdot(v3, kd3, 1, 1), _bdot(do3, qe3, 1, 1)
        st = sst_ref[...]
        for c in range(nc):
            states_ref[c] = st
            st = st * ebl3[c] + x_upd[c]
        dst = dst_ref[...]
        for c in reversed(range(nc)):
            dstates_ref[c] = dst
            dst = dst * ebl3[c] + y_upd[c]
        dst_ref[...] = dst
        states, dstates = states_ref[...], dstates_ref[...]
        dqe = _bdot(do3, states, 2, 1)
        dkd = _bdot(v3, dstates, 2, 1)
        row, col = _iota2((nc, CH, CH), 1), _iota2((nc, CH, CH), 2)
        tril, triu = row >= col, row <= col
        d_a = jnp.where(tril, _bdot(do3, v3, 2, 2), 0.0)
        d_at = jnp.where(triu, _bdot(v3, do3, 2, 2), 0.0)
        for t in range(CH):
            bt = b3[:, t:t + 1, :]
            lslab_ref[:, t * CH:(t + 1) * CH, :] = (q3 * jnp.exp(jnp.minimum(b3 - bt, 0.0))).astype(MM)
            kslab_ref[:, t * CH:(t + 1) * CH, :] = (kk3 * jnp.exp(jnp.minimum(bt - b3, 0.0))).astype(MM)
        r = _bdot(kslab_ref[...], q3, 2, 2)
        a_t = jnp.zeros((nc, CH, CH), F32)
        for t in range(CH):
            a_t = a_t + jnp.where(col == t, r[:, t * CH:(t + 1) * CH, :], 0.0)
        a_t = jnp.where(triu, a_t, 0.0)
        dv = _bdot(kd3, dstates, 2, 2) + _bdot(a_t, do3, 2, 1)
        sel = (_iota2((CH, CH * CH), 1) % CH == _iota2((CH, CH * CH), 0)).astype(MM)
        blockdiag = _iota2((nc, CH, CH * CH), 2) // CH == _iota2((nc, CH, CH * CH), 1)
        tile = lambda m: jnp.where(blockdiag, _dot(m.reshape(rb, CH), sel).reshape(nc, CH, CH * CH), 0.0)
        dq_in = _bdot(tile(d_a), kslab_ref[...], 2, 1)
        dk_in = _bdot(tile(d_at), lslab_ref[...], 2, 1)
        dkd_kd = dkd * kd3
        db = dqe * qe3 - dkd_kd + q3 * dq_in - kk3 * dk_in
        dbl = jnp.sum(dkd_kd, axis=1, keepdims=True) + jnp.sum(dstates * states, axis=1, keepdims=True) * ebl3
        last = _iota2((nc, CH, HG_D), 1) == CH - 1
        dg = _tri_dot(suffix, flat(db + jnp.where(last, dbl, 0.0)))
        df = dg / f - flat(dkd * dec3 + dk_in)
        dxq_ref[...] = flat(dqe * eb3 + dq_in) * (sgq * (1.0 + xq * (1.0 - sgq)))
        dxf_ref[...] = df * (1.0 - lb) * sg * sn
        dxi_ref[...] = flat(dv)
        dlb_ref[...] += jnp.sum(df * sn, axis=0, keepdims=True)

    rev = lambda i: nb - 1 - i
    sec = lambda j: pl.BlockSpec((None, rb, HG_D), lambda h, i, j=j: (j, rev(i), h))
    blk = pl.BlockSpec((rb, HG_D), lambda h, i: (rev(i), h))
    return pl.pallas_call(
        body, name="hgrn_bwd", grid=(HG_HEADS, nb),
        in_specs=[sec(0), sec(1), sec(2), pl.BlockSpec((2, HG_D), lambda h, i: (0, h)), blk,
                  pl.BlockSpec((None, None, HG_D, HG_D), lambda h, i: (rev(i), h, 0, 0))],
        out_specs=[blk, blk, blk, pl.BlockSpec((1, HG_D), lambda h, i: (0, h))],
        out_shape=[jax.ShapeDtypeStruct((s, SEC_W), F32)] * 3 + [jax.ShapeDtypeStruct((1, SEC_W), F32)],
        scratch_shapes=[pltpu.VMEM((HG_D, HG_D), F32), pltpu.VMEM((nc, HG_D, HG_D), F32),
                        pltpu.VMEM((nc, HG_D, HG_D), F32),
                        pltpu.VMEM((nc, CH * CH, HG_D), MM), pltpu.VMEM((nc, CH * CH, HG_D), MM)],
        compiler_params=_params("parallel", "arbitrary"),
    )(proj, proj, proj, lb_logits, d_o, sst)


def _rope_tables(s):
    half = AT_DH // 2
    inv_freq = 1.0 / (ROPE_THETA ** (jnp.arange(half, dtype=F32) / half))
    ang = jnp.arange(s, dtype=jnp.int32).astype(F32)[:, None] * inv_freq[None, :]
    cos, sin = jnp.cos(ang), jnp.sin(ang)
    return jnp.concatenate([cos] * 4, axis=-1), jnp.concatenate([-sin, sin] * 2, axis=-1)


def _rope128(x, cos, sin):
    lo = (_iota2(x.shape, 1) % AT_DH) < AT_DH // 2
    rot = jnp.where(lo, pltpu.roll(x, 128 - AT_DH // 2, 1), pltpu.roll(x, AT_DH // 2, 1))
    return x * cos + rot * sin


LANE_GROUPS = SEC_W // 128


def _set_lanes(ref, val):
    for j in range(LANE_GROUPS):
        ref[j] = val[:, j * 128:(j + 1) * 128]


def _get_lanes(ref):
    return jnp.concatenate([ref[j] for j in range(LANE_GROUPS)], axis=-1)


def _to_view(src_ref, dst_ref, d):
    n = src_ref.shape[1] // d
    for r in range(d):
        rows = pl.ds(r, n, stride=d) if d > 1 else slice(None)
        for j in range(LANE_GROUPS):
            c0 = r * SEC_W + j * 128
            dst_ref[:, c0:c0 + 128] = src_ref.at[j][rows, :].astype(dst_ref.dtype)


def _from_view(src_ref, dst_ref, d):
    n = dst_ref.shape[1] // d
    for r in range(d):
        for j in range(LANE_GROUPS):
            c0 = r * SEC_W + j * 128
            dst_ref.at[j][pl.ds(r, n, stride=d), :] = src_ref[:, c0:c0 + 128]


def _view_spec(tm, d):
    return pl.BlockSpec((tm // d, d * SEC_W), lambda i: (i, 0))


def _view_shape(s, d, dtype):
    return jax.ShapeDtypeStruct((s // d, d * SEC_W), dtype)


def _attn_prep(proj, cos, sin, tm=512):
    s = proj.shape[1]

    def body(q_ref, k_ref, v_ref, cos_ref, sin_ref, *refs):
        outs, (qs_ref, ks_ref, vs_ref) = refs[:-3], refs[-3:]
        c, sn = cos_ref[...], sin_ref[...]
        for j in range(LANE_GROUPS):
            sl = slice(j * 128, (j + 1) * 128)
            qs_ref[j] = _rope128(q_ref[:, sl], c, sn) * (AT_DH ** -0.5)
            ks_ref[j] = _rope128(k_ref[:, sl], c, sn)
            vs_ref[j] = v_ref[:, sl]
        for i, d in enumerate(DILATIONS):
            for src_ref, dst_ref in zip((qs_ref, ks_ref, vs_ref), outs[3 * i:3 * i + 3]):
                _to_view(src_ref, dst_ref, d)

    sec = lambda j: pl.BlockSpec((None, tm, SEC_W), lambda i, j=j: (j, i, 0))
    tab = pl.BlockSpec((tm, 128), lambda i: (i, 0))
    return pl.pallas_call(
        body, name="attn_prep", grid=(s // tm,),
        in_specs=[sec(4), sec(5), sec(6), tab, tab],
        out_specs=[_view_spec(tm, d) for d in DILATIONS for _ in range(3)],
        out_shape=[_view_shape(s, d, MM) for d in DILATIONS for _ in range(3)],
        scratch_shapes=[pltpu.VMEM((LANE_GROUPS, tm, 128), F32)] * 3,
        compiler_params=_params("parallel"),
    )(proj, proj, proj, cos, sin)


def _band_mask(neighbour_ok, keys_major=False):
    row, col = _iota2((ATT_BLK, 2 * ATT_BLK), 0), _iota2((ATT_BLK, 2 * ATT_BLK), 1)
    near, own = col < ATT_BLK, col >= ATT_BLK
    if keys_major:
        return (near & (col <= row) & neighbour_ok) | (own & ((col - ATT_BLK) >= row))
    return (near & (col >= row) & neighbour_ok) | (own & ((col - ATT_BLK) <= row))


def _own_lanes(rows, h):
    lane = _iota2((rows, 128), 1)
    return (lane < AT_DH) if h == 0 else (lane >= AT_DH)


def _neg_pieces(rows, h):
    lane = _iota2((rows, 128), 1) - (AT_DH if h == 0 else 0)
    return jnp.where((lane >= 0) & (lane < 3), -1.0, 0.0).astype(MM)


def _attn_fwd(qr, kr, vr, d):
    rows, cols = qr.shape
    nb = rows // ATT_BLK

    def body(q_ref, kc_ref, kp_ref, vc_ref, vp_ref, o_ref, lse_ref):
        valid = _band_mask(pl.program_id(1) > 0)
        ones = jnp.ones((2 * ATT_BLK, 128), MM)
        head0 = _own_lanes(ATT_BLK, 0)
        groups = [slice(g * 128, (g + 1) * 128) for g in range(AT_COLS // 128)]
        heads = [(sl, h) for sl in groups for h in range(2)]
        vbs = {sl.start: jnp.concatenate([vp_ref[:, sl], vc_ref[:, sl]], axis=0) for sl in groups}
        scs = []
        for sl, h in heads:
            q2 = q_ref[:, sl]
            kb = jnp.concatenate([kp_ref[:, sl], kc_ref[:, sl]], axis=0)
            qh = jnp.where(_own_lanes(ATT_BLK, h), q2, jnp.zeros_like(q2))
            scs.append(jnp.where(valid, _dot_nt(qh, kb), NEG))
        ms = [jnp.max(sc, axis=-1, keepdims=True) for sc in scs]
        ps = [jnp.exp(sc - m).astype(MM) for sc, m in zip(scs, ms)]
        ls = [jnp.dot(p, ones, preferred_element_type=F32) for p in ps]
        os_ = [jnp.dot(p, vbs[sl.start], preferred_element_type=F32) for p, (sl, _) in zip(ps, heads)]
        for g, sl in enumerate(groups):
            l = jnp.where(head0, ls[2 * g], ls[2 * g + 1])
            o_ref[:, sl] = jnp.where(head0, os_[2 * g], os_[2 * g + 1]) / l
            lse_ref[:, sl] = jnp.where(head0, ms[2 * g], ms[2 * g + 1]) + jnp.log(l)

    cur = pl.BlockSpec((ATT_BLK, AT_COLS), lambda c, n: (n, c))
    prev = pl.BlockSpec((ATT_BLK, AT_COLS), lambda c, n: (jnp.maximum(n - 1, 0), c))
    o, lse = pl.pallas_call(
        body, name=f"attn_fwd_d{d}", grid=(cols // AT_COLS, nb),
        in_specs=[cur, cur, prev, cur, prev], out_specs=[cur, cur],
        out_shape=[jax.ShapeDtypeStruct((rows, cols), F32)] * 2,
        compiler_params=_params("parallel", "parallel"),
    )(qr, kr, kr, vr, vr)
    return o, lse


def _attn_bwd_dq(qr, kr, vr, do, lse, delta, d):
    rows, cols = qr.shape
    nb = rows // ATT_BLK

    def body(q_ref, kc_ref, kp_ref, vc_ref, vp_ref, do_ref, lse_ref, dl_ref, dq_ref):
        valid = _band_mask(pl.program_id(1) > 0)
        groups = [slice(g * 128, (g + 1) * 128) for g in range(AT_COLS // 128)]
        heads = [(sl, h) for sl in groups for h in range(2)]
        kbs = {sl.start: jnp.concatenate([kp_ref[:, sl], kc_ref[:, sl]], axis=0) for sl in groups}
        vbs = {sl.start: jnp.concatenate([vp_ref[:, sl], vc_ref[:, sl]], axis=0) for sl in groups}
        sms, dps = [], []
        for sl, h in heads:
            own, own_b, neg = _own_lanes(ATT_BLK, h), _own_lanes(2 * ATT_BLK, h), _neg_pieces(2 * ATT_BLK, h)
            sms.append(_dot_nt(jnp.where(own, q_ref[:, sl], lse_ref[:, sl]), jnp.where(own_b, kbs[sl.start], neg)))
            dps.append(_dot_nt(jnp.where(own, do_ref[:, sl], dl_ref[:, sl]), jnp.where(own_b, vbs[sl.start], neg)))
        dss = [(jnp.exp(jnp.where(valid, sm, NEG)) * dp).astype(MM) for sm, dp in zip(sms, dps)]
        dqs = [jnp.dot(ds, kbs[sl.start], preferred_element_type=F32) * (AT_DH ** -0.5)
               for ds, (sl, _) in zip(dss, heads)]
        for g, sl in enumerate(groups):
            dq_ref[:, sl] = jnp.where(_own_lanes(ATT_BLK, 0), dqs[2 * g], dqs[2 * g + 1])

    cur = pl.BlockSpec((ATT_BLK, AT_COLS), lambda c, n: (n, c))
    prev = pl.BlockSpec((ATT_BLK, AT_COLS), lambda c, n: (jnp.maximum(n - 1, 0), c))
    dq = pl.pallas_call(
        body, name=f"attn_bwd_dq_d{d}", grid=(cols // AT_COLS, nb),
        in_specs=[cur, cur, prev, cur, prev, cur, cur, cur], out_specs=cur,
        out_shape=jax.ShapeDtypeStruct((rows, cols), F32),
        compiler_params=_params("parallel", "parallel"),
    )(qr, kr, kr, vr, vr, do, lse, delta)
    return dq


def _attn_bwd_dkv(qr, kr, vr, do, lse, delta, d):
    rows, cols = qr.shape
    nb = rows // ATT_BLK

    def body(k_ref, v_ref, qc_ref, qn_ref, doc_ref, don_ref, lsec_ref, lsen_ref, dlc_ref, dln_ref,
             dk_ref, dv_ref):
        valid = _band_mask(pl.program_id(1) < nb - 1, keys_major=True)
        band = lambda nxt_ref, cur_ref, sl: jnp.concatenate([nxt_ref[:, sl], cur_ref[:, sl]], axis=0)
        groups = [slice(g * 128, (g + 1) * 128) for g in range(AT_COLS // 128)]
        heads = [(sl, h) for sl in groups for h in range(2)]
        qbs = {sl.start: band(qn_ref, qc_ref, sl) for sl in groups}
        dobs = {sl.start: band(don_ref, doc_ref, sl) for sl in groups}
        sms, dps = [], []
        for sl, h in heads:
            own, own_b, neg = _own_lanes(ATT_BLK, h), _own_lanes(2 * ATT_BLK, h), _neg_pieces(ATT_BLK, h)
            sms.append(_dot_nt(jnp.where(own, k_ref[:, sl], neg),
                               jnp.where(own_b, qbs[sl.start], band(lsen_ref, lsec_ref, sl))))
            dps.append(_dot_nt(jnp.where(own, v_ref[:, sl], neg),
                               jnp.where(own_b, dobs[sl.start], band(dln_ref, dlc_ref, sl))))
        ps = [jnp.exp(jnp.where(valid, sm, NEG)) for sm in sms]
        dss = [(p * dp).astype(MM) for p, dp in zip(ps, dps)]
        dvs = [jnp.dot(p.astype(MM), dobs[sl.start], preferred_element_type=F32) for p, (sl, _) in zip(ps, heads)]
        dks = [jnp.dot(ds, qbs[sl.start], preferred_element_type=F32) for ds, (sl, _) in zip(dss, heads)]
        head0 = _own_lanes(ATT_BLK, 0)
        for g, sl in enumerate(groups):
            dk_ref[:, sl] = jnp.where(head0, dks[2 * g], dks[2 * g + 1])
            dv_ref[:, sl] = jnp.where(head0, dvs[2 * g], dvs[2 * g + 1])

    cur = pl.BlockSpec((ATT_BLK, AT_COLS), lambda c, n: (n, c))
    nxt = pl.BlockSpec((ATT_BLK, AT_COLS), lambda c, n: (jnp.minimum(n + 1, nb - 1), c))
    dk, dv = pl.pallas_call(
        body, name=f"attn_bwd_dkv_d{d}", grid=(cols // AT_COLS, nb),
        in_specs=[cur, cur, cur, nxt, cur, nxt, cur, nxt, cur, nxt], out_specs=[cur, cur],
        out_shape=[jax.ShapeDtypeStruct((rows, cols), F32)] * 2,
        compiler_params=_params("parallel", "parallel"),
    )(kr, vr, qr, qr, do, do, lse, lse, delta, delta)
    return dk, dv


def _head_sum(a, width):
    parts = []
    for j in range(a.shape[1] // width):
        sm = jnp.sum(a[:, j * width:(j + 1) * width], axis=-1, keepdims=True)
        parts.append(jnp.broadcast_to(sm, (a.shape[0], width)))
    return jnp.concatenate(parts, axis=-1)


def _partner_pieces(x):
    xs = jnp.concatenate([pltpu.roll(x[:, j * 128:(j + 1) * 128], AT_DH, 1) for j in range(x.shape[1] // 128)],
                         axis=-1)
    hi = xs.astype(jnp.bfloat16).astype(F32)
    mid = (xs - hi).astype(jnp.bfloat16).astype(F32)
    lo = (xs - hi - mid).astype(jnp.bfloat16).astype(F32)
    lane = _iota2(x.shape, 1) % AT_DH
    return jnp.where(lane == 0, hi, jnp.where(lane == 1, mid, jnp.where(lane == 2, lo, 0.0)))


def _mid(x, tgt, proj, o_hg, o_at, lse_at, hg_norm_w, final_norm_w, wo_all, tm=256):
    s = x.shape[0]
    nb = s // tm

    def body(x_ref, t_ref, hgz_ref, atz_ref, ohg_ref, o1_ref, o2_ref, o3_ref, l1_ref, l2_ref, l3_ref,
             g_ref, fw_ref, wo_ref,
             dh_ref, dohg_ref, dhgz_ref, datz_ref, do1_ref, do2_ref, do3_ref, dl1_ref, dl2_ref, dl3_ref,
             lp1_ref, lp2_ref, lp3_ref,
             gwo_ref, gfw_ref, ghg_ref, loss_ref, nat_ref, stage_ref, gwo_acc):
        @pl.when(pl.program_id(0) == 0)
        def _():
            gwo_acc[...] = jnp.zeros_like(gwo_acc)
            gfw_ref[...] = jnp.zeros_like(gfw_ref)
            ghg_ref[...] = jnp.zeros_like(ghg_ref)
            loss_ref[...] = jnp.zeros_like(loss_ref)

        ohg, g = ohg_ref[...], g_ref[...]
        rs = lax.rsqrt(_head_sum(ohg * ohg, HG_D) * (1.0 / HG_D) + NORM_EPS)
        on = ohg * rs
        hgz = hgz_ref[...]
        sz = _sigmoid(hgz)
        gate_hg = hgz * sz
        lses, outs = [l1_ref[...]], [o1_ref[...]]
        for k, (d, l_ref, o_ref) in enumerate(zip(DILATIONS[1:], (l2_ref, l3_ref), (o2_ref, o3_ref))):
            _from_view(l_ref, nat_ref.at[2 * k], d)
            _from_view(o_ref, nat_ref.at[2 * k + 1], d)
            lses.append(_get_lanes(nat_ref.at[2 * k]))
            outs.append(_get_lanes(nat_ref.at[2 * k + 1]))
        mx = jnp.maximum(jnp.maximum(lses[0], lses[1]), lses[2])
        es = [jnp.exp(l - mx) for l in lses]
        den = es[0] + es[1] + es[2]
        ws = [e / den for e in es]
        oat = ws[0] * outs[0] + ws[1] * outs[1] + ws[2] * outs[2]
        atz = atz_ref[...]
        sa = _sigmoid(atz)
        gate_at = atz * sa
        mixed = jnp.concatenate([on * g * gate_hg, oat * gate_at], axis=-1).astype(MM)
        h = x_ref[...] + jnp.dot(mixed, wo_ref[...], preferred_element_type=F32)
        rstd = lax.rsqrt(jnp.mean(h * h, axis=-1, keepdims=True) + NORM_EPS)
        hn = h * rstd
        fw = fw_ref[...]
        err = hn * fw - t_ref[...]
        loss_ref[...] += 0.5 * jnp.sum(jnp.mean(err * err, axis=-1, keepdims=True), axis=0, keepdims=True)
        dout = err * (1.0 / D_MODEL)
        gfw_ref[...] += jnp.sum(dout * hn, axis=0, keepdims=True)
        dhn = dout * fw
        dh = rstd * (dhn - hn * jnp.mean(dhn * hn, axis=-1, keepdims=True))
        dh_ref[...] = dh
        dh_mm = dh.astype(MM)
        gwo_acc[...] += _dot_tn(mixed, dh_mm)

        @pl.when(pl.program_id(0) == nb - 1)
        def _():
            gwo_ref[...] = gwo_acc[...].astype(gwo_ref.dtype)

        dmixed = _dot_nt(dh_mm, wo_ref[...])
        dm_hg = dmixed[:, :SEC_W]
        d_ong = dm_hg * gate_hg
        dhgz_ref[...] = dm_hg * (on * g) * (sz * (1.0 + hgz * (1.0 - sz)))
        ghg_ref[...] += jnp.sum(d_ong * on, axis=0, keepdims=True)
        d_on = d_ong * g
        dohg_ref[...] = rs * (d_on - on * (_head_sum(d_on * on, HG_D) * (1.0 / HG_D)))
        dm_at = dmixed[:, SEC_W:]
        d_oat = dm_at * gate_at
        datz_ref[...] = dm_at * oat * (sa * (1.0 + atz * (1.0 - sa)))
        drow = _head_sum(d_oat * oat, AT_DH)
        for d, w, lse, do_ref, dl_ref, lp_ref in zip(DILATIONS, ws, lses, (do1_ref, do2_ref, do3_ref),
                                                     (dl1_ref, dl2_ref, dl3_ref), (lp1_ref, lp2_ref, lp3_ref)):
            for val, dst_ref in ((w * d_oat, do_ref), (_partner_pieces(w * drow), dl_ref),
                                 (_partner_pieces(lse), lp_ref)):
                _set_lanes(stage_ref, val)
                _to_view(stage_ref, dst_ref, d)

    row = lambda w: pl.BlockSpec((tm, w), lambda i: (i, 0))
    sec = lambda j: pl.BlockSpec((None, tm, SEC_W), lambda i, j=j: (j, i, 0))
    const = lambda shp: pl.BlockSpec(shp, lambda i: (0,) * len(shp))
    half = row(SEC_W)
    views = [_view_spec(tm, d) for d in DILATIONS]
    return pl.pallas_call(
        body, name="mid", grid=(nb,),
        in_specs=[row(D_MODEL), row(D_MODEL), sec(3), sec(7), half] + views * 2
                 + [const((1, SEC_W)), const((1, D_MODEL)), const((D_MODEL, D_MODEL))],
        out_specs=[row(D_MODEL)] + [half] * 3 + views * 3
                  + [const((D_MODEL, D_MODEL)), const((1, D_MODEL)), const((1, SEC_W)), const((1, 1))],
        out_shape=[jax.ShapeDtypeStruct((s, D_MODEL), F32)] + [jax.ShapeDtypeStruct((s, SEC_W), F32)] * 3
                  + [_view_shape(s, d, MM) for d in DILATIONS] * 3
                  + [jax.ShapeDtypeStruct((D_MODEL, D_MODEL), XCH), jax.ShapeDtypeStruct((1, D_MODEL), F32),
                     jax.ShapeDtypeStruct((1, SEC_W), F32), jax.ShapeDtypeStruct((1, 1), F32)],
        scratch_shapes=[pltpu.VMEM((4, LANE_GROUPS, tm, 128), F32), pltpu.VMEM((LANE_GROUPS, tm, 128), F32),
                        pltpu.VMEM((D_MODEL, D_MODEL), F32)],
        compiler_params=_params("arbitrary"),
    )(x, tgt, proj, proj, o_hg, *o_at, *lse_at, hg_norm_w, final_norm_w, wo_all)


def _inproj_bwd_x(x, norm_w, w_all, dh, dsec, dq_r, dk_r, dv, cos, sin, tm=256):
    s = x.shape[0]

    def body(x_ref, nw_ref, w_ref, dh_ref, s0, s1, s2, s3, s7, q1, q2, q3, k1, k2, k3, v1, v2, v3,
             cos_ref, sin_ref, gx_ref, dp_ref, gnw_ref, nat_ref):
        @pl.when(pl.program_id(0) == 0)
        def _():
            gnw_ref[...] = jnp.zeros_like(gnw_ref)

        def total(refs):
            acc = refs[0][...]
            for d, ref in zip(DILATIONS[1:], refs[1:]):
                _from_view(ref, nat_ref, d)
                acc = acc + _get_lanes(nat_ref)
            return acc

        c, sn = cos_ref[...], -sin_ref[...]
        dq, dk = total((q1, q2, q3)), total((k1, k2, k3))
        unrot = lambda a: jnp.concatenate(
            [_rope128(a[:, j * 128:(j + 1) * 128], c, sn) for j in range(SEC_W // 128)], axis=-1)
        secs = (s0[...], s1[...], s2[...], s3[...], unrot(dq), unrot(dk), total((v1, v2, v3)), s7[...])
        du = jnp.zeros((tm, D_MODEL), F32)
        for j, dsj in enumerate(secs):
            dsj = dsj.astype(MM)
            dp_ref[j] = dsj
            du = du + _dot_nt(dsj, w_ref[j])
        xv, nw = x_ref[...], nw_ref[...]
        rstd = lax.rsqrt(jnp.mean(xv * xv, axis=-1, keepdims=True) + NORM_EPS)
        xn = xv * rstd
        gnw_ref[...] += jnp.sum(du * xn, axis=0, keepdims=True)
        dxn = du * nw
        gx_ref[...] = dh_ref[...] + rstd * (dxn - xn * jnp.mean(dxn * xn, axis=-1, keepdims=True))

    row = lambda w: pl.BlockSpec((tm, w), lambda i: (i, 0))
    const = lambda shp: pl.BlockSpec(shp, lambda i: (0,) * len(shp))
    return pl.pallas_call(
        body, name="inproj_bwd_x", grid=(s // tm,),
        in_specs=[row(D_MODEL), const((1, D_MODEL)), const((N_SEC, D_MODEL, SEC_W)), row(D_MODEL)]
                 + [row(SEC_W)] * 5 + [_view_spec(tm, d) for d in DILATIONS] * 3 + [row(128), row(128)],
        out_specs=[row(D_MODEL), pl.BlockSpec((N_SEC, tm, SEC_W), lambda i: (0, i, 0)), const((1, D_MODEL))],
        out_shape=[jax.ShapeDtypeStruct((s, D_MODEL), F32), jax.ShapeDtypeStruct((N_SEC, s, SEC_W), MM),
                   jax.ShapeDtypeStruct((1, D_MODEL), F32)],
        scratch_shapes=[pltpu.VMEM((LANE_GROUPS, tm, 128), F32)],
        compiler_params=_params("arbitrary"),
    )(x, norm_w, w_all, dh, *dsec, *dq_r, *dk_r, *dv, cos, sin)


def _inproj_bwd_w(x, norm_w, dproj, tm=1024):
    s = x.shape[0]
    nb = s // tm

    def body(x_ref, nw_ref, dp_ref, gw_hbm, acc_ref, stage_ref):
        @pl.when(pl.program_id(0) == 0)
        def _():
            acc_ref[...] = jnp.zeros_like(acc_ref)

        xv = x_ref[...]
        rstd = lax.rsqrt(jnp.mean(xv * xv, axis=-1, keepdims=True) + NORM_EPS)
        u_t = (xv * rstd * nw_ref[...]).T.astype(MM)
        for j in range(N_SEC):
            acc_ref[j] += jnp.dot(u_t, dp_ref[j], preferred_element_type=F32)

        @pl.when(pl.program_id(0) == nb - 1)
        def _():
            for j in range(N_SEC):
                stage_ref[...] = acc_ref[j].astype(stage_ref.dtype)
                pltpu.sync_copy(stage_ref, gw_hbm.at[j])

    return pl.pallas_call(
        body, name="inproj_bwd_w", grid=(nb,),
        in_specs=[pl.BlockSpec((tm, D_MODEL), lambda i: (i, 0)), pl.BlockSpec((1, D_MODEL), lambda i: (0, 0)),
                  pl.BlockSpec((N_SEC, tm, SEC_W), lambda i: (0, i, 0))],
        out_specs=pl.BlockSpec(memory_space=pl.ANY),
        out_shape=jax.ShapeDtypeStruct((N_SEC, D_MODEL, SEC_W), XCH),
        scratch_shapes=[pltpu.VMEM((N_SEC, D_MODEL, SEC_W), F32), pltpu.VMEM((D_MODEL, SEC_W), XCH)],
        compiler_params=_params("arbitrary"),
    )(x, norm_w, dproj)


def _local_step(x, tgt, norm_w, w_all, lb_logits, hg_norm_w, wo_all, final_norm_w):
    s = x.shape[0]
    cos, sin = _rope_tables(s)
    proj = _inproj_fwd(x, norm_w, w_all)
    o_hg, sst = _hgrn_fwd(proj, lb_logits)
    qkv = _attn_prep(proj, cos, sin)
    qkv = [qkv[3 * i:3 * i + 3] for i in range(len(DILATIONS))]
    att = [_attn_fwd(*qkv_d, d) for qkv_d, d in zip(qkv, DILATIONS)]
    (dh, d_ohg, d_hgz, d_atz, do1, do2, do3, dl1, dl2, dl3, lp1, lp2, lp3, gwo, gfw, ghg, loss) = _mid(
        x, tgt, proj, o_hg, [a[0] for a in att], [a[1] for a in att], hg_norm_w, final_norm_w[None, :], wo_all)
    dxq, dxf, dxi, dlb = _hgrn_bwd(proj, lb_logits, d_ohg, sst)
    dq_r, dk_r, dv = [], [], []
    for d, qkv_d, do, lp, dl in zip(DILATIONS, qkv, (do1, do2, do3), (lp1, lp2, lp3), (dl1, dl2, dl3)):
        dq_r.append(_attn_bwd_dq(*qkv_d, do, lp, dl, d))
        dk_d, dv_d = _attn_bwd_dkv(*qkv_d, do, lp, dl, d)
        dk_r.append(dk_d)
        dv.append(dv_d)
    gx, dproj, gnw = _inproj_bwd_x(x, norm_w, w_all, dh, (dxq, dxf, dxi, d_hgz, d_atz), dq_r, dk_r, dv, cos, sin)
    gwi = _inproj_bwd_w(x, norm_w, dproj)
    small = jnp.concatenate([gnw, jnp.concatenate([dlb, ghg], axis=-1), gfw,
                             jnp.pad(loss, ((0, 0), (0, D_MODEL - 1)))], axis=0)
    return gx, gwi, gwo, small


def _coords():
    return lax.axis_index("x"), lax.axis_index("y"), lax.axis_index("c")


def _gather_weights(w_in, w_out):
    wo_rows = w_out.shape[0]

    def body(wi_ref, wo_ref, wi_all, wo_all, send_sems, recv_sems):
        x, y, c = _coords()
        me, sibling = (x, y, c), (x, y, 1 - c)
        chips = [(1 - x, y), (x, 1 - y), (1 - x, 1 - y)]
        slot = lambda p: 4 * p[0] + 2 * p[1] + p[2]

        def copies(k, block, to):
            return [pltpu.make_async_remote_copy(
                src_ref=ref.at[slot(block)], dst_ref=ref.at[slot(block)], send_sem=send_sems.at[a, k],
                recv_sem=recv_sems.at[a, k], device_id=to, device_id_type=MESH)
                for a, ref in enumerate((wi_all, wo_all))]

        wi_all[slot(me)] = wi_ref[...].astype(MM)
        wo_all[slot(me)] = wo_ref[...].astype(MM)
        first = copies(0, me, sibling)
        for j, chip in enumerate(chips):
            first += copies(1 + j, me, (*chip, c))
        for cp in first:
            cp.start()
        passed = []
        for j, chip in enumerate(chips):
            for cp in copies(1 + j, (*chip, c), me):
                cp.wait_recv()
            fwd = copies(4 + j, (*chip, c), sibling)
            for cp in fwd:
                cp.start()
            passed += fwd
        for cp in copies(0, sibling, me):
            cp.wait_recv()
        for j, chip in enumerate(chips):
            for cp in copies(4 + j, (*chip, 1 - c), me):
                cp.wait_recv()
        for cp in first + passed:
            cp.wait_send()

    vmem = pl.BlockSpec(memory_space=pltpu.VMEM)
    return pl.pallas_call(
        body, name="gather_weights",
        in_specs=[vmem, vmem], out_specs=[vmem, vmem],
        out_shape=[jax.ShapeDtypeStruct((N_DEV, D_MODEL, SEC_W), MM),
                   jax.ShapeDtypeStruct((N_DEV, wo_rows, D_MODEL), MM)],
        scratch_shapes=[pltpu.SemaphoreType.DMA((2, 7)), pltpu.SemaphoreType.DMA((2, 7))],
        compiler_params=pltpu.CompilerParams(vmem_limit_bytes=VMEM_LIMIT),
    )(w_in, w_out)


def _exchange_grads(gwi, gwo, small):
    def body(gwi_ref, gwo_ref, sm_ref, li_ref, lo_ref, ls_ref, send_sems, recv_sems, local_sems):
        x, y, c = _coords()
        me = 4 * x + 2 * y + c
        refs = ((gwi_ref, li_ref), (gwo_ref, lo_ref))
        own = [pltpu.make_async_copy(src.at[me], dst.at[me], local_sems.at[a]) for a, (src, dst) in enumerate(refs)]
        own.append(pltpu.make_async_copy(sm_ref, ls_ref.at[me], local_sems.at[2]))
        for cp in own:
            cp.start()
        sends = []
        for k in range(1, N_DEV):
            px, py, pc = x ^ (k >> 2), y ^ ((k >> 1) & 1), c ^ (k & 1)
            peer = 4 * px + 2 * py + pc
            for a, (src, dst) in enumerate(refs):
                sends.append(pltpu.make_async_remote_copy(
                    src_ref=src.at[peer], dst_ref=dst.at[me], send_sem=send_sems.at[a, k - 1],
                    recv_sem=recv_sems.at[a, k - 1], device_id=(px, py, pc), device_id_type=MESH))
            sends.append(pltpu.make_async_remote_copy(
                src_ref=sm_ref, dst_ref=ls_ref.at[me], send_sem=send_sems.at[2, k - 1],
                recv_sem=recv_sems.at[2, k - 1], device_id=(px, py, pc), device_id_type=MESH))
        for cp in sends:
            cp.start()
        for cp in sends:
            cp.wait_recv()
        for cp in sends:
            cp.wait_send()
        for cp in own:
            cp.wait()

    hbm = pl.BlockSpec(memory_space=pl.ANY)
    return pl.pallas_call(
        body, name="exchange_grads",
        in_specs=[hbm, hbm, hbm], out_specs=[hbm, hbm, hbm],
        out_shape=[jax.ShapeDtypeStruct(gwi.shape, gwi.dtype), jax.ShapeDtypeStruct(gwo.shape, gwo.dtype),
                   jax.ShapeDtypeStruct((N_DEV,) + small.shape, F32)],
        scratch_shapes=[pltpu.SemaphoreType.DMA((3, 7)), pltpu.SemaphoreType.DMA((3, 7)),
                        pltpu.SemaphoreType.DMA((3,))],
    )(gwi, gwo, small)


def _adamw(w, g, m, v):
    m = ADAM_B1 * m + (1.0 - ADAM_B1) * g
    v = ADAM_B2 * v + (1.0 - ADAM_B2) * (g * g)
    m_hat = m / (1.0 - ADAM_B1 ** ADAM_STEP)
    v_hat = v / (1.0 - ADAM_B2 ** ADAM_STEP)
    return -ADAM_LR * (m_hat / (jnp.sqrt(v_hat) + ADAM_EPS) + ADAM_WD * w), m, v


def _slot_sum(ref):
    g = ref[0].astype(F32)
    for i in range(1, N_DEV):
        g = g + ref[i].astype(F32)
    return g


def _update_matrix(name, landed, w, m, v, rows):
    r, c = w.shape

    def body(l_ref, w_ref, m_ref, v_ref, g_ref, d_ref, nm_ref, nv_ref):
        g = _slot_sum(l_ref)
        g_ref[...] = g
        d_ref[...], nm_ref[...], nv_ref[...] = _adamw(w_ref[...], g, m_ref[...], v_ref[...])

    blk = pl.BlockSpec((rows, c), lambda i: (i, 0))
    return pl.pallas_call(
        body, name=name, grid=(r // rows,),
        in_specs=[pl.BlockSpec((N_DEV, rows, c), lambda i: (0, i, 0)), blk, blk, blk],
        out_specs=[blk] * 4, out_shape=[jax.ShapeDtypeStruct((r, c), F32)] * 4,
        compiler_params=_params("parallel"),
    )(landed, w, m, v)


def _update_small(landed, lb_logits, ws, ms, vs):
    def body(l_ref, lbl_ref, w_ref, m_ref, v_ref, g_ref, d_ref, nm_ref, nv_ref, loss_ref):
        tot = _slot_sum(l_ref)
        _, dlb = _lower_bound(lbl_ref[...])
        g_lb = tot[1:2, :SEC_W] * dlb
        g = jnp.concatenate([tot[0:1], jnp.concatenate([g_lb, -g_lb], axis=-1),
                             jnp.pad(tot[1:2, SEC_W:], ((0, 0), (0, SEC_W))), tot[2:3]], axis=0)
        g_ref[...] = g
        d_ref[...], nm_ref[...], nv_ref[...] = _adamw(w_ref[...], g, m_ref[...], v_ref[...])
        loss_ref[...] = tot[3:4, 0:1]

    vmem = pl.BlockSpec(memory_space=pltpu.VMEM)
    return pl.pallas_call(
        body, name="update_small", in_specs=[vmem] * 5, out_specs=[vmem] * 5,
        out_shape=[jax.ShapeDtypeStruct((4, D_MODEL), F32)] * 4 + [jax.ShapeDtypeStruct((1, 1), F32)],
    )(landed, lb_logits, ws, ms, vs)


def _pack_small(norm_w, lb_logits, hg_norm_w, final_norm_w):
    return jnp.concatenate([norm_w, lb_logits.reshape(1, D_MODEL),
                            jnp.pad(hg_norm_w, ((0, 0), (0, D_MODEL - SEC_W))), final_norm_w[None, :]], axis=0)


def _unpack_small(a):
    return a[0:1], a[1].reshape(2, SEC_W), a[2:3, :SEC_W], a[3]


def kernel(x, norm_w, w_in, hgrn_lb_logits, hg_norm_w, w_out, final_norm_w, loss_target, m_norm_w, m_w_in, m_hgrn_lb_logits, m_hg_norm_w, m_w_out, m_final_norm_w, v_norm_w, v_w_in, v_hgrn_lb_logits, v_hg_norm_w, v_w_out, v_final_norm_w):
    w_all, wo_all = _gather_weights(w_in[0], w_out[0])
    gx, gwi, gwo, small = _local_step(x[0], loss_target[0], norm_w, w_all, hgrn_lb_logits, hg_norm_w,
                                      wo_all.reshape(D_MODEL, D_MODEL), final_norm_w)
    li, lo, ls = _exchange_grads(gwi, gwo.reshape(N_DEV, D_MODEL // N_DEV, D_MODEL), small)
    g_wi, d_wi, nm_wi, nv_wi = _update_matrix("update_w_in", li, w_in[0], m_w_in[0], v_w_in[0], 256)
    g_wo, d_wo, nm_wo, nv_wo = _update_matrix("update_w_out", lo, w_out[0], m_w_out[0], v_w_out[0], 128)
    g_s, d_s, nm_s, nv_s, loss = _update_small(
        ls, hgrn_lb_logits, _pack_small(norm_w, hgrn_lb_logits, hg_norm_w, final_norm_w),
        _pack_small(m_norm_w, m_hgrn_lb_logits, m_hg_norm_w, m_final_norm_w),
        _pack_small(v_norm_w, v_hgrn_lb_logits, v_hg_norm_w, v_final_norm_w))
    outs = []
    for small_out, wi, wo in ((g_s, g_wi, g_wo), (d_s, d_wi, d_wo), (nm_s, nm_wi, nm_wo), (nv_s, nv_wi, nv_wo)):
        nw, lb, hg, fw = _unpack_small(small_out)
        outs += [nw, wi[None], lb, hg, wo[None], fw]
    return (loss[0, 0], gx[None], *outs)
```

```python
import functools

import jax
import jax.numpy as jnp
from jax import lax
from jax.experimental import pallas as pl
from jax.experimental.pallas import tpu as pltpu

F32 = jnp.float32
MM = jnp.bfloat16
XCH = jnp.bfloat16
NORM_EPS = 1e-6
NEG = -1e30
N_DEV = 8
D_MODEL = 1024
N_SEC = 8
SEC_W = 512
HG_HEADS = 4
HG_D = 128
AT_HEADS = 8
AT_DH = 64
ATT_BLK = 128
AT_COLS = 512
AT_QB = 4
DILATIONS = (1, 4, 16)
ROPE_THETA = 10000.0
CH = 16
LB_LO, LB_HI = 1e-6, 1.0 - 1e-6
ADAM_LR, ADAM_B1, ADAM_B2, ADAM_EPS, ADAM_WD, ADAM_STEP = 0.001, 0.9, 0.999, 1e-08, 0.01, 10
VMEM_LIMIT = 56 * 1024 * 1024
MESH = pl.DeviceIdType.MESH


def _params(*sem):
    return pltpu.CompilerParams(dimension_semantics=sem, vmem_limit_bytes=VMEM_LIMIT)


def _sigmoid(x):
    return 1.0 / (1.0 + jnp.exp(-x))


def _dot(a, b):
    return jnp.dot(a.astype(MM), b.astype(MM), preferred_element_type=F32)


def _dot_nt(a, b):
    return lax.dot_general(a.astype(MM), b.astype(MM), (((1,), (1,)), ((), ())), preferred_element_type=F32)


def _dot_tn(a, b):
    return lax.dot_general(a.astype(MM), b.astype(MM), (((0,), (0,)), ((), ())), preferred_element_type=F32)


def _tri_dot(tri, g):
    g1 = g.astype(jnp.bfloat16)
    r1 = g - g1.astype(F32)
    g2 = r1.astype(jnp.bfloat16)
    g3 = (r1 - g2.astype(F32)).astype(jnp.bfloat16)
    t = tri.astype(jnp.bfloat16)
    d = functools.partial(jnp.dot, preferred_element_type=F32)
    return d(t, g1) + d(t, g2) + d(t, g3)


def _lower_bound(lbl):
    l0, l1 = lbl[0:1, :], lbl[1:2, :]
    m = jnp.maximum(l0, l1)
    e0, e1 = jnp.exp(l0 - m), jnp.exp(l1 - m)
    p = e0 / (e0 + e1)
    inside = (p >= LB_LO) & (p <= LB_HI)
    return jnp.clip(p, LB_LO, LB_HI), jnp.where(inside, p * (e1 / (e0 + e1)), 0.0)


def _iota2(shape, dim):
    return lax.broadcasted_iota(jnp.int32, shape, dim)


def _inproj_fwd(x, norm_w, w_all, tm=256):
    s = x.shape[0]

    def body(x_ref, nw_ref, w_ref, proj_ref):
        xv = x_ref[...]
        rstd = lax.rsqrt(jnp.mean(xv * xv, axis=-1, keepdims=True) + NORM_EPS)
        u = (xv * rstd * nw_ref[...]).astype(MM)
        for j in range(N_SEC):
            proj_ref[j] = jnp.dot(u, w_ref[j], preferred_element_type=F32)

    return pl.pallas_call(
        body, name="inproj_fwd", grid=(s // tm,),
        in_specs=[pl.BlockSpec((tm, D_MODEL), lambda i: (i, 0)),
                  pl.BlockSpec((1, D_MODEL), lambda i: (0, 0)),
                  pl.BlockSpec((N_SEC, D_MODEL, SEC_W), lambda i: (0, 0, 0))],
        out_specs=pl.BlockSpec((N_SEC, tm, SEC_W), lambda i: (0, i, 0)),
        out_shape=jax.ShapeDtypeStruct((N_SEC, s, SEC_W), F32),
        compiler_params=_params("parallel"),
    )(x, norm_w, w_all)


def _hgrn_gates(xq, xf, lb):
    sgq = _sigmoid(xq)
    sg = _sigmoid(xf)
    sn = _sigmoid(-xf)
    f = lb + (1.0 - lb) * sg
    return sgq, xq * sgq, sg, sn, f, (1.0 - lb) * sn


def _bdot(a, b, ca, cb):
    return lax.dot_general(a.astype(MM), b.astype(MM), (((ca,), (cb,)), ((0,), (0,))), preferred_element_type=F32)


def _chunk_masks(rb):
    row, col = _iota2((rb, rb), 0), _iota2((rb, rb), 1)
    same = (row // CH) == (col // CH)
    return same & (row >= col), same & (row <= col)


def _hgrn_fwd(proj, lb_logits, rb=256):
    s = proj.shape[1]
    nb, nc = s // rb, rb // CH

    def body(q_ref, f_ref, i_ref, lbl_ref, o_ref, sst_ref, st_ref, slab_ref, states_ref):
        @pl.when(pl.program_id(1) == 0)
        def _():
            st_ref[...] = jnp.zeros_like(st_ref)

        sst_ref[...] = st_ref[...]
        lb, _ = _lower_bound(lbl_ref[...])
        prefix, _ = _chunk_masks(rb)
        c3 = lambda a: a.reshape(nc, CH, HG_D)
        _, q, _, _, f, kk = _hgrn_gates(q_ref[...], f_ref[...], lb)
        b3 = c3(_tri_dot(prefix, jnp.log(f)))
        q3, kk3, v3 = c3(q), c3(kk), c3(i_ref[...])
        bl3 = b3[:, CH - 1:CH, :]
        for t in range(CH):
            slab_ref[:, t * CH:(t + 1) * CH, :] = (q3 * jnp.exp(jnp.minimum(b3 - b3[:, t:t + 1, :], 0.0))).astype(MM)
        r = _bdot(slab_ref[...], kk3, 2, 2)
        row, col = _iota2((nc, CH, CH), 1), _iota2((nc, CH, CH), 2)
        a = jnp.zeros((nc, CH, CH), F32)
        for t in range(CH):
            a = a + jnp.where(col == t, r[:, t * CH:(t + 1) * CH, :], 0.0)
        a = jnp.where(row >= col, a, 0.0)
        x_upd = _bdot(v3, kk3 * jnp.exp(bl3 - b3), 1, 1)
        ebl3 = jnp.exp(bl3)
        st = st_ref[...]
        for c in range(nc):
            states_ref[c] = st
            st = st * ebl3[c] + x_upd[c]
        st_ref[...] = st
        o3 = _bdot(q3 * jnp.exp(b3), states_ref[...], 2, 2) + _bdot(a, v3, 2, 1)
        o_ref[...] = o3.reshape(rb, HG_D)

    sec = lambda j: pl.BlockSpec((None, rb, HG_D), lambda h, i, j=j: (j, i, h))
    return pl.pallas_call(
        body, name="hgrn_fwd", grid=(HG_HEADS, nb),
        in_specs=[sec(0), sec(1), sec(2), pl.BlockSpec((2, HG_D), lambda h, i: (0, h))],
        out_specs=[pl.BlockSpec((rb, HG_D), lambda h, i: (i, h)),
                   pl.BlockSpec((None, None, HG_D, HG_D), lambda h, i: (i, h, 0, 0))],
        out_shape=[jax.ShapeDtypeStruct((s, SEC_W), F32),
                   jax.ShapeDtypeStruct((nb, HG_HEADS, HG_D, HG_D), F32)],
        scratch_shapes=[pltpu.VMEM((HG_D, HG_D), F32), pltpu.VMEM((nc, CH * CH, HG_D), MM),
                        pltpu.VMEM((nc, HG_D, HG_D), F32)],
        compiler_params=_params("parallel", "arbitrary"),
    )(proj, proj, proj, lb_logits)


def _hgrn_bwd(proj, lb_logits, d_o, sst, rb=256):
    s = proj.shape[1]
    nb, nc = s // rb, rb // CH

    def body(q_ref, f_ref, i_ref, lbl_ref, do_ref, sst_ref, dxq_ref, dxf_ref, dxi_ref, dlb_ref,
             dst_ref, states_ref, dstates_ref, lslab_ref, kslab_ref):
        @pl.when(pl.program_id(1) == 0)
        def _():
            dst_ref[...] = jnp.zeros_like(dst_ref)
            dlb_ref[...] = jnp.zeros_like(dlb_ref)

        lb, _ = _lower_bound(lbl_ref[...])
        prefix, suffix = _chunk_masks(rb)
        c3 = lambda a: a.reshape(nc, CH, HG_D)
        flat = lambda a: a.reshape(rb, HG_D)
        xq = q_ref[...]
        sgq, q, sg, sn, f, kk = _hgrn_gates(xq, f_ref[...], lb)
        b3 = c3(_tri_dot(prefix, jnp.log(f)))
        q3, kk3, v3, do3 = c3(q), c3(kk), c3(i_ref[...]), c3(do_ref[...])
        bl3 = b3[:, CH - 1:CH, :]
        eb3, ebl3, dec3 = jnp.exp(b3), jnp.exp(bl3), jnp.exp(bl3 - b3)
        qe3, kd3 = q3 * eb3, kk3 * dec3
        x_upd, y_upd = _bdot(v3, kd3, 1, 1), _bdot(do3, qe3, 1, 1)
        st = sst_ref[...]
        for c in range(nc):
            states_ref[c] = st
            st = st * ebl3[c] + x_upd[c]
        dst = dst_ref[...]
        for c in reversed(range(nc)):
            dstates_ref[c] = dst
            dst = dst * ebl3[c] + y_upd[c]
        dst_ref[...] = dst
        states, dstates = states_ref[...], dstates_ref[...]
        dqe = _bdot(do3, states, 2, 1)
        dkd = _bdot(v3, dstates, 2, 1)
        row, col = _iota2((nc, CH, CH), 1), _iota2((nc, CH, CH), 2)
        tril, triu = row >= col, row <= col
        d_a = jnp.where(tril, _bdot(do3, v3, 2, 2), 0.0)
        d_at = jnp.where(triu, _bdot(v3, do3, 2, 2), 0.0)
        for t in range(CH):
            bt = b3[:, t:t + 1, :]
            lslab_ref[:, t * CH:(t + 1) * CH, :] = (q3 * jnp.exp(jnp.minimum(b3 - bt, 0.0))).astype(MM)
            kslab_ref[:, t * CH:(t + 1) * CH, :] = (kk3 * jnp.exp(jnp.minimum(bt - b3, 0.0))).astype(MM)
        r = _bdot(kslab_ref[...], q3, 2, 2)
        a_t = jnp.zeros((nc, CH, CH), F32)
        for t in range(CH):
            a_t = a_t + jnp.where(col == t, r[:, t * CH:(t + 1) * CH, :], 0.0)
        a_t = jnp.where(triu, a_t, 0.0)
        dv = _bdot(kd3, dstates, 2, 2) + _bdot(a_t, do3, 2, 1)
        sel = (_iota2((CH, CH * CH), 1) % CH == _iota2((CH, CH * CH), 0)).astype(MM)
        blockdiag = _iota2((nc, CH, CH * CH), 2) // CH == _iota2((nc, CH, CH * CH), 1)
        tile = lambda m: jnp.where(blockdiag, _dot(m.reshape(rb, CH), sel).reshape(nc, CH, CH * CH), 0.0)
        dq_in = _bdot(tile(d_a), kslab_ref[...], 2, 1)
        dk_in = _bdot(tile(d_at), lslab_ref[...], 2, 1)
        dkd_kd = dkd * kd3
        db = dqe * qe3 - dkd_kd + q3 * dq_in - kk3 * dk_in
        dbl = jnp.sum(dkd_kd, axis=1, keepdims=True) + jnp.sum(dstates * states, axis=1, keepdims=True) * ebl3
        last = _iota2((nc, CH, HG_D), 1) == CH - 1
        dg = _tri_dot(suffix, flat(db + jnp.where(last, dbl, 0.0)))
        df = dg / f - flat(dkd * dec3 + dk_in)
        dxq_ref[...] = flat(dqe * eb3 + dq_in) * (sgq * (1.0 + xq * (1.0 - sgq)))
        dxf_ref[...] = df * (1.0 - lb) * sg * sn
        dxi_ref[...] = flat(dv)
        dlb_ref[...] += jnp.sum(df * sn, axis=0, keepdims=True)

    rev = lambda i: nb - 1 - i
    sec = lambda j: pl.BlockSpec((None, rb, HG_D), lambda h, i, j=j: (j, rev(i), h))
    blk = pl.BlockSpec((rb, HG_D), lambda h, i: (rev(i), h))
    return pl.pallas_call(
        body, name="hgrn_bwd", grid=(HG_HEADS, nb),
        in_specs=[sec(0), sec(1), sec(2), pl.BlockSpec((2, HG_D), lambda h, i: (0, h)), blk,
                  pl.BlockSpec((None, None, HG_D, HG_D), lambda h, i: (rev(i), h, 0, 0))],
        out_specs=[blk, blk, blk, pl.BlockSpec((1, HG_D), lambda h, i: (0, h))],
        out_shape=[jax.ShapeDtypeStruct((s, SEC_W), F32)] * 3 + [jax.ShapeDtypeStruct((1, SEC_W), F32)],
        scratch_shapes=[pltpu.VMEM((HG_D, HG_D), F32), pltpu.VMEM((nc, HG_D, HG_D), F32),
                        pltpu.VMEM((nc, HG_D, HG_D), F32),
                        pltpu.VMEM((nc, CH * CH, HG_D), MM), pltpu.VMEM((nc, CH * CH, HG_D), MM)],
        compiler_params=_params("parallel", "arbitrary"),
    )(proj, proj, proj, lb_logits, d_o, sst)


def _rope_tables(s):
    half = AT_DH // 2
    inv_freq = 1.0 / (ROPE_THETA ** (jnp.arange(half, dtype=F32) / half))
    ang = jnp.arange(s, dtype=jnp.int32).astype(F32)[:, None] * inv_freq[None, :]
    cos, sin = jnp.cos(ang), jnp.sin(ang)
    return jnp.concatenate([cos] * 4, axis=-1), jnp.concatenate([-sin, sin] * 2, axis=-1)


def _rope128(x, cos, sin):
    lo = (_iota2(x.shape, 1) % AT_DH) < AT_DH // 2
    rot = jnp.where(lo, pltpu.roll(x, 128 - AT_DH // 2, 1), pltpu.roll(x, AT_DH // 2, 1))
    return x * cos + rot * sin


LANE_GROUPS = SEC_W // 128


def _set_lanes(ref, val):
    for j in range(LANE_GROUPS):
        ref[j] = val[:, j * 128:(j + 1) * 128]


def _get_lanes(ref):
    return jnp.concatenate([ref[j] for j in range(LANE_GROUPS)], axis=-1)


def _to_view(src_ref, dst_ref, d):
    n = src_ref.shape[1] // d
    for r in range(d):
        rows = pl.ds(r, n, stride=d) if d > 1 else slice(None)
        for j in range(LANE_GROUPS):
            c0 = r * SEC_W + j * 128
            dst_ref[:, c0:c0 + 128] = src_ref.at[j][rows, :].astype(dst_ref.dtype)


def _from_view(src_ref, dst_ref, d):
    n = dst_ref.shape[1] // d
    for r in range(d):
        for j in range(LANE_GROUPS):
            c0 = r * SEC_W + j * 128
            dst_ref.at[j][pl.ds(r, n, stride=d), :] = src_ref[:, c0:c0 + 128]


def _view_spec(tm, d):
    return pl.BlockSpec((tm // d, d * SEC_W), lambda i: (i, 0))


def _view_shape(s, d, dtype):
    return jax.ShapeDtypeStruct((s // d, d * SEC_W), dtype)


def _attn_prep(proj, cos, sin, tm=512):
    s = proj.shape[1]

    def body(q_ref, k_ref, v_ref, cos_ref, sin_ref, *refs):
        outs, (qs_ref, ks_ref, vs_ref) = refs[:-3], refs[-3:]
        c, sn = cos_ref[...], sin_ref[...]
        for j in range(LANE_GROUPS):
            sl = slice(j * 128, (j + 1) * 128)
            qs_ref[j] = _rope128(q_ref[:, sl], c, sn) * (AT_DH ** -0.5)
            ks_ref[j] = _rope128(k_ref[:, sl], c, sn)
            vs_ref[j] = v_ref[:, sl]
        for i, d in enumerate(DILATIONS):
            for src_ref, dst_ref in zip((qs_ref, ks_ref, vs_ref), outs[3 * i:3 * i + 3]):
                _to_view(src_ref, dst_ref, d)

    sec = lambda j: pl.BlockSpec((None, tm, SEC_W), lambda i, j=j: (j, i, 0))
    tab = pl.BlockSpec((tm, 128), lambda i: (i, 0))
    return pl.pallas_call(
        body, name="attn_prep", grid=(s // tm,),
        in_specs=[sec(4), sec(5), sec(6), tab, tab],
        out_specs=[_view_spec(tm, d) for d in DILATIONS for _ in range(3)],
        out_shape=[_view_shape(s, d, MM) for d in DILATIONS for _ in range(3)],
        scratch_shapes=[pltpu.VMEM((LANE_GROUPS, tm, 128), F32)] * 3,
        compiler_params=_params("parallel"),
    )(proj, proj, proj, cos, sin)


def _band_mask(first_ok, second_ok):
    row, col = _iota2((ATT_BLK, 2 * ATT_BLK), 0), _iota2((ATT_BLK, 2 * ATT_BLK), 1)
    return ((col < ATT_BLK) & (col >= row) & first_ok) | ((col >= ATT_BLK) & ((col - ATT_BLK) <= row) & second_ok)


def _own_lanes(rows, h):
    lane = _iota2((rows, 128), 1)
    return (lane < AT_DH) if h == 0 else (lane >= AT_DH)


def _neg_pieces(rows, h):
    lane = _iota2((rows, 128), 1) - (AT_DH if h == 0 else 0)
    return jnp.where((lane >= 0) & (lane < 3), -1.0, 0.0).astype(MM)


def _units():
    return [(b, slice(g * 128, (g + 1) * 128), h) for b in range(AT_QB) for g in range(AT_COLS // 128) for h in range(2)]


def _sub(b):
    return slice(b * ATT_BLK, (b + 1) * ATT_BLK)


def _band_before(cur_ref, prev_ref, b, sl):
    if b == 0:
        return jnp.concatenate([prev_ref[:, sl], cur_ref[0:ATT_BLK, sl]], axis=0)
    return cur_ref[(b - 1) * ATT_BLK:(b + 1) * ATT_BLK, sl]


def _band_after(cur_ref, next_ref, b, sl):
    if b == AT_QB - 1:
        return jnp.concatenate([cur_ref[b * ATT_BLK:(b + 1) * ATT_BLK, sl], next_ref[:, sl]], axis=0)
    return cur_ref[b * ATT_BLK:(b + 2) * ATT_BLK, sl]


def _attn_specs(rows):
    assert rows % (AT_QB * ATT_BLK) == 0
    last = rows // ATT_BLK - 1
    cur = pl.BlockSpec((AT_QB * ATT_BLK, AT_COLS), lambda c, n: (n, c))
    prev = pl.BlockSpec((ATT_BLK, AT_COLS), lambda c, n: (jnp.maximum(AT_QB * n - 1, 0), c))
    nxt = pl.BlockSpec((ATT_BLK, AT_COLS), lambda c, n: (jnp.minimum(AT_QB * (n + 1), last), c))
    return cur, prev, nxt


def _attn_fwd(qr, kr, vr, d):
    rows, cols = qr.shape
    nb = rows // (AT_QB * ATT_BLK)

    def body(q_ref, kc_ref, kp_ref, vc_ref, vp_ref, o_ref, lse_ref):
        masks = {True: _band_mask(pl.program_id(1) > 0, True), False: _band_mask(True, True)}
        ones = jnp.ones((2 * ATT_BLK, 128), MM)
        head0 = _own_lanes(ATT_BLK, 0)
        units = _units()
        scs = []
        for b, sl, h in units:
            q2 = q_ref[_sub(b), sl]
            qh = jnp.where(_own_lanes(ATT_BLK, h), q2, jnp.zeros_like(q2))
            scs.append(jnp.where(masks[b == 0], _dot_nt(qh, _band_before(kc_ref, kp_ref, b, sl)), NEG))
        ms = [jnp.max(sc, axis=-1, keepdims=True) for sc in scs]
        ps = [jnp.exp(sc - m).astype(MM) for sc, m in zip(scs, ms)]
        ls = [jnp.dot(p, ones, preferred_element_type=F32) for p in ps]
        os_ = [jnp.dot(p, _band_before(vc_ref, vp_ref, b, sl), preferred_element_type=F32)
               for p, (b, sl, _) in zip(ps, units)]
        for i in range(0, len(units), 2):
            b, sl, _ = units[i]
            l = jnp.where(head0, ls[i], ls[i + 1])
            o_ref[_sub(b), sl] = jnp.where(head0, os_[i], os_[i + 1]) / l
            lse_ref[_sub(b), sl] = jnp.where(head0, ms[i], ms[i + 1]) + jnp.log(l)

    cur, prev, _ = _attn_specs(rows)
    o, lse = pl.pallas_call(
        body, name=f"attn_fwd_d{d}", grid=(cols // AT_COLS, nb),
        in_specs=[cur, cur, prev, cur, prev], out_specs=[cur, cur],
        out_shape=[jax.ShapeDtypeStruct((rows, cols), F32)] * 2,
        compiler_params=_params("parallel", "parallel"),
    )(qr, kr, kr, vr, vr)
    return o, lse


def _attn_bwd_dq(qr, kr, vr, do, lse, delta, d):
    rows, cols = qr.shape
    nb = rows // (AT_QB * ATT_BLK)

    def body(q_ref, kc_ref, kp_ref, vc_ref, vp_ref, do_ref, lse_ref, dl_ref, dq_ref):
        masks = {True: _band_mask(pl.program_id(1) > 0, True), False: _band_mask(True, True)}
        units = _units()
        sms, dps = [], []
        for b, sl, h in units:
            own, own_b, neg = _own_lanes(ATT_BLK, h), _own_lanes(2 * ATT_BLK, h), _neg_pieces(2 * ATT_BLK, h)
            sms.append(_dot_nt(jnp.where(own, q_ref[_sub(b), sl], lse_ref[_sub(b), sl]),
                               jnp.where(own_b, _band_before(kc_ref, kp_ref, b, sl), neg)))
            dps.append(_dot_nt(jnp.where(own, do_ref[_sub(b), sl], dl_ref[_sub(b), sl]),
                               jnp.where(own_b, _band_before(vc_ref, vp_ref, b, sl), neg)))
        dss = [(jnp.exp(jnp.where(masks[b == 0], sm, NEG)) * dp).astype(MM)
               for sm, dp, (b, _, _) in zip(sms, dps, units)]
        dqs = [jnp.dot(ds, _band_before(kc_ref, kp_ref, b, sl), preferred_element_type=F32) * (AT_DH ** -0.5)
               for ds, (b, sl, _) in zip(dss, units)]
        for i in range(0, len(units), 2):
            b, sl, _ = units[i]
            dq_ref[_sub(b), sl] = jnp.where(_own_lanes(ATT_BLK, 0), dqs[i], dqs[i + 1])

    cur, prev, _ = _attn_specs(rows)
    dq = pl.pallas_call(
        body, name=f"attn_bwd_dq_d{d}", grid=(cols // AT_COLS, nb),
        in_specs=[cur, cur, prev, cur, prev, cur, cur, cur], out_specs=cur,
        out_shape=jax.ShapeDtypeStruct((rows, cols), F32),
        compiler_params=_params("parallel", "parallel"),
    )(qr, kr, kr, vr, vr, do, lse, delta)
    return dq


def _attn_bwd_dkv(qr, kr, vr, do, lse, delta, d):
    rows, cols = qr.shape
    nb = rows // (AT_QB * ATT_BLK)

    def body(k_ref, v_ref, qc_ref, qn_ref, doc_ref, don_ref, lsec_ref, lsen_ref, dlc_ref, dln_ref,
             dk_ref, dv_ref):
        masks = {True: _band_mask(True, pl.program_id(1) < nb - 1), False: _band_mask(True, True)}
        units = _units()
        sms, dps = [], []
        for b, sl, h in units:
            own, own_b, neg = _own_lanes(ATT_BLK, h), _own_lanes(2 * ATT_BLK, h), _neg_pieces(ATT_BLK, h)
            sms.append(_dot_nt(jnp.where(own, k_ref[_sub(b), sl], neg),
                               jnp.where(own_b, _band_after(qc_ref, qn_ref, b, sl),
                                         _band_after(lsec_ref, lsen_ref, b, sl))))
            dps.append(_dot_nt(jnp.where(own, v_ref[_sub(b), sl], neg),
                               jnp.where(own_b, _band_after(doc_ref, don_ref, b, sl),
                                         _band_after(dlc_ref, dln_ref, b, sl))))
        ps = [jnp.exp(jnp.where(masks[b == AT_QB - 1], sm, NEG)) for sm, (b, _, _) in zip(sms, units)]
        dss = [(p * dp).astype(MM) for p, dp in zip(ps, dps)]
        dvs = [jnp.dot(p.astype(MM), _band_after(doc_ref, don_ref, b, sl), preferred_element_type=F32)
               for p, (b, sl, _) in zip(ps, units)]
        dks = [jnp.dot(ds, _band_after(qc_ref, qn_ref, b, sl), preferred_element_type=F32)
               for ds, (b, sl, _) in zip(dss, units)]
        head0 = _own_lanes(ATT_BLK, 0)
        for i in range(0, len(units), 2):
            b, sl, _ = units[i]
            dk_ref[_sub(b), sl] = jnp.where(head0, dks[i], dks[i + 1])
            dv_ref[_sub(b), sl] = jnp.where(head0, dvs[i], dvs[i + 1])

    cur, _, nxt = _attn_specs(rows)
    dk, dv = pl.pallas_call(
        body, name=f"attn_bwd_dkv_d{d}", grid=(cols // AT_COLS, nb),
        in_specs=[cur, cur, cur, nxt, cur, nxt, cur, nxt, cur, nxt], out_specs=[cur, cur],
        out_shape=[jax.ShapeDtypeStruct((rows, cols), F32)] * 2,
        compiler_params=_params("parallel", "parallel"),
    )(kr, vr, qr, qr, do, do, lse, lse, delta, delta)
    return dk, dv


def _head_sum(a, width):
    parts = []
    for j in range(a.shape[1] // width):
        sm = jnp.sum(a[:, j * width:(j + 1) * width], axis=-1, keepdims=True)
        parts.append(jnp.broadcast_to(sm, (a.shape[0], width)))
    return jnp.concatenate(parts, axis=-1)


def _partner_pieces(x):
    xs = jnp.concatenate([pltpu.roll(x[:, j * 128:(j + 1) * 128], AT_DH, 1) for j in range(x.shape[1] // 128)],
                         axis=-1)
    hi = xs.astype(jnp.bfloat16).astype(F32)
    mid = (xs - hi).astype(jnp.bfloat16).astype(F32)
    lo = (xs - hi - mid).astype(jnp.bfloat16).astype(F32)
    lane = _iota2(x.shape, 1) % AT_DH
    return jnp.where(lane == 0, hi, jnp.where(lane == 1, mid, jnp.where(lane == 2, lo, 0.0)))


def _mid(x, tgt, proj, o_hg, o_at, lse_at, hg_norm_w, final_norm_w, wo_all, tm=256):
    s = x.shape[0]
    nb = s // tm

    def body(x_ref, t_ref, hgz_ref, atz_ref, ohg_ref, o1_ref, o2_ref, o3_ref, l1_ref, l2_ref, l3_ref,
             g_ref, fw_ref, wo_ref,
             dh_ref, dohg_ref, dhgz_ref, datz_ref, do1_ref, do2_ref, do3_ref, dl1_ref, dl2_ref, dl3_ref,
             lp1_ref, lp2_ref, lp3_ref,
             gwo_ref, gfw_ref, ghg_ref, loss_ref, nat_ref, stage_ref, gwo_acc):
        @pl.when(pl.program_id(0) == 0)
        def _():
            gwo_acc[...] = jnp.zeros_like(gwo_acc)
            gfw_ref[...] = jnp.zeros_like(gfw_ref)
            ghg_ref[...] = jnp.zeros_like(ghg_ref)
            loss_ref[...] = jnp.zeros_like(loss_ref)

        ohg, g = ohg_ref[...], g_ref[...]
        rs = lax.rsqrt(_head_sum(ohg * ohg, HG_D) * (1.0 / HG_D) + NORM_EPS)
        on = ohg * rs
        hgz = hgz_ref[...]
        sz = _sigmoid(hgz)
        gate_hg = hgz * sz
        lses, outs = [l1_ref[...]], [o1_ref[...]]
        for k, (d, l_ref, o_ref) in enumerate(zip(DILATIONS[1:], (l2_ref, l3_ref), (o2_ref, o3_ref))):
            _from_view(l_ref, nat_ref.at[2 * k], d)
            _from_view(o_ref, nat_ref.at[2 * k + 1], d)
            lses.append(_get_lanes(nat_ref.at[2 * k]))
            outs.append(_get_lanes(nat_ref.at[2 * k + 1]))
        mx = jnp.maximum(jnp.maximum(lses[0], lses[1]), lses[2])
        es = [jnp.exp(l - mx) for l in lses]
        den = es[0] + es[1] + es[2]
        ws = [e / den for e in es]
        oat = ws[0] * outs[0] + ws[1] * outs[1] + ws[2] * outs[2]
        atz = atz_ref[...]
        sa = _sigmoid(atz)
        gate_at = atz * sa
        mixed = jnp.concatenate([on * g * gate_hg, oat * gate_at], axis=-1).astype(MM)
        h = x_ref[...] + jnp.dot(mixed, wo_ref[...], preferred_element_type=F32)
        rstd = lax.rsqrt(jnp.mean(h * h, axis=-1, keepdims=True) + NORM_EPS)
        hn = h * rstd
        fw = fw_ref[...]
        err = hn * fw - t_ref[...]
        loss_ref[...] += 0.5 * jnp.sum(jnp.mean(err * err, axis=-1, keepdims=True), axis=0, keepdims=True)
        dout = err * (1.0 / D_MODEL)
        gfw_ref[...] += jnp.sum(dout * hn, axis=0, keepdims=True)
        dhn = dout * fw
        dh = rstd * (dhn - hn * jnp.mean(dhn * hn, axis=-1, keepdims=True))
        dh_ref[...] = dh
        dh_mm = dh.astype(MM)
        gwo_acc[...] += _dot_tn(mixed, dh_mm)

        @pl.when(pl.program_id(0) == nb - 1)
        def _():
            gwo_ref[...] = gwo_acc[...].astype(gwo_ref.dtype)

        dmixed = _dot_nt(dh_mm, wo_ref[...])
        dm_hg = dmixed[:, :SEC_W]
        d_ong = dm_hg * gate_hg
        dhgz_ref[...] = dm_hg * (on * g) * (sz * (1.0 + hgz * (1.0 - sz)))
        ghg_ref[...] += jnp.sum(d_ong * on, axis=0, keepdims=True)
        d_on = d_ong * g
        dohg_ref[...] = rs * (d_on - on * (_head_sum(d_on * on, HG_D) * (1.0 / HG_D)))
        dm_at = dmixed[:, SEC_W:]
        d_oat = dm_at * gate_at
        datz_ref[...] = dm_at * oat * (sa * (1.0 + atz * (1.0 - sa)))
        drow = _head_sum(d_oat * oat, AT_DH)
        for d, w, lse, do_ref, dl_ref, lp_ref in zip(DILATIONS, ws, lses, (do1_ref, do2_ref, do3_ref),
                                                     (dl1_ref, dl2_ref, dl3_ref), (lp1_ref, lp2_ref, lp3_ref)):
            for val, dst_ref in ((w * d_oat, do_ref), (_partner_pieces(w * drow), dl_ref),
                                 (_partner_pieces(lse), lp_ref)):
                _set_lanes(stage_ref, val)
                _to_view(stage_ref, dst_ref, d)

    row = lambda w: pl.BlockSpec((tm, w), lambda i: (i, 0))
    sec = lambda j: pl.BlockSpec((None, tm, SEC_W), lambda i, j=j: (j, i, 0))
    const = lambda shp: pl.BlockSpec(shp, lambda i: (0,) * len(shp))
    half = row(SEC_W)
    views = [_view_spec(tm, d) for d in DILATIONS]
    return pl.pallas_call(
        body, name="mid", grid=(nb,),
        in_specs=[row(D_MODEL), row(D_MODEL), sec(3), sec(7), half] + views * 2
                 + [const((1, SEC_W)), const((1, D_MODEL)), const((D_MODEL, D_MODEL))],
        out_specs=[row(D_MODEL)] + [half] * 3 + views * 3
                  + [const((D_MODEL, D_MODEL)), const((1, D_MODEL)), const((1, SEC_W)), const((1, 1))],
        out_shape=[jax.ShapeDtypeStruct((s, D_MODEL), F32)] + [jax.ShapeDtypeStruct((s, SEC_W), F32)] * 3
                  + [_view_shape(s, d, MM) for d in DILATIONS] * 3
                  + [jax.ShapeDtypeStruct((D_MODEL, D_MODEL), XCH), jax.ShapeDtypeStruct((1, D_MODEL), F32),
                     jax.ShapeDtypeStruct((1, SEC_W), F32), jax.ShapeDtypeStruct((1, 1), F32)],
        scratch_shapes=[pltpu.VMEM((4, LANE_GROUPS, tm, 128), F32), pltpu.VMEM((LANE_GROUPS, tm, 128), F32),
                        pltpu.VMEM((D_MODEL, D_MODEL), F32)],
        compiler_params=_params("arbitrary"),
    )(x, tgt, proj, proj, o_hg, *o_at, *lse_at, hg_norm_w, final_norm_w, wo_all)


def _inproj_bwd_x(x, norm_w, w_all, dh, dsec, dq_r, dk_r, dv, cos, sin, tm=256):
    s = x.shape[0]

    def body(x_ref, nw_ref, w_ref, dh_ref, s0, s1, s2, s3, s7, q1, q2, q3, k1, k2, k3, v1, v2, v3,
             cos_ref, sin_ref, gx_ref, dp_ref, gnw_ref, nat_ref):
        @pl.when(pl.program_id(0) == 0)
        def _():
            gnw_ref[...] = jnp.zeros_like(gnw_ref)

        def total(refs):
            acc = refs[0][...]
            for d, ref in zip(DILATIONS[1:], refs[1:]):
                _from_view(ref, nat_ref, d)
                acc = acc + _get_lanes(nat_ref)
            return acc

        c, sn = cos_ref[...], -sin_ref[...]
        dq, dk = total((q1, q2, q3)), total((k1, k2, k3))
        unrot = lambda a: jnp.concatenate(
            [_rope128(a[:, j * 128:(j + 1) * 128], c, sn) for j in range(SEC_W // 128)], axis=-1)
        secs = (s0[...], s1[...], s2[...], s3[...], unrot(dq), unrot(dk), total((v1, v2, v3)), s7[...])
        du = jnp.zeros((tm, D_MODEL), F32)
        for j, dsj in enumerate(secs):
            dsj = dsj.astype(MM)
            dp_ref[j] = dsj
            du = du + _dot_nt(dsj, w_ref[j])
        xv, nw = x_ref[...], nw_ref[...]
        rstd = lax.rsqrt(jnp.mean(xv * xv, axis=-1, keepdims=True) + NORM_EPS)
        xn = xv * rstd
        gnw_ref[...] += jnp.sum(du * xn, axis=0, keepdims=True)
        dxn = du * nw
        gx_ref[...] = dh_ref[...] + rstd * (dxn - xn * jnp.mean(dxn * xn, axis=-1, keepdims=True))

    row = lambda w: pl.BlockSpec((tm, w), lambda i: (i, 0))
    const = lambda shp: pl.BlockSpec(shp, lambda i: (0,) * len(shp))
    return pl.pallas_call(
        body, name="inproj_bwd_x", grid=(s // tm,),
        in_specs=[row(D_MODEL), const((1, D_MODEL)), const((N_SEC, D_MODEL, SEC_W)), row(D_MODEL)]
                 + [row(SEC_W)] * 5 + [_view_spec(tm, d) for d in DILATIONS] * 3 + [row(128), row(128)],
        out_specs=[row(D_MODEL), pl.BlockSpec((N_SEC, tm, SEC_W), lambda i: (0, i, 0)), const((1, D_MODEL))],
        out_shape=[jax.ShapeDtypeStruct((s, D_MODEL), F32), jax.ShapeDtypeStruct((N_SEC, s, SEC_W), MM),
                   jax.ShapeDtypeStruct((1, D_MODEL), F32)],
        scratch_shapes=[pltpu.VMEM((LANE_GROUPS, tm, 128), F32)],
        compiler_params=_params("arbitrary"),
    )(x, norm_w, w_all, dh, *dsec, *dq_r, *dk_r, *dv, cos, sin)


def _inproj_bwd_w(x, norm_w, dproj, tm=1024):
    s = x.shape[0]
    nb = s // tm

    def body(x_ref, nw_ref, dp_ref, gw_hbm, acc_ref, stage_ref):
        @pl.when(pl.program_id(0) == 0)
        def _():
            acc_ref[...] = jnp.zeros_like(acc_ref)

        xv = x_ref[...]
        rstd = lax.rsqrt(jnp.mean(xv * xv, axis=-1, keepdims=True) + NORM_EPS)
        u_t = (xv * rstd * nw_ref[...]).T.astype(MM)
        for j in range(N_SEC):
            acc_ref[j] += jnp.dot(u_t, dp_ref[j], preferred_element_type=F32)

        @pl.when(pl.program_id(0) == nb - 1)
        def _():
            for j in range(N_SEC):
                stage_ref[...] = acc_ref[j].astype(stage_ref.dtype)
                pltpu.sync_copy(stage_ref, gw_hbm.at[j])

    return pl.pallas_call(
        body, name="inproj_bwd_w", grid=(nb,),
        in_specs=[pl.BlockSpec((tm, D_MODEL), lambda i: (i, 0)), pl.BlockSpec((1, D_MODEL), lambda i: (0, 0)),
                  pl.BlockSpec((N_SEC, tm, SEC_W), lambda i: (0, i, 0))],
        out_specs=pl.BlockSpec(memory_space=pl.ANY),
        out_shape=jax.ShapeDtypeStruct((N_SEC, D_MODEL, SEC_W), XCH),
        scratch_shapes=[pltpu.VMEM((N_SEC, D_MODEL, SEC_W), F32), pltpu.VMEM((D_MODEL, SEC_W), XCH)],
        compiler_params=_params("arbitrary"),
    )(x, norm_w, dproj)


def _local_step(x, tgt, norm_w, w_all, lb_logits, hg_norm_w, wo_all, final_norm_w):
    s = x.shape[0]
    cos, sin = _rope_tables(s)
    proj = _inproj_fwd(x, norm_w, w_all)
    o_hg, sst = _hgrn_fwd(proj, lb_logits)
    qkv = _attn_prep(proj, cos, sin)
    qkv = [qkv[3 * i:3 * i + 3] for i in range(len(DILATIONS))]
    att = [_attn_fwd(*qkv_d, d) for qkv_d, d in zip(qkv, DILATIONS)]
    (dh, d_ohg, d_hgz, d_atz, do1, do2, do3, dl1, dl2, dl3, lp1, lp2, lp3, gwo, gfw, ghg, loss) = _mid(
        x, tgt, proj, o_hg, [a[0] for a in att], [a[1] for a in att], hg_norm_w, final_norm_w[None, :], wo_all)
    dxq, dxf, dxi, dlb = _hgrn_bwd(proj, lb_logits, d_ohg, sst)
    dq_r, dk_r, dv = [], [], []
    for d, qkv_d, do, lp, dl in zip(DILATIONS, qkv, (do1, do2, do3), (lp1, lp2, lp3), (dl1, dl2, dl3)):
        dq_r.append(_attn_bwd_dq(*qkv_d, do, lp, dl, d))
        dk_d, dv_d = _attn_bwd_dkv(*qkv_d, do, lp, dl, d)
        dk_r.append(dk_d)
        dv.append(dv_d)
    gx, dproj, gnw = _inproj_bwd_x(x, norm_w, w_all, dh, (dxq, dxf, dxi, d_hgz, d_atz), dq_r, dk_r, dv, cos, sin)
    gwi = _inproj_bwd_w(x, norm_w, dproj)
    small = jnp.concatenate([gnw, jnp.concatenate([dlb, ghg], axis=-1), gfw,
                             jnp.pad(loss, ((0, 0), (0, D_MODEL - 1)))], axis=0)
    return gx, gwi, gwo, small


def _coords():
    return lax.axis_index("x"), lax.axis_index("y"), lax.axis_index("c")


def _gather_weights(w_in, w_out):
    wo_rows = w_out.shape[0]

    def body(wi_ref, wo_ref, wi_all, wo_all, send_sems, recv_sems):
        x, y, c = _coords()
        me, sibling = (x, y, c), (x, y, 1 - c)
        chips = [(1 - x, y), (x, 1 - y), (1 - x, 1 - y)]
        slot = lambda p: 4 * p[0] + 2 * p[1] + p[2]

        def copies(k, block, to):
            return [pltpu.make_async_remote_copy(
                src_ref=ref.at[slot(block)], dst_ref=ref.at[slot(block)], send_sem=send_sems.at[a, k],
                recv_sem=recv_sems.at[a, k], device_id=to, device_id_type=MESH)
                for a, ref in enumerate((wi_all, wo_all))]

        wi_all[slot(me)] = wi_ref[...].astype(MM)
        wo_all[slot(me)] = wo_ref[...].astype(MM)
        first = copies(0, me, sibling)
        for j, chip in enumerate(chips):
            first += copies(1 + j, me, (*chip, c))
        for cp in first:
            cp.start()
        passed = []
        for j, chip in enumerate(chips):
            for cp in copies(1 + j, (*chip, c), me):
                cp.wait_recv()
            fwd = copies(4 + j, (*chip, c), sibling)
            for cp in fwd:
                cp.start()
            passed += fwd
        for cp in copies(0, sibling, me):
            cp.wait_recv()
        for j, chip in enumerate(chips):
            for cp in copies(4 + j, (*chip, 1 - c), me):
                cp.wait_recv()
        for cp in first + passed:
            cp.wait_send()

    vmem = pl.BlockSpec(memory_space=pltpu.VMEM)
    return pl.pallas_call(
        body, name="gather_weights",
        in_specs=[vmem, vmem], out_specs=[vmem, vmem],
        out_shape=[jax.ShapeDtypeStruct((N_DEV, D_MODEL, SEC_W), MM),
                   jax.ShapeDtypeStruct((N_DEV, wo_rows, D_MODEL), MM)],
        scratch_shapes=[pltpu.SemaphoreType.DMA((2, 7)), pltpu.SemaphoreType.DMA((2, 7))],
        compiler_params=pltpu.CompilerParams(vmem_limit_bytes=VMEM_LIMIT),
    )(w_in, w_out)


def _exchange_grads(gwi, gwo, small):
    def body(gwi_ref, gwo_ref, sm_ref, li_ref, lo_ref, ls_ref, send_sems, recv_sems, local_sems):
        x, y, c = _coords()
        me = 4 * x + 2 * y + c
        refs = ((gwi_ref, li_ref), (gwo_ref, lo_ref))
        own = [pltpu.make_async_copy(src.at[me], dst.at[me], local_sems.at[a]) for a, (src, dst) in enumerate(refs)]
        own.append(pltpu.make_async_copy(sm_ref, ls_ref.at[me], local_sems.at[2]))
        for cp in own:
            cp.start()
        sends = []
        for k in range(1, N_DEV):
            px, py, pc = x ^ (k >> 2), y ^ ((k >> 1) & 1), c ^ (k & 1)
            peer = 4 * px + 2 * py + pc
            for a, (src, dst) in enumerate(refs):
                sends.append(pltpu.make_async_remote_copy(
                    src_ref=src.at[peer], dst_ref=dst.at[me], send_sem=send_sems.at[a, k - 1],
                    recv_sem=recv_sems.at[a, k - 1], device_id=(px, py, pc), device_id_type=MESH))
            sends.append(pltpu.make_async_remote_copy(
                src_ref=sm_ref, dst_ref=ls_ref.at[me], send_sem=send_sems.at[2, k - 1],
                recv_sem=recv_sems.at[2, k - 1], device_id=(px, py, pc), device_id_type=MESH))
        for cp in sends:
            cp.start()
        for cp in sends:
            cp.wait_recv()
        for cp in sends:
            cp.wait_send()
        for cp in own:
            cp.wait()

    hbm = pl.BlockSpec(memory_space=pl.ANY)
    return pl.pallas_call(
        body, name="exchange_grads",
        in_specs=[hbm, hbm, hbm], out_specs=[hbm, hbm, hbm],
        out_shape=[jax.ShapeDtypeStruct(gwi.shape, gwi.dtype), jax.ShapeDtypeStruct(gwo.shape, gwo.dtype),
                   jax.ShapeDtypeStruct((N_DEV,) + small.shape, F32)],
        scratch_shapes=[pltpu.SemaphoreType.DMA((3, 7)), pltpu.SemaphoreType.DMA((3, 7)),
                        pltpu.SemaphoreType.DMA((3,))],
    )(gwi, gwo, small)


def _adamw(w, g, m, v):
    m = ADAM_B1 * m + (1.0 - ADAM_B1) * g
    v = ADAM_B2 * v + (1.0 - ADAM_B2) * (g * g)
    m_hat = m / (1.0 - ADAM_B1 ** ADAM_STEP)
    v_hat = v / (1.0 - ADAM_B2 ** ADAM_STEP)
    return -ADAM_LR * (m_hat / (jnp.sqrt(v_hat) + ADAM_EPS) + ADAM_WD * w), m, v


def _slot_sum(ref):
    g = ref[0].astype(F32)
    for i in range(1, N_DEV):
        g = g + ref[i].astype(F32)
    return g


def _update_matrix(name, landed, w, m, v, rows):
    r, c = w.shape

    def body(l_ref, w_ref, m_ref, v_ref, g_ref, d_ref, nm_ref, nv_ref):
        g = _slot_sum(l_ref)
        g_ref[...] = g
        d_ref[...], nm_ref[...], nv_ref[...] = _adamw(w_ref[...], g, m_ref[...], v_ref[...])

    blk = pl.BlockSpec((rows, c), lambda i: (i, 0))
    return pl.pallas_call(
        body, name=name, grid=(r // rows,),
        in_specs=[pl.BlockSpec((N_DEV, rows, c), lambda i: (0, i, 0)), blk, blk, blk],
        out_specs=[blk] * 4, out_shape=[jax.ShapeDtypeStruct((r, c), F32)] * 4,
        compiler_params=_params("parallel"),
    )(landed, w, m, v)


def _update_small(landed, lb_logits, ws, ms, vs):
    def body(l_ref, lbl_ref, w_ref, m_ref, v_ref, g_ref, d_ref, nm_ref, nv_ref, loss_ref):
        tot = _slot_sum(l_ref)
        _, dlb = _lower_bound(lbl_ref[...])
        g_lb = tot[1:2, :SEC_W] * dlb
        g = jnp.concatenate([tot[0:1], jnp.concatenate([g_lb, -g_lb], axis=-1),
                             jnp.pad(tot[1:2, SEC_W:], ((0, 0), (0, SEC_W))), tot[2:3]], axis=0)
        g_ref[...] = g
        d_ref[...], nm_ref[...], nv_ref[...] = _adamw(w_ref[...], g, m_ref[...], v_ref[...])
        loss_ref[...] = tot[3:4, 0:1]

    vmem = pl.BlockSpec(memory_space=pltpu.VMEM)
    return pl.pallas_call(
        body, name="update_small", in_specs=[vmem] * 5, out_specs=[vmem] * 5,
        out_shape=[jax.ShapeDtypeStruct((4, D_MODEL), F32)] * 4 + [jax.ShapeDtypeStruct((1, 1), F32)],
    )(landed, lb_logits, ws, ms, vs)


def _pack_small(norm_w, lb_logits, hg_norm_w, final_norm_w):
    return jnp.concatenate([norm_w, lb_logits.reshape(1, D_MODEL),
                            jnp.pad(hg_norm_w, ((0, 0), (0, D_MODEL - SEC_W))), final_norm_w[None, :]], axis=0)


def _unpack_small(a):
    return a[0:1], a[1].reshape(2, SEC_W), a[2:3, :SEC_W], a[3]


def kernel(x, norm_w, w_in, hgrn_lb_logits, hg_norm_w, w_out, final_norm_w, loss_target, m_norm_w, m_w_in, m_hgrn_lb_logits, m_hg_norm_w, m_w_out, m_final_norm_w, v_norm_w, v_w_in, v_hgrn_lb_logits, v_hg_norm_w, v_w_out, v_final_norm_w):
    w_all, wo_all = _gather_weights(w_in[0], w_out[0])
    gx, gwi, gwo, small = _local_step(x[0], loss_target[0], norm_w, w_all, hgrn_lb_logits, hg_norm_w,
                                      wo_all.reshape(D_MODEL, D_MODEL), final_norm_w)
    li, lo, ls = _exchange_grads(gwi, gwo.reshape(N_DEV, D_MODEL // N_DEV, D_MODEL), small)
    g_wi, d_wi, nm_wi, nv_wi = _update_matrix("update_w_in", li, w_in[0], m_w_in[0], v_w_in[0], 256)
    g_wo, d_wo, nm_wo, nv_wo = _update_matrix("update_w_out", lo, w_out[0], m_w_out[0], v_w_out[0], 128)
    g_s, d_s, nm_s, nv_s, loss = _update_small(
        ls, hgrn_lb_logits, _pack_small(norm_w, hgrn_lb_logits, hg_norm_w, final_norm_w),
        _pack_small(m_norm_w, m_hgrn_lb_logits, m_hg_norm_w, m_final_norm_w),
        _pack_small(v_norm_w, v_hgrn_lb_logits, v_hg_norm_w, v_final_norm_w))
    outs = []
    for small_out, wi, wo in ((g_s, g_wi, g_wo), (d_s, d_wi, d_wo), (nm_s, nm_wi, nm_wo), (nv_s, nv_wi, nv_wo)):
        nw, lb, hg, fw = _unpack_small(small_out)
        outs += [nw, wi[None], lb, hg, wo[None], fw]
    return (loss[0, 0], gx[None], *outs)
```

```python
import functools

import jax
import jax.numpy as jnp
from jax import lax
from jax.experimental import pallas as pl
from jax.experimental.pallas import tpu as pltpu

F32 = jnp.float32
MM = jnp.bfloat16
XCH = jnp.bfloat16
NORM_EPS = 1e-6
NEG = -1e30
N_DEV = 8
D_MODEL = 1024
N_SEC = 8
SEC_W = 512
HG_HEADS = 4
HG_D = 128
AT_HEADS = 8
AT_DH = 64
ATT_BLK = 128
AT_COLS = 512
AT_QB = 4
DILATIONS = (1, 4, 16)
ROPE_THETA = 10000.0
CH = 16
LB_LO, LB_HI = 1e-6, 1.0 - 1e-6
ADAM_LR, ADAM_B1, ADAM_B2, ADAM_EPS, ADAM_WD, ADAM_STEP = 0.001, 0.9, 0.999, 1e-08, 0.01, 10
VMEM_LIMIT = 56 * 1024 * 1024
MESH = pl.DeviceIdType.MESH


def _params(*sem):
    return pltpu.CompilerParams(dimension_semantics=sem, vmem_limit_bytes=VMEM_LIMIT)


def _sigmoid(x):
    return 1.0 / (1.0 + jnp.exp(-x))


def _dot(a, b):
    return jnp.dot(a.astype(MM), b.astype(MM), preferred_element_type=F32)


def _dot_nt(a, b):
    return lax.dot_general(a.astype(MM), b.astype(MM), (((1,), (1,)), ((), ())), preferred_element_type=F32)


def _dot_tn(a, b):
    return lax.dot_general(a.astype(MM), b.astype(MM), (((0,), (0,)), ((), ())), preferred_element_type=F32)


def _tri_dot(tri, g):
    g1 = g.astype(jnp.bfloat16)
    r1 = g - g1.astype(F32)
    g2 = r1.astype(jnp.bfloat16)
    g3 = (r1 - g2.astype(F32)).astype(jnp.bfloat16)
    t = tri.astype(jnp.bfloat16)
    d = functools.partial(jnp.dot, preferred_element_type=F32)
    return d(t, g1) + d(t, g2) + d(t, g3)


def _lower_bound(lbl):
    l0, l1 = lbl[0:1, :], lbl[1:2, :]
    m = jnp.maximum(l0, l1)
    e0, e1 = jnp.exp(l0 - m), jnp.exp(l1 - m)
    p = e0 / (e0 + e1)
    inside = (p >= LB_LO) & (p <= LB_HI)
    return jnp.clip(p, LB_LO, LB_HI), jnp.where(inside, p * (e1 / (e0 + e1)), 0.0)


def _iota2(shape, dim):
    return lax.broadcasted_iota(jnp.int32, shape, dim)


def _inproj_fwd(x, norm_w, w_all, tm=256):
    s = x.shape[0]

    def body(x_ref, nw_ref, w_ref, proj_ref):
        xv = x_ref[...]
        rstd = lax.rsqrt(jnp.mean(xv * xv, axis=-1, keepdims=True) + NORM_EPS)
        u = (xv * rstd * nw_ref[...]).astype(MM)
        for j in range(N_SEC):
            proj_ref[j] = jnp.dot(u, w_ref[j], preferred_element_type=F32)

    return pl.pallas_call(
        body, name="inproj_fwd", grid=(s // tm,),
        in_specs=[pl.BlockSpec((tm, D_MODEL), lambda i: (i, 0)),
                  pl.BlockSpec((1, D_MODEL), lambda i: (0, 0)),
                  pl.BlockSpec((N_SEC, D_MODEL, SEC_W), lambda i: (0, 0, 0))],
        out_specs=pl.BlockSpec((N_SEC, tm, SEC_W), lambda i: (0, i, 0)),
        out_shape=jax.ShapeDtypeStruct((N_SEC, s, SEC_W), F32),
        compiler_params=_params("parallel"),
    )(x, norm_w, w_all)


def _hgrn_gates(xq, xf, lb):
    sgq = _sigmoid(xq)
    sg = _sigmoid(xf)
    sn = _sigmoid(-xf)
    f = lb + (1.0 - lb) * sg
    return sgq, xq * sgq, sg, sn, f, (1.0 - lb) * sn


def _bdot(a, b, ca, cb):
    return lax.dot_general(a.astype(MM), b.astype(MM), (((ca,), (cb,)), ((0,), (0,))), preferred_element_type=F32)


def _chunk_masks(rb):
    row, col = _iota2((rb, rb), 0), _iota2((rb, rb), 1)
    same = (row // CH) == (col // CH)
    return same & (row >= col), same & (row <= col)


def _hgrn_fwd(proj, lb_logits, rb=256):
    s = proj.shape[1]
    nb, nc = s // rb, rb // CH

    def body(q_ref, f_ref, i_ref, lbl_ref, o_ref, sst_ref, st_ref, slab_ref, states_ref):
        @pl.when(pl.program_id(1) == 0)
        def _():
            st_ref[...] = jnp.zeros_like(st_ref)

        sst_ref[...] = st_ref[...]
        lb, _ = _lower_bound(lbl_ref[...])
        prefix, _ = _chunk_masks(rb)
        c3 = lambda a: a.reshape(nc, CH, HG_D)
        _, q, _, _, f, kk = _hgrn_gates(q_ref[...], f_ref[...], lb)
        b3 = c3(_tri_dot(prefix, jnp.log(f)))
        q3, kk3, v3 = c3(q), c3(kk), c3(i_ref[...])
        bl3 = b3[:, CH - 1:CH, :]
        for t in range(CH):
            slab_ref[:, t * CH:(t + 1) * CH, :] = (q3 * jnp.exp(jnp.minimum(b3 - b3[:, t:t + 1, :], 0.0))).astype(MM)
        r = _bdot(slab_ref[...], kk3, 2, 2)
        row, col = _iota2((nc, CH, CH), 1), _iota2((nc, CH, CH), 2)
        a = jnp.zeros((nc, CH, CH), F32)
        for t in range(CH):
            a = a + jnp.where(col == t, r[:, t * CH:(t + 1) * CH, :], 0.0)
        a = jnp.where(row >= col, a, 0.0)
        x_upd = _bdot(v3, kk3 * jnp.exp(bl3 - b3), 1, 1)
        ebl3 = jnp.exp(bl3)
        st = st_ref[...]
        for c in range(nc):
            states_ref[c] = st
            st = st * ebl3[c] + x_upd[c]
        st_ref[...] = st
        o3 = _bdot(q3 * jnp.exp(b3), states_ref[...], 2, 2) + _bdot(a, v3, 2, 1)
        o_ref[...] = o3.reshape(rb, HG_D)

    sec = lambda j: pl.BlockSpec((None, rb, HG_D), lambda h, i, j=j: (j, i, h))
    return pl.pallas_call(
        body, name="hgrn_fwd", grid=(HG_HEADS, nb),
        in_specs=[sec(0), sec(1), sec(2), pl.BlockSpec((2, HG_D), lambda h, i: (0, h))],
        out_specs=[pl.BlockSpec((rb, HG_D), lambda h, i: (i, h)),
                   pl.BlockSpec((None, None, HG_D, HG_D), lambda h, i: (i, h, 0, 0))],
        out_shape=[jax.ShapeDtypeStruct((s, SEC_W), F32),
                   jax.ShapeDtypeStruct((nb, HG_HEADS, HG_D, HG_D), F32)],
        scratch_shapes=[pltpu.VMEM((HG_D, HG_D), F32), pltpu.VMEM((nc, CH * CH, HG_D), MM),
                        pltpu.VMEM((nc, HG_D, HG_D), F32)],
        compiler_params=_params("parallel", "arbitrary"),
    )(proj, proj, proj, lb_logits)


def _hgrn_bwd(proj, lb_logits, d_o, sst, rb=256):
    s = proj.shape[1]
    nb, nc = s // rb, rb // CH

    def body(q_ref, f_ref, i_ref, lbl_ref, do_ref, sst_ref, dxq_ref, dxf_ref, dxi_ref, dlb_ref,
             dst_ref, states_ref, dstates_ref, lslab_ref, kslab_ref):
        @pl.when(pl.program_id(1) == 0)
        def _():
            dst_ref[...] = jnp.zeros_like(dst_ref)
            dlb_ref[...] = jnp.zeros_like(dlb_ref)

        lb, _ = _lower_bound(lbl_ref[...])
        prefix, suffix = _chunk_masks(rb)
        c3 = lambda a: a.reshape(nc, CH, HG_D)
        flat = lambda a: a.reshape(rb, HG_D)
        xq = q_ref[...]
        sgq, q, sg, sn, f, kk = _hgrn_gates(xq, f_ref[...], lb)
        b3 = c3(_tri_dot(prefix, jnp.log(f)))
        q3, kk3, v3, do3 = c3(q), c3(kk), c3(i_ref[...]), c3(do_ref[...])
        bl3 = b3[:, CH - 1:CH, :]
        eb3, ebl3, dec3 = jnp.exp(b3), jnp.exp(bl3), jnp.exp(bl3 - b3)
        qe3, kd3 = q3 * eb3, kk3 * dec3
        x_upd, y_upd = _bdot(v3, kd3, 1, 1), _bdot(do3, qe3, 1, 1)
        st = sst_ref[...]
        for c in range(nc):
            states_ref[c] = st
            st = st * ebl3[c] + x_upd[c]
        dst = dst_ref[...]
        for c in reversed(range(nc)):
            dstates_ref[c] = dst
            dst = dst * ebl3[c] + y_upd[c]
        dst_ref[...] = dst
        states, dstates = states_ref[...], dstates_ref[...]
        dqe = _bdot(do3, states, 2, 1)
        dkd = _bdot(v3, dstates, 2, 1)
        row, col = _iota2((nc, CH, CH), 1), _iota2((nc, CH, CH), 2)
        tril, triu = row >= col, row <= col
        d_a = jnp.where(tril, _bdot(do3, v3, 2, 2), 0.0)
        d_at = jnp.where(triu, _bdot(v3, do3, 2, 2), 0.0)
        for t in range(CH):
            bt = b3[:, t:t + 1, :]
            lslab_ref[:, t * CH:(t + 1) * CH, :] = (q3 * jnp.exp(jnp.minimum(b3 - bt, 0.0))).astype(MM)
            kslab_ref[:, t * CH:(t + 1) * CH, :] = (kk3 * jnp.exp(jnp.minimum(bt - b3, 0.0))).astype(MM)
        r = _bdot(kslab_ref[...], q3, 2, 2)
        a_t = jnp.zeros((nc, CH, CH), F32)
        for t in range(CH):
            a_t = a_t + jnp.where(col == t, r[:, t * CH:(t + 1) * CH, :], 0.0)
        a_t = jnp.where(triu, a_t, 0.0)
        dv = _bdot(kd3, dstates, 2, 2) + _bdot(a_t, do3, 2, 1)
        sel = (_iota2((CH, CH * CH), 1) % CH == _iota2((CH, CH * CH), 0)).astype(MM)
        blockdiag = _iota2((nc, CH, CH * CH), 2) // CH == _iota2((nc, CH, CH * CH), 1)
        tile = lambda m: jnp.where(blockdiag, _dot(m.reshape(rb, CH), sel).reshape(nc, CH, CH * CH), 0.0)
        dq_in = _bdot(tile(d_a), kslab_ref[...], 2, 1)
        dk_in = _bdot(tile(d_at), lslab_ref[...], 2, 1)
        dkd_kd = dkd * kd3
        db = dqe * qe3 - dkd_kd + q3 * dq_in - kk3 * dk_in
        dbl = jnp.sum(dkd_kd, axis=1, keepdims=True) + jnp.sum(dstates * states, axis=1, keepdims=True) * ebl3
        last = _iota2((nc, CH, HG_D), 1) == CH - 1
        dg = _tri_dot(suffix, flat(db + jnp.where(last, dbl, 0.0)))
        df = dg / f - flat(dkd * dec3 + dk_in)
        dxq_ref[...] = (flat(dqe * eb3 + dq_in) * (sgq * (1.0 + xq * (1.0 - sgq)))).astype(MM)
        dxf_ref[...] = (df * (1.0 - lb) * sg * sn).astype(MM)
        dxi_ref[...] = flat(dv).astype(MM)
        dlb_ref[...] += jnp.sum(df * sn, axis=0, keepdims=True)

    rev = lambda i: nb - 1 - i
    sec = lambda j: pl.BlockSpec((None, rb, HG_D), lambda h, i, j=j: (j, rev(i), h))
    blk = pl.BlockSpec((rb, HG_D), lambda h, i: (rev(i), h))
    return pl.pallas_call(
        body, name="hgrn_bwd", grid=(HG_HEADS, nb),
        in_specs=[sec(0), sec(1), sec(2), pl.BlockSpec((2, HG_D), lambda h, i: (0, h)), blk,
                  pl.BlockSpec((None, None, HG_D, HG_D), lambda h, i: (rev(i), h, 0, 0))],
        out_specs=[blk, blk, blk, pl.BlockSpec((1, HG_D), lambda h, i: (0, h))],
        out_shape=[jax.ShapeDtypeStruct((s, SEC_W), MM)] * 3 + [jax.ShapeDtypeStruct((1, SEC_W), F32)],
        scratch_shapes=[pltpu.VMEM((HG_D, HG_D), F32), pltpu.VMEM((nc, HG_D, HG_D), F32),
                        pltpu.VMEM((nc, HG_D, HG_D), F32),
                        pltpu.VMEM((nc, CH * CH, HG_D), MM), pltpu.VMEM((nc, CH * CH, HG_D), MM)],
        compiler_params=_params("parallel", "arbitrary"),
    )(proj, proj, proj, lb_logits, d_o, sst)


def _rope_tables(s):
    half = AT_DH // 2
    inv_freq = 1.0 / (ROPE_THETA ** (jnp.arange(half, dtype=F32) / half))
    ang = jnp.arange(s, dtype=jnp.int32).astype(F32)[:, None] * inv_freq[None, :]
    cos, sin = jnp.cos(ang), jnp.sin(ang)
    return jnp.concatenate([cos] * 4, axis=-1), jnp.concatenate([-sin, sin] * 2, axis=-1)


def _rope128(x, cos, sin):
    lo = (_iota2(x.shape, 1) % AT_DH) < AT_DH // 2
    rot = jnp.where(lo, pltpu.roll(x, 128 - AT_DH // 2, 1), pltpu.roll(x, AT_DH // 2, 1))
    return x * cos + rot * sin


LANE_GROUPS = SEC_W // 128


def _set_lanes(ref, val):
    for j in range(LANE_GROUPS):
        ref[j] = val[:, j * 128:(j + 1) * 128]


def _get_lanes(ref):
    return jnp.concatenate([ref[j] for j in range(LANE_GROUPS)], axis=-1)


def _to_view(src_ref, dst_ref, d):
    n = src_ref.shape[1] // d
    for r in range(d):
        rows = pl.ds(r, n, stride=d) if d > 1 else slice(None)
        for j in range(LANE_GROUPS):
            c0 = r * SEC_W + j * 128
            dst_ref[:, c0:c0 + 128] = src_ref.at[j][rows, :].astype(dst_ref.dtype)


def _from_view(src_ref, dst_ref, d):
    n = dst_ref.shape[1] // d
    for r in range(d):
        for j in range(LANE_GROUPS):
            c0 = r * SEC_W + j * 128
            dst_ref.at[j][pl.ds(r, n, stride=d), :] = src_ref[:, c0:c0 + 128].astype(dst_ref.dtype)


def _view_spec(tm, d):
    return pl.BlockSpec((tm // d, d * SEC_W), lambda i: (i, 0))


def _view_shape(s, d, dtype):
    return jax.ShapeDtypeStruct((s // d, d * SEC_W), dtype)


def _attn_prep(proj, cos, sin, tm=512):
    s = proj.shape[1]

    def body(q_ref, k_ref, v_ref, cos_ref, sin_ref, *refs):
        outs, (qs_ref, ks_ref, vs_ref) = refs[:-3], refs[-3:]
        c, sn = cos_ref[...], sin_ref[...]
        for j in range(LANE_GROUPS):
            sl = slice(j * 128, (j + 1) * 128)
            qs_ref[j] = _rope128(q_ref[:, sl], c, sn) * (AT_DH ** -0.5)
            ks_ref[j] = _rope128(k_ref[:, sl], c, sn)
            vs_ref[j] = v_ref[:, sl]
        for i, d in enumerate(DILATIONS):
            for src_ref, dst_ref in zip((qs_ref, ks_ref, vs_ref), outs[3 * i:3 * i + 3]):
                _to_view(src_ref, dst_ref, d)

    sec = lambda j: pl.BlockSpec((None, tm, SEC_W), lambda i, j=j: (j, i, 0))
    tab = pl.BlockSpec((tm, 128), lambda i: (i, 0))
    return pl.pallas_call(
        body, name="attn_prep", grid=(s // tm,),
        in_specs=[sec(4), sec(5), sec(6), tab, tab],
        out_specs=[_view_spec(tm, d) for d in DILATIONS for _ in range(3)],
        out_shape=[_view_shape(s, d, MM) for d in DILATIONS for _ in range(3)],
        scratch_shapes=[pltpu.VMEM((LANE_GROUPS, tm, 128), F32)] * 3,
        compiler_params=_params("parallel"),
    )(proj, proj, proj, cos, sin)


def _band_mask(first_ok, second_ok):
    row, col = _iota2((ATT_BLK, 2 * ATT_BLK), 0), _iota2((ATT_BLK, 2 * ATT_BLK), 1)
    return ((col < ATT_BLK) & (col >= row) & first_ok) | ((col >= ATT_BLK) & ((col - ATT_BLK) <= row) & second_ok)


def _own_lanes(rows, h):
    lane = _iota2((rows, 128), 1)
    return (lane < AT_DH) if h == 0 else (lane >= AT_DH)


def _neg_pieces(rows, h):
    lane = _iota2((rows, 128), 1) - (AT_DH if h == 0 else 0)
    return jnp.where((lane >= 0) & (lane < 3), -1.0, 0.0).astype(MM)


def _units():
    return [(b, slice(g * 128, (g + 1) * 128), h) for b in range(AT_QB) for g in range(AT_COLS // 128) for h in range(2)]


def _sub(b):
    return slice(b * ATT_BLK, (b + 1) * ATT_BLK)


def _band_before(cur_ref, prev_ref, b, sl):
    if b == 0:
        return jnp.concatenate([prev_ref[:, sl], cur_ref[0:ATT_BLK, sl]], axis=0)
    return cur_ref[(b - 1) * ATT_BLK:(b + 1) * ATT_BLK, sl]


def _band_after(cur_ref, next_ref, b, sl):
    if b == AT_QB - 1:
        return jnp.concatenate([cur_ref[b * ATT_BLK:(b + 1) * ATT_BLK, sl], next_ref[:, sl]], axis=0)
    return cur_ref[b * ATT_BLK:(b + 2) * ATT_BLK, sl]


def _attn_specs(rows):
    assert rows % (AT_QB * ATT_BLK) == 0
    last = rows // ATT_BLK - 1
    cur = pl.BlockSpec((AT_QB * ATT_BLK, AT_COLS), lambda c, n: (n, c))
    prev = pl.BlockSpec((ATT_BLK, AT_COLS), lambda c, n: (jnp.maximum(AT_QB * n - 1, 0), c))
    nxt = pl.BlockSpec((ATT_BLK, AT_COLS), lambda c, n: (jnp.minimum(AT_QB * (n + 1), last), c))
    return cur, prev, nxt


def _attn_fwd(qr, kr, vr, d):
    rows, cols = qr.shape
    nb = rows // (AT_QB * ATT_BLK)

    def body(q_ref, kc_ref, kp_ref, vc_ref, vp_ref, o_ref, lse_ref):
        masks = {True: _band_mask(pl.program_id(1) > 0, True), False: _band_mask(True, True)}
        ones = jnp.ones((2 * ATT_BLK, 128), MM)
        head0 = _own_lanes(ATT_BLK, 0)
        units = _units()
        scs = []
        for b, sl, h in units:
            q2 = q_ref[_sub(b), sl]
            qh = jnp.where(_own_lanes(ATT_BLK, h), q2, jnp.zeros_like(q2))
            scs.append(jnp.where(masks[b == 0], _dot_nt(qh, _band_before(kc_ref, kp_ref, b, sl)), NEG))
        ms = [jnp.max(sc, axis=-1, keepdims=True) for sc in scs]
        ps = [jnp.exp(sc - m).astype(MM) for sc, m in zip(scs, ms)]
        ls = [jnp.dot(p, ones, preferred_element_type=F32) for p in ps]
        os_ = [jnp.dot(p, _band_before(vc_ref, vp_ref, b, sl), preferred_element_type=F32)
               for p, (b, sl, _) in zip(ps, units)]
        for i in range(0, len(units), 2):
            b, sl, _ = units[i]
            l = jnp.where(head0, ls[i], ls[i + 1])
            o_ref[_sub(b), sl] = jnp.where(head0, os_[i], os_[i + 1]) / l
            lse_ref[_sub(b), sl] = jnp.where(head0, ms[i], ms[i + 1]) + jnp.log(l)

    cur, prev, _ = _attn_specs(rows)
    o, lse = pl.pallas_call(
        body, name=f"attn_fwd_d{d}", grid=(cols // AT_COLS, nb),
        in_specs=[cur, cur, prev, cur, prev], out_specs=[cur, cur],
        out_shape=[jax.ShapeDtypeStruct((rows, cols), F32)] * 2,
        compiler_params=_params("parallel", "parallel"),
    )(qr, kr, kr, vr, vr)
    return o, lse


def _attn_bwd_dq(qr, kr, vr, do, lse, delta, d):
    rows, cols = qr.shape
    nb = rows // (AT_QB * ATT_BLK)

    def body(q_ref, kc_ref, kp_ref, vc_ref, vp_ref, do_ref, lse_ref, dl_ref, dq_ref):
        masks = {True: _band_mask(pl.program_id(1) > 0, True), False: _band_mask(True, True)}
        units = _units()
        sms, dps = [], []
        for b, sl, h in units:
            own, own_b, neg = _own_lanes(ATT_BLK, h), _own_lanes(2 * ATT_BLK, h), _neg_pieces(2 * ATT_BLK, h)
            sms.append(_dot_nt(jnp.where(own, q_ref[_sub(b), sl], lse_ref[_sub(b), sl]),
                               jnp.where(own_b, _band_before(kc_ref, kp_ref, b, sl), neg)))
            dps.append(_dot_nt(jnp.where(own, do_ref[_sub(b), sl], dl_ref[_sub(b), sl]),
                               jnp.where(own_b, _band_before(vc_ref, vp_ref, b, sl), neg)))
        dss = [(jnp.exp(jnp.where(masks[b == 0], sm, NEG)) * dp).astype(MM)
               for sm, dp, (b, _, _) in zip(sms, dps, units)]
        dqs = [jnp.dot(ds, _band_before(kc_ref, kp_ref, b, sl), preferred_element_type=F32) * (AT_DH ** -0.5)
               for ds, (b, sl, _) in zip(dss, units)]
        for i in range(0, len(units), 2):
            b, sl, _ = units[i]
            dq_ref[_sub(b), sl] = jnp.where(_own_lanes(ATT_BLK, 0), dqs[i], dqs[i + 1]).astype(dq_ref.dtype)

    cur, prev, _ = _attn_specs(rows)
    dq = pl.pallas_call(
        body, name=f"attn_bwd_dq_d{d}", grid=(cols // AT_COLS, nb),
        in_specs=[cur, cur, prev, cur, prev, cur, cur, cur], out_specs=cur,
        out_shape=jax.ShapeDtypeStruct((rows, cols), MM),
        compiler_params=_params("parallel", "parallel"),
    )(qr, kr, kr, vr, vr, do, lse, delta)
    return dq


def _attn_bwd_dkv(qr, kr, vr, do, lse, delta, d):
    rows, cols = qr.shape
    nb = rows // (AT_QB * ATT_BLK)

    def body(k_ref, v_ref, qc_ref, qn_ref, doc_ref, don_ref, lsec_ref, lsen_ref, dlc_ref, dln_ref,
             dk_ref, dv_ref):
        masks = {True: _band_mask(True, pl.program_id(1) < nb - 1), False: _band_mask(True, True)}
        units = _units()
        sms, dps = [], []
        for b, sl, h in units:
            own, own_b, neg = _own_lanes(ATT_BLK, h), _own_lanes(2 * ATT_BLK, h), _neg_pieces(ATT_BLK, h)
            sms.append(_dot_nt(jnp.where(own, k_ref[_sub(b), sl], neg),
                               jnp.where(own_b, _band_after(qc_ref, qn_ref, b, sl),
                                         _band_after(lsec_ref, lsen_ref, b, sl))))
            dps.append(_dot_nt(jnp.where(own, v_ref[_sub(b), sl], neg),
                               jnp.where(own_b, _band_after(doc_ref, don_ref, b, sl),
                                         _band_after(dlc_ref, dln_ref, b, sl))))
        ps = [jnp.exp(jnp.where(masks[b == AT_QB - 1], sm, NEG)) for sm, (b, _, _) in zip(sms, units)]
        dss = [(p * dp).astype(MM) for p, dp in zip(ps, dps)]
        dvs = [jnp.dot(p.astype(MM), _band_after(doc_ref, don_ref, b, sl), preferred_element_type=F32)
               for p, (b, sl, _) in zip(ps, units)]
        dks = [jnp.dot(ds, _band_after(qc_ref, qn_ref, b, sl), preferred_element_type=F32)
               for ds, (b, sl, _) in zip(dss, units)]
        head0 = _own_lanes(ATT_BLK, 0)
        for i in range(0, len(units), 2):
            b, sl, _ = units[i]
            dk_ref[_sub(b), sl] = jnp.where(head0, dks[i], dks[i + 1]).astype(dk_ref.dtype)
            dv_ref[_sub(b), sl] = jnp.where(head0, dvs[i], dvs[i + 1]).astype(dv_ref.dtype)

    cur, _, nxt = _attn_specs(rows)
    dk, dv = pl.pallas_call(
        body, name=f"attn_bwd_dkv_d{d}", grid=(cols // AT_COLS, nb),
        in_specs=[cur, cur, cur, nxt, cur, nxt, cur, nxt, cur, nxt], out_specs=[cur, cur],
        out_shape=[jax.ShapeDtypeStruct((rows, cols), MM)] * 2,
        compiler_params=_params("parallel", "parallel"),
    )(kr, vr, qr, qr, do, do, lse, lse, delta, delta)
    return dk, dv


def _head_sum(a, width):
    parts = []
    for j in range(a.shape[1] // width):
        sm = jnp.sum(a[:, j * width:(j + 1) * width], axis=-1, keepdims=True)
        parts.append(jnp.broadcast_to(sm, (a.shape[0], width)))
    return jnp.concatenate(parts, axis=-1)


def _partner_pieces(x):
    xs = jnp.concatenate([pltpu.roll(x[:, j * 128:(j + 1) * 128], AT_DH, 1) for j in range(x.shape[1] // 128)],
                         axis=-1)
    hi = xs.astype(jnp.bfloat16).astype(F32)
    mid = (xs - hi).astype(jnp.bfloat16).astype(F32)
    lo = (xs - hi - mid).astype(jnp.bfloat16).astype(F32)
    lane = _iota2(x.shape, 1) % AT_DH
    return jnp.where(lane == 0, hi, jnp.where(lane == 1, mid, jnp.where(lane == 2, lo, 0.0)))


def _mid(x, tgt, proj, o_hg, o_at, lse_at, hg_norm_w, final_norm_w, wo_all, tm=256):
    s = x.shape[0]
    nb = s // tm

    def body(x_ref, t_ref, hgz_ref, atz_ref, ohg_ref, o1_ref, o2_ref, o3_ref, l1_ref, l2_ref, l3_ref,
             g_ref, fw_ref, wo_ref,
             dh_ref, dohg_ref, dhgz_ref, datz_ref, do1_ref, do2_ref, do3_ref, dl1_ref, dl2_ref, dl3_ref,
             lp1_ref, lp2_ref, lp3_ref,
             gwo_ref, gfw_ref, ghg_ref, loss_ref, nat_ref, stage_ref, gwo_acc):
        @pl.when(pl.program_id(0) == 0)
        def _():
            gwo_acc[...] = jnp.zeros_like(gwo_acc)
            gfw_ref[...] = jnp.zeros_like(gfw_ref)
            ghg_ref[...] = jnp.zeros_like(ghg_ref)
            loss_ref[...] = jnp.zeros_like(loss_ref)

        ohg, g = ohg_ref[...], g_ref[...]
        rs = lax.rsqrt(_head_sum(ohg * ohg, HG_D) * (1.0 / HG_D) + NORM_EPS)
        on = ohg * rs
        hgz = hgz_ref[...]
        sz = _sigmoid(hgz)
        gate_hg = hgz * sz
        lses, outs = [l1_ref[...]], [o1_ref[...]]
        for k, (d, l_ref, o_ref) in enumerate(zip(DILATIONS[1:], (l2_ref, l3_ref), (o2_ref, o3_ref))):
            _from_view(l_ref, nat_ref.at[2 * k], d)
            _from_view(o_ref, nat_ref.at[2 * k + 1], d)
            lses.append(_get_lanes(nat_ref.at[2 * k]))
            outs.append(_get_lanes(nat_ref.at[2 * k + 1]))
        mx = jnp.maximum(jnp.maximum(lses[0], lses[1]), lses[2])
        es = [jnp.exp(l - mx) for l in lses]
        den = es[0] + es[1] + es[2]
        ws = [e / den for e in es]
        oat = ws[0] * outs[0] + ws[1] * outs[1] + ws[2] * outs[2]
        atz = atz_ref[...]
        sa = _sigmoid(atz)
        gate_at = atz * sa
        mixed = jnp.concatenate([on * g * gate_hg, oat * gate_at], axis=-1).astype(MM)
        h = x_ref[...] + jnp.dot(mixed, wo_ref[...], preferred_element_type=F32)
        rstd = lax.rsqrt(jnp.mean(h * h, axis=-1, keepdims=True) + NORM_EPS)
        hn = h * rstd
        fw = fw_ref[...]
        err = hn * fw - t_ref[...]
        loss_ref[...] += 0.5 * jnp.sum(jnp.mean(err * err, axis=-1, keepdims=True), axis=0, keepdims=True)
        dout = err * (1.0 / D_MODEL)
        gfw_ref[...] += jnp.sum(dout * hn, axis=0, keepdims=True)
        dhn = dout * fw
        dh = rstd * (dhn - hn * jnp.mean(dhn * hn, axis=-1, keepdims=True))
        dh_ref[...] = dh
        dh_mm = dh.astype(MM)
        gwo_acc[...] += _dot_tn(mixed, dh_mm)

        @pl.when(pl.program_id(0) == nb - 1)
        def _():
            gwo_ref[...] = gwo_acc[...].astype(gwo_ref.dtype)

        dmixed = _dot_nt(dh_mm, wo_ref[...])
        dm_hg = dmixed[:, :SEC_W]
        d_ong = dm_hg * gate_hg
        dhgz_ref[...] = (dm_hg * (on * g) * (sz * (1.0 + hgz * (1.0 - sz)))).astype(MM)
        ghg_ref[...] += jnp.sum(d_ong * on, axis=0, keepdims=True)
        d_on = d_ong * g
        dohg_ref[...] = rs * (d_on - on * (_head_sum(d_on * on, HG_D) * (1.0 / HG_D)))
        dm_at = dmixed[:, SEC_W:]
        d_oat = dm_at * gate_at
        datz_ref[...] = (dm_at * oat * (sa * (1.0 + atz * (1.0 - sa)))).astype(MM)
        drow = _head_sum(d_oat * oat, AT_DH)
        lse_all = mx + jnp.log(den)
        for val, dst_refs in ((d_oat, (do1_ref, do2_ref, do3_ref)),
                              (_partner_pieces(drow), (dl1_ref, dl2_ref, dl3_ref)),
                              (_partner_pieces(lse_all), (lp1_ref, lp2_ref, lp3_ref))):
            _set_lanes(stage_ref, val)
            for d, dst_ref in zip(DILATIONS, dst_refs):
                _to_view(stage_ref, dst_ref, d)

    row = lambda w: pl.BlockSpec((tm, w), lambda i: (i, 0))
    sec = lambda j: pl.BlockSpec((None, tm, SEC_W), lambda i, j=j: (j, i, 0))
    const = lambda shp: pl.BlockSpec(shp, lambda i: (0,) * len(shp))
    half = row(SEC_W)
    views = [_view_spec(tm, d) for d in DILATIONS]
    return pl.pallas_call(
        body, name="mid", grid=(nb,),
        in_specs=[row(D_MODEL), row(D_MODEL), sec(3), sec(7), half] + views * 2
                 + [const((1, SEC_W)), const((1, D_MODEL)), const((D_MODEL, D_MODEL))],
        out_specs=[row(D_MODEL)] + [half] * 3 + views * 3
                  + [const((D_MODEL, D_MODEL)), const((1, D_MODEL)), const((1, SEC_W)), const((1, 1))],
        out_shape=[jax.ShapeDtypeStruct((s, D_MODEL), F32), jax.ShapeDtypeStruct((s, SEC_W), F32)]
                  + [jax.ShapeDtypeStruct((s, SEC_W), MM)] * 2
                  + [_view_shape(s, d, MM) for d in DILATIONS] * 3
                  + [jax.ShapeDtypeStruct((D_MODEL, D_MODEL), XCH), jax.ShapeDtypeStruct((1, D_MODEL), F32),
                     jax.ShapeDtypeStruct((1, SEC_W), F32), jax.ShapeDtypeStruct((1, 1), F32)],
        scratch_shapes=[pltpu.VMEM((4, LANE_GROUPS, tm, 128), F32), pltpu.VMEM((LANE_GROUPS, tm, 128), F32),
                        pltpu.VMEM((D_MODEL, D_MODEL), F32)],
        compiler_params=_params("arbitrary"),
    )(x, tgt, proj, proj, o_hg, *o_at, *lse_at, hg_norm_w, final_norm_w, wo_all)


def _inproj_bwd_x(x, norm_w, w_all, dh, dsec, dq_r, dk_r, dv, cos, sin, tm=256):
    s = x.shape[0]

    def body(x_ref, nw_ref, w_ref, dh_ref, s0, s1, s2, s3, s7, q1, q2, q3, k1, k2, k3, v1, v2, v3,
             cos_ref, sin_ref, gx_ref, dp_ref, gnw_ref, nat_ref):
        @pl.when(pl.program_id(0) == 0)
        def _():
            gnw_ref[...] = jnp.zeros_like(gnw_ref)

        def total(refs):
            acc = refs[0][...].astype(F32)
            for d, ref in zip(DILATIONS[1:], refs[1:]):
                _from_view(ref, nat_ref, d)
                acc = acc + _get_lanes(nat_ref)
            return acc

        c, sn = cos_ref[...], -sin_ref[...]
        dq, dk = total((q1, q2, q3)), total((k1, k2, k3))
        unrot = lambda a: jnp.concatenate(
            [_rope128(a[:, j * 128:(j + 1) * 128], c, sn) for j in range(SEC_W // 128)], axis=-1)
        att = [a.astype(MM) for a in (unrot(dq), unrot(dk), total((v1, v2, v3)))]
        for j, a in enumerate(att):
            dp_ref[j] = a
        secs = (s0[...], s1[...], s2[...], s3[...], *att, s7[...])
        du = jnp.zeros((tm, D_MODEL), F32)
        for j, dsj in enumerate(secs):
            du = du + _dot_nt(dsj, w_ref[j])
        xv, nw = x_ref[...], nw_ref[...]
        rstd = lax.rsqrt(jnp.mean(xv * xv, axis=-1, keepdims=True) + NORM_EPS)
        xn = xv * rstd
        gnw_ref[...] += jnp.sum(du * xn, axis=0, keepdims=True)
        dxn = du * nw
        gx_ref[...] = dh_ref[...] + rstd * (dxn - xn * jnp.mean(dxn * xn, axis=-1, keepdims=True))

    row = lambda w: pl.BlockSpec((tm, w), lambda i: (i, 0))
    const = lambda shp: pl.BlockSpec(shp, lambda i: (0,) * len(shp))
    return pl.pallas_call(
        body, name="inproj_bwd_x", grid=(s // tm,),
        in_specs=[row(D_MODEL), const((1, D_MODEL)), const((N_SEC, D_MODEL, SEC_W)), row(D_MODEL)]
                 + [row(SEC_W)] * 5 + [_view_spec(tm, d) for d in DILATIONS] * 3 + [row(128), row(128)],
        out_specs=[row(D_MODEL), pl.BlockSpec((3, tm, SEC_W), lambda i: (0, i, 0)), const((1, D_MODEL))],
        out_shape=[jax.ShapeDtypeStruct((s, D_MODEL), F32), jax.ShapeDtypeStruct((3, s, SEC_W), MM),
                   jax.ShapeDtypeStruct((1, D_MODEL), F32)],
        scratch_shapes=[pltpu.VMEM((LANE_GROUPS, tm, 128), F32)],
        compiler_params=_params("arbitrary"),
    )(x, norm_w, w_all, dh, *dsec, *dq_r, *dk_r, *dv, cos, sin)


def _inproj_bwd_w(x, norm_w, dsecs, tm=1024):
    s = x.shape[0]
    nb = s // tm

    def body(x_ref, nw_ref, *refs):
        dp_refs, (gw_hbm, acc_ref, stage_ref) = refs[:N_SEC], refs[N_SEC:]

        @pl.when(pl.program_id(0) == 0)
        def _():
            acc_ref[...] = jnp.zeros_like(acc_ref)

        xv = x_ref[...]
        rstd = lax.rsqrt(jnp.mean(xv * xv, axis=-1, keepdims=True) + NORM_EPS)
        u_t = (xv * rstd * nw_ref[...]).T.astype(MM)
        for j in range(N_SEC):
            acc_ref[j] += jnp.dot(u_t, dp_refs[j][...], preferred_element_type=F32)

        @pl.when(pl.program_id(0) == nb - 1)
        def _():
            for j in range(N_SEC):
                stage_ref[...] = acc_ref[j].astype(stage_ref.dtype)
                pltpu.sync_copy(stage_ref, gw_hbm.at[j])

    return pl.pallas_call(
        body, name="inproj_bwd_w", grid=(nb,),
        in_specs=[pl.BlockSpec((tm, D_MODEL), lambda i: (i, 0)), pl.BlockSpec((1, D_MODEL), lambda i: (0, 0))]
                 + [pl.BlockSpec((tm, SEC_W), lambda i: (i, 0))] * N_SEC,
        out_specs=pl.BlockSpec(memory_space=pl.ANY),
        out_shape=jax.ShapeDtypeStruct((N_SEC, D_MODEL, SEC_W), XCH),
        scratch_shapes=[pltpu.VMEM((N_SEC, D_MODEL, SEC_W), F32), pltpu.VMEM((D_MODEL, SEC_W), XCH)],
        compiler_params=_params("arbitrary"),
    )(x, norm_w, *dsecs)


def _local_step(x, tgt, norm_w, w_all, lb_logits, hg_norm_w, wo_all, final_norm_w):
    s = x.shape[0]
    cos, sin = _rope_tables(s)
    proj = _inproj_fwd(x, norm_w, w_all)
    o_hg, sst = _hgrn_fwd(proj, lb_logits)
    qkv = _attn_prep(proj, cos, sin)
    qkv = [qkv[3 * i:3 * i + 3] for i in range(len(DILATIONS))]
    att = [_attn_fwd(*qkv_d, d) for qkv_d, d in zip(qkv, DILATIONS)]
    (dh, d_ohg, d_hgz, d_atz, do1, do2, do3, dl1, dl2, dl3, lp1, lp2, lp3, gwo, gfw, ghg, loss) = _mid(
        x, tgt, proj, o_hg, [a[0] for a in att], [a[1] for a in att], hg_norm_w, final_norm_w[None, :], wo_all)
    dxq, dxf, dxi, dlb = _hgrn_bwd(proj, lb_logits, d_ohg, sst)
    dq_r, dk_r, dv = [], [], []
    for d, qkv_d, do, lp, dl in zip(DILATIONS, qkv, (do1, do2, do3), (lp1, lp2, lp3), (dl1, dl2, dl3)):
        dq_r.append(_attn_bwd_dq(*qkv_d, do, lp, dl, d))
        dk_d, dv_d = _attn_bwd_dkv(*qkv_d, do, lp, dl, d)
        dk_r.append(dk_d)
        dv.append(dv_d)
    gx, d_att, gnw = _inproj_bwd_x(x, norm_w, w_all, dh, (dxq, dxf, dxi, d_hgz, d_atz), dq_r, dk_r, dv, cos, sin)
    gwi = _inproj_bwd_w(x, norm_w, (dxq, dxf, dxi, d_hgz, d_att[0], d_att[1], d_att[2], d_atz))
    small = jnp.concatenate([gnw, jnp.concatenate([dlb, ghg], axis=-1), gfw,
                             jnp.pad(loss, ((0, 0), (0, D_MODEL - 1)))], axis=0)
    return gx, gwi, gwo, small


def _coords():
    return lax.axis_index("x"), lax.axis_index("y"), lax.axis_index("c")


def _gather_weights(w_in, w_out):
    wo_rows = w_out.shape[0]

    def body(wi_ref, wo_ref, wi_all, wo_all, send_sems, recv_sems):
        x, y, c = _coords()
        me, sibling = (x, y, c), (x, y, 1 - c)
        chips = [(1 - x, y), (x, 1 - y), (1 - x, 1 - y)]
        slot = lambda p: 4 * p[0] + 2 * p[1] + p[2]

        def copies(k, block, to):
            return [pltpu.make_async_remote_copy(
                src_ref=ref.at[slot(block)], dst_ref=ref.at[slot(block)], send_sem=send_sems.at[a, k],
                recv_sem=recv_sems.at[a, k], device_id=to, device_id_type=MESH)
                for a, ref in enumerate((wi_all, wo_all))]

        wi_all[slot(me)] = wi_ref[...].astype(MM)
        wo_all[slot(me)] = wo_ref[...].astype(MM)
        first = copies(0, me, sibling)
        for j, chip in enumerate(chips):
            first += copies(1 + j, me, (*chip, c))
        for cp in first:
            cp.start()
        passed = []
        for j, chip in enumerate(chips):
            for cp in copies(1 + j, (*chip, c), me):
                cp.wait_recv()
            fwd = copies(4 + j, (*chip, c), sibling)
            for cp in fwd:
                cp.start()
            passed += fwd
        for cp in copies(0, sibling, me):
            cp.wait_recv()
        for j, chip in enumerate(chips):
            for cp in copies(4 + j, (*chip, 1 - c), me):
                cp.wait_recv()
        for cp in first + passed:
            cp.wait_send()

    vmem = pl.BlockSpec(memory_space=pltpu.VMEM)
    return pl.pallas_call(
        body, name="gather_weights",
        in_specs=[vmem, vmem], out_specs=[vmem, vmem],
        out_shape=[jax.ShapeDtypeStruct((N_DEV, D_MODEL, SEC_W), MM),
                   jax.ShapeDtypeStruct((N_DEV, wo_rows, D_MODEL), MM)],
        scratch_shapes=[pltpu.SemaphoreType.DMA((2, 7)), pltpu.SemaphoreType.DMA((2, 7))],
        compiler_params=pltpu.CompilerParams(vmem_limit_bytes=VMEM_LIMIT),
    )(w_in, w_out)


def _exchange_grads(gwi, gwo, small):
    def body(gwi_ref, gwo_ref, sm_ref, li_ref, lo_ref, ls_ref, send_sems, recv_sems, local_sems):
        x, y, c = _coords()
        me = 4 * x + 2 * y + c
        refs = ((gwi_ref, li_ref), (gwo_ref, lo_ref))
        own = [pltpu.make_async_copy(src.at[me], dst.at[me], local_sems.at[a]) for a, (src, dst) in enumerate(refs)]
        own.append(pltpu.make_async_copy(sm_ref, ls_ref.at[me], local_sems.at[2]))
        for cp in own:
            cp.start()
        sends = []
        for k in range(1, N_DEV):
            px, py, pc = x ^ (k >> 2), y ^ ((k >> 1) & 1), c ^ (k & 1)
            peer = 4 * px + 2 * py + pc
            for a, (src, dst) in enumerate(refs):
                sends.append(pltpu.make_async_remote_copy(
                    src_ref=src.at[peer], dst_ref=dst.at[me], send_sem=send_sems.at[a, k - 1],
                    recv_sem=recv_sems.at[a, k - 1], device_id=(px, py, pc), device_id_type=MESH))
            sends.append(pltpu.make_async_remote_copy(
                src_ref=sm_ref, dst_ref=ls_ref.at[me], send_sem=send_sems.at[2, k - 1],
                recv_sem=recv_sems.at[2, k - 1], device_id=(px, py, pc), device_id_type=MESH))
        for cp in sends:
            cp.start()
        for cp in sends:
            cp.wait_recv()
        for cp in sends:
            cp.wait_send()
        for cp in own:
            cp.wait()

    hbm = pl.BlockSpec(memory_space=pl.ANY)
    return pl.pallas_call(
        body, name="exchange_grads",
        in_specs=[hbm, hbm, hbm], out_specs=[hbm, hbm, hbm],
        out_shape=[jax.ShapeDtypeStruct(gwi.shape, gwi.dtype), jax.ShapeDtypeStruct(gwo.shape, gwo.dtype),
                   jax.ShapeDtypeStruct((N_DEV,) + small.shape, F32)],
        scratch_shapes=[pltpu.SemaphoreType.DMA((3, 7)), pltpu.SemaphoreType.DMA((3, 7)),
                        pltpu.SemaphoreType.DMA((3,))],
    )(gwi, gwo, small)


def _adamw(w, g, m, v):
    m = ADAM_B1 * m + (1.0 - ADAM_B1) * g
    v = ADAM_B2 * v + (1.0 - ADAM_B2) * (g * g)
    m_hat = m / (1.0 - ADAM_B1 ** ADAM_STEP)
    v_hat = v / (1.0 - ADAM_B2 ** ADAM_STEP)
    return -ADAM_LR * (m_hat / (jnp.sqrt(v_hat) + ADAM_EPS) + ADAM_WD * w), m, v


def _slot_sum(ref):
    g = ref[0].astype(F32)
    for i in range(1, N_DEV):
        g = g + ref[i].astype(F32)
    return g


def _update_matrix(name, landed, w, m, v, rows):
    r, c = w.shape

    def body(l_ref, w_ref, m_ref, v_ref, g_ref, d_ref, nm_ref, nv_ref):
        g = _slot_sum(l_ref)
        g_ref[...] = g
        d_ref[...], nm_ref[...], nv_ref[...] = _adamw(w_ref[...], g, m_ref[...], v_ref[...])

    blk = pl.BlockSpec((rows, c), lambda i: (i, 0))
    return pl.pallas_call(
        body, name=name, grid=(r // rows,),
        in_specs=[pl.BlockSpec((N_DEV, rows, c), lambda i: (0, i, 0)), blk, blk, blk],
        out_specs=[blk] * 4, out_shape=[jax.ShapeDtypeStruct((r, c), F32)] * 4,
        compiler_params=_params("parallel"),
    )(landed, w, m, v)


def _update_small(landed, lb_logits, ws, ms, vs):
    def body(l_ref, lbl_ref, w_ref, m_ref, v_ref, g_ref, d_ref, nm_ref, nv_ref, loss_ref):
        tot = _slot_sum(l_ref)
        _, dlb = _lower_bound(lbl_ref[...])
        g_lb = tot[1:2, :SEC_W] * dlb
        g = jnp.concatenate([tot[0:1], jnp.concatenate([g_lb, -g_lb], axis=-1),
                             jnp.pad(tot[1:2, SEC_W:], ((0, 0), (0, SEC_W))), tot[2:3]], axis=0)
        g_ref[...] = g
        d_ref[...], nm_ref[...], nv_ref[...] = _adamw(w_ref[...], g, m_ref[...], v_ref[...])
        loss_ref[...] = tot[3:4, 0:1]

    vmem = pl.BlockSpec(memory_space=pltpu.VMEM)
    return pl.pallas_call(
        body, name="update_small", in_specs=[vmem] * 5, out_specs=[vmem] * 5,
        out_shape=[jax.ShapeDtypeStruct((4, D_MODEL), F32)] * 4 + [jax.ShapeDtypeStruct((1, 1), F32)],
    )(landed, lb_logits, ws, ms, vs)


def _pack_small(norm_w, lb_logits, hg_norm_w, final_norm_w):
    return jnp.concatenate([norm_w, lb_logits.reshape(1, D_MODEL),
                            jnp.pad(hg_norm_w, ((0, 0), (0, D_MODEL - SEC_W))), final_norm_w[None, :]], axis=0)


def _unpack_small(a):
    return a[0:1], a[1].reshape(2, SEC_W), a[2:3, :SEC_W], a[3]


def kernel(x, norm_w, w_in, hgrn_lb_logits, hg_norm_w, w_out, final_norm_w, loss_target, m_norm_w, m_w_in, m_hgrn_lb_logits, m_hg_norm_w, m_w_out, m_final_norm_w, v_norm_w, v_w_in, v_hgrn_lb_logits, v_hg_norm_w, v_w_out, v_final_norm_w):
    w_all, wo_all = _gather_weights(w_in[0], w_out[0])
    gx, gwi, gwo, small = _local_step(x[0], loss_target[0], norm_w, w_all, hgrn_lb_logits, hg_norm_w,
                                      wo_all.reshape(D_MODEL, D_MODEL), final_norm_w)
    li, lo, ls = _exchange_grads(gwi, gwo.reshape(N_DEV, D_MODEL // N_DEV, D_MODEL), small)
    g_wi, d_wi, nm_wi, nv_wi = _update_matrix("update_w_in", li, w_in[0], m_w_in[0], v_w_in[0], 256)
    g_wo, d_wo, nm_wo, nv_wo = _update_matrix("update_w_out", lo, w_out[0], m_w_out[0], v_w_out[0], 128)
    g_s, d_s, nm_s, nv_s, loss = _update_small(
        ls, hgrn_lb_logits, _pack_small(norm_w, hgrn_lb_logits, hg_norm_w, final_norm_w),
        _pack_small(m_norm_w, m_hgrn_lb_logits, m_hg_norm_w, m_final_norm_w),
        _pack_small(v_norm_w, v_hgrn_lb_logits, v_hg_norm_w, v_final_norm_w))
    outs = []
    for small_out, wi, wo in ((g_s, g_wi, g_wo), (d_s, d_wi, d_wo), (nm_s, nm_wi, nm_wo), (nv_s, nv_wi, nv_wo)):
        nw, lb, hg, fw = _unpack_small(small_out)
        outs += [nw, wi[None], lb, hg, wo[None], fw]
    return (loss[0, 0], gx[None], *outs)
```

```python
import functools

import jax
import jax.numpy as jnp
from jax import lax
from jax.experimental import pallas as pl
from jax.experimental.pallas import tpu as pltpu

F32 = jnp.float32
MM = jnp.bfloat16
XCH = jnp.bfloat16
NORM_EPS = 1e-6
NEG = -1e30
N_DEV = 8
D_MODEL = 1024
N_SEC = 8
SEC_W = 512
HG_HEADS = 4
HG_D = 128
AT_HEADS = 8
AT_DH = 64
ATT_BLK = 128
AT_COLS = 512
AT_QB = 4
DILATIONS = (1, 4, 16)
ROPE_THETA = 10000.0
CH = 16
LB_LO, LB_HI = 1e-6, 1.0 - 1e-6
ADAM_LR, ADAM_B1, ADAM_B2, ADAM_EPS, ADAM_WD, ADAM_STEP = 0.001, 0.9, 0.999, 1e-08, 0.01, 10
VMEM_LIMIT = 56 * 1024 * 1024
MESH = pl.DeviceIdType.MESH


def _params(*sem):
    return pltpu.CompilerParams(dimension_semantics=sem, vmem_limit_bytes=VMEM_LIMIT)


def _sigmoid(x):
    return 1.0 / (1.0 + jnp.exp(-x))


def _dot(a, b):
    return jnp.dot(a.astype(MM), b.astype(MM), preferred_element_type=F32)


def _dot_nt(a, b):
    return lax.dot_general(a.astype(MM), b.astype(MM), (((1,), (1,)), ((), ())), preferred_element_type=F32)


def _dot_tn(a, b):
    return lax.dot_general(a.astype(MM), b.astype(MM), (((0,), (0,)), ((), ())), preferred_element_type=F32)


def _tri_dot(tri, g):
    g1 = g.astype(jnp.bfloat16)
    r1 = g - g1.astype(F32)
    g2 = r1.astype(jnp.bfloat16)
    g3 = (r1 - g2.astype(F32)).astype(jnp.bfloat16)
    t = tri.astype(jnp.bfloat16)
    d = functools.partial(jnp.dot, preferred_element_type=F32)
    return d(t, g1) + d(t, g2) + d(t, g3)


def _lower_bound(lbl):
    l0, l1 = lbl[0:1, :], lbl[1:2, :]
    m = jnp.maximum(l0, l1)
    e0, e1 = jnp.exp(l0 - m), jnp.exp(l1 - m)
    p = e0 / (e0 + e1)
    inside = (p >= LB_LO) & (p <= LB_HI)
    return jnp.clip(p, LB_LO, LB_HI), jnp.where(inside, p * (e1 / (e0 + e1)), 0.0)


def _iota2(shape, dim):
    return lax.broadcasted_iota(jnp.int32, shape, dim)


def _inproj_fwd(x, norm_w, w_all, tm=256):
    s = x.shape[0]

    def body(x_ref, nw_ref, w_ref, proj_ref):
        xv = x_ref[...]
        rstd = lax.rsqrt(jnp.mean(xv * xv, axis=-1, keepdims=True) + NORM_EPS)
        u = (xv * rstd * nw_ref[...]).astype(MM)
        for j in range(N_SEC):
            proj_ref[j] = jnp.dot(u, w_ref[j], preferred_element_type=F32)

    return pl.pallas_call(
        body, name="inproj_fwd", grid=(s // tm,),
        in_specs=[pl.BlockSpec((tm, D_MODEL), lambda i: (i, 0)),
                  pl.BlockSpec((1, D_MODEL), lambda i: (0, 0)),
                  pl.BlockSpec((N_SEC, D_MODEL, SEC_W), lambda i: (0, 0, 0))],
        out_specs=pl.BlockSpec((N_SEC, tm, SEC_W), lambda i: (0, i, 0)),
        out_shape=jax.ShapeDtypeStruct((N_SEC, s, SEC_W), F32),
        compiler_params=_params("parallel"),
    )(x, norm_w, w_all)


def _hgrn_gates(xq, xf, lb):
    sgq = _sigmoid(xq)
    sg = _sigmoid(xf)
    sn = _sigmoid(-xf)
    f = lb + (1.0 - lb) * sg
    return sgq, xq * sgq, sg, sn, f, (1.0 - lb) * sn


def _bdot(a, b, ca, cb):
    return lax.dot_general(a.astype(MM), b.astype(MM), (((ca,), (cb,)), ((0,), (0,))), preferred_element_type=F32)


def _chunk_masks(rb):
    row, col = _iota2((rb, rb), 0), _iota2((rb, rb), 1)
    same = (row // CH) == (col // CH)
    return same & (row >= col), same & (row <= col)


def _hgrn_fwd(proj, lb_logits, rb=256):
    s = proj.shape[1]
    nb, nc = s // rb, rb // CH

    def body(q_ref, f_ref, i_ref, lbl_ref, o_ref, sst_ref, st_ref, slab_ref, states_ref):
        @pl.when(pl.program_id(1) == 0)
        def _():
            st_ref[...] = jnp.zeros_like(st_ref)

        sst_ref[...] = st_ref[...]
        lb, _ = _lower_bound(lbl_ref[...])
        prefix, _ = _chunk_masks(rb)
        c3 = lambda a: a.reshape(nc, CH, HG_D)
        _, q, _, _, f, kk = _hgrn_gates(q_ref[...], f_ref[...], lb)
        b3 = c3(_tri_dot(prefix, jnp.log(f)))
        q3, kk3, v3 = c3(q), c3(kk), c3(i_ref[...])
        bl3 = b3[:, CH - 1:CH, :]
        for t in range(CH):
            slab_ref[:, t * CH:(t + 1) * CH, :] = (q3 * jnp.exp(jnp.minimum(b3 - b3[:, t:t + 1, :], 0.0))).astype(MM)
        r = _bdot(slab_ref[...], kk3, 2, 2)
        row, col = _iota2((nc, CH, CH), 1), _iota2((nc, CH, CH), 2)
        a = jnp.zeros((nc, CH, CH), F32)
        for t in range(CH):
            a = a + jnp.where(col == t, r[:, t * CH:(t + 1) * CH, :], 0.0)
        a = jnp.where(row >= col, a, 0.0)
        x_upd = _bdot(v3, kk3 * jnp.exp(bl3 - b3), 1, 1)
        ebl3 = jnp.exp(bl3)
        st = st_ref[...]
        for c in range(nc):
            states_ref[c] = st
            st = st * ebl3[c] + x_upd[c]
        st_ref[...] = st
        o3 = _bdot(q3 * jnp.exp(b3), states_ref[...], 2, 2) + _bdot(a, v3, 2, 1)
        o_ref[...] = o3.reshape(rb, HG_D)

    sec = lambda j: pl.BlockSpec((None, rb, HG_D), lambda h, i, j=j: (j, i, h))
    return pl.pallas_call(
        body, name="hgrn_fwd", grid=(HG_HEADS, nb),
        in_specs=[sec(0), sec(1), sec(2), pl.BlockSpec((2, HG_D), lambda h, i: (0, h))],
        out_specs=[pl.BlockSpec((rb, HG_D), lambda h, i: (i, h)),
                   pl.BlockSpec((None, None, HG_D, HG_D), lambda h, i: (i, h, 0, 0))],
        out_shape=[jax.ShapeDtypeStruct((s, SEC_W), F32),
                   jax.ShapeDtypeStruct((nb, HG_HEADS, HG_D, HG_D), F32)],
        scratch_shapes=[pltpu.VMEM((HG_D, HG_D), F32), pltpu.VMEM((nc, CH * CH, HG_D), MM),
                        pltpu.VMEM((nc, HG_D, HG_D), F32)],
        compiler_params=_params("parallel", "arbitrary"),
    )(proj, proj, proj, lb_logits)


def _hgrn_bwd(proj, lb_logits, d_o, sst, rb=256):
    s = proj.shape[1]
    nb, nc = s // rb, rb // CH

    def body(q_ref, f_ref, i_ref, lbl_ref, do_ref, sst_ref, dxq_ref, dxf_ref, dxi_ref, dlb_ref,
             dst_ref, states_ref, dstates_ref, lslab_ref, kslab_ref):
        @pl.when(pl.program_id(1) == 0)
        def _():
            dst_ref[...] = jnp.zeros_like(dst_ref)
            dlb_ref[...] = jnp.zeros_like(dlb_ref)

        lb, _ = _lower_bound(lbl_ref[...])
        prefix, suffix = _chunk_masks(rb)
        c3 = lambda a: a.reshape(nc, CH, HG_D)
        flat = lambda a: a.reshape(rb, HG_D)
        xq = q_ref[...]
        sgq, q, sg, sn, f, kk = _hgrn_gates(xq, f_ref[...], lb)
        b3 = c3(_tri_dot(prefix, jnp.log(f)))
        q3, kk3, v3, do3 = c3(q), c3(kk), c3(i_ref[...]), c3(do_ref[...])
        bl3 = b3[:, CH - 1:CH, :]
        eb3, ebl3, dec3 = jnp.exp(b3), jnp.exp(bl3), jnp.exp(bl3 - b3)
        qe3, kd3 = q3 * eb3, kk3 * dec3
        x_upd, y_upd = _bdot(v3, kd3, 1, 1), _bdot(do3, qe3, 1, 1)
        st = sst_ref[...]
        for c in range(nc):
            states_ref[c] = st
            st = st * ebl3[c] + x_upd[c]
        dst = dst_ref[...]
        for c in reversed(range(nc)):
            dstates_ref[c] = dst
            dst = dst * ebl3[c] + y_upd[c]
        dst_ref[...] = dst
        states, dstates = states_ref[...], dstates_ref[...]
        dqe = _bdot(do3, states, 2, 1)
        dkd = _bdot(v3, dstates, 2, 1)
        row, col = _iota2((nc, CH, CH), 1), _iota2((nc, CH, CH), 2)
        tril, triu = row >= col, row <= col
        d_a = jnp.where(tril, _bdot(do3, v3, 2, 2), 0.0)
        d_at = jnp.where(triu, _bdot(v3, do3, 2, 2), 0.0)
        for t in range(CH):
            bt = b3[:, t:t + 1, :]
            lslab_ref[:, t * CH:(t + 1) * CH, :] = (q3 * jnp.exp(jnp.minimum(b3 - bt, 0.0))).astype(MM)
            kslab_ref[:, t * CH:(t + 1) * CH, :] = (kk3 * jnp.exp(jnp.minimum(bt - b3, 0.0))).astype(MM)
        r = _bdot(kslab_ref[...], q3, 2, 2)
        a_t = jnp.zeros((nc, CH, CH), F32)
        for t in range(CH):
            a_t = a_t + jnp.where(col == t, r[:, t * CH:(t + 1) * CH, :], 0.0)
        a_t = jnp.where(triu, a_t, 0.0)
        dv = _bdot(kd3, dstates, 2, 2) + _bdot(a_t, do3, 2, 1)
        sel = (_iota2((CH, CH * CH), 1) % CH == _iota2((CH, CH * CH), 0)).astype(MM)
        blockdiag = _iota2((nc, CH, CH * CH), 2) // CH == _iota2((nc, CH, CH * CH), 1)
        tile = lambda m: jnp.where(blockdiag, _dot(m.reshape(rb, CH), sel).reshape(nc, CH, CH * CH), 0.0)
        dq_in = _bdot(tile(d_a), kslab_ref[...], 2, 1)
        dk_in = _bdot(tile(d_at), lslab_ref[...], 2, 1)
        dkd_kd = dkd * kd3
        db = dqe * qe3 - dkd_kd + q3 * dq_in - kk3 * dk_in
        dbl = jnp.sum(dkd_kd, axis=1, keepdims=True) + jnp.sum(dstates * states, axis=1, keepdims=True) * ebl3
        last = _iota2((nc, CH, HG_D), 1) == CH - 1
        dg = _tri_dot(suffix, flat(db + jnp.where(last, dbl, 0.0)))
        df = dg / f - flat(dkd * dec3 + dk_in)
        dxq_ref[...] = (flat(dqe * eb3 + dq_in) * (sgq * (1.0 + xq * (1.0 - sgq)))).astype(MM)
        dxf_ref[...] = (df * (1.0 - lb) * sg * sn).astype(MM)
        dxi_ref[...] = flat(dv).astype(MM)
        dlb_ref[...] += jnp.sum(df * sn, axis=0, keepdims=True)

    rev = lambda i: nb - 1 - i
    sec = lambda j: pl.BlockSpec((None, rb, HG_D), lambda h, i, j=j: (j, rev(i), h))
    blk = pl.BlockSpec((rb, HG_D), lambda h, i: (rev(i), h))
    return pl.pallas_call(
        body, name="hgrn_bwd", grid=(HG_HEADS, nb),
        in_specs=[sec(0), sec(1), sec(2), pl.BlockSpec((2, HG_D), lambda h, i: (0, h)), blk,
                  pl.BlockSpec((None, None, HG_D, HG_D), lambda h, i: (rev(i), h, 0, 0))],
        out_specs=[blk, blk, blk, pl.BlockSpec((1, HG_D), lambda h, i: (0, h))],
        out_shape=[jax.ShapeDtypeStruct((s, SEC_W), MM)] * 3 + [jax.ShapeDtypeStruct((1, SEC_W), F32)],
        scratch_shapes=[pltpu.VMEM((HG_D, HG_D), F32), pltpu.VMEM((nc, HG_D, HG_D), F32),
                        pltpu.VMEM((nc, HG_D, HG_D), F32),
                        pltpu.VMEM((nc, CH * CH, HG_D), MM), pltpu.VMEM((nc, CH * CH, HG_D), MM)],
        compiler_params=_params("parallel", "arbitrary"),
    )(proj, proj, proj, lb_logits, d_o, sst)


def _rope_tables(s):
    half = AT_DH // 2
    inv_freq = 1.0 / (ROPE_THETA ** (jnp.arange(half, dtype=F32) / half))
    ang = jnp.arange(s, dtype=jnp.int32).astype(F32)[:, None] * inv_freq[None, :]
    cos, sin = jnp.cos(ang), jnp.sin(ang)
    return jnp.concatenate([cos] * 4, axis=-1), jnp.concatenate([-sin, sin] * 2, axis=-1)


def _rope128(x, cos, sin):
    lo = (_iota2(x.shape, 1) % AT_DH) < AT_DH // 2
    rot = jnp.where(lo, pltpu.roll(x, 128 - AT_DH // 2, 1), pltpu.roll(x, AT_DH // 2, 1))
    return x * cos + rot * sin


LANE_GROUPS = SEC_W // 128


def _set_lanes(ref, val):
    for j in range(LANE_GROUPS):
        ref[j] = val[:, j * 128:(j + 1) * 128]


def _get_lanes(ref):
    return jnp.concatenate([ref[j] for j in range(LANE_GROUPS)], axis=-1)


def _to_view(src_ref, dst_ref, d):
    n = src_ref.shape[1] // d
    for r in range(d):
        rows = pl.ds(r, n, stride=d) if d > 1 else slice(None)
        for j in range(LANE_GROUPS):
            c0 = r * SEC_W + j * 128
            dst_ref[:, c0:c0 + 128] = src_ref.at[j][rows, :].astype(dst_ref.dtype)


def _from_view(src_ref, dst_ref, d):
    n = dst_ref.shape[1] // d
    for r in range(d):
        for j in range(LANE_GROUPS):
            c0 = r * SEC_W + j * 128
            dst_ref.at[j][pl.ds(r, n, stride=d), :] = src_ref[:, c0:c0 + 128].astype(dst_ref.dtype)


def _view_spec(tm, d):
    return pl.BlockSpec((tm // d, d * SEC_W), lambda i: (i, 0))


def _view_shape(s, d, dtype):
    return jax.ShapeDtypeStruct((s // d, d * SEC_W), dtype)


def _attn_prep(proj, cos, sin, tm=512):
    s = proj.shape[1]

    def body(q_ref, k_ref, v_ref, cos_ref, sin_ref, *refs):
        outs, (qs_ref, ks_ref, vs_ref) = refs[:-3], refs[-3:]
        c, sn = cos_ref[...], sin_ref[...]
        for j in range(LANE_GROUPS):
            sl = slice(j * 128, (j + 1) * 128)
            qs_ref[j] = _rope128(q_ref[:, sl], c, sn) * (AT_DH ** -0.5)
            ks_ref[j] = _rope128(k_ref[:, sl], c, sn)
            vs_ref[j] = v_ref[:, sl]
        for i, d in enumerate(DILATIONS):
            for src_ref, dst_ref in zip((qs_ref, ks_ref, vs_ref), outs[3 * i:3 * i + 3]):
                _to_view(src_ref, dst_ref, d)

    sec = lambda j: pl.BlockSpec((None, tm, SEC_W), lambda i, j=j: (j, i, 0))
    tab = pl.BlockSpec((tm, 128), lambda i: (i, 0))
    return pl.pallas_call(
        body, name="attn_prep", grid=(s // tm,),
        in_specs=[sec(4), sec(5), sec(6), tab, tab],
        out_specs=[_view_spec(tm, d) for d in DILATIONS for _ in range(3)],
        out_shape=[_view_shape(s, d, MM) for d in DILATIONS for _ in range(3)],
        scratch_shapes=[pltpu.VMEM((LANE_GROUPS, tm, 128), F32)] * 3,
        compiler_params=_params("parallel"),
    )(proj, proj, proj, cos, sin)


def _band_mask(first_ok, second_ok):
    row, col = _iota2((ATT_BLK, 2 * ATT_BLK), 0), _iota2((ATT_BLK, 2 * ATT_BLK), 1)
    return ((col < ATT_BLK) & (col >= row) & first_ok) | ((col >= ATT_BLK) & ((col - ATT_BLK) <= row) & second_ok)


def _own_lanes(rows, h):
    lane = _iota2((rows, 128), 1)
    return (lane < AT_DH) if h == 0 else (lane >= AT_DH)


def _neg_pieces(rows, h):
    lane = _iota2((rows, 128), 1) - (AT_DH if h == 0 else 0)
    return jnp.where((lane >= 0) & (lane < 3), -1.0, 0.0).astype(MM)


def _units():
    return [(b, slice(g * 128, (g + 1) * 128), h) for b in range(AT_QB) for g in range(AT_COLS // 128) for h in range(2)]


def _sub(b):
    return slice(b * ATT_BLK, (b + 1) * ATT_BLK)


def _band_before(cur_ref, prev_ref, b, sl):
    if b == 0:
        return jnp.concatenate([prev_ref[:, sl], cur_ref[0:ATT_BLK, sl]], axis=0)
    return cur_ref[(b - 1) * ATT_BLK:(b + 1) * ATT_BLK, sl]


def _band_after(cur_ref, next_ref, b, sl):
    if b == AT_QB - 1:
        return jnp.concatenate([cur_ref[b * ATT_BLK:(b + 1) * ATT_BLK, sl], next_ref[:, sl]], axis=0)
    return cur_ref[b * ATT_BLK:(b + 2) * ATT_BLK, sl]


def _attn_specs(rows):
    assert rows % (AT_QB * ATT_BLK) == 0
    last = rows // ATT_BLK - 1
    cur = pl.BlockSpec((AT_QB * ATT_BLK, AT_COLS), lambda c, n: (n, c))
    prev = pl.BlockSpec((ATT_BLK, AT_COLS), lambda c, n: (jnp.maximum(AT_QB * n - 1, 0), c))
    nxt = pl.BlockSpec((ATT_BLK, AT_COLS), lambda c, n: (jnp.minimum(AT_QB * (n + 1), last), c))
    return cur, prev, nxt


def _attn_fwd(qr, kr, vr, d):
    rows, cols = qr.shape
    nb = rows // (AT_QB * ATT_BLK)

    def body(q_ref, kc_ref, kp_ref, vc_ref, vp_ref, o_ref, lse_ref):
        masks = {True: _band_mask(pl.program_id(1) > 0, True), False: _band_mask(True, True)}
        ones = jnp.ones((2 * ATT_BLK, 128), MM)
        head0 = _own_lanes(ATT_BLK, 0)
        units = _units()
        scs = []
        for b, sl, h in units:
            q2 = q_ref[_sub(b), sl]
            qh = jnp.where(_own_lanes(ATT_BLK, h), q2, jnp.zeros_like(q2))
            scs.append(jnp.where(masks[b == 0], _dot_nt(qh, _band_before(kc_ref, kp_ref, b, sl)), NEG))
        ms = [jnp.max(sc, axis=-1, keepdims=True) for sc in scs]
        ps = [jnp.exp(sc - m).astype(MM) for sc, m in zip(scs, ms)]
        ls = [jnp.dot(p, ones, preferred_element_type=F32) for p in ps]
        os_ = [jnp.dot(p, _band_before(vc_ref, vp_ref, b, sl), preferred_element_type=F32)
               for p, (b, sl, _) in zip(ps, units)]
        for i in range(0, len(units), 2):
            b, sl, _ = units[i]
            l = jnp.where(head0, ls[i], ls[i + 1])
            o_ref[_sub(b), sl] = jnp.where(head0, os_[i], os_[i + 1]) / l
            lse_ref[_sub(b), sl] = jnp.where(head0, ms[i], ms[i + 1]) + jnp.log(l)

    cur, prev, _ = _attn_specs(rows)
    o, lse = pl.pallas_call(
        body, name=f"attn_fwd_d{d}", grid=(cols // AT_COLS, nb),
        in_specs=[cur, cur, prev, cur, prev], out_specs=[cur, cur],
        out_shape=[jax.ShapeDtypeStruct((rows, cols), F32)] * 2,
        compiler_params=_params("parallel", "parallel"),
    )(qr, kr, kr, vr, vr)
    return o, lse


def _attn_bwd_dq(qr, kr, vr, do, lse, delta, d):
    rows, cols = qr.shape
    nb = rows // (AT_QB * ATT_BLK)

    def body(q_ref, kc_ref, kp_ref, vc_ref, vp_ref, do_ref, lse_ref, dl_ref, dq_ref):
        masks = {True: _band_mask(pl.program_id(1) > 0, True), False: _band_mask(True, True)}
        units = _units()
        sms, dps = [], []
        for b, sl, h in units:
            own, own_b, neg = _own_lanes(ATT_BLK, h), _own_lanes(2 * ATT_BLK, h), _neg_pieces(2 * ATT_BLK, h)
            sms.append(_dot_nt(jnp.where(own, q_ref[_sub(b), sl], lse_ref[_sub(b), sl]),
                               jnp.where(own_b, _band_before(kc_ref, kp_ref, b, sl), neg)))
            dps.append(_dot_nt(jnp.where(own, do_ref[_sub(b), sl], dl_ref[_sub(b), sl]),
                               jnp.where(own_b, _band_before(vc_ref, vp_ref, b, sl), neg)))
        dss = [(jnp.exp(jnp.where(masks[b == 0], sm, NEG)) * dp).astype(MM)
               for sm, dp, (b, _, _) in zip(sms, dps, units)]
        dqs = [jnp.dot(ds, _band_before(kc_ref, kp_ref, b, sl), preferred_element_type=F32) * (AT_DH ** -0.5)
               for ds, (b, sl, _) in zip(dss, units)]
        for i in range(0, len(units), 2):
            b, sl, _ = units[i]
            dq_ref[_sub(b), sl] = jnp.where(_own_lanes(ATT_BLK, 0), dqs[i], dqs[i + 1]).astype(dq_ref.dtype)

    cur, prev, _ = _attn_specs(rows)
    dq = pl.pallas_call(
        body, name=f"attn_bwd_dq_d{d}", grid=(cols // AT_COLS, nb),
        in_specs=[cur, cur, prev, cur, prev, cur, cur, cur], out_specs=cur,
        out_shape=jax.ShapeDtypeStruct((rows, cols), MM),
        compiler_params=_params("parallel", "parallel"),
    )(qr, kr, kr, vr, vr, do, lse, delta)
    return dq


def _attn_bwd_dkv(qr, kr, vr, do, lse, delta, d):
    rows, cols = qr.shape
    nb = rows // (AT_QB * ATT_BLK)

    def body(k_ref, v_ref, qc_ref, qn_ref, doc_ref, don_ref, lsec_ref, lsen_ref, dlc_ref, dln_ref,
             dk_ref, dv_ref):
        masks = {True: _band_mask(True, pl.program_id(1) < nb - 1), False: _band_mask(True, True)}
        units = _units()
        sms, dps = [], []
        for b, sl, h in units:
            own, own_b, neg = _own_lanes(ATT_BLK, h), _own_lanes(2 * ATT_BLK, h), _neg_pieces(ATT_BLK, h)
            sms.append(_dot_nt(jnp.where(own, k_ref[_sub(b), sl], neg),
                               jnp.where(own_b, _band_after(qc_ref, qn_ref, b, sl),
                                         _band_after(lsec_ref, lsen_ref, b, sl))))
            dps.append(_dot_nt(jnp.where(own, v_ref[_sub(b), sl], neg),
                               jnp.where(own_b, _band_after(doc_ref, don_ref, b, sl),
                                         _band_after(dlc_ref, dln_ref, b, sl))))
        ps = [jnp.exp(jnp.where(masks[b == AT_QB - 1], sm, NEG)) for sm, (b, _, _) in zip(sms, units)]
        dss = [(p * dp).astype(MM) for p, dp in zip(ps, dps)]
        dvs = [jnp.dot(p.astype(MM), _band_after(doc_ref, don_ref, b, sl), preferred_element_type=F32)
               for p, (b, sl, _) in zip(ps, units)]
        dks = [jnp.dot(ds, _band_after(qc_ref, qn_ref, b, sl), preferred_element_type=F32)
               for ds, (b, sl, _) in zip(dss, units)]
        head0 = _own_lanes(ATT_BLK, 0)
        for i in range(0, len(units), 2):
            b, sl, _ = units[i]
            dk_ref[_sub(b), sl] = jnp.where(head0, dks[i], dks[i + 1]).astype(dk_ref.dtype)
            dv_ref[_sub(b), sl] = jnp.where(head0, dvs[i], dvs[i + 1]).astype(dv_ref.dtype)

    cur, _, nxt = _attn_specs(rows)
    dk, dv = pl.pallas_call(
        body, name=f"attn_bwd_dkv_d{d}", grid=(cols // AT_COLS, nb),
        in_specs=[cur, cur, cur, nxt, cur, nxt, cur, nxt, cur, nxt], out_specs=[cur, cur],
        out_shape=[jax.ShapeDtypeStruct((rows, cols), MM)] * 2,
        compiler_params=_params("parallel", "parallel"),
    )(kr, vr, qr, qr, do, do, lse, lse, delta, delta)
    return dk, dv


def _head_sum(a, width):
    parts = []
    for j in range(a.shape[1] // width):
        sm = jnp.sum(a[:, j * width:(j + 1) * width], axis=-1, keepdims=True)
        parts.append(jnp.broadcast_to(sm, (a.shape[0], width)))
    return jnp.concatenate(parts, axis=-1)


def _partner_pieces(x):
    xs = jnp.concatenate([pltpu.roll(x[:, j * 128:(j + 1) * 128], AT_DH, 1) for j in range(x.shape[1] // 128)],
                         axis=-1)
    hi = xs.astype(jnp.bfloat16).astype(F32)
    mid = (xs - hi).astype(jnp.bfloat16).astype(F32)
    lo = (xs - hi - mid).astype(jnp.bfloat16).astype(F32)
    lane = _iota2(x.shape, 1) % AT_DH
    return jnp.where(lane == 0, hi, jnp.where(lane == 1, mid, jnp.where(lane == 2, lo, 0.0)))


def _mid(x, tgt, proj, o_hg, o_at, lse_at, hg_norm_w, final_norm_w, wo_all, tm=256):
    s = x.shape[0]
    nb = s // tm

    def body(x_ref, t_ref, hgz_ref, atz_ref, ohg_ref, o1_ref, o2_ref, o3_ref, l1_ref, l2_ref, l3_ref,
             g_ref, fw_ref, wo_ref,
             dh_ref, dohg_ref, dhgz_ref, datz_ref, do1_ref, do2_ref, do3_ref, dl1_ref, dl2_ref, dl3_ref,
             lp1_ref, lp2_ref, lp3_ref,
             gwo_ref, gfw_ref, ghg_ref, loss_ref, nat_ref, stage_ref, gwo_acc):
        @pl.when(pl.program_id(0) == 0)
        def _():
            gwo_acc[...] = jnp.zeros_like(gwo_acc)
            gfw_ref[...] = jnp.zeros_like(gfw_ref)
            ghg_ref[...] = jnp.zeros_like(ghg_ref)
            loss_ref[...] = jnp.zeros_like(loss_ref)

        ohg, g = ohg_ref[...], g_ref[...]
        rs = lax.rsqrt(_head_sum(ohg * ohg, HG_D) * (1.0 / HG_D) + NORM_EPS)
        on = ohg * rs
        hgz = hgz_ref[...]
        sz = _sigmoid(hgz)
        gate_hg = hgz * sz
        lses, outs = [l1_ref[...]], [o1_ref[...]]
        for k, (d, l_ref, o_ref) in enumerate(zip(DILATIONS[1:], (l2_ref, l3_ref), (o2_ref, o3_ref))):
            _from_view(l_ref, nat_ref.at[2 * k], d)
            _from_view(o_ref, nat_ref.at[2 * k + 1], d)
            lses.append(_get_lanes(nat_ref.at[2 * k]))
            outs.append(_get_lanes(nat_ref.at[2 * k + 1]))
        mx = jnp.maximum(jnp.maximum(lses[0], lses[1]), lses[2])
        es = [jnp.exp(l - mx) for l in lses]
        den = es[0] + es[1] + es[2]
        ws = [e / den for e in es]
        oat = ws[0] * outs[0] + ws[1] * outs[1] + ws[2] * outs[2]
        atz = atz_ref[...]
        sa = _sigmoid(atz)
        gate_at = atz * sa
        mixed = jnp.concatenate([on * g * gate_hg, oat * gate_at], axis=-1).astype(MM)
        h = x_ref[...] + jnp.dot(mixed, wo_ref[...], preferred_element_type=F32)
        rstd = lax.rsqrt(jnp.mean(h * h, axis=-1, keepdims=True) + NORM_EPS)
        hn = h * rstd
        fw = fw_ref[...]
        err = hn * fw - t_ref[...]
        loss_ref[...] += 0.5 * jnp.sum(jnp.mean(err * err, axis=-1, keepdims=True), axis=0, keepdims=True)
        dout = err * (1.0 / D_MODEL)
        gfw_ref[...] += jnp.sum(dout * hn, axis=0, keepdims=True)
        dhn = dout * fw
        dh = rstd * (dhn - hn * jnp.mean(dhn * hn, axis=-1, keepdims=True))
        dh_ref[...] = dh
        dh_mm = dh.astype(MM)
        gwo_acc[...] += _dot_tn(mixed, dh_mm)

        @pl.when(pl.program_id(0) == nb - 1)
        def _():
            gwo_ref[...] = gwo_acc[...].astype(gwo_ref.dtype)

        dmixed = _dot_nt(dh_mm, wo_ref[...])
        dm_hg = dmixed[:, :SEC_W]
        d_ong = dm_hg * gate_hg
        dhgz_ref[...] = (dm_hg * (on * g) * (sz * (1.0 + hgz * (1.0 - sz)))).astype(MM)
        ghg_ref[...] += jnp.sum(d_ong * on, axis=0, keepdims=True)
        d_on = d_ong * g
        dohg_ref[...] = rs * (d_on - on * (_head_sum(d_on * on, HG_D) * (1.0 / HG_D)))
        dm_at = dmixed[:, SEC_W:]
        d_oat = dm_at * gate_at
        datz_ref[...] = (dm_at * oat * (sa * (1.0 + atz * (1.0 - sa)))).astype(MM)
        drow = _head_sum(d_oat * oat, AT_DH)
        lse_all = mx + jnp.log(den)
        for val, dst_refs in ((d_oat, (do1_ref, do2_ref, do3_ref)),
                              (_partner_pieces(drow), (dl1_ref, dl2_ref, dl3_ref)),
                              (_partner_pieces(lse_all), (lp1_ref, lp2_ref, lp3_ref))):
            _set_lanes(stage_ref, val)
            for d, dst_ref in zip(DILATIONS, dst_refs):
                _to_view(stage_ref, dst_ref, d)

    row = lambda w: pl.BlockSpec((tm, w), lambda i: (i, 0))
    sec = lambda j: pl.BlockSpec((None, tm, SEC_W), lambda i, j=j: (j, i, 0))
    const = lambda shp: pl.BlockSpec(shp, lambda i: (0,) * len(shp))
    half = row(SEC_W)
    views = [_view_spec(tm, d) for d in DILATIONS]
    return pl.pallas_call(
        body, name="mid", grid=(nb,),
        in_specs=[row(D_MODEL), row(D_MODEL), sec(3), sec(7), half] + views * 2
                 + [const((1, SEC_W)), const((1, D_MODEL)), const((D_MODEL, D_MODEL))],
        out_specs=[row(D_MODEL)] + [half] * 3 + views * 3
                  + [const((D_MODEL, D_MODEL)), const((1, D_MODEL)), const((1, SEC_W)), const((1, 1))],
        out_shape=[jax.ShapeDtypeStruct((s, D_MODEL), F32), jax.ShapeDtypeStruct((s, SEC_W), F32)]
                  + [jax.ShapeDtypeStruct((s, SEC_W), MM)] * 2
                  + [_view_shape(s, d, MM) for d in DILATIONS] * 3
                  + [jax.ShapeDtypeStruct((D_MODEL, D_MODEL), XCH), jax.ShapeDtypeStruct((1, D_MODEL), F32),
                     jax.ShapeDtypeStruct((1, SEC_W), F32), jax.ShapeDtypeStruct((1, 1), F32)],
        scratch_shapes=[pltpu.VMEM((4, LANE_GROUPS, tm, 128), F32), pltpu.VMEM((LANE_GROUPS, tm, 128), F32),
                        pltpu.VMEM((D_MODEL, D_MODEL), F32)],
        compiler_params=_params("arbitrary"),
    )(x, tgt, proj, proj, o_hg, *o_at, *lse_at, hg_norm_w, final_norm_w, wo_all)


def _attn_dproj(dq_r, dk_r, dv, cos, sin, tm=512):
    s = cos.shape[0]

    def body(q1, q2, q3, k1, k2, k3, v1, v2, v3, cos_ref, sin_ref, dp_ref, nat_ref):
        def total(refs):
            acc = refs[0][...].astype(F32)
            for d, ref in zip(DILATIONS[1:], refs[1:]):
                _from_view(ref, nat_ref, d)
                acc = acc + _get_lanes(nat_ref)
            return acc

        c, sn = cos_ref[...], -sin_ref[...]
        unrot = lambda a: jnp.concatenate(
            [_rope128(a[:, j * 128:(j + 1) * 128], c, sn) for j in range(LANE_GROUPS)], axis=-1)
        dp_ref[0] = unrot(total((q1, q2, q3))).astype(MM)
        dp_ref[1] = unrot(total((k1, k2, k3))).astype(MM)
        dp_ref[2] = total((v1, v2, v3)).astype(MM)

    tab = pl.BlockSpec((tm, 128), lambda i: (i, 0))
    return pl.pallas_call(
        body, name="attn_dproj", grid=(s // tm,),
        in_specs=[_view_spec(tm, d) for d in DILATIONS] * 3 + [tab, tab],
        out_specs=pl.BlockSpec((3, tm, SEC_W), lambda i: (0, i, 0)),
        out_shape=jax.ShapeDtypeStruct((3, s, SEC_W), MM),
        scratch_shapes=[pltpu.VMEM((LANE_GROUPS, tm, 128), F32)],
        compiler_params=_params("parallel"),
    )(*dq_r, *dk_r, *dv, cos, sin)


def _section_specs(dsecs, tm):
    return [pl.BlockSpec((tm, SEC_W), lambda i: (i, 0)) if k is None
            else pl.BlockSpec((None, tm, SEC_W), lambda i, k=k: (k, i, 0)) for _, k in dsecs]


def _inproj_bwd_x(x, norm_w, w_all, dh, dsecs, token, tm=256):
    s = x.shape[0]

    def body(x_ref, nw_ref, w_ref, dh_ref, tok_ref, *refs):
        sec_refs, (gx_ref, gnw_ref) = refs[:N_SEC], refs[N_SEC:]

        @pl.when(pl.program_id(0) == 0)
        def _():
            gnw_ref[...] = jnp.zeros_like(gnw_ref)

        du = jnp.zeros((tm, D_MODEL), F32)
        for j in range(N_SEC):
            du = du + _dot_nt(sec_refs[j][...], w_ref[j])
        xv, nw = x_ref[...], nw_ref[...]
        rstd = lax.rsqrt(jnp.mean(xv * xv, axis=-1, keepdims=True) + NORM_EPS)
        xn = xv * rstd
        gnw_ref[...] += jnp.sum(du * xn, axis=0, keepdims=True)
        dxn = du * nw
        dx = rstd * (dxn - xn * jnp.mean(dxn * xn, axis=-1, keepdims=True))
        gx_ref[...] = (dh_ref[...] + tok_ref[0:1, 0:1]) + dx

    row = lambda w: pl.BlockSpec((tm, w), lambda i: (i, 0))
    const = lambda shp: pl.BlockSpec(shp, lambda i: (0,) * len(shp))
    return pl.pallas_call(
        body, name="inproj_bwd_x", grid=(s // tm,),
        in_specs=[row(D_MODEL), const((1, D_MODEL)), const((N_SEC, D_MODEL, SEC_W)), row(D_MODEL), const((8, 128))]
                 + _section_specs(dsecs, tm),
        out_specs=[row(D_MODEL), const((1, D_MODEL))],
        out_shape=[jax.ShapeDtypeStruct((s, D_MODEL), F32), jax.ShapeDtypeStruct((1, D_MODEL), F32)],
        compiler_params=_params("arbitrary"),
    )(x, norm_w, w_all, dh, token, *[a for a, _ in dsecs])


def _inproj_bwd_w(x, norm_w, dsecs, tm=1024):
    s = x.shape[0]
    nb = s // tm

    def body(x_ref, nw_ref, *refs):
        dp_refs, (gw_hbm, acc_ref, stage_ref) = refs[:N_SEC], refs[N_SEC:]

        @pl.when(pl.program_id(0) == 0)
        def _():
            acc_ref[...] = jnp.zeros_like(acc_ref)

        xv = x_ref[...]
        rstd = lax.rsqrt(jnp.mean(xv * xv, axis=-1, keepdims=True) + NORM_EPS)
        u_t = (xv * rstd * nw_ref[...]).T.astype(MM)
        for j in range(N_SEC):
            acc_ref[j] += jnp.dot(u_t, dp_refs[j][...], preferred_element_type=F32)

        @pl.when(pl.program_id(0) == nb - 1)
        def _():
            for j in range(N_SEC):
                stage_ref[...] = acc_ref[j].astype(stage_ref.dtype)
                pltpu.sync_copy(stage_ref, gw_hbm.at[j])

    return pl.pallas_call(
        body, name="inproj_bwd_w", grid=(nb,),
        in_specs=[pl.BlockSpec((tm, D_MODEL), lambda i: (i, 0)), pl.BlockSpec((1, D_MODEL), lambda i: (0, 0))]
                 + _section_specs(dsecs, tm),
        out_specs=pl.BlockSpec(memory_space=pl.ANY),
        out_shape=jax.ShapeDtypeStruct((N_SEC, D_MODEL, SEC_W), XCH),
        scratch_shapes=[pltpu.VMEM((N_SEC, D_MODEL, SEC_W), F32), pltpu.VMEM((D_MODEL, SEC_W), XCH)],
        compiler_params=_params("arbitrary"),
    )(x, norm_w, *[a for a, _ in dsecs])


def _local_step(x, tgt, norm_w, w_all, lb_logits, hg_norm_w, wo_all, final_norm_w, on_weight_grads):
    s = x.shape[0]
    cos, sin = _rope_tables(s)
    proj = _inproj_fwd(x, norm_w, w_all)
    o_hg, sst = _hgrn_fwd(proj, lb_logits)
    qkv = _attn_prep(proj, cos, sin)
    qkv = [qkv[3 * i:3 * i + 3] for i in range(len(DILATIONS))]
    att = [_attn_fwd(*qkv_d, d) for qkv_d, d in zip(qkv, DILATIONS)]
    (dh, d_ohg, d_hgz, d_atz, do1, do2, do3, dl1, dl2, dl3, lp1, lp2, lp3, gwo, gfw, ghg, loss) = _mid(
        x, tgt, proj, o_hg, [a[0] for a in att], [a[1] for a in att], hg_norm_w, final_norm_w[None, :], wo_all)
    dxq, dxf, dxi, dlb = _hgrn_bwd(proj, lb_logits, d_ohg, sst)
    dq_r, dk_r, dv = [], [], []
    for d, qkv_d, do, lp, dl in zip(DILATIONS, qkv, (do1, do2, do3), (lp1, lp2, lp3), (dl1, dl2, dl3)):
        dq_r.append(_attn_bwd_dq(*qkv_d, do, lp, dl, d))
        dk_d, dv_d = _attn_bwd_dkv(*qkv_d, do, lp, dl, d)
        dk_r.append(dk_d)
        dv.append(dv_d)
    d_att = _attn_dproj(dq_r, dk_r, dv, cos, sin)
    dsecs = [(dxq, None), (dxf, None), (dxi, None), (d_hgz, None), (d_att, 0), (d_att, 1), (d_att, 2), (d_atz, None)]
    gwi = _inproj_bwd_w(x, norm_w, dsecs)
    token = on_weight_grads(gwi, gwo)
    gx, gnw = _inproj_bwd_x(x, norm_w, w_all, dh, dsecs, token)
    small = jnp.concatenate([gnw, jnp.concatenate([dlb, ghg], axis=-1), gfw,
                             jnp.pad(loss, ((0, 0), (0, D_MODEL - 1)))], axis=0)
    return gx, gwi, gwo, small


def _coords():
    return lax.axis_index("x"), lax.axis_index("y"), lax.axis_index("c")


def _gather_weights(w_in, w_out):
    wo_rows = w_out.shape[0]

    def body(wi_ref, wo_ref, wi_all, wo_all, send_sems, recv_sems):
        x, y, c = _coords()
        me, sibling = (x, y, c), (x, y, 1 - c)
        chips = [(1 - x, y), (x, 1 - y), (1 - x, 1 - y)]
        slot = lambda p: 4 * p[0] + 2 * p[1] + p[2]

        def copies(k, block, to):
            return [pltpu.make_async_remote_copy(
                src_ref=ref.at[slot(block)], dst_ref=ref.at[slot(block)], send_sem=send_sems.at[a, k],
                recv_sem=recv_sems.at[a, k], device_id=to, device_id_type=MESH)
                for a, ref in enumerate((wi_all, wo_all))]

        wi_all[slot(me)] = wi_ref[...].astype(MM)
        wo_all[slot(me)] = wo_ref[...].astype(MM)
        first = copies(0, me, sibling)
        for j, chip in enumerate(chips):
            first += copies(1 + j, me, (*chip, c))
        for cp in first:
            cp.start()
        passed = []
        for j, chip in enumerate(chips):
            for cp in copies(1 + j, (*chip, c), me):
                cp.wait_recv()
            fwd = copies(4 + j, (*chip, c), sibling)
            for cp in fwd:
                cp.start()
            passed += fwd
        for cp in copies(0, sibling, me):
            cp.wait_recv()
        for j, chip in enumerate(chips):
            for cp in copies(4 + j, (*chip, 1 - c), me):
                cp.wait_recv()
        for cp in first + passed:
            cp.wait_send()

    vmem = pl.BlockSpec(memory_space=pltpu.VMEM)
    return pl.pallas_call(
        body, name="gather_weights",
        in_specs=[vmem, vmem], out_specs=[vmem, vmem],
        out_shape=[jax.ShapeDtypeStruct((N_DEV, D_MODEL, SEC_W), MM),
                   jax.ShapeDtypeStruct((N_DEV, wo_rows, D_MODEL), MM)],
        scratch_shapes=[pltpu.SemaphoreType.DMA((2, 7)), pltpu.SemaphoreType.DMA((2, 7))],
        compiler_params=pltpu.CompilerParams(vmem_limit_bytes=VMEM_LIMIT),
    )(w_in, w_out)


def _me():
    x, y, c = _coords()
    return 4 * x + 2 * y + c


def _grad_copies(srcs, lands, send_sems, recv_sems):
    x, y, c = _coords()
    me = 4 * x + 2 * y + c
    copies = []
    for k in range(1, N_DEV):
        px, py, pc = x ^ (k >> 2), y ^ ((k >> 1) & 1), c ^ (k & 1)
        peer = 4 * px + 2 * py + pc
        for a, (src, dst) in enumerate(zip(srcs, lands)):
            copies.append(pltpu.make_async_remote_copy(
                src_ref=src.at[peer], dst_ref=dst.at[me], send_sem=send_sems.at[a * (N_DEV - 1) + k - 1],
                recv_sem=recv_sems.at[a * (N_DEV - 1) + k - 1], device_id=(px, py, pc), device_id_type=MESH))
    return copies


HBM_SPEC = pl.BlockSpec(memory_space=pltpu.HBM)
SEM_SPEC = pl.BlockSpec(memory_space=pltpu.SEMAPHORE)
SPLIT_COPY_EFFECT = pltpu.SideEffectType.DATAFLOW_SIDE_EFFECTING


def _exchange_start(gwi, gwo):
    def body(gwi_ref, gwo_ref, li_ref, lo_ref, send_sems, recv_sems, gwi_thru, gwo_thru, li_thru, lo_thru, token):
        for cp in _grad_copies((gwi_ref, gwo_ref), (li_ref, lo_ref), send_sems, recv_sems):
            cp.start()
        token[...] = jnp.zeros_like(token)

    hbm = lambda a: pltpu.with_memory_space_constraint(a, pltpu.HBM)
    bufs = (gwi, gwo, lax.empty(gwi.shape, gwi.dtype), lax.empty(gwo.shape, gwo.dtype))
    return pl.pallas_call(
        body, name="exchange_start",
        out_shape=(pltpu.SemaphoreType.DMA((2 * (N_DEV - 1),)), pltpu.SemaphoreType.DMA((2 * (N_DEV - 1),)),
                   *[pltpu.HBM(a.shape, a.dtype) for a in bufs], jax.ShapeDtypeStruct((8, 128), F32)),
        in_specs=[HBM_SPEC] * 4,
        out_specs=(SEM_SPEC, SEM_SPEC, HBM_SPEC, HBM_SPEC, HBM_SPEC, HBM_SPEC, pl.BlockSpec(memory_space=pltpu.VMEM)),
        input_output_aliases={0: 2, 1: 3, 2: 4, 3: 5},
        compiler_params=pltpu.CompilerParams(has_side_effects=SPLIT_COPY_EFFECT),
    )(*[hbm(a) for a in bufs])


def _exchange_wait(send_sems, recv_sems, gwi, gwo, li, lo, after):
    def body(gwi_ref, gwo_ref, li_ref, lo_ref, send_sems, recv_sems, after_ref, gwi_out, gwo_out, li_out, lo_out):
        for cp in _grad_copies((gwi_ref, gwo_ref), (li_ref, lo_ref), send_sems, recv_sems):
            cp.wait_send()
            cp.wait_recv()

    return pl.pallas_call(
        body, name="exchange_wait",
        out_shape=tuple(pltpu.HBM(a.shape, a.dtype) for a in (gwi, gwo, li, lo)),
        in_specs=[HBM_SPEC] * 4 + [SEM_SPEC, SEM_SPEC, pl.BlockSpec(memory_space=pl.ANY)],
        out_specs=(HBM_SPEC,) * 4,
        input_output_aliases={0: 0, 1: 1, 2: 2, 3: 3},
        compiler_params=pltpu.CompilerParams(has_side_effects=SPLIT_COPY_EFFECT),
    )(gwi, gwo, li, lo, send_sems, recv_sems, after)


def _gather_small(small):
    def body(sm_ref, ls_ref, send_sems, recv_sems, local_sem):
        x, y, c = _coords()
        me = 4 * x + 2 * y + c
        own = pltpu.make_async_copy(sm_ref, ls_ref.at[me], local_sem)
        own.start()
        sends = []
        for k in range(1, N_DEV):
            peer = (x ^ (k >> 2), y ^ ((k >> 1) & 1), c ^ (k & 1))
            sends.append(pltpu.make_async_remote_copy(
                src_ref=sm_ref, dst_ref=ls_ref.at[me], send_sem=send_sems.at[k - 1], recv_sem=recv_sems.at[k - 1],
                device_id=peer, device_id_type=MESH))
        for cp in sends:
            cp.start()
        for cp in sends:
            cp.wait_recv()
        for cp in sends:
            cp.wait_send()
        own.wait()

    vmem = pl.BlockSpec(memory_space=pltpu.VMEM)
    return pl.pallas_call(
        body, name="gather_small", in_specs=[vmem], out_specs=vmem,
        out_shape=jax.ShapeDtypeStruct((N_DEV,) + small.shape, F32),
        scratch_shapes=[pltpu.SemaphoreType.DMA((N_DEV - 1,)), pltpu.SemaphoreType.DMA((N_DEV - 1,)),
                        pltpu.SemaphoreType.DMA],
    )(small)


def _adamw(w, g, m, v):
    m = ADAM_B1 * m + (1.0 - ADAM_B1) * g
    v = ADAM_B2 * v + (1.0 - ADAM_B2) * (g * g)
    m_hat = m / (1.0 - ADAM_B1 ** ADAM_STEP)
    v_hat = v / (1.0 - ADAM_B2 ** ADAM_STEP)
    return -ADAM_LR * (m_hat / (jnp.sqrt(v_hat) + ADAM_EPS) + ADAM_WD * w), m, v


def _slot_sum(ref, own=None, me=None):
    g = None
    for i in range(N_DEV):
        term = ref[i].astype(F32)
        if own is not None:
            term = jnp.where(i == me, own, term)
        g = term if g is None else g + term
    return g


def _update_matrix(name, me, landed, own, w, m, v, rows):
    r, c = w.shape

    def body(me_ref, l_ref, own_ref, w_ref, m_ref, v_ref, g_ref, d_ref, nm_ref, nv_ref):
        g = _slot_sum(l_ref, own_ref[...].astype(F32), me_ref[0])
        g_ref[...] = g
        d_ref[...], nm_ref[...], nv_ref[...] = _adamw(w_ref[...], g, m_ref[...], v_ref[...])

    blk = pl.BlockSpec((rows, c), lambda i, me_ref: (i, 0))
    return pl.pallas_call(
        body, name=name,
        grid_spec=pltpu.PrefetchScalarGridSpec(
            num_scalar_prefetch=1, grid=(r // rows,),
            in_specs=[pl.BlockSpec((N_DEV, rows, c), lambda i, me_ref: (0, i, 0)),
                      pl.BlockSpec((None, rows, c), lambda i, me_ref: (me_ref[0], i, 0)), blk, blk, blk],
            out_specs=[blk] * 4),
        out_shape=[jax.ShapeDtypeStruct((r, c), F32)] * 4,
        compiler_params=_params("parallel"),
    )(me, landed, own, w, m, v)


def _update_small(landed, lb_logits, ws, ms, vs):
    def body(l_ref, lbl_ref, w_ref, m_ref, v_ref, g_ref, d_ref, nm_ref, nv_ref, loss_ref):
        tot = _slot_sum(l_ref)
        _, dlb = _lower_bound(lbl_ref[...])
        g_lb = tot[1:2, :SEC_W] * dlb
        g = jnp.concatenate([tot[0:1], jnp.concatenate([g_lb, -g_lb], axis=-1),
                             jnp.pad(tot[1:2, SEC_W:], ((0, 0), (0, SEC_W))), tot[2:3]], axis=0)
        g_ref[...] = g
        d_ref[...], nm_ref[...], nv_ref[...] = _adamw(w_ref[...], g, m_ref[...], v_ref[...])
        loss_ref[...] = tot[3:4, 0:1]

    vmem = pl.BlockSpec(memory_space=pltpu.VMEM)
    return pl.pallas_call(
        body, name="update_small", in_specs=[vmem] * 5, out_specs=[vmem] * 5,
        out_shape=[jax.ShapeDtypeStruct((4, D_MODEL), F32)] * 4 + [jax.ShapeDtypeStruct((1, 1), F32)],
    )(landed, lb_logits, ws, ms, vs)


def _pack_small(norm_w, lb_logits, hg_norm_w, final_norm_w):
    return jnp.concatenate([norm_w, lb_logits.reshape(1, D_MODEL),
                            jnp.pad(hg_norm_w, ((0, 0), (0, D_MODEL - SEC_W))), final_norm_w[None, :]], axis=0)


def _unpack_small(a):
    return a[0:1], a[1].reshape(2, SEC_W), a[2:3, :SEC_W], a[3]


def kernel(x, norm_w, w_in, hgrn_lb_logits, hg_norm_w, w_out, final_norm_w, loss_target, m_norm_w, m_w_in, m_hgrn_lb_logits, m_hg_norm_w, m_w_out, m_final_norm_w, v_norm_w, v_w_in, v_hgrn_lb_logits, v_hg_norm_w, v_w_out, v_final_norm_w):
    w_all, wo_all = _gather_weights(w_in[0], w_out[0])
    in_flight = []

    def start_exchange(gwi, gwo):
        *handles, token = _exchange_start(gwi, gwo.reshape(N_DEV, D_MODEL // N_DEV, D_MODEL))
        in_flight.extend(handles)
        return token

    gx, _, _, small = _local_step(x[0], loss_target[0], norm_w, w_all, hgrn_lb_logits, hg_norm_w,
                                  wo_all.reshape(D_MODEL, D_MODEL), final_norm_w, start_exchange)
    ls = _gather_small(small)
    gwi, gwo, li, lo = _exchange_wait(*in_flight, gx)
    me = _me().astype(jnp.int32).reshape(1)
    g_wi, d_wi, nm_wi, nv_wi = _update_matrix("update_w_in", me, li, gwi, w_in[0], m_w_in[0], v_w_in[0], 256)
    g_wo, d_wo, nm_wo, nv_wo = _update_matrix("update_w_out", me, lo, gwo, w_out[0], m_w_out[0], v_w_out[0], 128)
    g_s, d_s, nm_s, nv_s, loss = _update_small(
        ls, hgrn_lb_logits, _pack_small(norm_w, hgrn_lb_logits, hg_norm_w, final_norm_w),
        _pack_small(m_norm_w, m_hgrn_lb_logits, m_hg_norm_w, m_final_norm_w),
        _pack_small(v_norm_w, v_hgrn_lb_logits, v_hg_norm_w, v_final_norm_w))
    outs = []
    for small_out, wi, wo in ((g_s, g_wi, g_wo), (d_s, d_wi, d_wo), (nm_s, nm_wi, nm_wo), (nv_s, nv_wi, nv_wo)):
        nw, lb, hg, fw = _unpack_small(small_out)
        outs += [nw, wi[None], lb, hg, wo[None], fw]
    return (loss[0, 0], gx[None], *outs)
```

```python
import functools

import jax
import jax.numpy as jnp
from jax import lax
from jax.experimental import pallas as pl
from jax.experimental.pallas import tpu as pltpu

F32 = jnp.float32
MM = jnp.bfloat16
XCH = jnp.bfloat16
NORM_EPS = 1e-6
NEG = -1e30
N_DEV = 8
D_MODEL = 1024
N_SEC = 8
SEC_W = 512
HG_HEADS = 4
HG_D = 128
AT_HEADS = 8
AT_DH = 64
ATT_BLK = 128
AT_COLS = 512
AT_QB = 4
DILATIONS = (1, 4, 16)
ROPE_THETA = 10000.0
CH = 16
LB_LO, LB_HI = 1e-6, 1.0 - 1e-6
ADAM_LR, ADAM_B1, ADAM_B2, ADAM_EPS, ADAM_WD, ADAM_STEP = 0.001, 0.9, 0.999, 1e-08, 0.01, 10
VMEM_LIMIT = 56 * 1024 * 1024
MESH = pl.DeviceIdType.MESH


def _params(*sem):
    return pltpu.CompilerParams(dimension_semantics=sem, vmem_limit_bytes=VMEM_LIMIT)


def _sigmoid(x):
    return 1.0 / (1.0 + jnp.exp(-x))


def _dot(a, b):
    return jnp.dot(a.astype(MM), b.astype(MM), preferred_element_type=F32)


def _dot_nt(a, b):
    return lax.dot_general(a.astype(MM), b.astype(MM), (((1,), (1,)), ((), ())), preferred_element_type=F32)


def _dot_tn(a, b):
    return lax.dot_general(a.astype(MM), b.astype(MM), (((0,), (0,)), ((), ())), preferred_element_type=F32)


def _tri_dot(tri, g):
    g1 = g.astype(jnp.bfloat16)
    r1 = g - g1.astype(F32)
    g2 = r1.astype(jnp.bfloat16)
    g3 = (r1 - g2.astype(F32)).astype(jnp.bfloat16)
    t = tri.astype(jnp.bfloat16)
    d = functools.partial(jnp.dot, preferred_element_type=F32)
    return d(t, g1) + d(t, g2) + d(t, g3)


def _lower_bound(lbl):
    l0, l1 = lbl[0:1, :], lbl[1:2, :]
    m = jnp.maximum(l0, l1)
    e0, e1 = jnp.exp(l0 - m), jnp.exp(l1 - m)
    p = e0 / (e0 + e1)
    inside = (p >= LB_LO) & (p <= LB_HI)
    return jnp.clip(p, LB_LO, LB_HI), jnp.where(inside, p * (e1 / (e0 + e1)), 0.0)


def _iota2(shape, dim):
    return lax.broadcasted_iota(jnp.int32, shape, dim)


def _hgrn_gates(xq, xf, lb):
    sgq = _sigmoid(xq)
    sg = _sigmoid(xf)
    sn = _sigmoid(-xf)
    f = lb + (1.0 - lb) * sg
    return sgq, xq * sgq, sg, sn, f, (1.0 - lb) * sn


def _bdot(a, b, ca, cb):
    return lax.dot_general(a.astype(MM), b.astype(MM), (((ca,), (cb,)), ((0,), (0,))), preferred_element_type=F32)


def _chunk_masks(rb):
    row, col = _iota2((rb, rb), 0), _iota2((rb, rb), 1)
    same = (row // CH) == (col // CH)
    return same & (row >= col), same & (row <= col)


def _hgrn_fwd(proj, lb_logits, rb=256):
    s = proj.shape[1]
    nb, nc = s // rb, rb // CH

    def body(q_ref, f_ref, i_ref, lbl_ref, o_ref, sst_ref, st_ref, slab_ref, states_ref):
        @pl.when(pl.program_id(1) == 0)
        def _():
            st_ref[...] = jnp.zeros_like(st_ref)

        sst_ref[...] = st_ref[...]
        lb, _ = _lower_bound(lbl_ref[...])
        prefix, _ = _chunk_masks(rb)
        c3 = lambda a: a.reshape(nc, CH, HG_D)
        _, q, _, _, f, kk = _hgrn_gates(q_ref[...], f_ref[...], lb)
        b3 = c3(_tri_dot(prefix, jnp.log(f)))
        q3, kk3, v3 = c3(q), c3(kk), c3(i_ref[...])
        bl3 = b3[:, CH - 1:CH, :]
        for t in range(CH):
            slab_ref[:, t * CH:(t + 1) * CH, :] = (q3 * jnp.exp(jnp.minimum(b3 - b3[:, t:t + 1, :], 0.0))).astype(MM)
        r = _bdot(slab_ref[...], kk3, 2, 2)
        row, col = _iota2((nc, CH, CH), 1), _iota2((nc, CH, CH), 2)
        a = jnp.zeros((nc, CH, CH), F32)
        for t in range(CH):
            a = a + jnp.where(col == t, r[:, t * CH:(t + 1) * CH, :], 0.0)
        a = jnp.where(row >= col, a, 0.0)
        x_upd = _bdot(v3, kk3 * jnp.exp(bl3 - b3), 1, 1)
        ebl3 = jnp.exp(bl3)
        st = st_ref[...]
        for c in range(nc):
            states_ref[c] = st
            st = st * ebl3[c] + x_upd[c]
        st_ref[...] = st
        o3 = _bdot(q3 * jnp.exp(b3), states_ref[...], 2, 2) + _bdot(a, v3, 2, 1)
        o_ref[...] = o3.reshape(rb, HG_D)

    sec = lambda j: pl.BlockSpec((None, rb, HG_D), lambda h, i, j=j: (j, i, h))
    return pl.pallas_call(
        body, name="hgrn_fwd", grid=(HG_HEADS, nb),
        in_specs=[sec(0), sec(1), sec(2), pl.BlockSpec((2, HG_D), lambda h, i: (0, h))],
        out_specs=[pl.BlockSpec((rb, HG_D), lambda h, i: (i, h)),
                   pl.BlockSpec((None, None, HG_D, HG_D), lambda h, i: (i, h, 0, 0))],
        out_shape=[jax.ShapeDtypeStruct((s, SEC_W), F32),
                   jax.ShapeDtypeStruct((nb, HG_HEADS, HG_D, HG_D), F32)],
        scratch_shapes=[pltpu.VMEM((HG_D, HG_D), F32), pltpu.VMEM((nc, CH * CH, HG_D), MM),
                        pltpu.VMEM((nc, HG_D, HG_D), F32)],
        compiler_params=_params("parallel", "arbitrary"),
    )(proj, proj, proj, lb_logits)


def _hgrn_bwd(proj, lb_logits, d_o, sst, rb=256):
    s = proj.shape[1]
    nb, nc = s // rb, rb // CH

    def body(q_ref, f_ref, i_ref, lbl_ref, do_ref, sst_ref, dxq_ref, dxf_ref, dxi_ref, dlb_ref,
             dst_ref, states_ref, dstates_ref, lslab_ref, kslab_ref):
        @pl.when(pl.program_id(1) == 0)
        def _():
            dst_ref[...] = jnp.zeros_like(dst_ref)
            dlb_ref[...] = jnp.zeros_like(dlb_ref)

        lb, _ = _lower_bound(lbl_ref[...])
        prefix, suffix = _chunk_masks(rb)
        c3 = lambda a: a.reshape(nc, CH, HG_D)
        flat = lambda a: a.reshape(rb, HG_D)
        xq = q_ref[...]
        sgq, q, sg, sn, f, kk = _hgrn_gates(xq, f_ref[...], lb)
        b3 = c3(_tri_dot(prefix, jnp.log(f)))
        q3, kk3, v3, do3 = c3(q), c3(kk), c3(i_ref[...]), c3(do_ref[...])
        bl3 = b3[:, CH - 1:CH, :]
        eb3, ebl3, dec3 = jnp.exp(b3), jnp.exp(bl3), jnp.exp(bl3 - b3)
        qe3, kd3 = q3 * eb3, kk3 * dec3
        x_upd, y_upd = _bdot(v3, kd3, 1, 1), _bdot(do3, qe3, 1, 1)
        st = sst_ref[...]
        for c in range(nc):
            states_ref[c] = st
            st = st * ebl3[c] + x_upd[c]
        dst = dst_ref[...]
        for c in reversed(range(nc)):
            dstates_ref[c] = dst
            dst = dst * ebl3[c] + y_upd[c]
        dst_ref[...] = dst
        states, dstates = states_ref[...], dstates_ref[...]
        dqe = _bdot(do3, states, 2, 1)
        dkd = _bdot(v3, dstates, 2, 1)
        row, col = _iota2((nc, CH, CH), 1), _iota2((nc, CH, CH), 2)
        tril, triu = row >= col, row <= col
        d_a = jnp.where(tril, _bdot(do3, v3, 2, 2), 0.0)
        d_at = jnp.where(triu, _bdot(v3, do3, 2, 2), 0.0)
        for t in range(CH):
            bt = b3[:, t:t + 1, :]
            lslab_ref[:, t * CH:(t + 1) * CH, :] = (q3 * jnp.exp(jnp.minimum(b3 - bt, 0.0))).astype(MM)
            kslab_ref[:, t * CH:(t + 1) * CH, :] = (kk3 * jnp.exp(jnp.minimum(bt - b3, 0.0))).astype(MM)
        r = _bdot(kslab_ref[...], q3, 2, 2)
        a_t = jnp.zeros((nc, CH, CH), F32)
        for t in range(CH):
            a_t = a_t + jnp.where(col == t, r[:, t * CH:(t + 1) * CH, :], 0.0)
        a_t = jnp.where(triu, a_t, 0.0)
        dv = _bdot(kd3, dstates, 2, 2) + _bdot(a_t, do3, 2, 1)
        sel = (_iota2((CH, CH * CH), 1) % CH == _iota2((CH, CH * CH), 0)).astype(MM)
        blockdiag = _iota2((nc, CH, CH * CH), 2) // CH == _iota2((nc, CH, CH * CH), 1)
        tile = lambda m: jnp.where(blockdiag, _dot(m.reshape(rb, CH), sel).reshape(nc, CH, CH * CH), 0.0)
        dq_in = _bdot(tile(d_a), kslab_ref[...], 2, 1)
        dk_in = _bdot(tile(d_at), lslab_ref[...], 2, 1)
        dkd_kd = dkd * kd3
        db = dqe * qe3 - dkd_kd + q3 * dq_in - kk3 * dk_in
        dbl = jnp.sum(dkd_kd, axis=1, keepdims=True) + jnp.sum(dstates * states, axis=1, keepdims=True) * ebl3
        last = _iota2((nc, CH, HG_D), 1) == CH - 1
        dg = _tri_dot(suffix, flat(db + jnp.where(last, dbl, 0.0)))
        df = dg / f - flat(dkd * dec3 + dk_in)
        dxq_ref[...] = (flat(dqe * eb3 + dq_in) * (sgq * (1.0 + xq * (1.0 - sgq)))).astype(MM)
        dxf_ref[...] = (df * (1.0 - lb) * sg * sn).astype(MM)
        dxi_ref[...] = flat(dv).astype(MM)
        dlb_ref[...] += jnp.sum(df * sn, axis=0, keepdims=True)

    rev = lambda i: nb - 1 - i
    sec = lambda j: pl.BlockSpec((None, rb, HG_D), lambda h, i, j=j: (j, rev(i), h))
    blk = pl.BlockSpec((rb, HG_D), lambda h, i: (rev(i), h))
    return pl.pallas_call(
        body, name="hgrn_bwd", grid=(HG_HEADS, nb),
        in_specs=[sec(0), sec(1), sec(2), pl.BlockSpec((2, HG_D), lambda h, i: (0, h)), blk,
                  pl.BlockSpec((None, None, HG_D, HG_D), lambda h, i: (rev(i), h, 0, 0))],
        out_specs=[blk, blk, blk, pl.BlockSpec((1, HG_D), lambda h, i: (0, h))],
        out_shape=[jax.ShapeDtypeStruct((s, SEC_W), MM)] * 3 + [jax.ShapeDtypeStruct((1, SEC_W), F32)],
        scratch_shapes=[pltpu.VMEM((HG_D, HG_D), F32), pltpu.VMEM((nc, HG_D, HG_D), F32),
                        pltpu.VMEM((nc, HG_D, HG_D), F32),
                        pltpu.VMEM((nc, CH * CH, HG_D), MM), pltpu.VMEM((nc, CH * CH, HG_D), MM)],
        compiler_params=_params("parallel", "arbitrary"),
    )(proj, proj, proj, lb_logits, d_o, sst)


def _rope_tables(s):
    half = AT_DH // 2
    inv_freq = 1.0 / (ROPE_THETA ** (jnp.arange(half, dtype=F32) / half))
    ang = jnp.arange(s, dtype=jnp.int32).astype(F32)[:, None] * inv_freq[None, :]
    cos, sin = jnp.cos(ang), jnp.sin(ang)
    return jnp.concatenate([cos] * 4, axis=-1), jnp.concatenate([-sin, sin] * 2, axis=-1)


def _rope128(x, cos, sin):
    lo = (_iota2(x.shape, 1) % AT_DH) < AT_DH // 2
    rot = jnp.where(lo, pltpu.roll(x, 128 - AT_DH // 2, 1), pltpu.roll(x, AT_DH // 2, 1))
    return x * cos + rot * sin


LANE_GROUPS = SEC_W // 128


def _set_lanes(ref, val):
    for j in range(LANE_GROUPS):
        ref[j] = val[:, j * 128:(j + 1) * 128]


def _get_lanes(ref):
    return jnp.concatenate([ref[j] for j in range(LANE_GROUPS)], axis=-1)


def _to_view(src_ref, dst_ref, d):
    n = src_ref.shape[1] // d
    for r in range(d):
        rows = pl.ds(r, n, stride=d) if d > 1 else slice(None)
        for j in range(LANE_GROUPS):
            c0 = r * SEC_W + j * 128
            dst_ref[:, c0:c0 + 128] = src_ref.at[j][rows, :].astype(dst_ref.dtype)


def _from_view(src_ref, dst_ref, d):
    n = dst_ref.shape[1] // d
    for r in range(d):
        for j in range(LANE_GROUPS):
            c0 = r * SEC_W + j * 128
            dst_ref.at[j][pl.ds(r, n, stride=d), :] = src_ref[:, c0:c0 + 128].astype(dst_ref.dtype)


def _view_spec(tm, d):
    return pl.BlockSpec((tm // d, d * SEC_W), lambda i: (i, 0))


def _view_shape(s, d, dtype):
    return jax.ShapeDtypeStruct((s // d, d * SEC_W), dtype)


PROJ_KEPT = (0, 1, 2, 3, 7)


def _inproj_fwd(x, norm_w, w_all, cos, sin, tm=256):
    s = x.shape[0]

    def body(x_ref, nw_ref, w_ref, cos_ref, sin_ref, proj_ref, *refs):
        outs, (qs_ref, ks_ref, vs_ref) = refs[:-3], refs[-3:]
        xv = x_ref[...]
        rstd = lax.rsqrt(jnp.mean(xv * xv, axis=-1, keepdims=True) + NORM_EPS)
        u = (xv * rstd * nw_ref[...]).astype(MM)
        for slot, j in enumerate(PROJ_KEPT):
            proj_ref[slot] = jnp.dot(u, w_ref[j], preferred_element_type=F32)
        q, k, v = [jnp.dot(u, w_ref[j], preferred_element_type=F32) for j in (4, 5, 6)]
        c, sn = cos_ref[...], sin_ref[...]
        for g in range(LANE_GROUPS):
            sl = slice(g * 128, (g + 1) * 128)
            qs_ref[g] = _rope128(q[:, sl], c, sn) * (AT_DH ** -0.5)
            ks_ref[g] = _rope128(k[:, sl], c, sn)
            vs_ref[g] = v[:, sl]
        for i, d in enumerate(DILATIONS):
            for src_ref, dst_ref in zip((qs_ref, ks_ref, vs_ref), outs[3 * i:3 * i + 3]):
                _to_view(src_ref, dst_ref, d)

    tab = pl.BlockSpec((tm, 128), lambda i: (i, 0))
    return pl.pallas_call(
        body, name="inproj_fwd", grid=(s // tm,),
        in_specs=[pl.BlockSpec((tm, D_MODEL), lambda i: (i, 0)),
                  pl.BlockSpec((1, D_MODEL), lambda i: (0, 0)),
                  pl.BlockSpec((N_SEC, D_MODEL, SEC_W), lambda i: (0, 0, 0)), tab, tab],
        out_specs=[pl.BlockSpec((len(PROJ_KEPT), tm, SEC_W), lambda i: (0, i, 0))]
                  + [_view_spec(tm, d) for d in DILATIONS for _ in range(3)],
        out_shape=[jax.ShapeDtypeStruct((len(PROJ_KEPT), s, SEC_W), F32)]
                  + [_view_shape(s, d, MM) for d in DILATIONS for _ in range(3)],
        scratch_shapes=[pltpu.VMEM((LANE_GROUPS, tm, 128), F32)] * 3,
        compiler_params=_params("parallel"),
    )(x, norm_w, w_all, cos, sin)


def _band_mask(first_ok, second_ok):
    row, col = _iota2((ATT_BLK, 2 * ATT_BLK), 0), _iota2((ATT_BLK, 2 * ATT_BLK), 1)
    return ((col < ATT_BLK) & (col >= row) & first_ok) | ((col >= ATT_BLK) & ((col - ATT_BLK) <= row) & second_ok)


def _own_lanes(rows, h):
    lane = _iota2((rows, 128), 1)
    return (lane < AT_DH) if h == 0 else (lane >= AT_DH)


def _neg_pieces(rows, h):
    lane = _iota2((rows, 128), 1) - (AT_DH if h == 0 else 0)
    return jnp.where((lane >= 0) & (lane < 3), -1.0, 0.0).astype(MM)


def _units():
    return [(b, slice(g * 128, (g + 1) * 128), h) for b in range(AT_QB) for g in range(AT_COLS // 128) for h in range(2)]


def _sub(b):
    return slice(b * ATT_BLK, (b + 1) * ATT_BLK)


def _band_before(cur_ref, prev_ref, b, sl):
    if b == 0:
        return jnp.concatenate([prev_ref[:, sl], cur_ref[0:ATT_BLK, sl]], axis=0)
    return cur_ref[(b - 1) * ATT_BLK:(b + 1) * ATT_BLK, sl]


def _band_after(cur_ref, next_ref, b, sl):
    if b == AT_QB - 1:
        return jnp.concatenate([cur_ref[b * ATT_BLK:(b + 1) * ATT_BLK, sl], next_ref[:, sl]], axis=0)
    return cur_ref[b * ATT_BLK:(b + 2) * ATT_BLK, sl]


def _attn_specs(rows):
    assert rows % (AT_QB * ATT_BLK) == 0
    last = rows // ATT_BLK - 1
    cur = pl.BlockSpec((AT_QB * ATT_BLK, AT_COLS), lambda c, n: (n, c))
    prev = pl.BlockSpec((ATT_BLK, AT_COLS), lambda c, n: (jnp.maximum(AT_QB * n - 1, 0), c))
    nxt = pl.BlockSpec((ATT_BLK, AT_COLS), lambda c, n: (jnp.minimum(AT_QB * (n + 1), last), c))
    return cur, prev, nxt


def _attn_fwd(qr, kr, vr, d):
    rows, cols = qr.shape
    nb = rows // (AT_QB * ATT_BLK)

    def body(q_ref, kc_ref, kp_ref, vc_ref, vp_ref, o_ref, lse_ref):
        masks = {True: _band_mask(pl.program_id(1) > 0, True), False: _band_mask(True, True)}
        ones = jnp.ones((2 * ATT_BLK, 128), MM)
        head0 = _own_lanes(ATT_BLK, 0)
        units = _units()
        scs = []
        for b, sl, h in units:
            q2 = q_ref[_sub(b), sl]
            qh = jnp.where(_own_lanes(ATT_BLK, h), q2, jnp.zeros_like(q2))
            scs.append(jnp.where(masks[b == 0], _dot_nt(qh, _band_before(kc_ref, kp_ref, b, sl)), NEG))
        ms = [jnp.max(sc, axis=-1, keepdims=True) for sc in scs]
        ps = [jnp.exp(sc - m).astype(MM) for sc, m in zip(scs, ms)]
        ls = [jnp.dot(p, ones, preferred_element_type=F32) for p in ps]
        os_ = [jnp.dot(p, _band_before(vc_ref, vp_ref, b, sl), preferred_element_type=F32)
               for p, (b, sl, _) in zip(ps, units)]
        for i in range(0, len(units), 2):
            b, sl, _ = units[i]
            l = jnp.where(head0, ls[i], ls[i + 1])
            o_ref[_sub(b), sl] = jnp.where(head0, os_[i], os_[i + 1]) / l
            lse_ref[_sub(b), sl] = jnp.where(head0, ms[i], ms[i + 1]) + jnp.log(l)

    cur, prev, _ = _attn_specs(rows)
    o, lse = pl.pallas_call(
        body, name=f"attn_fwd_d{d}", grid=(cols // AT_COLS, nb),
        in_specs=[cur, cur, prev, cur, prev], out_specs=[cur, cur],
        out_shape=[jax.ShapeDtypeStruct((rows, cols), F32)] * 2,
        compiler_params=_params("parallel", "parallel"),
    )(qr, kr, kr, vr, vr)
    return o, lse


def _attn_bwd_dq(qr, kr, vr, do, lse, delta, d):
    rows, cols = qr.shape
    nb = rows // (AT_QB * ATT_BLK)

    def body(q_ref, kc_ref, kp_ref, vc_ref, vp_ref, do_ref, lse_ref, dl_ref, dq_ref):
        masks = {True: _band_mask(pl.program_id(1) > 0, True), False: _band_mask(True, True)}
        units = _units()
        sms, dps = [], []
        for b, sl, h in units:
            own, own_b, neg = _own_lanes(ATT_BLK, h), _own_lanes(2 * ATT_BLK, h), _neg_pieces(2 * ATT_BLK, h)
            sms.append(_dot_nt(jnp.where(own, q_ref[_sub(b), sl], lse_ref[_sub(b), sl]),
                               jnp.where(own_b, _band_before(kc_ref, kp_ref, b, sl), neg)))
            dps.append(_dot_nt(jnp.where(own, do_ref[_sub(b), sl], dl_ref[_sub(b), sl]),
                               jnp.where(own_b, _band_before(vc_ref, vp_ref, b, sl), neg)))
        dss = [(jnp.exp(jnp.where(masks[b == 0], sm, NEG)) * dp).astype(MM)
               for sm, dp, (b, _, _) in zip(sms, dps, units)]
        dqs = [jnp.dot(ds, _band_before(kc_ref, kp_ref, b, sl), preferred_element_type=F32) * (AT_DH ** -0.5)
               for ds, (b, sl, _) in zip(dss, units)]
        for i in range(0, len(units), 2):
            b, sl, _ = units[i]
            dq_ref[_sub(b), sl] = jnp.where(_own_lanes(ATT_BLK, 0), dqs[i], dqs[i + 1]).astype(dq_ref.dtype)

    cur, prev, _ = _attn_specs(rows)
    dq = pl.pallas_call(
        body, name=f"attn_bwd_dq_d{d}", grid=(cols // AT_COLS, nb),
        in_specs=[cur, cur, prev, cur, prev, cur, cur, cur], out_specs=cur,
        out_shape=jax.ShapeDtypeStruct((rows, cols), MM),
        compiler_params=_params("parallel", "parallel"),
    )(qr, kr, kr, vr, vr, do, lse, delta)
    return dq


def _attn_bwd_dkv(qr, kr, vr, do, lse, delta, d):
    rows, cols = qr.shape
    nb = rows // (AT_QB * ATT_BLK)

    def body(k_ref, v_ref, qc_ref, qn_ref, doc_ref, don_ref, lsec_ref, lsen_ref, dlc_ref, dln_ref,
             dk_ref, dv_ref):
        masks = {True: _band_mask(True, pl.program_id(1) < nb - 1), False: _band_mask(True, True)}
        units = _units()
        sms, dps = [], []
        for b, sl, h in units:
            own, own_b, neg = _own_lanes(ATT_BLK, h), _own_lanes(2 * ATT_BLK, h), _neg_pieces(ATT_BLK, h)
            sms.append(_dot_nt(jnp.where(own, k_ref[_sub(b), sl], neg),
                               jnp.where(own_b, _band_after(qc_ref, qn_ref, b, sl),
                                         _band_after(lsec_ref, lsen_ref, b, sl))))
            dps.append(_dot_nt(jnp.where(own, v_ref[_sub(b), sl], neg),
                               jnp.where(own_b, _band_after(doc_ref, don_ref, b, sl),
                                         _band_after(dlc_ref, dln_ref, b, sl))))
        ps = [jnp.exp(jnp.where(masks[b == AT_QB - 1], sm, NEG)) for sm, (b, _, _) in zip(sms, units)]
        dss = [(p * dp).astype(MM) for p, dp in zip(ps, dps)]
        dvs = [jnp.dot(p.astype(MM), _band_after(doc_ref, don_ref, b, sl), preferred_element_type=F32)
               for p, (b, sl, _) in zip(ps, units)]
        dks = [jnp.dot(ds, _band_after(qc_ref, qn_ref, b, sl), preferred_element_type=F32)
               for ds, (b, sl, _) in zip(dss, units)]
        head0 = _own_lanes(ATT_BLK, 0)
        for i in range(0, len(units), 2):
            b, sl, _ = units[i]
            dk_ref[_sub(b), sl] = jnp.where(head0, dks[i], dks[i + 1]).astype(dk_ref.dtype)
            dv_ref[_sub(b), sl] = jnp.where(head0, dvs[i], dvs[i + 1]).astype(dv_ref.dtype)

    cur, _, nxt = _attn_specs(rows)
    dk, dv = pl.pallas_call(
        body, name=f"attn_bwd_dkv_d{d}", grid=(cols // AT_COLS, nb),
        in_specs=[cur, cur, cur, nxt, cur, nxt, cur, nxt, cur, nxt], out_specs=[cur, cur],
        out_shape=[jax.ShapeDtypeStruct((rows, cols), MM)] * 2,
        compiler_params=_params("parallel", "parallel"),
    )(kr, vr, qr, qr, do, do, lse, lse, delta, delta)
    return dk, dv


def _head_sum(a, width):
    parts = []
    for j in range(a.shape[1] // width):
        sm = jnp.sum(a[:, j * width:(j + 1) * width], axis=-1, keepdims=True)
        parts.append(jnp.broadcast_to(sm, (a.shape[0], width)))
    return jnp.concatenate(parts, axis=-1)


def _partner_pieces(x):
    xs = jnp.concatenate([pltpu.roll(x[:, j * 128:(j + 1) * 128], AT_DH, 1) for j in range(x.shape[1] // 128)],
                         axis=-1)
    hi = xs.astype(jnp.bfloat16).astype(F32)
    mid = (xs - hi).astype(jnp.bfloat16).astype(F32)
    lo = (xs - hi - mid).astype(jnp.bfloat16).astype(F32)
    lane = _iota2(x.shape, 1) % AT_DH
    return jnp.where(lane == 0, hi, jnp.where(lane == 1, mid, jnp.where(lane == 2, lo, 0.0)))


def _mid(x, tgt, proj, o_hg, o_at, lse_at, hg_norm_w, final_norm_w, wo_all, tm=256):
    s = x.shape[0]
    nb = s // tm

    def body(x_ref, t_ref, hgz_ref, atz_ref, ohg_ref, o1_ref, o2_ref, o3_ref, l1_ref, l2_ref, l3_ref,
             g_ref, fw_ref, wo_ref,
             dh_ref, dohg_ref, dhgz_ref, datz_ref, do1_ref, do2_ref, do3_ref, dl1_ref, dl2_ref, dl3_ref,
             lp1_ref, lp2_ref, lp3_ref,
             gwo_ref, gfw_ref, ghg_ref, loss_ref, nat_ref, stage_ref, gwo_acc):
        @pl.when(pl.program_id(0) == 0)
        def _():
            gwo_acc[...] = jnp.zeros_like(gwo_acc)
            gfw_ref[...] = jnp.zeros_like(gfw_ref)
            ghg_ref[...] = jnp.zeros_like(ghg_ref)
            loss_ref[...] = jnp.zeros_like(loss_ref)

        ohg, g = ohg_ref[...], g_ref[...]
        rs = lax.rsqrt(_head_sum(ohg * ohg, HG_D) * (1.0 / HG_D) + NORM_EPS)
        on = ohg * rs
        hgz = hgz_ref[...]
        sz = _sigmoid(hgz)
        gate_hg = hgz * sz
        lses, outs = [l1_ref[...]], [o1_ref[...]]
        for k, (d, l_ref, o_ref) in enumerate(zip(DILATIONS[1:], (l2_ref, l3_ref), (o2_ref, o3_ref))):
            _from_view(l_ref, nat_ref.at[2 * k], d)
            _from_view(o_ref, nat_ref.at[2 * k + 1], d)
            lses.append(_get_lanes(nat_ref.at[2 * k]))
            outs.append(_get_lanes(nat_ref.at[2 * k + 1]))
        mx = jnp.maximum(jnp.maximum(lses[0], lses[1]), lses[2])
        es = [jnp.exp(l - mx) for l in lses]
        den = es[0] + es[1] + es[2]
        ws = [e / den for e in es]
        oat = ws[0] * outs[0] + ws[1] * outs[1] + ws[2] * outs[2]
        atz = atz_ref[...]
        sa = _sigmoid(atz)
        gate_at = atz * sa
        mixed = jnp.concatenate([on * g * gate_hg, oat * gate_at], axis=-1).astype(MM)
        h = x_ref[...] + jnp.dot(mixed, wo_ref[...], preferred_element_type=F32)
        rstd = lax.rsqrt(jnp.mean(h * h, axis=-1, keepdims=True) + NORM_EPS)
        hn = h * rstd
        fw = fw_ref[...]
        err = hn * fw - t_ref[...]
        loss_ref[...] += 0.5 * jnp.sum(jnp.mean(err * err, axis=-1, keepdims=True), axis=0, keepdims=True)
        dout = err * (1.0 / D_MODEL)
        gfw_ref[...] += jnp.sum(dout * hn, axis=0, keepdims=True)
        dhn = dout * fw
        dh = rstd * (dhn - hn * jnp.mean(dhn * hn, axis=-1, keepdims=True))
        dh_ref[...] = dh
        dh_mm = dh.astype(MM)
        gwo_acc[...] += _dot_tn(mixed, dh_mm)

        @pl.when(pl.program_id(0) == nb - 1)
        def _():
            gwo_ref[...] = gwo_acc[...].astype(gwo_ref.dtype)

        dmixed = _dot_nt(dh_mm, wo_ref[...])
        dm_hg = dmixed[:, :SEC_W]
        d_ong = dm_hg * gate_hg
        dhgz_ref[...] = (dm_hg * (on * g) * (sz * (1.0 + hgz * (1.0 - sz)))).astype(MM)
        ghg_ref[...] += jnp.sum(d_ong * on, axis=0, keepdims=True)
        d_on = d_ong * g
        dohg_ref[...] = rs * (d_on - on * (_head_sum(d_on * on, HG_D) * (1.0 / HG_D)))
        dm_at = dmixed[:, SEC_W:]
        d_oat = dm_at * gate_at
        datz_ref[...] = (dm_at * oat * (sa * (1.0 + atz * (1.0 - sa)))).astype(MM)
        drow = _head_sum(d_oat * oat, AT_DH)
        lse_all = mx + jnp.log(den)
        for val, dst_refs in ((d_oat, (do1_ref, do2_ref, do3_ref)),
                              (_partner_pieces(drow), (dl1_ref, dl2_ref, dl3_ref)),
                              (_partner_pieces(lse_all), (lp1_ref, lp2_ref, lp3_ref))):
            _set_lanes(stage_ref, val)
            for d, dst_ref in zip(DILATIONS, dst_refs):
                _to_view(stage_ref, dst_ref, d)

    row = lambda w: pl.BlockSpec((tm, w), lambda i: (i, 0))
    sec = lambda j: pl.BlockSpec((None, tm, SEC_W), lambda i, j=j: (j, i, 0))
    const = lambda shp: pl.BlockSpec(shp, lambda i: (0,) * len(shp))
    half = row(SEC_W)
    views = [_view_spec(tm, d) for d in DILATIONS]
    return pl.pallas_call(
        body, name="mid", grid=(nb,),
        in_specs=[row(D_MODEL), row(D_MODEL), sec(PROJ_KEPT.index(3)), sec(PROJ_KEPT.index(7)), half] + views * 2
                 + [const((1, SEC_W)), const((1, D_MODEL)), const((D_MODEL, D_MODEL))],
        out_specs=[row(D_MODEL)] + [half] * 3 + views * 3
                  + [const((D_MODEL, D_MODEL)), const((1, D_MODEL)), const((1, SEC_W)), const((1, 1))],
        out_shape=[jax.ShapeDtypeStruct((s, D_MODEL), F32), jax.ShapeDtypeStruct((s, SEC_W), F32)]
                  + [jax.ShapeDtypeStruct((s, SEC_W), MM)] * 2
                  + [_view_shape(s, d, MM) for d in DILATIONS] * 3
                  + [jax.ShapeDtypeStruct((D_MODEL, D_MODEL), XCH), jax.ShapeDtypeStruct((1, D_MODEL), F32),
                     jax.ShapeDtypeStruct((1, SEC_W), F32), jax.ShapeDtypeStruct((1, 1), F32)],
        scratch_shapes=[pltpu.VMEM((4, LANE_GROUPS, tm, 128), F32), pltpu.VMEM((LANE_GROUPS, tm, 128), F32),
                        pltpu.VMEM((D_MODEL, D_MODEL), F32)],
        compiler_params=_params("arbitrary"),
    )(x, tgt, proj, proj, o_hg, *o_at, *lse_at, hg_norm_w, final_norm_w, wo_all)


def _section_specs(dsecs, tm):
    return [pl.BlockSpec((tm, SEC_W), lambda i: (i, 0)) if k is None
            else pl.BlockSpec((None, tm, SEC_W), lambda i, k=k: (k, i, 0)) for _, k in dsecs]


def _inproj_bwd_x(x, norm_w, w_all, dh, dsecs, token, tm=256):
    s = x.shape[0]

    def body(x_ref, nw_ref, w_ref, dh_ref, tok_ref, *refs):
        sec_refs, (gx_ref, gnw_ref) = refs[:N_SEC], refs[N_SEC:]

        @pl.when(pl.program_id(0) == 0)
        def _():
            gnw_ref[...] = jnp.zeros_like(gnw_ref)

        du = jnp.zeros((tm, D_MODEL), F32)
        for j in range(N_SEC):
            du = du + _dot_nt(sec_refs[j][...], w_ref[j])
        xv, nw = x_ref[...], nw_ref[...]
        rstd = lax.rsqrt(jnp.mean(xv * xv, axis=-1, keepdims=True) + NORM_EPS)
        xn = xv * rstd
        gnw_ref[...] += jnp.sum(du * xn, axis=0, keepdims=True)
        dxn = du * nw
        dx = rstd * (dxn - xn * jnp.mean(dxn * xn, axis=-1, keepdims=True))
        gx_ref[...] = (dh_ref[...] + tok_ref[0:1, 0:1]) + dx

    row = lambda w: pl.BlockSpec((tm, w), lambda i: (i, 0))
    const = lambda shp: pl.BlockSpec(shp, lambda i: (0,) * len(shp))
    return pl.pallas_call(
        body, name="inproj_bwd_x", grid=(s // tm,),
        in_specs=[row(D_MODEL), const((1, D_MODEL)), const((N_SEC, D_MODEL, SEC_W)), row(D_MODEL), const((8, 128))]
                 + _section_specs(dsecs, tm),
        out_specs=[row(D_MODEL), const((1, D_MODEL))],
        out_shape=[jax.ShapeDtypeStruct((s, D_MODEL), F32), jax.ShapeDtypeStruct((1, D_MODEL), F32)],
        compiler_params=_params("arbitrary"),
    )(x, norm_w, w_all, dh, token, *[a for a, _ in dsecs])


def _inproj_bwd_w(x, norm_w, dsec, dq_r, dk_r, dv, cos, sin, tm=512):
    s = x.shape[0]
    nb = s // tm

    def body(x_ref, nw_ref, s0, s1, s2, s3, s7, q1, q2, q3, k1, k2, k3, v1, v2, v3, cos_ref, sin_ref,
             gw_hbm, datt_ref, acc_ref, stage_ref, nat_ref):
        @pl.when(pl.program_id(0) == 0)
        def _():
            acc_ref[...] = jnp.zeros_like(acc_ref)

        def total(refs):
            acc = refs[0][...].astype(F32)
            for d, ref in zip(DILATIONS[1:], refs[1:]):
                _from_view(ref, nat_ref, d)
                acc = acc + _get_lanes(nat_ref)
            return acc

        c, sn = cos_ref[...], -sin_ref[...]
        unrot = lambda a: jnp.concatenate(
            [_rope128(a[:, j * 128:(j + 1) * 128], c, sn) for j in range(LANE_GROUPS)], axis=-1)
        att = [a.astype(MM) for a in (unrot(total((q1, q2, q3))), unrot(total((k1, k2, k3))), total((v1, v2, v3)))]
        for j, a in enumerate(att):
            datt_ref[j] = a
        xv = x_ref[...]
        rstd = lax.rsqrt(jnp.mean(xv * xv, axis=-1, keepdims=True) + NORM_EPS)
        u_t = (xv * rstd * nw_ref[...]).T.astype(MM)
        for j, dsj in enumerate((s0[...], s1[...], s2[...], s3[...], *att, s7[...])):
            acc_ref[j] += jnp.dot(u_t, dsj, preferred_element_type=F32)

        @pl.when(pl.program_id(0) == nb - 1)
        def _():
            for j in range(N_SEC):
                stage_ref[...] = acc_ref[j].astype(stage_ref.dtype)
                pltpu.sync_copy(stage_ref, gw_hbm.at[j])

    row = lambda w: pl.BlockSpec((tm, w), lambda i: (i, 0))
    return pl.pallas_call(
        body, name="inproj_bwd_w", grid=(nb,),
        in_specs=[row(D_MODEL), pl.BlockSpec((1, D_MODEL), lambda i: (0, 0))] + [row(SEC_W)] * 5
                 + [_view_spec(tm, d) for d in DILATIONS] * 3 + [row(128), row(128)],
        out_specs=[pl.BlockSpec(memory_space=pl.ANY), pl.BlockSpec((3, tm, SEC_W), lambda i: (0, i, 0))],
        out_shape=[jax.ShapeDtypeStruct((N_SEC, D_MODEL, SEC_W), XCH), jax.ShapeDtypeStruct((3, s, SEC_W), MM)],
        scratch_shapes=[pltpu.VMEM((N_SEC, D_MODEL, SEC_W), F32), pltpu.VMEM((D_MODEL, SEC_W), XCH),
                        pltpu.VMEM((LANE_GROUPS, tm, 128), F32)],
        compiler_params=_params("arbitrary"),
    )(x, norm_w, *dsec, *dq_r, *dk_r, *dv, cos, sin)


def _local_step(x, tgt, norm_w, w_all, lb_logits, hg_norm_w, wo_all, final_norm_w, on_weight_grads):
    s = x.shape[0]
    cos, sin = _rope_tables(s)
    proj, *qkv = _inproj_fwd(x, norm_w, w_all, cos, sin)
    o_hg, sst = _hgrn_fwd(proj, lb_logits)
    qkv = [qkv[3 * i:3 * i + 3] for i in range(len(DILATIONS))]
    att = [_attn_fwd(*qkv_d, d) for qkv_d, d in zip(qkv, DILATIONS)]
    (dh, d_ohg, d_hgz, d_atz, do1, do2, do3, dl1, dl2, dl3, lp1, lp2, lp3, gwo, gfw, ghg, loss) = _mid(
        x, tgt, proj, o_hg, [a[0] for a in att], [a[1] for a in att], hg_norm_w, final_norm_w[None, :], wo_all)
    dxq, dxf, dxi, dlb = _hgrn_bwd(proj, lb_logits, d_ohg, sst)
    dq_r, dk_r, dv = [], [], []
    for d, qkv_d, do, lp, dl in zip(DILATIONS, qkv, (do1, do2, do3), (lp1, lp2, lp3), (dl1, dl2, dl3)):
        dq_r.append(_attn_bwd_dq(*qkv_d, do, lp, dl, d))
        dk_d, dv_d = _attn_bwd_dkv(*qkv_d, do, lp, dl, d)
        dk_r.append(dk_d)
        dv.append(dv_d)
    gwi, d_att = _inproj_bwd_w(x, norm_w, (dxq, dxf, dxi, d_hgz, d_atz), dq_r, dk_r, dv, cos, sin)
    dsecs = [(dxq, None), (dxf, None), (dxi, None), (d_hgz, None), (d_att, 0), (d_att, 1), (d_att, 2), (d_atz, None)]
    token = on_weight_grads(gwi, gwo)
    gx, gnw = _inproj_bwd_x(x, norm_w, w_all, dh, dsecs, token)
    small = jnp.concatenate([gnw, jnp.concatenate([dlb, ghg], axis=-1), gfw,
                             jnp.pad(loss, ((0, 0), (0, D_MODEL - 1)))], axis=0)
    return gx, gwi, gwo, small


def _coords():
    return lax.axis_index("x"), lax.axis_index("y"), lax.axis_index("c")


def _gather_weights(w_in, w_out):
    wo_rows = w_out.shape[0]

    def body(wi_ref, wo_ref, wi_all, wo_all, send_sems, recv_sems):
        x, y, c = _coords()
        me, sibling = (x, y, c), (x, y, 1 - c)
        chips = [(1 - x, y), (x, 1 - y), (1 - x, 1 - y)]
        slot = lambda p: 4 * p[0] + 2 * p[1] + p[2]

        def copies(k, block, to):
            return [pltpu.make_async_remote_copy(
                src_ref=ref.at[slot(block)], dst_ref=ref.at[slot(block)], send_sem=send_sems.at[a, k],
                recv_sem=recv_sems.at[a, k], device_id=to, device_id_type=MESH)
                for a, ref in enumerate((wi_all, wo_all))]

        wi_all[slot(me)] = wi_ref[...].astype(MM)
        wo_all[slot(me)] = wo_ref[...].astype(MM)
        first = copies(0, me, sibling)
        for j, chip in enumerate(chips):
            first += copies(1 + j, me, (*chip, c))
        for cp in first:
            cp.start()
        passed = []
        for j, chip in enumerate(chips):
            for cp in copies(1 + j, (*chip, c), me):
                cp.wait_recv()
            fwd = copies(4 + j, (*chip, c), sibling)
            for cp in fwd:
                cp.start()
            passed += fwd
        for cp in copies(0, sibling, me):
            cp.wait_recv()
        for j, chip in enumerate(chips):
            for cp in copies(4 + j, (*chip, 1 - c), me):
                cp.wait_recv()
        for cp in first + passed:
            cp.wait_send()

    vmem = pl.BlockSpec(memory_space=pltpu.VMEM)
    return pl.pallas_call(
        body, name="gather_weights",
        in_specs=[vmem, vmem], out_specs=[vmem, vmem],
        out_shape=[jax.ShapeDtypeStruct((N_DEV, D_MODEL, SEC_W), MM),
                   jax.ShapeDtypeStruct((N_DEV, wo_rows, D_MODEL), MM)],
        scratch_shapes=[pltpu.SemaphoreType.DMA((2, 7)), pltpu.SemaphoreType.DMA((2, 7))],
        compiler_params=pltpu.CompilerParams(vmem_limit_bytes=VMEM_LIMIT),
    )(w_in, w_out)


def _me():
    x, y, c = _coords()
    return 4 * x + 2 * y + c


def _grad_copies(srcs, lands, send_sems, recv_sems):
    x, y, c = _coords()
    me = 4 * x + 2 * y + c
    copies = []
    for k in range(1, N_DEV):
        px, py, pc = x ^ (k >> 2), y ^ ((k >> 1) & 1), c ^ (k & 1)
        peer = 4 * px + 2 * py + pc
        for a, (src, dst) in enumerate(zip(srcs, lands)):
            copies.append(pltpu.make_async_remote_copy(
                src_ref=src.at[peer], dst_ref=dst.at[me], send_sem=send_sems.at[a * (N_DEV - 1) + k - 1],
                recv_sem=recv_sems.at[a * (N_DEV - 1) + k - 1], device_id=(px, py, pc), device_id_type=MESH))
    return copies


HBM_SPEC = pl.BlockSpec(memory_space=pltpu.HBM)
SEM_SPEC = pl.BlockSpec(memory_space=pltpu.SEMAPHORE)
SPLIT_COPY_EFFECT = pltpu.SideEffectType.DATAFLOW_SIDE_EFFECTING


def _exchange_start(gwi, gwo):
    def body(gwi_ref, gwo_ref, li_ref, lo_ref, send_sems, recv_sems, gwi_thru, gwo_thru, li_thru, lo_thru, token):
        for cp in _grad_copies((gwi_ref, gwo_ref), (li_ref, lo_ref), send_sems, recv_sems):
            cp.start()
        token[...] = jnp.zeros_like(token)

    hbm = lambda a: pltpu.with_memory_space_constraint(a, pltpu.HBM)
    bufs = (gwi, gwo, lax.empty(gwi.shape, gwi.dtype), lax.empty(gwo.shape, gwo.dtype))
    return pl.pallas_call(
        body, name="exchange_start",
        out_shape=(pltpu.SemaphoreType.DMA((2 * (N_DEV - 1),)), pltpu.SemaphoreType.DMA((2 * (N_DEV - 1),)),
                   *[pltpu.HBM(a.shape, a.dtype) for a in bufs], jax.ShapeDtypeStruct((8, 128), F32)),
        in_specs=[HBM_SPEC] * 4,
        out_specs=(SEM_SPEC, SEM_SPEC, HBM_SPEC, HBM_SPEC, HBM_SPEC, HBM_SPEC, pl.BlockSpec(memory_space=pltpu.VMEM)),
        input_output_aliases={0: 2, 1: 3, 2: 4, 3: 5},
        compiler_params=pltpu.CompilerParams(has_side_effects=SPLIT_COPY_EFFECT),
    )(*[hbm(a) for a in bufs])


def _exchange_wait(send_sems, recv_sems, gwi, gwo, li, lo, after):
    def body(gwi_ref, gwo_ref, li_ref, lo_ref, send_sems, recv_sems, after_ref, gwi_out, gwo_out, li_out, lo_out):
        for cp in _grad_copies((gwi_ref, gwo_ref), (li_ref, lo_ref), send_sems, recv_sems):
            cp.wait_send()
            cp.wait_recv()

    return pl.pallas_call(
        body, name="exchange_wait",
        out_shape=tuple(pltpu.HBM(a.shape, a.dtype) for a in (gwi, gwo, li, lo)),
        in_specs=[HBM_SPEC] * 4 + [SEM_SPEC, SEM_SPEC, pl.BlockSpec(memory_space=pl.ANY)],
        out_specs=(HBM_SPEC,) * 4,
        input_output_aliases={0: 0, 1: 1, 2: 2, 3: 3},
        compiler_params=pltpu.CompilerParams(has_side_effects=SPLIT_COPY_EFFECT),
    )(gwi, gwo, li, lo, send_sems, recv_sems, after)


def _gather_small(small):
    def body(sm_ref, ls_ref, send_sems, recv_sems, local_sem):
        x, y, c = _coords()
        me = 4 * x + 2 * y + c
        own = pltpu.make_async_copy(sm_ref, ls_ref.at[me], local_sem)
        own.start()
        sends = []
        for k in range(1, N_DEV):
            peer = (x ^ (k >> 2), y ^ ((k >> 1) & 1), c ^ (k & 1))
            sends.append(pltpu.make_async_remote_copy(
                src_ref=sm_ref, dst_ref=ls_ref.at[me], send_sem=send_sems.at[k - 1], recv_sem=recv_sems.at[k - 1],
                device_id=peer, device_id_type=MESH))
        for cp in sends:
            cp.start()
        for cp in sends:
            cp.wait_recv()
        for cp in sends:
            cp.wait_send()
        own.wait()

    vmem = pl.BlockSpec(memory_space=pltpu.VMEM)
    return pl.pallas_call(
        body, name="gather_small", in_specs=[vmem], out_specs=vmem,
        out_shape=jax.ShapeDtypeStruct((N_DEV,) + small.shape, F32),
        scratch_shapes=[pltpu.SemaphoreType.DMA((N_DEV - 1,)), pltpu.SemaphoreType.DMA((N_DEV - 1,)),
                        pltpu.SemaphoreType.DMA],
    )(small)


def _adamw(w, g, m, v):
    m = ADAM_B1 * m + (1.0 - ADAM_B1) * g
    v = ADAM_B2 * v + (1.0 - ADAM_B2) * (g * g)
    m_hat = m / (1.0 - ADAM_B1 ** ADAM_STEP)
    v_hat = v / (1.0 - ADAM_B2 ** ADAM_STEP)
    return -ADAM_LR * (m_hat / (jnp.sqrt(v_hat) + ADAM_EPS) + ADAM_WD * w), m, v


def _slot_sum(ref, own=None, me=None):
    g = None
    for i in range(N_DEV):
        term = ref[i].astype(F32)
        if own is not None:
            term = jnp.where(i == me, own, term)
        g = term if g is None else g + term
    return g


def _update_matrix(name, me, landed, own, w, m, v, rows):
    r, c = w.shape

    def body(me_ref, l_ref, own_ref, w_ref, m_ref, v_ref, g_ref, d_ref, nm_ref, nv_ref):
        g = _slot_sum(l_ref, own_ref[...].astype(F32), me_ref[0])
        g_ref[...] = g
        d_ref[...], nm_ref[...], nv_ref[...] = _adamw(w_ref[...], g, m_ref[...], v_ref[...])

    blk = pl.BlockSpec((rows, c), lambda i, me_ref: (i, 0))
    return pl.pallas_call(
        body, name=name,
        grid_spec=pltpu.PrefetchScalarGridSpec(
            num_scalar_prefetch=1, grid=(r // rows,),
            in_specs=[pl.BlockSpec((N_DEV, rows, c), lambda i, me_ref: (0, i, 0)),
                      pl.BlockSpec((None, rows, c), lambda i, me_ref: (me_ref[0], i, 0)), blk, blk, blk],
            out_specs=[blk] * 4),
        out_shape=[jax.ShapeDtypeStruct((r, c), F32)] * 4,
        compiler_params=_params("parallel"),
    )(me, landed, own, w, m, v)


def _update_small(landed, lb_logits, ws, ms, vs):
    def body(l_ref, lbl_ref, w_ref, m_ref, v_ref, g_ref, d_ref, nm_ref, nv_ref, loss_ref):
        tot = _slot_sum(l_ref)
        _, dlb = _lower_bound(lbl_ref[...])
        g_lb = tot[1:2, :SEC_W] * dlb
        g = jnp.concatenate([tot[0:1], jnp.concatenate([g_lb, -g_lb], axis=-1),
                             jnp.pad(tot[1:2, SEC_W:], ((0, 0), (0, SEC_W))), tot[2:3]], axis=0)
        g_ref[...] = g
        d_ref[...], nm_ref[...], nv_ref[...] = _adamw(w_ref[...], g, m_ref[...], v_ref[...])
        loss_ref[...] = tot[3:4, 0:1]

    vmem = pl.BlockSpec(memory_space=pltpu.VMEM)
    return pl.pallas_call(
        body, name="update_small", in_specs=[vmem] * 5, out_specs=[vmem] * 5,
        out_shape=[jax.ShapeDtypeStruct((4, D_MODEL), F32)] * 4 + [jax.ShapeDtypeStruct((1, 1), F32)],
    )(landed, lb_logits, ws, ms, vs)


def _pack_small(norm_w, lb_logits, hg_norm_w, final_norm_w):
    return jnp.concatenate([norm_w, lb_logits.reshape(1, D_MODEL),
                            jnp.pad(hg_norm_w, ((0, 0), (0, D_MODEL - SEC_W))), final_norm_w[None, :]], axis=0)


def _unpack_small(a):
    return a[0:1], a[1].reshape(2, SEC_W), a[2:3, :SEC_W], a[3]


def kernel(x, norm_w, w_in, hgrn_lb_logits, hg_norm_w, w_out, final_norm_w, loss_target, m_norm_w, m_w_in, m_hgrn_lb_logits, m_hg_norm_w, m_w_out, m_final_norm_w, v_norm_w, v_w_in, v_hgrn_lb_logits, v_hg_norm_w, v_w_out, v_final_norm_w):
    w_all, wo_all = _gather_weights(w_in[0], w_out[0])
    in_flight = []

    def start_exchange(gwi, gwo):
        *handles, token = _exchange_start(gwi, gwo.reshape(N_DEV, D_MODEL // N_DEV, D_MODEL))
        in_flight.extend(handles)
        return token

    gx, _, _, small = _local_step(x[0], loss_target[0], norm_w, w_all, hgrn_lb_logits, hg_norm_w,
                                  wo_all.reshape(D_MODEL, D_MODEL), final_norm_w, start_exchange)
    ls = _gather_small(small)
    gwi, gwo, li, lo = _exchange_wait(*in_flight, gx)
    me = _me().astype(jnp.int32).reshape(1)
    g_wi, d_wi, nm_wi, nv_wi = _update_matrix("update_w_in", me, li, gwi, w_in[0], m_w_in[0], v_w_in[0], 256)
    g_wo, d_wo, nm_wo, nv_wo = _update_matrix("update_w_out", me, lo, gwo, w_out[0], m_w_out[0], v_w_out[0], 128)
    g_s, d_s, nm_s, nv_s, loss = _update_small(
        ls, hgrn_lb_logits, _pack_small(norm_w, hgrn_lb_logits, hg_norm_w, final_norm_w),
        _pack_small(m_norm_w, m_hgrn_lb_logits, m_hg_norm_w, m_final_norm_w),
        _pack_small(v_norm_w, v_hgrn_lb_logits, v_hg_norm_w, v_final_norm_w))
    outs = []
    for small_out, wi, wo in ((g_s, g_wi, g_wo), (d_s, d_wi, d_wo), (nm_s, nm_wi, nm_wo), (nv_s, nv_wi, nv_wo)):
        nw, lb, hg, fw = _unpack_small(small_out)
        outs += [nw, wi[None], lb, hg, wo[None], fw]
    return (loss[0, 0], gx[None], *outs)
```

```python
import functools

import jax
import jax.numpy as jnp
from jax import lax
from jax.experimental import pallas as pl
from jax.experimental.pallas import tpu as pltpu

F32 = jnp.float32
MM = jnp.bfloat16
XCH = jnp.bfloat16
NORM_EPS = 1e-6
NEG = -1e30
N_DEV = 8
D_MODEL = 1024
N_SEC = 8
SEC_W = 512
HG_HEADS = 4
HG_D = 128
HG_GROUP = 4
AT_HEADS = 8
AT_DH = 64
ATT_BLK = 128
AT_COLS = 512
AT_QB = 4
DILATIONS = (1, 4, 16)
ROPE_THETA = 10000.0
CH = 16
LB_LO, LB_HI = 1e-6, 1.0 - 1e-6
ADAM_LR, ADAM_B1, ADAM_B2, ADAM_EPS, ADAM_WD, ADAM_STEP = 0.001, 0.9, 0.999, 1e-08, 0.01, 10
VMEM_LIMIT = 56 * 1024 * 1024
MESH = pl.DeviceIdType.MESH


def _params(*sem):
    return pltpu.CompilerParams(dimension_semantics=sem, vmem_limit_bytes=VMEM_LIMIT)


def _sigmoid(x):
    return 1.0 / (1.0 + jnp.exp(-x))


def _dot(a, b):
    return jnp.dot(a.astype(MM), b.astype(MM), preferred_element_type=F32)


def _dot_nt(a, b):
    return lax.dot_general(a.astype(MM), b.astype(MM), (((1,), (1,)), ((), ())), preferred_element_type=F32)


def _dot_tn(a, b):
    return lax.dot_general(a.astype(MM), b.astype(MM), (((0,), (0,)), ((), ())), preferred_element_type=F32)


def _tri_dot(tri, g):
    g1 = g.astype(jnp.bfloat16)
    r1 = g - g1.astype(F32)
    g2 = r1.astype(jnp.bfloat16)
    g3 = (r1 - g2.astype(F32)).astype(jnp.bfloat16)
    t = tri.astype(jnp.bfloat16)
    d = functools.partial(jnp.dot, preferred_element_type=F32)
    return d(t, g1) + d(t, g2) + d(t, g3)


def _lower_bound(lbl):
    l0, l1 = lbl[0:1, :], lbl[1:2, :]
    m = jnp.maximum(l0, l1)
    e0, e1 = jnp.exp(l0 - m), jnp.exp(l1 - m)
    p = e0 / (e0 + e1)
    inside = (p >= LB_LO) & (p <= LB_HI)
    return jnp.clip(p, LB_LO, LB_HI), jnp.where(inside, p * (e1 / (e0 + e1)), 0.0)


def _iota2(shape, dim):
    return lax.broadcasted_iota(jnp.int32, shape, dim)


def _hgrn_gates(xq, xf, lb):
    sgq = _sigmoid(xq)
    sg = _sigmoid(xf)
    sn = _sigmoid(-xf)
    f = lb + (1.0 - lb) * sg
    return sgq, xq * sgq, sg, sn, f, (1.0 - lb) * sn


def _bdot(a, b, ca, cb):
    return lax.dot_general(a.astype(MM), b.astype(MM), (((ca,), (cb,)), ((0,), (0,))), preferred_element_type=F32)


def _chunk_masks(rb):
    row, col = _iota2((rb, rb), 0), _iota2((rb, rb), 1)
    same = (row // CH) == (col // CH)
    return same & (row >= col), same & (row <= col)


def _hgrn_fwd(proj, lb_logits, rb=256):
    s = proj.shape[1]
    nb, nc = s // rb, rb // CH

    def body(q_ref, f_ref, i_ref, lbl_ref, o_ref, sst_ref, st_ref, slab_ref, states_ref):
        @pl.when(pl.program_id(1) == 0)
        def _():
            st_ref[...] = jnp.zeros_like(st_ref)

        sst_ref[...] = st_ref[...]
        prefix, _ = _chunk_masks(rb)
        c3 = lambda a: a.reshape(nc, CH, HG_D)
        row, col = _iota2((nc, CH, CH), 1), _iota2((nc, CH, CH), 2)
        for g in range(HG_GROUP):
            hs = slice(g * HG_D, (g + 1) * HG_D)
            lb, _ = _lower_bound(lbl_ref[:, hs])
            _, q, _, _, f, kk = _hgrn_gates(q_ref[:, hs], f_ref[:, hs], lb)
            b3 = c3(_tri_dot(prefix, jnp.log(f)))
            q3, kk3, v3 = c3(q), c3(kk), c3(i_ref[:, hs])
            bl3 = b3[:, CH - 1:CH, :]
            for t in range(CH):
                slab_ref[g, :, t * CH:(t + 1) * CH, :] = (
                    q3 * jnp.exp(jnp.minimum(b3 - b3[:, t:t + 1, :], 0.0))).astype(MM)
            r = _bdot(slab_ref[g], kk3, 2, 2)
            a = jnp.zeros((nc, CH, CH), F32)
            for t in range(CH):
                a = a + jnp.where(col == t, r[:, t * CH:(t + 1) * CH, :], 0.0)
            a = jnp.where(row >= col, a, 0.0)
            x_upd = _bdot(v3, kk3 * jnp.exp(bl3 - b3), 1, 1)
            ebl3 = jnp.exp(bl3)
            st = st_ref[g]
            for c in range(nc):
                states_ref[g, c] = st
                st = st * ebl3[c] + x_upd[c]
            st_ref[g] = st
            o3 = _bdot(q3 * jnp.exp(b3), states_ref[g], 2, 2) + _bdot(a, v3, 2, 1)
            o_ref[:, hs] = o3.reshape(rb, HG_D)

    wide = HG_GROUP * HG_D
    sec = lambda j: pl.BlockSpec((None, rb, wide), lambda h, i, j=j: (j, i, h))
    return pl.pallas_call(
        body, name="hgrn_fwd", grid=(HG_HEADS // HG_GROUP, nb),
        in_specs=[sec(0), sec(1), sec(2), pl.BlockSpec((2, wide), lambda h, i: (0, h))],
        out_specs=[pl.BlockSpec((rb, wide), lambda h, i: (i, h)),
                   pl.BlockSpec((None, HG_GROUP, HG_D, HG_D), lambda h, i: (i, h, 0, 0))],
        out_shape=[jax.ShapeDtypeStruct((s, SEC_W), F32),
                   jax.ShapeDtypeStruct((nb, HG_HEADS, HG_D, HG_D), F32)],
        scratch_shapes=[pltpu.VMEM((HG_GROUP, HG_D, HG_D), F32), pltpu.VMEM((HG_GROUP, nc, CH * CH, HG_D), MM),
                        pltpu.VMEM((HG_GROUP, nc, HG_D, HG_D), F32)],
        compiler_params=_params("parallel", "arbitrary"),
    )(proj, proj, proj, lb_logits)


def _hgrn_bwd(proj, lb_logits, d_o, sst, rb=256):
    s = proj.shape[1]
    nb, nc = s // rb, rb // CH

    def body(q_ref, f_ref, i_ref, lbl_ref, do_ref, sst_ref, dxq_ref, dxf_ref, dxi_ref, dlb_ref,
             dst_ref, states_ref, dstates_ref, lslab_ref, kslab_ref):
        @pl.when(pl.program_id(1) == 0)
        def _():
            dst_ref[...] = jnp.zeros_like(dst_ref)
            dlb_ref[...] = jnp.zeros_like(dlb_ref)

        prefix, suffix = _chunk_masks(rb)
        c3 = lambda a: a.reshape(nc, CH, HG_D)
        flat = lambda a: a.reshape(rb, HG_D)
        row, col = _iota2((nc, CH, CH), 1), _iota2((nc, CH, CH), 2)
        tril, triu = row >= col, row <= col
        sel = (_iota2((CH, CH * CH), 1) % CH == _iota2((CH, CH * CH), 0)).astype(MM)
        blockdiag = _iota2((nc, CH, CH * CH), 2) // CH == _iota2((nc, CH, CH * CH), 1)
        tile = lambda m: jnp.where(blockdiag, _dot(m.reshape(rb, CH), sel).reshape(nc, CH, CH * CH), 0.0)
        last = _iota2((nc, CH, HG_D), 1) == CH - 1
        for g in range(HG_GROUP):
            hs = slice(g * HG_D, (g + 1) * HG_D)
            lb, _ = _lower_bound(lbl_ref[:, hs])
            xq = q_ref[:, hs]
            sgq, q, sg, sn, f, kk = _hgrn_gates(xq, f_ref[:, hs], lb)
            b3 = c3(_tri_dot(prefix, jnp.log(f)))
            q3, kk3, v3, do3 = c3(q), c3(kk), c3(i_ref[:, hs]), c3(do_ref[:, hs])
            bl3 = b3[:, CH - 1:CH, :]
            eb3, ebl3, dec3 = jnp.exp(b3), jnp.exp(bl3), jnp.exp(bl3 - b3)
            qe3, kd3 = q3 * eb3, kk3 * dec3
            x_upd, y_upd = _bdot(v3, kd3, 1, 1), _bdot(do3, qe3, 1, 1)
            st = sst_ref[g]
            for c in range(nc):
                states_ref[g, c] = st
                st = st * ebl3[c] + x_upd[c]
            dst = dst_ref[g]
            for c in reversed(range(nc)):
                dstates_ref[g, c] = dst
                dst = dst * ebl3[c] + y_upd[c]
            dst_ref[g] = dst
            states, dstates = states_ref[g], dstates_ref[g]
            dqe = _bdot(do3, states, 2, 1)
            dkd = _bdot(v3, dstates, 2, 1)
            d_a = jnp.where(tril, _bdot(do3, v3, 2, 2), 0.0)
            d_at = jnp.where(triu, _bdot(v3, do3, 2, 2), 0.0)
            for t in range(CH):
                bt = b3[:, t:t + 1, :]
                lslab_ref[g, :, t * CH:(t + 1) * CH, :] = (q3 * jnp.exp(jnp.minimum(b3 - bt, 0.0))).astype(MM)
                kslab_ref[g, :, t * CH:(t + 1) * CH, :] = (kk3 * jnp.exp(jnp.minimum(bt - b3, 0.0))).astype(MM)
            r = _bdot(kslab_ref[g], q3, 2, 2)
            a_t = jnp.zeros((nc, CH, CH), F32)
            for t in range(CH):
                a_t = a_t + jnp.where(col == t, r[:, t * CH:(t + 1) * CH, :], 0.0)
            a_t = jnp.where(triu, a_t, 0.0)
            dv = _bdot(kd3, dstates, 2, 2) + _bdot(a_t, do3, 2, 1)
            dq_in = _bdot(tile(d_a), kslab_ref[g], 2, 1)
            dk_in = _bdot(tile(d_at), lslab_ref[g], 2, 1)
            dkd_kd = dkd * kd3
            db = dqe * qe3 - dkd_kd + q3 * dq_in - kk3 * dk_in
            dbl = jnp.sum(dkd_kd, axis=1, keepdims=True) + jnp.sum(dstates * states, axis=1, keepdims=True) * ebl3
            dg = _tri_dot(suffix, flat(db + jnp.where(last, dbl, 0.0)))
            df = dg / f - flat(dkd * dec3 + dk_in)
            dxq_ref[:, hs] = (flat(dqe * eb3 + dq_in) * (sgq * (1.0 + xq * (1.0 - sgq)))).astype(MM)
            dxf_ref[:, hs] = (df * (1.0 - lb) * sg * sn).astype(MM)
            dxi_ref[:, hs] = flat(dv).astype(MM)
            dlb_ref[:, hs] += jnp.sum(df * sn, axis=0, keepdims=True)

    wide = HG_GROUP * HG_D
    rev = lambda i: nb - 1 - i
    sec = lambda j: pl.BlockSpec((None, rb, wide), lambda h, i, j=j: (j, rev(i), h))
    blk = pl.BlockSpec((rb, wide), lambda h, i: (rev(i), h))
    state = (pltpu.VMEM((HG_GROUP, nc, HG_D, HG_D), F32), pltpu.VMEM((HG_GROUP, nc, CH * CH, HG_D), MM))
    return pl.pallas_call(
        body, name="hgrn_bwd", grid=(HG_HEADS // HG_GROUP, nb),
        in_specs=[sec(0), sec(1), sec(2), pl.BlockSpec((2, wide), lambda h, i: (0, h)), blk,
                  pl.BlockSpec((None, HG_GROUP, HG_D, HG_D), lambda h, i: (rev(i), h, 0, 0))],
        out_specs=[blk, blk, blk, pl.BlockSpec((1, wide), lambda h, i: (0, h))],
        out_shape=[jax.ShapeDtypeStruct((s, SEC_W), MM)] * 3 + [jax.ShapeDtypeStruct((1, SEC_W), F32)],
        scratch_shapes=[pltpu.VMEM((HG_GROUP, HG_D, HG_D), F32), state[0], state[0], state[1], state[1]],
        compiler_params=_params("parallel", "arbitrary"),
    )(proj, proj, proj, lb_logits, d_o, sst)


def _rope_tables(s):
    half = AT_DH // 2
    inv_freq = 1.0 / (ROPE_THETA ** (jnp.arange(half, dtype=F32) / half))
    ang = jnp.arange(s, dtype=jnp.int32).astype(F32)[:, None] * inv_freq[None, :]
    cos, sin = jnp.cos(ang), jnp.sin(ang)
    return jnp.concatenate([cos] * 4, axis=-1), jnp.concatenate([-sin, sin] * 2, axis=-1)


def _rope128(x, cos, sin):
    lo = (_iota2(x.shape, 1) % AT_DH) < AT_DH // 2
    rot = jnp.where(lo, pltpu.roll(x, 128 - AT_DH // 2, 1), pltpu.roll(x, AT_DH // 2, 1))
    return x * cos + rot * sin


LANE_GROUPS = SEC_W // 128


def _set_lanes(ref, val):
    for j in range(LANE_GROUPS):
        ref[j] = val[:, j * 128:(j + 1) * 128]


def _get_lanes(ref):
    return jnp.concatenate([ref[j] for j in range(LANE_GROUPS)], axis=-1)


def _to_view(src_ref, dst_ref, d):
    n = src_ref.shape[1] // d
    for r in range(d):
        rows = pl.ds(r, n, stride=d) if d > 1 else slice(None)
        for j in range(LANE_GROUPS):
            c0 = r * SEC_W + j * 128
            dst_ref[:, c0:c0 + 128] = src_ref.at[j][rows, :].astype(dst_ref.dtype)


def _from_view(src_ref, dst_ref, d):
    n = dst_ref.shape[1] // d
    for r in range(d):
        for j in range(LANE_GROUPS):
            c0 = r * SEC_W + j * 128
            dst_ref.at[j][pl.ds(r, n, stride=d), :] = src_ref[:, c0:c0 + 128].astype(dst_ref.dtype)


def _view_spec(tm, d):
    return pl.BlockSpec((tm // d, d * SEC_W), lambda i: (i, 0))


def _view_shape(s, d, dtype):
    return jax.ShapeDtypeStruct((s // d, d * SEC_W), dtype)


PROJ_KEPT = (0, 1, 2, 3, 7)


def _inproj_fwd(x, norm_w, w_all, cos, sin, tm=256):
    s = x.shape[0]

    def body(x_ref, nw_ref, w_ref, cos_ref, sin_ref, proj_ref, *refs):
        outs, (qs_ref, ks_ref, vs_ref) = refs[:-3], refs[-3:]
        xv = x_ref[...]
        rstd = lax.rsqrt(jnp.mean(xv * xv, axis=-1, keepdims=True) + NORM_EPS)
        u = (xv * rstd * nw_ref[...]).astype(MM)
        for slot, j in enumerate(PROJ_KEPT):
            proj_ref[slot] = jnp.dot(u, w_ref[j], preferred_element_type=F32)
        q, k, v = [jnp.dot(u, w_ref[j], preferred_element_type=F32) for j in (4, 5, 6)]
        c, sn = cos_ref[...], sin_ref[...]
        for g in range(LANE_GROUPS):
            sl = slice(g * 128, (g + 1) * 128)
            qs_ref[g] = _rope128(q[:, sl], c, sn) * (AT_DH ** -0.5)
            ks_ref[g] = _rope128(k[:, sl], c, sn)
            vs_ref[g] = v[:, sl]
        for i, d in enumerate(DILATIONS):
            for src_ref, dst_ref in zip((qs_ref, ks_ref, vs_ref), outs[3 * i:3 * i + 3]):
                _to_view(src_ref, dst_ref, d)

    tab = pl.BlockSpec((tm, 128), lambda i: (i, 0))
    return pl.pallas_call(
        body, name="inproj_fwd", grid=(s // tm,),
        in_specs=[pl.BlockSpec((tm, D_MODEL), lambda i: (i, 0)),
                  pl.BlockSpec((1, D_MODEL), lambda i: (0, 0)),
                  pl.BlockSpec((N_SEC, D_MODEL, SEC_W), lambda i: (0, 0, 0)), tab, tab],
        out_specs=[pl.BlockSpec((len(PROJ_KEPT), tm, SEC_W), lambda i: (0, i, 0))]
                  + [_view_spec(tm, d) for d in DILATIONS for _ in range(3)],
        out_shape=[jax.ShapeDtypeStruct((len(PROJ_KEPT), s, SEC_W), F32)]
                  + [_view_shape(s, d, MM) for d in DILATIONS for _ in range(3)],
        scratch_shapes=[pltpu.VMEM((LANE_GROUPS, tm, 128), F32)] * 3,
        compiler_params=_params("parallel"),
    )(x, norm_w, w_all, cos, sin)


def _band_mask(first_ok, second_ok):
    row, col = _iota2((ATT_BLK, 2 * ATT_BLK), 0), _iota2((ATT_BLK, 2 * ATT_BLK), 1)
    return ((col < ATT_BLK) & (col >= row) & first_ok) | ((col >= ATT_BLK) & ((col - ATT_BLK) <= row) & second_ok)


def _own_lanes(rows, h):
    lane = _iota2((rows, 128), 1)
    return (lane < AT_DH) if h == 0 else (lane >= AT_DH)


def _neg_pieces(rows, h):
    lane = _iota2((rows, 128), 1) - (AT_DH if h == 0 else 0)
    return jnp.where((lane >= 0) & (lane < 3), -1.0, 0.0).astype(MM)


def _units():
    return [(b, slice(g * 128, (g + 1) * 128), h) for b in range(AT_QB) for g in range(AT_COLS // 128) for h in range(2)]


def _sub(b):
    return slice(b * ATT_BLK, (b + 1) * ATT_BLK)


def _band_before(cur_ref, prev_ref, b, sl):
    if b == 0:
        return jnp.concatenate([prev_ref[:, sl], cur_ref[0:ATT_BLK, sl]], axis=0)
    return cur_ref[(b - 1) * ATT_BLK:(b + 1) * ATT_BLK, sl]


def _band_after(cur_ref, next_ref, b, sl):
    if b == AT_QB - 1:
        return jnp.concatenate([cur_ref[b * ATT_BLK:(b + 1) * ATT_BLK, sl], next_ref[:, sl]], axis=0)
    return cur_ref[b * ATT_BLK:(b + 2) * ATT_BLK, sl]


def _attn_specs(rows):
    assert rows % (AT_QB * ATT_BLK) == 0
    last = rows // ATT_BLK - 1
    cur = pl.BlockSpec((AT_QB * ATT_BLK, AT_COLS), lambda c, n: (n, c))
    prev = pl.BlockSpec((ATT_BLK, AT_COLS), lambda c, n: (jnp.maximum(AT_QB * n - 1, 0), c))
    nxt = pl.BlockSpec((ATT_BLK, AT_COLS), lambda c, n: (jnp.minimum(AT_QB * (n + 1), last), c))
    return cur, prev, nxt


def _attn_fwd(qr, kr, vr, d):
    rows, cols = qr.shape
    nb = rows // (AT_QB * ATT_BLK)

    def body(q_ref, kc_ref, kp_ref, vc_ref, vp_ref, o_ref, lse_ref):
        masks = {True: _band_mask(pl.program_id(1) > 0, True), False: _band_mask(True, True)}
        ones = jnp.ones((2 * ATT_BLK, 128), MM)
        head0 = _own_lanes(ATT_BLK, 0)
        units = _units()
        scs = []
        for b, sl, h in units:
            q2 = q_ref[_sub(b), sl]
            qh = jnp.where(_own_lanes(ATT_BLK, h), q2, jnp.zeros_like(q2))
            scs.append(jnp.where(masks[b == 0], _dot_nt(qh, _band_before(kc_ref, kp_ref, b, sl)), NEG))
        ms = [jnp.max(sc, axis=-1, keepdims=True) for sc in scs]
        ps = [jnp.exp(sc - m).astype(MM) for sc, m in zip(scs, ms)]
        ls = [jnp.dot(p, ones, preferred_element_type=F32) for p in ps]
        os_ = [jnp.dot(p, _band_before(vc_ref, vp_ref, b, sl), preferred_element_type=F32)
               for p, (b, sl, _) in zip(ps, units)]
        for i in range(0, len(units), 2):
            b, sl, _ = units[i]
            l = jnp.where(head0, ls[i], ls[i + 1])
            o_ref[_sub(b), sl] = jnp.where(head0, os_[i], os_[i + 1]) / l
            lse_ref[_sub(b), sl] = jnp.where(head0, ms[i], ms[i + 1]) + jnp.log(l)

    cur, prev, _ = _attn_specs(rows)
    o, lse = pl.pallas_call(
        body, name=f"attn_fwd_d{d}", grid=(cols // AT_COLS, nb),
        in_specs=[cur, cur, prev, cur, prev], out_specs=[cur, cur],
        out_shape=[jax.ShapeDtypeStruct((rows, cols), F32)] * 2,
        compiler_params=_params("parallel", "parallel"),
    )(qr, kr, kr, vr, vr)
    return o, lse


def _attn_bwd_dq(qr, kr, vr, do, lse, delta, d):
    rows, cols = qr.shape
    nb = rows // (AT_QB * ATT_BLK)

    def body(q_ref, kc_ref, kp_ref, vc_ref, vp_ref, do_ref, lse_ref, dl_ref, dq_ref):
        masks = {True: _band_mask(pl.program_id(1) > 0, True), False: _band_mask(True, True)}
        units = _units()
        sms, dps = [], []
        for b, sl, h in units:
            own, own_b, neg = _own_lanes(ATT_BLK, h), _own_lanes(2 * ATT_BLK, h), _neg_pieces(2 * ATT_BLK, h)
            sms.append(_dot_nt(jnp.where(own, q_ref[_sub(b), sl], lse_ref[_sub(b), sl]),
                               jnp.where(own_b, _band_before(kc_ref, kp_ref, b, sl), neg)))
            dps.append(_dot_nt(jnp.where(own, do_ref[_sub(b), sl], dl_ref[_sub(b), sl]),
                               jnp.where(own_b, _band_before(vc_ref, vp_ref, b, sl), neg)))
        dss = [(jnp.exp(jnp.where(masks[b == 0], sm, NEG)) * dp).astype(MM)
               for sm, dp, (b, _, _) in zip(sms, dps, units)]
        dqs = [jnp.dot(ds, _band_before(kc_ref, kp_ref, b, sl), preferred_element_type=F32) * (AT_DH ** -0.5)
               for ds, (b, sl, _) in zip(dss, units)]
        for i in range(0, len(units), 2):
            b, sl, _ = units[i]
            dq_ref[_sub(b), sl] = jnp.where(_own_lanes(ATT_BLK, 0), dqs[i], dqs[i + 1]).astype(dq_ref.dtype)

    cur, prev, _ = _attn_specs(rows)
    dq = pl.pallas_call(
        body, name=f"attn_bwd_dq_d{d}", grid=(cols // AT_COLS, nb),
        in_specs=[cur, cur, prev, cur, prev, cur, cur, cur], out_specs=cur,
        out_shape=jax.ShapeDtypeStruct((rows, cols), MM),
        compiler_params=_params("parallel", "parallel"),
    )(qr, kr, kr, vr, vr, do, lse, delta)
    return dq


def _attn_bwd_dkv(qr, kr, vr, do, lse, delta, d):
    rows, cols = qr.shape
    nb = rows // (AT_QB * ATT_BLK)

    def body(k_ref, v_ref, qc_ref, qn_ref, doc_ref, don_ref, lsec_ref, lsen_ref, dlc_ref, dln_ref,
             dk_ref, dv_ref):
        masks = {True: _band_mask(True, pl.program_id(1) < nb - 1), False: _band_mask(True, True)}
        units = _units()
        sms, dps = [], []
        for b, sl, h in units:
            own, own_b, neg = _own_lanes(ATT_BLK, h), _own_lanes(2 * ATT_BLK, h), _neg_pieces(ATT_BLK, h)
            sms.append(_dot_nt(jnp.where(own, k_ref[_sub(b), sl], neg),
                               jnp.where(own_b, _band_after(qc_ref, qn_ref, b, sl),
                                         _band_after(lsec_ref, lsen_ref, b, sl))))
            dps.append(_dot_nt(jnp.where(own, v_ref[_sub(b), sl], neg),
                               jnp.where(own_b, _band_after(doc_ref, don_ref, b, sl),
                                         _band_after(dlc_ref, dln_ref, b, sl))))
        ps = [jnp.exp(jnp.where(masks[b == AT_QB - 1], sm, NEG)) for sm, (b, _, _) in zip(sms, units)]
        dss = [(p * dp).astype(MM) for p, dp in zip(ps, dps)]
        dvs = [jnp.dot(p.astype(MM), _band_after(doc_ref, don_ref, b, sl), preferred_element_type=F32)
               for p, (b, sl, _) in zip(ps, units)]
        dks = [jnp.dot(ds, _band_after(qc_ref, qn_ref, b, sl), preferred_element_type=F32)
               for ds, (b, sl, _) in zip(dss, units)]
        head0 = _own_lanes(ATT_BLK, 0)
        for i in range(0, len(units), 2):
            b, sl, _ = units[i]
            dk_ref[_sub(b), sl] = jnp.where(head0, dks[i], dks[i + 1]).astype(dk_ref.dtype)
            dv_ref[_sub(b), sl] = jnp.where(head0, dvs[i], dvs[i + 1]).astype(dv_ref.dtype)

    cur, _, nxt = _attn_specs(rows)
    dk, dv = pl.pallas_call(
        body, name=f"attn_bwd_dkv_d{d}", grid=(cols // AT_COLS, nb),
        in_specs=[cur, cur, cur, nxt, cur, nxt, cur, nxt, cur, nxt], out_specs=[cur, cur],
        out_shape=[jax.ShapeDtypeStruct((rows, cols), MM)] * 2,
        compiler_params=_params("parallel", "parallel"),
    )(kr, vr, qr, qr, do, do, lse, lse, delta, delta)
    return dk, dv


def _head_sum(a, width):
    parts = []
    for j in range(a.shape[1] // width):
        sm = jnp.sum(a[:, j * width:(j + 1) * width], axis=-1, keepdims=True)
        parts.append(jnp.broadcast_to(sm, (a.shape[0], width)))
    return jnp.concatenate(parts, axis=-1)


def _partner_pieces(x):
    xs = jnp.concatenate([pltpu.roll(x[:, j * 128:(j + 1) * 128], AT_DH, 1) for j in range(x.shape[1] // 128)],
                         axis=-1)
    hi = xs.astype(jnp.bfloat16).astype(F32)
    mid = (xs - hi).astype(jnp.bfloat16).astype(F32)
    lo = (xs - hi - mid).astype(jnp.bfloat16).astype(F32)
    lane = _iota2(x.shape, 1) % AT_DH
    return jnp.where(lane == 0, hi, jnp.where(lane == 1, mid, jnp.where(lane == 2, lo, 0.0)))


def _mid(x, tgt, proj, o_hg, o_at, lse_at, hg_norm_w, final_norm_w, wo_all, tm=256):
    s = x.shape[0]
    nb = s // tm

    def body(x_ref, t_ref, hgz_ref, atz_ref, ohg_ref, o1_ref, o2_ref, o3_ref, l1_ref, l2_ref, l3_ref,
             g_ref, fw_ref, wo_ref,
             dh_ref, dohg_ref, dhgz_ref, datz_ref, do1_ref, do2_ref, do3_ref, dl1_ref, dl2_ref, dl3_ref,
             lp1_ref, lp2_ref, lp3_ref,
             gwo_ref, gfw_ref, ghg_ref, loss_ref, nat_ref, stage_ref, gwo_acc):
        @pl.when(pl.program_id(0) == 0)
        def _():
            gwo_acc[...] = jnp.zeros_like(gwo_acc)
            gfw_ref[...] = jnp.zeros_like(gfw_ref)
            ghg_ref[...] = jnp.zeros_like(ghg_ref)
            loss_ref[...] = jnp.zeros_like(loss_ref)

        ohg, g = ohg_ref[...], g_ref[...]
        rs = lax.rsqrt(_head_sum(ohg * ohg, HG_D) * (1.0 / HG_D) + NORM_EPS)
        on = ohg * rs
        hgz = hgz_ref[...]
        sz = _sigmoid(hgz)
        gate_hg = hgz * sz
        lses, outs = [l1_ref[...]], [o1_ref[...]]
        for k, (d, l_ref, o_ref) in enumerate(zip(DILATIONS[1:], (l2_ref, l3_ref), (o2_ref, o3_ref))):
            _from_view(l_ref, nat_ref.at[2 * k], d)
            _from_view(o_ref, nat_ref.at[2 * k + 1], d)
            lses.append(_get_lanes(nat_ref.at[2 * k]))
            outs.append(_get_lanes(nat_ref.at[2 * k + 1]))
        mx = jnp.maximum(jnp.maximum(lses[0], lses[1]), lses[2])
        es = [jnp.exp(l - mx) for l in lses]
        den = es[0] + es[1] + es[2]
        ws = [e / den for e in es]
        oat = ws[0] * outs[0] + ws[1] * outs[1] + ws[2] * outs[2]
        atz = atz_ref[...]
        sa = _sigmoid(atz)
        gate_at = atz * sa
        mixed = jnp.concatenate([on * g * gate_hg, oat * gate_at], axis=-1).astype(MM)
        h = x_ref[...] + jnp.dot(mixed, wo_ref[...], preferred_element_type=F32)
        rstd = lax.rsqrt(jnp.mean(h * h, axis=-1, keepdims=True) + NORM_EPS)
        hn = h * rstd
        fw = fw_ref[...]
        err = hn * fw - t_ref[...]
        loss_ref[...] += 0.5 * jnp.sum(jnp.mean(err * err, axis=-1, keepdims=True), axis=0, keepdims=True)
        dout = err * (1.0 / D_MODEL)
        gfw_ref[...] += jnp.sum(dout * hn, axis=0, keepdims=True)
        dhn = dout * fw
        dh = rstd * (dhn - hn * jnp.mean(dhn * hn, axis=-1, keepdims=True))
        dh_ref[...] = dh
        dh_mm = dh.astype(MM)
        gwo_acc[...] += _dot_tn(mixed, dh_mm)

        @pl.when(pl.program_id(0) == nb - 1)
        def _():
            gwo_ref[...] = gwo_acc[...].astype(gwo_ref.dtype)

        dmixed = _dot_nt(dh_mm, wo_ref[...])
        dm_hg = dmixed[:, :SEC_W]
        d_ong = dm_hg * gate_hg
        dhgz_ref[...] = (dm_hg * (on * g) * (sz * (1.0 + hgz * (1.0 - sz)))).astype(MM)
        ghg_ref[...] += jnp.sum(d_ong * on, axis=0, keepdims=True)
        d_on = d_ong * g
        dohg_ref[...] = rs * (d_on - on * (_head_sum(d_on * on, HG_D) * (1.0 / HG_D)))
        dm_at = dmixed[:, SEC_W:]
        d_oat = dm_at * gate_at
        datz_ref[...] = (dm_at * oat * (sa * (1.0 + atz * (1.0 - sa)))).astype(MM)
        drow = _head_sum(d_oat * oat, AT_DH)
        lse_all = mx + jnp.log(den)
        for val, dst_refs in ((d_oat, (do1_ref, do2_ref, do3_ref)),
                              (_partner_pieces(drow), (dl1_ref, dl2_ref, dl3_ref)),
                              (_partner_pieces(lse_all), (lp1_ref, lp2_ref, lp3_ref))):
            _set_lanes(stage_ref, val)
            for d, dst_ref in zip(DILATIONS, dst_refs):
                _to_view(stage_ref, dst_ref, d)

    row = lambda w: pl.BlockSpec((tm, w), lambda i: (i, 0))
    sec = lambda j: pl.BlockSpec((None, tm, SEC_W), lambda i, j=j: (j, i, 0))
    const = lambda shp: pl.BlockSpec(shp, lambda i: (0,) * len(shp))
    half = row(SEC_W)
    views = [_view_spec(tm, d) for d in DILATIONS]
    return pl.pallas_call(
        body, name="mid", grid=(nb,),
        in_specs=[row(D_MODEL), row(D_MODEL), sec(PROJ_KEPT.index(3)), sec(PROJ_KEPT.index(7)), half] + views * 2
                 + [const((1, SEC_W)), const((1, D_MODEL)), const((D_MODEL, D_MODEL))],
        out_specs=[row(D_MODEL)] + [half] * 3 + views * 3
                  + [const((D_MODEL, D_MODEL)), const((1, D_MODEL)), const((1, SEC_W)), const((1, 1))],
        out_shape=[jax.ShapeDtypeStruct((s, D_MODEL), F32), jax.ShapeDtypeStruct((s, SEC_W), F32)]
                  + [jax.ShapeDtypeStruct((s, SEC_W), MM)] * 2
                  + [_view_shape(s, d, MM) for d in DILATIONS] * 3
                  + [jax.ShapeDtypeStruct((D_MODEL, D_MODEL), XCH), jax.ShapeDtypeStruct((1, D_MODEL), F32),
                     jax.ShapeDtypeStruct((1, SEC_W), F32), jax.ShapeDtypeStruct((1, 1), F32)],
        scratch_shapes=[pltpu.VMEM((4, LANE_GROUPS, tm, 128), F32), pltpu.VMEM((LANE_GROUPS, tm, 128), F32),
                        pltpu.VMEM((D_MODEL, D_MODEL), F32)],
        compiler_params=_params("arbitrary"),
    )(x, tgt, proj, proj, o_hg, *o_at, *lse_at, hg_norm_w, final_norm_w, wo_all)


def _section_specs(dsecs, tm):
    return [pl.BlockSpec((tm, SEC_W), lambda i: (i, 0)) if k is None
            else pl.BlockSpec((None, tm, SEC_W), lambda i, k=k: (k, i, 0)) for _, k in dsecs]


def _inproj_bwd_x(x, norm_w, w_all, dh, dsecs, token, tm=256):
    s = x.shape[0]

    def body(x_ref, nw_ref, w_ref, dh_ref, tok_ref, *refs):
        sec_refs, (gx_ref, gnw_ref) = refs[:N_SEC], refs[N_SEC:]

        @pl.when(pl.program_id(0) == 0)
        def _():
            gnw_ref[...] = jnp.zeros_like(gnw_ref)

        du = jnp.zeros((tm, D_MODEL), F32)
        for j in range(N_SEC):
            du = du + _dot_nt(sec_refs[j][...], w_ref[j])
        xv, nw = x_ref[...], nw_ref[...]
        rstd = lax.rsqrt(jnp.mean(xv * xv, axis=-1, keepdims=True) + NORM_EPS)
        xn = xv * rstd
        gnw_ref[...] += jnp.sum(du * xn, axis=0, keepdims=True)
        dxn = du * nw
        dx = rstd * (dxn - xn * jnp.mean(dxn * xn, axis=-1, keepdims=True))
        gx_ref[...] = (dh_ref[...] + tok_ref[0:1, 0:1]) + dx

    row = lambda w: pl.BlockSpec((tm, w), lambda i: (i, 0))
    const = lambda shp: pl.BlockSpec(shp, lambda i: (0,) * len(shp))
    return pl.pallas_call(
        body, name="inproj_bwd_x", grid=(s // tm,),
        in_specs=[row(D_MODEL), const((1, D_MODEL)), const((N_SEC, D_MODEL, SEC_W)), row(D_MODEL), const((8, 128))]
                 + _section_specs(dsecs, tm),
        out_specs=[row(D_MODEL), const((1, D_MODEL))],
        out_shape=[jax.ShapeDtypeStruct((s, D_MODEL), F32), jax.ShapeDtypeStruct((1, D_MODEL), F32)],
        compiler_params=_params("arbitrary"),
    )(x, norm_w, w_all, dh, token, *[a for a, _ in dsecs])


def _inproj_bwd_w(x, norm_w, dsec, dq_r, dk_r, dv, cos, sin, tm=512):
    s = x.shape[0]
    nb = s // tm

    def body(x_ref, nw_ref, s0, s1, s2, s3, s7, q1, q2, q3, k1, k2, k3, v1, v2, v3, cos_ref, sin_ref,
             gw_hbm, datt_ref, acc_ref, stage_ref, nat_ref):
        @pl.when(pl.program_id(0) == 0)
        def _():
            acc_ref[...] = jnp.zeros_like(acc_ref)

        def total(refs):
            acc = refs[0][...].astype(F32)
            for d, ref in zip(DILATIONS[1:], refs[1:]):
                _from_view(ref, nat_ref, d)
                acc = acc + _get_lanes(nat_ref)
            return acc

        c, sn = cos_ref[...], -sin_ref[...]
        unrot = lambda a: jnp.concatenate(
            [_rope128(a[:, j * 128:(j + 1) * 128], c, sn) for j in range(LANE_GROUPS)], axis=-1)
        att = [a.astype(MM) for a in (unrot(total((q1, q2, q3))), unrot(total((k1, k2, k3))), total((v1, v2, v3)))]
        for j, a in enumerate(att):
            datt_ref[j] = a
        xv = x_ref[...]
        rstd = lax.rsqrt(jnp.mean(xv * xv, axis=-1, keepdims=True) + NORM_EPS)
        u_t = (xv * rstd * nw_ref[...]).T.astype(MM)
        for j, dsj in enumerate((s0[...], s1[...], s2[...], s3[...], *att, s7[...])):
            acc_ref[j] += jnp.dot(u_t, dsj, preferred_element_type=F32)

        @pl.when(pl.program_id(0) == nb - 1)
        def _():
            for j in range(N_SEC):
                stage_ref[...] = acc_ref[j].astype(stage_ref.dtype)
                pltpu.sync_copy(stage_ref, gw_hbm.at[j])

    row = lambda w: pl.BlockSpec((tm, w), lambda i: (i, 0))
    return pl.pallas_call(
        body, name="inproj_bwd_w", grid=(nb,),
        in_specs=[row(D_MODEL), pl.BlockSpec((1, D_MODEL), lambda i: (0, 0))] + [row(SEC_W)] * 5
                 + [_view_spec(tm, d) for d in DILATIONS] * 3 + [row(128), row(128)],
        out_specs=[pl.BlockSpec(memory_space=pl.ANY), pl.BlockSpec((3, tm, SEC_W), lambda i: (0, i, 0))],
        out_shape=[jax.ShapeDtypeStruct((N_SEC, D_MODEL, SEC_W), XCH), jax.ShapeDtypeStruct((3, s, SEC_W), MM)],
        scratch_shapes=[pltpu.VMEM((N_SEC, D_MODEL, SEC_W), F32), pltpu.VMEM((D_MODEL, SEC_W), XCH),
                        pltpu.VMEM((LANE_GROUPS, tm, 128), F32)],
        compiler_params=_params("arbitrary"),
    )(x, norm_w, *dsec, *dq_r, *dk_r, *dv, cos, sin)


def _local_step(x, tgt, norm_w, w_all, lb_logits, hg_norm_w, wo_all, final_norm_w, on_weight_grads):
    s = x.shape[0]
    cos, sin = _rope_tables(s)
    proj, *qkv = _inproj_fwd(x, norm_w, w_all, cos, sin)
    o_hg, sst = _hgrn_fwd(proj, lb_logits)
    qkv = [qkv[3 * i:3 * i + 3] for i in range(len(DILATIONS))]
    att = [_attn_fwd(*qkv_d, d) for qkv_d, d in zip(qkv, DILATIONS)]
    (dh, d_ohg, d_hgz, d_atz, do1, do2, do3, dl1, dl2, dl3, lp1, lp2, lp3, gwo, gfw, ghg, loss) = _mid(
        x, tgt, proj, o_hg, [a[0] for a in att], [a[1] for a in att], hg_norm_w, final_norm_w[None, :], wo_all)
    dxq, dxf, dxi, dlb = _hgrn_bwd(proj, lb_logits, d_ohg, sst)
    dq_r, dk_r, dv = [], [], []
    for d, qkv_d, do, lp, dl in zip(DILATIONS, qkv, (do1, do2, do3), (lp1, lp2, lp3), (dl1, dl2, dl3)):
        dq_r.append(_attn_bwd_dq(*qkv_d, do, lp, dl, d))
        dk_d, dv_d = _attn_bwd_dkv(*qkv_d, do, lp, dl, d)
        dk_r.append(dk_d)
        dv.append(dv_d)
    gwi, d_att = _inproj_bwd_w(x, norm_w, (dxq, dxf, dxi, d_hgz, d_atz), dq_r, dk_r, dv, cos, sin)
    dsecs = [(dxq, None), (dxf, None), (dxi, None), (d_hgz, None), (d_att, 0), (d_att, 1), (d_att, 2), (d_atz, None)]
    token = on_weight_grads(gwi, gwo)
    gx, gnw = _inproj_bwd_x(x, norm_w, w_all, dh, dsecs, token)
    small = jnp.concatenate([gnw, jnp.concatenate([dlb, ghg], axis=-1), gfw,
                             jnp.pad(loss, ((0, 0), (0, D_MODEL - 1)))], axis=0)
    return gx, gwi, gwo, small


def _coords():
    return lax.axis_index("x"), lax.axis_index("y"), lax.axis_index("c")


def _gather_weights(w_in, w_out):
    wo_rows = w_out.shape[0]

    def body(wi_ref, wo_ref, wi_all, wo_all, send_sems, recv_sems):
        x, y, c = _coords()
        me, sibling = (x, y, c), (x, y, 1 - c)
        chips = [(1 - x, y), (x, 1 - y), (1 - x, 1 - y)]
        slot = lambda p: 4 * p[0] + 2 * p[1] + p[2]

        def copies(k, block, to):
            return [pltpu.make_async_remote_copy(
                src_ref=ref.at[slot(block)], dst_ref=ref.at[slot(block)], send_sem=send_sems.at[a, k],
                recv_sem=recv_sems.at[a, k], device_id=to, device_id_type=MESH)
                for a, ref in enumerate((wi_all, wo_all))]

        wi_all[slot(me)] = wi_ref[...].astype(MM)
        wo_all[slot(me)] = wo_ref[...].astype(MM)
        first = copies(0, me, sibling)
        for j, chip in enumerate(chips):
            first += copies(1 + j, me, (*chip, c))
        for cp in first:
            cp.start()
        passed = []
        for j, chip in enumerate(chips):
            for cp in copies(1 + j, (*chip, c), me):
                cp.wait_recv()
            fwd = copies(4 + j, (*chip, c), sibling)
            for cp in fwd:
                cp.start()
            passed += fwd
        for cp in copies(0, sibling, me):
            cp.wait_recv()
        for j, chip in enumerate(chips):
            for cp in copies(4 + j, (*chip, 1 - c), me):
                cp.wait_recv()
        for cp in first + passed:
            cp.wait_send()

    vmem = pl.BlockSpec(memory_space=pltpu.VMEM)
    return pl.pallas_call(
        body, name="gather_weights",
        in_specs=[vmem, vmem], out_specs=[vmem, vmem],
        out_shape=[jax.ShapeDtypeStruct((N_DEV, D_MODEL, SEC_W), MM),
                   jax.ShapeDtypeStruct((N_DEV, wo_rows, D_MODEL), MM)],
        scratch_shapes=[pltpu.SemaphoreType.DMA((2, 7)), pltpu.SemaphoreType.DMA((2, 7))],
        compiler_params=pltpu.CompilerParams(vmem_limit_bytes=VMEM_LIMIT),
    )(w_in, w_out)


def _me():
    x, y, c = _coords()
    return 4 * x + 2 * y + c


def _grad_copies(srcs, lands, send_sems, recv_sems):
    x, y, c = _coords()
    me = 4 * x + 2 * y + c
    copies = []
    for k in range(1, N_DEV):
        px, py, pc = x ^ (k >> 2), y ^ ((k >> 1) & 1), c ^ (k & 1)
        peer = 4 * px + 2 * py + pc
        for a, (src, dst) in enumerate(zip(srcs, lands)):
            copies.append(pltpu.make_async_remote_copy(
                src_ref=src.at[peer], dst_ref=dst.at[me], send_sem=send_sems.at[a * (N_DEV - 1) + k - 1],
                recv_sem=recv_sems.at[a * (N_DEV - 1) + k - 1], device_id=(px, py, pc), device_id_type=MESH))
    return copies


HBM_SPEC = pl.BlockSpec(memory_space=pltpu.HBM)
SEM_SPEC = pl.BlockSpec(memory_space=pltpu.SEMAPHORE)
SPLIT_COPY_EFFECT = pltpu.SideEffectType.DATAFLOW_SIDE_EFFECTING


def _exchange_start(gwi, gwo):
    def body(gwi_ref, gwo_ref, li_ref, lo_ref, send_sems, recv_sems, gwi_thru, gwo_thru, li_thru, lo_thru, token):
        for cp in _grad_copies((gwi_ref, gwo_ref), (li_ref, lo_ref), send_sems, recv_sems):
            cp.start()
        token[...] = jnp.zeros_like(token)

    hbm = lambda a: pltpu.with_memory_space_constraint(a, pltpu.HBM)
    bufs = (gwi, gwo, lax.empty(gwi.shape, gwi.dtype), lax.empty(gwo.shape, gwo.dtype))
    return pl.pallas_call(
        body, name="exchange_start",
        out_shape=(pltpu.SemaphoreType.DMA((2 * (N_DEV - 1),)), pltpu.SemaphoreType.DMA((2 * (N_DEV - 1),)),
                   *[pltpu.HBM(a.shape, a.dtype) for a in bufs], jax.ShapeDtypeStruct((8, 128), F32)),
        in_specs=[HBM_SPEC] * 4,
        out_specs=(SEM_SPEC, SEM_SPEC, HBM_SPEC, HBM_SPEC, HBM_SPEC, HBM_SPEC, pl.BlockSpec(memory_space=pltpu.VMEM)),
        input_output_aliases={0: 2, 1: 3, 2: 4, 3: 5},
        compiler_params=pltpu.CompilerParams(has_side_effects=SPLIT_COPY_EFFECT),
    )(*[hbm(a) for a in bufs])


def _exchange_wait(send_sems, recv_sems, gwi, gwo, li, lo, after):
    def body(gwi_ref, gwo_ref, li_ref, lo_ref, send_sems, recv_sems, after_ref, gwi_out, gwo_out, li_out, lo_out):
        for cp in _grad_copies((gwi_ref, gwo_ref), (li_ref, lo_ref), send_sems, recv_sems):
            cp.wait_send()
            cp.wait_recv()

    return pl.pallas_call(
        body, name="exchange_wait",
        out_shape=tuple(pltpu.HBM(a.shape, a.dtype) for a in (gwi, gwo, li, lo)),
        in_specs=[HBM_SPEC] * 4 + [SEM_SPEC, SEM_SPEC, pl.BlockSpec(memory_space=pl.ANY)],
        out_specs=(HBM_SPEC,) * 4,
        input_output_aliases={0: 0, 1: 1, 2: 2, 3: 3},
        compiler_params=pltpu.CompilerParams(has_side_effects=SPLIT_COPY_EFFECT),
    )(gwi, gwo, li, lo, send_sems, recv_sems, after)


def _gather_small(small):
    def body(sm_ref, ls_ref, send_sems, recv_sems, local_sem):
        x, y, c = _coords()
        me = 4 * x + 2 * y + c
        own = pltpu.make_async_copy(sm_ref, ls_ref.at[me], local_sem)
        own.start()
        sends = []
        for k in range(1, N_DEV):
            peer = (x ^ (k >> 2), y ^ ((k >> 1) & 1), c ^ (k & 1))
            sends.append(pltpu.make_async_remote_copy(
                src_ref=sm_ref, dst_ref=ls_ref.at[me], send_sem=send_sems.at[k - 1], recv_sem=recv_sems.at[k - 1],
                device_id=peer, device_id_type=MESH))
        for cp in sends:
            cp.start()
        for cp in sends:
            cp.wait_recv()
        for cp in sends:
            cp.wait_send()
        own.wait()

    vmem = pl.BlockSpec(memory_space=pltpu.VMEM)
    return pl.pallas_call(
        body, name="gather_small", in_specs=[vmem], out_specs=vmem,
        out_shape=jax.ShapeDtypeStruct((N_DEV,) + small.shape, F32),
        scratch_shapes=[pltpu.SemaphoreType.DMA((N_DEV - 1,)), pltpu.SemaphoreType.DMA((N_DEV - 1,)),
                        pltpu.SemaphoreType.DMA],
    )(small)


def _adamw(w, g, m, v):
    m = ADAM_B1 * m + (1.0 - ADAM_B1) * g
    v = ADAM_B2 * v + (1.0 - ADAM_B2) * (g * g)
    m_hat = m / (1.0 - ADAM_B1 ** ADAM_STEP)
    v_hat = v / (1.0 - ADAM_B2 ** ADAM_STEP)
    return -ADAM_LR * (m_hat / (jnp.sqrt(v_hat) + ADAM_EPS) + ADAM_WD * w), m, v


def _slot_sum(ref, own=None, me=None):
    g = None
    for i in range(N_DEV):
        term = ref[i].astype(F32)
        if own is not None:
            term = jnp.where(i == me, own, term)
        g = term if g is None else g + term
    return g


def _update_matrix(name, me, landed, own, w, m, v, rows):
    r, c = w.shape

    def body(me_ref, l_ref, own_ref, w_ref, m_ref, v_ref, g_ref, d_ref, nm_ref, nv_ref):
        g = _slot_sum(l_ref, own_ref[...].astype(F32), me_ref[0])
        g_ref[...] = g
        d_ref[...], nm_ref[...], nv_ref[...] = _adamw(w_ref[...], g, m_ref[...], v_ref[...])

    blk = pl.BlockSpec((rows, c), lambda i, me_ref: (i, 0))
    return pl.pallas_call(
        body, name=name,
        grid_spec=pltpu.PrefetchScalarGridSpec(
            num_scalar_prefetch=1, grid=(r // rows,),
            in_specs=[pl.BlockSpec((N_DEV, rows, c), lambda i, me_ref: (0, i, 0)),
                      pl.BlockSpec((None, rows, c), lambda i, me_ref: (me_ref[0], i, 0)), blk, blk, blk],
            out_specs=[blk] * 4),
        out_shape=[jax.ShapeDtypeStruct((r, c), F32)] * 4,
        compiler_params=_params("parallel"),
    )(me, landed, own, w, m, v)


def _update_small(landed, lb_logits, ws, ms, vs):
    def body(l_ref, lbl_ref, w_ref, m_ref, v_ref, g_ref, d_ref, nm_ref, nv_ref, loss_ref):
        tot = _slot_sum(l_ref)
        _, dlb = _lower_bound(lbl_ref[...])
        g_lb = tot[1:2, :SEC_W] * dlb
        g = jnp.concatenate([tot[0:1], jnp.concatenate([g_lb, -g_lb], axis=-1),
                             jnp.pad(tot[1:2, SEC_W:], ((0, 0), (0, SEC_W))), tot[2:3]], axis=0)
        g_ref[...] = g
        d_ref[...], nm_ref[...], nv_ref[...] = _adamw(w_ref[...], g, m_ref[...], v_ref[...])
        loss_ref[...] = tot[3:4, 0:1]

    vmem = pl.BlockSpec(memory_space=pltpu.VMEM)
    return pl.pallas_call(
        body, name="update_small", in_specs=[vmem] * 5, out_specs=[vmem] * 5,
        out_shape=[jax.ShapeDtypeStruct((4, D_MODEL), F32)] * 4 + [jax.ShapeDtypeStruct((1, 1), F32)],
    )(landed, lb_logits, ws, ms, vs)


def _pack_small(norm_w, lb_logits, hg_norm_w, final_norm_w):
    return jnp.concatenate([norm_w, lb_logits.reshape(1, D_MODEL),
                            jnp.pad(hg_norm_w, ((0, 0), (0, D_MODEL - SEC_W))), final_norm_w[None, :]], axis=0)


def _unpack_small(a):
    return a[0:1], a[1].reshape(2, SEC_W), a[2:3, :SEC_W], a[3]


def kernel(x, norm_w, w_in, hgrn_lb_logits, hg_norm_w, w_out, final_norm_w, loss_target, m_norm_w, m_w_in, m_hgrn_lb_logits, m_hg_norm_w, m_w_out, m_final_norm_w, v_norm_w, v_w_in, v_hgrn_lb_logits, v_hg_norm_w, v_w_out, v_final_norm_w):
    w_all, wo_all = _gather_weights(w_in[0], w_out[0])
    in_flight = []

    def start_exchange(gwi, gwo):
        *handles, token = _exchange_start(gwi, gwo.reshape(N_DEV, D_MODEL // N_DEV, D_MODEL))
        in_flight.extend(handles)
        return token

    gx, _, _, small = _local_step(x[0], loss_target[0], norm_w, w_all, hgrn_lb_logits, hg_norm_w,
                                  wo_all.reshape(D_MODEL, D_MODEL), final_norm_w, start_exchange)
    ls = _gather_small(small)
    gwi, gwo, li, lo = _exchange_wait(*in_flight, gx)
    me = _me().astype(jnp.int32).reshape(1)
    g_wi, d_wi, nm_wi, nv_wi = _update_matrix("update_w_in", me, li, gwi, w_in[0], m_w_in[0], v_w_in[0], 256)
    g_wo, d_wo, nm_wo, nv_wo = _update_matrix("update_w_out", me, lo, gwo, w_out[0], m_w_out[0], v_w_out[0], 128)
    g_s, d_s, nm_s, nv_s, loss = _update_small(
        ls, hgrn_lb_logits, _pack_small(norm_w, hgrn_lb_logits, hg_norm_w, final_norm_w),
        _pack_small(m_norm_w, m_hgrn_lb_logits, m_hg_norm_w, m_final_norm_w),
        _pack_small(v_norm_w, v_hgrn_lb_logits, v_hg_norm_w, v_final_norm_w))
    outs = []
    for small_out, wi, wo in ((g_s, g_wi, g_wo), (d_s, d_wi, d_wo), (nm_s, nm_wi, nm_wo), (nv_s, nv_wi, nv_wo)):
        nw, lb, hg, fw = _unpack_small(small_out)
        outs += [nw, wi[None], lb, hg, wo[None], fw]
    return (loss[0, 0], gx[None], *outs)
```

```python
import functools

import jax
import jax.numpy as jnp
from jax import lax
from jax.experimental import pallas as pl
from jax.experimental.pallas import tpu as pltpu

F32 = jnp.float32
MM = jnp.bfloat16
XCH = jnp.bfloat16
NORM_EPS = 1e-6
NEG = -1e30
N_DEV = 8
D_MODEL = 1024
N_SEC = 8
SEC_W = 512
HG_HEADS = 4
HG_D = 128
HG_GROUP = 4
AT_HEADS = 8
AT_DH = 64
ATT_BLK = 128
AT_COLS = 512
AT_QB = 4
DILATIONS = (1, 4, 16)
ROPE_THETA = 10000.0
CH = 16
LB_LO, LB_HI = 1e-6, 1.0 - 1e-6
ADAM_LR, ADAM_B1, ADAM_B2, ADAM_EPS, ADAM_WD, ADAM_STEP = 0.001, 0.9, 0.999, 1e-08, 0.01, 10
VMEM_LIMIT = 56 * 1024 * 1024
MESH = pl.DeviceIdType.MESH


def _params(*sem):
    return pltpu.CompilerParams(dimension_semantics=sem, vmem_limit_bytes=VMEM_LIMIT)


def _sigmoid(x):
    return 1.0 / (1.0 + jnp.exp(-x))


def _dot(a, b):
    return jnp.dot(a.astype(MM), b.astype(MM), preferred_element_type=F32)


def _dot_nt(a, b):
    return lax.dot_general(a.astype(MM), b.astype(MM), (((1,), (1,)), ((), ())), preferred_element_type=F32)


def _dot_tn(a, b):
    return lax.dot_general(a.astype(MM), b.astype(MM), (((0,), (0,)), ((), ())), preferred_element_type=F32)


def _tri_dot(tri, g):
    g1 = g.astype(jnp.bfloat16)
    r1 = g - g1.astype(F32)
    g2 = r1.astype(jnp.bfloat16)
    g3 = (r1 - g2.astype(F32)).astype(jnp.bfloat16)
    t = tri.astype(jnp.bfloat16)
    d = functools.partial(jnp.dot, preferred_element_type=F32)
    return d(t, g1) + d(t, g2) + d(t, g3)


def _lower_bound(lbl):
    l0, l1 = lbl[0:1, :], lbl[1:2, :]
    m = jnp.maximum(l0, l1)
    e0, e1 = jnp.exp(l0 - m), jnp.exp(l1 - m)
    p = e0 / (e0 + e1)
    inside = (p >= LB_LO) & (p <= LB_HI)
    return jnp.clip(p, LB_LO, LB_HI), jnp.where(inside, p * (e1 / (e0 + e1)), 0.0)


def _iota2(shape, dim):
    return lax.broadcasted_iota(jnp.int32, shape, dim)


def _hgrn_gates(xq, xf, lb):
    sgq = _sigmoid(xq)
    sg = _sigmoid(xf)
    sn = _sigmoid(-xf)
    f = lb + (1.0 - lb) * sg
    return sgq, xq * sgq, sg, sn, f, (1.0 - lb) * sn


def _bdot(a, b, ca, cb):
    return lax.dot_general(a.astype(MM), b.astype(MM), (((ca,), (cb,)), ((0,), (0,))), preferred_element_type=F32)


def _chunk_masks(rb):
    row, col = _iota2((rb, rb), 0), _iota2((rb, rb), 1)
    same = (row // CH) == (col // CH)
    return same & (row >= col), same & (row <= col)


def _hgrn_fwd(proj, lb_logits, rb=256):
    s = proj.shape[1]
    nb, nc = s // rb, rb // CH

    def body(q_ref, f_ref, i_ref, lbl_ref, o_ref, sst_ref, st_ref, slab_ref, states_ref):
        @pl.when(pl.program_id(1) == 0)
        def _():
            st_ref[...] = jnp.zeros_like(st_ref)

        sst_ref[...] = st_ref[...]
        prefix, _ = _chunk_masks(rb)
        c3 = lambda a: a.reshape(nc, CH, HG_D)
        row, col = _iota2((nc, CH, CH), 1), _iota2((nc, CH, CH), 2)
        heads = []
        for g in range(HG_GROUP):
            hs = slice(g * HG_D, (g + 1) * HG_D)
            lb, _ = _lower_bound(lbl_ref[:, hs])
            _, q, _, _, f, kk = _hgrn_gates(q_ref[:, hs], f_ref[:, hs], lb)
            b3 = c3(_tri_dot(prefix, jnp.log(f)))
            q3, kk3, v3 = c3(q), c3(kk), c3(i_ref[:, hs])
            bl3 = b3[:, CH - 1:CH, :]
            for t in range(CH):
                slab_ref[g, :, t * CH:(t + 1) * CH, :] = (
                    q3 * jnp.exp(jnp.minimum(b3 - b3[:, t:t + 1, :], 0.0))).astype(MM)
            x_upd = _bdot(v3, kk3 * jnp.exp(bl3 - b3), 1, 1)
            heads.append(dict(hs=hs, kk3=kk3, v3=v3, qe3=q3 * jnp.exp(b3), ebl3=jnp.exp(bl3), x_upd=x_upd))
        for g, hd in enumerate(heads):
            st = st_ref[g]
            for c in range(nc):
                states_ref[g, c] = st
                st = st * hd["ebl3"][c] + hd["x_upd"][c]
            st_ref[g] = st
        for g, hd in enumerate(heads):
            r = _bdot(slab_ref[g], hd["kk3"], 2, 2)
            a = jnp.zeros((nc, CH, CH), F32)
            for t in range(CH):
                a = a + jnp.where(col == t, r[:, t * CH:(t + 1) * CH, :], 0.0)
            a = jnp.where(row >= col, a, 0.0)
            o3 = _bdot(hd["qe3"], states_ref[g], 2, 2) + _bdot(a, hd["v3"], 2, 1)
            o_ref[:, hd["hs"]] = o3.reshape(rb, HG_D)

    wide = HG_GROUP * HG_D
    sec = lambda j: pl.BlockSpec((None, rb, wide), lambda h, i, j=j: (j, i, h))
    return pl.pallas_call(
        body, name="hgrn_fwd", grid=(HG_HEADS // HG_GROUP, nb),
        in_specs=[sec(0), sec(1), sec(2), pl.BlockSpec((2, wide), lambda h, i: (0, h))],
        out_specs=[pl.BlockSpec((rb, wide), lambda h, i: (i, h)),
                   pl.BlockSpec((None, HG_GROUP, HG_D, HG_D), lambda h, i: (i, h, 0, 0))],
        out_shape=[jax.ShapeDtypeStruct((s, SEC_W), F32),
                   jax.ShapeDtypeStruct((nb, HG_HEADS, HG_D, HG_D), F32)],
        scratch_shapes=[pltpu.VMEM((HG_GROUP, HG_D, HG_D), F32), pltpu.VMEM((HG_GROUP, nc, CH * CH, HG_D), MM),
                        pltpu.VMEM((HG_GROUP, nc, HG_D, HG_D), F32)],
        compiler_params=_params("parallel", "arbitrary"),
    )(proj, proj, proj, lb_logits)


def _hgrn_bwd(proj, lb_logits, d_o, sst, rb=256):
    s = proj.shape[1]
    nb, nc = s // rb, rb // CH

    def body(q_ref, f_ref, i_ref, lbl_ref, do_ref, sst_ref, dxq_ref, dxf_ref, dxi_ref, dlb_ref,
             dst_ref, states_ref, dstates_ref, lslab_ref, kslab_ref):
        @pl.when(pl.program_id(1) == 0)
        def _():
            dst_ref[...] = jnp.zeros_like(dst_ref)
            dlb_ref[...] = jnp.zeros_like(dlb_ref)

        prefix, suffix = _chunk_masks(rb)
        c3 = lambda a: a.reshape(nc, CH, HG_D)
        flat = lambda a: a.reshape(rb, HG_D)
        row, col = _iota2((nc, CH, CH), 1), _iota2((nc, CH, CH), 2)
        tril, triu = row >= col, row <= col
        sel = (_iota2((CH, CH * CH), 1) % CH == _iota2((CH, CH * CH), 0)).astype(MM)
        blockdiag = _iota2((nc, CH, CH * CH), 2) // CH == _iota2((nc, CH, CH * CH), 1)
        tile = lambda m: jnp.where(blockdiag, _dot(m.reshape(rb, CH), sel).reshape(nc, CH, CH * CH), 0.0)
        last = _iota2((nc, CH, HG_D), 1) == CH - 1
        heads = []
        for g in range(HG_GROUP):
            hs = slice(g * HG_D, (g + 1) * HG_D)
            lb, _ = _lower_bound(lbl_ref[:, hs])
            xq = q_ref[:, hs]
            sgq, q, sg, sn, f, kk = _hgrn_gates(xq, f_ref[:, hs], lb)
            b3 = c3(_tri_dot(prefix, jnp.log(f)))
            q3, kk3, v3, do3 = c3(q), c3(kk), c3(i_ref[:, hs]), c3(do_ref[:, hs])
            bl3 = b3[:, CH - 1:CH, :]
            eb3, ebl3, dec3 = jnp.exp(b3), jnp.exp(bl3), jnp.exp(bl3 - b3)
            qe3, kd3 = q3 * eb3, kk3 * dec3
            x_upd, y_upd = _bdot(v3, kd3, 1, 1), _bdot(do3, qe3, 1, 1)
            for t in range(CH):
                bt = b3[:, t:t + 1, :]
                lslab_ref[g, :, t * CH:(t + 1) * CH, :] = (q3 * jnp.exp(jnp.minimum(b3 - bt, 0.0))).astype(MM)
                kslab_ref[g, :, t * CH:(t + 1) * CH, :] = (kk3 * jnp.exp(jnp.minimum(bt - b3, 0.0))).astype(MM)
            d_a = jnp.where(tril, _bdot(do3, v3, 2, 2), 0.0)
            d_at = jnp.where(triu, _bdot(v3, do3, 2, 2), 0.0)
            heads.append(dict(hs=hs, lb=lb, xq=xq, sgq=sgq, sg=sg, sn=sn, f=f, q3=q3, kk3=kk3, v3=v3, do3=do3,
                              eb3=eb3, ebl3=ebl3, dec3=dec3, qe3=qe3, kd3=kd3, x_upd=x_upd, y_upd=y_upd,
                              d_a=d_a, d_at=d_at))
        for g, hd in enumerate(heads):
            st = sst_ref[g]
            for c in range(nc):
                states_ref[g, c] = st
                st = st * hd["ebl3"][c] + hd["x_upd"][c]
            dst = dst_ref[g]
            for c in reversed(range(nc)):
                dstates_ref[g, c] = dst
                dst = dst * hd["ebl3"][c] + hd["y_upd"][c]
            dst_ref[g] = dst
        for g, hd in enumerate(heads):
            q3, v3, do3, kd3 = hd["q3"], hd["v3"], hd["do3"], hd["kd3"]
            states, dstates = states_ref[g], dstates_ref[g]
            hd["dqe"] = _bdot(do3, states, 2, 1)
            hd["dkd"] = _bdot(v3, dstates, 2, 1)
            r = _bdot(kslab_ref[g], q3, 2, 2)
            a_t = jnp.zeros((nc, CH, CH), F32)
            for t in range(CH):
                a_t = a_t + jnp.where(col == t, r[:, t * CH:(t + 1) * CH, :], 0.0)
            a_t = jnp.where(triu, a_t, 0.0)
            hd["dv"] = _bdot(kd3, dstates, 2, 2) + _bdot(a_t, do3, 2, 1)
            hd["dq_in"] = _bdot(tile(hd["d_a"]), kslab_ref[g], 2, 1)
            hd["dk_in"] = _bdot(tile(hd["d_at"]), lslab_ref[g], 2, 1)
            hd["ss"] = jnp.sum(dstates * states, axis=1, keepdims=True)
        for g, hd in enumerate(heads):
            q3, kk3, eb3, ebl3, dec3, qe3, kd3 = (hd[k] for k in ("q3", "kk3", "eb3", "ebl3", "dec3", "qe3", "kd3"))
            dqe, dkd, dq_in, dk_in = hd["dqe"], hd["dkd"], hd["dq_in"], hd["dk_in"]
            dkd_kd = dkd * kd3
            db = dqe * qe3 - dkd_kd + q3 * dq_in - kk3 * dk_in
            dbl = jnp.sum(dkd_kd, axis=1, keepdims=True) + hd["ss"] * ebl3
            dg = _tri_dot(suffix, flat(db + jnp.where(last, dbl, 0.0)))
            df = dg / hd["f"] - flat(dkd * dec3 + dk_in)
            xq, sgq, hs = hd["xq"], hd["sgq"], hd["hs"]
            dxq_ref[:, hs] = (flat(dqe * eb3 + dq_in) * (sgq * (1.0 + xq * (1.0 - sgq)))).astype(MM)
            dxf_ref[:, hs] = (df * (1.0 - hd["lb"]) * hd["sg"] * hd["sn"]).astype(MM)
            dxi_ref[:, hs] = flat(hd["dv"]).astype(MM)
            dlb_ref[:, hs] += jnp.sum(df * hd["sn"], axis=0, keepdims=True)

    wide = HG_GROUP * HG_D
    rev = lambda i: nb - 1 - i
    sec = lambda j: pl.BlockSpec((None, rb, wide), lambda h, i, j=j: (j, rev(i), h))
    blk = pl.BlockSpec((rb, wide), lambda h, i: (rev(i), h))
    state = (pltpu.VMEM((HG_GROUP, nc, HG_D, HG_D), F32), pltpu.VMEM((HG_GROUP, nc, CH * CH, HG_D), MM))
    return pl.pallas_call(
        body, name="hgrn_bwd", grid=(HG_HEADS // HG_GROUP, nb),
        in_specs=[sec(0), sec(1), sec(2), pl.BlockSpec((2, wide), lambda h, i: (0, h)), blk,
                  pl.BlockSpec((None, HG_GROUP, HG_D, HG_D), lambda h, i: (rev(i), h, 0, 0))],
        out_specs=[blk, blk, blk, pl.BlockSpec((1, wide), lambda h, i: (0, h))],
        out_shape=[jax.ShapeDtypeStruct((s, SEC_W), MM)] * 3 + [jax.ShapeDtypeStruct((1, SEC_W), F32)],
        scratch_shapes=[pltpu.VMEM((HG_GROUP, HG_D, HG_D), F32), state[0], state[0], state[1], state[1]],
        compiler_params=_params("parallel", "arbitrary"),
    )(proj, proj, proj, lb_logits, d_o, sst)


def _rope_tables(s):
    half = AT_DH // 2
    inv_freq = 1.0 / (ROPE_THETA ** (jnp.arange(half, dtype=F32) / half))
    ang = jnp.arange(s, dtype=jnp.int32).astype(F32)[:, None] * inv_freq[None, :]
    cos, sin = jnp.cos(ang), jnp.sin(ang)
    return jnp.concatenate([cos] * 4, axis=-1), jnp.concatenate([-sin, sin] * 2, axis=-1)


def _rope128(x, cos, sin):
    lo = (_iota2(x.shape, 1) % AT_DH) < AT_DH // 2
    rot = jnp.where(lo, pltpu.roll(x, 128 - AT_DH // 2, 1), pltpu.roll(x, AT_DH // 2, 1))
    return x * cos + rot * sin


LANE_GROUPS = SEC_W // 128


def _set_lanes(ref, val):
    for j in range(LANE_GROUPS):
        ref[j] = val[:, j * 128:(j + 1) * 128]


def _get_lanes(ref):
    return jnp.concatenate([ref[j] for j in range(LANE_GROUPS)], axis=-1)


def _to_view(src_ref, dst_ref, d):
    n = src_ref.shape[1] // d
    for r in range(d):
        rows = pl.ds(r, n, stride=d) if d > 1 else slice(None)
        for j in range(LANE_GROUPS):
            c0 = r * SEC_W + j * 128
            dst_ref[:, c0:c0 + 128] = src_ref.at[j][rows, :].astype(dst_ref.dtype)


def _from_view(src_ref, dst_ref, d):
    n = dst_ref.shape[1] // d
    for r in range(d):
        for j in range(LANE_GROUPS):
            c0 = r * SEC_W + j * 128
            dst_ref.at[j][pl.ds(r, n, stride=d), :] = src_ref[:, c0:c0 + 128].astype(dst_ref.dtype)


def _view_spec(tm, d):
    return pl.BlockSpec((tm // d, d * SEC_W), lambda i: (i, 0))


def _view_shape(s, d, dtype):
    return jax.ShapeDtypeStruct((s // d, d * SEC_W), dtype)


PROJ_KEPT = (0, 1, 2, 3, 7)


def _inproj_fwd(x, norm_w, w_all, cos, sin, tm=256):
    s = x.shape[0]

    def body(x_ref, nw_ref, w_ref, cos_ref, sin_ref, proj_ref, *refs):
        outs, (qs_ref, ks_ref, vs_ref) = refs[:-3], refs[-3:]
        xv = x_ref[...]
        rstd = lax.rsqrt(jnp.mean(xv * xv, axis=-1, keepdims=True) + NORM_EPS)
        u = (xv * rstd * nw_ref[...]).astype(MM)
        for slot, j in enumerate(PROJ_KEPT):
            proj_ref[slot] = jnp.dot(u, w_ref[j], preferred_element_type=F32)
        q, k, v = [jnp.dot(u, w_ref[j], preferred_element_type=F32) for j in (4, 5, 6)]
        c, sn = cos_ref[...], sin_ref[...]
        for g in range(LANE_GROUPS):
            sl = slice(g * 128, (g + 1) * 128)
            qs_ref[g] = _rope128(q[:, sl], c, sn) * (AT_DH ** -0.5)
            ks_ref[g] = _rope128(k[:, sl], c, sn)
            vs_ref[g] = v[:, sl]
        for i, d in enumerate(DILATIONS):
            for src_ref, dst_ref in zip((qs_ref, ks_ref, vs_ref), outs[3 * i:3 * i + 3]):
                _to_view(src_ref, dst_ref, d)

    tab = pl.BlockSpec((tm, 128), lambda i: (i, 0))
    return pl.pallas_call(
        body, name="inproj_fwd", grid=(s // tm,),
        in_specs=[pl.BlockSpec((tm, D_MODEL), lambda i: (i, 0)),
                  pl.BlockSpec((1, D_MODEL), lambda i: (0, 0)),
                  pl.BlockSpec((N_SEC, D_MODEL, SEC_W), lambda i: (0, 0, 0)), tab, tab],
        out_specs=[pl.BlockSpec((len(PROJ_KEPT), tm, SEC_W), lambda i: (0, i, 0))]
                  + [_view_spec(tm, d) for d in DILATIONS for _ in range(3)],
        out_shape=[jax.ShapeDtypeStruct((len(PROJ_KEPT), s, SEC_W), F32)]
                  + [_view_shape(s, d, MM) for d in DILATIONS for _ in range(3)],
        scratch_shapes=[pltpu.VMEM((LANE_GROUPS, tm, 128), F32)] * 3,
        compiler_params=_params("parallel"),
    )(x, norm_w, w_all, cos, sin)


def _band_mask(first_ok, second_ok):
    row, col = _iota2((ATT_BLK, 2 * ATT_BLK), 0), _iota2((ATT_BLK, 2 * ATT_BLK), 1)
    return ((col < ATT_BLK) & (col >= row) & first_ok) | ((col >= ATT_BLK) & ((col - ATT_BLK) <= row) & second_ok)


def _own_lanes(rows, h):
    lane = _iota2((rows, 128), 1)
    return (lane < AT_DH) if h == 0 else (lane >= AT_DH)


def _neg_pieces(rows, h):
    lane = _iota2((rows, 128), 1) - (AT_DH if h == 0 else 0)
    return jnp.where((lane >= 0) & (lane < 3), -1.0, 0.0).astype(MM)


def _units():
    return [(b, slice(g * 128, (g + 1) * 128), h) for b in range(AT_QB) for g in range(AT_COLS // 128) for h in range(2)]


def _sub(b):
    return slice(b * ATT_BLK, (b + 1) * ATT_BLK)


def _band_before(cur_ref, prev_ref, b, sl):
    if b == 0:
        return jnp.concatenate([prev_ref[:, sl], cur_ref[0:ATT_BLK, sl]], axis=0)
    return cur_ref[(b - 1) * ATT_BLK:(b + 1) * ATT_BLK, sl]


def _band_after(cur_ref, next_ref, b, sl):
    if b == AT_QB - 1:
        return jnp.concatenate([cur_ref[b * ATT_BLK:(b + 1) * ATT_BLK, sl], next_ref[:, sl]], axis=0)
    return cur_ref[b * ATT_BLK:(b + 2) * ATT_BLK, sl]


def _attn_specs(rows):
    assert rows % (AT_QB * ATT_BLK) == 0
    last = rows // ATT_BLK - 1
    cur = pl.BlockSpec((AT_QB * ATT_BLK, AT_COLS), lambda c, n: (n, c))
    prev = pl.BlockSpec((ATT_BLK, AT_COLS), lambda c, n: (jnp.maximum(AT_QB * n - 1, 0), c))
    nxt = pl.BlockSpec((ATT_BLK, AT_COLS), lambda c, n: (jnp.minimum(AT_QB * (n + 1), last), c))
    return cur, prev, nxt


def _attn_fwd(qr, kr, vr, d):
    rows, cols = qr.shape
    nb = rows // (AT_QB * ATT_BLK)

    def body(q_ref, kc_ref, kp_ref, vc_ref, vp_ref, o_ref, lse_ref):
        masks = {True: _band_mask(pl.program_id(1) > 0, True), False: _band_mask(True, True)}
        ones = jnp.ones((2 * ATT_BLK, 128), MM)
        head0 = _own_lanes(ATT_BLK, 0)
        units = _units()
        scs = []
        for b, sl, h in units:
            q2 = q_ref[_sub(b), sl]
            qh = jnp.where(_own_lanes(ATT_BLK, h), q2, jnp.zeros_like(q2))
            scs.append(jnp.where(masks[b == 0], _dot_nt(qh, _band_before(kc_ref, kp_ref, b, sl)), NEG))
        ms = [jnp.max(sc, axis=-1, keepdims=True) for sc in scs]
        ps = [jnp.exp(sc - m).astype(MM) for sc, m in zip(scs, ms)]
        ls = [jnp.dot(p, ones, preferred_element_type=F32) for p in ps]
        os_ = [jnp.dot(p, _band_before(vc_ref, vp_ref, b, sl), preferred_element_type=F32)
               for p, (b, sl, _) in zip(ps, units)]
        for i in range(0, len(units), 2):
            b, sl, _ = units[i]
            l = jnp.where(head0, ls[i], ls[i + 1])
            o_ref[_sub(b), sl] = jnp.where(head0, os_[i], os_[i + 1]) / l
            lse_ref[_sub(b), sl] = jnp.where(head0, ms[i], ms[i + 1]) + jnp.log(l)

    cur, prev, _ = _attn_specs(rows)
    o, lse = pl.pallas_call(
        body, name=f"attn_fwd_d{d}", grid=(cols // AT_COLS, nb),
        in_specs=[cur, cur, prev, cur, prev], out_specs=[cur, cur],
        out_shape=[jax.ShapeDtypeStruct((rows, cols), F32)] * 2,
        compiler_params=_params("parallel", "parallel"),
    )(qr, kr, kr, vr, vr)
    return o, lse


def _attn_bwd_dq(qr, kr, vr, do, lse, delta, d):
    rows, cols = qr.shape
    nb = rows // (AT_QB * ATT_BLK)

    def body(q_ref, kc_ref, kp_ref, vc_ref, vp_ref, do_ref, lse_ref, dl_ref, dq_ref):
        masks = {True: _band_mask(pl.program_id(1) > 0, True), False: _band_mask(True, True)}
        units = _units()
        sms, dps = [], []
        for b, sl, h in units:
            own, own_b, neg = _own_lanes(ATT_BLK, h), _own_lanes(2 * ATT_BLK, h), _neg_pieces(2 * ATT_BLK, h)
            sms.append(_dot_nt(jnp.where(own, q_ref[_sub(b), sl], lse_ref[_sub(b), sl]),
                               jnp.where(own_b, _band_before(kc_ref, kp_ref, b, sl), neg)))
            dps.append(_dot_nt(jnp.where(own, do_ref[_sub(b), sl], dl_ref[_sub(b), sl]),
                               jnp.where(own_b, _band_before(vc_ref, vp_ref, b, sl), neg)))
        dss = [(jnp.exp(jnp.where(masks[b == 0], sm, NEG)) * dp).astype(MM)
               for sm, dp, (b, _, _) in zip(sms, dps, units)]
        dqs = [jnp.dot(ds, _band_before(kc_ref, kp_ref, b, sl), preferred_element_type=F32) * (AT_DH ** -0.5)
               for ds, (b, sl, _) in zip(dss, units)]
        for i in range(0, len(units), 2):
            b, sl, _ = units[i]
            dq_ref[_sub(b), sl] = jnp.where(_own_lanes(ATT_BLK, 0), dqs[i], dqs[i + 1]).astype(dq_ref.dtype)

    cur, prev, _ = _attn_specs(rows)
    dq = pl.pallas_call(
        body, name=f"attn_bwd_dq_d{d}", grid=(cols // AT_COLS, nb),
        in_specs=[cur, cur, prev, cur, prev, cur, cur, cur], out_specs=cur,
        out_shape=jax.ShapeDtypeStruct((rows, cols), MM),
        compiler_params=_params("parallel", "parallel"),
    )(qr, kr, kr, vr, vr, do, lse, delta)
    return dq


def _attn_bwd_dkv(qr, kr, vr, do, lse, delta, d):
    rows, cols = qr.shape
    nb = rows // (AT_QB * ATT_BLK)

    def body(k_ref, v_ref, qc_ref, qn_ref, doc_ref, don_ref, lsec_ref, lsen_ref, dlc_ref, dln_ref,
             dk_ref, dv_ref):
        masks = {True: _band_mask(True, pl.program_id(1) < nb - 1), False: _band_mask(True, True)}
        units = _units()
        sms, dps = [], []
        for b, sl, h in units:
            own, own_b, neg = _own_lanes(ATT_BLK, h), _own_lanes(2 * ATT_BLK, h), _neg_pieces(ATT_BLK, h)
            sms.append(_dot_nt(jnp.where(own, k_ref[_sub(b), sl], neg),
                               jnp.where(own_b, _band_after(qc_ref, qn_ref, b, sl),
                                         _band_after(lsec_ref, lsen_ref, b, sl))))
            dps.append(_dot_nt(jnp.where(own, v_ref[_sub(b), sl], neg),
                               jnp.where(own_b, _band_after(doc_ref, don_ref, b, sl),
                                         _band_after(dlc_ref, dln_ref, b, sl))))
        ps = [jnp.exp(jnp.where(masks[b == AT_QB - 1], sm, NEG)) for sm, (b, _, _) in zip(sms, units)]
        dss = [(p * dp).astype(MM) for p, dp in zip(ps, dps)]
        dvs = [jnp.dot(p.astype(MM), _band_after(doc_ref, don_ref, b, sl), preferred_element_type=F32)
               for p, (b, sl, _) in zip(ps, units)]
        dks = [jnp.dot(ds, _band_after(qc_ref, qn_ref, b, sl), preferred_element_type=F32)
               for ds, (b, sl, _) in zip(dss, units)]
        head0 = _own_lanes(ATT_BLK, 0)
        for i in range(0, len(units), 2):
            b, sl, _ = units[i]
            dk_ref[_sub(b), sl] = jnp.where(head0, dks[i], dks[i + 1]).astype(dk_ref.dtype)
            dv_ref[_sub(b), sl] = jnp.where(head0, dvs[i], dvs[i + 1]).astype(dv_ref.dtype)

    cur, _, nxt = _attn_specs(rows)
    dk, dv = pl.pallas_call(
        body, name=f"attn_bwd_dkv_d{d}", grid=(cols // AT_COLS, nb),
        in_specs=[cur, cur, cur, nxt, cur, nxt, cur, nxt, cur, nxt], out_specs=[cur, cur],
        out_shape=[jax.ShapeDtypeStruct((rows, cols), MM)] * 2,
        compiler_params=_params("parallel", "parallel"),
    )(kr, vr, qr, qr, do, do, lse, lse, delta, delta)
    return dk, dv


def _head_sum(a, width):
    parts = []
    for j in range(a.shape[1] // width):
        sm = jnp.sum(a[:, j * width:(j + 1) * width], axis=-1, keepdims=True)
        parts.append(jnp.broadcast_to(sm, (a.shape[0], width)))
    return jnp.concatenate(parts, axis=-1)


def _partner_pieces(x):
    xs = jnp.concatenate([pltpu.roll(x[:, j * 128:(j + 1) * 128], AT_DH, 1) for j in range(x.shape[1] // 128)],
                         axis=-1)
    hi = xs.astype(jnp.bfloat16).astype(F32)
    mid = (xs - hi).astype(jnp.bfloat16).astype(F32)
    lo = (xs - hi - mid).astype(jnp.bfloat16).astype(F32)
    lane = _iota2(x.shape, 1) % AT_DH
    return jnp.where(lane == 0, hi, jnp.where(lane == 1, mid, jnp.where(lane == 2, lo, 0.0)))


def _mid(x, tgt, proj, o_hg, o_at, lse_at, hg_norm_w, final_norm_w, wo_all, tm=256):
    s = x.shape[0]
    nb = s // tm

    def body(x_ref, t_ref, hgz_ref, atz_ref, ohg_ref, o1_ref, o2_ref, o3_ref, l1_ref, l2_ref, l3_ref,
             g_ref, fw_ref, wo_ref,
             dh_ref, dohg_ref, dhgz_ref, datz_ref, do1_ref, do2_ref, do3_ref, dl1_ref, dl2_ref, dl3_ref,
             lp1_ref, lp2_ref, lp3_ref,
             gwo_ref, gfw_ref, ghg_ref, loss_ref, nat_ref, stage_ref, gwo_acc):
        @pl.when(pl.program_id(0) == 0)
        def _():
            gwo_acc[...] = jnp.zeros_like(gwo_acc)
            gfw_ref[...] = jnp.zeros_like(gfw_ref)
            ghg_ref[...] = jnp.zeros_like(ghg_ref)
            loss_ref[...] = jnp.zeros_like(loss_ref)

        ohg, g = ohg_ref[...], g_ref[...]
        rs = lax.rsqrt(_head_sum(ohg * ohg, HG_D) * (1.0 / HG_D) + NORM_EPS)
        on = ohg * rs
        hgz = hgz_ref[...]
        sz = _sigmoid(hgz)
        gate_hg = hgz * sz
        lses, outs = [l1_ref[...]], [o1_ref[...]]
        for k, (d, l_ref, o_ref) in enumerate(zip(DILATIONS[1:], (l2_ref, l3_ref), (o2_ref, o3_ref))):
            _from_view(l_ref, nat_ref.at[2 * k], d)
            _from_view(o_ref, nat_ref.at[2 * k + 1], d)
            lses.append(_get_lanes(nat_ref.at[2 * k]))
            outs.append(_get_lanes(nat_ref.at[2 * k + 1]))
        mx = jnp.maximum(jnp.maximum(lses[0], lses[1]), lses[2])
        es = [jnp.exp(l - mx) for l in lses]
        den = es[0] + es[1] + es[2]
        ws = [e / den for e in es]
        oat = ws[0] * outs[0] + ws[1] * outs[1] + ws[2] * outs[2]
        atz = atz_ref[...]
        sa = _sigmoid(atz)
        gate_at = atz * sa
        mixed = jnp.concatenate([on * g * gate_hg, oat * gate_at], axis=-1).astype(MM)
        h = x_ref[...] + jnp.dot(mixed, wo_ref[...], preferred_element_type=F32)
        rstd = lax.rsqrt(jnp.mean(h * h, axis=-1, keepdims=True) + NORM_EPS)
        hn = h * rstd
        fw = fw_ref[...]
        err = hn * fw - t_ref[...]
        loss_ref[...] += 0.5 * jnp.sum(jnp.mean(err * err, axis=-1, keepdims=True), axis=0, keepdims=True)
        dout = err * (1.0 / D_MODEL)
        gfw_ref[...] += jnp.sum(dout * hn, axis=0, keepdims=True)
        dhn = dout * fw
        dh = rstd * (dhn - hn * jnp.mean(dhn * hn, axis=-1, keepdims=True))
        dh_ref[...] = dh
        dh_mm = dh.astype(MM)
        gwo_acc[...] += _dot_tn(mixed, dh_mm)

        @pl.when(pl.program_id(0) == nb - 1)
        def _():
            gwo_ref[...] = gwo_acc[...].astype(gwo_ref.dtype)

        dmixed = _dot_nt(dh_mm, wo_ref[...])
        dm_hg = dmixed[:, :SEC_W]
        d_ong = dm_hg * gate_hg
        dhgz_ref[...] = (dm_hg * (on * g) * (sz * (1.0 + hgz * (1.0 - sz)))).astype(MM)
        ghg_ref[...] += jnp.sum(d_ong * on, axis=0, keepdims=True)
        d_on = d_ong * g
        dohg_ref[...] = rs * (d_on - on * (_head_sum(d_on * on, HG_D) * (1.0 / HG_D)))
        dm_at = dmixed[:, SEC_W:]
        d_oat = dm_at * gate_at
        datz_ref[...] = (dm_at * oat * (sa * (1.0 + atz * (1.0 - sa)))).astype(MM)
        drow = _head_sum(d_oat * oat, AT_DH)
        lse_all = mx + jnp.log(den)
        for val, dst_refs in ((d_oat, (do1_ref, do2_ref, do3_ref)),
                              (_partner_pieces(drow), (dl1_ref, dl2_ref, dl3_ref)),
                              (_partner_pieces(lse_all), (lp1_ref, lp2_ref, lp3_ref))):
            _set_lanes(stage_ref, val)
            for d, dst_ref in zip(DILATIONS, dst_refs):
                _to_view(stage_ref, dst_ref, d)

    row = lambda w: pl.BlockSpec((tm, w), lambda i: (i, 0))
    sec = lambda j: pl.BlockSpec((None, tm, SEC_W), lambda i, j=j: (j, i, 0))
    const = lambda shp: pl.BlockSpec(shp, lambda i: (0,) * len(shp))
    half = row(SEC_W)
    views = [_view_spec(tm, d) for d in DILATIONS]
    return pl.pallas_call(
        body, name="mid", grid=(nb,),
        in_specs=[row(D_MODEL), row(D_MODEL), sec(PROJ_KEPT.index(3)), sec(PROJ_KEPT.index(7)), half] + views * 2
                 + [const((1, SEC_W)), const((1, D_MODEL)), const((D_MODEL, D_MODEL))],
        out_specs=[row(D_MODEL)] + [half] * 3 + views * 3
                  + [const((D_MODEL, D_MODEL)), const((1, D_MODEL)), const((1, SEC_W)), const((1, 1))],
        out_shape=[jax.ShapeDtypeStruct((s, D_MODEL), F32), jax.ShapeDtypeStruct((s, SEC_W), F32)]
                  + [jax.ShapeDtypeStruct((s, SEC_W), MM)] * 2
                  + [_view_shape(s, d, MM) for d in DILATIONS] * 3
                  + [jax.ShapeDtypeStruct((D_MODEL, D_MODEL), XCH), jax.ShapeDtypeStruct((1, D_MODEL), F32),
                     jax.ShapeDtypeStruct((1, SEC_W), F32), jax.ShapeDtypeStruct((1, 1), F32)],
        scratch_shapes=[pltpu.VMEM((4, LANE_GROUPS, tm, 128), F32), pltpu.VMEM((LANE_GROUPS, tm, 128), F32),
                        pltpu.VMEM((D_MODEL, D_MODEL), F32)],
        compiler_params=_params("arbitrary"),
    )(x, tgt, proj, proj, o_hg, *o_at, *lse_at, hg_norm_w, final_norm_w, wo_all)


def _section_specs(dsecs, tm):
    return [pl.BlockSpec((tm, SEC_W), lambda i: (i, 0)) if k is None
            else pl.BlockSpec((None, tm, SEC_W), lambda i, k=k: (k, i, 0)) for _, k in dsecs]


def _inproj_bwd_x(x, norm_w, w_all, dh, dsecs, token, tm=256):
    s = x.shape[0]

    def body(x_ref, nw_ref, w_ref, dh_ref, tok_ref, *refs):
        sec_refs, (gx_ref, gnw_ref) = refs[:N_SEC], refs[N_SEC:]

        @pl.when(pl.program_id(0) == 0)
        def _():
            gnw_ref[...] = jnp.zeros_like(gnw_ref)

        du = jnp.zeros((tm, D_MODEL), F32)
        for j in range(N_SEC):
            du = du + _dot_nt(sec_refs[j][...], w_ref[j])
        xv, nw = x_ref[...], nw_ref[...]
        rstd = lax.rsqrt(jnp.mean(xv * xv, axis=-1, keepdims=True) + NORM_EPS)
        xn = xv * rstd
        gnw_ref[...] += jnp.sum(du * xn, axis=0, keepdims=True)
        dxn = du * nw
        dx = rstd * (dxn - xn * jnp.mean(dxn * xn, axis=-1, keepdims=True))
        gx_ref[...] = (dh_ref[...] + tok_ref[0:1, 0:1]) + dx

    row = lambda w: pl.BlockSpec((tm, w), lambda i: (i, 0))
    const = lambda shp: pl.BlockSpec(shp, lambda i: (0,) * len(shp))
    return pl.pallas_call(
        body, name="inproj_bwd_x", grid=(s // tm,),
        in_specs=[row(D_MODEL), const((1, D_MODEL)), const((N_SEC, D_MODEL, SEC_W)), row(D_MODEL), const((8, 128))]
                 + _section_specs(dsecs, tm),
        out_specs=[row(D_MODEL), const((1, D_MODEL))],
        out_shape=[jax.ShapeDtypeStruct((s, D_MODEL), F32), jax.ShapeDtypeStruct((1, D_MODEL), F32)],
        compiler_params=_params("arbitrary"),
    )(x, norm_w, w_all, dh, token, *[a for a, _ in dsecs])


def _inproj_bwd_w(x, norm_w, dsec, dq_r, dk_r, dv, cos, sin, tm=512):
    s = x.shape[0]
    nb = s // tm

    def body(x_ref, nw_ref, s0, s1, s2, s3, s7, q1, q2, q3, k1, k2, k3, v1, v2, v3, cos_ref, sin_ref,
             gw_hbm, datt_ref, acc_ref, stage_ref, nat_ref):
        @pl.when(pl.program_id(0) == 0)
        def _():
            acc_ref[...] = jnp.zeros_like(acc_ref)

        def total(refs):
            acc = refs[0][...].astype(F32)
            for d, ref in zip(DILATIONS[1:], refs[1:]):
                _from_view(ref, nat_ref, d)
                acc = acc + _get_lanes(nat_ref)
            return acc

        c, sn = cos_ref[...], -sin_ref[...]
        unrot = lambda a: jnp.concatenate(
            [_rope128(a[:, j * 128:(j + 1) * 128], c, sn) for j in range(LANE_GROUPS)], axis=-1)
        att = [a.astype(MM) for a in (unrot(total((q1, q2, q3))), unrot(total((k1, k2, k3))), total((v1, v2, v3)))]
        for j, a in enumerate(att):
            datt_ref[j] = a
        xv = x_ref[...]
        rstd = lax.rsqrt(jnp.mean(xv * xv, axis=-1, keepdims=True) + NORM_EPS)
        u_t = (xv * rstd * nw_ref[...]).T.astype(MM)
        for j, dsj in enumerate((s0[...], s1[...], s2[...], s3[...], *att, s7[...])):
            acc_ref[j] += jnp.dot(u_t, dsj, preferred_element_type=F32)

        @pl.when(pl.program_id(0) == nb - 1)
        def _():
            for j in range(N_SEC):
                stage_ref[...] = acc_ref[j].astype(stage_ref.dtype)
                pltpu.sync_copy(stage_ref, gw_hbm.at[j])

    row = lambda w: pl.BlockSpec((tm, w), lambda i: (i, 0))
    return pl.pallas_call(
        body, name="inproj_bwd_w", grid=(nb,),
        in_specs=[row(D_MODEL), pl.BlockSpec((1, D_MODEL), lambda i: (0, 0))] + [row(SEC_W)] * 5
                 + [_view_spec(tm, d) for d in DILATIONS] * 3 + [row(128), row(128)],
        out_specs=[pl.BlockSpec(memory_space=pl.ANY), pl.BlockSpec((3, tm, SEC_W), lambda i: (0, i, 0))],
        out_shape=[jax.ShapeDtypeStruct((N_SEC, D_MODEL, SEC_W), XCH), jax.ShapeDtypeStruct((3, s, SEC_W), MM)],
        scratch_shapes=[pltpu.VMEM((N_SEC, D_MODEL, SEC_W), F32), pltpu.VMEM((D_MODEL, SEC_W), XCH),
                        pltpu.VMEM((LANE_GROUPS, tm, 128), F32)],
        compiler_params=_params("arbitrary"),
    )(x, norm_w, *dsec, *dq_r, *dk_r, *dv, cos, sin)


def _local_step(x, tgt, norm_w, w_all, lb_logits, hg_norm_w, wo_all, final_norm_w, on_weight_grads):
    s = x.shape[0]
    cos, sin = _rope_tables(s)
    proj, *qkv = _inproj_fwd(x, norm_w, w_all, cos, sin)
    o_hg, sst = _hgrn_fwd(proj, lb_logits)
    qkv = [qkv[3 * i:3 * i + 3] for i in range(len(DILATIONS))]
    att = [_attn_fwd(*qkv_d, d) for qkv_d, d in zip(qkv, DILATIONS)]
    (dh, d_ohg, d_hgz, d_atz, do1, do2, do3, dl1, dl2, dl3, lp1, lp2, lp3, gwo, gfw, ghg, loss) = _mid(
        x, tgt, proj, o_hg, [a[0] for a in att], [a[1] for a in att], hg_norm_w, final_norm_w[None, :], wo_all)
    dxq, dxf, dxi, dlb = _hgrn_bwd(proj, lb_logits, d_ohg, sst)
    dq_r, dk_r, dv = [], [], []
    for d, qkv_d, do, lp, dl in zip(DILATIONS, qkv, (do1, do2, do3), (lp1, lp2, lp3), (dl1, dl2, dl3)):
        dq_r.append(_attn_bwd_dq(*qkv_d, do, lp, dl, d))
        dk_d, dv_d = _attn_bwd_dkv(*qkv_d, do, lp, dl, d)
        dk_r.append(dk_d)
        dv.append(dv_d)
    gwi, d_att = _inproj_bwd_w(x, norm_w, (dxq, dxf, dxi, d_hgz, d_atz), dq_r, dk_r, dv, cos, sin)
    dsecs = [(dxq, None), (dxf, None), (dxi, None), (d_hgz, None), (d_att, 0), (d_att, 1), (d_att, 2), (d_atz, None)]
    token = on_weight_grads(gwi, gwo)
    gx, gnw = _inproj_bwd_x(x, norm_w, w_all, dh, dsecs, token)
    small = jnp.concatenate([gnw, jnp.concatenate([dlb, ghg], axis=-1), gfw,
                             jnp.pad(loss, ((0, 0), (0, D_MODEL - 1)))], axis=0)
    return gx, gwi, gwo, small


def _coords():
    return lax.axis_index("x"), lax.axis_index("y"), lax.axis_index("c")


def _gather_weights(w_in, w_out):
    wo_rows = w_out.shape[0]

    def body(wi_ref, wo_ref, wi_all, wo_all, send_sems, recv_sems):
        x, y, c = _coords()
        me, sibling = (x, y, c), (x, y, 1 - c)
        chips = [(1 - x, y), (x, 1 - y), (1 - x, 1 - y)]
        slot = lambda p: 4 * p[0] + 2 * p[1] + p[2]

        def copies(k, block, to):
            return [pltpu.make_async_remote_copy(
                src_ref=ref.at[slot(block)], dst_ref=ref.at[slot(block)], send_sem=send_sems.at[a, k],
                recv_sem=recv_sems.at[a, k], device_id=to, device_id_type=MESH)
                for a, ref in enumerate((wi_all, wo_all))]

        wi_all[slot(me)] = wi_ref[...].astype(MM)
        wo_all[slot(me)] = wo_ref[...].astype(MM)
        first = copies(0, me, sibling)
        for j, chip in enumerate(chips):
            first += copies(1 + j, me, (*chip, c))
        for cp in first:
            cp.start()
        passed = []
        for j, chip in enumerate(chips):
            for cp in copies(1 + j, (*chip, c), me):
                cp.wait_recv()
            fwd = copies(4 + j, (*chip, c), sibling)
            for cp in fwd:
                cp.start()
            passed += fwd
        for cp in copies(0, sibling, me):
            cp.wait_recv()
        for j, chip in enumerate(chips):
            for cp in copies(4 + j, (*chip, 1 - c), me):
                cp.wait_recv()
        for cp in first + passed:
            cp.wait_send()

    vmem = pl.BlockSpec(memory_space=pltpu.VMEM)
    return pl.pallas_call(
        body, name="gather_weights",
        in_specs=[vmem, vmem], out_specs=[vmem, vmem],
        out_shape=[jax.ShapeDtypeStruct((N_DEV, D_MODEL, SEC_W), MM),
                   jax.ShapeDtypeStruct((N_DEV, wo_rows, D_MODEL), MM)],
        scratch_shapes=[pltpu.SemaphoreType.DMA((2, 7)), pltpu.SemaphoreType.DMA((2, 7))],
        compiler_params=pltpu.CompilerParams(vmem_limit_bytes=VMEM_LIMIT),
    )(w_in, w_out)


def _me():
    x, y, c = _coords()
    return 4 * x + 2 * y + c


def _grad_copies(srcs, lands, send_sems, recv_sems):
    x, y, c = _coords()
    me = 4 * x + 2 * y + c
    copies = []
    for k in range(1, N_DEV):
        px, py, pc = x ^ (k >> 2), y ^ ((k >> 1) & 1), c ^ (k & 1)
        peer = 4 * px + 2 * py + pc
        for a, (src, dst) in enumerate(zip(srcs, lands)):
            copies.append(pltpu.make_async_remote_copy(
                src_ref=src.at[peer], dst_ref=dst.at[me], send_sem=send_sems.at[a * (N_DEV - 1) + k - 1],
                recv_sem=recv_sems.at[a * (N_DEV - 1) + k - 1], device_id=(px, py, pc), device_id_type=MESH))
    return copies


HBM_SPEC = pl.BlockSpec(memory_space=pltpu.HBM)
SEM_SPEC = pl.BlockSpec(memory_space=pltpu.SEMAPHORE)
SPLIT_COPY_EFFECT = pltpu.SideEffectType.DATAFLOW_SIDE_EFFECTING


def _exchange_start(gwi, gwo):
    def body(gwi_ref, gwo_ref, li_ref, lo_ref, send_sems, recv_sems, gwi_thru, gwo_thru, li_thru, lo_thru, token):
        for cp in _grad_copies((gwi_ref, gwo_ref), (li_ref, lo_ref), send_sems, recv_sems):
            cp.start()
        token[...] = jnp.zeros_like(token)

    hbm = lambda a: pltpu.with_memory_space_constraint(a, pltpu.HBM)
    bufs = (gwi, gwo, lax.empty(gwi.shape, gwi.dtype), lax.empty(gwo.shape, gwo.dtype))
    return pl.pallas_call(
        body, name="exchange_start",
        out_shape=(pltpu.SemaphoreType.DMA((2 * (N_DEV - 1),)), pltpu.SemaphoreType.DMA((2 * (N_DEV - 1),)),
                   *[pltpu.HBM(a.shape, a.dtype) for a in bufs], jax.ShapeDtypeStruct((8, 128), F32)),
        in_specs=[HBM_SPEC] * 4,
        out_specs=(SEM_SPEC, SEM_SPEC, HBM_SPEC, HBM_SPEC, HBM_SPEC, HBM_SPEC, pl.BlockSpec(memory_space=pltpu.VMEM)),
        input_output_aliases={0: 2, 1: 3, 2: 4, 3: 5},
        compiler_params=pltpu.CompilerParams(has_side_effects=SPLIT_COPY_EFFECT),
    )(*[hbm(a) for a in bufs])


def _exchange_wait(send_sems, recv_sems, gwi, gwo, li, lo, after):
    def body(gwi_ref, gwo_ref, li_ref, lo_ref, send_sems, recv_sems, after_ref, gwi_out, gwo_out, li_out, lo_out):
        for cp in _grad_copies((gwi_ref, gwo_ref), (li_ref, lo_ref), send_sems, recv_sems):
            cp.wait_send()
            cp.wait_recv()

    return pl.pallas_call(
        body, name="exchange_wait",
        out_shape=tuple(pltpu.HBM(a.shape, a.dtype) for a in (gwi, gwo, li, lo)),
        in_specs=[HBM_SPEC] * 4 + [SEM_SPEC, SEM_SPEC, pl.BlockSpec(memory_space=pl.ANY)],
        out_specs=(HBM_SPEC,) * 4,
        input_output_aliases={0: 0, 1: 1, 2: 2, 3: 3},
        compiler_params=pltpu.CompilerParams(has_side_effects=SPLIT_COPY_EFFECT),
    )(gwi, gwo, li, lo, send_sems, recv_sems, after)


def _gather_small(small):
    def body(sm_ref, ls_ref, send_sems, recv_sems, local_sem):
        x, y, c = _coords()
        me = 4 * x + 2 * y + c
        own = pltpu.make_async_copy(sm_ref, ls_ref.at[me], local_sem)
        own.start()
        sends = []
        for k in range(1, N_DEV):
            peer = (x ^ (k >> 2), y ^ ((k >> 1) & 1), c ^ (k & 1))
            sends.append(pltpu.make_async_remote_copy(
                src_ref=sm_ref, dst_ref=ls_ref.at[me], send_sem=send_sems.at[k - 1], recv_sem=recv_sems.at[k - 1],
                device_id=peer, device_id_type=MESH))
        for cp in sends:
            cp.start()
        for cp in sends:
            cp.wait_recv()
        for cp in sends:
            cp.wait_send()
        own.wait()

    vmem = pl.BlockSpec(memory_space=pltpu.VMEM)
    return pl.pallas_call(
        body, name="gather_small", in_specs=[vmem], out_specs=vmem,
        out_shape=jax.ShapeDtypeStruct((N_DEV,) + small.shape, F32),
        scratch_shapes=[pltpu.SemaphoreType.DMA((N_DEV - 1,)), pltpu.SemaphoreType.DMA((N_DEV - 1,)),
                        pltpu.SemaphoreType.DMA],
    )(small)


def _adamw(w, g, m, v):
    m = ADAM_B1 * m + (1.0 - ADAM_B1) * g
    v = ADAM_B2 * v + (1.0 - ADAM_B2) * (g * g)
    m_hat = m / (1.0 - ADAM_B1 ** ADAM_STEP)
    v_hat = v / (1.0 - ADAM_B2 ** ADAM_STEP)
    return -ADAM_LR * (m_hat / (jnp.sqrt(v_hat) + ADAM_EPS) + ADAM_WD * w), m, v


def _slot_sum(ref, own=None, me=None):
    g = None
    for i in range(N_DEV):
        term = ref[i].astype(F32)
        if own is not None:
            term = jnp.where(i == me, own, term)
        g = term if g is None else g + term
    return g


def _update_matrix(name, me, landed, own, w, m, v, rows):
    r, c = w.shape

    def body(me_ref, l_ref, own_ref, w_ref, m_ref, v_ref, g_ref, d_ref, nm_ref, nv_ref):
        g = _slot_sum(l_ref, own_ref[...].astype(F32), me_ref[0])
        g_ref[...] = g
        d_ref[...], nm_ref[...], nv_ref[...] = _adamw(w_ref[...], g, m_ref[...], v_ref[...])

    blk = pl.BlockSpec((rows, c), lambda i, me_ref: (i, 0))
    return pl.pallas_call(
        body, name=name,
        grid_spec=pltpu.PrefetchScalarGridSpec(
            num_scalar_prefetch=1, grid=(r // rows,),
            in_specs=[pl.BlockSpec((N_DEV, rows, c), lambda i, me_ref: (0, i, 0)),
                      pl.BlockSpec((None, rows, c), lambda i, me_ref: (me_ref[0], i, 0)), blk, blk, blk],
            out_specs=[blk] * 4),
        out_shape=[jax.ShapeDtypeStruct((r, c), F32)] * 4,
        compiler_params=_params("parallel"),
    )(me, landed, own, w, m, v)


def _update_small(landed, lb_logits, ws, ms, vs):
    def body(l_ref, lbl_ref, w_ref, m_ref, v_ref, g_ref, d_ref, nm_ref, nv_ref, loss_ref):
        tot = _slot_sum(l_ref)
        _, dlb = _lower_bound(lbl_ref[...])
        g_lb = tot[1:2, :SEC_W] * dlb
        g = jnp.concatenate([tot[0:1], jnp.concatenate([g_lb, -g_lb], axis=-1),
                             jnp.pad(tot[1:2, SEC_W:], ((0, 0), (0, SEC_W))), tot[2:3]], axis=0)
        g_ref[...] = g
        d_ref[...], nm_ref[...], nv_ref[...] = _adamw(w_ref[...], g, m_ref[...], v_ref[...])
        loss_ref[...] = tot[3:4, 0:1]

    vmem = pl.BlockSpec(memory_space=pltpu.VMEM)
    return pl.pallas_call(
        body, name="update_small", in_specs=[vmem] * 5, out_specs=[vmem] * 5,
        out_shape=[jax.ShapeDtypeStruct((4, D_MODEL), F32)] * 4 + [jax.ShapeDtypeStruct((1, 1), F32)],
    )(landed, lb_logits, ws, ms, vs)


def _pack_small(norm_w, lb_logits, hg_norm_w, final_norm_w):
    return jnp.concatenate([norm_w, lb_logits.reshape(1, D_MODEL),
                            jnp.pad(hg_norm_w, ((0, 0), (0, D_MODEL - SEC_W))), final_norm_w[None, :]], axis=0)


def _unpack_small(a):
    return a[0:1], a[1].reshape(2, SEC_W), a[2:3, :SEC_W], a[3]


def kernel(x, norm_w, w_in, hgrn_lb_logits, hg_norm_w, w_out, final_norm_w, loss_target, m_norm_w, m_w_in, m_hgrn_lb_logits, m_hg_norm_w, m_w_out, m_final_norm_w, v_norm_w, v_w_in, v_hgrn_lb_logits, v_hg_norm_w, v_w_out, v_final_norm_w):
    w_all, wo_all = _gather_weights(w_in[0], w_out[0])
    in_flight = []

    def start_exchange(gwi, gwo):
        *handles, token = _exchange_start(gwi, gwo.reshape(N_DEV, D_MODEL // N_DEV, D_MODEL))
        in_flight.extend(handles)
        return token

    gx, _, _, small = _local_step(x[0], loss_target[0], norm_w, w_all, hgrn_lb_logits, hg_norm_w,
                                  wo_all.reshape(D_MODEL, D_MODEL), final_norm_w, start_exchange)
    ls = _gather_small(small)
    gwi, gwo, li, lo = _exchange_wait(*in_flight, gx)
    me = _me().astype(jnp.int32).reshape(1)
    g_wi, d_wi, nm_wi, nv_wi = _update_matrix("update_w_in", me, li, gwi, w_in[0], m_w_in[0], v_w_in[0], 256)
    g_wo, d_wo, nm_wo, nv_wo = _update_matrix("update_w_out", me, lo, gwo, w_out[0], m_w_out[0], v_w_out[0], 128)
    g_s, d_s, nm_s, nv_s, loss = _update_small(
        ls, hgrn_lb_logits, _pack_small(norm_w, hgrn_lb_logits, hg_norm_w, final_norm_w),
        _pack_small(m_norm_w, m_hgrn_lb_logits, m_hg_norm_w, m_final_norm_w),
        _pack_small(v_norm_w, v_hgrn_lb_logits, v_hg_norm_w, v_final_norm_w))
    outs = []
    for small_out, wi, wo in ((g_s, g_wi, g_wo), (d_s, d_wi, d_wo), (nm_s, nm_wi, nm_wo), (nv_s, nv_wi, nv_wo)):
        nw, lb, hg, fw = _unpack_small(small_out)
        outs += [nw, wi[None], lb, hg, wo[None], fw]
    return (loss[0, 0], gx[None], *outs)
```

```python
import functools

import jax
import jax.numpy as jnp
from jax import lax
from jax.experimental import pallas as pl
from jax.experimental.pallas import tpu as pltpu

F32 = jnp.float32
MM = jnp.bfloat16
XCH = jnp.bfloat16
NORM_EPS = 1e-6
NEG = -1e30
N_DEV = 8
D_MODEL = 1024
N_SEC = 8
SEC_W = 512
HG_HEADS = 4
HG_D = 128
HG_GROUP = 4
AT_HEADS = 8
AT_DH = 64
ATT_BLK = 128
AT_COLS = 512
AT_QB = 4
DILATIONS = (1, 4, 16)
ROPE_THETA = 10000.0
CH = 16
LB_LO, LB_HI = 1e-6, 1.0 - 1e-6
ADAM_LR, ADAM_B1, ADAM_B2, ADAM_EPS, ADAM_WD, ADAM_STEP = 0.001, 0.9, 0.999, 1e-08, 0.01, 10
VMEM_LIMIT = 56 * 1024 * 1024
MESH = pl.DeviceIdType.MESH


def _params(*sem):
    return pltpu.CompilerParams(dimension_semantics=sem, vmem_limit_bytes=VMEM_LIMIT)


def _sigmoid(x):
    return 1.0 / (1.0 + jnp.exp(-x))


def _dot(a, b):
    return jnp.dot(a.astype(MM), b.astype(MM), preferred_element_type=F32)


def _dot_nt(a, b):
    return lax.dot_general(a.astype(MM), b.astype(MM), (((1,), (1,)), ((), ())), preferred_element_type=F32)


def _dot_tn(a, b):
    return lax.dot_general(a.astype(MM), b.astype(MM), (((0,), (0,)), ((), ())), preferred_element_type=F32)


def _tri_dot(tri, g):
    g1 = g.astype(jnp.bfloat16)
    r1 = g - g1.astype(F32)
    g2 = r1.astype(jnp.bfloat16)
    g3 = (r1 - g2.astype(F32)).astype(jnp.bfloat16)
    t = tri.astype(jnp.bfloat16)
    d = functools.partial(jnp.dot, preferred_element_type=F32)
    return d(t, g1) + d(t, g2) + d(t, g3)


def _lower_bound(lbl):
    l0, l1 = lbl[0:1, :], lbl[1:2, :]
    m = jnp.maximum(l0, l1)
    e0, e1 = jnp.exp(l0 - m), jnp.exp(l1 - m)
    p = e0 / (e0 + e1)
    inside = (p >= LB_LO) & (p <= LB_HI)
    return jnp.clip(p, LB_LO, LB_HI), jnp.where(inside, p * (e1 / (e0 + e1)), 0.0)


def _iota2(shape, dim):
    return lax.broadcasted_iota(jnp.int32, shape, dim)


def _hgrn_gates(xq, xf, lb):
    sgq = _sigmoid(xq)
    sg = _sigmoid(xf)
    sn = _sigmoid(-xf)
    f = lb + (1.0 - lb) * sg
    return sgq, xq * sgq, sg, sn, f, (1.0 - lb) * sn


def _bdot(a, b, ca, cb):
    return lax.dot_general(a.astype(MM), b.astype(MM), (((ca,), (cb,)), ((0,), (0,))), preferred_element_type=F32)


def _chunk_masks(rb):
    row, col = _iota2((rb, rb), 0), _iota2((rb, rb), 1)
    same = (row // CH) == (col // CH)
    return same & (row >= col), same & (row <= col)


def _hgrn_fwd(proj, lb_logits, rb=256):
    s = proj.shape[1]
    nb, nc = s // rb, rb // CH

    def body(q_ref, f_ref, i_ref, lbl_ref, o_ref, sst_ref, st_ref, slab_ref, states_ref):
        @pl.when(pl.program_id(1) == 0)
        def _():
            st_ref[...] = jnp.zeros_like(st_ref)

        sst_ref[...] = st_ref[...]
        prefix, _ = _chunk_masks(rb)
        c3 = lambda a: a.reshape(nc, CH, HG_D)
        row, col = _iota2((nc, CH, CH), 1), _iota2((nc, CH, CH), 2)
        heads = []
        for g in range(HG_GROUP):
            hs = slice(g * HG_D, (g + 1) * HG_D)
            lb, _ = _lower_bound(lbl_ref[:, hs])
            _, q, _, _, f, kk = _hgrn_gates(q_ref[:, hs], f_ref[:, hs], lb)
            b3 = c3(_tri_dot(prefix, jnp.log(f)))
            q3, kk3, v3 = c3(q), c3(kk), c3(i_ref[:, hs])
            bl3 = b3[:, CH - 1:CH, :]
            for t in range(CH):
                slab_ref[g, :, t * CH:(t + 1) * CH, :] = (
                    q3 * jnp.exp(jnp.minimum(b3 - b3[:, t:t + 1, :], 0.0))).astype(MM)
            x_upd = _bdot(v3, kk3 * jnp.exp(bl3 - b3), 1, 1)
            heads.append(dict(hs=hs, kk3=kk3, v3=v3, qe3=q3 * jnp.exp(b3), ebl3=jnp.exp(bl3), x_upd=x_upd))
        for g, hd in enumerate(heads):
            st = st_ref[g]
            for c in range(nc):
                states_ref[g, c] = st
                st = st * hd["ebl3"][c] + hd["x_upd"][c]
            st_ref[g] = st
        for g, hd in enumerate(heads):
            r = _bdot(slab_ref[g], hd["kk3"], 2, 2)
            a = jnp.zeros((nc, CH, CH), F32)
            for t in range(CH):
                a = a + jnp.where(col == t, r[:, t * CH:(t + 1) * CH, :], 0.0)
            a = jnp.where(row >= col, a, 0.0)
            o3 = _bdot(hd["qe3"], states_ref[g], 2, 2) + _bdot(a, hd["v3"], 2, 1)
            o_ref[:, hd["hs"]] = o3.reshape(rb, HG_D)

    wide = HG_GROUP * HG_D
    sec = lambda j: pl.BlockSpec((None, rb, wide), lambda h, i, j=j: (j, i, h))
    return pl.pallas_call(
        body, name="hgrn_fwd", grid=(HG_HEADS // HG_GROUP, nb),
        in_specs=[sec(0), sec(1), sec(2), pl.BlockSpec((2, wide), lambda h, i: (0, h))],
        out_specs=[pl.BlockSpec((rb, wide), lambda h, i: (i, h)),
                   pl.BlockSpec((None, HG_GROUP, HG_D, HG_D), lambda h, i: (i, h, 0, 0))],
        out_shape=[jax.ShapeDtypeStruct((s, SEC_W), F32),
                   jax.ShapeDtypeStruct((nb, HG_HEADS, HG_D, HG_D), F32)],
        scratch_shapes=[pltpu.VMEM((HG_GROUP, HG_D, HG_D), F32), pltpu.VMEM((HG_GROUP, nc, CH * CH, HG_D), MM),
                        pltpu.VMEM((HG_GROUP, nc, HG_D, HG_D), F32)],
        compiler_params=_params("parallel", "arbitrary"),
    )(proj, proj, proj, lb_logits)


def _hgrn_bwd(proj, lb_logits, d_o, sst, rb=256):
    s = proj.shape[1]
    nb, nc = s // rb, rb // CH

    def body(q_ref, f_ref, i_ref, lbl_ref, do_ref, sst_ref, dxq_ref, dxf_ref, dxi_ref, dlb_ref,
             dst_ref, states_ref, dstates_ref, lslab_ref, kslab_ref):
        @pl.when(pl.program_id(1) == 0)
        def _():
            dst_ref[...] = jnp.zeros_like(dst_ref)
            dlb_ref[...] = jnp.zeros_like(dlb_ref)

        prefix, suffix = _chunk_masks(rb)
        c3 = lambda a: a.reshape(nc, CH, HG_D)
        flat = lambda a: a.reshape(rb, HG_D)
        row, col = _iota2((nc, CH, CH), 1), _iota2((nc, CH, CH), 2)
        tril, triu = row >= col, row <= col
        sel = (_iota2((CH, CH * CH), 1) % CH == _iota2((CH, CH * CH), 0)).astype(MM)
        blockdiag = _iota2((nc, CH, CH * CH), 2) // CH == _iota2((nc, CH, CH * CH), 1)
        tile = lambda m: jnp.where(blockdiag, _dot(m.reshape(rb, CH), sel).reshape(nc, CH, CH * CH), 0.0)
        last = _iota2((nc, CH, HG_D), 1) == CH - 1
        heads = []
        for g in range(HG_GROUP):
            hs = slice(g * HG_D, (g + 1) * HG_D)
            lb, _ = _lower_bound(lbl_ref[:, hs])
            xq = q_ref[:, hs]
            sgq, q, sg, sn, f, kk = _hgrn_gates(xq, f_ref[:, hs], lb)
            b3 = c3(_tri_dot(prefix, jnp.log(f)))
            q3, kk3, v3, do3 = c3(q), c3(kk), c3(i_ref[:, hs]), c3(do_ref[:, hs])
            bl3 = b3[:, CH - 1:CH, :]
            eb3, ebl3, dec3 = jnp.exp(b3), jnp.exp(bl3), jnp.exp(bl3 - b3)
            qe3, kd3 = q3 * eb3, kk3 * dec3
            x_upd, y_upd = _bdot(v3, kd3, 1, 1), _bdot(do3, qe3, 1, 1)
            for t in range(CH):
                bt = b3[:, t:t + 1, :]
                lslab_ref[g, :, t * CH:(t + 1) * CH, :] = (q3 * jnp.exp(jnp.minimum(b3 - bt, 0.0))).astype(MM)
                kslab_ref[g, :, t * CH:(t + 1) * CH, :] = (kk3 * jnp.exp(jnp.minimum(bt - b3, 0.0))).astype(MM)
            d_a = jnp.where(tril, _bdot(do3, v3, 2, 2), 0.0)
            d_at = jnp.where(triu, _bdot(v3, do3, 2, 2), 0.0)
            heads.append(dict(hs=hs, lb=lb, xq=xq, sgq=sgq, sg=sg, sn=sn, f=f, q3=q3, kk3=kk3, v3=v3, do3=do3,
                              eb3=eb3, ebl3=ebl3, dec3=dec3, qe3=qe3, kd3=kd3, x_upd=x_upd, y_upd=y_upd,
                              d_a=d_a, d_at=d_at))
        for g, hd in enumerate(heads):
            st = sst_ref[g]
            for c in range(nc):
                states_ref[g, c] = st
                st = st * hd["ebl3"][c] + hd["x_upd"][c]
            dst = dst_ref[g]
            for c in reversed(range(nc)):
                dstates_ref[g, c] = dst
                dst = dst * hd["ebl3"][c] + hd["y_upd"][c]
            dst_ref[g] = dst
        for g, hd in enumerate(heads):
            q3, v3, do3, kd3 = hd["q3"], hd["v3"], hd["do3"], hd["kd3"]
            states, dstates = states_ref[g], dstates_ref[g]
            hd["dqe"] = _bdot(do3, states, 2, 1)
            hd["dkd"] = _bdot(v3, dstates, 2, 1)
            r = _bdot(kslab_ref[g], q3, 2, 2)
            a_t = jnp.zeros((nc, CH, CH), F32)
            for t in range(CH):
                a_t = a_t + jnp.where(col == t, r[:, t * CH:(t + 1) * CH, :], 0.0)
            a_t = jnp.where(triu, a_t, 0.0)
            hd["dv"] = _bdot(kd3, dstates, 2, 2) + _bdot(a_t, do3, 2, 1)
            hd["dq_in"] = _bdot(tile(hd["d_a"]), kslab_ref[g], 2, 1)
            hd["dk_in"] = _bdot(tile(hd["d_at"]), lslab_ref[g], 2, 1)
            hd["ss"] = jnp.sum(dstates * states, axis=1, keepdims=True)
        for g, hd in enumerate(heads):
            q3, kk3, eb3, ebl3, dec3, qe3, kd3 = (hd[k] for k in ("q3", "kk3", "eb3", "ebl3", "dec3", "qe3", "kd3"))
            dqe, dkd, dq_in, dk_in = hd["dqe"], hd["dkd"], hd["dq_in"], hd["dk_in"]
            dkd_kd = dkd * kd3
            db = dqe * qe3 - dkd_kd + q3 * dq_in - kk3 * dk_in
            dbl = jnp.sum(dkd_kd, axis=1, keepdims=True) + hd["ss"] * ebl3
            dg = _tri_dot(suffix, flat(db + jnp.where(last, dbl, 0.0)))
            df = dg / hd["f"] - flat(dkd * dec3 + dk_in)
            xq, sgq, hs = hd["xq"], hd["sgq"], hd["hs"]
            dxq_ref[:, hs] = (flat(dqe * eb3 + dq_in) * (sgq * (1.0 + xq * (1.0 - sgq)))).astype(MM)
            dxf_ref[:, hs] = (df * (1.0 - hd["lb"]) * hd["sg"] * hd["sn"]).astype(MM)
            dxi_ref[:, hs] = flat(hd["dv"]).astype(MM)
            dlb_ref[:, hs] += jnp.sum(df * hd["sn"], axis=0, keepdims=True)

    wide = HG_GROUP * HG_D
    rev = lambda i: nb - 1 - i
    sec = lambda j: pl.BlockSpec((None, rb, wide), lambda h, i, j=j: (j, rev(i), h))
    blk = pl.BlockSpec((rb, wide), lambda h, i: (rev(i), h))
    state = (pltpu.VMEM((HG_GROUP, nc, HG_D, HG_D), F32), pltpu.VMEM((HG_GROUP, nc, CH * CH, HG_D), MM))
    return pl.pallas_call(
        body, name="hgrn_bwd", grid=(HG_HEADS // HG_GROUP, nb),
        in_specs=[sec(0), sec(1), sec(2), pl.BlockSpec((2, wide), lambda h, i: (0, h)), blk,
                  pl.BlockSpec((None, HG_GROUP, HG_D, HG_D), lambda h, i: (rev(i), h, 0, 0))],
        out_specs=[blk, blk, blk, pl.BlockSpec((1, wide), lambda h, i: (0, h))],
        out_shape=[jax.ShapeDtypeStruct((s, SEC_W), MM)] * 3 + [jax.ShapeDtypeStruct((1, SEC_W), F32)],
        scratch_shapes=[pltpu.VMEM((HG_GROUP, HG_D, HG_D), F32), state[0], state[0], state[1], state[1]],
        compiler_params=_params("parallel", "arbitrary"),
    )(proj, proj, proj, lb_logits, d_o, sst)


def _rope_tables(s):
    half = AT_DH // 2
    inv_freq = 1.0 / (ROPE_THETA ** (jnp.arange(half, dtype=F32) / half))
    ang = jnp.arange(s, dtype=jnp.int32).astype(F32)[:, None] * inv_freq[None, :]
    cos, sin = jnp.cos(ang), jnp.sin(ang)
    return jnp.concatenate([cos] * 4, axis=-1), jnp.concatenate([-sin, sin] * 2, axis=-1)


def _rope128(x, cos, sin):
    lo = (_iota2(x.shape, 1) % AT_DH) < AT_DH // 2
    rot = jnp.where(lo, pltpu.roll(x, 128 - AT_DH // 2, 1), pltpu.roll(x, AT_DH // 2, 1))
    return x * cos + rot * sin


LANE_GROUPS = SEC_W // 128


def _set_lanes(ref, val):
    for j in range(LANE_GROUPS):
        ref[j] = val[:, j * 128:(j + 1) * 128]


def _get_lanes(ref):
    return jnp.concatenate([ref[j] for j in range(LANE_GROUPS)], axis=-1)


def _to_view(src_ref, dst_ref, d):
    n = src_ref.shape[1] // d
    for r in range(d):
        rows = pl.ds(r, n, stride=d) if d > 1 else slice(None)
        for j in range(LANE_GROUPS):
            c0 = r * SEC_W + j * 128
            dst_ref[:, c0:c0 + 128] = src_ref.at[j][rows, :].astype(dst_ref.dtype)


def _from_view(src_ref, dst_ref, d):
    n = dst_ref.shape[1] // d
    for r in range(d):
        for j in range(LANE_GROUPS):
            c0 = r * SEC_W + j * 128
            dst_ref.at[j][pl.ds(r, n, stride=d), :] = src_ref[:, c0:c0 + 128].astype(dst_ref.dtype)


def _view_spec(tm, d):
    return pl.BlockSpec((tm // d, d * SEC_W), lambda i: (i, 0))


def _view_shape(s, d, dtype):
    return jax.ShapeDtypeStruct((s // d, d * SEC_W), dtype)


PROJ_KEPT = (0, 1, 2, 3, 7)


def _inproj_fwd(x, norm_w, w_all, cos, sin, tm=256):
    s = x.shape[0]

    def body(x_ref, nw_ref, w_ref, cos_ref, sin_ref, proj_ref, *refs):
        outs, (qs_ref, ks_ref, vs_ref) = refs[:-3], refs[-3:]
        xv = x_ref[...]
        rstd = lax.rsqrt(jnp.mean(xv * xv, axis=-1, keepdims=True) + NORM_EPS)
        u = (xv * rstd * nw_ref[...]).astype(MM)
        for slot, j in enumerate(PROJ_KEPT):
            proj_ref[slot] = jnp.dot(u, w_ref[j], preferred_element_type=F32)
        q, k, v = [jnp.dot(u, w_ref[j], preferred_element_type=F32) for j in (4, 5, 6)]
        c, sn = cos_ref[...], sin_ref[...]
        for g in range(LANE_GROUPS):
            sl = slice(g * 128, (g + 1) * 128)
            qs_ref[g] = _rope128(q[:, sl], c, sn) * (AT_DH ** -0.5)
            ks_ref[g] = _rope128(k[:, sl], c, sn)
            vs_ref[g] = v[:, sl]
        for i, d in enumerate(DILATIONS):
            for src_ref, dst_ref in zip((qs_ref, ks_ref, vs_ref), outs[3 * i:3 * i + 3]):
                _to_view(src_ref, dst_ref, d)

    tab = pl.BlockSpec((tm, 128), lambda i: (i, 0))
    return pl.pallas_call(
        body, name="inproj_fwd", grid=(s // tm,),
        in_specs=[pl.BlockSpec((tm, D_MODEL), lambda i: (i, 0)),
                  pl.BlockSpec((1, D_MODEL), lambda i: (0, 0)),
                  pl.BlockSpec((N_SEC, D_MODEL, SEC_W), lambda i: (0, 0, 0)), tab, tab],
        out_specs=[pl.BlockSpec((len(PROJ_KEPT), tm, SEC_W), lambda i: (0, i, 0))]
                  + [_view_spec(tm, d) for d in DILATIONS for _ in range(3)],
        out_shape=[jax.ShapeDtypeStruct((len(PROJ_KEPT), s, SEC_W), F32)]
                  + [_view_shape(s, d, MM) for d in DILATIONS for _ in range(3)],
        scratch_shapes=[pltpu.VMEM((LANE_GROUPS, tm, 128), F32)] * 3,
        compiler_params=_params("parallel"),
    )(x, norm_w, w_all, cos, sin)


def _band_mask(first_ok, second_ok):
    row, col = _iota2((ATT_BLK, 2 * ATT_BLK), 0), _iota2((ATT_BLK, 2 * ATT_BLK), 1)
    return ((col < ATT_BLK) & (col >= row) & first_ok) | ((col >= ATT_BLK) & ((col - ATT_BLK) <= row) & second_ok)


def _own_lanes(rows, h):
    lane = _iota2((rows, 128), 1)
    return (lane < AT_DH) if h == 0 else (lane >= AT_DH)


def _neg_pieces(rows, h):
    lane = _iota2((rows, 128), 1) - (AT_DH if h == 0 else 0)
    return jnp.where((lane >= 0) & (lane < 3), -1.0, 0.0).astype(MM)


def _units():
    return [(b, slice(g * 128, (g + 1) * 128), h) for b in range(AT_QB) for g in range(AT_COLS // 128) for h in range(2)]


def _sub(b):
    return slice(b * ATT_BLK, (b + 1) * ATT_BLK)


def _band_before(cur_ref, prev_ref, b, sl):
    if b == 0:
        return jnp.concatenate([prev_ref[:, sl], cur_ref[0:ATT_BLK, sl]], axis=0)
    return cur_ref[(b - 1) * ATT_BLK:(b + 1) * ATT_BLK, sl]


def _band_after(cur_ref, next_ref, b, sl):
    if b == AT_QB - 1:
        return jnp.concatenate([cur_ref[b * ATT_BLK:(b + 1) * ATT_BLK, sl], next_ref[:, sl]], axis=0)
    return cur_ref[b * ATT_BLK:(b + 2) * ATT_BLK, sl]


def _attn_specs(rows):
    assert rows % (AT_QB * ATT_BLK) == 0
    last = rows // ATT_BLK - 1
    cur = pl.BlockSpec((AT_QB * ATT_BLK, AT_COLS), lambda c, n: (n, c))
    prev = pl.BlockSpec((ATT_BLK, AT_COLS), lambda c, n: (jnp.maximum(AT_QB * n - 1, 0), c))
    nxt = pl.BlockSpec((ATT_BLK, AT_COLS), lambda c, n: (jnp.minimum(AT_QB * (n + 1), last), c))
    return cur, prev, nxt


def _stack_heads(a):
    h0 = _own_lanes(a.shape[0], 0)
    zero = jnp.zeros_like(a)
    return jnp.concatenate([jnp.where(h0, a, zero), jnp.where(h0, zero, a)], axis=0)


def _unstack_heads(a2):
    return jnp.where(_own_lanes(ATT_BLK, 0), a2[:ATT_BLK], a2[ATT_BLK:])


def _attn_fwd(qr, kr, vr, d):
    rows, cols = qr.shape
    nb = rows // (AT_QB * ATT_BLK)

    def body(q_ref, kc_ref, kp_ref, vc_ref, vp_ref, o_ref, lse_ref):
        twice = lambda m: jnp.concatenate([m, m], axis=0)
        masks = {True: twice(_band_mask(pl.program_id(1) > 0, True)), False: twice(_band_mask(True, True))}
        ones = jnp.ones((2 * ATT_BLK, 128), MM)
        units = [(b, sl) for b, sl, h in _units() if h == 0]
        scs = [jnp.where(masks[b == 0], _dot_nt(_stack_heads(q_ref[_sub(b), sl]), _band_before(kc_ref, kp_ref, b, sl)),
                         NEG) for b, sl in units]
        ms = [jnp.max(sc, axis=-1, keepdims=True) for sc in scs]
        ps = [jnp.exp(sc - m).astype(MM) for sc, m in zip(scs, ms)]
        ols = [jnp.dot(p, jnp.concatenate([_band_before(vc_ref, vp_ref, b, sl), ones], axis=1),
                       preferred_element_type=F32) for p, (b, sl) in zip(ps, units)]
        for (b, sl), m, ol in zip(units, ms, ols):
            l = _unstack_heads(ol[:, 128:])
            o_ref[_sub(b), sl] = _unstack_heads(ol[:, :128]) / l
            lse_ref[_sub(b), sl] = _unstack_heads(jnp.broadcast_to(m, (2 * ATT_BLK, 128))) + jnp.log(l)

    cur, prev, _ = _attn_specs(rows)
    o, lse = pl.pallas_call(
        body, name=f"attn_fwd_d{d}", grid=(cols // AT_COLS, nb),
        in_specs=[cur, cur, prev, cur, prev], out_specs=[cur, cur],
        out_shape=[jax.ShapeDtypeStruct((rows, cols), F32)] * 2,
        compiler_params=_params("parallel", "parallel"),
    )(qr, kr, kr, vr, vr)
    return o, lse


def _attn_bwd_dq(qr, kr, vr, do, lse, delta, d):
    rows, cols = qr.shape
    nb = rows // (AT_QB * ATT_BLK)

    def body(q_ref, kc_ref, kp_ref, vc_ref, vp_ref, do_ref, lse_ref, dl_ref, dq_ref):
        masks = {True: _band_mask(pl.program_id(1) > 0, True), False: _band_mask(True, True)}
        units = _units()
        sms, dps = [], []
        for b, sl, h in units:
            own, own_b, neg = _own_lanes(ATT_BLK, h), _own_lanes(2 * ATT_BLK, h), _neg_pieces(2 * ATT_BLK, h)
            sms.append(_dot_nt(jnp.where(own, q_ref[_sub(b), sl], lse_ref[_sub(b), sl]),
                               jnp.where(own_b, _band_before(kc_ref, kp_ref, b, sl), neg)))
            dps.append(_dot_nt(jnp.where(own, do_ref[_sub(b), sl], dl_ref[_sub(b), sl]),
                               jnp.where(own_b, _band_before(vc_ref, vp_ref, b, sl), neg)))
        dss = [(jnp.exp(jnp.where(masks[b == 0], sm, NEG)) * dp).astype(MM)
               for sm, dp, (b, _, _) in zip(sms, dps, units)]
        dqs = [jnp.dot(ds, _band_before(kc_ref, kp_ref, b, sl), preferred_element_type=F32) * (AT_DH ** -0.5)
               for ds, (b, sl, _) in zip(dss, units)]
        for i in range(0, len(units), 2):
            b, sl, _ = units[i]
            dq_ref[_sub(b), sl] = jnp.where(_own_lanes(ATT_BLK, 0), dqs[i], dqs[i + 1]).astype(dq_ref.dtype)

    cur, prev, _ = _attn_specs(rows)
    dq = pl.pallas_call(
        body, name=f"attn_bwd_dq_d{d}", grid=(cols // AT_COLS, nb),
        in_specs=[cur, cur, prev, cur, prev, cur, cur, cur], out_specs=cur,
        out_shape=jax.ShapeDtypeStruct((rows, cols), MM),
        compiler_params=_params("parallel", "parallel"),
    )(qr, kr, kr, vr, vr, do, lse, delta)
    return dq


def _attn_bwd_dkv(qr, kr, vr, do, lse, delta, d):
    rows, cols = qr.shape
    nb = rows // (AT_QB * ATT_BLK)

    def body(k_ref, v_ref, qc_ref, qn_ref, doc_ref, don_ref, lsec_ref, lsen_ref, dlc_ref, dln_ref,
             dk_ref, dv_ref):
        masks = {True: _band_mask(True, pl.program_id(1) < nb - 1), False: _band_mask(True, True)}
        units = _units()
        sms, dps = [], []
        for b, sl, h in units:
            own, own_b, neg = _own_lanes(ATT_BLK, h), _own_lanes(2 * ATT_BLK, h), _neg_pieces(ATT_BLK, h)
            sms.append(_dot_nt(jnp.where(own, k_ref[_sub(b), sl], neg),
                               jnp.where(own_b, _band_after(qc_ref, qn_ref, b, sl),
                                         _band_after(lsec_ref, lsen_ref, b, sl))))
            dps.append(_dot_nt(jnp.where(own, v_ref[_sub(b), sl], neg),
                               jnp.where(own_b, _band_after(doc_ref, don_ref, b, sl),
                                         _band_after(dlc_ref, dln_ref, b, sl))))
        ps = [jnp.exp(jnp.where(masks[b == AT_QB - 1], sm, NEG)) for sm, (b, _, _) in zip(sms, units)]
        dss = [(p * dp).astype(MM) for p, dp in zip(ps, dps)]
        dvs = [jnp.dot(p.astype(MM), _band_after(doc_ref, don_ref, b, sl), preferred_element_type=F32)
               for p, (b, sl, _) in zip(ps, units)]
        dks = [jnp.dot(ds, _band_after(qc_ref, qn_ref, b, sl), preferred_element_type=F32)
               for ds, (b, sl, _) in zip(dss, units)]
        head0 = _own_lanes(ATT_BLK, 0)
        for i in range(0, len(units), 2):
            b, sl, _ = units[i]
            dk_ref[_sub(b), sl] = jnp.where(head0, dks[i], dks[i + 1]).astype(dk_ref.dtype)
            dv_ref[_sub(b), sl] = jnp.where(head0, dvs[i], dvs[i + 1]).astype(dv_ref.dtype)

    cur, _, nxt = _attn_specs(rows)
    dk, dv = pl.pallas_call(
        body, name=f"attn_bwd_dkv_d{d}", grid=(cols // AT_COLS, nb),
        in_specs=[cur, cur, cur, nxt, cur, nxt, cur, nxt, cur, nxt], out_specs=[cur, cur],
        out_shape=[jax.ShapeDtypeStruct((rows, cols), MM)] * 2,
        compiler_params=_params("parallel", "parallel"),
    )(kr, vr, qr, qr, do, do, lse, lse, delta, delta)
    return dk, dv


def _head_sum(a, width):
    parts = []
    for j in range(a.shape[1] // width):
        sm = jnp.sum(a[:, j * width:(j + 1) * width], axis=-1, keepdims=True)
        parts.append(jnp.broadcast_to(sm, (a.shape[0], width)))
    return jnp.concatenate(parts, axis=-1)


def _partner_pieces(x):
    xs = jnp.concatenate([pltpu.roll(x[:, j * 128:(j + 1) * 128], AT_DH, 1) for j in range(x.shape[1] // 128)],
                         axis=-1)
    hi = xs.astype(jnp.bfloat16).astype(F32)
    mid = (xs - hi).astype(jnp.bfloat16).astype(F32)
    lo = (xs - hi - mid).astype(jnp.bfloat16).astype(F32)
    lane = _iota2(x.shape, 1) % AT_DH
    return jnp.where(lane == 0, hi, jnp.where(lane == 1, mid, jnp.where(lane == 2, lo, 0.0)))


def _mid(x, tgt, proj, o_hg, o_at, lse_at, hg_norm_w, final_norm_w, wo_all, tm=256):
    s = x.shape[0]
    nb = s // tm

    def body(x_ref, t_ref, hgz_ref, atz_ref, ohg_ref, o1_ref, o2_ref, o3_ref, l1_ref, l2_ref, l3_ref,
             g_ref, fw_ref, wo_ref,
             dh_ref, dohg_ref, dhgz_ref, datz_ref, do1_ref, do2_ref, do3_ref, dl1_ref, dl2_ref, dl3_ref,
             lp1_ref, lp2_ref, lp3_ref,
             gwo_ref, gfw_ref, ghg_ref, loss_ref, nat_ref, stage_ref, gwo_acc):
        @pl.when(pl.program_id(0) == 0)
        def _():
            gwo_acc[...] = jnp.zeros_like(gwo_acc)
            gfw_ref[...] = jnp.zeros_like(gfw_ref)
            ghg_ref[...] = jnp.zeros_like(ghg_ref)
            loss_ref[...] = jnp.zeros_like(loss_ref)

        ohg, g = ohg_ref[...], g_ref[...]
        rs = lax.rsqrt(_head_sum(ohg * ohg, HG_D) * (1.0 / HG_D) + NORM_EPS)
        on = ohg * rs
        hgz = hgz_ref[...]
        sz = _sigmoid(hgz)
        gate_hg = hgz * sz
        lses, outs = [l1_ref[...]], [o1_ref[...]]
        for k, (d, l_ref, o_ref) in enumerate(zip(DILATIONS[1:], (l2_ref, l3_ref), (o2_ref, o3_ref))):
            _from_view(l_ref, nat_ref.at[2 * k], d)
            _from_view(o_ref, nat_ref.at[2 * k + 1], d)
            lses.append(_get_lanes(nat_ref.at[2 * k]))
            outs.append(_get_lanes(nat_ref.at[2 * k + 1]))
        mx = jnp.maximum(jnp.maximum(lses[0], lses[1]), lses[2])
        es = [jnp.exp(l - mx) for l in lses]
        den = es[0] + es[1] + es[2]
        ws = [e / den for e in es]
        oat = ws[0] * outs[0] + ws[1] * outs[1] + ws[2] * outs[2]
        atz = atz_ref[...]
        sa = _sigmoid(atz)
        gate_at = atz * sa
        mixed = jnp.concatenate([on * g * gate_hg, oat * gate_at], axis=-1).astype(MM)
        h = x_ref[...] + jnp.dot(mixed, wo_ref[...], preferred_element_type=F32)
        rstd = lax.rsqrt(jnp.mean(h * h, axis=-1, keepdims=True) + NORM_EPS)
        hn = h * rstd
        fw = fw_ref[...]
        err = hn * fw - t_ref[...]
        loss_ref[...] += 0.5 * jnp.sum(jnp.mean(err * err, axis=-1, keepdims=True), axis=0, keepdims=True)
        dout = err * (1.0 / D_MODEL)
        gfw_ref[...] += jnp.sum(dout * hn, axis=0, keepdims=True)
        dhn = dout * fw
        dh = rstd * (dhn - hn * jnp.mean(dhn * hn, axis=-1, keepdims=True))
        dh_ref[...] = dh
        dh_mm = dh.astype(MM)
        gwo_acc[...] += _dot_tn(mixed, dh_mm)

        @pl.when(pl.program_id(0) == nb - 1)
        def _():
            gwo_ref[...] = gwo_acc[...].astype(gwo_ref.dtype)

        dmixed = _dot_nt(dh_mm, wo_ref[...])
        dm_hg = dmixed[:, :SEC_W]
        d_ong = dm_hg * gate_hg
        dhgz_ref[...] = (dm_hg * (on * g) * (sz * (1.0 + hgz * (1.0 - sz)))).astype(MM)
        ghg_ref[...] += jnp.sum(d_ong * on, axis=0, keepdims=True)
        d_on = d_ong * g
        dohg_ref[...] = rs * (d_on - on * (_head_sum(d_on * on, HG_D) * (1.0 / HG_D)))
        dm_at = dmixed[:, SEC_W:]
        d_oat = dm_at * gate_at
        datz_ref[...] = (dm_at * oat * (sa * (1.0 + atz * (1.0 - sa)))).astype(MM)
        drow = _head_sum(d_oat * oat, AT_DH)
        lse_all = mx + jnp.log(den)
        for val, dst_refs in ((d_oat, (do1_ref, do2_ref, do3_ref)),
                              (_partner_pieces(drow), (dl1_ref, dl2_ref, dl3_ref)),
                              (_partner_pieces(lse_all), (lp1_ref, lp2_ref, lp3_ref))):
            _set_lanes(stage_ref, val)
            for d, dst_ref in zip(DILATIONS, dst_refs):
                _to_view(stage_ref, dst_ref, d)

    row = lambda w: pl.BlockSpec((tm, w), lambda i: (i, 0))
    sec = lambda j: pl.BlockSpec((None, tm, SEC_W), lambda i, j=j: (j, i, 0))
    const = lambda shp: pl.BlockSpec(shp, lambda i: (0,) * len(shp))
    half = row(SEC_W)
    views = [_view_spec(tm, d) for d in DILATIONS]
    return pl.pallas_call(
        body, name="mid", grid=(nb,),
        in_specs=[row(D_MODEL), row(D_MODEL), sec(PROJ_KEPT.index(3)), sec(PROJ_KEPT.index(7)), half] + views * 2
                 + [const((1, SEC_W)), const((1, D_MODEL)), const((D_MODEL, D_MODEL))],
        out_specs=[row(D_MODEL)] + [half] * 3 + views * 3
                  + [const((D_MODEL, D_MODEL)), const((1, D_MODEL)), const((1, SEC_W)), const((1, 1))],
        out_shape=[jax.ShapeDtypeStruct((s, D_MODEL), F32), jax.ShapeDtypeStruct((s, SEC_W), F32)]
                  + [jax.ShapeDtypeStruct((s, SEC_W), MM)] * 2
                  + [_view_shape(s, d, MM) for d in DILATIONS] * 3
                  + [jax.ShapeDtypeStruct((D_MODEL, D_MODEL), XCH), jax.ShapeDtypeStruct((1, D_MODEL), F32),
                     jax.ShapeDtypeStruct((1, SEC_W), F32), jax.ShapeDtypeStruct((1, 1), F32)],
        scratch_shapes=[pltpu.VMEM((4, LANE_GROUPS, tm, 128), F32), pltpu.VMEM((LANE_GROUPS, tm, 128), F32),
                        pltpu.VMEM((D_MODEL, D_MODEL), F32)],
        compiler_params=_params("arbitrary"),
    )(x, tgt, proj, proj, o_hg, *o_at, *lse_at, hg_norm_w, final_norm_w, wo_all)


def _section_specs(dsecs, tm):
    return [pl.BlockSpec((tm, SEC_W), lambda i: (i, 0)) if k is None
            else pl.BlockSpec((None, tm, SEC_W), lambda i, k=k: (k, i, 0)) for _, k in dsecs]


def _inproj_bwd_x(x, norm_w, w_all, dh, dsecs, token, tm=256):
    s = x.shape[0]

    def body(x_ref, nw_ref, w_ref, dh_ref, tok_ref, *refs):
        sec_refs, (gx_ref, gnw_ref) = refs[:N_SEC], refs[N_SEC:]

        @pl.when(pl.program_id(0) == 0)
        def _():
            gnw_ref[...] = jnp.zeros_like(gnw_ref)

        du = jnp.zeros((tm, D_MODEL), F32)
        for j in range(N_SEC):
            du = du + _dot_nt(sec_refs[j][...], w_ref[j])
        xv, nw = x_ref[...], nw_ref[...]
        rstd = lax.rsqrt(jnp.mean(xv * xv, axis=-1, keepdims=True) + NORM_EPS)
        xn = xv * rstd
        gnw_ref[...] += jnp.sum(du * xn, axis=0, keepdims=True)
        dxn = du * nw
        dx = rstd * (dxn - xn * jnp.mean(dxn * xn, axis=-1, keepdims=True))
        gx_ref[...] = (dh_ref[...] + tok_ref[0:1, 0:1]) + dx

    row = lambda w: pl.BlockSpec((tm, w), lambda i: (i, 0))
    const = lambda shp: pl.BlockSpec(shp, lambda i: (0,) * len(shp))
    return pl.pallas_call(
        body, name="inproj_bwd_x", grid=(s // tm,),
        in_specs=[row(D_MODEL), const((1, D_MODEL)), const((N_SEC, D_MODEL, SEC_W)), row(D_MODEL), const((8, 128))]
                 + _section_specs(dsecs, tm),
        out_specs=[row(D_MODEL), const((1, D_MODEL))],
        out_shape=[jax.ShapeDtypeStruct((s, D_MODEL), F32), jax.ShapeDtypeStruct((1, D_MODEL), F32)],
        compiler_params=_params("arbitrary"),
    )(x, norm_w, w_all, dh, token, *[a for a, _ in dsecs])


def _inproj_bwd_w(x, norm_w, dsec, dq_r, dk_r, dv, cos, sin, tm=512):
    s = x.shape[0]
    nb = s // tm

    def body(x_ref, nw_ref, s0, s1, s2, s3, s7, q1, q2, q3, k1, k2, k3, v1, v2, v3, cos_ref, sin_ref,
             gw_hbm, datt_ref, acc_ref, stage_ref, nat_ref):
        @pl.when(pl.program_id(0) == 0)
        def _():
            acc_ref[...] = jnp.zeros_like(acc_ref)

        def total(refs):
            acc = refs[0][...].astype(F32)
            for d, ref in zip(DILATIONS[1:], refs[1:]):
                _from_view(ref, nat_ref, d)
                acc = acc + _get_lanes(nat_ref)
            return acc

        c, sn = cos_ref[...], -sin_ref[...]
        unrot = lambda a: jnp.concatenate(
            [_rope128(a[:, j * 128:(j + 1) * 128], c, sn) for j in range(LANE_GROUPS)], axis=-1)
        att = [a.astype(MM) for a in (unrot(total((q1, q2, q3))), unrot(total((k1, k2, k3))), total((v1, v2, v3)))]
        for j, a in enumerate(att):
            datt_ref[j] = a
        xv = x_ref[...]
        rstd = lax.rsqrt(jnp.mean(xv * xv, axis=-1, keepdims=True) + NORM_EPS)
        u_t = (xv * rstd * nw_ref[...]).T.astype(MM)
        for j, dsj in enumerate((s0[...], s1[...], s2[...], s3[...], *att, s7[...])):
            acc_ref[j] += jnp.dot(u_t, dsj, preferred_element_type=F32)

        @pl.when(pl.program_id(0) == nb - 1)
        def _():
            for j in range(N_SEC):
                stage_ref[...] = acc_ref[j].astype(stage_ref.dtype)
                pltpu.sync_copy(stage_ref, gw_hbm.at[j])

    row = lambda w: pl.BlockSpec((tm, w), lambda i: (i, 0))
    return pl.pallas_call(
        body, name="inproj_bwd_w", grid=(nb,),
        in_specs=[row(D_MODEL), pl.BlockSpec((1, D_MODEL), lambda i: (0, 0))] + [row(SEC_W)] * 5
                 + [_view_spec(tm, d) for d in DILATIONS] * 3 + [row(128), row(128)],
        out_specs=[pl.BlockSpec(memory_space=pl.ANY), pl.BlockSpec((3, tm, SEC_W), lambda i: (0, i, 0))],
        out_shape=[jax.ShapeDtypeStruct((N_SEC, D_MODEL, SEC_W), XCH), jax.ShapeDtypeStruct((3, s, SEC_W), MM)],
        scratch_shapes=[pltpu.VMEM((N_SEC, D_MODEL, SEC_W), F32), pltpu.VMEM((D_MODEL, SEC_W), XCH),
                        pltpu.VMEM((LANE_GROUPS, tm, 128), F32)],
        compiler_params=_params("arbitrary"),
    )(x, norm_w, *dsec, *dq_r, *dk_r, *dv, cos, sin)


def _local_step(x, tgt, norm_w, w_all, lb_logits, hg_norm_w, wo_all, final_norm_w, on_weight_grads):
    s = x.shape[0]
    cos, sin = _rope_tables(s)
    proj, *qkv = _inproj_fwd(x, norm_w, w_all, cos, sin)
    o_hg, sst = _hgrn_fwd(proj, lb_logits)
    qkv = [qkv[3 * i:3 * i + 3] for i in range(len(DILATIONS))]
    att = [_attn_fwd(*qkv_d, d) for qkv_d, d in zip(qkv, DILATIONS)]
    (dh, d_ohg, d_hgz, d_atz, do1, do2, do3, dl1, dl2, dl3, lp1, lp2, lp3, gwo, gfw, ghg, loss) = _mid(
        x, tgt, proj, o_hg, [a[0] for a in att], [a[1] for a in att], hg_norm_w, final_norm_w[None, :], wo_all)
    dxq, dxf, dxi, dlb = _hgrn_bwd(proj, lb_logits, d_ohg, sst)
    dq_r, dk_r, dv = [], [], []
    for d, qkv_d, do, lp, dl in zip(DILATIONS, qkv, (do1, do2, do3), (lp1, lp2, lp3), (dl1, dl2, dl3)):
        dq_r.append(_attn_bwd_dq(*qkv_d, do, lp, dl, d))
        dk_d, dv_d = _attn_bwd_dkv(*qkv_d, do, lp, dl, d)
        dk_r.append(dk_d)
        dv.append(dv_d)
    gwi, d_att = _inproj_bwd_w(x, norm_w, (dxq, dxf, dxi, d_hgz, d_atz), dq_r, dk_r, dv, cos, sin)
    dsecs = [(dxq, None), (dxf, None), (dxi, None), (d_hgz, None), (d_att, 0), (d_att, 1), (d_att, 2), (d_atz, None)]
    token = on_weight_grads(gwi, gwo)
    gx, gnw = _inproj_bwd_x(x, norm_w, w_all, dh, dsecs, token)
    small = jnp.concatenate([gnw, jnp.concatenate([dlb, ghg], axis=-1), gfw,
                             jnp.pad(loss, ((0, 0), (0, D_MODEL - 1)))], axis=0)
    return gx, gwi, gwo, small


def _coords():
    return lax.axis_index("x"), lax.axis_index("y"), lax.axis_index("c")


def _gather_weights(w_in, w_out):
    wo_rows = w_out.shape[0]

    def body(wi_ref, wo_ref, wi_all, wo_all, send_sems, recv_sems):
        x, y, c = _coords()
        me, sibling = (x, y, c), (x, y, 1 - c)
        chips = [(1 - x, y), (x, 1 - y), (1 - x, 1 - y)]
        slot = lambda p: 4 * p[0] + 2 * p[1] + p[2]

        def copies(k, block, to):
            return [pltpu.make_async_remote_copy(
                src_ref=ref.at[slot(block)], dst_ref=ref.at[slot(block)], send_sem=send_sems.at[a, k],
                recv_sem=recv_sems.at[a, k], device_id=to, device_id_type=MESH)
                for a, ref in enumerate((wi_all, wo_all))]

        wi_all[slot(me)] = wi_ref[...].astype(MM)
        wo_all[slot(me)] = wo_ref[...].astype(MM)
        first = copies(0, me, sibling)
        for j, chip in enumerate(chips):
            first += copies(1 + j, me, (*chip, c))
        for cp in first:
            cp.start()
        passed = []
        for j, chip in enumerate(chips):
            for cp in copies(1 + j, (*chip, c), me):
                cp.wait_recv()
            fwd = copies(4 + j, (*chip, c), sibling)
            for cp in fwd:
                cp.start()
            passed += fwd
        for cp in copies(0, sibling, me):
            cp.wait_recv()
        for j, chip in enumerate(chips):
            for cp in copies(4 + j, (*chip, 1 - c), me):
                cp.wait_recv()
        for cp in first + passed:
            cp.wait_send()

    vmem = pl.BlockSpec(memory_space=pltpu.VMEM)
    return pl.pallas_call(
        body, name="gather_weights",
        in_specs=[vmem, vmem], out_specs=[vmem, vmem],
        out_shape=[jax.ShapeDtypeStruct((N_DEV, D_MODEL, SEC_W), MM),
                   jax.ShapeDtypeStruct((N_DEV, wo_rows, D_MODEL), MM)],
        scratch_shapes=[pltpu.SemaphoreType.DMA((2, 7)), pltpu.SemaphoreType.DMA((2, 7))],
        compiler_params=pltpu.CompilerParams(vmem_limit_bytes=VMEM_LIMIT),
    )(w_in, w_out)


def _me():
    x, y, c = _coords()
    return 4 * x + 2 * y + c


def _grad_copies(srcs, lands, send_sems, recv_sems):
    x, y, c = _coords()
    me = 4 * x + 2 * y + c
    copies = []
    for k in range(1, N_DEV):
        px, py, pc = x ^ (k >> 2), y ^ ((k >> 1) & 1), c ^ (k & 1)
        peer = 4 * px + 2 * py + pc
        for a, (src, dst) in enumerate(zip(srcs, lands)):
            copies.append(pltpu.make_async_remote_copy(
                src_ref=src.at[peer], dst_ref=dst.at[me], send_sem=send_sems.at[a * (N_DEV - 1) + k - 1],
                recv_sem=recv_sems.at[a * (N_DEV - 1) + k - 1], device_id=(px, py, pc), device_id_type=MESH))
    return copies


HBM_SPEC = pl.BlockSpec(memory_space=pltpu.HBM)
SEM_SPEC = pl.BlockSpec(memory_space=pltpu.SEMAPHORE)
SPLIT_COPY_EFFECT = pltpu.SideEffectType.DATAFLOW_SIDE_EFFECTING


def _exchange_start(gwi, gwo):
    def body(gwi_ref, gwo_ref, li_ref, lo_ref, send_sems, recv_sems, gwi_thru, gwo_thru, li_thru, lo_thru, token):
        for cp in _grad_copies((gwi_ref, gwo_ref), (li_ref, lo_ref), send_sems, recv_sems):
            cp.start()
        token[...] = jnp.zeros_like(token)

    hbm = lambda a: pltpu.with_memory_space_constraint(a, pltpu.HBM)
    bufs = (gwi, gwo, lax.empty(gwi.shape, gwi.dtype), lax.empty(gwo.shape, gwo.dtype))
    return pl.pallas_call(
        body, name="exchange_start",
        out_shape=(pltpu.SemaphoreType.DMA((2 * (N_DEV - 1),)), pltpu.SemaphoreType.DMA((2 * (N_DEV - 1),)),
                   *[pltpu.HBM(a.shape, a.dtype) for a in bufs], jax.ShapeDtypeStruct((8, 128), F32)),
        in_specs=[HBM_SPEC] * 4,
        out_specs=(SEM_SPEC, SEM_SPEC, HBM_SPEC, HBM_SPEC, HBM_SPEC, HBM_SPEC, pl.BlockSpec(memory_space=pltpu.VMEM)),
        input_output_aliases={0: 2, 1: 3, 2: 4, 3: 5},
        compiler_params=pltpu.CompilerParams(has_side_effects=SPLIT_COPY_EFFECT),
    )(*[hbm(a) for a in bufs])


def _exchange_wait(send_sems, recv_sems, gwi, gwo, li, lo, after):
    def body(gwi_ref, gwo_ref, li_ref, lo_ref, send_sems, recv_sems, after_ref, gwi_out, gwo_out, li_out, lo_out):
        for cp in _grad_copies((gwi_ref, gwo_ref), (li_ref, lo_ref), send_sems, recv_sems):
            cp.wait_send()
            cp.wait_recv()

    return pl.pallas_call(
        body, name="exchange_wait",
        out_shape=tuple(pltpu.HBM(a.shape, a.dtype) for a in (gwi, gwo, li, lo)),
        in_specs=[HBM_SPEC] * 4 + [SEM_SPEC, SEM_SPEC, pl.BlockSpec(memory_space=pl.ANY)],
        out_specs=(HBM_SPEC,) * 4,
        input_output_aliases={0: 0, 1: 1, 2: 2, 3: 3},
        compiler_params=pltpu.CompilerParams(has_side_effects=SPLIT_COPY_EFFECT),
    )(gwi, gwo, li, lo, send_sems, recv_sems, after)


def _gather_small(small):
    def body(sm_ref, ls_ref, send_sems, recv_sems, local_sem):
        x, y, c = _coords()
        me = 4 * x + 2 * y + c
        own = pltpu.make_async_copy(sm_ref, ls_ref.at[me], local_sem)
        own.start()
        sends = []
        for k in range(1, N_DEV):
            peer = (x ^ (k >> 2), y ^ ((k >> 1) & 1), c ^ (k & 1))
            sends.append(pltpu.make_async_remote_copy(
                src_ref=sm_ref, dst_ref=ls_ref.at[me], send_sem=send_sems.at[k - 1], recv_sem=recv_sems.at[k - 1],
                device_id=peer, device_id_type=MESH))
        for cp in sends:
            cp.start()
        for cp in sends:
            cp.wait_recv()
        for cp in sends:
            cp.wait_send()
        own.wait()

    vmem = pl.BlockSpec(memory_space=pltpu.VMEM)
    return pl.pallas_call(
        body, name="gather_small", in_specs=[vmem], out_specs=vmem,
        out_shape=jax.ShapeDtypeStruct((N_DEV,) + small.shape, F32),
        scratch_shapes=[pltpu.SemaphoreType.DMA((N_DEV - 1,)), pltpu.SemaphoreType.DMA((N_DEV - 1,)),
                        pltpu.SemaphoreType.DMA],
    )(small)


def _adamw(w, g, m, v):
    m = ADAM_B1 * m + (1.0 - ADAM_B1) * g
    v = ADAM_B2 * v + (1.0 - ADAM_B2) * (g * g)
    m_hat = m / (1.0 - ADAM_B1 ** ADAM_STEP)
    v_hat = v / (1.0 - ADAM_B2 ** ADAM_STEP)
    return -ADAM_LR * (m_hat / (jnp.sqrt(v_hat) + ADAM_EPS) + ADAM_WD * w), m, v


def _slot_sum(ref, own=None, me=None):
    g = None
    for i in range(N_DEV):
        term = ref[i].astype(F32)
        if own is not None:
            term = jnp.where(i == me, own, term)
        g = term if g is None else g + term
    return g


def _update_matrix(name, me, landed, own, w, m, v, rows):
    r, c = w.shape

    def body(me_ref, l_ref, own_ref, w_ref, m_ref, v_ref, g_ref, d_ref, nm_ref, nv_ref):
        g = _slot_sum(l_ref, own_ref[...].astype(F32), me_ref[0])
        g_ref[...] = g
        d_ref[...], nm_ref[...], nv_ref[...] = _adamw(w_ref[...], g, m_ref[...], v_ref[...])

    blk = pl.BlockSpec((rows, c), lambda i, me_ref: (i, 0))
    return pl.pallas_call(
        body, name=name,
        grid_spec=pltpu.PrefetchScalarGridSpec(
            num_scalar_prefetch=1, grid=(r // rows,),
            in_specs=[pl.BlockSpec((N_DEV, rows, c), lambda i, me_ref: (0, i, 0)),
                      pl.BlockSpec((None, rows, c), lambda i, me_ref: (me_ref[0], i, 0)), blk, blk, blk],
            out_specs=[blk] * 4),
        out_shape=[jax.ShapeDtypeStruct((r, c), F32)] * 4,
        compiler_params=_params("parallel"),
    )(me, landed, own, w, m, v)


def _update_small(landed, lb_logits, ws, ms, vs):
    def body(l_ref, lbl_ref, w_ref, m_ref, v_ref, g_ref, d_ref, nm_ref, nv_ref, loss_ref):
        tot = _slot_sum(l_ref)
        _, dlb = _lower_bound(lbl_ref[...])
        g_lb = tot[1:2, :SEC_W] * dlb
        g = jnp.concatenate([tot[0:1], jnp.concatenate([g_lb, -g_lb], axis=-1),
                             jnp.pad(tot[1:2, SEC_W:], ((0, 0), (0, SEC_W))), tot[2:3]], axis=0)
        g_ref[...] = g
        d_ref[...], nm_ref[...], nv_ref[...] = _adamw(w_ref[...], g, m_ref[...], v_ref[...])
        loss_ref[...] = tot[3:4, 0:1]

    vmem = pl.BlockSpec(memory_space=pltpu.VMEM)
    return pl.pallas_call(
        body, name="update_small", in_specs=[vmem] * 5, out_specs=[vmem] * 5,
        out_shape=[jax.ShapeDtypeStruct((4, D_MODEL), F32)] * 4 + [jax.ShapeDtypeStruct((1, 1), F32)],
    )(landed, lb_logits, ws, ms, vs)


def _pack_small(norm_w, lb_logits, hg_norm_w, final_norm_w):
    return jnp.concatenate([norm_w, lb_logits.reshape(1, D_MODEL),
                            jnp.pad(hg_norm_w, ((0, 0), (0, D_MODEL - SEC_W))), final_norm_w[None, :]], axis=0)


def _unpack_small(a):
    return a[0:1], a[1].reshape(2, SEC_W), a[2:3, :SEC_W], a[3]


def kernel(x, norm_w, w_in, hgrn_lb_logits, hg_norm_w, w_out, final_norm_w, loss_target, m_norm_w, m_w_in, m_hgrn_lb_logits, m_hg_norm_w, m_w_out, m_final_norm_w, v_norm_w, v_w_in, v_hgrn_lb_logits, v_hg_norm_w, v_w_out, v_final_norm_w):
    w_all, wo_all = _gather_weights(w_in[0], w_out[0])
    in_flight = []

    def start_exchange(gwi, gwo):
        *handles, token = _exchange_start(gwi, gwo.reshape(N_DEV, D_MODEL // N_DEV, D_MODEL))
        in_flight.extend(handles)
        return token

    gx, _, _, small = _local_step(x[0], loss_target[0], norm_w, w_all, hgrn_lb_logits, hg_norm_w,
                                  wo_all.reshape(D_MODEL, D_MODEL), final_norm_w, start_exchange)
    ls = _gather_small(small)
    gwi, gwo, li, lo = _exchange_wait(*in_flight, gx)
    me = _me().astype(jnp.int32).reshape(1)
    g_wi, d_wi, nm_wi, nv_wi = _update_matrix("update_w_in", me, li, gwi, w_in[0], m_w_in[0], v_w_in[0], 256)
    g_wo, d_wo, nm_wo, nv_wo = _update_matrix("update_w_out", me, lo, gwo, w_out[0], m_w_out[0], v_w_out[0], 128)
    g_s, d_s, nm_s, nv_s, loss = _update_small(
        ls, hgrn_lb_logits, _pack_small(norm_w, hgrn_lb_logits, hg_norm_w, final_norm_w),
        _pack_small(m_norm_w, m_hgrn_lb_logits, m_hg_norm_w, m_final_norm_w),
        _pack_small(v_norm_w, v_hgrn_lb_logits, v_hg_norm_w, v_final_norm_w))
    outs = []
    for small_out, wi, wo in ((g_s, g_wi, g_wo), (d_s, d_wi, d_wo), (nm_s, nm_wi, nm_wo), (nv_s, nv_wi, nv_wo)):
        nw, lb, hg, fw = _unpack_small(small_out)
        outs += [nw, wi[None], lb, hg, wo[None], fw]
    return (loss[0, 0], gx[None], *outs)
```

```python
import functools

import jax
import jax.numpy as jnp
from jax import lax
from jax.experimental import pallas as pl
from jax.experimental.pallas import tpu as pltpu

F32 = jnp.float32
MM = jnp.bfloat16
XCH = jnp.bfloat16
NORM_EPS = 1e-6
NEG = -1e30
N_DEV = 8
D_MODEL = 1024
N_SEC = 8
SEC_W = 512
HG_HEADS = 4
HG_D = 128
HG_GROUP = 4
AT_HEADS = 8
AT_DH = 64
ATT_BLK = 128
AT_COLS = 512
AT_QB = 4
DILATIONS = (1, 4, 16)
ROPE_THETA = 10000.0
CH = 16
LB_LO, LB_HI = 1e-6, 1.0 - 1e-6
ADAM_LR, ADAM_B1, ADAM_B2, ADAM_EPS, ADAM_WD, ADAM_STEP = 0.001, 0.9, 0.999, 1e-08, 0.01, 10
VMEM_LIMIT = 56 * 1024 * 1024
MESH = pl.DeviceIdType.MESH


def _params(*sem):
    return pltpu.CompilerParams(dimension_semantics=sem, vmem_limit_bytes=VMEM_LIMIT)


def _sigmoid(x):
    return 1.0 / (1.0 + jnp.exp(-x))


def _dot(a, b):
    return jnp.dot(a.astype(MM), b.astype(MM), preferred_element_type=F32)


def _dot_nt(a, b):
    return lax.dot_general(a.astype(MM), b.astype(MM), (((1,), (1,)), ((), ())), preferred_element_type=F32)


def _dot_tn(a, b):
    return lax.dot_general(a.astype(MM), b.astype(MM), (((0,), (0,)), ((), ())), preferred_element_type=F32)


def _tri_dot(tri, g):
    g1 = g.astype(jnp.bfloat16)
    r1 = g - g1.astype(F32)
    g2 = r1.astype(jnp.bfloat16)
    g3 = (r1 - g2.astype(F32)).astype(jnp.bfloat16)
    t = tri.astype(jnp.bfloat16)
    d = functools.partial(jnp.dot, preferred_element_type=F32)
    return d(t, g1) + d(t, g2) + d(t, g3)


def _lower_bound(lbl):
    l0, l1 = lbl[0:1, :], lbl[1:2, :]
    m = jnp.maximum(l0, l1)
    e0, e1 = jnp.exp(l0 - m), jnp.exp(l1 - m)
    p = e0 / (e0 + e1)
    inside = (p >= LB_LO) & (p <= LB_HI)
    return jnp.clip(p, LB_LO, LB_HI), jnp.where(inside, p * (e1 / (e0 + e1)), 0.0)


def _iota2(shape, dim):
    return lax.broadcasted_iota(jnp.int32, shape, dim)


def _hgrn_gates(xq, xf, lb):
    sgq = _sigmoid(xq)
    sg = _sigmoid(xf)
    sn = _sigmoid(-xf)
    f = lb + (1.0 - lb) * sg
    return sgq, xq * sgq, sg, sn, f, (1.0 - lb) * sn


def _bdot(a, b, ca, cb):
    return lax.dot_general(a.astype(MM), b.astype(MM), (((ca,), (cb,)), ((0,), (0,))), preferred_element_type=F32)


def _chunk_masks(rb):
    row, col = _iota2((rb, rb), 0), _iota2((rb, rb), 1)
    same = (row // CH) == (col // CH)
    return same & (row >= col), same & (row <= col)


def _hgrn_fwd(proj, lb_logits, rb=256):
    s = proj.shape[1]
    nb, nc = s // rb, rb // CH

    def body(q_ref, f_ref, i_ref, lbl_ref, o_ref, sst_ref, st_ref, slab_ref, states_ref):
        @pl.when(pl.program_id(1) == 0)
        def _():
            st_ref[...] = jnp.zeros_like(st_ref)

        sst_ref[...] = st_ref[...]
        prefix, _ = _chunk_masks(rb)
        c3 = lambda a: a.reshape(nc, CH, HG_D)
        row, col = _iota2((nc, CH, CH), 1), _iota2((nc, CH, CH), 2)
        heads = []
        for g in range(HG_GROUP):
            hs = slice(g * HG_D, (g + 1) * HG_D)
            lb, _ = _lower_bound(lbl_ref[:, hs])
            _, q, _, _, f, kk = _hgrn_gates(q_ref[:, hs], f_ref[:, hs], lb)
            b3 = c3(_tri_dot(prefix, jnp.log(f)))
            q3, kk3, v3 = c3(q), c3(kk), c3(i_ref[:, hs])
            bl3 = b3[:, CH - 1:CH, :]
            for t in range(CH):
                slab_ref[g, :, t * CH:(t + 1) * CH, :] = (
                    q3 * jnp.exp(jnp.minimum(b3 - b3[:, t:t + 1, :], 0.0))).astype(MM)
            x_upd = _bdot(v3, kk3 * jnp.exp(bl3 - b3), 1, 1)
            heads.append(dict(hs=hs, kk3=kk3, v3=v3, qe3=q3 * jnp.exp(b3), ebl3=jnp.exp(bl3), x_upd=x_upd))
        for g, hd in enumerate(heads):
            st = st_ref[g]
            for c in range(nc):
                states_ref[g, c] = st
                st = st * hd["ebl3"][c] + hd["x_upd"][c]
            st_ref[g] = st
        for g, hd in enumerate(heads):
            r = _bdot(slab_ref[g], hd["kk3"], 2, 2)
            a = jnp.zeros((nc, CH, CH), F32)
            for t in range(CH):
                a = a + jnp.where(col == t, r[:, t * CH:(t + 1) * CH, :], 0.0)
            a = jnp.where(row >= col, a, 0.0)
            o3 = _bdot(hd["qe3"], states_ref[g], 2, 2) + _bdot(a, hd["v3"], 2, 1)
            o_ref[:, hd["hs"]] = o3.reshape(rb, HG_D)

    wide = HG_GROUP * HG_D
    sec = lambda j: pl.BlockSpec((None, rb, wide), lambda h, i, j=j: (j, i, h))
    return pl.pallas_call(
        body, name="hgrn_fwd", grid=(HG_HEADS // HG_GROUP, nb),
        in_specs=[sec(0), sec(1), sec(2), pl.BlockSpec((2, wide), lambda h, i: (0, h))],
        out_specs=[pl.BlockSpec((rb, wide), lambda h, i: (i, h)),
                   pl.BlockSpec((None, HG_GROUP, HG_D, HG_D), lambda h, i: (i, h, 0, 0))],
        out_shape=[jax.ShapeDtypeStruct((s, SEC_W), F32),
                   jax.ShapeDtypeStruct((nb, HG_HEADS, HG_D, HG_D), F32)],
        scratch_shapes=[pltpu.VMEM((HG_GROUP, HG_D, HG_D), F32), pltpu.VMEM((HG_GROUP, nc, CH * CH, HG_D), MM),
                        pltpu.VMEM((HG_GROUP, nc, HG_D, HG_D), F32)],
        compiler_params=_params("parallel", "arbitrary"),
    )(proj, proj, proj, lb_logits)


def _hgrn_bwd(proj, lb_logits, d_o, sst, rb=256):
    s = proj.shape[1]
    nb, nc = s // rb, rb // CH

    def body(q_ref, f_ref, i_ref, lbl_ref, do_ref, sst_ref, dxq_ref, dxf_ref, dxi_ref, dlb_ref,
             dst_ref, states_ref, dstates_ref, lslab_ref, kslab_ref):
        @pl.when(pl.program_id(1) == 0)
        def _():
            dst_ref[...] = jnp.zeros_like(dst_ref)
            dlb_ref[...] = jnp.zeros_like(dlb_ref)

        prefix, suffix = _chunk_masks(rb)
        c3 = lambda a: a.reshape(nc, CH, HG_D)
        flat = lambda a: a.reshape(rb, HG_D)
        row, col = _iota2((nc, CH, CH), 1), _iota2((nc, CH, CH), 2)
        tril, triu = row >= col, row <= col
        sel = (_iota2((CH, CH * CH), 1) % CH == _iota2((CH, CH * CH), 0)).astype(MM)
        blockdiag = _iota2((nc, CH, CH * CH), 2) // CH == _iota2((nc, CH, CH * CH), 1)
        tile = lambda m: jnp.where(blockdiag, _dot(m.reshape(rb, CH), sel).reshape(nc, CH, CH * CH), 0.0)
        last = _iota2((nc, CH, HG_D), 1) == CH - 1
        heads = []
        for g in range(HG_GROUP):
            hs = slice(g * HG_D, (g + 1) * HG_D)
            lb, _ = _lower_bound(lbl_ref[:, hs])
            xq = q_ref[:, hs]
            sgq, q, sg, sn, f, kk = _hgrn_gates(xq, f_ref[:, hs], lb)
            b3 = c3(_tri_dot(prefix, jnp.log(f)))
            q3, kk3, v3, do3 = c3(q), c3(kk), c3(i_ref[:, hs]), c3(do_ref[:, hs])
            bl3 = b3[:, CH - 1:CH, :]
            eb3, ebl3, dec3 = jnp.exp(b3), jnp.exp(bl3), jnp.exp(bl3 - b3)
            qe3, kd3 = q3 * eb3, kk3 * dec3
            x_upd, y_upd = _bdot(v3, kd3, 1, 1), _bdot(do3, qe3, 1, 1)
            for t in range(CH):
                bt = b3[:, t:t + 1, :]
                lslab_ref[g, :, t * CH:(t + 1) * CH, :] = (q3 * jnp.exp(jnp.minimum(b3 - bt, 0.0))).astype(MM)
                kslab_ref[g, :, t * CH:(t + 1) * CH, :] = (kk3 * jnp.exp(jnp.minimum(bt - b3, 0.0))).astype(MM)
            d_a = jnp.where(tril, _bdot(do3, v3, 2, 2), 0.0)
            d_at = jnp.where(triu, _bdot(v3, do3, 2, 2), 0.0)
            heads.append(dict(hs=hs, lb=lb, xq=xq, sgq=sgq, sg=sg, sn=sn, f=f, q3=q3, kk3=kk3, v3=v3, do3=do3,
                              eb3=eb3, ebl3=ebl3, dec3=dec3, qe3=qe3, kd3=kd3, x_upd=x_upd, y_upd=y_upd,
                              d_a=d_a, d_at=d_at))
        for g, hd in enumerate(heads):
            st = sst_ref[g]
            for c in range(nc):
                states_ref[g, c] = st
                st = st * hd["ebl3"][c] + hd["x_upd"][c]
            dst = dst_ref[g]
            for c in reversed(range(nc)):
                dstates_ref[g, c] = dst
                dst = dst * hd["ebl3"][c] + hd["y_upd"][c]
            dst_ref[g] = dst
        for g, hd in enumerate(heads):
            q3, v3, do3, kd3 = hd["q3"], hd["v3"], hd["do3"], hd["kd3"]
            states, dstates = states_ref[g], dstates_ref[g]
            hd["dqe"] = _bdot(do3, states, 2, 1)
            hd["dkd"] = _bdot(v3, dstates, 2, 1)
            r = _bdot(kslab_ref[g], q3, 2, 2)
            a_t = jnp.zeros((nc, CH, CH), F32)
            for t in range(CH):
                a_t = a_t + jnp.where(col == t, r[:, t * CH:(t + 1) * CH, :], 0.0)
            a_t = jnp.where(triu, a_t, 0.0)
            hd["dv"] = _bdot(kd3, dstates, 2, 2) + _bdot(a_t, do3, 2, 1)
            hd["dq_in"] = _bdot(tile(hd["d_a"]), kslab_ref[g], 2, 1)
            hd["dk_in"] = _bdot(tile(hd["d_at"]), lslab_ref[g], 2, 1)
            hd["ss"] = jnp.sum(dstates * states, axis=1, keepdims=True)
        for g, hd in enumerate(heads):
            q3, kk3, eb3, ebl3, dec3, qe3, kd3 = (hd[k] for k in ("q3", "kk3", "eb3", "ebl3", "dec3", "qe3", "kd3"))
            dqe, dkd, dq_in, dk_in = hd["dqe"], hd["dkd"], hd["dq_in"], hd["dk_in"]
            dkd_kd = dkd * kd3
            db = dqe * qe3 - dkd_kd + q3 * dq_in - kk3 * dk_in
            dbl = jnp.sum(dkd_kd, axis=1, keepdims=True) + hd["ss"] * ebl3
            dg = _tri_dot(suffix, flat(db + jnp.where(last, dbl, 0.0)))
            df = dg / hd["f"] - flat(dkd * dec3 + dk_in)
            xq, sgq, hs = hd["xq"], hd["sgq"], hd["hs"]
            dxq_ref[:, hs] = (flat(dqe * eb3 + dq_in) * (sgq * (1.0 + xq * (1.0 - sgq)))).astype(MM)
            dxf_ref[:, hs] = (df * (1.0 - hd["lb"]) * hd["sg"] * hd["sn"]).astype(MM)
            dxi_ref[:, hs] = flat(hd["dv"]).astype(MM)
            dlb_ref[:, hs] += jnp.sum(df * hd["sn"], axis=0, keepdims=True)

    wide = HG_GROUP * HG_D
    rev = lambda i: nb - 1 - i
    sec = lambda j: pl.BlockSpec((None, rb, wide), lambda h, i, j=j: (j, rev(i), h))
    blk = pl.BlockSpec((rb, wide), lambda h, i: (rev(i), h))
    state = (pltpu.VMEM((HG_GROUP, nc, HG_D, HG_D), F32), pltpu.VMEM((HG_GROUP, nc, CH * CH, HG_D), MM))
    return pl.pallas_call(
        body, name="hgrn_bwd", grid=(HG_HEADS // HG_GROUP, nb),
        in_specs=[sec(0), sec(1), sec(2), pl.BlockSpec((2, wide), lambda h, i: (0, h)), blk,
                  pl.BlockSpec((None, HG_GROUP, HG_D, HG_D), lambda h, i: (rev(i), h, 0, 0))],
        out_specs=[blk, blk, blk, pl.BlockSpec((1, wide), lambda h, i: (0, h))],
        out_shape=[jax.ShapeDtypeStruct((s, SEC_W), MM)] * 3 + [jax.ShapeDtypeStruct((1, SEC_W), F32)],
        scratch_shapes=[pltpu.VMEM((HG_GROUP, HG_D, HG_D), F32), state[0], state[0], state[1], state[1]],
        compiler_params=_params("parallel", "arbitrary"),
    )(proj, proj, proj, lb_logits, d_o, sst)


def _rope_tables(s):
    half = AT_DH // 2
    inv_freq = 1.0 / (ROPE_THETA ** (jnp.arange(half, dtype=F32) / half))
    ang = jnp.arange(s, dtype=jnp.int32).astype(F32)[:, None] * inv_freq[None, :]
    cos, sin = jnp.cos(ang), jnp.sin(ang)
    return jnp.concatenate([cos] * 4, axis=-1), jnp.concatenate([-sin, sin] * 2, axis=-1)


def _rope128(x, cos, sin):
    lo = (_iota2(x.shape, 1) % AT_DH) < AT_DH // 2
    rot = jnp.where(lo, pltpu.roll(x, 128 - AT_DH // 2, 1), pltpu.roll(x, AT_DH // 2, 1))
    return x * cos + rot * sin


LANE_GROUPS = SEC_W // 128


def _set_lanes(ref, val):
    for j in range(LANE_GROUPS):
        ref[j] = val[:, j * 128:(j + 1) * 128]


def _get_lanes(ref):
    return jnp.concatenate([ref[j] for j in range(LANE_GROUPS)], axis=-1)


def _to_view(src_ref, dst_ref, d):
    n = src_ref.shape[1] // d
    for r in range(d):
        rows = pl.ds(r, n, stride=d) if d > 1 else slice(None)
        for j in range(LANE_GROUPS):
            c0 = r * SEC_W + j * 128
            dst_ref[:, c0:c0 + 128] = src_ref.at[j][rows, :].astype(dst_ref.dtype)


def _from_view(src_ref, dst_ref, d):
    n = dst_ref.shape[1] // d
    for r in range(d):
        for j in range(LANE_GROUPS):
            c0 = r * SEC_W + j * 128
            dst_ref.at[j][pl.ds(r, n, stride=d), :] = src_ref[:, c0:c0 + 128].astype(dst_ref.dtype)


def _view_spec(tm, d):
    return pl.BlockSpec((tm // d, d * SEC_W), lambda i: (i, 0))


def _view_shape(s, d, dtype):
    return jax.ShapeDtypeStruct((s // d, d * SEC_W), dtype)


PROJ_KEPT = (0, 1, 2, 3, 7)


def _inproj_fwd(x, norm_w, w_all, cos, sin, tm=512):
    s = x.shape[0]

    def body(x_ref, nw_ref, w_ref, cos_ref, sin_ref, proj_ref, *refs):
        outs, (qs_ref, ks_ref, vs_ref) = refs[:-3], refs[-3:]
        xv = x_ref[...]
        rstd = lax.rsqrt(jnp.mean(xv * xv, axis=-1, keepdims=True) + NORM_EPS)
        u = (xv * rstd * nw_ref[...]).astype(MM)
        for slot, j in enumerate(PROJ_KEPT):
            proj_ref[slot] = jnp.dot(u, w_ref[j], preferred_element_type=F32)
        q, k, v = [jnp.dot(u, w_ref[j], preferred_element_type=F32) for j in (4, 5, 6)]
        c, sn = cos_ref[...], sin_ref[...]
        for g in range(LANE_GROUPS):
            sl = slice(g * 128, (g + 1) * 128)
            qs_ref[g] = _rope128(q[:, sl], c, sn) * (AT_DH ** -0.5)
            ks_ref[g] = _rope128(k[:, sl], c, sn)
            vs_ref[g] = v[:, sl]
        for i, d in enumerate(DILATIONS):
            for src_ref, dst_ref in zip((qs_ref, ks_ref, vs_ref), outs[3 * i:3 * i + 3]):
                _to_view(src_ref, dst_ref, d)

    tab = pl.BlockSpec((tm, 128), lambda i: (i, 0))
    return pl.pallas_call(
        body, name="inproj_fwd", grid=(s // tm,),
        in_specs=[pl.BlockSpec((tm, D_MODEL), lambda i: (i, 0)),
                  pl.BlockSpec((1, D_MODEL), lambda i: (0, 0)),
                  pl.BlockSpec((N_SEC, D_MODEL, SEC_W), lambda i: (0, 0, 0)), tab, tab],
        out_specs=[pl.BlockSpec((len(PROJ_KEPT), tm, SEC_W), lambda i: (0, i, 0))]
                  + [_view_spec(tm, d) for d in DILATIONS for _ in range(3)],
        out_shape=[jax.ShapeDtypeStruct((len(PROJ_KEPT), s, SEC_W), F32)]
                  + [_view_shape(s, d, MM) for d in DILATIONS for _ in range(3)],
        scratch_shapes=[pltpu.VMEM((LANE_GROUPS, tm, 128), F32)] * 3,
        compiler_params=_params("parallel"),
    )(x, norm_w, w_all, cos, sin)


def _band_mask(first_ok, second_ok):
    row, col = _iota2((ATT_BLK, 2 * ATT_BLK), 0), _iota2((ATT_BLK, 2 * ATT_BLK), 1)
    return ((col < ATT_BLK) & (col >= row) & first_ok) | ((col >= ATT_BLK) & ((col - ATT_BLK) <= row) & second_ok)


def _own_lanes(rows, h):
    lane = _iota2((rows, 128), 1)
    return (lane < AT_DH) if h == 0 else (lane >= AT_DH)


def _neg_pieces(rows, h):
    lane = _iota2((rows, 128), 1) - (AT_DH if h == 0 else 0)
    return jnp.where((lane >= 0) & (lane < 3), -1.0, 0.0).astype(MM)


def _units():
    return [(b, slice(g * 128, (g + 1) * 128), h) for b in range(AT_QB) for g in range(AT_COLS // 128) for h in range(2)]


def _sub(b):
    return slice(b * ATT_BLK, (b + 1) * ATT_BLK)


def _band_before(cur_ref, prev_ref, b, sl):
    if b == 0:
        return jnp.concatenate([prev_ref[:, sl], cur_ref[0:ATT_BLK, sl]], axis=0)
    return cur_ref[(b - 1) * ATT_BLK:(b + 1) * ATT_BLK, sl]


def _band_after(cur_ref, next_ref, b, sl):
    if b == AT_QB - 1:
        return jnp.concatenate([cur_ref[b * ATT_BLK:(b + 1) * ATT_BLK, sl], next_ref[:, sl]], axis=0)
    return cur_ref[b * ATT_BLK:(b + 2) * ATT_BLK, sl]


def _attn_specs(rows):
    assert rows % (AT_QB * ATT_BLK) == 0
    last = rows // ATT_BLK - 1
    cur = pl.BlockSpec((AT_QB * ATT_BLK, AT_COLS), lambda c, n: (n, c))
    prev = pl.BlockSpec((ATT_BLK, AT_COLS), lambda c, n: (jnp.maximum(AT_QB * n - 1, 0), c))
    nxt = pl.BlockSpec((ATT_BLK, AT_COLS), lambda c, n: (jnp.minimum(AT_QB * (n + 1), last), c))
    return cur, prev, nxt


def _stack_heads(a):
    h0 = _own_lanes(a.shape[0], 0)
    zero = jnp.zeros_like(a)
    return jnp.concatenate([jnp.where(h0, a, zero), jnp.where(h0, zero, a)], axis=0)


def _unstack_heads(a2):
    return jnp.where(_own_lanes(ATT_BLK, 0), a2[:ATT_BLK], a2[ATT_BLK:])


def _attn_fwd(qr, kr, vr, d):
    rows, cols = qr.shape
    nb = rows // (AT_QB * ATT_BLK)

    def body(q_ref, kc_ref, kp_ref, vc_ref, vp_ref, o_ref, lse_ref):
        twice = lambda m: jnp.concatenate([m, m], axis=0)
        masks = {True: twice(_band_mask(pl.program_id(1) > 0, True)), False: twice(_band_mask(True, True))}
        ones = jnp.ones((2 * ATT_BLK, 128), MM)
        units = [(b, sl) for b, sl, h in _units() if h == 0]
        scs = [jnp.where(masks[b == 0], _dot_nt(_stack_heads(q_ref[_sub(b), sl]), _band_before(kc_ref, kp_ref, b, sl)),
                         NEG) for b, sl in units]
        ms = [jnp.max(sc, axis=-1, keepdims=True) for sc in scs]
        ps = [jnp.exp(sc - m).astype(MM) for sc, m in zip(scs, ms)]
        ols = [jnp.dot(p, jnp.concatenate([_band_before(vc_ref, vp_ref, b, sl), ones], axis=1),
                       preferred_element_type=F32) for p, (b, sl) in zip(ps, units)]
        for (b, sl), m, ol in zip(units, ms, ols):
            l = _unstack_heads(ol[:, 128:])
            o_ref[_sub(b), sl] = _unstack_heads(ol[:, :128]) / l
            lse_ref[_sub(b), sl] = _unstack_heads(jnp.broadcast_to(m, (2 * ATT_BLK, 128))) + jnp.log(l)

    cur, prev, _ = _attn_specs(rows)
    o, lse = pl.pallas_call(
        body, name=f"attn_fwd_d{d}", grid=(cols // AT_COLS, nb),
        in_specs=[cur, cur, prev, cur, prev], out_specs=[cur, cur],
        out_shape=[jax.ShapeDtypeStruct((rows, cols), F32)] * 2,
        compiler_params=_params("parallel", "parallel"),
    )(qr, kr, kr, vr, vr)
    return o, lse


def _attn_bwd_dq(qr, kr, vr, do, lse, delta, d):
    rows, cols = qr.shape
    nb = rows // (AT_QB * ATT_BLK)

    def body(q_ref, kc_ref, kp_ref, vc_ref, vp_ref, do_ref, lse_ref, dl_ref, dq_ref):
        masks = {True: _band_mask(pl.program_id(1) > 0, True), False: _band_mask(True, True)}
        units = _units()
        sms, dps = [], []
        for b, sl, h in units:
            own, own_b, neg = _own_lanes(ATT_BLK, h), _own_lanes(2 * ATT_BLK, h), _neg_pieces(2 * ATT_BLK, h)
            sms.append(_dot_nt(jnp.where(own, q_ref[_sub(b), sl], lse_ref[_sub(b), sl]),
                               jnp.where(own_b, _band_before(kc_ref, kp_ref, b, sl), neg)))
            dps.append(_dot_nt(jnp.where(own, do_ref[_sub(b), sl], dl_ref[_sub(b), sl]),
                               jnp.where(own_b, _band_before(vc_ref, vp_ref, b, sl), neg)))
        dss = [(jnp.exp(jnp.where(masks[b == 0], sm, NEG)) * dp).astype(MM)
               for sm, dp, (b, _, _) in zip(sms, dps, units)]
        dqs = [jnp.dot(ds, _band_before(kc_ref, kp_ref, b, sl), preferred_element_type=F32) * (AT_DH ** -0.5)
               for ds, (b, sl, _) in zip(dss, units)]
        for i in range(0, len(units), 2):
            b, sl, _ = units[i]
            dq_ref[_sub(b), sl] = jnp.where(_own_lanes(ATT_BLK, 0), dqs[i], dqs[i + 1]).astype(dq_ref.dtype)

    cur, prev, _ = _attn_specs(rows)
    dq = pl.pallas_call(
        body, name=f"attn_bwd_dq_d{d}", grid=(cols // AT_COLS, nb),
        in_specs=[cur, cur, prev, cur, prev, cur, cur, cur], out_specs=cur,
        out_shape=jax.ShapeDtypeStruct((rows, cols), MM),
        compiler_params=_params("parallel", "parallel"),
    )(qr, kr, kr, vr, vr, do, lse, delta)
    return dq


def _attn_bwd_dkv(qr, kr, vr, do, lse, delta, d):
    rows, cols = qr.shape
    nb = rows // (AT_QB * ATT_BLK)

    def body(k_ref, v_ref, qc_ref, qn_ref, doc_ref, don_ref, lsec_ref, lsen_ref, dlc_ref, dln_ref,
             dk_ref, dv_ref):
        masks = {True: _band_mask(True, pl.program_id(1) < nb - 1), False: _band_mask(True, True)}
        units = _units()
        sms, dps = [], []
        for b, sl, h in units:
            own, own_b, neg = _own_lanes(ATT_BLK, h), _own_lanes(2 * ATT_BLK, h), _neg_pieces(ATT_BLK, h)
            sms.append(_dot_nt(jnp.where(own, k_ref[_sub(b), sl], neg),
                               jnp.where(own_b, _band_after(qc_ref, qn_ref, b, sl),
                                         _band_after(lsec_ref, lsen_ref, b, sl))))
            dps.append(_dot_nt(jnp.where(own, v_ref[_sub(b), sl], neg),
                               jnp.where(own_b, _band_after(doc_ref, don_ref, b, sl),
                                         _band_after(dlc_ref, dln_ref, b, sl))))
        ps = [jnp.exp(jnp.where(masks[b == AT_QB - 1], sm, NEG)) for sm, (b, _, _) in zip(sms, units)]
        dss = [(p * dp).astype(MM) for p, dp in zip(ps, dps)]
        dvs = [jnp.dot(p.astype(MM), _band_after(doc_ref, don_ref, b, sl), preferred_element_type=F32)
               for p, (b, sl, _) in zip(ps, units)]
        dks = [jnp.dot(ds, _band_after(qc_ref, qn_ref, b, sl), preferred_element_type=F32)
               for ds, (b, sl, _) in zip(dss, units)]
        head0 = _own_lanes(ATT_BLK, 0)
        for i in range(0, len(units), 2):
            b, sl, _ = units[i]
            dk_ref[_sub(b), sl] = jnp.where(head0, dks[i], dks[i + 1]).astype(dk_ref.dtype)
            dv_ref[_sub(b), sl] = jnp.where(head0, dvs[i], dvs[i + 1]).astype(dv_ref.dtype)

    cur, _, nxt = _attn_specs(rows)
    dk, dv = pl.pallas_call(
        body, name=f"attn_bwd_dkv_d{d}", grid=(cols // AT_COLS, nb),
        in_specs=[cur, cur, cur, nxt, cur, nxt, cur, nxt, cur, nxt], out_specs=[cur, cur],
        out_shape=[jax.ShapeDtypeStruct((rows, cols), MM)] * 2,
        compiler_params=_params("parallel", "parallel"),
    )(kr, vr, qr, qr, do, do, lse, lse, delta, delta)
    return dk, dv


def _head_sum(a, width):
    parts = []
    for j in range(a.shape[1] // width):
        sm = jnp.sum(a[:, j * width:(j + 1) * width], axis=-1, keepdims=True)
        parts.append(jnp.broadcast_to(sm, (a.shape[0], width)))
    return jnp.concatenate(parts, axis=-1)


def _partner_pieces(x):
    xs = jnp.concatenate([pltpu.roll(x[:, j * 128:(j + 1) * 128], AT_DH, 1) for j in range(x.shape[1] // 128)],
                         axis=-1)
    hi = xs.astype(jnp.bfloat16).astype(F32)
    mid = (xs - hi).astype(jnp.bfloat16).astype(F32)
    lo = (xs - hi - mid).astype(jnp.bfloat16).astype(F32)
    lane = _iota2(x.shape, 1) % AT_DH
    return jnp.where(lane == 0, hi, jnp.where(lane == 1, mid, jnp.where(lane == 2, lo, 0.0)))


def _mid(x, tgt, proj, o_hg, o_at, lse_at, hg_norm_w, final_norm_w, wo_all, tm=256):
    s = x.shape[0]
    nb = s // tm

    def body(x_ref, t_ref, hgz_ref, atz_ref, ohg_ref, o1_ref, o2_ref, o3_ref, l1_ref, l2_ref, l3_ref,
             g_ref, fw_ref, wo_ref,
             dh_ref, dohg_ref, dhgz_ref, datz_ref, do1_ref, do2_ref, do3_ref, dl1_ref, dl2_ref, dl3_ref,
             lp1_ref, lp2_ref, lp3_ref,
             gwo_ref, gfw_ref, ghg_ref, loss_ref, nat_ref, stage_ref, gwo_acc):
        @pl.when(pl.program_id(0) == 0)
        def _():
            gwo_acc[...] = jnp.zeros_like(gwo_acc)
            gfw_ref[...] = jnp.zeros_like(gfw_ref)
            ghg_ref[...] = jnp.zeros_like(ghg_ref)
            loss_ref[...] = jnp.zeros_like(loss_ref)

        ohg, g = ohg_ref[...], g_ref[...]
        rs = lax.rsqrt(_head_sum(ohg * ohg, HG_D) * (1.0 / HG_D) + NORM_EPS)
        on = ohg * rs
        hgz = hgz_ref[...]
        sz = _sigmoid(hgz)
        gate_hg = hgz * sz
        lses, outs = [l1_ref[...]], [o1_ref[...]]
        for k, (d, l_ref, o_ref) in enumerate(zip(DILATIONS[1:], (l2_ref, l3_ref), (o2_ref, o3_ref))):
            _from_view(l_ref, nat_ref.at[2 * k], d)
            _from_view(o_ref, nat_ref.at[2 * k + 1], d)
            lses.append(_get_lanes(nat_ref.at[2 * k]))
            outs.append(_get_lanes(nat_ref.at[2 * k + 1]))
        mx = jnp.maximum(jnp.maximum(lses[0], lses[1]), lses[2])
        es = [jnp.exp(l - mx) for l in lses]
        den = es[0] + es[1] + es[2]
        ws = [e / den for e in es]
        oat = ws[0] * outs[0] + ws[1] * outs[1] + ws[2] * outs[2]
        atz = atz_ref[...]
        sa = _sigmoid(atz)
        gate_at = atz * sa
        mixed = jnp.concatenate([on * g * gate_hg, oat * gate_at], axis=-1).astype(MM)
        h = x_ref[...] + jnp.dot(mixed, wo_ref[...], preferred_element_type=F32)
        rstd = lax.rsqrt(jnp.mean(h * h, axis=-1, keepdims=True) + NORM_EPS)
        hn = h * rstd
        fw = fw_ref[...]
        err = hn * fw - t_ref[...]
        loss_ref[...] += 0.5 * jnp.sum(jnp.mean(err * err, axis=-1, keepdims=True), axis=0, keepdims=True)
        dout = err * (1.0 / D_MODEL)
        gfw_ref[...] += jnp.sum(dout * hn, axis=0, keepdims=True)
        dhn = dout * fw
        dh = rstd * (dhn - hn * jnp.mean(dhn * hn, axis=-1, keepdims=True))
        dh_ref[...] = dh
        dh_mm = dh.astype(MM)
        gwo_acc[...] += _dot_tn(mixed, dh_mm)

        @pl.when(pl.program_id(0) == nb - 1)
        def _():
            gwo_ref[...] = gwo_acc[...].astype(gwo_ref.dtype)

        dmixed = _dot_nt(dh_mm, wo_ref[...])
        dm_hg = dmixed[:, :SEC_W]
        d_ong = dm_hg * gate_hg
        dhgz_ref[...] = (dm_hg * (on * g) * (sz * (1.0 + hgz * (1.0 - sz)))).astype(MM)
        ghg_ref[...] += jnp.sum(d_ong * on, axis=0, keepdims=True)
        d_on = d_ong * g
        dohg_ref[...] = rs * (d_on - on * (_head_sum(d_on * on, HG_D) * (1.0 / HG_D)))
        dm_at = dmixed[:, SEC_W:]
        d_oat = dm_at * gate_at
        datz_ref[...] = (dm_at * oat * (sa * (1.0 + atz * (1.0 - sa)))).astype(MM)
        drow = _head_sum(d_oat * oat, AT_DH)
        lse_all = mx + jnp.log(den)
        for val, dst_refs in ((d_oat, (do1_ref, do2_ref, do3_ref)),
                              (_partner_pieces(drow), (dl1_ref, dl2_ref, dl3_ref)),
                              (_partner_pieces(lse_all), (lp1_ref, lp2_ref, lp3_ref))):
            _set_lanes(stage_ref, val)
            for d, dst_ref in zip(DILATIONS, dst_refs):
                _to_view(stage_ref, dst_ref, d)

    row = lambda w: pl.BlockSpec((tm, w), lambda i: (i, 0))
    sec = lambda j: pl.BlockSpec((None, tm, SEC_W), lambda i, j=j: (j, i, 0))
    const = lambda shp: pl.BlockSpec(shp, lambda i: (0,) * len(shp))
    half = row(SEC_W)
    views = [_view_spec(tm, d) for d in DILATIONS]
    return pl.pallas_call(
        body, name="mid", grid=(nb,),
        in_specs=[row(D_MODEL), row(D_MODEL), sec(PROJ_KEPT.index(3)), sec(PROJ_KEPT.index(7)), half] + views * 2
                 + [const((1, SEC_W)), const((1, D_MODEL)), const((D_MODEL, D_MODEL))],
        out_specs=[row(D_MODEL)] + [half] * 3 + views * 3
                  + [const((D_MODEL, D_MODEL)), const((1, D_MODEL)), const((1, SEC_W)), const((1, 1))],
        out_shape=[jax.ShapeDtypeStruct((s, D_MODEL), F32), jax.ShapeDtypeStruct((s, SEC_W), F32)]
                  + [jax.ShapeDtypeStruct((s, SEC_W), MM)] * 2
                  + [_view_shape(s, d, MM) for d in DILATIONS] * 3
                  + [jax.ShapeDtypeStruct((D_MODEL, D_MODEL), XCH), jax.ShapeDtypeStruct((1, D_MODEL), F32),
                     jax.ShapeDtypeStruct((1, SEC_W), F32), jax.ShapeDtypeStruct((1, 1), F32)],
        scratch_shapes=[pltpu.VMEM((4, LANE_GROUPS, tm, 128), F32), pltpu.VMEM((LANE_GROUPS, tm, 128), F32),
                        pltpu.VMEM((D_MODEL, D_MODEL), F32)],
        compiler_params=_params("arbitrary"),
    )(x, tgt, proj, proj, o_hg, *o_at, *lse_at, hg_norm_w, final_norm_w, wo_all)


def _section_specs(dsecs, tm):
    return [pl.BlockSpec((tm, SEC_W), lambda i: (i, 0)) if k is None
            else pl.BlockSpec((None, tm, SEC_W), lambda i, k=k: (k, i, 0)) for _, k in dsecs]


def _inproj_bwd_x(x, norm_w, w_all, dh, dsecs, token, tm=512):
    s = x.shape[0]

    def body(x_ref, nw_ref, w_ref, dh_ref, tok_ref, *refs):
        sec_refs, (gx_ref, gnw_ref) = refs[:N_SEC], refs[N_SEC:]

        @pl.when(pl.program_id(0) == 0)
        def _():
            gnw_ref[...] = jnp.zeros_like(gnw_ref)

        du = jnp.zeros((tm, D_MODEL), F32)
        for j in range(N_SEC):
            du = du + _dot_nt(sec_refs[j][...], w_ref[j])
        xv, nw = x_ref[...], nw_ref[...]
        rstd = lax.rsqrt(jnp.mean(xv * xv, axis=-1, keepdims=True) + NORM_EPS)
        xn = xv * rstd
        gnw_ref[...] += jnp.sum(du * xn, axis=0, keepdims=True)
        dxn = du * nw
        dx = rstd * (dxn - xn * jnp.mean(dxn * xn, axis=-1, keepdims=True))
        gx_ref[...] = (dh_ref[...] + tok_ref[0:1, 0:1]) + dx

    row = lambda w: pl.BlockSpec((tm, w), lambda i: (i, 0))
    const = lambda shp: pl.BlockSpec(shp, lambda i: (0,) * len(shp))
    return pl.pallas_call(
        body, name="inproj_bwd_x", grid=(s // tm,),
        in_specs=[row(D_MODEL), const((1, D_MODEL)), const((N_SEC, D_MODEL, SEC_W)), row(D_MODEL), const((8, 128))]
                 + _section_specs(dsecs, tm),
        out_specs=[row(D_MODEL), const((1, D_MODEL))],
        out_shape=[jax.ShapeDtypeStruct((s, D_MODEL), F32), jax.ShapeDtypeStruct((1, D_MODEL), F32)],
        compiler_params=_params("arbitrary"),
    )(x, norm_w, w_all, dh, token, *[a for a, _ in dsecs])


def _inproj_bwd_w(x, norm_w, dsec, dq_r, dk_r, dv, cos, sin, tm=512):
    s = x.shape[0]
    nb = s // tm

    def body(x_ref, nw_ref, s0, s1, s2, s3, s7, q1, q2, q3, k1, k2, k3, v1, v2, v3, cos_ref, sin_ref,
             gw_hbm, datt_ref, acc_ref, stage_ref, nat_ref):
        @pl.when(pl.program_id(0) == 0)
        def _():
            acc_ref[...] = jnp.zeros_like(acc_ref)

        def total(refs):
            acc = refs[0][...].astype(F32)
            for d, ref in zip(DILATIONS[1:], refs[1:]):
                _from_view(ref, nat_ref, d)
                acc = acc + _get_lanes(nat_ref)
            return acc

        c, sn = cos_ref[...], -sin_ref[...]
        unrot = lambda a: jnp.concatenate(
            [_rope128(a[:, j * 128:(j + 1) * 128], c, sn) for j in range(LANE_GROUPS)], axis=-1)
        att = [a.astype(MM) for a in (unrot(total((q1, q2, q3))), unrot(total((k1, k2, k3))), total((v1, v2, v3)))]
        for j, a in enumerate(att):
            datt_ref[j] = a
        xv = x_ref[...]
        rstd = lax.rsqrt(jnp.mean(xv * xv, axis=-1, keepdims=True) + NORM_EPS)
        u_t = (xv * rstd * nw_ref[...]).T.astype(MM)
        for j, dsj in enumerate((s0[...], s1[...], s2[...], s3[...], *att, s7[...])):
            acc_ref[j] += jnp.dot(u_t, dsj, preferred_element_type=F32)

        @pl.when(pl.program_id(0) == nb - 1)
        def _():
            for j in range(N_SEC):
                stage_ref[...] = acc_ref[j].astype(stage_ref.dtype)
                pltpu.sync_copy(stage_ref, gw_hbm.at[j])

    row = lambda w: pl.BlockSpec((tm, w), lambda i: (i, 0))
    return pl.pallas_call(
        body, name="inproj_bwd_w", grid=(nb,),
        in_specs=[row(D_MODEL), pl.BlockSpec((1, D_MODEL), lambda i: (0, 0))] + [row(SEC_W)] * 5
                 + [_view_spec(tm, d) for d in DILATIONS] * 3 + [row(128), row(128)],
        out_specs=[pl.BlockSpec(memory_space=pl.ANY), pl.BlockSpec((3, tm, SEC_W), lambda i: (0, i, 0))],
        out_shape=[jax.ShapeDtypeStruct((N_SEC, D_MODEL, SEC_W), XCH), jax.ShapeDtypeStruct((3, s, SEC_W), MM)],
        scratch_shapes=[pltpu.VMEM((N_SEC, D_MODEL, SEC_W), F32), pltpu.VMEM((D_MODEL, SEC_W), XCH),
                        pltpu.VMEM((LANE_GROUPS, tm, 128), F32)],
        compiler_params=_params("arbitrary"),
    )(x, norm_w, *dsec, *dq_r, *dk_r, *dv, cos, sin)


def _local_step(x, tgt, norm_w, w_all, lb_logits, hg_norm_w, wo_all, final_norm_w, on_weight_grads):
    s = x.shape[0]
    cos, sin = _rope_tables(s)
    proj, *qkv = _inproj_fwd(x, norm_w, w_all, cos, sin)
    o_hg, sst = _hgrn_fwd(proj, lb_logits)
    qkv = [qkv[3 * i:3 * i + 3] for i in range(len(DILATIONS))]
    att = [_attn_fwd(*qkv_d, d) for qkv_d, d in zip(qkv, DILATIONS)]
    (dh, d_ohg, d_hgz, d_atz, do1, do2, do3, dl1, dl2, dl3, lp1, lp2, lp3, gwo, gfw, ghg, loss) = _mid(
        x, tgt, proj, o_hg, [a[0] for a in att], [a[1] for a in att], hg_norm_w, final_norm_w[None, :], wo_all)
    dxq, dxf, dxi, dlb = _hgrn_bwd(proj, lb_logits, d_ohg, sst)
    dq_r, dk_r, dv = [], [], []
    for d, qkv_d, do, lp, dl in zip(DILATIONS, qkv, (do1, do2, do3), (lp1, lp2, lp3), (dl1, dl2, dl3)):
        dq_r.append(_attn_bwd_dq(*qkv_d, do, lp, dl, d))
        dk_d, dv_d = _attn_bwd_dkv(*qkv_d, do, lp, dl, d)
        dk_r.append(dk_d)
        dv.append(dv_d)
    gwi, d_att = _inproj_bwd_w(x, norm_w, (dxq, dxf, dxi, d_hgz, d_atz), dq_r, dk_r, dv, cos, sin)
    dsecs = [(dxq, None), (dxf, None), (dxi, None), (d_hgz, None), (d_att, 0), (d_att, 1), (d_att, 2), (d_atz, None)]
    token = on_weight_grads(gwi, gwo)
    gx, gnw = _inproj_bwd_x(x, norm_w, w_all, dh, dsecs, token)
    small = jnp.concatenate([gnw, jnp.concatenate([dlb, ghg], axis=-1), gfw,
                             jnp.pad(loss, ((0, 0), (0, D_MODEL - 1)))], axis=0)
    return gx, gwi, gwo, small


def _coords():
    return lax.axis_index("x"), lax.axis_index("y"), lax.axis_index("c")


def _gather_weights(w_in, w_out):
    wo_rows = w_out.shape[0]

    def body(wi_ref, wo_ref, wi_all, wo_all, send_sems, recv_sems):
        x, y, c = _coords()
        me, sibling = (x, y, c), (x, y, 1 - c)
        chips = [(1 - x, y), (x, 1 - y), (1 - x, 1 - y)]
        slot = lambda p: 4 * p[0] + 2 * p[1] + p[2]

        def copies(k, block, to):
            return [pltpu.make_async_remote_copy(
                src_ref=ref.at[slot(block)], dst_ref=ref.at[slot(block)], send_sem=send_sems.at[a, k],
                recv_sem=recv_sems.at[a, k], device_id=to, device_id_type=MESH)
                for a, ref in enumerate((wi_all, wo_all))]

        wi_all[slot(me)] = wi_ref[...].astype(MM)
        wo_all[slot(me)] = wo_ref[...].astype(MM)
        first = copies(0, me, sibling)
        for j, chip in enumerate(chips):
            first += copies(1 + j, me, (*chip, c))
        for cp in first:
            cp.start()
        passed = []
        for j, chip in enumerate(chips):
            for cp in copies(1 + j, (*chip, c), me):
                cp.wait_recv()
            fwd = copies(4 + j, (*chip, c), sibling)
            for cp in fwd:
                cp.start()
            passed += fwd
        for cp in copies(0, sibling, me):
            cp.wait_recv()
        for j, chip in enumerate(chips):
            for cp in copies(4 + j, (*chip, 1 - c), me):
                cp.wait_recv()
        for cp in first + passed:
            cp.wait_send()

    vmem = pl.BlockSpec(memory_space=pltpu.VMEM)
    return pl.pallas_call(
        body, name="gather_weights",
        in_specs=[vmem, vmem], out_specs=[vmem, vmem],
        out_shape=[jax.ShapeDtypeStruct((N_DEV, D_MODEL, SEC_W), MM),
                   jax.ShapeDtypeStruct((N_DEV, wo_rows, D_MODEL), MM)],
        scratch_shapes=[pltpu.SemaphoreType.DMA((2, 7)), pltpu.SemaphoreType.DMA((2, 7))],
        compiler_params=pltpu.CompilerParams(vmem_limit_bytes=VMEM_LIMIT),
    )(w_in, w_out)


def _me():
    x, y, c = _coords()
    return 4 * x + 2 * y + c


def _grad_copies(srcs, lands, send_sems, recv_sems):
    x, y, c = _coords()
    me = 4 * x + 2 * y + c
    copies = []
    for k in range(1, N_DEV):
        px, py, pc = x ^ (k >> 2), y ^ ((k >> 1) & 1), c ^ (k & 1)
        peer = 4 * px + 2 * py + pc
        for a, (src, dst) in enumerate(zip(srcs, lands)):
            copies.append(pltpu.make_async_remote_copy(
                src_ref=src.at[peer], dst_ref=dst.at[me], send_sem=send_sems.at[a * (N_DEV - 1) + k - 1],
                recv_sem=recv_sems.at[a * (N_DEV - 1) + k - 1], device_id=(px, py, pc), device_id_type=MESH))
    return copies


HBM_SPEC = pl.BlockSpec(memory_space=pltpu.HBM)
SEM_SPEC = pl.BlockSpec(memory_space=pltpu.SEMAPHORE)
SPLIT_COPY_EFFECT = pltpu.SideEffectType.DATAFLOW_SIDE_EFFECTING


def _exchange_start(gwi, gwo):
    def body(gwi_ref, gwo_ref, li_ref, lo_ref, send_sems, recv_sems, gwi_thru, gwo_thru, li_thru, lo_thru, token):
        for cp in _grad_copies((gwi_ref, gwo_ref), (li_ref, lo_ref), send_sems, recv_sems):
            cp.start()
        token[...] = jnp.zeros_like(token)

    hbm = lambda a: pltpu.with_memory_space_constraint(a, pltpu.HBM)
    bufs = (gwi, gwo, lax.empty(gwi.shape, gwi.dtype), lax.empty(gwo.shape, gwo.dtype))
    return pl.pallas_call(
        body, name="exchange_start",
        out_shape=(pltpu.SemaphoreType.DMA((2 * (N_DEV - 1),)), pltpu.SemaphoreType.DMA((2 * (N_DEV - 1),)),
                   *[pltpu.HBM(a.shape, a.dtype) for a in bufs], jax.ShapeDtypeStruct((8, 128), F32)),
        in_specs=[HBM_SPEC] * 4,
        out_specs=(SEM_SPEC, SEM_SPEC, HBM_SPEC, HBM_SPEC, HBM_SPEC, HBM_SPEC, pl.BlockSpec(memory_space=pltpu.VMEM)),
        input_output_aliases={0: 2, 1: 3, 2: 4, 3: 5},
        compiler_params=pltpu.CompilerParams(has_side_effects=SPLIT_COPY_EFFECT),
    )(*[hbm(a) for a in bufs])


def _exchange_wait(send_sems, recv_sems, gwi, gwo, li, lo, after):
    def body(gwi_ref, gwo_ref, li_ref, lo_ref, send_sems, recv_sems, after_ref, gwi_out, gwo_out, li_out, lo_out):
        for cp in _grad_copies((gwi_ref, gwo_ref), (li_ref, lo_ref), send_sems, recv_sems):
            cp.wait_send()
            cp.wait_recv()

    return pl.pallas_call(
        body, name="exchange_wait",
        out_shape=tuple(pltpu.HBM(a.shape, a.dtype) for a in (gwi, gwo, li, lo)),
        in_specs=[HBM_SPEC] * 4 + [SEM_SPEC, SEM_SPEC, pl.BlockSpec(memory_space=pl.ANY)],
        out_specs=(HBM_SPEC,) * 4,
        input_output_aliases={0: 0, 1: 1, 2: 2, 3: 3},
        compiler_params=pltpu.CompilerParams(has_side_effects=SPLIT_COPY_EFFECT),
    )(gwi, gwo, li, lo, send_sems, recv_sems, after)


def _gather_small(small):
    def body(sm_ref, ls_ref, send_sems, recv_sems, local_sem):
        x, y, c = _coords()
        me = 4 * x + 2 * y + c
        own = pltpu.make_async_copy(sm_ref, ls_ref.at[me], local_sem)
        own.start()
        sends = []
        for k in range(1, N_DEV):
            peer = (x ^ (k >> 2), y ^ ((k >> 1) & 1), c ^ (k & 1))
            sends.append(pltpu.make_async_remote_copy(
                src_ref=sm_ref, dst_ref=ls_ref.at[me], send_sem=send_sems.at[k - 1], recv_sem=recv_sems.at[k - 1],
                device_id=peer, device_id_type=MESH))
        for cp in sends:
            cp.start()
        for cp in sends:
            cp.wait_recv()
        for cp in sends:
            cp.wait_send()
        own.wait()

    vmem = pl.BlockSpec(memory_space=pltpu.VMEM)
    return pl.pallas_call(
        body, name="gather_small", in_specs=[vmem], out_specs=vmem,
        out_shape=jax.ShapeDtypeStruct((N_DEV,) + small.shape, F32),
        scratch_shapes=[pltpu.SemaphoreType.DMA((N_DEV - 1,)), pltpu.SemaphoreType.DMA((N_DEV - 1,)),
                        pltpu.SemaphoreType.DMA],
    )(small)


def _adamw(w, g, m, v):
    m = ADAM_B1 * m + (1.0 - ADAM_B1) * g
    v = ADAM_B2 * v + (1.0 - ADAM_B2) * (g * g)
    m_hat = m / (1.0 - ADAM_B1 ** ADAM_STEP)
    v_hat = v / (1.0 - ADAM_B2 ** ADAM_STEP)
    return -ADAM_LR * (m_hat / (jnp.sqrt(v_hat) + ADAM_EPS) + ADAM_WD * w), m, v


def _slot_sum(ref, own=None, me=None):
    g = None
    for i in range(N_DEV):
        term = ref[i].astype(F32)
        if own is not None:
            term = jnp.where(i == me, own, term)
        g = term if g is None else g + term
    return g


def _update_matrix(name, me, landed, own, w, m, v, rows):
    r, c = w.shape

    def body(me_ref, l_ref, own_ref, w_ref, m_ref, v_ref, g_ref, d_ref, nm_ref, nv_ref):
        g = _slot_sum(l_ref, own_ref[...].astype(F32), me_ref[0])
        g_ref[...] = g
        d_ref[...], nm_ref[...], nv_ref[...] = _adamw(w_ref[...], g, m_ref[...], v_ref[...])

    blk = pl.BlockSpec((rows, c), lambda i, me_ref: (i, 0))
    return pl.pallas_call(
        body, name=name,
        grid_spec=pltpu.PrefetchScalarGridSpec(
            num_scalar_prefetch=1, grid=(r // rows,),
            in_specs=[pl.BlockSpec((N_DEV, rows, c), lambda i, me_ref: (0, i, 0)),
                      pl.BlockSpec((None, rows, c), lambda i, me_ref: (me_ref[0], i, 0)), blk, blk, blk],
            out_specs=[blk] * 4),
        out_shape=[jax.ShapeDtypeStruct((r, c), F32)] * 4,
        compiler_params=_params("parallel"),
    )(me, landed, own, w, m, v)


def _update_small(landed, lb_logits, ws, ms, vs):
    def body(l_ref, lbl_ref, w_ref, m_ref, v_ref, g_ref, d_ref, nm_ref, nv_ref, loss_ref):
        tot = _slot_sum(l_ref)
        _, dlb = _lower_bound(lbl_ref[...])
        g_lb = tot[1:2, :SEC_W] * dlb
        g = jnp.concatenate([tot[0:1], jnp.concatenate([g_lb, -g_lb], axis=-1),
                             jnp.pad(tot[1:2, SEC_W:], ((0, 0), (0, SEC_W))), tot[2:3]], axis=0)
        g_ref[...] = g
        d_ref[...], nm_ref[...], nv_ref[...] = _adamw(w_ref[...], g, m_ref[...], v_ref[...])
        loss_ref[...] = tot[3:4, 0:1]

    vmem = pl.BlockSpec(memory_space=pltpu.VMEM)
    return pl.pallas_call(
        body, name="update_small", in_specs=[vmem] * 5, out_specs=[vmem] * 5,
        out_shape=[jax.ShapeDtypeStruct((4, D_MODEL), F32)] * 4 + [jax.ShapeDtypeStruct((1, 1), F32)],
    )(landed, lb_logits, ws, ms, vs)


def _pack_small(norm_w, lb_logits, hg_norm_w, final_norm_w):
    return jnp.concatenate([norm_w, lb_logits.reshape(1, D_MODEL),
                            jnp.pad(hg_norm_w, ((0, 0), (0, D_MODEL - SEC_W))), final_norm_w[None, :]], axis=0)


def _unpack_small(a):
    return a[0:1], a[1].reshape(2, SEC_W), a[2:3, :SEC_W], a[3]


def kernel(x, norm_w, w_in, hgrn_lb_logits, hg_norm_w, w_out, final_norm_w, loss_target, m_norm_w, m_w_in, m_hgrn_lb_logits, m_hg_norm_w, m_w_out, m_final_norm_w, v_norm_w, v_w_in, v_hgrn_lb_logits, v_hg_norm_w, v_w_out, v_final_norm_w):
    w_all, wo_all = _gather_weights(w_in[0], w_out[0])
    in_flight = []

    def start_exchange(gwi, gwo):
        *handles, token = _exchange_start(gwi, gwo.reshape(N_DEV, D_MODEL // N_DEV, D_MODEL))
        in_flight.extend(handles)
        return token

    gx, _, _, small = _local_step(x[0], loss_target[0], norm_w, w_all, hgrn_lb_logits, hg_norm_w,
                                  wo_all.reshape(D_MODEL, D_MODEL), final_norm_w, start_exchange)
    ls = _gather_small(small)
    gwi, gwo, li, lo = _exchange_wait(*in_flight, gx)
    me = _me().astype(jnp.int32).reshape(1)
    g_wi, d_wi, nm_wi, nv_wi = _update_matrix("update_w_in", me, li, gwi, w_in[0], m_w_in[0], v_w_in[0], 256)
    g_wo, d_wo, nm_wo, nv_wo = _update_matrix("update_w_out", me, lo, gwo, w_out[0], m_w_out[0], v_w_out[0], 128)
    g_s, d_s, nm_s, nv_s, loss = _update_small(
        ls, hgrn_lb_logits, _pack_small(norm_w, hgrn_lb_logits, hg_norm_w, final_norm_w),
        _pack_small(m_norm_w, m_hgrn_lb_logits, m_hg_norm_w, m_final_norm_w),
        _pack_small(v_norm_w, v_hgrn_lb_logits, v_hg_norm_w, v_final_norm_w))
    outs = []
    for small_out, wi, wo in ((g_s, g_wi, g_wo), (d_s, d_wi, d_wo), (nm_s, nm_wi, nm_wo), (nv_s, nv_wi, nv_wo)):
        nw, lb, hg, fw = _unpack_small(small_out)
        outs += [nw, wi[None], lb, hg, wo[None], fw]
    return (loss[0, 0], gx[None], *outs)
```

```python
import functools

import jax
import jax.numpy as jnp
from jax import lax
from jax.experimental import pallas as pl
from jax.experimental.pallas import tpu as pltpu

F32 = jnp.float32
MM = jnp.bfloat16
XCH = jnp.bfloat16
NORM_EPS = 1e-6
NEG = -1e30
N_DEV = 8
D_MODEL = 1024
N_SEC = 8
SEC_W = 512
HG_HEADS = 4
HG_D = 128
HG_GROUP = 4
AT_HEADS = 8
AT_DH = 64
ATT_BLK = 128
AT_COLS = 512
AT_QB = 8
DILATIONS = (1, 4, 16)
ROPE_THETA = 10000.0
CH = 16
LB_LO, LB_HI = 1e-6, 1.0 - 1e-6
ADAM_LR, ADAM_B1, ADAM_B2, ADAM_EPS, ADAM_WD, ADAM_STEP = 0.001, 0.9, 0.999, 1e-08, 0.01, 10
VMEM_LIMIT = 56 * 1024 * 1024
MESH = pl.DeviceIdType.MESH


def _params(*sem):
    return pltpu.CompilerParams(dimension_semantics=sem, vmem_limit_bytes=VMEM_LIMIT)


def _sigmoid(x):
    return 1.0 / (1.0 + jnp.exp(-x))


def _dot(a, b):
    return jnp.dot(a.astype(MM), b.astype(MM), preferred_element_type=F32)


def _dot_nt(a, b):
    return lax.dot_general(a.astype(MM), b.astype(MM), (((1,), (1,)), ((), ())), preferred_element_type=F32)


def _dot_tn(a, b):
    return lax.dot_general(a.astype(MM), b.astype(MM), (((0,), (0,)), ((), ())), preferred_element_type=F32)


def _tri_dot(tri, g):
    g1 = g.astype(jnp.bfloat16)
    r1 = g - g1.astype(F32)
    g2 = r1.astype(jnp.bfloat16)
    g3 = (r1 - g2.astype(F32)).astype(jnp.bfloat16)
    t = tri.astype(jnp.bfloat16)
    d = functools.partial(jnp.dot, preferred_element_type=F32)
    return d(t, g1) + d(t, g2) + d(t, g3)


def _lower_bound(lbl):
    l0, l1 = lbl[0:1, :], lbl[1:2, :]
    m = jnp.maximum(l0, l1)
    e0, e1 = jnp.exp(l0 - m), jnp.exp(l1 - m)
    p = e0 / (e0 + e1)
    inside = (p >= LB_LO) & (p <= LB_HI)
    return jnp.clip(p, LB_LO, LB_HI), jnp.where(inside, p * (e1 / (e0 + e1)), 0.0)


def _iota2(shape, dim):
    return lax.broadcasted_iota(jnp.int32, shape, dim)


def _hgrn_gates(xq, xf, lb):
    sgq = _sigmoid(xq)
    sg = _sigmoid(xf)
    sn = _sigmoid(-xf)
    f = lb + (1.0 - lb) * sg
    return sgq, xq * sgq, sg, sn, f, (1.0 - lb) * sn


def _bdot(a, b, ca, cb):
    return lax.dot_general(a.astype(MM), b.astype(MM), (((ca,), (cb,)), ((0,), (0,))), preferred_element_type=F32)


def _chunk_masks(rb):
    row, col = _iota2((rb, rb), 0), _iota2((rb, rb), 1)
    same = (row // CH) == (col // CH)
    return same & (row >= col), same & (row <= col)


def _hgrn_fwd(proj, lb_logits, rb=256):
    s = proj.shape[1]
    nb, nc = s // rb, rb // CH

    def body(q_ref, f_ref, i_ref, lbl_ref, o_ref, sst_ref, st_ref, slab_ref, states_ref):
        @pl.when(pl.program_id(1) == 0)
        def _():
            st_ref[...] = jnp.zeros_like(st_ref)

        sst_ref[...] = st_ref[...]
        prefix, _ = _chunk_masks(rb)
        c3 = lambda a: a.reshape(nc, CH, HG_D)
        row, col = _iota2((nc, CH, CH), 1), _iota2((nc, CH, CH), 2)
        heads = []
        for g in range(HG_GROUP):
            hs = slice(g * HG_D, (g + 1) * HG_D)
            lb, _ = _lower_bound(lbl_ref[:, hs])
            _, q, _, _, f, kk = _hgrn_gates(q_ref[:, hs], f_ref[:, hs], lb)
            b3 = c3(_tri_dot(prefix, jnp.log(f)))
            q3, kk3, v3 = c3(q), c3(kk), c3(i_ref[:, hs])
            bl3 = b3[:, CH - 1:CH, :]
            for t in range(CH):
                slab_ref[g, :, t * CH:(t + 1) * CH, :] = (
                    q3 * jnp.exp(jnp.minimum(b3 - b3[:, t:t + 1, :], 0.0))).astype(MM)
            x_upd = _bdot(v3, kk3 * jnp.exp(bl3 - b3), 1, 1)
            heads.append(dict(hs=hs, kk3=kk3, v3=v3, qe3=q3 * jnp.exp(b3), ebl3=jnp.exp(bl3), x_upd=x_upd))
        for g, hd in enumerate(heads):
            st = st_ref[g]
            for c in range(nc):
                states_ref[g, c] = st
                st = st * hd["ebl3"][c] + hd["x_upd"][c]
            st_ref[g] = st
        for g, hd in enumerate(heads):
            r = _bdot(slab_ref[g], hd["kk3"], 2, 2)
            a = jnp.zeros((nc, CH, CH), F32)
            for t in range(CH):
                a = a + jnp.where(col == t, r[:, t * CH:(t + 1) * CH, :], 0.0)
            a = jnp.where(row >= col, a, 0.0)
            o3 = _bdot(hd["qe3"], states_ref[g], 2, 2) + _bdot(a, hd["v3"], 2, 1)
            o_ref[:, hd["hs"]] = o3.reshape(rb, HG_D)

    wide = HG_GROUP * HG_D
    sec = lambda j: pl.BlockSpec((None, rb, wide), lambda h, i, j=j: (j, i, h))
    return pl.pallas_call(
        body, name="hgrn_fwd", grid=(HG_HEADS // HG_GROUP, nb),
        in_specs=[sec(0), sec(1), sec(2), pl.BlockSpec((2, wide), lambda h, i: (0, h))],
        out_specs=[pl.BlockSpec((rb, wide), lambda h, i: (i, h)),
                   pl.BlockSpec((None, HG_GROUP, HG_D, HG_D), lambda h, i: (i, h, 0, 0))],
        out_shape=[jax.ShapeDtypeStruct((s, SEC_W), F32),
                   jax.ShapeDtypeStruct((nb, HG_HEADS, HG_D, HG_D), F32)],
        scratch_shapes=[pltpu.VMEM((HG_GROUP, HG_D, HG_D), F32), pltpu.VMEM((HG_GROUP, nc, CH * CH, HG_D), MM),
                        pltpu.VMEM((HG_GROUP, nc, HG_D, HG_D), F32)],
        compiler_params=_params("parallel", "arbitrary"),
    )(proj, proj, proj, lb_logits)


def _hgrn_bwd(proj, lb_logits, d_o, sst, rb=256):
    s = proj.shape[1]
    nb, nc = s // rb, rb // CH

    def body(q_ref, f_ref, i_ref, lbl_ref, do_ref, sst_ref, dxq_ref, dxf_ref, dxi_ref, dlb_ref,
             dst_ref, states_ref, dstates_ref, lslab_ref, kslab_ref):
        @pl.when(pl.program_id(1) == 0)
        def _():
            dst_ref[...] = jnp.zeros_like(dst_ref)
            dlb_ref[...] = jnp.zeros_like(dlb_ref)

        prefix, suffix = _chunk_masks(rb)
        c3 = lambda a: a.reshape(nc, CH, HG_D)
        flat = lambda a: a.reshape(rb, HG_D)
        row, col = _iota2((nc, CH, CH), 1), _iota2((nc, CH, CH), 2)
        tril, triu = row >= col, row <= col
        sel = (_iota2((CH, CH * CH), 1) % CH == _iota2((CH, CH * CH), 0)).astype(MM)
        blockdiag = _iota2((nc, CH, CH * CH), 2) // CH == _iota2((nc, CH, CH * CH), 1)
        tile = lambda m: jnp.where(blockdiag, _dot(m.reshape(rb, CH), sel).reshape(nc, CH, CH * CH), 0.0)
        last = _iota2((nc, CH, HG_D), 1) == CH - 1
        heads = []
        for g in range(HG_GROUP):
            hs = slice(g * HG_D, (g + 1) * HG_D)
            lb, _ = _lower_bound(lbl_ref[:, hs])
            xq = q_ref[:, hs]
            sgq, q, sg, sn, f, kk = _hgrn_gates(xq, f_ref[:, hs], lb)
            b3 = c3(_tri_dot(prefix, jnp.log(f)))
            q3, kk3, v3, do3 = c3(q), c3(kk), c3(i_ref[:, hs]), c3(do_ref[:, hs])
            bl3 = b3[:, CH - 1:CH, :]
            eb3, ebl3, dec3 = jnp.exp(b3), jnp.exp(bl3), jnp.exp(bl3 - b3)
            qe3, kd3 = q3 * eb3, kk3 * dec3
            x_upd, y_upd = _bdot(v3, kd3, 1, 1), _bdot(do3, qe3, 1, 1)
            for t in range(CH):
                bt = b3[:, t:t + 1, :]
                lslab_ref[g, :, t * CH:(t + 1) * CH, :] = (q3 * jnp.exp(jnp.minimum(b3 - bt, 0.0))).astype(MM)
                kslab_ref[g, :, t * CH:(t + 1) * CH, :] = (kk3 * jnp.exp(jnp.minimum(bt - b3, 0.0))).astype(MM)
            d_a = jnp.where(tril, _bdot(do3, v3, 2, 2), 0.0)
            d_at = jnp.where(triu, _bdot(v3, do3, 2, 2), 0.0)
            heads.append(dict(hs=hs, lb=lb, xq=xq, sgq=sgq, sg=sg, sn=sn, f=f, q3=q3, kk3=kk3, v3=v3, do3=do3,
                              eb3=eb3, ebl3=ebl3, dec3=dec3, qe3=qe3, kd3=kd3, x_upd=x_upd, y_upd=y_upd,
                              d_a=d_a, d_at=d_at))
        for g, hd in enumerate(heads):
            st = sst_ref[g]
            for c in range(nc):
                states_ref[g, c] = st
                st = st * hd["ebl3"][c] + hd["x_upd"][c]
            dst = dst_ref[g]
            for c in reversed(range(nc)):
                dstates_ref[g, c] = dst
                dst = dst * hd["ebl3"][c] + hd["y_upd"][c]
            dst_ref[g] = dst
        for g, hd in enumerate(heads):
            q3, v3, do3, kd3 = hd["q3"], hd["v3"], hd["do3"], hd["kd3"]
            states, dstates = states_ref[g], dstates_ref[g]
            hd["dqe"] = _bdot(do3, states, 2, 1)
            hd["dkd"] = _bdot(v3, dstates, 2, 1)
            r = _bdot(kslab_ref[g], q3, 2, 2)
            a_t = jnp.zeros((nc, CH, CH), F32)
            for t in range(CH):
                a_t = a_t + jnp.where(col == t, r[:, t * CH:(t + 1) * CH, :], 0.0)
            a_t = jnp.where(triu, a_t, 0.0)
            hd["dv"] = _bdot(kd3, dstates, 2, 2) + _bdot(a_t, do3, 2, 1)
            hd["dq_in"] = _bdot(tile(hd["d_a"]), kslab_ref[g], 2, 1)
            hd["dk_in"] = _bdot(tile(hd["d_at"]), lslab_ref[g], 2, 1)
            hd["ss"] = jnp.sum(dstates * states, axis=1, keepdims=True)
        for g, hd in enumerate(heads):
            q3, kk3, eb3, ebl3, dec3, qe3, kd3 = (hd[k] for k in ("q3", "kk3", "eb3", "ebl3", "dec3", "qe3", "kd3"))
            dqe, dkd, dq_in, dk_in = hd["dqe"], hd["dkd"], hd["dq_in"], hd["dk_in"]
            dkd_kd = dkd * kd3
            db = dqe * qe3 - dkd_kd + q3 * dq_in - kk3 * dk_in
            dbl = jnp.sum(dkd_kd, axis=1, keepdims=True) + hd["ss"] * ebl3
            dg = _tri_dot(suffix, flat(db + jnp.where(last, dbl, 0.0)))
            df = dg / hd["f"] - flat(dkd * dec3 + dk_in)
            xq, sgq, hs = hd["xq"], hd["sgq"], hd["hs"]
            dxq_ref[:, hs] = (flat(dqe * eb3 + dq_in) * (sgq * (1.0 + xq * (1.0 - sgq)))).astype(MM)
            dxf_ref[:, hs] = (df * (1.0 - hd["lb"]) * hd["sg"] * hd["sn"]).astype(MM)
            dxi_ref[:, hs] = flat(hd["dv"]).astype(MM)
            dlb_ref[:, hs] += jnp.sum(df * hd["sn"], axis=0, keepdims=True)

    wide = HG_GROUP * HG_D
    rev = lambda i: nb - 1 - i
    sec = lambda j: pl.BlockSpec((None, rb, wide), lambda h, i, j=j: (j, rev(i), h))
    blk = pl.BlockSpec((rb, wide), lambda h, i: (rev(i), h))
    state = (pltpu.VMEM((HG_GROUP, nc, HG_D, HG_D), F32), pltpu.VMEM((HG_GROUP, nc, CH * CH, HG_D), MM))
    return pl.pallas_call(
        body, name="hgrn_bwd", grid=(HG_HEADS // HG_GROUP, nb),
        in_specs=[sec(0), sec(1), sec(2), pl.BlockSpec((2, wide), lambda h, i: (0, h)), blk,
                  pl.BlockSpec((None, HG_GROUP, HG_D, HG_D), lambda h, i: (rev(i), h, 0, 0))],
        out_specs=[blk, blk, blk, pl.BlockSpec((1, wide), lambda h, i: (0, h))],
        out_shape=[jax.ShapeDtypeStruct((s, SEC_W), MM)] * 3 + [jax.ShapeDtypeStruct((1, SEC_W), F32)],
        scratch_shapes=[pltpu.VMEM((HG_GROUP, HG_D, HG_D), F32), state[0], state[0], state[1], state[1]],
        compiler_params=_params("parallel", "arbitrary"),
    )(proj, proj, proj, lb_logits, d_o, sst)


def _rope_tables(s):
    half = AT_DH // 2
    inv_freq = 1.0 / (ROPE_THETA ** (jnp.arange(half, dtype=F32) / half))
    ang = jnp.arange(s, dtype=jnp.int32).astype(F32)[:, None] * inv_freq[None, :]
    cos, sin = jnp.cos(ang), jnp.sin(ang)
    return jnp.concatenate([cos] * 4, axis=-1), jnp.concatenate([-sin, sin] * 2, axis=-1)


def _rope128(x, cos, sin):
    lo = (_iota2(x.shape, 1) % AT_DH) < AT_DH // 2
    rot = jnp.where(lo, pltpu.roll(x, 128 - AT_DH // 2, 1), pltpu.roll(x, AT_DH // 2, 1))
    return x * cos + rot * sin


LANE_GROUPS = SEC_W // 128


def _set_lanes(ref, val):
    for j in range(LANE_GROUPS):
        ref[j] = val[:, j * 128:(j + 1) * 128]


def _get_lanes(ref):
    return jnp.concatenate([ref[j] for j in range(LANE_GROUPS)], axis=-1)


def _to_view(src_ref, dst_ref, d):
    n = src_ref.shape[1] // d
    for r in range(d):
        rows = pl.ds(r, n, stride=d) if d > 1 else slice(None)
        for j in range(LANE_GROUPS):
            c0 = r * SEC_W + j * 128
            dst_ref[:, c0:c0 + 128] = src_ref.at[j][rows, :].astype(dst_ref.dtype)


def _from_view(src_ref, dst_ref, d):
    n = dst_ref.shape[1] // d
    for r in range(d):
        for j in range(LANE_GROUPS):
            c0 = r * SEC_W + j * 128
            dst_ref.at[j][pl.ds(r, n, stride=d), :] = src_ref[:, c0:c0 + 128].astype(dst_ref.dtype)


def _view_spec(tm, d):
    return pl.BlockSpec((tm // d, d * SEC_W), lambda i: (i, 0))


def _view_shape(s, d, dtype):
    return jax.ShapeDtypeStruct((s // d, d * SEC_W), dtype)


PROJ_KEPT = (0, 1, 2, 3, 7)


def _inproj_fwd(x, norm_w, w_all, cos, sin, tm=512):
    s = x.shape[0]

    def body(x_ref, nw_ref, w_ref, cos_ref, sin_ref, proj_ref, *refs):
        outs, (qs_ref, ks_ref, vs_ref) = refs[:-3], refs[-3:]
        xv = x_ref[...]
        rstd = lax.rsqrt(jnp.mean(xv * xv, axis=-1, keepdims=True) + NORM_EPS)
        u = (xv * rstd * nw_ref[...]).astype(MM)
        for slot, j in enumerate(PROJ_KEPT):
            proj_ref[slot] = jnp.dot(u, w_ref[j], preferred_element_type=F32)
        q, k, v = [jnp.dot(u, w_ref[j], preferred_element_type=F32) for j in (4, 5, 6)]
        c, sn = cos_ref[...], sin_ref[...]
        for g in range(LANE_GROUPS):
            sl = slice(g * 128, (g + 1) * 128)
            qs_ref[g] = _rope128(q[:, sl], c, sn) * (AT_DH ** -0.5)
            ks_ref[g] = _rope128(k[:, sl], c, sn)
            vs_ref[g] = v[:, sl]
        for i, d in enumerate(DILATIONS):
            for src_ref, dst_ref in zip((qs_ref, ks_ref, vs_ref), outs[3 * i:3 * i + 3]):
                _to_view(src_ref, dst_ref, d)

    tab = pl.BlockSpec((tm, 128), lambda i: (i, 0))
    return pl.pallas_call(
        body, name="inproj_fwd", grid=(s // tm,),
        in_specs=[pl.BlockSpec((tm, D_MODEL), lambda i: (i, 0)),
                  pl.BlockSpec((1, D_MODEL), lambda i: (0, 0)),
                  pl.BlockSpec((N_SEC, D_MODEL, SEC_W), lambda i: (0, 0, 0)), tab, tab],
        out_specs=[pl.BlockSpec((len(PROJ_KEPT), tm, SEC_W), lambda i: (0, i, 0))]
                  + [_view_spec(tm, d) for d in DILATIONS for _ in range(3)],
        out_shape=[jax.ShapeDtypeStruct((len(PROJ_KEPT), s, SEC_W), F32)]
                  + [_view_shape(s, d, MM) for d in DILATIONS for _ in range(3)],
        scratch_shapes=[pltpu.VMEM((LANE_GROUPS, tm, 128), F32)] * 3,
        compiler_params=_params("parallel"),
    )(x, norm_w, w_all, cos, sin)


def _band_mask(first_ok, second_ok):
    row, col = _iota2((ATT_BLK, 2 * ATT_BLK), 0), _iota2((ATT_BLK, 2 * ATT_BLK), 1)
    return ((col < ATT_BLK) & (col >= row) & first_ok) | ((col >= ATT_BLK) & ((col - ATT_BLK) <= row) & second_ok)


def _own_lanes(rows, h):
    lane = _iota2((rows, 128), 1)
    return (lane < AT_DH) if h == 0 else (lane >= AT_DH)


def _neg_pieces(rows, h):
    lane = _iota2((rows, 128), 1) - (AT_DH if h == 0 else 0)
    return jnp.where((lane >= 0) & (lane < 3), -1.0, 0.0).astype(MM)


def _units(qb):
    return [(b, slice(g * 128, (g + 1) * 128), h) for b in range(qb) for g in range(AT_COLS // 128) for h in range(2)]


def _sub(b):
    return slice(b * ATT_BLK, (b + 1) * ATT_BLK)


def _band_before(cur_ref, prev_ref, b, sl):
    if b == 0:
        return jnp.concatenate([prev_ref[:, sl], cur_ref[0:ATT_BLK, sl]], axis=0)
    return cur_ref[(b - 1) * ATT_BLK:(b + 1) * ATT_BLK, sl]


def _band_after(cur_ref, next_ref, b, sl):
    if (b + 1) * ATT_BLK == cur_ref.shape[0]:
        return jnp.concatenate([cur_ref[b * ATT_BLK:(b + 1) * ATT_BLK, sl], next_ref[:, sl]], axis=0)
    return cur_ref[b * ATT_BLK:(b + 2) * ATT_BLK, sl]


def _attn_specs(rows):
    qb = min(AT_QB, rows // ATT_BLK)
    assert rows % (qb * ATT_BLK) == 0
    last = rows // ATT_BLK - 1
    cur = pl.BlockSpec((qb * ATT_BLK, AT_COLS), lambda c, n: (n, c))
    prev = pl.BlockSpec((ATT_BLK, AT_COLS), lambda c, n: (jnp.maximum(qb * n - 1, 0), c))
    nxt = pl.BlockSpec((ATT_BLK, AT_COLS), lambda c, n: (jnp.minimum(qb * (n + 1), last), c))
    return qb, cur, prev, nxt


def _stack_heads(a):
    h0 = _own_lanes(a.shape[0], 0)
    zero = jnp.zeros_like(a)
    return jnp.concatenate([jnp.where(h0, a, zero), jnp.where(h0, zero, a)], axis=0)


def _unstack_heads(a2):
    return jnp.where(_own_lanes(ATT_BLK, 0), a2[:ATT_BLK], a2[ATT_BLK:])


def _attn_fwd(qr, kr, vr, d):
    rows, cols = qr.shape
    qb, cur, prev, nxt = _attn_specs(rows)
    nb = rows // (qb * ATT_BLK)

    def body(q_ref, kc_ref, kp_ref, vc_ref, vp_ref, o_ref, lse_ref):
        twice = lambda m: jnp.concatenate([m, m], axis=0)
        masks = {True: twice(_band_mask(pl.program_id(1) > 0, True)), False: twice(_band_mask(True, True))}
        ones = jnp.ones((2 * ATT_BLK, 128), MM)
        units = [(b, sl) for b, sl, h in _units(qb) if h == 0]
        scs = [jnp.where(masks[b == 0], _dot_nt(_stack_heads(q_ref[_sub(b), sl]), _band_before(kc_ref, kp_ref, b, sl)),
                         NEG) for b, sl in units]
        ms = [jnp.max(sc, axis=-1, keepdims=True) for sc in scs]
        ps = [jnp.exp(sc - m).astype(MM) for sc, m in zip(scs, ms)]
        ols = [jnp.dot(p, jnp.concatenate([_band_before(vc_ref, vp_ref, b, sl), ones], axis=1),
                       preferred_element_type=F32) for p, (b, sl) in zip(ps, units)]
        for (b, sl), m, ol in zip(units, ms, ols):
            l = _unstack_heads(ol[:, 128:])
            o_ref[_sub(b), sl] = _unstack_heads(ol[:, :128]) / l
            lse_ref[_sub(b), sl] = _unstack_heads(jnp.broadcast_to(m, (2 * ATT_BLK, 128))) + jnp.log(l)

    o, lse = pl.pallas_call(
        body, name=f"attn_fwd_d{d}", grid=(cols // AT_COLS, nb),
        in_specs=[cur, cur, prev, cur, prev], out_specs=[cur, cur],
        out_shape=[jax.ShapeDtypeStruct((rows, cols), F32)] * 2,
        compiler_params=_params("parallel", "parallel"),
    )(qr, kr, kr, vr, vr)
    return o, lse


def _attn_bwd_dq(qr, kr, vr, do, lse, delta, d):
    rows, cols = qr.shape
    qb, cur, prev, nxt = _attn_specs(rows)
    nb = rows // (qb * ATT_BLK)

    def body(q_ref, kc_ref, kp_ref, vc_ref, vp_ref, do_ref, lse_ref, dl_ref, dq_ref):
        masks = {True: _band_mask(pl.program_id(1) > 0, True), False: _band_mask(True, True)}
        units = _units(qb)
        sms, dps = [], []
        for b, sl, h in units:
            own, own_b, neg = _own_lanes(ATT_BLK, h), _own_lanes(2 * ATT_BLK, h), _neg_pieces(2 * ATT_BLK, h)
            sms.append(_dot_nt(jnp.where(own, q_ref[_sub(b), sl], lse_ref[_sub(b), sl]),
                               jnp.where(own_b, _band_before(kc_ref, kp_ref, b, sl), neg)))
            dps.append(_dot_nt(jnp.where(own, do_ref[_sub(b), sl], dl_ref[_sub(b), sl]),
                               jnp.where(own_b, _band_before(vc_ref, vp_ref, b, sl), neg)))
        dss = [(jnp.exp(jnp.where(masks[b == 0], sm, NEG)) * dp).astype(MM)
               for sm, dp, (b, _, _) in zip(sms, dps, units)]
        dqs = [jnp.dot(ds, _band_before(kc_ref, kp_ref, b, sl), preferred_element_type=F32) * (AT_DH ** -0.5)
               for ds, (b, sl, _) in zip(dss, units)]
        for i in range(0, len(units), 2):
            b, sl, _ = units[i]
            dq_ref[_sub(b), sl] = jnp.where(_own_lanes(ATT_BLK, 0), dqs[i], dqs[i + 1]).astype(dq_ref.dtype)

    dq = pl.pallas_call(
        body, name=f"attn_bwd_dq_d{d}", grid=(cols // AT_COLS, nb),
        in_specs=[cur, cur, prev, cur, prev, cur, cur, cur], out_specs=cur,
        out_shape=jax.ShapeDtypeStruct((rows, cols), MM),
        compiler_params=_params("parallel", "parallel"),
    )(qr, kr, kr, vr, vr, do, lse, delta)
    return dq


def _attn_bwd_dkv(qr, kr, vr, do, lse, delta, d):
    rows, cols = qr.shape
    qb, cur, prev, nxt = _attn_specs(rows)
    nb = rows // (qb * ATT_BLK)

    def body(k_ref, v_ref, qc_ref, qn_ref, doc_ref, don_ref, lsec_ref, lsen_ref, dlc_ref, dln_ref,
             dk_ref, dv_ref):
        masks = {True: _band_mask(True, pl.program_id(1) < nb - 1), False: _band_mask(True, True)}
        units = _units(qb)
        sms, dps = [], []
        for b, sl, h in units:
            own, own_b, neg = _own_lanes(ATT_BLK, h), _own_lanes(2 * ATT_BLK, h), _neg_pieces(ATT_BLK, h)
            sms.append(_dot_nt(jnp.where(own, k_ref[_sub(b), sl], neg),
                               jnp.where(own_b, _band_after(qc_ref, qn_ref, b, sl),
                                         _band_after(lsec_ref, lsen_ref, b, sl))))
            dps.append(_dot_nt(jnp.where(own, v_ref[_sub(b), sl], neg),
                               jnp.where(own_b, _band_after(doc_ref, don_ref, b, sl),
                                         _band_after(dlc_ref, dln_ref, b, sl))))
        ps = [jnp.exp(jnp.where(masks[b == qb - 1], sm, NEG)) for sm, (b, _, _) in zip(sms, units)]
        dss = [(p * dp).astype(MM) for p, dp in zip(ps, dps)]
        dvs = [jnp.dot(p.astype(MM), _band_after(doc_ref, don_ref, b, sl), preferred_element_type=F32)
               for p, (b, sl, _) in zip(ps, units)]
        dks = [jnp.dot(ds, _band_after(qc_ref, qn_ref, b, sl), preferred_element_type=F32)
               for ds, (b, sl, _) in zip(dss, units)]
        head0 = _own_lanes(ATT_BLK, 0)
        for i in range(0, len(units), 2):
            b, sl, _ = units[i]
            dk_ref[_sub(b), sl] = jnp.where(head0, dks[i], dks[i + 1]).astype(dk_ref.dtype)
            dv_ref[_sub(b), sl] = jnp.where(head0, dvs[i], dvs[i + 1]).astype(dv_ref.dtype)

    dk, dv = pl.pallas_call(
        body, name=f"attn_bwd_dkv_d{d}", grid=(cols // AT_COLS, nb),
        in_specs=[cur, cur, cur, nxt, cur, nxt, cur, nxt, cur, nxt], out_specs=[cur, cur],
        out_shape=[jax.ShapeDtypeStruct((rows, cols), MM)] * 2,
        compiler_params=_params("parallel", "parallel"),
    )(kr, vr, qr, qr, do, do, lse, lse, delta, delta)
    return dk, dv


def _head_sum(a, width):
    parts = []
    for j in range(a.shape[1] // width):
        sm = jnp.sum(a[:, j * width:(j + 1) * width], axis=-1, keepdims=True)
        parts.append(jnp.broadcast_to(sm, (a.shape[0], width)))
    return jnp.concatenate(parts, axis=-1)


def _partner_pieces(x):
    xs = jnp.concatenate([pltpu.roll(x[:, j * 128:(j + 1) * 128], AT_DH, 1) for j in range(x.shape[1] // 128)],
                         axis=-1)
    hi = xs.astype(jnp.bfloat16).astype(F32)
    mid = (xs - hi).astype(jnp.bfloat16).astype(F32)
    lo = (xs - hi - mid).astype(jnp.bfloat16).astype(F32)
    lane = _iota2(x.shape, 1) % AT_DH
    return jnp.where(lane == 0, hi, jnp.where(lane == 1, mid, jnp.where(lane == 2, lo, 0.0)))


def _mid(x, tgt, proj, o_hg, o_at, lse_at, hg_norm_w, final_norm_w, wo_all, tm=256):
    s = x.shape[0]
    nb = s // tm

    def body(x_ref, t_ref, hgz_ref, atz_ref, ohg_ref, o1_ref, o2_ref, o3_ref, l1_ref, l2_ref, l3_ref,
             g_ref, fw_ref, wo_ref,
             dh_ref, dohg_ref, dhgz_ref, datz_ref, do1_ref, do2_ref, do3_ref, dl1_ref, dl2_ref, dl3_ref,
             lp1_ref, lp2_ref, lp3_ref,
             gwo_ref, gfw_ref, ghg_ref, loss_ref, nat_ref, stage_ref, gwo_acc):
        @pl.when(pl.program_id(0) == 0)
        def _():
            gwo_acc[...] = jnp.zeros_like(gwo_acc)
            gfw_ref[...] = jnp.zeros_like(gfw_ref)
            ghg_ref[...] = jnp.zeros_like(ghg_ref)
            loss_ref[...] = jnp.zeros_like(loss_ref)

        ohg, g = ohg_ref[...], g_ref[...]
        rs = lax.rsqrt(_head_sum(ohg * ohg, HG_D) * (1.0 / HG_D) + NORM_EPS)
        on = ohg * rs
        hgz = hgz_ref[...]
        sz = _sigmoid(hgz)
        gate_hg = hgz * sz
        lses, outs = [l1_ref[...]], [o1_ref[...]]
        for k, (d, l_ref, o_ref) in enumerate(zip(DILATIONS[1:], (l2_ref, l3_ref), (o2_ref, o3_ref))):
            _from_view(l_ref, nat_ref.at[2 * k], d)
            _from_view(o_ref, nat_ref.at[2 * k + 1], d)
            lses.append(_get_lanes(nat_ref.at[2 * k]))
            outs.append(_get_lanes(nat_ref.at[2 * k + 1]))
        mx = jnp.maximum(jnp.maximum(lses[0], lses[1]), lses[2])
        es = [jnp.exp(l - mx) for l in lses]
        den = es[0] + es[1] + es[2]
        ws = [e / den for e in es]
        oat = ws[0] * outs[0] + ws[1] * outs[1] + ws[2] * outs[2]
        atz = atz_ref[...]
        sa = _sigmoid(atz)
        gate_at = atz * sa
        mixed = jnp.concatenate([on * g * gate_hg, oat * gate_at], axis=-1).astype(MM)
        h = x_ref[...] + jnp.dot(mixed, wo_ref[...], preferred_element_type=F32)
        rstd = lax.rsqrt(jnp.mean(h * h, axis=-1, keepdims=True) + NORM_EPS)
        hn = h * rstd
        fw = fw_ref[...]
        err = hn * fw - t_ref[...]
        loss_ref[...] += 0.5 * jnp.sum(jnp.mean(err * err, axis=-1, keepdims=True), axis=0, keepdims=True)
        dout = err * (1.0 / D_MODEL)
        gfw_ref[...] += jnp.sum(dout * hn, axis=0, keepdims=True)
        dhn = dout * fw
        dh = rstd * (dhn - hn * jnp.mean(dhn * hn, axis=-1, keepdims=True))
        dh_ref[...] = dh
        dh_mm = dh.astype(MM)
        gwo_acc[...] += _dot_tn(mixed, dh_mm)

        @pl.when(pl.program_id(0) == nb - 1)
        def _():
            gwo_ref[...] = gwo_acc[...].astype(gwo_ref.dtype)

        dmixed = _dot_nt(dh_mm, wo_ref[...])
        dm_hg = dmixed[:, :SEC_W]
        d_ong = dm_hg * gate_hg
        dhgz_ref[...] = (dm_hg * (on * g) * (sz * (1.0 + hgz * (1.0 - sz)))).astype(MM)
        ghg_ref[...] += jnp.sum(d_ong * on, axis=0, keepdims=True)
        d_on = d_ong * g
        dohg_ref[...] = rs * (d_on - on * (_head_sum(d_on * on, HG_D) * (1.0 / HG_D)))
        dm_at = dmixed[:, SEC_W:]
        d_oat = dm_at * gate_at
        datz_ref[...] = (dm_at * oat * (sa * (1.0 + atz * (1.0 - sa)))).astype(MM)
        drow = _head_sum(d_oat * oat, AT_DH)
        lse_all = mx + jnp.log(den)
        for val, dst_refs in ((d_oat, (do1_ref, do2_ref, do3_ref)),
                              (_partner_pieces(drow), (dl1_ref, dl2_ref, dl3_ref)),
                              (_partner_pieces(lse_all), (lp1_ref, lp2_ref, lp3_ref))):
            _set_lanes(stage_ref, val)
            for d, dst_ref in zip(DILATIONS, dst_refs):
                _to_view(stage_ref, dst_ref, d)

    row = lambda w: pl.BlockSpec((tm, w), lambda i: (i, 0))
    sec = lambda j: pl.BlockSpec((None, tm, SEC_W), lambda i, j=j: (j, i, 0))
    const = lambda shp: pl.BlockSpec(shp, lambda i: (0,) * len(shp))
    half = row(SEC_W)
    views = [_view_spec(tm, d) for d in DILATIONS]
    return pl.pallas_call(
        body, name="mid", grid=(nb,),
        in_specs=[row(D_MODEL), row(D_MODEL), sec(PROJ_KEPT.index(3)), sec(PROJ_KEPT.index(7)), half] + views * 2
                 + [const((1, SEC_W)), const((1, D_MODEL)), const((D_MODEL, D_MODEL))],
        out_specs=[row(D_MODEL)] + [half] * 3 + views * 3
                  + [const((D_MODEL, D_MODEL)), const((1, D_MODEL)), const((1, SEC_W)), const((1, 1))],
        out_shape=[jax.ShapeDtypeStruct((s, D_MODEL), F32), jax.ShapeDtypeStruct((s, SEC_W), F32)]
                  + [jax.ShapeDtypeStruct((s, SEC_W), MM)] * 2
                  + [_view_shape(s, d, MM) for d in DILATIONS] * 3
                  + [jax.ShapeDtypeStruct((D_MODEL, D_MODEL), XCH), jax.ShapeDtypeStruct((1, D_MODEL), F32),
                     jax.ShapeDtypeStruct((1, SEC_W), F32), jax.ShapeDtypeStruct((1, 1), F32)],
        scratch_shapes=[pltpu.VMEM((4, LANE_GROUPS, tm, 128), F32), pltpu.VMEM((LANE_GROUPS, tm, 128), F32),
                        pltpu.VMEM((D_MODEL, D_MODEL), F32)],
        compiler_params=_params("arbitrary"),
    )(x, tgt, proj, proj, o_hg, *o_at, *lse_at, hg_norm_w, final_norm_w, wo_all)


def _section_specs(dsecs, tm):
    return [pl.BlockSpec((tm, SEC_W), lambda i: (i, 0)) if k is None
            else pl.BlockSpec((None, tm, SEC_W), lambda i, k=k: (k, i, 0)) for _, k in dsecs]


def _inproj_bwd_x(x, norm_w, w_all, dh, dsecs, token, tm=512):
    s = x.shape[0]

    def body(x_ref, nw_ref, w_ref, dh_ref, tok_ref, *refs):
        sec_refs, (gx_ref, gnw_ref) = refs[:N_SEC], refs[N_SEC:]

        @pl.when(pl.program_id(0) == 0)
        def _():
            gnw_ref[...] = jnp.zeros_like(gnw_ref)

        du = jnp.zeros((tm, D_MODEL), F32)
        for j in range(N_SEC):
            du = du + _dot_nt(sec_refs[j][...], w_ref[j])
        xv, nw = x_ref[...], nw_ref[...]
        rstd = lax.rsqrt(jnp.mean(xv * xv, axis=-1, keepdims=True) + NORM_EPS)
        xn = xv * rstd
        gnw_ref[...] += jnp.sum(du * xn, axis=0, keepdims=True)
        dxn = du * nw
        dx = rstd * (dxn - xn * jnp.mean(dxn * xn, axis=-1, keepdims=True))
        gx_ref[...] = (dh_ref[...] + tok_ref[0:1, 0:1]) + dx

    row = lambda w: pl.BlockSpec((tm, w), lambda i: (i, 0))
    const = lambda shp: pl.BlockSpec(shp, lambda i: (0,) * len(shp))
    return pl.pallas_call(
        body, name="inproj_bwd_x", grid=(s // tm,),
        in_specs=[row(D_MODEL), const((1, D_MODEL)), const((N_SEC, D_MODEL, SEC_W)), row(D_MODEL), const((8, 128))]
                 + _section_specs(dsecs, tm),
        out_specs=[row(D_MODEL), const((1, D_MODEL))],
        out_shape=[jax.ShapeDtypeStruct((s, D_MODEL), F32), jax.ShapeDtypeStruct((1, D_MODEL), F32)],
        compiler_params=_params("arbitrary"),
    )(x, norm_w, w_all, dh, token, *[a for a, _ in dsecs])


def _inproj_bwd_w(x, norm_w, dsec, dq_r, dk_r, dv, cos, sin, tm=512):
    s = x.shape[0]
    nb = s // tm

    def body(x_ref, nw_ref, s0, s1, s2, s3, s7, q1, q2, q3, k1, k2, k3, v1, v2, v3, cos_ref, sin_ref,
             gw_hbm, datt_ref, acc_ref, stage_ref, nat_ref):
        @pl.when(pl.program_id(0) == 0)
        def _():
            acc_ref[...] = jnp.zeros_like(acc_ref)

        def total(refs):
            acc = refs[0][...].astype(F32)
            for d, ref in zip(DILATIONS[1:], refs[1:]):
                _from_view(ref, nat_ref, d)
                acc = acc + _get_lanes(nat_ref)
            return acc

        c, sn = cos_ref[...], -sin_ref[...]
        unrot = lambda a: jnp.concatenate(
            [_rope128(a[:, j * 128:(j + 1) * 128], c, sn) for j in range(LANE_GROUPS)], axis=-1)
        att = [a.astype(MM) for a in (unrot(total((q1, q2, q3))), unrot(total((k1, k2, k3))), total((v1, v2, v3)))]
        for j, a in enumerate(att):
            datt_ref[j] = a
        xv = x_ref[...]
        rstd = lax.rsqrt(jnp.mean(xv * xv, axis=-1, keepdims=True) + NORM_EPS)
        u_t = (xv * rstd * nw_ref[...]).T.astype(MM)
        for j, dsj in enumerate((s0[...], s1[...], s2[...], s3[...], *att, s7[...])):
            acc_ref[j] += jnp.dot(u_t, dsj, preferred_element_type=F32)

        @pl.when(pl.program_id(0) == nb - 1)
        def _():
            for j in range(N_SEC):
                stage_ref[...] = acc_ref[j].astype(stage_ref.dtype)
                pltpu.sync_copy(stage_ref, gw_hbm.at[j])

    row = lambda w: pl.BlockSpec((tm, w), lambda i: (i, 0))
    return pl.pallas_call(
        body, name="inproj_bwd_w", grid=(nb,),
        in_specs=[row(D_MODEL), pl.BlockSpec((1, D_MODEL), lambda i: (0, 0))] + [row(SEC_W)] * 5
                 + [_view_spec(tm, d) for d in DILATIONS] * 3 + [row(128), row(128)],
        out_specs=[pl.BlockSpec(memory_space=pl.ANY), pl.BlockSpec((3, tm, SEC_W), lambda i: (0, i, 0))],
        out_shape=[jax.ShapeDtypeStruct((N_SEC, D_MODEL, SEC_W), XCH), jax.ShapeDtypeStruct((3, s, SEC_W), MM)],
        scratch_shapes=[pltpu.VMEM((N_SEC, D_MODEL, SEC_W), F32), pltpu.VMEM((D_MODEL, SEC_W), XCH),
                        pltpu.VMEM((LANE_GROUPS, tm, 128), F32)],
        compiler_params=_params("arbitrary"),
    )(x, norm_w, *dsec, *dq_r, *dk_r, *dv, cos, sin)


def _local_step(x, tgt, norm_w, w_all, lb_logits, hg_norm_w, wo_all, final_norm_w, on_weight_grads):
    s = x.shape[0]
    cos, sin = _rope_tables(s)
    proj, *qkv = _inproj_fwd(x, norm_w, w_all, cos, sin)
    o_hg, sst = _hgrn_fwd(proj, lb_logits)
    qkv = [qkv[3 * i:3 * i + 3] for i in range(len(DILATIONS))]
    att = [_attn_fwd(*qkv_d, d) for qkv_d, d in zip(qkv, DILATIONS)]
    (dh, d_ohg, d_hgz, d_atz, do1, do2, do3, dl1, dl2, dl3, lp1, lp2, lp3, gwo, gfw, ghg, loss) = _mid(
        x, tgt, proj, o_hg, [a[0] for a in att], [a[1] for a in att], hg_norm_w, final_norm_w[None, :], wo_all)
    dxq, dxf, dxi, dlb = _hgrn_bwd(proj, lb_logits, d_ohg, sst)
    dq_r, dk_r, dv = [], [], []
    for d, qkv_d, do, lp, dl in zip(DILATIONS, qkv, (do1, do2, do3), (lp1, lp2, lp3), (dl1, dl2, dl3)):
        dq_r.append(_attn_bwd_dq(*qkv_d, do, lp, dl, d))
        dk_d, dv_d = _attn_bwd_dkv(*qkv_d, do, lp, dl, d)
        dk_r.append(dk_d)
        dv.append(dv_d)
    gwi, d_att = _inproj_bwd_w(x, norm_w, (dxq, dxf, dxi, d_hgz, d_atz), dq_r, dk_r, dv, cos, sin)
    dsecs = [(dxq, None), (dxf, None), (dxi, None), (d_hgz, None), (d_att, 0), (d_att, 1), (d_att, 2), (d_atz, None)]
    token = on_weight_grads(gwi, gwo)
    gx, gnw = _inproj_bwd_x(x, norm_w, w_all, dh, dsecs, token)
    small = jnp.concatenate([gnw, jnp.concatenate([dlb, ghg], axis=-1), gfw,
                             jnp.pad(loss, ((0, 0), (0, D_MODEL - 1)))], axis=0)
    return gx, gwi, gwo, small


def _coords():
    return lax.axis_index("x"), lax.axis_index("y"), lax.axis_index("c")


def _gather_weights(w_in, w_out):
    wo_rows = w_out.shape[0]

    def body(wi_ref, wo_ref, wi_all, wo_all, send_sems, recv_sems):
        x, y, c = _coords()
        me, sibling = (x, y, c), (x, y, 1 - c)
        chips = [(1 - x, y), (x, 1 - y), (1 - x, 1 - y)]
        slot = lambda p: 4 * p[0] + 2 * p[1] + p[2]

        def copies(k, block, to):
            return [pltpu.make_async_remote_copy(
                src_ref=ref.at[slot(block)], dst_ref=ref.at[slot(block)], send_sem=send_sems.at[a, k],
                recv_sem=recv_sems.at[a, k], device_id=to, device_id_type=MESH)
                for a, ref in enumerate((wi_all, wo_all))]

        wi_all[slot(me)] = wi_ref[...].astype(MM)
        wo_all[slot(me)] = wo_ref[...].astype(MM)
        first = copies(0, me, sibling)
        for j, chip in enumerate(chips):
            first += copies(1 + j, me, (*chip, c))
        for cp in first:
            cp.start()
        passed = []
        for j, chip in enumerate(chips):
            for cp in copies(1 + j, (*chip, c), me):
                cp.wait_recv()
            fwd = copies(4 + j, (*chip, c), sibling)
            for cp in fwd:
                cp.start()
            passed += fwd
        for cp in copies(0, sibling, me):
            cp.wait_recv()
        for j, chip in enumerate(chips):
            for cp in copies(4 + j, (*chip, 1 - c), me):
                cp.wait_recv()
        for cp in first + passed:
            cp.wait_send()

    vmem = pl.BlockSpec(memory_space=pltpu.VMEM)
    return pl.pallas_call(
        body, name="gather_weights",
        in_specs=[vmem, vmem], out_specs=[vmem, vmem],
        out_shape=[jax.ShapeDtypeStruct((N_DEV, D_MODEL, SEC_W), MM),
                   jax.ShapeDtypeStruct((N_DEV, wo_rows, D_MODEL), MM)],
        scratch_shapes=[pltpu.SemaphoreType.DMA((2, 7)), pltpu.SemaphoreType.DMA((2, 7))],
        compiler_params=pltpu.CompilerParams(vmem_limit_bytes=VMEM_LIMIT),
    )(w_in, w_out)


def _me():
    x, y, c = _coords()
    return 4 * x + 2 * y + c


def _grad_copies(srcs, lands, send_sems, recv_sems):
    x, y, c = _coords()
    me = 4 * x + 2 * y + c
    copies = []
    for k in range(1, N_DEV):
        px, py, pc = x ^ (k >> 2), y ^ ((k >> 1) & 1), c ^ (k & 1)
        peer = 4 * px + 2 * py + pc
        for a, (src, dst) in enumerate(zip(srcs, lands)):
            copies.append(pltpu.make_async_remote_copy(
                src_ref=src.at[peer], dst_ref=dst.at[me], send_sem=send_sems.at[a * (N_DEV - 1) + k - 1],
                recv_sem=recv_sems.at[a * (N_DEV - 1) + k - 1], device_id=(px, py, pc), device_id_type=MESH))
    return copies


HBM_SPEC = pl.BlockSpec(memory_space=pltpu.HBM)
SEM_SPEC = pl.BlockSpec(memory_space=pltpu.SEMAPHORE)
SPLIT_COPY_EFFECT = pltpu.SideEffectType.DATAFLOW_SIDE_EFFECTING


def _exchange_start(gwi, gwo):
    def body(gwi_ref, gwo_ref, li_ref, lo_ref, send_sems, recv_sems, gwi_thru, gwo_thru, li_thru, lo_thru, token):
        for cp in _grad_copies((gwi_ref, gwo_ref), (li_ref, lo_ref), send_sems, recv_sems):
            cp.start()
        token[...] = jnp.zeros_like(token)

    hbm = lambda a: pltpu.with_memory_space_constraint(a, pltpu.HBM)
    bufs = (gwi, gwo, lax.empty(gwi.shape, gwi.dtype), lax.empty(gwo.shape, gwo.dtype))
    return pl.pallas_call(
        body, name="exchange_start",
        out_shape=(pltpu.SemaphoreType.DMA((2 * (N_DEV - 1),)), pltpu.SemaphoreType.DMA((2 * (N_DEV - 1),)),
                   *[pltpu.HBM(a.shape, a.dtype) for a in bufs], jax.ShapeDtypeStruct((8, 128), F32)),
        in_specs=[HBM_SPEC] * 4,
        out_specs=(SEM_SPEC, SEM_SPEC, HBM_SPEC, HBM_SPEC, HBM_SPEC, HBM_SPEC, pl.BlockSpec(memory_space=pltpu.VMEM)),
        input_output_aliases={0: 2, 1: 3, 2: 4, 3: 5},
        compiler_params=pltpu.CompilerParams(has_side_effects=SPLIT_COPY_EFFECT),
    )(*[hbm(a) for a in bufs])


def _exchange_wait(send_sems, recv_sems, gwi, gwo, li, lo, after):
    def body(gwi_ref, gwo_ref, li_ref, lo_ref, send_sems, recv_sems, after_ref, gwi_out, gwo_out, li_out, lo_out):
        for cp in _grad_copies((gwi_ref, gwo_ref), (li_ref, lo_ref), send_sems, recv_sems):
            cp.wait_send()
            cp.wait_recv()

    return pl.pallas_call(
        body, name="exchange_wait",
        out_shape=tuple(pltpu.HBM(a.shape, a.dtype) for a in (gwi, gwo, li, lo)),
        in_specs=[HBM_SPEC] * 4 + [SEM_SPEC, SEM_SPEC, pl.BlockSpec(memory_space=pl.ANY)],
        out_specs=(HBM_SPEC,) * 4,
        input_output_aliases={0: 0, 1: 1, 2: 2, 3: 3},
        compiler_params=pltpu.CompilerParams(has_side_effects=SPLIT_COPY_EFFECT),
    )(gwi, gwo, li, lo, send_sems, recv_sems, after)


def _gather_small(small):
    def body(sm_ref, ls_ref, send_sems, recv_sems, local_sem):
        x, y, c = _coords()
        me = 4 * x + 2 * y + c
        own = pltpu.make_async_copy(sm_ref, ls_ref.at[me], local_sem)
        own.start()
        sends = []
        for k in range(1, N_DEV):
            peer = (x ^ (k >> 2), y ^ ((k >> 1) & 1), c ^ (k & 1))
            sends.append(pltpu.make_async_remote_copy(
                src_ref=sm_ref, dst_ref=ls_ref.at[me], send_sem=send_sems.at[k - 1], recv_sem=recv_sems.at[k - 1],
                device_id=peer, device_id_type=MESH))
        for cp in sends:
            cp.start()
        for cp in sends:
            cp.wait_recv()
        for cp in sends:
            cp.wait_send()
        own.wait()

    vmem = pl.BlockSpec(memory_space=pltpu.VMEM)
    return pl.pallas_call(
        body, name="gather_small", in_specs=[vmem], out_specs=vmem,
        out_shape=jax.ShapeDtypeStruct((N_DEV,) + small.shape, F32),
        scratch_shapes=[pltpu.SemaphoreType.DMA((N_DEV - 1,)), pltpu.SemaphoreType.DMA((N_DEV - 1,)),
                        pltpu.SemaphoreType.DMA],
    )(small)


def _adamw(w, g, m, v):
    m = ADAM_B1 * m + (1.0 - ADAM_B1) * g
    v = ADAM_B2 * v + (1.0 - ADAM_B2) * (g * g)
    m_hat = m / (1.0 - ADAM_B1 ** ADAM_STEP)
    v_hat = v / (1.0 - ADAM_B2 ** ADAM_STEP)
    return -ADAM_LR * (m_hat / (jnp.sqrt(v_hat) + ADAM_EPS) + ADAM_WD * w), m, v


def _slot_sum(ref, own=None, me=None):
    g = None
    for i in range(N_DEV):
        term = ref[i].astype(F32)
        if own is not None:
            term = jnp.where(i == me, own, term)
        g = term if g is None else g + term
    return g


def _update_matrix(name, me, landed, own, w, m, v, rows):
    r, c = w.shape

    def body(me_ref, l_ref, own_ref, w_ref, m_ref, v_ref, g_ref, d_ref, nm_ref, nv_ref):
        g = _slot_sum(l_ref, own_ref[...].astype(F32), me_ref[0])
        g_ref[...] = g
        d_ref[...], nm_ref[...], nv_ref[...] = _adamw(w_ref[...], g, m_ref[...], v_ref[...])

    blk = pl.BlockSpec((rows, c), lambda i, me_ref: (i, 0))
    return pl.pallas_call(
        body, name=name,
        grid_spec=pltpu.PrefetchScalarGridSpec(
            num_scalar_prefetch=1, grid=(r // rows,),
            in_specs=[pl.BlockSpec((N_DEV, rows, c), lambda i, me_ref: (0, i, 0)),
                      pl.BlockSpec((None, rows, c), lambda i, me_ref: (me_ref[0], i, 0)), blk, blk, blk],
            out_specs=[blk] * 4),
        out_shape=[jax.ShapeDtypeStruct((r, c), F32)] * 4,
        compiler_params=_params("parallel"),
    )(me, landed, own, w, m, v)


def _update_small(landed, lb_logits, ws, ms, vs):
    def body(l_ref, lbl_ref, w_ref, m_ref, v_ref, g_ref, d_ref, nm_ref, nv_ref, loss_ref):
        tot = _slot_sum(l_ref)
        _, dlb = _lower_bound(lbl_ref[...])
        g_lb = tot[1:2, :SEC_W] * dlb
        g = jnp.concatenate([tot[0:1], jnp.concatenate([g_lb, -g_lb], axis=-1),
                             jnp.pad(tot[1:2, SEC_W:], ((0, 0), (0, SEC_W))), tot[2:3]], axis=0)
        g_ref[...] = g
        d_ref[...], nm_ref[...], nv_ref[...] = _adamw(w_ref[...], g, m_ref[...], v_ref[...])
        loss_ref[...] = tot[3:4, 0:1]

    vmem = pl.BlockSpec(memory_space=pltpu.VMEM)
    return pl.pallas_call(
        body, name="update_small", in_specs=[vmem] * 5, out_specs=[vmem] * 5,
        out_shape=[jax.ShapeDtypeStruct((4, D_MODEL), F32)] * 4 + [jax.ShapeDtypeStruct((1, 1), F32)],
    )(landed, lb_logits, ws, ms, vs)


def _pack_small(norm_w, lb_logits, hg_norm_w, final_norm_w):
    return jnp.concatenate([norm_w, lb_logits.reshape(1, D_MODEL),
                            jnp.pad(hg_norm_w, ((0, 0), (0, D_MODEL - SEC_W))), final_norm_w[None, :]], axis=0)


def _unpack_small(a):
    return a[0:1], a[1].reshape(2, SEC_W), a[2:3, :SEC_W], a[3]


def kernel(x, norm_w, w_in, hgrn_lb_logits, hg_norm_w, w_out, final_norm_w, loss_target, m_norm_w, m_w_in, m_hgrn_lb_logits, m_hg_norm_w, m_w_out, m_final_norm_w, v_norm_w, v_w_in, v_hgrn_lb_logits, v_hg_norm_w, v_w_out, v_final_norm_w):
    w_all, wo_all = _gather_weights(w_in[0], w_out[0])
    in_flight = []

    def start_exchange(gwi, gwo):
        *handles, token = _exchange_start(gwi, gwo.reshape(N_DEV, D_MODEL // N_DEV, D_MODEL))
        in_flight.extend(handles)
        return token

    gx, _, _, small = _local_step(x[0], loss_target[0], norm_w, w_all, hgrn_lb_logits, hg_norm_w,
                                  wo_all.reshape(D_MODEL, D_MODEL), final_norm_w, start_exchange)
    ls = _gather_small(small)
    gwi, gwo, li, lo = _exchange_wait(*in_flight, gx)
    me = _me().astype(jnp.int32).reshape(1)
    g_wi, d_wi, nm_wi, nv_wi = _update_matrix("update_w_in", me, li, gwi, w_in[0], m_w_in[0], v_w_in[0], 256)
    g_wo, d_wo, nm_wo, nv_wo = _update_matrix("update_w_out", me, lo, gwo, w_out[0], m_w_out[0], v_w_out[0], 128)
    g_s, d_s, nm_s, nv_s, loss = _update_small(
        ls, hgrn_lb_logits, _pack_small(norm_w, hgrn_lb_logits, hg_norm_w, final_norm_w),
        _pack_small(m_norm_w, m_hgrn_lb_logits, m_hg_norm_w, m_final_norm_w),
        _pack_small(v_norm_w, v_hgrn_lb_logits, v_hg_norm_w, v_final_norm_w))
    outs = []
    for small_out, wi, wo in ((g_s, g_wi, g_wo), (d_s, d_wi, d_wo), (nm_s, nm_wi, nm_wo), (nv_s, nv_wi, nv_wo)):
        nw, lb, hg, fw = _unpack_small(small_out)
        outs += [nw, wi[None], lb, hg, wo[None], fw]
    return (loss[0, 0], gx[None], *outs)
```

```python
import functools

import jax
import jax.numpy as jnp
import numpy as np
from jax import lax
from jax.experimental import pallas as pl
from jax.experimental.pallas import tpu as pltpu

F32 = jnp.float32
MM = jnp.bfloat16
XCH = jnp.bfloat16
NORM_EPS = 1e-6
NEG = -1e30
N_DEV = 8
D_MODEL = 1024
N_SEC = 8
SEC_W = 512
HG_HEADS = 4
HG_D = 128
HG_GROUP = 4
AT_DH = 64
LANES = 128
ATT_BLK = 128
AT_COLS = 512
AT_QB = 8
DILATIONS = (1, 4, 16)
ROPE_THETA = 10000.0
CH = 16
LB_LO, LB_HI = 1e-6, 1.0 - 1e-6
ADAM_LR, ADAM_B1, ADAM_B2, ADAM_EPS, ADAM_WD, ADAM_STEP = 0.001, 0.9, 0.999, 1e-08, 0.01, 10
VMEM_LIMIT = 56 * 1024 * 1024
MESH = pl.DeviceIdType.MESH


def _params(*sem):
    return pltpu.CompilerParams(dimension_semantics=sem, vmem_limit_bytes=VMEM_LIMIT)


def _sigmoid(x):
    return 1.0 / (1.0 + jnp.exp(-x))


def _dot(a, b):
    return jnp.dot(a.astype(MM), b.astype(MM), preferred_element_type=F32)


def _dot_nt(a, b):
    return lax.dot_general(a.astype(MM), b.astype(MM), (((1,), (1,)), ((), ())), preferred_element_type=F32)


def _dot_tn(a, b):
    return lax.dot_general(a.astype(MM), b.astype(MM), (((0,), (0,)), ((), ())), preferred_element_type=F32)


def _tri_dot(tri, g):
    g1 = g.astype(jnp.bfloat16)
    r1 = g - g1.astype(F32)
    g2 = r1.astype(jnp.bfloat16)
    g3 = (r1 - g2.astype(F32)).astype(jnp.bfloat16)
    t = tri.astype(jnp.bfloat16)
    d = functools.partial(jnp.dot, preferred_element_type=F32)
    return d(t, g1) + d(t, g2) + d(t, g3)


def _lower_bound(lbl):
    l0, l1 = lbl[0:1, :], lbl[1:2, :]
    m = jnp.maximum(l0, l1)
    e0, e1 = jnp.exp(l0 - m), jnp.exp(l1 - m)
    p = e0 / (e0 + e1)
    inside = (p >= LB_LO) & (p <= LB_HI)
    return jnp.clip(p, LB_LO, LB_HI), jnp.where(inside, p * (e1 / (e0 + e1)), 0.0)


def _iota2(shape, dim):
    return lax.broadcasted_iota(jnp.int32, shape, dim)


def _hgrn_gates(xq, xf, lb):
    sgq = _sigmoid(xq)
    sg = _sigmoid(xf)
    sn = _sigmoid(-xf)
    f = lb + (1.0 - lb) * sg
    return sgq, xq * sgq, sg, sn, f, (1.0 - lb) * sn


def _bdot(a, b, ca, cb):
    return lax.dot_general(a.astype(MM), b.astype(MM), (((ca,), (cb,)), ((0,), (0,))), preferred_element_type=F32)


def _chunk_masks(rb):
    row, col = _iota2((rb, rb), 0), _iota2((rb, rb), 1)
    same = (row // CH) == (col // CH)
    return same & (row >= col), same & (row <= col)


def _hgrn_fwd(proj, lb_logits, rb=256):
    s = proj.shape[1]
    nb, nc = s // rb, rb // CH

    def body(q_ref, f_ref, i_ref, lbl_ref, o_ref, sst_ref, st_ref, slab_ref, states_ref):
        @pl.when(pl.program_id(1) == 0)
        def _():
            st_ref[...] = jnp.zeros_like(st_ref)

        sst_ref[...] = st_ref[...]
        prefix, _ = _chunk_masks(rb)
        c3 = lambda a: a.reshape(nc, CH, HG_D)
        row, col = _iota2((nc, CH, CH), 1), _iota2((nc, CH, CH), 2)
        heads = []
        for g in range(HG_GROUP):
            hs = slice(g * HG_D, (g + 1) * HG_D)
            lb, _ = _lower_bound(lbl_ref[:, hs])
            _, q, _, _, f, kk = _hgrn_gates(q_ref[:, hs], f_ref[:, hs], lb)
            b3 = c3(_tri_dot(prefix, jnp.log(f)))
            q3, kk3, v3 = c3(q), c3(kk), c3(i_ref[:, hs])
            bl3 = b3[:, CH - 1:CH, :]
            for t in range(CH):
                slab_ref[g, :, t * CH:(t + 1) * CH, :] = (
                    q3 * jnp.exp(jnp.minimum(b3 - b3[:, t:t + 1, :], 0.0))).astype(MM)
            x_upd = _bdot(v3, kk3 * jnp.exp(bl3 - b3), 1, 1)
            heads.append(dict(hs=hs, kk3=kk3, v3=v3, qe3=q3 * jnp.exp(b3), ebl3=jnp.exp(bl3), x_upd=x_upd))
        for g, hd in enumerate(heads):
            st = st_ref[g]
            for c in range(nc):
                states_ref[g, c] = st
                st = st * hd["ebl3"][c] + hd["x_upd"][c]
            st_ref[g] = st
        for g, hd in enumerate(heads):
            r = _bdot(slab_ref[g], hd["kk3"], 2, 2)
            a = jnp.zeros((nc, CH, CH), F32)
            for t in range(CH):
                a = a + jnp.where(col == t, r[:, t * CH:(t + 1) * CH, :], 0.0)
            a = jnp.where(row >= col, a, 0.0)
            o3 = _bdot(hd["qe3"], states_ref[g], 2, 2) + _bdot(a, hd["v3"], 2, 1)
            o_ref[:, hd["hs"]] = o3.reshape(rb, HG_D)

    wide = HG_GROUP * HG_D
    sec = lambda j: pl.BlockSpec((None, rb, wide), lambda h, i, j=j: (j, i, h))
    return pl.pallas_call(
        body, name="hgrn_fwd", grid=(HG_HEADS // HG_GROUP, nb),
        in_specs=[sec(0), sec(1), sec(2), pl.BlockSpec((2, wide), lambda h, i: (0, h))],
        out_specs=[pl.BlockSpec((rb, wide), lambda h, i: (i, h)),
                   pl.BlockSpec((None, HG_GROUP, HG_D, HG_D), lambda h, i: (i, h, 0, 0))],
        out_shape=[jax.ShapeDtypeStruct((s, SEC_W), F32),
                   jax.ShapeDtypeStruct((nb, HG_HEADS, HG_D, HG_D), F32)],
        scratch_shapes=[pltpu.VMEM((HG_GROUP, HG_D, HG_D), F32), pltpu.VMEM((HG_GROUP, nc, CH * CH, HG_D), MM),
                        pltpu.VMEM((HG_GROUP, nc, HG_D, HG_D), F32)],
        compiler_params=_params("parallel", "arbitrary"),
    )(proj, proj, proj, lb_logits)


def _hgrn_bwd(proj, lb_logits, d_o, sst, rb=256):
    s = proj.shape[1]
    nb, nc = s // rb, rb // CH

    def body(q_ref, f_ref, i_ref, lbl_ref, do_ref, sst_ref, dxq_ref, dxf_ref, dxi_ref, dlb_ref,
             dst_ref, states_ref, dstates_ref, lslab_ref, kslab_ref):
        @pl.when(pl.program_id(1) == 0)
        def _():
            dst_ref[...] = jnp.zeros_like(dst_ref)
            dlb_ref[...] = jnp.zeros_like(dlb_ref)

        prefix, suffix = _chunk_masks(rb)
        c3 = lambda a: a.reshape(nc, CH, HG_D)
        flat = lambda a: a.reshape(rb, HG_D)
        row, col = _iota2((nc, CH, CH), 1), _iota2((nc, CH, CH), 2)
        tril, triu = row >= col, row <= col
        sel = (_iota2((CH, CH * CH), 1) % CH == _iota2((CH, CH * CH), 0)).astype(MM)
        blockdiag = _iota2((nc, CH, CH * CH), 2) // CH == _iota2((nc, CH, CH * CH), 1)
        tile = lambda m: jnp.where(blockdiag, _dot(m.reshape(rb, CH), sel).reshape(nc, CH, CH * CH), 0.0)
        last = _iota2((nc, CH, HG_D), 1) == CH - 1
        heads = []
        for g in range(HG_GROUP):
            hs = slice(g * HG_D, (g + 1) * HG_D)
            lb, _ = _lower_bound(lbl_ref[:, hs])
            xq = q_ref[:, hs]
            sgq, q, sg, sn, f, kk = _hgrn_gates(xq, f_ref[:, hs], lb)
            b3 = c3(_tri_dot(prefix, jnp.log(f)))
            q3, kk3, v3, do3 = c3(q), c3(kk), c3(i_ref[:, hs]), c3(do_ref[:, hs])
            bl3 = b3[:, CH - 1:CH, :]
            eb3, ebl3, dec3 = jnp.exp(b3), jnp.exp(bl3), jnp.exp(bl3 - b3)
            qe3, kd3 = q3 * eb3, kk3 * dec3
            x_upd, y_upd = _bdot(v3, kd3, 1, 1), _bdot(do3, qe3, 1, 1)
            for t in range(CH):
                bt = b3[:, t:t + 1, :]
                lslab_ref[g, :, t * CH:(t + 1) * CH, :] = (q3 * jnp.exp(jnp.minimum(b3 - bt, 0.0))).astype(MM)
                kslab_ref[g, :, t * CH:(t + 1) * CH, :] = (kk3 * jnp.exp(jnp.minimum(bt - b3, 0.0))).astype(MM)
            d_a = jnp.where(tril, _bdot(do3, v3, 2, 2), 0.0)
            d_at = jnp.where(triu, _bdot(v3, do3, 2, 2), 0.0)
            heads.append(dict(hs=hs, lb=lb, xq=xq, sgq=sgq, sg=sg, sn=sn, f=f, q3=q3, kk3=kk3, v3=v3, do3=do3,
                              eb3=eb3, ebl3=ebl3, dec3=dec3, qe3=qe3, kd3=kd3, x_upd=x_upd, y_upd=y_upd,
                              d_a=d_a, d_at=d_at))
        for g, hd in enumerate(heads):
            st = sst_ref[g]
            for c in range(nc):
                states_ref[g, c] = st
                st = st * hd["ebl3"][c] + hd["x_upd"][c]
            dst = dst_ref[g]
            for c in reversed(range(nc)):
                dstates_ref[g, c] = dst
                dst = dst * hd["ebl3"][c] + hd["y_upd"][c]
            dst_ref[g] = dst
        for g, hd in enumerate(heads):
            q3, v3, do3, kd3 = hd["q3"], hd["v3"], hd["do3"], hd["kd3"]
            states, dstates = states_ref[g], dstates_ref[g]
            hd["dqe"] = _bdot(do3, states, 2, 1)
            hd["dkd"] = _bdot(v3, dstates, 2, 1)
            r = _bdot(kslab_ref[g], q3, 2, 2)
            a_t = jnp.zeros((nc, CH, CH), F32)
            for t in range(CH):
                a_t = a_t + jnp.where(col == t, r[:, t * CH:(t + 1) * CH, :], 0.0)
            a_t = jnp.where(triu, a_t, 0.0)
            hd["dv"] = _bdot(kd3, dstates, 2, 2) + _bdot(a_t, do3, 2, 1)
            hd["dq_in"] = _bdot(tile(hd["d_a"]), kslab_ref[g], 2, 1)
            hd["dk_in"] = _bdot(tile(hd["d_at"]), lslab_ref[g], 2, 1)
            hd["ss"] = jnp.sum(dstates * states, axis=1, keepdims=True)
        for g, hd in enumerate(heads):
            q3, kk3, eb3, ebl3, dec3, qe3, kd3 = (hd[k] for k in ("q3", "kk3", "eb3", "ebl3", "dec3", "qe3", "kd3"))
            dqe, dkd, dq_in, dk_in = hd["dqe"], hd["dkd"], hd["dq_in"], hd["dk_in"]
            dkd_kd = dkd * kd3
            db = dqe * qe3 - dkd_kd + q3 * dq_in - kk3 * dk_in
            dbl = jnp.sum(dkd_kd, axis=1, keepdims=True) + hd["ss"] * ebl3
            dg = _tri_dot(suffix, flat(db + jnp.where(last, dbl, 0.0)))
            df = dg / hd["f"] - flat(dkd * dec3 + dk_in)
            xq, sgq, hs = hd["xq"], hd["sgq"], hd["hs"]
            dxq_ref[:, hs] = (flat(dqe * eb3 + dq_in) * (sgq * (1.0 + xq * (1.0 - sgq)))).astype(MM)
            dxf_ref[:, hs] = (df * (1.0 - hd["lb"]) * hd["sg"] * hd["sn"]).astype(MM)
            dxi_ref[:, hs] = flat(hd["dv"]).astype(MM)
            dlb_ref[:, hs] += jnp.sum(df * hd["sn"], axis=0, keepdims=True)

    wide = HG_GROUP * HG_D
    rev = lambda i: nb - 1 - i
    sec = lambda j: pl.BlockSpec((None, rb, wide), lambda h, i, j=j: (j, rev(i), h))
    blk = pl.BlockSpec((rb, wide), lambda h, i: (rev(i), h))
    state = (pltpu.VMEM((HG_GROUP, nc, HG_D, HG_D), F32), pltpu.VMEM((HG_GROUP, nc, CH * CH, HG_D), MM))
    return pl.pallas_call(
        body, name="hgrn_bwd", grid=(HG_HEADS // HG_GROUP, nb),
        in_specs=[sec(0), sec(1), sec(2), pl.BlockSpec((2, wide), lambda h, i: (0, h)), blk,
                  pl.BlockSpec((None, HG_GROUP, HG_D, HG_D), lambda h, i: (rev(i), h, 0, 0))],
        out_specs=[blk, blk, blk, pl.BlockSpec((1, wide), lambda h, i: (0, h))],
        out_shape=[jax.ShapeDtypeStruct((s, SEC_W), MM)] * 3 + [jax.ShapeDtypeStruct((1, SEC_W), F32)],
        scratch_shapes=[pltpu.VMEM((HG_GROUP, HG_D, HG_D), F32), state[0], state[0], state[1], state[1]],
        compiler_params=_params("parallel", "arbitrary"),
    )(proj, proj, proj, lb_logits, d_o, sst)


def _rope_tables(s):
    half = AT_DH // 2
    inv_freq = np.float32(1.0) / (np.float32(ROPE_THETA) ** (np.arange(half, dtype=np.float32) / np.float32(half)))
    ang = np.arange(s, dtype=np.float32)[:, None] * inv_freq[None, :]
    cos, sin = np.cos(ang), np.sin(ang)
    return np.concatenate([cos] * 4, axis=-1), np.concatenate([-sin, sin] * 2, axis=-1)


def _rope128(x, cos, sin):
    lo = (_iota2(x.shape, 1) % AT_DH) < AT_DH // 2
    rot = jnp.where(lo, pltpu.roll(x, LANES - AT_DH // 2, 1), pltpu.roll(x, AT_DH // 2, 1))
    return x * cos + rot * sin


LANE_GROUPS = SEC_W // LANES


def _set_lanes(ref, val):
    for j in range(LANE_GROUPS):
        ref[j] = val[:, j * LANES:(j + 1) * LANES]


def _get_lanes(ref):
    return jnp.concatenate([ref[j] for j in range(LANE_GROUPS)], axis=-1)


def _to_view(src_ref, dst_ref, d):
    n = src_ref.shape[1] // d
    for r in range(d):
        rows = pl.ds(r, n, stride=d) if d > 1 else slice(None)
        for j in range(LANE_GROUPS):
            c0 = r * SEC_W + j * LANES
            dst_ref[:, c0:c0 + LANES] = src_ref.at[j][rows, :].astype(dst_ref.dtype)


def _from_view(src_ref, dst_ref, d):
    n = dst_ref.shape[1] // d
    for r in range(d):
        for j in range(LANE_GROUPS):
            c0 = r * SEC_W + j * LANES
            dst_ref.at[j][pl.ds(r, n, stride=d), :] = src_ref[:, c0:c0 + LANES].astype(dst_ref.dtype)


def _view_spec(tm, d):
    return pl.BlockSpec((tm // d, d * SEC_W), lambda i: (i, 0))


def _view_shape(s, d, dtype):
    return jax.ShapeDtypeStruct((s // d, d * SEC_W), dtype)


PROJ_KEPT = (0, 1, 2, 3, 7)


def _inproj_fwd(x, norm_w, w_all, cos, sin, tm=512):
    s = x.shape[0]

    def body(x_ref, nw_ref, w_ref, cos_ref, sin_ref, proj_ref, *refs):
        outs, (qs_ref, ks_ref, vs_ref) = refs[:-3], refs[-3:]
        xv = x_ref[...]
        rstd = lax.rsqrt(jnp.mean(xv * xv, axis=-1, keepdims=True) + NORM_EPS)
        u = (xv * rstd * nw_ref[...]).astype(MM)
        for slot, j in enumerate(PROJ_KEPT):
            proj_ref[slot] = jnp.dot(u, w_ref[j], preferred_element_type=F32)
        q, k, v = [jnp.dot(u, w_ref[j], preferred_element_type=F32) for j in (4, 5, 6)]
        c, sn = cos_ref[...], sin_ref[...]
        for g in range(LANE_GROUPS):
            sl = slice(g * LANES, (g + 1) * LANES)
            qs_ref[g] = _rope128(q[:, sl], c, sn) * (AT_DH ** -0.5)
            ks_ref[g] = _rope128(k[:, sl], c, sn)
            vs_ref[g] = v[:, sl]
        for i, d in enumerate(DILATIONS):
            for src_ref, dst_ref in zip((qs_ref, ks_ref, vs_ref), outs[3 * i:3 * i + 3]):
                _to_view(src_ref, dst_ref, d)

    tab = pl.BlockSpec((tm, LANES), lambda i: (i, 0))
    return pl.pallas_call(
        body, name="inproj_fwd", grid=(s // tm,),
        in_specs=[pl.BlockSpec((tm, D_MODEL), lambda i: (i, 0)),
                  pl.BlockSpec((1, D_MODEL), lambda i: (0, 0)),
                  pl.BlockSpec((N_SEC, D_MODEL, SEC_W), lambda i: (0, 0, 0)), tab, tab],
        out_specs=[pl.BlockSpec((len(PROJ_KEPT), tm, SEC_W), lambda i: (0, i, 0))]
                  + [_view_spec(tm, d) for d in DILATIONS for _ in range(3)],
        out_shape=[jax.ShapeDtypeStruct((len(PROJ_KEPT), s, SEC_W), F32)]
                  + [_view_shape(s, d, MM) for d in DILATIONS for _ in range(3)],
        scratch_shapes=[pltpu.VMEM((LANE_GROUPS, tm, LANES), F32)] * 3,
        compiler_params=_params("parallel"),
    )(x, norm_w, w_all, cos, sin)


def _band_mask(first_ok, second_ok):
    row, col = _iota2((ATT_BLK, 2 * ATT_BLK), 0), _iota2((ATT_BLK, 2 * ATT_BLK), 1)
    return ((col < ATT_BLK) & (col >= row) & first_ok) | ((col >= ATT_BLK) & ((col - ATT_BLK) <= row) & second_ok)


def _own_lanes(rows, h):
    lane = _iota2((rows, LANES), 1)
    return (lane < AT_DH) if h == 0 else (lane >= AT_DH)


def _neg_pieces(rows, h):
    lane = _iota2((rows, LANES), 1) - (AT_DH if h == 0 else 0)
    return jnp.where((lane >= 0) & (lane < 3), -1.0, 0.0).astype(MM)


def _units(qb):
    return [(b, slice(g * LANES, (g + 1) * LANES), h) for b in range(qb) for g in range(AT_COLS // LANES) for h in range(2)]


def _sub(b):
    return slice(b * ATT_BLK, (b + 1) * ATT_BLK)


def _band_before(cur_ref, prev_ref, b, sl):
    if b == 0:
        return jnp.concatenate([prev_ref[:, sl], cur_ref[0:ATT_BLK, sl]], axis=0)
    return cur_ref[(b - 1) * ATT_BLK:(b + 1) * ATT_BLK, sl]


def _band_after(cur_ref, next_ref, b, sl):
    if (b + 1) * ATT_BLK == cur_ref.shape[0]:
        return jnp.concatenate([cur_ref[b * ATT_BLK:(b + 1) * ATT_BLK, sl], next_ref[:, sl]], axis=0)
    return cur_ref[b * ATT_BLK:(b + 2) * ATT_BLK, sl]


def _attn_specs(rows):
    qb = min(AT_QB, rows // ATT_BLK)
    assert rows % (qb * ATT_BLK) == 0
    last = rows // ATT_BLK - 1
    cur = pl.BlockSpec((qb * ATT_BLK, AT_COLS), lambda c, n: (n, c))
    prev = pl.BlockSpec((ATT_BLK, AT_COLS), lambda c, n: (jnp.maximum(qb * n - 1, 0), c))
    nxt = pl.BlockSpec((ATT_BLK, AT_COLS), lambda c, n: (jnp.minimum(qb * (n + 1), last), c))
    return qb, cur, prev, nxt


def _stack_heads(a):
    h0 = _own_lanes(a.shape[0], 0)
    zero = jnp.zeros_like(a)
    return jnp.concatenate([jnp.where(h0, a, zero), jnp.where(h0, zero, a)], axis=0)


def _unstack_heads(a2):
    return jnp.where(_own_lanes(ATT_BLK, 0), a2[:ATT_BLK], a2[ATT_BLK:])


def _attn_fwd(qr, kr, vr, d):
    rows, cols = qr.shape
    qb, cur, prev, nxt = _attn_specs(rows)
    nb = rows // (qb * ATT_BLK)

    def body(q_ref, kc_ref, kp_ref, vc_ref, vp_ref, o_ref, lse_ref):
        twice = lambda m: jnp.concatenate([m, m], axis=0)
        masks = {True: twice(_band_mask(pl.program_id(1) > 0, True)), False: twice(_band_mask(True, True))}
        ones = jnp.ones((2 * ATT_BLK, LANES), MM)
        units = [(b, sl) for b, sl, h in _units(qb) if h == 0]
        scs = [jnp.where(masks[b == 0], _dot_nt(_stack_heads(q_ref[_sub(b), sl]), _band_before(kc_ref, kp_ref, b, sl)),
                         NEG) for b, sl in units]
        ms = [jnp.max(sc, axis=-1, keepdims=True) for sc in scs]
        ps = [jnp.exp(sc - m).astype(MM) for sc, m in zip(scs, ms)]
        ols = [jnp.dot(p, jnp.concatenate([_band_before(vc_ref, vp_ref, b, sl), ones], axis=1),
                       preferred_element_type=F32) for p, (b, sl) in zip(ps, units)]
        for (b, sl), m, ol in zip(units, ms, ols):
            l = _unstack_heads(ol[:, LANES:])
            o_ref[_sub(b), sl] = _unstack_heads(ol[:, :LANES]) / l
            lse_ref[_sub(b), sl] = _unstack_heads(jnp.broadcast_to(m, (2 * ATT_BLK, LANES))) + jnp.log(l)

    o, lse = pl.pallas_call(
        body, name=f"attn_fwd_d{d}", grid=(cols // AT_COLS, nb),
        in_specs=[cur, cur, prev, cur, prev], out_specs=[cur, cur],
        out_shape=[jax.ShapeDtypeStruct((rows, cols), F32)] * 2,
        compiler_params=_params("parallel", "parallel"),
    )(qr, kr, kr, vr, vr)
    return o, lse


def _attn_bwd_dq(qr, kr, vr, do, lse, delta, d):
    rows, cols = qr.shape
    qb, cur, prev, nxt = _attn_specs(rows)
    nb = rows // (qb * ATT_BLK)

    def body(q_ref, kc_ref, kp_ref, vc_ref, vp_ref, do_ref, lse_ref, dl_ref, dq_ref):
        masks = {True: _band_mask(pl.program_id(1) > 0, True), False: _band_mask(True, True)}
        units = _units(qb)
        sms, dps = [], []
        for b, sl, h in units:
            own, own_b, neg = _own_lanes(ATT_BLK, h), _own_lanes(2 * ATT_BLK, h), _neg_pieces(2 * ATT_BLK, h)
            sms.append(_dot_nt(jnp.where(own, q_ref[_sub(b), sl], lse_ref[_sub(b), sl]),
                               jnp.where(own_b, _band_before(kc_ref, kp_ref, b, sl), neg)))
            dps.append(_dot_nt(jnp.where(own, do_ref[_sub(b), sl], dl_ref[_sub(b), sl]),
                               jnp.where(own_b, _band_before(vc_ref, vp_ref, b, sl), neg)))
        dss = [(jnp.exp(jnp.where(masks[b == 0], sm, NEG)) * dp).astype(MM)
               for sm, dp, (b, _, _) in zip(sms, dps, units)]
        dqs = [jnp.dot(ds, _band_before(kc_ref, kp_ref, b, sl), preferred_element_type=F32) * (AT_DH ** -0.5)
               for ds, (b, sl, _) in zip(dss, units)]
        for i in range(0, len(units), 2):
            b, sl, _ = units[i]
            dq_ref[_sub(b), sl] = jnp.where(_own_lanes(ATT_BLK, 0), dqs[i], dqs[i + 1]).astype(dq_ref.dtype)

    dq = pl.pallas_call(
        body, name=f"attn_bwd_dq_d{d}", grid=(cols // AT_COLS, nb),
        in_specs=[cur, cur, prev, cur, prev, cur, cur, cur], out_specs=cur,
        out_shape=jax.ShapeDtypeStruct((rows, cols), MM),
        compiler_params=_params("parallel", "parallel"),
    )(qr, kr, kr, vr, vr, do, lse, delta)
    return dq


def _attn_bwd_dkv(qr, kr, vr, do, lse, delta, d):
    rows, cols = qr.shape
    qb, cur, prev, nxt = _attn_specs(rows)
    nb = rows // (qb * ATT_BLK)

    def body(k_ref, v_ref, qc_ref, qn_ref, doc_ref, don_ref, lsec_ref, lsen_ref, dlc_ref, dln_ref,
             dk_ref, dv_ref):
        masks = {True: _band_mask(True, pl.program_id(1) < nb - 1), False: _band_mask(True, True)}
        units = _units(qb)
        sms, dps = [], []
        for b, sl, h in units:
            own, own_b, neg = _own_lanes(ATT_BLK, h), _own_lanes(2 * ATT_BLK, h), _neg_pieces(ATT_BLK, h)
            sms.append(_dot_nt(jnp.where(own, k_ref[_sub(b), sl], neg),
                               jnp.where(own_b, _band_after(qc_ref, qn_ref, b, sl),
                                         _band_after(lsec_ref, lsen_ref, b, sl))))
            dps.append(_dot_nt(jnp.where(own, v_ref[_sub(b), sl], neg),
                               jnp.where(own_b, _band_after(doc_ref, don_ref, b, sl),
                                         _band_after(dlc_ref, dln_ref, b, sl))))
        ps = [jnp.exp(jnp.where(masks[b == qb - 1], sm, NEG)) for sm, (b, _, _) in zip(sms, units)]
        dss = [(p * dp).astype(MM) for p, dp in zip(ps, dps)]
        dvs = [jnp.dot(p.astype(MM), _band_after(doc_ref, don_ref, b, sl), preferred_element_type=F32)
               for p, (b, sl, _) in zip(ps, units)]
        dks = [jnp.dot(ds, _band_after(qc_ref, qn_ref, b, sl), preferred_element_type=F32)
               for ds, (b, sl, _) in zip(dss, units)]
        head0 = _own_lanes(ATT_BLK, 0)
        for i in range(0, len(units), 2):
            b, sl, _ = units[i]
            dk_ref[_sub(b), sl] = jnp.where(head0, dks[i], dks[i + 1]).astype(dk_ref.dtype)
            dv_ref[_sub(b), sl] = jnp.where(head0, dvs[i], dvs[i + 1]).astype(dv_ref.dtype)

    dk, dv = pl.pallas_call(
        body, name=f"attn_bwd_dkv_d{d}", grid=(cols // AT_COLS, nb),
        in_specs=[cur, cur, cur, nxt, cur, nxt, cur, nxt, cur, nxt], out_specs=[cur, cur],
        out_shape=[jax.ShapeDtypeStruct((rows, cols), MM)] * 2,
        compiler_params=_params("parallel", "parallel"),
    )(kr, vr, qr, qr, do, do, lse, lse, delta, delta)
    return dk, dv


def _head_sum(a, width):
    parts = []
    for j in range(a.shape[1] // width):
        sm = jnp.sum(a[:, j * width:(j + 1) * width], axis=-1, keepdims=True)
        parts.append(jnp.broadcast_to(sm, (a.shape[0], width)))
    return jnp.concatenate(parts, axis=-1)


def _partner_pieces(x):
    xs = jnp.concatenate([pltpu.roll(x[:, j * LANES:(j + 1) * LANES], AT_DH, 1) for j in range(x.shape[1] // LANES)],
                         axis=-1)
    hi = xs.astype(jnp.bfloat16).astype(F32)
    mid = (xs - hi).astype(jnp.bfloat16).astype(F32)
    lo = (xs - hi - mid).astype(jnp.bfloat16).astype(F32)
    lane = _iota2(x.shape, 1) % AT_DH
    return jnp.where(lane == 0, hi, jnp.where(lane == 1, mid, jnp.where(lane == 2, lo, 0.0)))


def _mid(x, tgt, proj, o_hg, o_at, lse_at, hg_norm_w, final_norm_w, wo_all, tm=256):
    s = x.shape[0]
    nb = s // tm

    def body(x_ref, t_ref, hgz_ref, atz_ref, ohg_ref, o1_ref, o2_ref, o3_ref, l1_ref, l2_ref, l3_ref,
             g_ref, fw_ref, wo_ref,
             dh_ref, dohg_ref, dhgz_ref, datz_ref, do1_ref, do2_ref, do3_ref, dl1_ref, dl2_ref, dl3_ref,
             lp1_ref, lp2_ref, lp3_ref,
             gwo_ref, gfw_ref, ghg_ref, loss_ref, nat_ref, stage_ref, gwo_acc):
        @pl.when(pl.program_id(0) == 0)
        def _():
            gwo_acc[...] = jnp.zeros_like(gwo_acc)
            gfw_ref[...] = jnp.zeros_like(gfw_ref)
            ghg_ref[...] = jnp.zeros_like(ghg_ref)
            loss_ref[...] = jnp.zeros_like(loss_ref)

        ohg, g = ohg_ref[...], g_ref[...]
        rs = lax.rsqrt(_head_sum(ohg * ohg, HG_D) * (1.0 / HG_D) + NORM_EPS)
        on = ohg * rs
        hgz = hgz_ref[...]
        sz = _sigmoid(hgz)
        gate_hg = hgz * sz
        lses, outs = [l1_ref[...]], [o1_ref[...]]
        for k, (d, l_ref, o_ref) in enumerate(zip(DILATIONS[1:], (l2_ref, l3_ref), (o2_ref, o3_ref))):
            _from_view(l_ref, nat_ref.at[2 * k], d)
            _from_view(o_ref, nat_ref.at[2 * k + 1], d)
            lses.append(_get_lanes(nat_ref.at[2 * k]))
            outs.append(_get_lanes(nat_ref.at[2 * k + 1]))
        mx = jnp.maximum(jnp.maximum(lses[0], lses[1]), lses[2])
        es = [jnp.exp(l - mx) for l in lses]
        den = es[0] + es[1] + es[2]
        ws = [e / den for e in es]
        oat = ws[0] * outs[0] + ws[1] * outs[1] + ws[2] * outs[2]
        atz = atz_ref[...]
        sa = _sigmoid(atz)
        gate_at = atz * sa
        mixed = jnp.concatenate([on * g * gate_hg, oat * gate_at], axis=-1).astype(MM)
        h = x_ref[...] + jnp.dot(mixed, wo_ref[...], preferred_element_type=F32)
        rstd = lax.rsqrt(jnp.mean(h * h, axis=-1, keepdims=True) + NORM_EPS)
        hn = h * rstd
        fw = fw_ref[...]
        err = hn * fw - t_ref[...]
        loss_ref[...] += 0.5 * jnp.sum(jnp.mean(err * err, axis=-1, keepdims=True), axis=0, keepdims=True)
        dout = err * (1.0 / D_MODEL)
        gfw_ref[...] += jnp.sum(dout * hn, axis=0, keepdims=True)
        dhn = dout * fw
        dh = rstd * (dhn - hn * jnp.mean(dhn * hn, axis=-1, keepdims=True))
        dh_ref[...] = dh
        dh_mm = dh.astype(MM)
        gwo_acc[...] += _dot_tn(mixed, dh_mm)

        @pl.when(pl.program_id(0) == nb - 1)
        def _():
            gwo_ref[...] = gwo_acc[...].astype(gwo_ref.dtype)

        dmixed = _dot_nt(dh_mm, wo_ref[...])
        dm_hg = dmixed[:, :SEC_W]
        d_ong = dm_hg * gate_hg
        dhgz_ref[...] = (dm_hg * (on * g) * (sz * (1.0 + hgz * (1.0 - sz)))).astype(MM)
        ghg_ref[...] += jnp.sum(d_ong * on, axis=0, keepdims=True)
        d_on = d_ong * g
        dohg_ref[...] = rs * (d_on - on * (_head_sum(d_on * on, HG_D) * (1.0 / HG_D)))
        dm_at = dmixed[:, SEC_W:]
        d_oat = dm_at * gate_at
        datz_ref[...] = (dm_at * oat * (sa * (1.0 + atz * (1.0 - sa)))).astype(MM)
        drow = _head_sum(d_oat * oat, AT_DH)
        lse_all = mx + jnp.log(den)
        for val, dst_refs in ((d_oat, (do1_ref, do2_ref, do3_ref)),
                              (_partner_pieces(drow), (dl1_ref, dl2_ref, dl3_ref)),
                              (_partner_pieces(lse_all), (lp1_ref, lp2_ref, lp3_ref))):
            _set_lanes(stage_ref, val)
            for d, dst_ref in zip(DILATIONS, dst_refs):
                _to_view(stage_ref, dst_ref, d)

    row = lambda w: pl.BlockSpec((tm, w), lambda i: (i, 0))
    sec = lambda j: pl.BlockSpec((None, tm, SEC_W), lambda i, j=j: (j, i, 0))
    const = lambda shp: pl.BlockSpec(shp, lambda i: (0,) * len(shp))
    half = row(SEC_W)
    views = [_view_spec(tm, d) for d in DILATIONS]
    return pl.pallas_call(
        body, name="mid", grid=(nb,),
        in_specs=[row(D_MODEL), row(D_MODEL), sec(PROJ_KEPT.index(3)), sec(PROJ_KEPT.index(7)), half] + views * 2
                 + [const((1, SEC_W)), const((1, D_MODEL)), const((D_MODEL, D_MODEL))],
        out_specs=[row(D_MODEL)] + [half] * 3 + views * 3
                  + [const((D_MODEL, D_MODEL)), const((1, D_MODEL)), const((1, SEC_W)), const((1, 1))],
        out_shape=[jax.ShapeDtypeStruct((s, D_MODEL), F32), jax.ShapeDtypeStruct((s, SEC_W), F32)]
                  + [jax.ShapeDtypeStruct((s, SEC_W), MM)] * 2
                  + [_view_shape(s, d, MM) for d in DILATIONS] * 3
                  + [jax.ShapeDtypeStruct((D_MODEL, D_MODEL), XCH), jax.ShapeDtypeStruct((1, D_MODEL), F32),
                     jax.ShapeDtypeStruct((1, SEC_W), F32), jax.ShapeDtypeStruct((1, 1), F32)],
        scratch_shapes=[pltpu.VMEM((4, LANE_GROUPS, tm, LANES), F32), pltpu.VMEM((LANE_GROUPS, tm, LANES), F32),
                        pltpu.VMEM((D_MODEL, D_MODEL), F32)],
        compiler_params=_params("arbitrary"),
    )(x, tgt, proj, proj, o_hg, *o_at, *lse_at, hg_norm_w, final_norm_w, wo_all)


def _section_specs(dsecs, tm):
    return [pl.BlockSpec((tm, SEC_W), lambda i: (i, 0)) if k is None
            else pl.BlockSpec((None, tm, SEC_W), lambda i, k=k: (k, i, 0)) for _, k in dsecs]


def _inproj_bwd_x(x, norm_w, w_all, dh, dsecs, token, tm=512):
    s = x.shape[0]

    def body(x_ref, nw_ref, w_ref, dh_ref, tok_ref, *refs):
        sec_refs, (gx_ref, gnw_ref) = refs[:N_SEC], refs[N_SEC:]

        @pl.when(pl.program_id(0) == 0)
        def _():
            gnw_ref[...] = jnp.zeros_like(gnw_ref)

        du = jnp.zeros((tm, D_MODEL), F32)
        for j in range(N_SEC):
            du = du + _dot_nt(sec_refs[j][...], w_ref[j])
        xv, nw = x_ref[...], nw_ref[...]
        rstd = lax.rsqrt(jnp.mean(xv * xv, axis=-1, keepdims=True) + NORM_EPS)
        xn = xv * rstd
        gnw_ref[...] += jnp.sum(du * xn, axis=0, keepdims=True)
        dxn = du * nw
        dx = rstd * (dxn - xn * jnp.mean(dxn * xn, axis=-1, keepdims=True))
        gx_ref[...] = (dh_ref[...] + tok_ref[0:1, 0:1]) + dx

    row = lambda w: pl.BlockSpec((tm, w), lambda i: (i, 0))
    const = lambda shp: pl.BlockSpec(shp, lambda i: (0,) * len(shp))
    return pl.pallas_call(
        body, name="inproj_bwd_x", grid=(s // tm,),
        in_specs=[row(D_MODEL), const((1, D_MODEL)), const((N_SEC, D_MODEL, SEC_W)), row(D_MODEL), const((8, 128))]
                 + _section_specs(dsecs, tm),
        out_specs=[row(D_MODEL), const((1, D_MODEL))],
        out_shape=[jax.ShapeDtypeStruct((s, D_MODEL), F32), jax.ShapeDtypeStruct((1, D_MODEL), F32)],
        compiler_params=_params("arbitrary"),
    )(x, norm_w, w_all, dh, token, *[a for a, _ in dsecs])


def _inproj_bwd_w(x, norm_w, dsec, dq_r, dk_r, dv, cos, sin, tm=512):
    s = x.shape[0]
    nb = s // tm

    def body(x_ref, nw_ref, s0, s1, s2, s3, s7, q1, q2, q3, k1, k2, k3, v1, v2, v3, cos_ref, sin_ref,
             gw_hbm, datt_ref, acc_ref, stage_ref, nat_ref):
        @pl.when(pl.program_id(0) == 0)
        def _():
            acc_ref[...] = jnp.zeros_like(acc_ref)

        def total(refs):
            acc = refs[0][...].astype(F32)
            for d, ref in zip(DILATIONS[1:], refs[1:]):
                _from_view(ref, nat_ref, d)
                acc = acc + _get_lanes(nat_ref)
            return acc

        c, sn = cos_ref[...], -sin_ref[...]
        unrot = lambda a: jnp.concatenate(
            [_rope128(a[:, j * LANES:(j + 1) * LANES], c, sn) for j in range(LANE_GROUPS)], axis=-1)
        att = [a.astype(MM) for a in (unrot(total((q1, q2, q3))), unrot(total((k1, k2, k3))), total((v1, v2, v3)))]
        for j, a in enumerate(att):
            datt_ref[j] = a
        xv = x_ref[...]
        rstd = lax.rsqrt(jnp.mean(xv * xv, axis=-1, keepdims=True) + NORM_EPS)
        u_t = (xv * rstd * nw_ref[...]).T.astype(MM)
        for j, dsj in enumerate((s0[...], s1[...], s2[...], s3[...], *att, s7[...])):
            acc_ref[j] += jnp.dot(u_t, dsj, preferred_element_type=F32)

        @pl.when(pl.program_id(0) == nb - 1)
        def _():
            for j in range(N_SEC):
                stage_ref[...] = acc_ref[j].astype(stage_ref.dtype)
                pltpu.sync_copy(stage_ref, gw_hbm.at[j])

    row = lambda w: pl.BlockSpec((tm, w), lambda i: (i, 0))
    return pl.pallas_call(
        body, name="inproj_bwd_w", grid=(nb,),
        in_specs=[row(D_MODEL), pl.BlockSpec((1, D_MODEL), lambda i: (0, 0))] + [row(SEC_W)] * 5
                 + [_view_spec(tm, d) for d in DILATIONS] * 3 + [row(LANES), row(LANES)],
        out_specs=[pl.BlockSpec(memory_space=pl.ANY), pl.BlockSpec((3, tm, SEC_W), lambda i: (0, i, 0))],
        out_shape=[jax.ShapeDtypeStruct((N_SEC, D_MODEL, SEC_W), XCH), jax.ShapeDtypeStruct((3, s, SEC_W), MM)],
        scratch_shapes=[pltpu.VMEM((N_SEC, D_MODEL, SEC_W), F32), pltpu.VMEM((D_MODEL, SEC_W), XCH),
                        pltpu.VMEM((LANE_GROUPS, tm, LANES), F32)],
        compiler_params=_params("arbitrary"),
    )(x, norm_w, *dsec, *dq_r, *dk_r, *dv, cos, sin)


def _local_step(x, tgt, norm_w, w_all, lb_logits, hg_norm_w, wo_all, final_norm_w, on_weight_grads):
    s = x.shape[0]
    cos, sin = _rope_tables(s)
    proj, *qkv = _inproj_fwd(x, norm_w, w_all, cos, sin)
    o_hg, sst = _hgrn_fwd(proj, lb_logits)
    qkv = [qkv[3 * i:3 * i + 3] for i in range(len(DILATIONS))]
    att = [_attn_fwd(*qkv_d, d) for qkv_d, d in zip(qkv, DILATIONS)]
    (dh, d_ohg, d_hgz, d_atz, do1, do2, do3, dl1, dl2, dl3, lp1, lp2, lp3, gwo, gfw, ghg, loss) = _mid(
        x, tgt, proj, o_hg, [a[0] for a in att], [a[1] for a in att], hg_norm_w, final_norm_w[None, :], wo_all)
    dxq, dxf, dxi, dlb = _hgrn_bwd(proj, lb_logits, d_ohg, sst)
    dq_r, dk_r, dv = [], [], []
    for d, qkv_d, do, lp, dl in zip(DILATIONS, qkv, (do1, do2, do3), (lp1, lp2, lp3), (dl1, dl2, dl3)):
        dq_r.append(_attn_bwd_dq(*qkv_d, do, lp, dl, d))
        dk_d, dv_d = _attn_bwd_dkv(*qkv_d, do, lp, dl, d)
        dk_r.append(dk_d)
        dv.append(dv_d)
    gwi, d_att = _inproj_bwd_w(x, norm_w, (dxq, dxf, dxi, d_hgz, d_atz), dq_r, dk_r, dv, cos, sin)
    dsecs = [(dxq, None), (dxf, None), (dxi, None), (d_hgz, None), (d_att, 0), (d_att, 1), (d_att, 2), (d_atz, None)]
    token = on_weight_grads(gwi, gwo)
    gx, gnw = _inproj_bwd_x(x, norm_w, w_all, dh, dsecs, token)
    small = jnp.concatenate([gnw, jnp.concatenate([dlb, ghg], axis=-1), gfw,
                             jnp.pad(loss, ((0, 0), (0, D_MODEL - 1)))], axis=0)
    return gx, gwi, gwo, small


def _coords():
    return lax.axis_index("x"), lax.axis_index("y"), lax.axis_index("c")


def _gather_weights(w_in, w_out):
    wo_rows = w_out.shape[0]

    def body(wi_ref, wo_ref, wi_all, wo_all, send_sems, recv_sems):
        x, y, c = _coords()
        me, sibling = (x, y, c), (x, y, 1 - c)
        chips = [(1 - x, y), (x, 1 - y), (1 - x, 1 - y)]
        slot = lambda p: 4 * p[0] + 2 * p[1] + p[2]

        def copies(k, block, to):
            return [pltpu.make_async_remote_copy(
                src_ref=ref.at[slot(block)], dst_ref=ref.at[slot(block)], send_sem=send_sems.at[a, k],
                recv_sem=recv_sems.at[a, k], device_id=to, device_id_type=MESH)
                for a, ref in enumerate((wi_all, wo_all))]

        wi_all[slot(me)] = wi_ref[...].astype(MM)
        wo_all[slot(me)] = wo_ref[...].astype(MM)
        first = copies(0, me, sibling)
        for j, chip in enumerate(chips):
            first += copies(1 + j, me, (*chip, c))
        for cp in first:
            cp.start()
        passed = []
        for j, chip in enumerate(chips):
            for cp in copies(1 + j, (*chip, c), me):
                cp.wait_recv()
            fwd = copies(4 + j, (*chip, c), sibling)
            for cp in fwd:
                cp.start()
            passed += fwd
        for cp in copies(0, sibling, me):
            cp.wait_recv()
        for j, chip in enumerate(chips):
            for cp in copies(4 + j, (*chip, 1 - c), me):
                cp.wait_recv()
        for cp in first + passed:
            cp.wait_send()

    vmem = pl.BlockSpec(memory_space=pltpu.VMEM)
    return pl.pallas_call(
        body, name="gather_weights",
        in_specs=[vmem, vmem], out_specs=[vmem, vmem],
        out_shape=[jax.ShapeDtypeStruct((N_DEV, D_MODEL, SEC_W), MM),
                   jax.ShapeDtypeStruct((N_DEV, wo_rows, D_MODEL), MM)],
        scratch_shapes=[pltpu.SemaphoreType.DMA((2, 7)), pltpu.SemaphoreType.DMA((2, 7))],
        compiler_params=pltpu.CompilerParams(vmem_limit_bytes=VMEM_LIMIT),
    )(w_in, w_out)


def _me():
    x, y, c = _coords()
    return 4 * x + 2 * y + c


def _grad_copies(srcs, lands, send_sems, recv_sems):
    x, y, c = _coords()
    me = 4 * x + 2 * y + c
    copies = []
    for k in range(1, N_DEV):
        px, py, pc = x ^ (k >> 2), y ^ ((k >> 1) & 1), c ^ (k & 1)
        peer = 4 * px + 2 * py + pc
        for a, (src, dst) in enumerate(zip(srcs, lands)):
            copies.append(pltpu.make_async_remote_copy(
                src_ref=src.at[peer], dst_ref=dst.at[me], send_sem=send_sems.at[a * (N_DEV - 1) + k - 1],
                recv_sem=recv_sems.at[a * (N_DEV - 1) + k - 1], device_id=(px, py, pc), device_id_type=MESH))
    return copies


HBM_SPEC = pl.BlockSpec(memory_space=pltpu.HBM)
SEM_SPEC = pl.BlockSpec(memory_space=pltpu.SEMAPHORE)
SPLIT_COPY_EFFECT = pltpu.SideEffectType.DATAFLOW_SIDE_EFFECTING


def _exchange_start(gwi, gwo):
    def body(gwi_ref, gwo_ref, li_ref, lo_ref, send_sems, recv_sems, gwi_thru, gwo_thru, li_thru, lo_thru, token):
        for cp in _grad_copies((gwi_ref, gwo_ref), (li_ref, lo_ref), send_sems, recv_sems):
            cp.start()
        token[...] = jnp.zeros_like(token)

    hbm = lambda a: pltpu.with_memory_space_constraint(a, pltpu.HBM)
    bufs = (gwi, gwo, lax.empty(gwi.shape, gwi.dtype), lax.empty(gwo.shape, gwo.dtype))
    return pl.pallas_call(
        body, name="exchange_start",
        out_shape=(pltpu.SemaphoreType.DMA((2 * (N_DEV - 1),)), pltpu.SemaphoreType.DMA((2 * (N_DEV - 1),)),
                   *[pltpu.HBM(a.shape, a.dtype) for a in bufs], jax.ShapeDtypeStruct((8, 128), F32)),
        in_specs=[HBM_SPEC] * 4,
        out_specs=(SEM_SPEC, SEM_SPEC, HBM_SPEC, HBM_SPEC, HBM_SPEC, HBM_SPEC, pl.BlockSpec(memory_space=pltpu.VMEM)),
        input_output_aliases={0: 2, 1: 3, 2: 4, 3: 5},
        compiler_params=pltpu.CompilerParams(has_side_effects=SPLIT_COPY_EFFECT),
    )(*[hbm(a) for a in bufs])


def _exchange_wait(send_sems, recv_sems, gwi, gwo, li, lo, after):
    def body(gwi_ref, gwo_ref, li_ref, lo_ref, send_sems, recv_sems, after_ref, gwi_out, gwo_out, li_out, lo_out):
        for cp in _grad_copies((gwi_ref, gwo_ref), (li_ref, lo_ref), send_sems, recv_sems):
            cp.wait_send()
            cp.wait_recv()

    return pl.pallas_call(
        body, name="exchange_wait",
        out_shape=tuple(pltpu.HBM(a.shape, a.dtype) for a in (gwi, gwo, li, lo)),
        in_specs=[HBM_SPEC] * 4 + [SEM_SPEC, SEM_SPEC, pl.BlockSpec(memory_space=pl.ANY)],
        out_specs=(HBM_SPEC,) * 4,
        input_output_aliases={0: 0, 1: 1, 2: 2, 3: 3},
        compiler_params=pltpu.CompilerParams(has_side_effects=SPLIT_COPY_EFFECT),
    )(gwi, gwo, li, lo, send_sems, recv_sems, after)


def _gather_small(small):
    def body(sm_ref, ls_ref, send_sems, recv_sems, local_sem):
        x, y, c = _coords()
        me = 4 * x + 2 * y + c
        own = pltpu.make_async_copy(sm_ref, ls_ref.at[me], local_sem)
        own.start()
        sends = []
        for k in range(1, N_DEV):
            peer = (x ^ (k >> 2), y ^ ((k >> 1) & 1), c ^ (k & 1))
            sends.append(pltpu.make_async_remote_copy(
                src_ref=sm_ref, dst_ref=ls_ref.at[me], send_sem=send_sems.at[k - 1], recv_sem=recv_sems.at[k - 1],
                device_id=peer, device_id_type=MESH))
        for cp in sends:
            cp.start()
        for cp in sends:
            cp.wait_recv()
        for cp in sends:
            cp.wait_send()
        own.wait()

    vmem = pl.BlockSpec(memory_space=pltpu.VMEM)
    return pl.pallas_call(
        body, name="gather_small", in_specs=[vmem], out_specs=vmem,
        out_shape=jax.ShapeDtypeStruct((N_DEV,) + small.shape, F32),
        scratch_shapes=[pltpu.SemaphoreType.DMA((N_DEV - 1,)), pltpu.SemaphoreType.DMA((N_DEV - 1,)),
                        pltpu.SemaphoreType.DMA],
    )(small)


def _adamw(w, g, m, v):
    m = ADAM_B1 * m + (1.0 - ADAM_B1) * g
    v = ADAM_B2 * v + (1.0 - ADAM_B2) * (g * g)
    m_hat = m / (1.0 - ADAM_B1 ** ADAM_STEP)
    v_hat = v / (1.0 - ADAM_B2 ** ADAM_STEP)
    return -ADAM_LR * (m_hat / (jnp.sqrt(v_hat) + ADAM_EPS) + ADAM_WD * w), m, v


def _slot_sum(ref, own=None, me=None):
    g = None
    for i in range(N_DEV):
        term = ref[i].astype(F32)
        if own is not None:
            term = jnp.where(i == me, own, term)
        g = term if g is None else g + term
    return g


def _update_matrix(name, me, landed, own, w, m, v, rows):
    r, c = w.shape

    def body(me_ref, l_ref, own_ref, w_ref, m_ref, v_ref, g_ref, d_ref, nm_ref, nv_ref):
        g = _slot_sum(l_ref, own_ref[...].astype(F32), me_ref[0])
        g_ref[...] = g
        d_ref[...], nm_ref[...], nv_ref[...] = _adamw(w_ref[...], g, m_ref[...], v_ref[...])

    blk = pl.BlockSpec((rows, c), lambda i, me_ref: (i, 0))
    return pl.pallas_call(
        body, name=name,
        grid_spec=pltpu.PrefetchScalarGridSpec(
            num_scalar_prefetch=1, grid=(r // rows,),
            in_specs=[pl.BlockSpec((N_DEV, rows, c), lambda i, me_ref: (0, i, 0)),
                      pl.BlockSpec((None, rows, c), lambda i, me_ref: (me_ref[0], i, 0)), blk, blk, blk],
            out_specs=[blk] * 4),
        out_shape=[jax.ShapeDtypeStruct((r, c), F32)] * 4,
        compiler_params=_params("parallel"),
    )(me, landed, own, w, m, v)


def _update_small(landed, lb_logits, ws, ms, vs):
    def body(l_ref, lbl_ref, w_ref, m_ref, v_ref, g_ref, d_ref, nm_ref, nv_ref, loss_ref):
        tot = _slot_sum(l_ref)
        _, dlb = _lower_bound(lbl_ref[...])
        g_lb = tot[1:2, :SEC_W] * dlb
        g = jnp.concatenate([tot[0:1], jnp.concatenate([g_lb, -g_lb], axis=-1),
                             jnp.pad(tot[1:2, SEC_W:], ((0, 0), (0, SEC_W))), tot[2:3]], axis=0)
        g_ref[...] = g
        d_ref[...], nm_ref[...], nv_ref[...] = _adamw(w_ref[...], g, m_ref[...], v_ref[...])
        loss_ref[...] = tot[3:4, 0:1]

    vmem = pl.BlockSpec(memory_space=pltpu.VMEM)
    return pl.pallas_call(
        body, name="update_small", in_specs=[vmem] * 5, out_specs=[vmem] * 5,
        out_shape=[jax.ShapeDtypeStruct((4, D_MODEL), F32)] * 4 + [jax.ShapeDtypeStruct((1, 1), F32)],
    )(landed, lb_logits, ws, ms, vs)


def _pack_small(norm_w, lb_logits, hg_norm_w, final_norm_w):
    return jnp.concatenate([norm_w, lb_logits.reshape(1, D_MODEL),
                            jnp.pad(hg_norm_w, ((0, 0), (0, D_MODEL - SEC_W))), final_norm_w[None, :]], axis=0)


def _unpack_small(a):
    return a[0:1], a[1].reshape(2, SEC_W), a[2:3, :SEC_W], a[3]


def kernel(x, norm_w, w_in, hgrn_lb_logits, hg_norm_w, w_out, final_norm_w, loss_target, m_norm_w, m_w_in, m_hgrn_lb_logits, m_hg_norm_w, m_w_out, m_final_norm_w, v_norm_w, v_w_in, v_hgrn_lb_logits, v_hg_norm_w, v_w_out, v_final_norm_w):
    w_all, wo_all = _gather_weights(w_in[0], w_out[0])
    in_flight = []

    def start_exchange(gwi, gwo):
        *handles, token = _exchange_start(gwi, gwo.reshape(N_DEV, D_MODEL // N_DEV, D_MODEL))
        in_flight.extend(handles)
        return token

    gx, _, _, small = _local_step(x[0], loss_target[0], norm_w, w_all, hgrn_lb_logits, hg_norm_w,
                                  wo_all.reshape(D_MODEL, D_MODEL), final_norm_w, start_exchange)
    ls = _gather_small(small)
    gwi, gwo, li, lo = _exchange_wait(*in_flight, gx)
    me = _me().astype(jnp.int32).reshape(1)
    g_wi, d_wi, nm_wi, nv_wi = _update_matrix("update_w_in", me, li, gwi, w_in[0], m_w_in[0], v_w_in[0], 256)
    g_wo, d_wo, nm_wo, nv_wo = _update_matrix("update_w_out", me, lo, gwo, w_out[0], m_w_out[0], v_w_out[0], 128)
    g_s, d_s, nm_s, nv_s, loss = _update_small(
        ls, hgrn_lb_logits, _pack_small(norm_w, hgrn_lb_logits, hg_norm_w, final_norm_w),
        _pack_small(m_norm_w, m_hgrn_lb_logits, m_hg_norm_w, m_final_norm_w),
        _pack_small(v_norm_w, v_hgrn_lb_logits, v_hg_norm_w, v_final_norm_w))
    outs = []
    for small_out, wi, wo in ((g_s, g_wi, g_wo), (d_s, d_wi, d_wo), (nm_s, nm_wi, nm_wo), (nv_s, nv_wi, nv_wo)):
        nw, lb, hg, fw = _unpack_small(small_out)
        outs += [nw, wi[None], lb, hg, wo[None], fw]
    return (loss[0, 0], gx[None], *outs)
```

```python
import functools

import jax
import jax.numpy as jnp
import numpy as np
from jax import lax
from jax.experimental import pallas as pl
from jax.experimental.pallas import tpu as pltpu

F32 = jnp.float32
MM = jnp.bfloat16
XCH = jnp.bfloat16
NORM_EPS = 1e-6
NEG = -1e30
N_DEV = 8
D_MODEL = 1024
N_SEC = 8
SEC_W = 512
HG_HEADS = 4
HG_D = 128
HG_GROUP = 4
AT_DH = 64
LANES = 128
ATT_BLK = 128
AT_COLS = 512
AT_QB = 8
DILATIONS = (1, 4, 16)
ROPE_THETA = 10000.0
CH = 16
LB_LO, LB_HI = 1e-6, 1.0 - 1e-6
ADAM_LR, ADAM_B1, ADAM_B2, ADAM_EPS, ADAM_WD, ADAM_STEP = 0.001, 0.9, 0.999, 1e-08, 0.01, 10
VMEM_LIMIT = 56 * 1024 * 1024
MESH = pl.DeviceIdType.MESH


def _params(*sem):
    return pltpu.CompilerParams(dimension_semantics=sem, vmem_limit_bytes=VMEM_LIMIT)


def _sigmoid(x):
    return 1.0 / (1.0 + jnp.exp(-x))


def _dot(a, b):
    return jnp.dot(a.astype(MM), b.astype(MM), preferred_element_type=F32)


def _dot_nt(a, b):
    return lax.dot_general(a.astype(MM), b.astype(MM), (((1,), (1,)), ((), ())), preferred_element_type=F32)


def _dot_tn(a, b):
    return lax.dot_general(a.astype(MM), b.astype(MM), (((0,), (0,)), ((), ())), preferred_element_type=F32)


def _tri_dot(tri, g):
    g1 = g.astype(jnp.bfloat16)
    r1 = g - g1.astype(F32)
    g2 = r1.astype(jnp.bfloat16)
    g3 = (r1 - g2.astype(F32)).astype(jnp.bfloat16)
    t = tri.astype(jnp.bfloat16)
    d = functools.partial(jnp.dot, preferred_element_type=F32)
    return d(t, g1) + d(t, g2) + d(t, g3)


def _lower_bound(lbl):
    l0, l1 = lbl[0:1, :], lbl[1:2, :]
    m = jnp.maximum(l0, l1)
    e0, e1 = jnp.exp(l0 - m), jnp.exp(l1 - m)
    p = e0 / (e0 + e1)
    inside = (p >= LB_LO) & (p <= LB_HI)
    return jnp.clip(p, LB_LO, LB_HI), jnp.where(inside, p * (e1 / (e0 + e1)), 0.0)


def _iota2(shape, dim):
    return lax.broadcasted_iota(jnp.int32, shape, dim)


def _hgrn_gates(xq, xf, lb):
    sgq = _sigmoid(xq)
    sg = _sigmoid(xf)
    sn = _sigmoid(-xf)
    f = lb + (1.0 - lb) * sg
    return sgq, xq * sgq, sg, sn, f, (1.0 - lb) * sn


def _bdot(a, b, ca, cb):
    return lax.dot_general(a.astype(MM), b.astype(MM), (((ca,), (cb,)), ((0,), (0,))), preferred_element_type=F32)


def _chunk_masks(rb):
    row, col = _iota2((rb, rb), 0), _iota2((rb, rb), 1)
    same = (row // CH) == (col // CH)
    return same & (row >= col), same & (row <= col)


HALF = CH // 2
SLAB_ROWS = HALF * CH + (HALF // 2) * CH


def _write_slabs(slab_ref, g, q3, b3):
    slab = lambda t, rows: q3[:, rows, :] * jnp.exp(jnp.minimum(b3[:, rows, :] - b3[:, t:t + 1, :], 0.0))
    late = slice(HALF, CH)
    for t in range(HALF):
        slab_ref[g, :, t * CH:(t + 1) * CH, :] = slab(t, slice(0, CH)).astype(MM)
    for p in range(HALF // 2):
        t = HALF + 2 * p
        two = jnp.concatenate([slab(t, late), slab(t + 1, late)], axis=1)
        slab_ref[g, :, (HALF + p) * CH:(HALF + p + 1) * CH, :] = two.astype(MM)


def _read_diag(r):
    nc = r.shape[0]
    col, col_late = _iota2((nc, CH, CH), 2), _iota2((nc, HALF, CH), 2)
    a, a_late = jnp.zeros((nc, CH, CH), F32), jnp.zeros((nc, HALF, CH), F32)
    for t in range(HALF):
        a = a + jnp.where(col == t, r[:, t * CH:(t + 1) * CH, :], 0.0)
    for p in range(HALF // 2):
        t, two = HALF + 2 * p, r[:, (HALF + p) * CH:(HALF + p + 1) * CH, :]
        a_late = a_late + jnp.where(col_late == t, two[:, :HALF, :], 0.0) + jnp.where(col_late == t + 1, two[:, HALF:, :], 0.0)
    return a + jnp.concatenate([jnp.zeros_like(a_late), a_late], axis=1)


def _hgrn_fwd(proj, lb_logits, rb=256):
    s = proj.shape[1]
    nb, nc = s // rb, rb // CH

    def body(q_ref, f_ref, i_ref, lbl_ref, o_ref, sst_ref, st_ref, slab_ref, states_ref):
        @pl.when(pl.program_id(1) == 0)
        def _():
            st_ref[...] = jnp.zeros_like(st_ref)

        sst_ref[...] = st_ref[...]
        prefix, _ = _chunk_masks(rb)
        c3 = lambda a: a.reshape(nc, CH, HG_D)
        row, col = _iota2((nc, CH, CH), 1), _iota2((nc, CH, CH), 2)
        heads = []
        for g in range(HG_GROUP):
            hs = slice(g * HG_D, (g + 1) * HG_D)
            lb, _ = _lower_bound(lbl_ref[:, hs])
            _, q, _, _, f, kk = _hgrn_gates(q_ref[:, hs], f_ref[:, hs], lb)
            b3 = c3(_tri_dot(prefix, jnp.log(f)))
            q3, kk3, v3 = c3(q), c3(kk), c3(i_ref[:, hs])
            bl3 = b3[:, CH - 1:CH, :]
            _write_slabs(slab_ref, g, q3, b3)
            x_upd = _bdot(v3, kk3 * jnp.exp(bl3 - b3), 1, 1)
            heads.append(dict(hs=hs, kk3=kk3, v3=v3, qe3=q3 * jnp.exp(b3), ebl3=jnp.exp(bl3), x_upd=x_upd))
        for g, hd in enumerate(heads):
            st = st_ref[g]
            for c in range(nc):
                states_ref[g, c] = st
                st = st * hd["ebl3"][c] + hd["x_upd"][c]
            st_ref[g] = st
        for g, hd in enumerate(heads):
            a = _read_diag(_bdot(slab_ref[g], hd["kk3"], 2, 2))
            a = jnp.where(row >= col, a, 0.0)
            o3 = _bdot(hd["qe3"], states_ref[g], 2, 2) + _bdot(a, hd["v3"], 2, 1)
            o_ref[:, hd["hs"]] = o3.reshape(rb, HG_D)

    wide = HG_GROUP * HG_D
    sec = lambda j: pl.BlockSpec((None, rb, wide), lambda h, i, j=j: (j, i, h))
    return pl.pallas_call(
        body, name="hgrn_fwd", grid=(HG_HEADS // HG_GROUP, nb),
        in_specs=[sec(0), sec(1), sec(2), pl.BlockSpec((2, wide), lambda h, i: (0, h))],
        out_specs=[pl.BlockSpec((rb, wide), lambda h, i: (i, h)),
                   pl.BlockSpec((None, HG_GROUP, HG_D, HG_D), lambda h, i: (i, h, 0, 0))],
        out_shape=[jax.ShapeDtypeStruct((s, SEC_W), F32),
                   jax.ShapeDtypeStruct((nb, HG_HEADS, HG_D, HG_D), F32)],
        scratch_shapes=[pltpu.VMEM((HG_GROUP, HG_D, HG_D), F32), pltpu.VMEM((HG_GROUP, nc, SLAB_ROWS, HG_D), MM),
                        pltpu.VMEM((HG_GROUP, nc, HG_D, HG_D), F32)],
        compiler_params=_params("parallel", "arbitrary"),
    )(proj, proj, proj, lb_logits)


def _hgrn_bwd(proj, lb_logits, d_o, sst, rb=256):
    s = proj.shape[1]
    nb, nc = s // rb, rb // CH

    def body(q_ref, f_ref, i_ref, lbl_ref, do_ref, sst_ref, dxq_ref, dxf_ref, dxi_ref, dlb_ref,
             dst_ref, states_ref, dstates_ref, lslab_ref, kslab_ref):
        @pl.when(pl.program_id(1) == 0)
        def _():
            dst_ref[...] = jnp.zeros_like(dst_ref)
            dlb_ref[...] = jnp.zeros_like(dlb_ref)

        prefix, suffix = _chunk_masks(rb)
        c3 = lambda a: a.reshape(nc, CH, HG_D)
        flat = lambda a: a.reshape(rb, HG_D)
        row, col = _iota2((nc, CH, CH), 1), _iota2((nc, CH, CH), 2)
        tril, triu = row >= col, row <= col
        sel = (_iota2((CH, CH * CH), 1) % CH == _iota2((CH, CH * CH), 0)).astype(MM)
        blockdiag = _iota2((nc, CH, CH * CH), 2) // CH == _iota2((nc, CH, CH * CH), 1)
        tile = lambda m: jnp.where(blockdiag, _dot(m.reshape(rb, CH), sel).reshape(nc, CH, CH * CH), 0.0)
        last = _iota2((nc, CH, HG_D), 1) == CH - 1
        heads = []
        for g in range(HG_GROUP):
            hs = slice(g * HG_D, (g + 1) * HG_D)
            lb, _ = _lower_bound(lbl_ref[:, hs])
            xq = q_ref[:, hs]
            sgq, q, sg, sn, f, kk = _hgrn_gates(xq, f_ref[:, hs], lb)
            b3 = c3(_tri_dot(prefix, jnp.log(f)))
            q3, kk3, v3, do3 = c3(q), c3(kk), c3(i_ref[:, hs]), c3(do_ref[:, hs])
            bl3 = b3[:, CH - 1:CH, :]
            eb3, ebl3, dec3 = jnp.exp(b3), jnp.exp(bl3), jnp.exp(bl3 - b3)
            qe3, kd3 = q3 * eb3, kk3 * dec3
            x_upd, y_upd = _bdot(v3, kd3, 1, 1), _bdot(do3, qe3, 1, 1)
            for t in range(CH):
                bt = b3[:, t:t + 1, :]
                lslab_ref[g, :, t * CH:(t + 1) * CH, :] = (q3 * jnp.exp(jnp.minimum(b3 - bt, 0.0))).astype(MM)
                kslab_ref[g, :, t * CH:(t + 1) * CH, :] = (kk3 * jnp.exp(jnp.minimum(bt - b3, 0.0))).astype(MM)
            d_a = jnp.where(tril, _bdot(do3, v3, 2, 2), 0.0)
            d_at = jnp.where(triu, _bdot(v3, do3, 2, 2), 0.0)
            heads.append(dict(hs=hs, lb=lb, xq=xq, sgq=sgq, sg=sg, sn=sn, f=f, q3=q3, kk3=kk3, v3=v3, do3=do3,
                              eb3=eb3, ebl3=ebl3, dec3=dec3, qe3=qe3, kd3=kd3, x_upd=x_upd, y_upd=y_upd,
                              d_a=d_a, d_at=d_at))
        for g, hd in enumerate(heads):
            st = sst_ref[g]
            for c in range(nc):
                states_ref[g, c] = st
                st = st * hd["ebl3"][c] + hd["x_upd"][c]
            dst = dst_ref[g]
            for c in reversed(range(nc)):
                dstates_ref[g, c] = dst
                dst = dst * hd["ebl3"][c] + hd["y_upd"][c]
            dst_ref[g] = dst
        for g, hd in enumerate(heads):
            q3, v3, do3, kd3 = hd["q3"], hd["v3"], hd["do3"], hd["kd3"]
            states, dstates = states_ref[g], dstates_ref[g]
            hd["dqe"] = _bdot(do3, states, 2, 1)
            hd["dkd"] = _bdot(v3, dstates, 2, 1)
            r = _bdot(kslab_ref[g], q3, 2, 2)
            a_t = jnp.zeros((nc, CH, CH), F32)
            for t in range(CH):
                a_t = a_t + jnp.where(col == t, r[:, t * CH:(t + 1) * CH, :], 0.0)
            a_t = jnp.where(triu, a_t, 0.0)
            hd["dv"] = _bdot(kd3, dstates, 2, 2) + _bdot(a_t, do3, 2, 1)
            hd["dq_in"] = _bdot(tile(hd["d_a"]), kslab_ref[g], 2, 1)
            hd["dk_in"] = _bdot(tile(hd["d_at"]), lslab_ref[g], 2, 1)
            hd["ss"] = jnp.sum(dstates * states, axis=1, keepdims=True)
        for g, hd in enumerate(heads):
            q3, kk3, eb3, ebl3, dec3, qe3, kd3 = (hd[k] for k in ("q3", "kk3", "eb3", "ebl3", "dec3", "qe3", "kd3"))
            dqe, dkd, dq_in, dk_in = hd["dqe"], hd["dkd"], hd["dq_in"], hd["dk_in"]
            dkd_kd = dkd * kd3
            db = dqe * qe3 - dkd_kd + q3 * dq_in - kk3 * dk_in
            dbl = jnp.sum(dkd_kd, axis=1, keepdims=True) + hd["ss"] * ebl3
            dg = _tri_dot(suffix, flat(db + jnp.where(last, dbl, 0.0)))
            df = dg / hd["f"] - flat(dkd * dec3 + dk_in)
            xq, sgq, hs = hd["xq"], hd["sgq"], hd["hs"]
            dxq_ref[:, hs] = (flat(dqe * eb3 + dq_in) * (sgq * (1.0 + xq * (1.0 - sgq)))).astype(MM)
            dxf_ref[:, hs] = (df * (1.0 - hd["lb"]) * hd["sg"] * hd["sn"]).astype(MM)
            dxi_ref[:, hs] = flat(hd["dv"]).astype(MM)
            dlb_ref[:, hs] += jnp.sum(df * hd["sn"], axis=0, keepdims=True)

    wide = HG_GROUP * HG_D
    rev = lambda i: nb - 1 - i
    sec = lambda j: pl.BlockSpec((None, rb, wide), lambda h, i, j=j: (j, rev(i), h))
    blk = pl.BlockSpec((rb, wide), lambda h, i: (rev(i), h))
    state = (pltpu.VMEM((HG_GROUP, nc, HG_D, HG_D), F32), pltpu.VMEM((HG_GROUP, nc, CH * CH, HG_D), MM))
    return pl.pallas_call(
        body, name="hgrn_bwd", grid=(HG_HEADS // HG_GROUP, nb),
        in_specs=[sec(0), sec(1), sec(2), pl.BlockSpec((2, wide), lambda h, i: (0, h)), blk,
                  pl.BlockSpec((None, HG_GROUP, HG_D, HG_D), lambda h, i: (rev(i), h, 0, 0))],
        out_specs=[blk, blk, blk, pl.BlockSpec((1, wide), lambda h, i: (0, h))],
        out_shape=[jax.ShapeDtypeStruct((s, SEC_W), MM)] * 3 + [jax.ShapeDtypeStruct((1, SEC_W), F32)],
        scratch_shapes=[pltpu.VMEM((HG_GROUP, HG_D, HG_D), F32), state[0], state[0], state[1], state[1]],
        compiler_params=_params("parallel", "arbitrary"),
    )(proj, proj, proj, lb_logits, d_o, sst)


def _rope_tables(s):
    half = AT_DH // 2
    inv_freq = np.float32(1.0) / (np.float32(ROPE_THETA) ** (np.arange(half, dtype=np.float32) / np.float32(half)))
    ang = np.arange(s, dtype=np.float32)[:, None] * inv_freq[None, :]
    cos, sin = np.cos(ang), np.sin(ang)
    return np.concatenate([cos] * 4, axis=-1), np.concatenate([-sin, sin] * 2, axis=-1)


def _rope128(x, cos, sin):
    lo = (_iota2(x.shape, 1) % AT_DH) < AT_DH // 2
    rot = jnp.where(lo, pltpu.roll(x, LANES - AT_DH // 2, 1), pltpu.roll(x, AT_DH // 2, 1))
    return x * cos + rot * sin


LANE_GROUPS = SEC_W // LANES


def _set_lanes(ref, val):
    for j in range(LANE_GROUPS):
        ref[j] = val[:, j * LANES:(j + 1) * LANES]


def _get_lanes(ref):
    return jnp.concatenate([ref[j] for j in range(LANE_GROUPS)], axis=-1)


def _to_view(src_ref, dst_ref, d):
    n = src_ref.shape[1] // d
    for r in range(d):
        rows = pl.ds(r, n, stride=d) if d > 1 else slice(None)
        for j in range(LANE_GROUPS):
            c0 = r * SEC_W + j * LANES
            dst_ref[:, c0:c0 + LANES] = src_ref.at[j][rows, :].astype(dst_ref.dtype)


def _from_view(src_ref, dst_ref, d):
    n = dst_ref.shape[1] // d
    for r in range(d):
        for j in range(LANE_GROUPS):
            c0 = r * SEC_W + j * LANES
            dst_ref.at[j][pl.ds(r, n, stride=d), :] = src_ref[:, c0:c0 + LANES].astype(dst_ref.dtype)


def _view_spec(tm, d):
    return pl.BlockSpec((tm // d, d * SEC_W), lambda i: (i, 0))


def _view_shape(s, d, dtype):
    return jax.ShapeDtypeStruct((s // d, d * SEC_W), dtype)


PROJ_KEPT = (0, 1, 2, 3, 7)


def _inproj_fwd(x, norm_w, w_all, cos, sin, tm=512):
    s = x.shape[0]

    def body(x_ref, nw_ref, w_ref, cos_ref, sin_ref, proj_ref, *refs):
        outs, (qs_ref, ks_ref, vs_ref) = refs[:-3], refs[-3:]
        xv = x_ref[...]
        rstd = lax.rsqrt(jnp.mean(xv * xv, axis=-1, keepdims=True) + NORM_EPS)
        u = (xv * rstd * nw_ref[...]).astype(MM)
        for slot, j in enumerate(PROJ_KEPT):
            proj_ref[slot] = jnp.dot(u, w_ref[j], preferred_element_type=F32)
        q, k, v = [jnp.dot(u, w_ref[j], preferred_element_type=F32) for j in (4, 5, 6)]
        c, sn = cos_ref[...], sin_ref[...]
        for g in range(LANE_GROUPS):
            sl = slice(g * LANES, (g + 1) * LANES)
            qs_ref[g] = _rope128(q[:, sl], c, sn) * (AT_DH ** -0.5)
            ks_ref[g] = _rope128(k[:, sl], c, sn)
            vs_ref[g] = v[:, sl]
        for i, d in enumerate(DILATIONS):
            for src_ref, dst_ref in zip((qs_ref, ks_ref, vs_ref), outs[3 * i:3 * i + 3]):
                _to_view(src_ref, dst_ref, d)

    tab = pl.BlockSpec((tm, LANES), lambda i: (i, 0))
    return pl.pallas_call(
        body, name="inproj_fwd", grid=(s // tm,),
        in_specs=[pl.BlockSpec((tm, D_MODEL), lambda i: (i, 0)),
                  pl.BlockSpec((1, D_MODEL), lambda i: (0, 0)),
                  pl.BlockSpec((N_SEC, D_MODEL, SEC_W), lambda i: (0, 0, 0)), tab, tab],
        out_specs=[pl.BlockSpec((len(PROJ_KEPT), tm, SEC_W), lambda i: (0, i, 0))]
                  + [_view_spec(tm, d) for d in DILATIONS for _ in range(3)],
        out_shape=[jax.ShapeDtypeStruct((len(PROJ_KEPT), s, SEC_W), F32)]
                  + [_view_shape(s, d, MM) for d in DILATIONS for _ in range(3)],
        scratch_shapes=[pltpu.VMEM((LANE_GROUPS, tm, LANES), F32)] * 3,
        compiler_params=_params("parallel"),
    )(x, norm_w, w_all, cos, sin)


def _band_mask(first_ok, second_ok):
    row, col = _iota2((ATT_BLK, 2 * ATT_BLK), 0), _iota2((ATT_BLK, 2 * ATT_BLK), 1)
    return ((col < ATT_BLK) & (col >= row) & first_ok) | ((col >= ATT_BLK) & ((col - ATT_BLK) <= row) & second_ok)


def _own_lanes(rows, h):
    lane = _iota2((rows, LANES), 1)
    return (lane < AT_DH) if h == 0 else (lane >= AT_DH)


def _neg_pieces(rows, h):
    lane = _iota2((rows, LANES), 1) - (AT_DH if h == 0 else 0)
    return jnp.where((lane >= 0) & (lane < 3), -1.0, 0.0).astype(MM)


def _units(qb):
    return [(b, slice(g * LANES, (g + 1) * LANES), h) for b in range(qb) for g in range(AT_COLS // LANES) for h in range(2)]


def _sub(b):
    return slice(b * ATT_BLK, (b + 1) * ATT_BLK)


def _band_before(cur_ref, prev_ref, b, sl):
    if b == 0:
        return jnp.concatenate([prev_ref[:, sl], cur_ref[0:ATT_BLK, sl]], axis=0)
    return cur_ref[(b - 1) * ATT_BLK:(b + 1) * ATT_BLK, sl]


def _band_after(cur_ref, next_ref, b, sl):
    if (b + 1) * ATT_BLK == cur_ref.shape[0]:
        return jnp.concatenate([cur_ref[b * ATT_BLK:(b + 1) * ATT_BLK, sl], next_ref[:, sl]], axis=0)
    return cur_ref[b * ATT_BLK:(b + 2) * ATT_BLK, sl]


def _attn_specs(rows):
    qb = min(AT_QB, rows // ATT_BLK)
    assert rows % (qb * ATT_BLK) == 0
    last = rows // ATT_BLK - 1
    cur = pl.BlockSpec((qb * ATT_BLK, AT_COLS), lambda c, n: (n, c))
    prev = pl.BlockSpec((ATT_BLK, AT_COLS), lambda c, n: (jnp.maximum(qb * n - 1, 0), c))
    nxt = pl.BlockSpec((ATT_BLK, AT_COLS), lambda c, n: (jnp.minimum(qb * (n + 1), last), c))
    return qb, cur, prev, nxt


def _stack_heads(a):
    h0 = _own_lanes(a.shape[0], 0)
    zero = jnp.zeros_like(a)
    return jnp.concatenate([jnp.where(h0, a, zero), jnp.where(h0, zero, a)], axis=0)


def _unstack_heads(a2):
    return jnp.where(_own_lanes(ATT_BLK, 0), a2[:ATT_BLK], a2[ATT_BLK:])


def _attn_fwd(qr, kr, vr, d):
    rows, cols = qr.shape
    qb, cur, prev, nxt = _attn_specs(rows)
    nb = rows // (qb * ATT_BLK)

    def body(q_ref, kc_ref, kp_ref, vc_ref, vp_ref, o_ref, lse_ref):
        twice = lambda m: jnp.concatenate([m, m], axis=0)
        masks = {True: twice(_band_mask(pl.program_id(1) > 0, True)), False: twice(_band_mask(True, True))}
        ones = jnp.ones((2 * ATT_BLK, LANES), MM)
        units = [(b, sl) for b, sl, h in _units(qb) if h == 0]
        scs = [jnp.where(masks[b == 0], _dot_nt(_stack_heads(q_ref[_sub(b), sl]), _band_before(kc_ref, kp_ref, b, sl)),
                         NEG) for b, sl in units]
        ms = [jnp.max(sc, axis=-1, keepdims=True) for sc in scs]
        ps = [jnp.exp(sc - m).astype(MM) for sc, m in zip(scs, ms)]
        ols = [jnp.dot(p, jnp.concatenate([_band_before(vc_ref, vp_ref, b, sl), ones], axis=1),
                       preferred_element_type=F32) for p, (b, sl) in zip(ps, units)]
        for (b, sl), m, ol in zip(units, ms, ols):
            l = _unstack_heads(ol[:, LANES:])
            o_ref[_sub(b), sl] = _unstack_heads(ol[:, :LANES]) / l
            lse_ref[_sub(b), sl] = _unstack_heads(jnp.broadcast_to(m, (2 * ATT_BLK, LANES))) + jnp.log(l)

    o, lse = pl.pallas_call(
        body, name=f"attn_fwd_d{d}", grid=(cols // AT_COLS, nb),
        in_specs=[cur, cur, prev, cur, prev], out_specs=[cur, cur],
        out_shape=[jax.ShapeDtypeStruct((rows, cols), F32)] * 2,
        compiler_params=_params("parallel", "parallel"),
    )(qr, kr, kr, vr, vr)
    return o, lse


def _attn_bwd_dq(qr, kr, vr, do, lse, delta, d):
    rows, cols = qr.shape
    qb, cur, prev, nxt = _attn_specs(rows)
    nb = rows // (qb * ATT_BLK)

    def body(q_ref, kc_ref, kp_ref, vc_ref, vp_ref, do_ref, lse_ref, dl_ref, dq_ref):
        masks = {True: _band_mask(pl.program_id(1) > 0, True), False: _band_mask(True, True)}
        units = _units(qb)
        sms, dps = [], []
        for b, sl, h in units:
            own, own_b, neg = _own_lanes(ATT_BLK, h), _own_lanes(2 * ATT_BLK, h), _neg_pieces(2 * ATT_BLK, h)
            sms.append(_dot_nt(jnp.where(own, q_ref[_sub(b), sl], lse_ref[_sub(b), sl]),
                               jnp.where(own_b, _band_before(kc_ref, kp_ref, b, sl), neg)))
            dps.append(_dot_nt(jnp.where(own, do_ref[_sub(b), sl], dl_ref[_sub(b), sl]),
                               jnp.where(own_b, _band_before(vc_ref, vp_ref, b, sl), neg)))
        dss = [(jnp.exp(jnp.where(masks[b == 0], sm, NEG)) * dp).astype(MM)
               for sm, dp, (b, _, _) in zip(sms, dps, units)]
        dqs = [jnp.dot(ds, _band_before(kc_ref, kp_ref, b, sl), preferred_element_type=F32) * (AT_DH ** -0.5)
               for ds, (b, sl, _) in zip(dss, units)]
        for i in range(0, len(units), 2):
            b, sl, _ = units[i]
            dq_ref[_sub(b), sl] = jnp.where(_own_lanes(ATT_BLK, 0), dqs[i], dqs[i + 1]).astype(dq_ref.dtype)

    dq = pl.pallas_call(
        body, name=f"attn_bwd_dq_d{d}", grid=(cols // AT_COLS, nb),
        in_specs=[cur, cur, prev, cur, prev, cur, cur, cur], out_specs=cur,
        out_shape=jax.ShapeDtypeStruct((rows, cols), MM),
        compiler_params=_params("parallel", "parallel"),
    )(qr, kr, kr, vr, vr, do, lse, delta)
    return dq


def _attn_bwd_dkv(qr, kr, vr, do, lse, delta, d):
    rows, cols = qr.shape
    qb, cur, prev, nxt = _attn_specs(rows)
    nb = rows // (qb * ATT_BLK)

    def body(k_ref, v_ref, qc_ref, qn_ref, doc_ref, don_ref, lsec_ref, lsen_ref, dlc_ref, dln_ref,
             dk_ref, dv_ref):
        masks = {True: _band_mask(True, pl.program_id(1) < nb - 1), False: _band_mask(True, True)}
        units = _units(qb)
        sms, dps = [], []
        for b, sl, h in units:
            own, own_b, neg = _own_lanes(ATT_BLK, h), _own_lanes(2 * ATT_BLK, h), _neg_pieces(ATT_BLK, h)
            sms.append(_dot_nt(jnp.where(own, k_ref[_sub(b), sl], neg),
                               jnp.where(own_b, _band_after(qc_ref, qn_ref, b, sl),
                                         _band_after(lsec_ref, lsen_ref, b, sl))))
            dps.append(_dot_nt(jnp.where(own, v_ref[_sub(b), sl], neg),
                               jnp.where(own_b, _band_after(doc_ref, don_ref, b, sl),
                                         _band_after(dlc_ref, dln_ref, b, sl))))
        ps = [jnp.exp(jnp.where(masks[b == qb - 1], sm, NEG)) for sm, (b, _, _) in zip(sms, units)]
        dss = [(p * dp).astype(MM) for p, dp in zip(ps, dps)]
        dvs = [jnp.dot(p.astype(MM), _band_after(doc_ref, don_ref, b, sl), preferred_element_type=F32)
               for p, (b, sl, _) in zip(ps, units)]
        dks = [jnp.dot(ds, _band_after(qc_ref, qn_ref, b, sl), preferred_element_type=F32)
               for ds, (b, sl, _) in zip(dss, units)]
        head0 = _own_lanes(ATT_BLK, 0)
        for i in range(0, len(units), 2):
            b, sl, _ = units[i]
            dk_ref[_sub(b), sl] = jnp.where(head0, dks[i], dks[i + 1]).astype(dk_ref.dtype)
            dv_ref[_sub(b), sl] = jnp.where(head0, dvs[i], dvs[i + 1]).astype(dv_ref.dtype)

    dk, dv = pl.pallas_call(
        body, name=f"attn_bwd_dkv_d{d}", grid=(cols // AT_COLS, nb),
        in_specs=[cur, cur, cur, nxt, cur, nxt, cur, nxt, cur, nxt], out_specs=[cur, cur],
        out_shape=[jax.ShapeDtypeStruct((rows, cols), MM)] * 2,
        compiler_params=_params("parallel", "parallel"),
    )(kr, vr, qr, qr, do, do, lse, lse, delta, delta)
    return dk, dv


def _head_sum(a, width):
    parts = []
    for j in range(a.shape[1] // width):
        sm = jnp.sum(a[:, j * width:(j + 1) * width], axis=-1, keepdims=True)
        parts.append(jnp.broadcast_to(sm, (a.shape[0], width)))
    return jnp.concatenate(parts, axis=-1)


def _partner_pieces(x):
    xs = jnp.concatenate([pltpu.roll(x[:, j * LANES:(j + 1) * LANES], AT_DH, 1) for j in range(x.shape[1] // LANES)],
                         axis=-1)
    hi = xs.astype(jnp.bfloat16).astype(F32)
    mid = (xs - hi).astype(jnp.bfloat16).astype(F32)
    lo = (xs - hi - mid).astype(jnp.bfloat16).astype(F32)
    lane = _iota2(x.shape, 1) % AT_DH
    return jnp.where(lane == 0, hi, jnp.where(lane == 1, mid, jnp.where(lane == 2, lo, 0.0)))


def _mid(x, tgt, proj, o_hg, o_at, lse_at, hg_norm_w, final_norm_w, wo_all, tm=256):
    s = x.shape[0]
    nb = s // tm

    def body(x_ref, t_ref, hgz_ref, atz_ref, ohg_ref, o1_ref, o2_ref, o3_ref, l1_ref, l2_ref, l3_ref,
             g_ref, fw_ref, wo_ref,
             dh_ref, dohg_ref, dhgz_ref, datz_ref, do1_ref, do2_ref, do3_ref, dl1_ref, dl2_ref, dl3_ref,
             lp1_ref, lp2_ref, lp3_ref,
             gwo_ref, gfw_ref, ghg_ref, loss_ref, nat_ref, stage_ref, gwo_acc):
        @pl.when(pl.program_id(0) == 0)
        def _():
            gwo_acc[...] = jnp.zeros_like(gwo_acc)
            gfw_ref[...] = jnp.zeros_like(gfw_ref)
            ghg_ref[...] = jnp.zeros_like(ghg_ref)
            loss_ref[...] = jnp.zeros_like(loss_ref)

        ohg, g = ohg_ref[...], g_ref[...]
        rs = lax.rsqrt(_head_sum(ohg * ohg, HG_D) * (1.0 / HG_D) + NORM_EPS)
        on = ohg * rs
        hgz = hgz_ref[...]
        sz = _sigmoid(hgz)
        gate_hg = hgz * sz
        lses, outs = [l1_ref[...]], [o1_ref[...]]
        for k, (d, l_ref, o_ref) in enumerate(zip(DILATIONS[1:], (l2_ref, l3_ref), (o2_ref, o3_ref))):
            _from_view(l_ref, nat_ref.at[2 * k], d)
            _from_view(o_ref, nat_ref.at[2 * k + 1], d)
            lses.append(_get_lanes(nat_ref.at[2 * k]))
            outs.append(_get_lanes(nat_ref.at[2 * k + 1]))
        mx = jnp.maximum(jnp.maximum(lses[0], lses[1]), lses[2])
        es = [jnp.exp(l - mx) for l in lses]
        den = es[0] + es[1] + es[2]
        ws = [e / den for e in es]
        oat = ws[0] * outs[0] + ws[1] * outs[1] + ws[2] * outs[2]
        atz = atz_ref[...]
        sa = _sigmoid(atz)
        gate_at = atz * sa
        mixed = jnp.concatenate([on * g * gate_hg, oat * gate_at], axis=-1).astype(MM)
        h = x_ref[...] + jnp.dot(mixed, wo_ref[...], preferred_element_type=F32)
        rstd = lax.rsqrt(jnp.mean(h * h, axis=-1, keepdims=True) + NORM_EPS)
        hn = h * rstd
        fw = fw_ref[...]
        err = hn * fw - t_ref[...]
        loss_ref[...] += 0.5 * jnp.sum(jnp.mean(err * err, axis=-1, keepdims=True), axis=0, keepdims=True)
        dout = err * (1.0 / D_MODEL)
        gfw_ref[...] += jnp.sum(dout * hn, axis=0, keepdims=True)
        dhn = dout * fw
        dh = rstd * (dhn - hn * jnp.mean(dhn * hn, axis=-1, keepdims=True))
        dh_ref[...] = dh
        dh_mm = dh.astype(MM)
        gwo_acc[...] += _dot_tn(mixed, dh_mm)

        @pl.when(pl.program_id(0) == nb - 1)
        def _():
            gwo_ref[...] = gwo_acc[...].astype(gwo_ref.dtype)

        dmixed = _dot_nt(dh_mm, wo_ref[...])
        dm_hg = dmixed[:, :SEC_W]
        d_ong = dm_hg * gate_hg
        dhgz_ref[...] = (dm_hg * (on * g) * (sz * (1.0 + hgz * (1.0 - sz)))).astype(MM)
        ghg_ref[...] += jnp.sum(d_ong * on, axis=0, keepdims=True)
        d_on = d_ong * g
        dohg_ref[...] = rs * (d_on - on * (_head_sum(d_on * on, HG_D) * (1.0 / HG_D)))
        dm_at = dmixed[:, SEC_W:]
        d_oat = dm_at * gate_at
        datz_ref[...] = (dm_at * oat * (sa * (1.0 + atz * (1.0 - sa)))).astype(MM)
        drow = _head_sum(d_oat * oat, AT_DH)
        lse_all = mx + jnp.log(den)
        for val, dst_refs in ((d_oat, (do1_ref, do2_ref, do3_ref)),
                              (_partner_pieces(drow), (dl1_ref, dl2_ref, dl3_ref)),
                              (_partner_pieces(lse_all), (lp1_ref, lp2_ref, lp3_ref))):
            _set_lanes(stage_ref, val)
            for d, dst_ref in zip(DILATIONS, dst_refs):
                _to_view(stage_ref, dst_ref, d)

    row = lambda w: pl.BlockSpec((tm, w), lambda i: (i, 0))
    sec = lambda j: pl.BlockSpec((None, tm, SEC_W), lambda i, j=j: (j, i, 0))
    const = lambda shp: pl.BlockSpec(shp, lambda i: (0,) * len(shp))
    half = row(SEC_W)
    views = [_view_spec(tm, d) for d in DILATIONS]
    return pl.pallas_call(
        body, name="mid", grid=(nb,),
        in_specs=[row(D_MODEL), row(D_MODEL), sec(PROJ_KEPT.index(3)), sec(PROJ_KEPT.index(7)), half] + views * 2
                 + [const((1, SEC_W)), const((1, D_MODEL)), const((D_MODEL, D_MODEL))],
        out_specs=[row(D_MODEL)] + [half] * 3 + views * 3
                  + [const((D_MODEL, D_MODEL)), const((1, D_MODEL)), const((1, SEC_W)), const((1, 1))],
        out_shape=[jax.ShapeDtypeStruct((s, D_MODEL), F32), jax.ShapeDtypeStruct((s, SEC_W), F32)]
                  + [jax.ShapeDtypeStruct((s, SEC_W), MM)] * 2
                  + [_view_shape(s, d, MM) for d in DILATIONS] * 3
                  + [jax.ShapeDtypeStruct((D_MODEL, D_MODEL), XCH), jax.ShapeDtypeStruct((1, D_MODEL), F32),
                     jax.ShapeDtypeStruct((1, SEC_W), F32), jax.ShapeDtypeStruct((1, 1), F32)],
        scratch_shapes=[pltpu.VMEM((4, LANE_GROUPS, tm, LANES), F32), pltpu.VMEM((LANE_GROUPS, tm, LANES), F32),
                        pltpu.VMEM((D_MODEL, D_MODEL), F32)],
        compiler_params=_params("arbitrary"),
    )(x, tgt, proj, proj, o_hg, *o_at, *lse_at, hg_norm_w, final_norm_w, wo_all)


def _section_specs(dsecs, tm):
    return [pl.BlockSpec((tm, SEC_W), lambda i: (i, 0)) if k is None
            else pl.BlockSpec((None, tm, SEC_W), lambda i, k=k: (k, i, 0)) for _, k in dsecs]


def _inproj_bwd_x(x, norm_w, w_all, dh, dsecs, token, tm=512):
    s = x.shape[0]

    def body(x_ref, nw_ref, w_ref, dh_ref, tok_ref, *refs):
        sec_refs, (gx_ref, gnw_ref) = refs[:N_SEC], refs[N_SEC:]

        @pl.when(pl.program_id(0) == 0)
        def _():
            gnw_ref[...] = jnp.zeros_like(gnw_ref)

        du = jnp.zeros((tm, D_MODEL), F32)
        for j in range(N_SEC):
            du = du + _dot_nt(sec_refs[j][...], w_ref[j])
        xv, nw = x_ref[...], nw_ref[...]
        rstd = lax.rsqrt(jnp.mean(xv * xv, axis=-1, keepdims=True) + NORM_EPS)
        xn = xv * rstd
        gnw_ref[...] += jnp.sum(du * xn, axis=0, keepdims=True)
        dxn = du * nw
        dx = rstd * (dxn - xn * jnp.mean(dxn * xn, axis=-1, keepdims=True))
        gx_ref[...] = (dh_ref[...] + tok_ref[0:1, 0:1]) + dx

    row = lambda w: pl.BlockSpec((tm, w), lambda i: (i, 0))
    const = lambda shp: pl.BlockSpec(shp, lambda i: (0,) * len(shp))
    return pl.pallas_call(
        body, name="inproj_bwd_x", grid=(s // tm,),
        in_specs=[row(D_MODEL), const((1, D_MODEL)), const((N_SEC, D_MODEL, SEC_W)), row(D_MODEL), const((8, 128))]
                 + _section_specs(dsecs, tm),
        out_specs=[row(D_MODEL), const((1, D_MODEL))],
        out_shape=[jax.ShapeDtypeStruct((s, D_MODEL), F32), jax.ShapeDtypeStruct((1, D_MODEL), F32)],
        compiler_params=_params("arbitrary"),
    )(x, norm_w, w_all, dh, token, *[a for a, _ in dsecs])


def _inproj_bwd_w(x, norm_w, dsec, dq_r, dk_r, dv, cos, sin, tm=512):
    s = x.shape[0]
    nb = s // tm

    def body(x_ref, nw_ref, s0, s1, s2, s3, s7, q1, q2, q3, k1, k2, k3, v1, v2, v3, cos_ref, sin_ref,
             gw_hbm, datt_ref, acc_ref, stage_ref, nat_ref):
        @pl.when(pl.program_id(0) == 0)
        def _():
            acc_ref[...] = jnp.zeros_like(acc_ref)

        def total(refs):
            acc = refs[0][...].astype(F32)
            for d, ref in zip(DILATIONS[1:], refs[1:]):
                _from_view(ref, nat_ref, d)
                acc = acc + _get_lanes(nat_ref)
            return acc

        c, sn = cos_ref[...], -sin_ref[...]
        unrot = lambda a: jnp.concatenate(
            [_rope128(a[:, j * LANES:(j + 1) * LANES], c, sn) for j in range(LANE_GROUPS)], axis=-1)
        att = [a.astype(MM) for a in (unrot(total((q1, q2, q3))), unrot(total((k1, k2, k3))), total((v1, v2, v3)))]
        for j, a in enumerate(att):
            datt_ref[j] = a
        xv = x_ref[...]
        rstd = lax.rsqrt(jnp.mean(xv * xv, axis=-1, keepdims=True) + NORM_EPS)
        u_t = (xv * rstd * nw_ref[...]).T.astype(MM)
        for j, dsj in enumerate((s0[...], s1[...], s2[...], s3[...], *att, s7[...])):
            acc_ref[j] += jnp.dot(u_t, dsj, preferred_element_type=F32)

        @pl.when(pl.program_id(0) == nb - 1)
        def _():
            for j in range(N_SEC):
                stage_ref[...] = acc_ref[j].astype(stage_ref.dtype)
                pltpu.sync_copy(stage_ref, gw_hbm.at[j])

    row = lambda w: pl.BlockSpec((tm, w), lambda i: (i, 0))
    return pl.pallas_call(
        body, name="inproj_bwd_w", grid=(nb,),
        in_specs=[row(D_MODEL), pl.BlockSpec((1, D_MODEL), lambda i: (0, 0))] + [row(SEC_W)] * 5
                 + [_view_spec(tm, d) for d in DILATIONS] * 3 + [row(LANES), row(LANES)],
        out_specs=[pl.BlockSpec(memory_space=pl.ANY), pl.BlockSpec((3, tm, SEC_W), lambda i: (0, i, 0))],
        out_shape=[jax.ShapeDtypeStruct((N_SEC, D_MODEL, SEC_W), XCH), jax.ShapeDtypeStruct((3, s, SEC_W), MM)],
        scratch_shapes=[pltpu.VMEM((N_SEC, D_MODEL, SEC_W), F32), pltpu.VMEM((D_MODEL, SEC_W), XCH),
                        pltpu.VMEM((LANE_GROUPS, tm, LANES), F32)],
        compiler_params=_params("arbitrary"),
    )(x, norm_w, *dsec, *dq_r, *dk_r, *dv, cos, sin)


def _local_step(x, tgt, norm_w, w_all, lb_logits, hg_norm_w, wo_all, final_norm_w, on_weight_grads):
    s = x.shape[0]
    cos, sin = _rope_tables(s)
    proj, *qkv = _inproj_fwd(x, norm_w, w_all, cos, sin)
    o_hg, sst = _hgrn_fwd(proj, lb_logits)
    qkv = [qkv[3 * i:3 * i + 3] for i in range(len(DILATIONS))]
    att = [_attn_fwd(*qkv_d, d) for qkv_d, d in zip(qkv, DILATIONS)]
    (dh, d_ohg, d_hgz, d_atz, do1, do2, do3, dl1, dl2, dl3, lp1, lp2, lp3, gwo, gfw, ghg, loss) = _mid(
        x, tgt, proj, o_hg, [a[0] for a in att], [a[1] for a in att], hg_norm_w, final_norm_w[None, :], wo_all)
    dxq, dxf, dxi, dlb = _hgrn_bwd(proj, lb_logits, d_ohg, sst)
    dq_r, dk_r, dv = [], [], []
    for d, qkv_d, do, lp, dl in zip(DILATIONS, qkv, (do1, do2, do3), (lp1, lp2, lp3), (dl1, dl2, dl3)):
        dq_r.append(_attn_bwd_dq(*qkv_d, do, lp, dl, d))
        dk_d, dv_d = _attn_bwd_dkv(*qkv_d, do, lp, dl, d)
        dk_r.append(dk_d)
        dv.append(dv_d)
    gwi, d_att = _inproj_bwd_w(x, norm_w, (dxq, dxf, dxi, d_hgz, d_atz), dq_r, dk_r, dv, cos, sin)
    dsecs = [(dxq, None), (dxf, None), (dxi, None), (d_hgz, None), (d_att, 0), (d_att, 1), (d_att, 2), (d_atz, None)]
    token = on_weight_grads(gwi, gwo)
    gx, gnw = _inproj_bwd_x(x, norm_w, w_all, dh, dsecs, token)
    small = jnp.concatenate([gnw, jnp.concatenate([dlb, ghg], axis=-1), gfw,
                             jnp.pad(loss, ((0, 0), (0, D_MODEL - 1)))], axis=0)
    return gx, gwi, gwo, small


def _coords():
    return lax.axis_index("x"), lax.axis_index("y"), lax.axis_index("c")


def _gather_weights(w_in, w_out):
    wo_rows = w_out.shape[0]

    def body(wi_ref, wo_ref, wi_all, wo_all, send_sems, recv_sems):
        x, y, c = _coords()
        me, sibling = (x, y, c), (x, y, 1 - c)
        chips = [(1 - x, y), (x, 1 - y), (1 - x, 1 - y)]
        slot = lambda p: 4 * p[0] + 2 * p[1] + p[2]

        def copies(k, block, to):
            return [pltpu.make_async_remote_copy(
                src_ref=ref.at[slot(block)], dst_ref=ref.at[slot(block)], send_sem=send_sems.at[a, k],
                recv_sem=recv_sems.at[a, k], device_id=to, device_id_type=MESH)
                for a, ref in enumerate((wi_all, wo_all))]

        wi_all[slot(me)] = wi_ref[...].astype(MM)
        wo_all[slot(me)] = wo_ref[...].astype(MM)
        first = copies(0, me, sibling)
        for j, chip in enumerate(chips):
            first += copies(1 + j, me, (*chip, c))
        for cp in first:
            cp.start()
        passed = []
        for j, chip in enumerate(chips):
            for cp in copies(1 + j, (*chip, c), me):
                cp.wait_recv()
            fwd = copies(4 + j, (*chip, c), sibling)
            for cp in fwd:
                cp.start()
            passed += fwd
        for cp in copies(0, sibling, me):
            cp.wait_recv()
        for j, chip in enumerate(chips):
            for cp in copies(4 + j, (*chip, 1 - c), me):
                cp.wait_recv()
        for cp in first + passed:
            cp.wait_send()

    vmem = pl.BlockSpec(memory_space=pltpu.VMEM)
    return pl.pallas_call(
        body, name="gather_weights",
        in_specs=[vmem, vmem], out_specs=[vmem, vmem],
        out_shape=[jax.ShapeDtypeStruct((N_DEV, D_MODEL, SEC_W), MM),
                   jax.ShapeDtypeStruct((N_DEV, wo_rows, D_MODEL), MM)],
        scratch_shapes=[pltpu.SemaphoreType.DMA((2, 7)), pltpu.SemaphoreType.DMA((2, 7))],
        compiler_params=pltpu.CompilerParams(vmem_limit_bytes=VMEM_LIMIT),
    )(w_in, w_out)


def _me():
    x, y, c = _coords()
    return 4 * x + 2 * y + c


def _grad_copies(srcs, lands, send_sems, recv_sems):
    x, y, c = _coords()
    me = 4 * x + 2 * y + c
    copies = []
    for k in range(1, N_DEV):
        px, py, pc = x ^ (k >> 2), y ^ ((k >> 1) & 1), c ^ (k & 1)
        peer = 4 * px + 2 * py + pc
        for a, (src, dst) in enumerate(zip(srcs, lands)):
            copies.append(pltpu.make_async_remote_copy(
                src_ref=src.at[peer], dst_ref=dst.at[me], send_sem=send_sems.at[a * (N_DEV - 1) + k - 1],
                recv_sem=recv_sems.at[a * (N_DEV - 1) + k - 1], device_id=(px, py, pc), device_id_type=MESH))
    return copies


HBM_SPEC = pl.BlockSpec(memory_space=pltpu.HBM)
SEM_SPEC = pl.BlockSpec(memory_space=pltpu.SEMAPHORE)
SPLIT_COPY_EFFECT = pltpu.SideEffectType.DATAFLOW_SIDE_EFFECTING


def _exchange_start(gwi, gwo):
    def body(gwi_ref, gwo_ref, li_ref, lo_ref, send_sems, recv_sems, gwi_thru, gwo_thru, li_thru, lo_thru, token):
        for cp in _grad_copies((gwi_ref, gwo_ref), (li_ref, lo_ref), send_sems, recv_sems):
            cp.start()
        token[...] = jnp.zeros_like(token)

    hbm = lambda a: pltpu.with_memory_space_constraint(a, pltpu.HBM)
    bufs = (gwi, gwo, lax.empty(gwi.shape, gwi.dtype), lax.empty(gwo.shape, gwo.dtype))
    return pl.pallas_call(
        body, name="exchange_start",
        out_shape=(pltpu.SemaphoreType.DMA((2 * (N_DEV - 1),)), pltpu.SemaphoreType.DMA((2 * (N_DEV - 1),)),
                   *[pltpu.HBM(a.shape, a.dtype) for a in bufs], jax.ShapeDtypeStruct((8, 128), F32)),
        in_specs=[HBM_SPEC] * 4,
        out_specs=(SEM_SPEC, SEM_SPEC, HBM_SPEC, HBM_SPEC, HBM_SPEC, HBM_SPEC, pl.BlockSpec(memory_space=pltpu.VMEM)),
        input_output_aliases={0: 2, 1: 3, 2: 4, 3: 5},
        compiler_params=pltpu.CompilerParams(has_side_effects=SPLIT_COPY_EFFECT),
    )(*[hbm(a) for a in bufs])


def _exchange_wait(send_sems, recv_sems, gwi, gwo, li, lo, after):
    def body(gwi_ref, gwo_ref, li_ref, lo_ref, send_sems, recv_sems, after_ref, gwi_out, gwo_out, li_out, lo_out):
        for cp in _grad_copies((gwi_ref, gwo_ref), (li_ref, lo_ref), send_sems, recv_sems):
            cp.wait_send()
            cp.wait_recv()

    return pl.pallas_call(
        body, name="exchange_wait",
        out_shape=tuple(pltpu.HBM(a.shape, a.dtype) for a in (gwi, gwo, li, lo)),
        in_specs=[HBM_SPEC] * 4 + [SEM_SPEC, SEM_SPEC, pl.BlockSpec(memory_space=pl.ANY)],
        out_specs=(HBM_SPEC,) * 4,
        input_output_aliases={0: 0, 1: 1, 2: 2, 3: 3},
        compiler_params=pltpu.CompilerParams(has_side_effects=SPLIT_COPY_EFFECT),
    )(gwi, gwo, li, lo, send_sems, recv_sems, after)


def _gather_small(small):
    def body(sm_ref, ls_ref, send_sems, recv_sems, local_sem):
        x, y, c = _coords()
        me = 4 * x + 2 * y + c
        own = pltpu.make_async_copy(sm_ref, ls_ref.at[me], local_sem)
        own.start()
        sends = []
        for k in range(1, N_DEV):
            peer = (x ^ (k >> 2), y ^ ((k >> 1) & 1), c ^ (k & 1))
            sends.append(pltpu.make_async_remote_copy(
                src_ref=sm_ref, dst_ref=ls_ref.at[me], send_sem=send_sems.at[k - 1], recv_sem=recv_sems.at[k - 1],
                device_id=peer, device_id_type=MESH))
        for cp in sends:
            cp.start()
        for cp in sends:
            cp.wait_recv()
        for cp in sends:
            cp.wait_send()
        own.wait()

    vmem = pl.BlockSpec(memory_space=pltpu.VMEM)
    return pl.pallas_call(
        body, name="gather_small", in_specs=[vmem], out_specs=vmem,
        out_shape=jax.ShapeDtypeStruct((N_DEV,) + small.shape, F32),
        scratch_shapes=[pltpu.SemaphoreType.DMA((N_DEV - 1,)), pltpu.SemaphoreType.DMA((N_DEV - 1,)),
                        pltpu.SemaphoreType.DMA],
    )(small)


def _adamw(w, g, m, v):
    m = ADAM_B1 * m + (1.0 - ADAM_B1) * g
    v = ADAM_B2 * v + (1.0 - ADAM_B2) * (g * g)
    m_hat = m / (1.0 - ADAM_B1 ** ADAM_STEP)
    v_hat = v / (1.0 - ADAM_B2 ** ADAM_STEP)
    return -ADAM_LR * (m_hat / (jnp.sqrt(v_hat) + ADAM_EPS) + ADAM_WD * w), m, v


def _slot_sum(ref, own=None, me=None):
    g = None
    for i in range(N_DEV):
        term = ref[i].astype(F32)
        if own is not None:
            term = jnp.where(i == me, own, term)
        g = term if g is None else g + term
    return g


def _update_matrix(name, me, landed, own, w, m, v, rows):
    r, c = w.shape

    def body(me_ref, l_ref, own_ref, w_ref, m_ref, v_ref, g_ref, d_ref, nm_ref, nv_ref):
        g = _slot_sum(l_ref, own_ref[...].astype(F32), me_ref[0])
        g_ref[...] = g
        d_ref[...], nm_ref[...], nv_ref[...] = _adamw(w_ref[...], g, m_ref[...], v_ref[...])

    blk = pl.BlockSpec((rows, c), lambda i, me_ref: (i, 0))
    return pl.pallas_call(
        body, name=name,
        grid_spec=pltpu.PrefetchScalarGridSpec(
            num_scalar_prefetch=1, grid=(r // rows,),
            in_specs=[pl.BlockSpec((N_DEV, rows, c), lambda i, me_ref: (0, i, 0)),
                      pl.BlockSpec((None, rows, c), lambda i, me_ref: (me_ref[0], i, 0)), blk, blk, blk],
            out_specs=[blk] * 4),
        out_shape=[jax.ShapeDtypeStruct((r, c), F32)] * 4,
        compiler_params=_params("parallel"),
    )(me, landed, own, w, m, v)


def _update_small(landed, lb_logits, ws, ms, vs):
    def body(l_ref, lbl_ref, w_ref, m_ref, v_ref, g_ref, d_ref, nm_ref, nv_ref, loss_ref):
        tot = _slot_sum(l_ref)
        _, dlb = _lower_bound(lbl_ref[...])
        g_lb = tot[1:2, :SEC_W] * dlb
        g = jnp.concatenate([tot[0:1], jnp.concatenate([g_lb, -g_lb], axis=-1),
                             jnp.pad(tot[1:2, SEC_W:], ((0, 0), (0, SEC_W))), tot[2:3]], axis=0)
        g_ref[...] = g
        d_ref[...], nm_ref[...], nv_ref[...] = _adamw(w_ref[...], g, m_ref[...], v_ref[...])
        loss_ref[...] = tot[3:4, 0:1]

    vmem = pl.BlockSpec(memory_space=pltpu.VMEM)
    return pl.pallas_call(
        body, name="update_small", in_specs=[vmem] * 5, out_specs=[vmem] * 5,
        out_shape=[jax.ShapeDtypeStruct((4, D_MODEL), F32)] * 4 + [jax.ShapeDtypeStruct((1, 1), F32)],
    )(landed, lb_logits, ws, ms, vs)


def _pack_small(norm_w, lb_logits, hg_norm_w, final_norm_w):
    return jnp.concatenate([norm_w, lb_logits.reshape(1, D_MODEL),
                            jnp.pad(hg_norm_w, ((0, 0), (0, D_MODEL - SEC_W))), final_norm_w[None, :]], axis=0)


def _unpack_small(a):
    return a[0:1], a[1].reshape(2, SEC_W), a[2:3, :SEC_W], a[3]


def kernel(x, norm_w, w_in, hgrn_lb_logits, hg_norm_w, w_out, final_norm_w, loss_target, m_norm_w, m_w_in, m_hgrn_lb_logits, m_hg_norm_w, m_w_out, m_final_norm_w, v_norm_w, v_w_in, v_hgrn_lb_logits, v_hg_norm_w, v_w_out, v_final_norm_w):
    w_all, wo_all = _gather_weights(w_in[0], w_out[0])
    in_flight = []

    def start_exchange(gwi, gwo):
        *handles, token = _exchange_start(gwi, gwo.reshape(N_DEV, D_MODEL // N_DEV, D_MODEL))
        in_flight.extend(handles)
        return token

    gx, _, _, small = _local_step(x[0], loss_target[0], norm_w, w_all, hgrn_lb_logits, hg_norm_w,
                                  wo_all.reshape(D_MODEL, D_MODEL), final_norm_w, start_exchange)
    ls = _gather_small(small)
    gwi, gwo, li, lo = _exchange_wait(*in_flight, gx)
    me = _me().astype(jnp.int32).reshape(1)
    g_wi, d_wi, nm_wi, nv_wi = _update_matrix("update_w_in", me, li, gwi, w_in[0], m_w_in[0], v_w_in[0], 256)
    g_wo, d_wo, nm_wo, nv_wo = _update_matrix("update_w_out", me, lo, gwo, w_out[0], m_w_out[0], v_w_out[0], 128)
    g_s, d_s, nm_s, nv_s, loss = _update_small(
        ls, hgrn_lb_logits, _pack_small(norm_w, hgrn_lb_logits, hg_norm_w, final_norm_w),
        _pack_small(m_norm_w, m_hgrn_lb_logits, m_hg_norm_w, m_final_norm_w),
        _pack_small(v_norm_w, v_hgrn_lb_logits, v_hg_norm_w, v_final_norm_w))
    outs = []
    for small_out, wi, wo in ((g_s, g_wi, g_wo), (d_s, d_wi, d_wo), (nm_s, nm_wi, nm_wo), (nv_s, nv_wi, nv_wo)):
        nw, lb, hg, fw = _unpack_small(small_out)
        outs += [nw, wi[None], lb, hg, wo[None], fw]
    return (loss[0, 0], gx[None], *outs)
```

```python
import functools

import jax
import jax.numpy as jnp
import numpy as np
from jax import lax
from jax.experimental import pallas as pl
from jax.experimental.pallas import tpu as pltpu

F32 = jnp.float32
MM = jnp.bfloat16
XCH = jnp.bfloat16
NORM_EPS = 1e-6
NEG = -1e30
N_DEV = 8
D_MODEL = 1024
N_SEC = 8
SEC_W = 512
HG_HEADS = 4
HG_D = 128
HG_GROUP = 4
AT_DH = 64
LANES = 128
ATT_BLK = 128
AT_COLS = 512
AT_QB = 8
DILATIONS = (1, 4, 16)
ROPE_THETA = 10000.0
CH = 16
LB_LO, LB_HI = 1e-6, 1.0 - 1e-6
ADAM_LR, ADAM_B1, ADAM_B2, ADAM_EPS, ADAM_WD, ADAM_STEP = 0.001, 0.9, 0.999, 1e-08, 0.01, 10
VMEM_LIMIT = 56 * 1024 * 1024
MESH = pl.DeviceIdType.MESH


def _params(*sem):
    return pltpu.CompilerParams(dimension_semantics=sem, vmem_limit_bytes=VMEM_LIMIT)


def _sigmoid(x):
    return 1.0 / (1.0 + jnp.exp(-x))


def _dot(a, b):
    return jnp.dot(a.astype(MM), b.astype(MM), preferred_element_type=F32)


def _dot_nt(a, b):
    return lax.dot_general(a.astype(MM), b.astype(MM), (((1,), (1,)), ((), ())), preferred_element_type=F32)


def _dot_tn(a, b):
    return lax.dot_general(a.astype(MM), b.astype(MM), (((0,), (0,)), ((), ())), preferred_element_type=F32)


def _split3(g):
    g1 = g.astype(jnp.bfloat16)
    r1 = g - g1.astype(F32)
    g2 = r1.astype(jnp.bfloat16)
    return g1, g2, (r1 - g2.astype(F32)).astype(jnp.bfloat16)


def _tri_dot(tri, g):
    t = tri.astype(jnp.bfloat16)
    g1, g2, g3 = _split3(g)
    d = functools.partial(jnp.dot, preferred_element_type=F32)
    return d(t, g1) + d(t, g2) + d(t, g3)


def _lower_bound(lbl):
    l0, l1 = lbl[0:1, :], lbl[1:2, :]
    m = jnp.maximum(l0, l1)
    e0, e1 = jnp.exp(l0 - m), jnp.exp(l1 - m)
    p = e0 / (e0 + e1)
    inside = (p >= LB_LO) & (p <= LB_HI)
    return jnp.clip(p, LB_LO, LB_HI), jnp.where(inside, p * (e1 / (e0 + e1)), 0.0)


def _iota2(shape, dim):
    return lax.broadcasted_iota(jnp.int32, shape, dim)


def _hgrn_gates(xq, xf, lb):
    sgq = _sigmoid(xq)
    sg = _sigmoid(xf)
    sn = _sigmoid(-xf)
    f = lb + (1.0 - lb) * sg
    return sgq, xq * sgq, sg, sn, f, (1.0 - lb) * sn


def _bdot(a, b, ca, cb):
    return lax.dot_general(a.astype(MM), b.astype(MM), (((ca,), (cb,)), ((0,), (0,))), preferred_element_type=F32)


def _chunk_masks(rb):
    row, col = _iota2((rb, rb), 0), _iota2((rb, rb), 1)
    same = (row // CH) == (col // CH)
    return same & (row >= col), same & (row <= col)


HALF = CH // 2
SLAB_ROWS = HALF * CH + (HALF // 2) * CH


def _write_slabs(slab_ref, g, q3, b3):
    slab = lambda t, rows: q3[:, rows, :] * jnp.exp(jnp.minimum(b3[:, rows, :] - b3[:, t:t + 1, :], 0.0))
    late = slice(HALF, CH)
    for t in range(HALF):
        slab_ref[g, :, t * CH:(t + 1) * CH, :] = slab(t, slice(0, CH)).astype(MM)
    for p in range(HALF // 2):
        t = HALF + 2 * p
        two = jnp.concatenate([slab(t, late), slab(t + 1, late)], axis=1)
        slab_ref[g, :, (HALF + p) * CH:(HALF + p + 1) * CH, :] = two.astype(MM)


def _read_diag(r):
    nc = r.shape[0]
    col, col_late = _iota2((nc, CH, CH), 2), _iota2((nc, HALF, CH), 2)
    a, a_late = jnp.zeros((nc, CH, CH), F32), jnp.zeros((nc, HALF, CH), F32)
    for t in range(HALF):
        a = a + jnp.where(col == t, r[:, t * CH:(t + 1) * CH, :], 0.0)
    for p in range(HALF // 2):
        t, two = HALF + 2 * p, r[:, (HALF + p) * CH:(HALF + p + 1) * CH, :]
        a_late = a_late + jnp.where(col_late == t, two[:, :HALF, :], 0.0) + jnp.where(col_late == t + 1, two[:, HALF:, :], 0.0)
    return a + jnp.concatenate([jnp.zeros_like(a_late), a_late], axis=1)


def _hgrn_fwd(proj, lb_logits, rb=256):
    s = proj.shape[1]
    nb, nc = s // rb, rb // CH

    def body(q_ref, f_ref, i_ref, lbl_ref, o_ref, sst_ref, st_ref, slab_ref, states_ref):
        @pl.when(pl.program_id(1) == 0)
        def _():
            st_ref[...] = jnp.zeros_like(st_ref)

        sst_ref[...] = st_ref[...]
        prefix, _ = _chunk_masks(rb)
        c3 = lambda a: a.reshape(nc, CH, HG_D)
        row, col = _iota2((nc, CH, CH), 1), _iota2((nc, CH, CH), 2)
        heads = []
        for g in range(HG_GROUP):
            hs = slice(g * HG_D, (g + 1) * HG_D)
            lb, _ = _lower_bound(lbl_ref[:, hs])
            _, q, _, _, f, kk = _hgrn_gates(q_ref[:, hs], f_ref[:, hs], lb)
            b3 = c3(_tri_dot(prefix, jnp.log(f)))
            q3, kk3, v3 = c3(q), c3(kk), c3(i_ref[:, hs])
            bl3 = b3[:, CH - 1:CH, :]
            _write_slabs(slab_ref, g, q3, b3)
            x_upd = _bdot(v3, kk3 * jnp.exp(bl3 - b3), 1, 1)
            heads.append(dict(hs=hs, kk3=kk3, v3=v3, qe3=q3 * jnp.exp(b3), ebl3=jnp.exp(bl3), x_upd=x_upd))
        for g, hd in enumerate(heads):
            st = st_ref[g]
            for c in range(nc):
                states_ref[g, c] = st
                st = st * hd["ebl3"][c] + hd["x_upd"][c]
            st_ref[g] = st
        for g, hd in enumerate(heads):
            a = _read_diag(_bdot(slab_ref[g], hd["kk3"], 2, 2))
            a = jnp.where(row >= col, a, 0.0)
            o3 = _bdot(hd["qe3"], states_ref[g], 2, 2) + _bdot(a, hd["v3"], 2, 1)
            o_ref[:, hd["hs"]] = o3.reshape(rb, HG_D)

    wide = HG_GROUP * HG_D
    sec = lambda j: pl.BlockSpec((None, rb, wide), lambda h, i, j=j: (j, i, h))
    return pl.pallas_call(
        body, name="hgrn_fwd", grid=(HG_HEADS // HG_GROUP, nb),
        in_specs=[sec(0), sec(1), sec(2), pl.BlockSpec((2, wide), lambda h, i: (0, h))],
        out_specs=[pl.BlockSpec((rb, wide), lambda h, i: (i, h)),
                   pl.BlockSpec((None, HG_GROUP, HG_D, HG_D), lambda h, i: (i, h, 0, 0))],
        out_shape=[jax.ShapeDtypeStruct((s, SEC_W), F32),
                   jax.ShapeDtypeStruct((nb, HG_HEADS, HG_D, HG_D), F32)],
        scratch_shapes=[pltpu.VMEM((HG_GROUP, HG_D, HG_D), F32), pltpu.VMEM((HG_GROUP, nc, SLAB_ROWS, HG_D), MM),
                        pltpu.VMEM((HG_GROUP, nc, HG_D, HG_D), F32)],
        compiler_params=_params("parallel", "arbitrary"),
    )(proj, proj, proj, lb_logits)


def _hgrn_bwd(proj, lb_logits, d_o, sst, rb=256):
    s = proj.shape[1]
    nb, nc = s // rb, rb // CH

    def body(q_ref, f_ref, i_ref, lbl_ref, do_ref, sst_ref, dxq_ref, dxf_ref, dxi_ref, dlb_ref,
             dst_ref, states_ref, dstates_ref, lslab_ref, kslab_ref):
        @pl.when(pl.program_id(1) == 0)
        def _():
            dst_ref[...] = jnp.zeros_like(dst_ref)
            dlb_ref[...] = jnp.zeros_like(dlb_ref)

        prefix, suffix = _chunk_masks(rb)
        c3 = lambda a: a.reshape(nc, CH, HG_D)
        flat = lambda a: a.reshape(rb, HG_D)
        row, col = _iota2((nc, CH, CH), 1), _iota2((nc, CH, CH), 2)
        tril, triu = row >= col, row <= col
        sel = (_iota2((CH, CH * CH), 1) % CH == _iota2((CH, CH * CH), 0)).astype(MM)
        blockdiag = _iota2((nc, CH, CH * CH), 2) // CH == _iota2((nc, CH, CH * CH), 1)
        tile = lambda m: jnp.where(blockdiag, _dot(m.reshape(rb, CH), sel).reshape(nc, CH, CH * CH), 0.0)
        last = _iota2((nc, CH, HG_D), 1) == CH - 1
        heads = []
        for g in range(HG_GROUP):
            hs = slice(g * HG_D, (g + 1) * HG_D)
            lb, _ = _lower_bound(lbl_ref[:, hs])
            xq = q_ref[:, hs]
            sgq, q, sg, sn, f, kk = _hgrn_gates(xq, f_ref[:, hs], lb)
            b3 = c3(_tri_dot(prefix, jnp.log(f)))
            q3, kk3, v3, do3 = c3(q), c3(kk), c3(i_ref[:, hs]), c3(do_ref[:, hs])
            bl3 = b3[:, CH - 1:CH, :]
            eb3, ebl3, dec3 = jnp.exp(b3), jnp.exp(bl3), jnp.exp(bl3 - b3)
            qe3, kd3 = q3 * eb3, kk3 * dec3
            x_upd, y_upd = _bdot(v3, kd3, 1, 1), _bdot(do3, qe3, 1, 1)
            for t in range(CH):
                bt = b3[:, t:t + 1, :]
                lslab_ref[g, :, t * CH:(t + 1) * CH, :] = (q3 * jnp.exp(jnp.minimum(b3 - bt, 0.0))).astype(MM)
                kslab_ref[g, :, t * CH:(t + 1) * CH, :] = (kk3 * jnp.exp(jnp.minimum(bt - b3, 0.0))).astype(MM)
            d_a = jnp.where(tril, _bdot(do3, v3, 2, 2), 0.0)
            d_at = jnp.where(triu, _bdot(v3, do3, 2, 2), 0.0)
            heads.append(dict(hs=hs, lb=lb, xq=xq, sgq=sgq, sg=sg, sn=sn, f=f, q3=q3, kk3=kk3, v3=v3, do3=do3,
                              eb3=eb3, ebl3=ebl3, dec3=dec3, qe3=qe3, kd3=kd3, x_upd=x_upd, y_upd=y_upd,
                              d_a=d_a, d_at=d_at))
        for g, hd in enumerate(heads):
            st = sst_ref[g]
            for c in range(nc):
                states_ref[g, c] = st
                st = st * hd["ebl3"][c] + hd["x_upd"][c]
            dst = dst_ref[g]
            for c in reversed(range(nc)):
                dstates_ref[g, c] = dst
                dst = dst * hd["ebl3"][c] + hd["y_upd"][c]
            dst_ref[g] = dst
        for g, hd in enumerate(heads):
            q3, v3, do3, kd3 = hd["q3"], hd["v3"], hd["do3"], hd["kd3"]
            states, dstates = states_ref[g], dstates_ref[g]
            hd["dqe"] = _bdot(do3, states, 2, 1)
            hd["dkd"] = _bdot(v3, dstates, 2, 1)
            r = _bdot(kslab_ref[g], q3, 2, 2)
            a_t = jnp.zeros((nc, CH, CH), F32)
            for t in range(CH):
                a_t = a_t + jnp.where(col == t, r[:, t * CH:(t + 1) * CH, :], 0.0)
            a_t = jnp.where(triu, a_t, 0.0)
            hd["dv"] = _bdot(kd3, dstates, 2, 2) + _bdot(a_t, do3, 2, 1)
            hd["dq_in"] = _bdot(tile(hd["d_a"]), kslab_ref[g], 2, 1)
            hd["dk_in"] = _bdot(tile(hd["d_at"]), lslab_ref[g], 2, 1)
            hd["ss"] = jnp.sum(dstates * states, axis=1, keepdims=True)
        for g, hd in enumerate(heads):
            q3, kk3, eb3, ebl3, dec3, qe3, kd3 = (hd[k] for k in ("q3", "kk3", "eb3", "ebl3", "dec3", "qe3", "kd3"))
            dqe, dkd, dq_in, dk_in = hd["dqe"], hd["dkd"], hd["dq_in"], hd["dk_in"]
            dkd_kd = dkd * kd3
            db = dqe * qe3 - dkd_kd + q3 * dq_in - kk3 * dk_in
            dbl = jnp.sum(dkd_kd, axis=1, keepdims=True) + hd["ss"] * ebl3
            dg = _tri_dot(suffix, flat(db + jnp.where(last, dbl, 0.0)))
            df = dg / hd["f"] - flat(dkd * dec3 + dk_in)
            xq, sgq, hs = hd["xq"], hd["sgq"], hd["hs"]
            dxq_ref[:, hs] = (flat(dqe * eb3 + dq_in) * (sgq * (1.0 + xq * (1.0 - sgq)))).astype(MM)
            dxf_ref[:, hs] = (df * (1.0 - hd["lb"]) * hd["sg"] * hd["sn"]).astype(MM)
            dxi_ref[:, hs] = flat(hd["dv"]).astype(MM)
            dlb_ref[:, hs] += jnp.sum(df * hd["sn"], axis=0, keepdims=True)

    wide = HG_GROUP * HG_D
    rev = lambda i: nb - 1 - i
    sec = lambda j: pl.BlockSpec((None, rb, wide), lambda h, i, j=j: (j, rev(i), h))
    blk = pl.BlockSpec((rb, wide), lambda h, i: (rev(i), h))
    state = (pltpu.VMEM((HG_GROUP, nc, HG_D, HG_D), F32), pltpu.VMEM((HG_GROUP, nc, CH * CH, HG_D), MM))
    return pl.pallas_call(
        body, name="hgrn_bwd", grid=(HG_HEADS // HG_GROUP, nb),
        in_specs=[sec(0), sec(1), sec(2), pl.BlockSpec((2, wide), lambda h, i: (0, h)), blk,
                  pl.BlockSpec((None, HG_GROUP, HG_D, HG_D), lambda h, i: (rev(i), h, 0, 0))],
        out_specs=[blk, blk, blk, pl.BlockSpec((1, wide), lambda h, i: (0, h))],
        out_shape=[jax.ShapeDtypeStruct((s, SEC_W), MM)] * 3 + [jax.ShapeDtypeStruct((1, SEC_W), F32)],
        scratch_shapes=[pltpu.VMEM((HG_GROUP, HG_D, HG_D), F32), state[0], state[0], state[1], state[1]],
        compiler_params=_params("parallel", "arbitrary"),
    )(proj, proj, proj, lb_logits, d_o, sst)


def _rope_tables(s):
    half = AT_DH // 2
    inv_freq = np.float32(1.0) / (np.float32(ROPE_THETA) ** (np.arange(half, dtype=np.float32) / np.float32(half)))
    ang = np.arange(s, dtype=np.float32)[:, None] * inv_freq[None, :]
    cos, sin = np.cos(ang), np.sin(ang)
    return np.concatenate([cos] * 4, axis=-1), np.concatenate([-sin, sin] * 2, axis=-1)


def _rope128(x, cos, sin):
    lo = (_iota2(x.shape, 1) % AT_DH) < AT_DH // 2
    rot = jnp.where(lo, pltpu.roll(x, LANES - AT_DH // 2, 1), pltpu.roll(x, AT_DH // 2, 1))
    return x * cos + rot * sin


LANE_GROUPS = SEC_W // LANES


def _set_lanes(ref, val):
    for j in range(LANE_GROUPS):
        ref[j] = val[:, j * LANES:(j + 1) * LANES]


def _get_lanes(ref):
    return jnp.concatenate([ref[j] for j in range(LANE_GROUPS)], axis=-1)


def _to_view(src_ref, dst_ref, d):
    n = src_ref.shape[1] // d
    for r in range(d):
        rows = pl.ds(r, n, stride=d) if d > 1 else slice(None)
        for j in range(LANE_GROUPS):
            c0 = r * SEC_W + j * LANES
            dst_ref[:, c0:c0 + LANES] = src_ref.at[j][rows, :].astype(dst_ref.dtype)


def _from_view(src_ref, dst_ref, d):
    n = dst_ref.shape[1] // d
    for r in range(d):
        for j in range(LANE_GROUPS):
            c0 = r * SEC_W + j * LANES
            dst_ref.at[j][pl.ds(r, n, stride=d), :] = src_ref[:, c0:c0 + LANES].astype(dst_ref.dtype)


def _view_spec(tm, d):
    return pl.BlockSpec((tm // d, d * SEC_W), lambda i: (i, 0))


def _view_shape(s, d, dtype):
    return jax.ShapeDtypeStruct((s // d, d * SEC_W), dtype)


PROJ_KEPT = (0, 1, 2, 3, 7)


def _inproj_fwd(x, norm_w, w_all, cos, sin, tm=512):
    s = x.shape[0]

    def body(x_ref, nw_ref, w_ref, cos_ref, sin_ref, proj_ref, *refs):
        outs, (qs_ref, ks_ref, vs_ref) = refs[:-3], refs[-3:]
        xv = x_ref[...]
        rstd = lax.rsqrt(jnp.mean(xv * xv, axis=-1, keepdims=True) + NORM_EPS)
        u = (xv * rstd * nw_ref[...]).astype(MM)
        for slot, j in enumerate(PROJ_KEPT):
            proj_ref[slot] = jnp.dot(u, w_ref[j], preferred_element_type=F32)
        q, k, v = [jnp.dot(u, w_ref[j], preferred_element_type=F32) for j in (4, 5, 6)]
        c, sn = cos_ref[...], sin_ref[...]
        for g in range(LANE_GROUPS):
            sl = slice(g * LANES, (g + 1) * LANES)
            qs_ref[g] = _rope128(q[:, sl], c, sn) * (AT_DH ** -0.5)
            ks_ref[g] = _rope128(k[:, sl], c, sn)
            vs_ref[g] = v[:, sl]
        for i, d in enumerate(DILATIONS):
            for src_ref, dst_ref in zip((qs_ref, ks_ref, vs_ref), outs[3 * i:3 * i + 3]):
                _to_view(src_ref, dst_ref, d)

    tab = pl.BlockSpec((tm, LANES), lambda i: (i, 0))
    return pl.pallas_call(
        body, name="inproj_fwd", grid=(s // tm,),
        in_specs=[pl.BlockSpec((tm, D_MODEL), lambda i: (i, 0)),
                  pl.BlockSpec((1, D_MODEL), lambda i: (0, 0)),
                  pl.BlockSpec((N_SEC, D_MODEL, SEC_W), lambda i: (0, 0, 0)), tab, tab],
        out_specs=[pl.BlockSpec((len(PROJ_KEPT), tm, SEC_W), lambda i: (0, i, 0))]
                  + [_view_spec(tm, d) for d in DILATIONS for _ in range(3)],
        out_shape=[jax.ShapeDtypeStruct((len(PROJ_KEPT), s, SEC_W), F32)]
                  + [_view_shape(s, d, MM) for d in DILATIONS for _ in range(3)],
        scratch_shapes=[pltpu.VMEM((LANE_GROUPS, tm, LANES), F32)] * 3,
        compiler_params=_params("parallel"),
    )(x, norm_w, w_all, cos, sin)


def _band_mask(first_ok, second_ok):
    row, col = _iota2((ATT_BLK, 2 * ATT_BLK), 0), _iota2((ATT_BLK, 2 * ATT_BLK), 1)
    return ((col < ATT_BLK) & (col >= row) & first_ok) | ((col >= ATT_BLK) & ((col - ATT_BLK) <= row) & second_ok)


def _own_lanes(rows, h):
    lane = _iota2((rows, LANES), 1)
    return (lane < AT_DH) if h == 0 else (lane >= AT_DH)


def _neg_pieces(rows, h):
    lane = _iota2((rows, LANES), 1) - (AT_DH if h == 0 else 0)
    return jnp.where((lane >= 0) & (lane < 3), -1.0, 0.0).astype(MM)


def _units(qb):
    return [(b, slice(g * LANES, (g + 1) * LANES), h) for b in range(qb) for g in range(AT_COLS // LANES) for h in range(2)]


def _sub(b):
    return slice(b * ATT_BLK, (b + 1) * ATT_BLK)


def _band_before(cur_ref, prev_ref, b, sl):
    if b == 0:
        return jnp.concatenate([prev_ref[:, sl], cur_ref[0:ATT_BLK, sl]], axis=0)
    return cur_ref[(b - 1) * ATT_BLK:(b + 1) * ATT_BLK, sl]


def _band_after(cur_ref, next_ref, b, sl):
    if (b + 1) * ATT_BLK == cur_ref.shape[0]:
        return jnp.concatenate([cur_ref[b * ATT_BLK:(b + 1) * ATT_BLK, sl], next_ref[:, sl]], axis=0)
    return cur_ref[b * ATT_BLK:(b + 2) * ATT_BLK, sl]


def _attn_specs(rows):
    qb = min(AT_QB, rows // ATT_BLK)
    assert rows % (qb * ATT_BLK) == 0
    last = rows // ATT_BLK - 1
    cur = pl.BlockSpec((qb * ATT_BLK, AT_COLS), lambda c, n: (n, c))
    prev = pl.BlockSpec((ATT_BLK, AT_COLS), lambda c, n: (jnp.maximum(qb * n - 1, 0), c))
    nxt = pl.BlockSpec((ATT_BLK, AT_COLS), lambda c, n: (jnp.minimum(qb * (n + 1), last), c))
    return qb, cur, prev, nxt


def _stack_heads(a):
    h0 = _own_lanes(a.shape[0], 0)
    zero = jnp.zeros_like(a)
    return jnp.concatenate([jnp.where(h0, a, zero), jnp.where(h0, zero, a)], axis=0)


def _unstack_heads(a2):
    return jnp.where(_own_lanes(ATT_BLK, 0), a2[:ATT_BLK], a2[ATT_BLK:])


def _attn_fwd(qr, kr, vr, d):
    rows, cols = qr.shape
    qb, cur, prev, nxt = _attn_specs(rows)
    nb = rows // (qb * ATT_BLK)

    def body(q_ref, kc_ref, kp_ref, vc_ref, vp_ref, o_ref, lse_ref):
        twice = lambda m: jnp.concatenate([m, m], axis=0)
        masks = {True: twice(_band_mask(pl.program_id(1) > 0, True)), False: twice(_band_mask(True, True))}
        ones = jnp.ones((2 * ATT_BLK, LANES), MM)
        units = [(b, sl) for b, sl, h in _units(qb) if h == 0]
        scs = [jnp.where(masks[b == 0], _dot_nt(_stack_heads(q_ref[_sub(b), sl]), _band_before(kc_ref, kp_ref, b, sl)),
                         NEG) for b, sl in units]
        ms = [jnp.max(sc, axis=-1, keepdims=True) for sc in scs]
        ps = [jnp.exp(sc - m).astype(MM) for sc, m in zip(scs, ms)]
        ols = [jnp.dot(p, jnp.concatenate([_band_before(vc_ref, vp_ref, b, sl), ones], axis=1),
                       preferred_element_type=F32) for p, (b, sl) in zip(ps, units)]
        for (b, sl), m, ol in zip(units, ms, ols):
            l = _unstack_heads(ol[:, LANES:])
            o_ref[_sub(b), sl] = _unstack_heads(ol[:, :LANES]) / l
            lse_ref[_sub(b), sl] = _unstack_heads(jnp.broadcast_to(m, (2 * ATT_BLK, LANES))) + jnp.log(l)

    o, lse = pl.pallas_call(
        body, name=f"attn_fwd_d{d}", grid=(cols // AT_COLS, nb),
        in_specs=[cur, cur, prev, cur, prev], out_specs=[cur, cur],
        out_shape=[jax.ShapeDtypeStruct((rows, cols), F32)] * 2,
        compiler_params=_params("parallel", "parallel"),
    )(qr, kr, kr, vr, vr)
    return o, lse


def _attn_bwd_dq(qr, kr, vr, do, lse, delta, d):
    rows, cols = qr.shape
    qb, cur, prev, nxt = _attn_specs(rows)
    nb = rows // (qb * ATT_BLK)

    def body(q_ref, kc_ref, kp_ref, vc_ref, vp_ref, do_ref, lse_ref, dl_ref, dq_ref):
        masks = {True: _band_mask(pl.program_id(1) > 0, True), False: _band_mask(True, True)}
        units = _units(qb)
        sms, dps = [], []
        for b, sl, h in units:
            own, own_b, neg = _own_lanes(ATT_BLK, h), _own_lanes(2 * ATT_BLK, h), _neg_pieces(2 * ATT_BLK, h)
            sms.append(_dot_nt(jnp.where(own, q_ref[_sub(b), sl], lse_ref[_sub(b), sl]),
                               jnp.where(own_b, _band_before(kc_ref, kp_ref, b, sl), neg)))
            dps.append(_dot_nt(jnp.where(own, do_ref[_sub(b), sl], dl_ref[_sub(b), sl]),
                               jnp.where(own_b, _band_before(vc_ref, vp_ref, b, sl), neg)))
        dss = [(jnp.exp(jnp.where(masks[b == 0], sm, NEG)) * dp).astype(MM)
               for sm, dp, (b, _, _) in zip(sms, dps, units)]
        dqs = [jnp.dot(ds, _band_before(kc_ref, kp_ref, b, sl), preferred_element_type=F32) * (AT_DH ** -0.5)
               for ds, (b, sl, _) in zip(dss, units)]
        for i in range(0, len(units), 2):
            b, sl, _ = units[i]
            dq_ref[_sub(b), sl] = jnp.where(_own_lanes(ATT_BLK, 0), dqs[i], dqs[i + 1]).astype(dq_ref.dtype)

    dq = pl.pallas_call(
        body, name=f"attn_bwd_dq_d{d}", grid=(cols // AT_COLS, nb),
        in_specs=[cur, cur, prev, cur, prev, cur, cur, cur], out_specs=cur,
        out_shape=jax.ShapeDtypeStruct((rows, cols), MM),
        compiler_params=_params("parallel", "parallel"),
    )(qr, kr, kr, vr, vr, do, lse, delta)
    return dq


def _attn_bwd_dkv(qr, kr, vr, do, lse, delta, d):
    rows, cols = qr.shape
    qb, cur, prev, nxt = _attn_specs(rows)
    nb = rows // (qb * ATT_BLK)

    def body(k_ref, v_ref, qc_ref, qn_ref, doc_ref, don_ref, lsec_ref, lsen_ref, dlc_ref, dln_ref,
             dk_ref, dv_ref):
        masks = {True: _band_mask(True, pl.program_id(1) < nb - 1), False: _band_mask(True, True)}
        units = _units(qb)
        sms, dps = [], []
        for b, sl, h in units:
            own, own_b, neg = _own_lanes(ATT_BLK, h), _own_lanes(2 * ATT_BLK, h), _neg_pieces(ATT_BLK, h)
            sms.append(_dot_nt(jnp.where(own, k_ref[_sub(b), sl], neg),
                               jnp.where(own_b, _band_after(qc_ref, qn_ref, b, sl),
                                         _band_after(lsec_ref, lsen_ref, b, sl))))
            dps.append(_dot_nt(jnp.where(own, v_ref[_sub(b), sl], neg),
                               jnp.where(own_b, _band_after(doc_ref, don_ref, b, sl),
                                         _band_after(dlc_ref, dln_ref, b, sl))))
        ps = [jnp.exp(jnp.where(masks[b == qb - 1], sm, NEG)) for sm, (b, _, _) in zip(sms, units)]
        dss = [(p * dp).astype(MM) for p, dp in zip(ps, dps)]
        dvs = [jnp.dot(p.astype(MM), _band_after(doc_ref, don_ref, b, sl), preferred_element_type=F32)
               for p, (b, sl, _) in zip(ps, units)]
        dks = [jnp.dot(ds, _band_after(qc_ref, qn_ref, b, sl), preferred_element_type=F32)
               for ds, (b, sl, _) in zip(dss, units)]
        head0 = _own_lanes(ATT_BLK, 0)
        for i in range(0, len(units), 2):
            b, sl, _ = units[i]
            dk_ref[_sub(b), sl] = jnp.where(head0, dks[i], dks[i + 1]).astype(dk_ref.dtype)
            dv_ref[_sub(b), sl] = jnp.where(head0, dvs[i], dvs[i + 1]).astype(dv_ref.dtype)

    dk, dv = pl.pallas_call(
        body, name=f"attn_bwd_dkv_d{d}", grid=(cols // AT_COLS, nb),
        in_specs=[cur, cur, cur, nxt, cur, nxt, cur, nxt, cur, nxt], out_specs=[cur, cur],
        out_shape=[jax.ShapeDtypeStruct((rows, cols), MM)] * 2,
        compiler_params=_params("parallel", "parallel"),
    )(kr, vr, qr, qr, do, do, lse, lse, delta, delta)
    return dk, dv


def _head_sum(a, width):
    parts = []
    for j in range(a.shape[1] // width):
        sm = jnp.sum(a[:, j * width:(j + 1) * width], axis=-1, keepdims=True)
        parts.append(jnp.broadcast_to(sm, (a.shape[0], width)))
    return jnp.concatenate(parts, axis=-1)


def _partner_sum(a):
    swap = (_iota2((LANES, LANES), 0) // AT_DH != _iota2((LANES, LANES), 1) // AT_DH).astype(jnp.bfloat16)
    d = functools.partial(jnp.dot, preferred_element_type=F32)
    parts = _split3(a)
    return jnp.concatenate([d(parts[0][:, sl], swap) + d(parts[1][:, sl], swap) + d(parts[2][:, sl], swap)
                            for sl in (slice(j * LANES, (j + 1) * LANES) for j in range(a.shape[1] // LANES))], axis=-1)


def _partner_value(x):
    return jnp.concatenate([pltpu.roll(x[:, j * LANES:(j + 1) * LANES], AT_DH, 1) for j in range(x.shape[1] // LANES)],
                           axis=-1)


def _pieces(xs):
    hi = xs.astype(jnp.bfloat16).astype(F32)
    mid = (xs - hi).astype(jnp.bfloat16).astype(F32)
    lo = (xs - hi - mid).astype(jnp.bfloat16).astype(F32)
    lane = _iota2(xs.shape, 1) % AT_DH
    return jnp.where(lane == 0, hi, jnp.where(lane == 1, mid, jnp.where(lane == 2, lo, 0.0)))


def _mid(x, tgt, proj, o_hg, o_at, lse_at, hg_norm_w, final_norm_w, wo_all, tm=256):
    s = x.shape[0]
    nb = s // tm

    def body(x_ref, t_ref, hgz_ref, atz_ref, ohg_ref, o1_ref, o2_ref, o3_ref, l1_ref, l2_ref, l3_ref,
             g_ref, fw_ref, wo_ref,
             dh_ref, dohg_ref, dhgz_ref, datz_ref, do1_ref, do2_ref, do3_ref, dl1_ref, dl2_ref, dl3_ref,
             lp1_ref, lp2_ref, lp3_ref,
             gwo_ref, gfw_ref, ghg_ref, loss_ref, nat_ref, stage_ref, gwo_acc):
        @pl.when(pl.program_id(0) == 0)
        def _():
            gwo_acc[...] = jnp.zeros_like(gwo_acc)
            gfw_ref[...] = jnp.zeros_like(gfw_ref)
            ghg_ref[...] = jnp.zeros_like(ghg_ref)
            loss_ref[...] = jnp.zeros_like(loss_ref)

        ohg, g = ohg_ref[...], g_ref[...]
        rs = lax.rsqrt(_head_sum(ohg * ohg, HG_D) * (1.0 / HG_D) + NORM_EPS)
        on = ohg * rs
        hgz = hgz_ref[...]
        sz = _sigmoid(hgz)
        gate_hg = hgz * sz
        lses, outs = [l1_ref[...]], [o1_ref[...]]
        for k, (d, l_ref, o_ref) in enumerate(zip(DILATIONS[1:], (l2_ref, l3_ref), (o2_ref, o3_ref))):
            _from_view(l_ref, nat_ref.at[2 * k], d)
            _from_view(o_ref, nat_ref.at[2 * k + 1], d)
            lses.append(_get_lanes(nat_ref.at[2 * k]))
            outs.append(_get_lanes(nat_ref.at[2 * k + 1]))
        mx = jnp.maximum(jnp.maximum(lses[0], lses[1]), lses[2])
        es = [jnp.exp(l - mx) for l in lses]
        den = es[0] + es[1] + es[2]
        ws = [e / den for e in es]
        oat = ws[0] * outs[0] + ws[1] * outs[1] + ws[2] * outs[2]
        atz = atz_ref[...]
        sa = _sigmoid(atz)
        gate_at = atz * sa
        mixed = jnp.concatenate([on * g * gate_hg, oat * gate_at], axis=-1).astype(MM)
        h = x_ref[...] + jnp.dot(mixed, wo_ref[...], preferred_element_type=F32)
        rstd = lax.rsqrt(jnp.mean(h * h, axis=-1, keepdims=True) + NORM_EPS)
        hn = h * rstd
        fw = fw_ref[...]
        err = hn * fw - t_ref[...]
        loss_ref[...] += 0.5 * jnp.sum(jnp.mean(err * err, axis=-1, keepdims=True), axis=0, keepdims=True)
        dout = err * (1.0 / D_MODEL)
        gfw_ref[...] += jnp.sum(dout * hn, axis=0, keepdims=True)
        dhn = dout * fw
        dh = rstd * (dhn - hn * jnp.mean(dhn * hn, axis=-1, keepdims=True))
        dh_ref[...] = dh
        dh_mm = dh.astype(MM)
        gwo_acc[...] += _dot_tn(mixed, dh_mm)

        @pl.when(pl.program_id(0) == nb - 1)
        def _():
            gwo_ref[...] = gwo_acc[...].astype(gwo_ref.dtype)

        dmixed = _dot_nt(dh_mm, wo_ref[...])
        dm_hg = dmixed[:, :SEC_W]
        d_ong = dm_hg * gate_hg
        dhgz_ref[...] = (dm_hg * (on * g) * (sz * (1.0 + hgz * (1.0 - sz)))).astype(MM)
        ghg_ref[...] += jnp.sum(d_ong * on, axis=0, keepdims=True)
        d_on = d_ong * g
        dohg_ref[...] = rs * (d_on - on * (_head_sum(d_on * on, HG_D) * (1.0 / HG_D)))
        dm_at = dmixed[:, SEC_W:]
        d_oat = dm_at * gate_at
        datz_ref[...] = (dm_at * oat * (sa * (1.0 + atz * (1.0 - sa)))).astype(MM)
        lse_all = mx + jnp.log(den)
        for val, dst_refs in ((d_oat, (do1_ref, do2_ref, do3_ref)),
                              (_pieces(_partner_sum(d_oat * oat)), (dl1_ref, dl2_ref, dl3_ref)),
                              (_pieces(_partner_value(lse_all)), (lp1_ref, lp2_ref, lp3_ref))):
            _set_lanes(stage_ref, val)
            for d, dst_ref in zip(DILATIONS, dst_refs):
                _to_view(stage_ref, dst_ref, d)

    row = lambda w: pl.BlockSpec((tm, w), lambda i: (i, 0))
    sec = lambda j: pl.BlockSpec((None, tm, SEC_W), lambda i, j=j: (j, i, 0))
    const = lambda shp: pl.BlockSpec(shp, lambda i: (0,) * len(shp))
    half = row(SEC_W)
    views = [_view_spec(tm, d) for d in DILATIONS]
    return pl.pallas_call(
        body, name="mid", grid=(nb,),
        in_specs=[row(D_MODEL), row(D_MODEL), sec(PROJ_KEPT.index(3)), sec(PROJ_KEPT.index(7)), half] + views * 2
                 + [const((1, SEC_W)), const((1, D_MODEL)), const((D_MODEL, D_MODEL))],
        out_specs=[row(D_MODEL)] + [half] * 3 + views * 3
                  + [const((D_MODEL, D_MODEL)), const((1, D_MODEL)), const((1, SEC_W)), const((1, 1))],
        out_shape=[jax.ShapeDtypeStruct((s, D_MODEL), F32), jax.ShapeDtypeStruct((s, SEC_W), F32)]
                  + [jax.ShapeDtypeStruct((s, SEC_W), MM)] * 2
                  + [_view_shape(s, d, MM) for d in DILATIONS] * 3
                  + [jax.ShapeDtypeStruct((D_MODEL, D_MODEL), XCH), jax.ShapeDtypeStruct((1, D_MODEL), F32),
                     jax.ShapeDtypeStruct((1, SEC_W), F32), jax.ShapeDtypeStruct((1, 1), F32)],
        scratch_shapes=[pltpu.VMEM((4, LANE_GROUPS, tm, LANES), F32), pltpu.VMEM((LANE_GROUPS, tm, LANES), F32),
                        pltpu.VMEM((D_MODEL, D_MODEL), F32)],
        compiler_params=_params("arbitrary"),
    )(x, tgt, proj, proj, o_hg, *o_at, *lse_at, hg_norm_w, final_norm_w, wo_all)


def _section_specs(dsecs, tm):
    return [pl.BlockSpec((tm, SEC_W), lambda i: (i, 0)) if k is None
            else pl.BlockSpec((None, tm, SEC_W), lambda i, k=k: (k, i, 0)) for _, k in dsecs]


def _inproj_bwd_x(x, norm_w, w_all, dh, dsecs, token, tm=512):
    s = x.shape[0]

    def body(x_ref, nw_ref, w_ref, dh_ref, tok_ref, *refs):
        sec_refs, (gx_ref, gnw_ref) = refs[:N_SEC], refs[N_SEC:]

        @pl.when(pl.program_id(0) == 0)
        def _():
            gnw_ref[...] = jnp.zeros_like(gnw_ref)

        du = jnp.zeros((tm, D_MODEL), F32)
        for j in range(N_SEC):
            du = du + _dot_nt(sec_refs[j][...], w_ref[j])
        xv, nw = x_ref[...], nw_ref[...]
        rstd = lax.rsqrt(jnp.mean(xv * xv, axis=-1, keepdims=True) + NORM_EPS)
        xn = xv * rstd
        gnw_ref[...] += jnp.sum(du * xn, axis=0, keepdims=True)
        dxn = du * nw
        dx = rstd * (dxn - xn * jnp.mean(dxn * xn, axis=-1, keepdims=True))
        gx_ref[...] = (dh_ref[...] + tok_ref[0:1, 0:1]) + dx

    row = lambda w: pl.BlockSpec((tm, w), lambda i: (i, 0))
    const = lambda shp: pl.BlockSpec(shp, lambda i: (0,) * len(shp))
    return pl.pallas_call(
        body, name="inproj_bwd_x", grid=(s // tm,),
        in_specs=[row(D_MODEL), const((1, D_MODEL)), const((N_SEC, D_MODEL, SEC_W)), row(D_MODEL), const((8, 128))]
                 + _section_specs(dsecs, tm),
        out_specs=[row(D_MODEL), const((1, D_MODEL))],
        out_shape=[jax.ShapeDtypeStruct((s, D_MODEL), F32), jax.ShapeDtypeStruct((1, D_MODEL), F32)],
        compiler_params=_params("arbitrary"),
    )(x, norm_w, w_all, dh, token, *[a for a, _ in dsecs])


def _inproj_bwd_w(x, norm_w, dsec, dq_r, dk_r, dv, cos, sin, tm=512):
    s = x.shape[0]
    nb = s // tm

    def body(x_ref, nw_ref, s0, s1, s2, s3, s7, q1, q2, q3, k1, k2, k3, v1, v2, v3, cos_ref, sin_ref,
             gw_hbm, datt_ref, acc_ref, stage_ref, nat_ref):
        @pl.when(pl.program_id(0) == 0)
        def _():
            acc_ref[...] = jnp.zeros_like(acc_ref)

        def total(refs):
            acc = refs[0][...].astype(F32)
            for d, ref in zip(DILATIONS[1:], refs[1:]):
                _from_view(ref, nat_ref, d)
                acc = acc + _get_lanes(nat_ref)
            return acc

        c, sn = cos_ref[...], -sin_ref[...]
        unrot = lambda a: jnp.concatenate(
            [_rope128(a[:, j * LANES:(j + 1) * LANES], c, sn) for j in range(LANE_GROUPS)], axis=-1)
        att = [a.astype(MM) for a in (unrot(total((q1, q2, q3))), unrot(total((k1, k2, k3))), total((v1, v2, v3)))]
        for j, a in enumerate(att):
            datt_ref[j] = a
        xv = x_ref[...]
        rstd = lax.rsqrt(jnp.mean(xv * xv, axis=-1, keepdims=True) + NORM_EPS)
        u_t = (xv * rstd * nw_ref[...]).T.astype(MM)
        for j, dsj in enumerate((s0[...], s1[...], s2[...], s3[...], *att, s7[...])):
            acc_ref[j] += jnp.dot(u_t, dsj, preferred_element_type=F32)

        @pl.when(pl.program_id(0) == nb - 1)
        def _():
            for j in range(N_SEC):
                stage_ref[...] = acc_ref[j].astype(stage_ref.dtype)
                pltpu.sync_copy(stage_ref, gw_hbm.at[j])

    row = lambda w: pl.BlockSpec((tm, w), lambda i: (i, 0))
    return pl.pallas_call(
        body, name="inproj_bwd_w", grid=(nb,),
        in_specs=[row(D_MODEL), pl.BlockSpec((1, D_MODEL), lambda i: (0, 0))] + [row(SEC_W)] * 5
                 + [_view_spec(tm, d) for d in DILATIONS] * 3 + [row(LANES), row(LANES)],
        out_specs=[pl.BlockSpec(memory_space=pl.ANY), pl.BlockSpec((3, tm, SEC_W), lambda i: (0, i, 0))],
        out_shape=[jax.ShapeDtypeStruct((N_SEC, D_MODEL, SEC_W), XCH), jax.ShapeDtypeStruct((3, s, SEC_W), MM)],
        scratch_shapes=[pltpu.VMEM((N_SEC, D_MODEL, SEC_W), F32), pltpu.VMEM((D_MODEL, SEC_W), XCH),
                        pltpu.VMEM((LANE_GROUPS, tm, LANES), F32)],
        compiler_params=_params("arbitrary"),
    )(x, norm_w, *dsec, *dq_r, *dk_r, *dv, cos, sin)


def _local_step(x, tgt, norm_w, w_all, lb_logits, hg_norm_w, wo_all, final_norm_w, on_weight_grads):
    s = x.shape[0]
    cos, sin = _rope_tables(s)
    proj, *qkv = _inproj_fwd(x, norm_w, w_all, cos, sin)
    o_hg, sst = _hgrn_fwd(proj, lb_logits)
    qkv = [qkv[3 * i:3 * i + 3] for i in range(len(DILATIONS))]
    att = [_attn_fwd(*qkv_d, d) for qkv_d, d in zip(qkv, DILATIONS)]
    (dh, d_ohg, d_hgz, d_atz, do1, do2, do3, dl1, dl2, dl3, lp1, lp2, lp3, gwo, gfw, ghg, loss) = _mid(
        x, tgt, proj, o_hg, [a[0] for a in att], [a[1] for a in att], hg_norm_w, final_norm_w[None, :], wo_all)
    dxq, dxf, dxi, dlb = _hgrn_bwd(proj, lb_logits, d_ohg, sst)
    dq_r, dk_r, dv = [], [], []
    for d, qkv_d, do, lp, dl in zip(DILATIONS, qkv, (do1, do2, do3), (lp1, lp2, lp3), (dl1, dl2, dl3)):
        dq_r.append(_attn_bwd_dq(*qkv_d, do, lp, dl, d))
        dk_d, dv_d = _attn_bwd_dkv(*qkv_d, do, lp, dl, d)
        dk_r.append(dk_d)
        dv.append(dv_d)
    gwi, d_att = _inproj_bwd_w(x, norm_w, (dxq, dxf, dxi, d_hgz, d_atz), dq_r, dk_r, dv, cos, sin)
    dsecs = [(dxq, None), (dxf, None), (dxi, None), (d_hgz, None), (d_att, 0), (d_att, 1), (d_att, 2), (d_atz, None)]
    token = on_weight_grads(gwi, gwo)
    gx, gnw = _inproj_bwd_x(x, norm_w, w_all, dh, dsecs, token)
    small = jnp.concatenate([gnw, jnp.concatenate([dlb, ghg], axis=-1), gfw,
                             jnp.pad(loss, ((0, 0), (0, D_MODEL - 1)))], axis=0)
    return gx, gwi, gwo, small


def _coords():
    return lax.axis_index("x"), lax.axis_index("y"), lax.axis_index("c")


def _gather_weights(w_in, w_out):
    wo_rows = w_out.shape[0]

    def body(wi_ref, wo_ref, wi_all, wo_all, send_sems, recv_sems):
        x, y, c = _coords()
        me, sibling = (x, y, c), (x, y, 1 - c)
        chips = [(1 - x, y), (x, 1 - y), (1 - x, 1 - y)]
        slot = lambda p: 4 * p[0] + 2 * p[1] + p[2]

        def copies(k, block, to):
            return [pltpu.make_async_remote_copy(
                src_ref=ref.at[slot(block)], dst_ref=ref.at[slot(block)], send_sem=send_sems.at[a, k],
                recv_sem=recv_sems.at[a, k], device_id=to, device_id_type=MESH)
                for a, ref in enumerate((wi_all, wo_all))]

        wi_all[slot(me)] = wi_ref[...].astype(MM)
        wo_all[slot(me)] = wo_ref[...].astype(MM)
        first = copies(0, me, sibling)
        for j, chip in enumerate(chips):
            first += copies(1 + j, me, (*chip, c))
        for cp in first:
            cp.start()
        passed = []
        for j, chip in enumerate(chips):
            for cp in copies(1 + j, (*chip, c), me):
                cp.wait_recv()
            fwd = copies(4 + j, (*chip, c), sibling)
            for cp in fwd:
                cp.start()
            passed += fwd
        for cp in copies(0, sibling, me):
            cp.wait_recv()
        for j, chip in enumerate(chips):
            for cp in copies(4 + j, (*chip, 1 - c), me):
                cp.wait_recv()
        for cp in first + passed:
            cp.wait_send()

    vmem = pl.BlockSpec(memory_space=pltpu.VMEM)
    return pl.pallas_call(
        body, name="gather_weights",
        in_specs=[vmem, vmem], out_specs=[vmem, vmem],
        out_shape=[jax.ShapeDtypeStruct((N_DEV, D_MODEL, SEC_W), MM),
                   jax.ShapeDtypeStruct((N_DEV, wo_rows, D_MODEL), MM)],
        scratch_shapes=[pltpu.SemaphoreType.DMA((2, 7)), pltpu.SemaphoreType.DMA((2, 7))],
        compiler_params=pltpu.CompilerParams(vmem_limit_bytes=VMEM_LIMIT),
    )(w_in, w_out)


def _me():
    x, y, c = _coords()
    return 4 * x + 2 * y + c


def _grad_copies(srcs, lands, send_sems, recv_sems):
    x, y, c = _coords()
    me = 4 * x + 2 * y + c
    copies = []
    for k in range(1, N_DEV):
        px, py, pc = x ^ (k >> 2), y ^ ((k >> 1) & 1), c ^ (k & 1)
        peer = 4 * px + 2 * py + pc
        for a, (src, dst) in enumerate(zip(srcs, lands)):
            copies.append(pltpu.make_async_remote_copy(
                src_ref=src.at[peer], dst_ref=dst.at[me], send_sem=send_sems.at[a * (N_DEV - 1) + k - 1],
                recv_sem=recv_sems.at[a * (N_DEV - 1) + k - 1], device_id=(px, py, pc), device_id_type=MESH))
    return copies


HBM_SPEC = pl.BlockSpec(memory_space=pltpu.HBM)
SEM_SPEC = pl.BlockSpec(memory_space=pltpu.SEMAPHORE)
SPLIT_COPY_EFFECT = pltpu.SideEffectType.DATAFLOW_SIDE_EFFECTING


def _exchange_start(gwi, gwo):
    def body(gwi_ref, gwo_ref, li_ref, lo_ref, send_sems, recv_sems, gwi_thru, gwo_thru, li_thru, lo_thru, token):
        for cp in _grad_copies((gwi_ref, gwo_ref), (li_ref, lo_ref), send_sems, recv_sems):
            cp.start()
        token[...] = jnp.zeros_like(token)

    hbm = lambda a: pltpu.with_memory_space_constraint(a, pltpu.HBM)
    bufs = (gwi, gwo, lax.empty(gwi.shape, gwi.dtype), lax.empty(gwo.shape, gwo.dtype))
    return pl.pallas_call(
        body, name="exchange_start",
        out_shape=(pltpu.SemaphoreType.DMA((2 * (N_DEV - 1),)), pltpu.SemaphoreType.DMA((2 * (N_DEV - 1),)),
                   *[pltpu.HBM(a.shape, a.dtype) for a in bufs], jax.ShapeDtypeStruct((8, 128), F32)),
        in_specs=[HBM_SPEC] * 4,
        out_specs=(SEM_SPEC, SEM_SPEC, HBM_SPEC, HBM_SPEC, HBM_SPEC, HBM_SPEC, pl.BlockSpec(memory_space=pltpu.VMEM)),
        input_output_aliases={0: 2, 1: 3, 2: 4, 3: 5},
        compiler_params=pltpu.CompilerParams(has_side_effects=SPLIT_COPY_EFFECT),
    )(*[hbm(a) for a in bufs])


def _exchange_wait(send_sems, recv_sems, gwi, gwo, li, lo, after):
    def body(gwi_ref, gwo_ref, li_ref, lo_ref, send_sems, recv_sems, after_ref, gwi_out, gwo_out, li_out, lo_out):
        for cp in _grad_copies((gwi_ref, gwo_ref), (li_ref, lo_ref), send_sems, recv_sems):
            cp.wait_send()
            cp.wait_recv()

    return pl.pallas_call(
        body, name="exchange_wait",
        out_shape=tuple(pltpu.HBM(a.shape, a.dtype) for a in (gwi, gwo, li, lo)),
        in_specs=[HBM_SPEC] * 4 + [SEM_SPEC, SEM_SPEC, pl.BlockSpec(memory_space=pl.ANY)],
        out_specs=(HBM_SPEC,) * 4,
        input_output_aliases={0: 0, 1: 1, 2: 2, 3: 3},
        compiler_params=pltpu.CompilerParams(has_side_effects=SPLIT_COPY_EFFECT),
    )(gwi, gwo, li, lo, send_sems, recv_sems, after)


def _gather_small(small):
    def body(sm_ref, ls_ref, send_sems, recv_sems, local_sem):
        x, y, c = _coords()
        me = 4 * x + 2 * y + c
        own = pltpu.make_async_copy(sm_ref, ls_ref.at[me], local_sem)
        own.start()
        sends = []
        for k in range(1, N_DEV):
            peer = (x ^ (k >> 2), y ^ ((k >> 1) & 1), c ^ (k & 1))
            sends.append(pltpu.make_async_remote_copy(
                src_ref=sm_ref, dst_ref=ls_ref.at[me], send_sem=send_sems.at[k - 1], recv_sem=recv_sems.at[k - 1],
                device_id=peer, device_id_type=MESH))
        for cp in sends:
            cp.start()
        for cp in sends:
            cp.wait_recv()
        for cp in sends:
            cp.wait_send()
        own.wait()

    vmem = pl.BlockSpec(memory_space=pltpu.VMEM)
    return pl.pallas_call(
        body, name="gather_small", in_specs=[vmem], out_specs=vmem,
        out_shape=jax.ShapeDtypeStruct((N_DEV,) + small.shape, F32),
        scratch_shapes=[pltpu.SemaphoreType.DMA((N_DEV - 1,)), pltpu.SemaphoreType.DMA((N_DEV - 1,)),
                        pltpu.SemaphoreType.DMA],
    )(small)


def _adamw(w, g, m, v):
    m = ADAM_B1 * m + (1.0 - ADAM_B1) * g
    v = ADAM_B2 * v + (1.0 - ADAM_B2) * (g * g)
    m_hat = m / (1.0 - ADAM_B1 ** ADAM_STEP)
    v_hat = v / (1.0 - ADAM_B2 ** ADAM_STEP)
    return -ADAM_LR * (m_hat / (jnp.sqrt(v_hat) + ADAM_EPS) + ADAM_WD * w), m, v


def _slot_sum(ref, own=None, me=None):
    g = None
    for i in range(N_DEV):
        term = ref[i].astype(F32)
        if own is not None:
            term = jnp.where(i == me, own, term)
        g = term if g is None else g + term
    return g


def _update_matrix(name, me, landed, own, w, m, v, rows):
    r, c = w.shape

    def body(me_ref, l_ref, own_ref, w_ref, m_ref, v_ref, g_ref, d_ref, nm_ref, nv_ref):
        g = _slot_sum(l_ref, own_ref[...].astype(F32), me_ref[0])
        g_ref[...] = g
        d_ref[...], nm_ref[...], nv_ref[...] = _adamw(w_ref[...], g, m_ref[...], v_ref[...])

    blk = pl.BlockSpec((rows, c), lambda i, me_ref: (i, 0))
    return pl.pallas_call(
        body, name=name,
        grid_spec=pltpu.PrefetchScalarGridSpec(
            num_scalar_prefetch=1, grid=(r // rows,),
            in_specs=[pl.BlockSpec((N_DEV, rows, c), lambda i, me_ref: (0, i, 0)),
                      pl.BlockSpec((None, rows, c), lambda i, me_ref: (me_ref[0], i, 0)), blk, blk, blk],
            out_specs=[blk] * 4),
        out_shape=[jax.ShapeDtypeStruct((r, c), F32)] * 4,
        compiler_params=_params("parallel"),
    )(me, landed, own, w, m, v)


def _update_small(landed, lb_logits, ws, ms, vs):
    def body(l_ref, lbl_ref, w_ref, m_ref, v_ref, g_ref, d_ref, nm_ref, nv_ref, loss_ref):
        tot = _slot_sum(l_ref)
        _, dlb = _lower_bound(lbl_ref[...])
        g_lb = tot[1:2, :SEC_W] * dlb
        g = jnp.concatenate([tot[0:1], jnp.concatenate([g_lb, -g_lb], axis=-1),
                             jnp.pad(tot[1:2, SEC_W:], ((0, 0), (0, SEC_W))), tot[2:3]], axis=0)
        g_ref[...] = g
        d_ref[...], nm_ref[...], nv_ref[...] = _adamw(w_ref[...], g, m_ref[...], v_ref[...])
        loss_ref[...] = tot[3:4, 0:1]

    vmem = pl.BlockSpec(memory_space=pltpu.VMEM)
    return pl.pallas_call(
        body, name="update_small", in_specs=[vmem] * 5, out_specs=[vmem] * 5,
        out_shape=[jax.ShapeDtypeStruct((4, D_MODEL), F32)] * 4 + [jax.ShapeDtypeStruct((1, 1), F32)],
    )(landed, lb_logits, ws, ms, vs)


def _pack_small(norm_w, lb_logits, hg_norm_w, final_norm_w):
    return jnp.concatenate([norm_w, lb_logits.reshape(1, D_MODEL),
                            jnp.pad(hg_norm_w, ((0, 0), (0, D_MODEL - SEC_W))), final_norm_w[None, :]], axis=0)


def _unpack_small(a):
    return a[0:1], a[1].reshape(2, SEC_W), a[2:3, :SEC_W], a[3]


def kernel(x, norm_w, w_in, hgrn_lb_logits, hg_norm_w, w_out, final_norm_w, loss_target, m_norm_w, m_w_in, m_hgrn_lb_logits, m_hg_norm_w, m_w_out, m_final_norm_w, v_norm_w, v_w_in, v_hgrn_lb_logits, v_hg_norm_w, v_w_out, v_final_norm_w):
    w_all, wo_all = _gather_weights(w_in[0], w_out[0])
    in_flight = []

    def start_exchange(gwi, gwo):
        *handles, token = _exchange_start(gwi, gwo.reshape(N_DEV, D_MODEL // N_DEV, D_MODEL))
        in_flight.extend(handles)
        return token

    gx, _, _, small = _local_step(x[0], loss_target[0], norm_w, w_all, hgrn_lb_logits, hg_norm_w,
                                  wo_all.reshape(D_MODEL, D_MODEL), final_norm_w, start_exchange)
    ls = _gather_small(small)
    gwi, gwo, li, lo = _exchange_wait(*in_flight, gx)
    me = _me().astype(jnp.int32).reshape(1)
    g_wi, d_wi, nm_wi, nv_wi = _update_matrix("update_w_in", me, li, gwi, w_in[0], m_w_in[0], v_w_in[0], 256)
    g_wo, d_wo, nm_wo, nv_wo = _update_matrix("update_w_out", me, lo, gwo, w_out[0], m_w_out[0], v_w_out[0], 128)
    g_s, d_s, nm_s, nv_s, loss = _update_small(
        ls, hgrn_lb_logits, _pack_small(norm_w, hgrn_lb_logits, hg_norm_w, final_norm_w),
        _pack_small(m_norm_w, m_hgrn_lb_logits, m_hg_norm_w, m_final_norm_w),
        _pack_small(v_norm_w, v_hgrn_lb_logits, v_hg_norm_w, v_final_norm_w))
    outs = []
    for small_out, wi, wo in ((g_s, g_wi, g_wo), (d_s, d_wi, d_wo), (nm_s, nm_wi, nm_wo), (nv_s, nv_wi, nv_wo)):
        nw, lb, hg, fw = _unpack_small(small_out)
        outs += [nw, wi[None], lb, hg, wo[None], fw]
    return (loss[0, 0], gx[None], *outs)
```

```python
import functools

import jax
import jax.numpy as jnp
import numpy as np
from jax import lax
from jax.experimental import pallas as pl
from jax.experimental.pallas import tpu as pltpu

F32 = jnp.float32
MM = jnp.bfloat16
XCH = jnp.bfloat16
NORM_EPS = 1e-6
NEG = -1e30
N_DEV = 8
D_MODEL = 1024
N_SEC = 8
SEC_W = 512
HG_HEADS = 4
HG_D = 128
HG_GROUP = 4
AT_DH = 64
LANES = 128
ATT_BLK = 128
AT_COLS = 512
AT_QB = 8
DILATIONS = (1, 4, 16)
ROPE_THETA = 10000.0
CH = 16
LB_LO, LB_HI = 1e-6, 1.0 - 1e-6
ADAM_LR, ADAM_B1, ADAM_B2, ADAM_EPS, ADAM_WD, ADAM_STEP = 0.001, 0.9, 0.999, 1e-08, 0.01, 10
VMEM_LIMIT = 56 * 1024 * 1024
MESH = pl.DeviceIdType.MESH


def _params(*sem):
    return pltpu.CompilerParams(dimension_semantics=sem, vmem_limit_bytes=VMEM_LIMIT)


def _sigmoid(x):
    return 1.0 / (1.0 + jnp.exp(-x))


def _dot(a, b):
    return jnp.dot(a.astype(MM), b.astype(MM), preferred_element_type=F32)


def _dot_nt(a, b):
    return lax.dot_general(a.astype(MM), b.astype(MM), (((1,), (1,)), ((), ())), preferred_element_type=F32)


def _dot_tn(a, b):
    return lax.dot_general(a.astype(MM), b.astype(MM), (((0,), (0,)), ((), ())), preferred_element_type=F32)


def _split3(g):
    g1 = g.astype(jnp.bfloat16)
    r1 = g - g1.astype(F32)
    g2 = r1.astype(jnp.bfloat16)
    return g1, g2, (r1 - g2.astype(F32)).astype(jnp.bfloat16)


def _tri_dot(tri, g):
    t = tri.astype(jnp.bfloat16)
    g1, g2, g3 = _split3(g)
    d = functools.partial(jnp.dot, preferred_element_type=F32)
    return d(t, g1) + d(t, g2) + d(t, g3)


def _lower_bound(lbl):
    l0, l1 = lbl[0:1, :], lbl[1:2, :]
    m = jnp.maximum(l0, l1)
    e0, e1 = jnp.exp(l0 - m), jnp.exp(l1 - m)
    p = e0 / (e0 + e1)
    inside = (p >= LB_LO) & (p <= LB_HI)
    return jnp.clip(p, LB_LO, LB_HI), jnp.where(inside, p * (e1 / (e0 + e1)), 0.0)


def _iota2(shape, dim):
    return lax.broadcasted_iota(jnp.int32, shape, dim)


def _hgrn_gates(xq, xf, lb):
    sgq = _sigmoid(xq)
    sg = _sigmoid(xf)
    sn = _sigmoid(-xf)
    f = lb + (1.0 - lb) * sg
    return sgq, xq * sgq, sg, sn, f, (1.0 - lb) * sn


def _bdot(a, b, ca, cb):
    return lax.dot_general(a.astype(MM), b.astype(MM), (((ca,), (cb,)), ((0,), (0,))), preferred_element_type=F32)


def _chunk_masks(rb):
    row, col = _iota2((rb, rb), 0), _iota2((rb, rb), 1)
    same = (row // CH) == (col // CH)
    return same & (row >= col), same & (row <= col)


HALF = CH // 2
SLAB_ROWS = HALF * CH + (HALF // 2) * CH


def _write_slabs(slab_ref, g, q3, b3):
    slab = lambda t, rows: q3[:, rows, :] * jnp.exp(jnp.minimum(b3[:, rows, :] - b3[:, t:t + 1, :], 0.0))
    late = slice(HALF, CH)
    for t in range(HALF):
        slab_ref[g, :, t * CH:(t + 1) * CH, :] = slab(t, slice(0, CH)).astype(MM)
    for p in range(HALF // 2):
        t = HALF + 2 * p
        two = jnp.concatenate([slab(t, late), slab(t + 1, late)], axis=1)
        slab_ref[g, :, (HALF + p) * CH:(HALF + p + 1) * CH, :] = two.astype(MM)


def _read_diag(r):
    nc = r.shape[0]
    col, col_late = _iota2((nc, CH, CH), 2), _iota2((nc, HALF, CH), 2)
    a, a_late = jnp.zeros((nc, CH, CH), F32), jnp.zeros((nc, HALF, CH), F32)
    for t in range(HALF):
        a = a + jnp.where(col == t, r[:, t * CH:(t + 1) * CH, :], 0.0)
    for p in range(HALF // 2):
        t, two = HALF + 2 * p, r[:, (HALF + p) * CH:(HALF + p + 1) * CH, :]
        a_late = a_late + jnp.where(col_late == t, two[:, :HALF, :], 0.0) + jnp.where(col_late == t + 1, two[:, HALF:, :], 0.0)
    return a + jnp.concatenate([jnp.zeros_like(a_late), a_late], axis=1)


def _hgrn_fwd(proj, lb_logits, rb=256):
    s = proj.shape[1]
    nb, nc = s // rb, rb // CH

    def body(q_ref, f_ref, i_ref, lbl_ref, o_ref, sst_ref, a_ref, st_ref, slab_ref, states_ref):
        @pl.when(pl.program_id(1) == 0)
        def _():
            st_ref[...] = jnp.zeros_like(st_ref)

        sst_ref[...] = st_ref[...]
        prefix, _ = _chunk_masks(rb)
        c3 = lambda a: a.reshape(nc, CH, HG_D)
        row, col = _iota2((nc, CH, CH), 1), _iota2((nc, CH, CH), 2)
        heads = []
        for g in range(HG_GROUP):
            hs = slice(g * HG_D, (g + 1) * HG_D)
            lb, _ = _lower_bound(lbl_ref[:, hs])
            _, q, _, _, f, kk = _hgrn_gates(q_ref[:, hs], f_ref[:, hs], lb)
            b3 = c3(_tri_dot(prefix, jnp.log(f)))
            q3, kk3, v3 = c3(q), c3(kk), c3(i_ref[:, hs])
            bl3 = b3[:, CH - 1:CH, :]
            _write_slabs(slab_ref, g, q3, b3)
            x_upd = _bdot(v3, kk3 * jnp.exp(bl3 - b3), 1, 1)
            heads.append(dict(hs=hs, kk3=kk3, v3=v3, qe3=q3 * jnp.exp(b3), ebl3=jnp.exp(bl3), x_upd=x_upd))
        for g, hd in enumerate(heads):
            st = st_ref[g]
            for c in range(nc):
                states_ref[g, c] = st
                st = st * hd["ebl3"][c] + hd["x_upd"][c]
            st_ref[g] = st
        for g, hd in enumerate(heads):
            a = _read_diag(_bdot(slab_ref[g], hd["kk3"], 2, 2))
            a = jnp.where(row >= col, a, 0.0)
            a_ref[:, g * CH:(g + 1) * CH] = a.reshape(rb, CH)
            o3 = _bdot(hd["qe3"], states_ref[g], 2, 2) + _bdot(a, hd["v3"], 2, 1)
            o_ref[:, hd["hs"]] = o3.reshape(rb, HG_D)

    wide = HG_GROUP * HG_D
    sec = lambda j: pl.BlockSpec((None, rb, wide), lambda h, i, j=j: (j, i, h))
    return pl.pallas_call(
        body, name="hgrn_fwd", grid=(HG_HEADS // HG_GROUP, nb),
        in_specs=[sec(0), sec(1), sec(2), pl.BlockSpec((2, wide), lambda h, i: (0, h))],
        out_specs=[pl.BlockSpec((rb, wide), lambda h, i: (i, h)),
                   pl.BlockSpec((None, HG_GROUP, HG_D, HG_D), lambda h, i: (i, h, 0, 0)),
                   pl.BlockSpec((rb, HG_GROUP * CH), lambda h, i: (i, h))],
        out_shape=[jax.ShapeDtypeStruct((s, SEC_W), F32),
                   jax.ShapeDtypeStruct((nb, HG_HEADS, HG_D, HG_D), F32),
                   jax.ShapeDtypeStruct((s, HG_HEADS * CH), F32)],
        scratch_shapes=[pltpu.VMEM((HG_GROUP, HG_D, HG_D), F32), pltpu.VMEM((HG_GROUP, nc, SLAB_ROWS, HG_D), MM),
                        pltpu.VMEM((HG_GROUP, nc, HG_D, HG_D), F32)],
        compiler_params=_params("parallel", "arbitrary"),
    )(proj, proj, proj, lb_logits)


def _hgrn_bwd(proj, lb_logits, d_o, sst, a_in, rb=256):
    s = proj.shape[1]
    nb, nc = s // rb, rb // CH

    def body(q_ref, f_ref, i_ref, lbl_ref, do_ref, sst_ref, a_ref, dxq_ref, dxf_ref, dxi_ref, dlb_ref,
             dst_ref, states_ref, dstates_ref, lslab_ref, kslab_ref):
        @pl.when(pl.program_id(1) == 0)
        def _():
            dst_ref[...] = jnp.zeros_like(dst_ref)
            dlb_ref[...] = jnp.zeros_like(dlb_ref)

        prefix, suffix = _chunk_masks(rb)
        c3 = lambda a: a.reshape(nc, CH, HG_D)
        flat = lambda a: a.reshape(rb, HG_D)
        row, col = _iota2((nc, CH, CH), 1), _iota2((nc, CH, CH), 2)
        tril, triu = row >= col, row <= col
        sel = (_iota2((CH, CH * CH), 1) % CH == _iota2((CH, CH * CH), 0)).astype(MM)
        blockdiag = _iota2((nc, CH, CH * CH), 2) // CH == _iota2((nc, CH, CH * CH), 1)
        tile = lambda m: jnp.where(blockdiag, _dot(m.reshape(rb, CH), sel).reshape(nc, CH, CH * CH), 0.0)
        last = _iota2((nc, CH, HG_D), 1) == CH - 1
        heads = []
        for g in range(HG_GROUP):
            hs = slice(g * HG_D, (g + 1) * HG_D)
            lb, _ = _lower_bound(lbl_ref[:, hs])
            xq = q_ref[:, hs]
            sgq, q, sg, sn, f, kk = _hgrn_gates(xq, f_ref[:, hs], lb)
            b3 = c3(_tri_dot(prefix, jnp.log(f)))
            q3, kk3, v3, do3 = c3(q), c3(kk), c3(i_ref[:, hs]), c3(do_ref[:, hs])
            bl3 = b3[:, CH - 1:CH, :]
            eb3, ebl3, dec3 = jnp.exp(b3), jnp.exp(bl3), jnp.exp(bl3 - b3)
            qe3, kd3 = q3 * eb3, kk3 * dec3
            x_upd, y_upd = _bdot(v3, kd3, 1, 1), _bdot(do3, qe3, 1, 1)
            for t in range(CH):
                bt = b3[:, t:t + 1, :]
                lslab_ref[g, :, t * CH:(t + 1) * CH, :] = (q3 * jnp.exp(jnp.minimum(b3 - bt, 0.0))).astype(MM)
                kslab_ref[g, :, t * CH:(t + 1) * CH, :] = (kk3 * jnp.exp(jnp.minimum(bt - b3, 0.0))).astype(MM)
            d_a = jnp.where(tril, _bdot(do3, v3, 2, 2), 0.0)
            d_at = jnp.where(triu, _bdot(v3, do3, 2, 2), 0.0)
            heads.append(dict(hs=hs, lb=lb, xq=xq, sgq=sgq, sg=sg, sn=sn, f=f, q3=q3, kk3=kk3, v3=v3, do3=do3,
                              eb3=eb3, ebl3=ebl3, dec3=dec3, qe3=qe3, kd3=kd3, x_upd=x_upd, y_upd=y_upd,
                              d_a=d_a, d_at=d_at))
        for g, hd in enumerate(heads):
            st = sst_ref[g]
            for c in range(nc):
                states_ref[g, c] = st
                st = st * hd["ebl3"][c] + hd["x_upd"][c]
            dst = dst_ref[g]
            for c in reversed(range(nc)):
                dstates_ref[g, c] = dst
                dst = dst * hd["ebl3"][c] + hd["y_upd"][c]
            dst_ref[g] = dst
        for g, hd in enumerate(heads):
            q3, v3, do3, kd3 = hd["q3"], hd["v3"], hd["do3"], hd["kd3"]
            states, dstates = states_ref[g], dstates_ref[g]
            hd["dqe"] = _bdot(do3, states, 2, 1)
            hd["dkd"] = _bdot(v3, dstates, 2, 1)
            a = a_ref[:, g * CH:(g + 1) * CH].reshape(nc, CH, CH)
            hd["dv"] = _bdot(kd3, dstates, 2, 2) + _bdot(a, do3, 1, 1)
            hd["dq_in"] = _bdot(tile(hd["d_a"]), kslab_ref[g], 2, 1)
            hd["dk_in"] = _bdot(tile(hd["d_at"]), lslab_ref[g], 2, 1)
            hd["ss"] = jnp.sum(dstates * states, axis=1, keepdims=True)
        for g, hd in enumerate(heads):
            q3, kk3, eb3, ebl3, dec3, qe3, kd3 = (hd[k] for k in ("q3", "kk3", "eb3", "ebl3", "dec3", "qe3", "kd3"))
            dqe, dkd, dq_in, dk_in = hd["dqe"], hd["dkd"], hd["dq_in"], hd["dk_in"]
            dkd_kd = dkd * kd3
            db = dqe * qe3 - dkd_kd + q3 * dq_in - kk3 * dk_in
            dbl = jnp.sum(dkd_kd, axis=1, keepdims=True) + hd["ss"] * ebl3
            dg = _tri_dot(suffix, flat(db + jnp.where(last, dbl, 0.0)))
            df = dg / hd["f"] - flat(dkd * dec3 + dk_in)
            xq, sgq, hs = hd["xq"], hd["sgq"], hd["hs"]
            dxq_ref[:, hs] = (flat(dqe * eb3 + dq_in) * (sgq * (1.0 + xq * (1.0 - sgq)))).astype(MM)
            dxf_ref[:, hs] = (df * (1.0 - hd["lb"]) * hd["sg"] * hd["sn"]).astype(MM)
            dxi_ref[:, hs] = flat(hd["dv"]).astype(MM)
            dlb_ref[:, hs] += jnp.sum(df * hd["sn"], axis=0, keepdims=True)

    wide = HG_GROUP * HG_D
    rev = lambda i: nb - 1 - i
    sec = lambda j: pl.BlockSpec((None, rb, wide), lambda h, i, j=j: (j, rev(i), h))
    blk = pl.BlockSpec((rb, wide), lambda h, i: (rev(i), h))
    state = (pltpu.VMEM((HG_GROUP, nc, HG_D, HG_D), F32), pltpu.VMEM((HG_GROUP, nc, CH * CH, HG_D), MM))
    return pl.pallas_call(
        body, name="hgrn_bwd", grid=(HG_HEADS // HG_GROUP, nb),
        in_specs=[sec(0), sec(1), sec(2), pl.BlockSpec((2, wide), lambda h, i: (0, h)), blk,
                  pl.BlockSpec((None, HG_GROUP, HG_D, HG_D), lambda h, i: (rev(i), h, 0, 0)),
                  pl.BlockSpec((rb, HG_GROUP * CH), lambda h, i: (rev(i), h))],
        out_specs=[blk, blk, blk, pl.BlockSpec((1, wide), lambda h, i: (0, h))],
        out_shape=[jax.ShapeDtypeStruct((s, SEC_W), MM)] * 3 + [jax.ShapeDtypeStruct((1, SEC_W), F32)],
        scratch_shapes=[pltpu.VMEM((HG_GROUP, HG_D, HG_D), F32), state[0], state[0], state[1], state[1]],
        compiler_params=_params("parallel", "arbitrary"),
    )(proj, proj, proj, lb_logits, d_o, sst, a_in)


def _rope_tables(s):
    half = AT_DH // 2
    inv_freq = np.float32(1.0) / (np.float32(ROPE_THETA) ** (np.arange(half, dtype=np.float32) / np.float32(half)))
    ang = np.arange(s, dtype=np.float32)[:, None] * inv_freq[None, :]
    cos, sin = np.cos(ang), np.sin(ang)
    return np.concatenate([cos] * 4, axis=-1), np.concatenate([-sin, sin] * 2, axis=-1)


def _rope128(x, cos, sin):
    lo = (_iota2(x.shape, 1) % AT_DH) < AT_DH // 2
    rot = jnp.where(lo, pltpu.roll(x, LANES - AT_DH // 2, 1), pltpu.roll(x, AT_DH // 2, 1))
    return x * cos + rot * sin


LANE_GROUPS = SEC_W // LANES


def _set_lanes(ref, val):
    for j in range(LANE_GROUPS):
        ref[j] = val[:, j * LANES:(j + 1) * LANES]


def _get_lanes(ref):
    return jnp.concatenate([ref[j] for j in range(LANE_GROUPS)], axis=-1)


def _to_view(src_ref, dst_ref, d):
    n = src_ref.shape[1] // d
    for r in range(d):
        rows = pl.ds(r, n, stride=d) if d > 1 else slice(None)
        for j in range(LANE_GROUPS):
            c0 = r * SEC_W + j * LANES
            dst_ref[:, c0:c0 + LANES] = src_ref.at[j][rows, :].astype(dst_ref.dtype)


def _from_view(src_ref, dst_ref, d):
    n = dst_ref.shape[1] // d
    for r in range(d):
        for j in range(LANE_GROUPS):
            c0 = r * SEC_W + j * LANES
            dst_ref.at[j][pl.ds(r, n, stride=d), :] = src_ref[:, c0:c0 + LANES].astype(dst_ref.dtype)


def _view_spec(tm, d):
    return pl.BlockSpec((tm // d, d * SEC_W), lambda i: (i, 0))


def _view_shape(s, d, dtype):
    return jax.ShapeDtypeStruct((s // d, d * SEC_W), dtype)


PROJ_KEPT = (0, 1, 2, 3, 7)


def _inproj_fwd(x, norm_w, w_all, cos, sin, tm=512):
    s = x.shape[0]

    def body(x_ref, nw_ref, w_ref, cos_ref, sin_ref, proj_ref, *refs):
        outs, (qs_ref, ks_ref, vs_ref) = refs[:-3], refs[-3:]
        xv = x_ref[...]
        rstd = lax.rsqrt(jnp.mean(xv * xv, axis=-1, keepdims=True) + NORM_EPS)
        u = (xv * rstd * nw_ref[...]).astype(MM)
        for slot, j in enumerate(PROJ_KEPT):
            proj_ref[slot] = jnp.dot(u, w_ref[j], preferred_element_type=F32)
        q, k, v = [jnp.dot(u, w_ref[j], preferred_element_type=F32) for j in (4, 5, 6)]
        c, sn = cos_ref[...], sin_ref[...]
        for g in range(LANE_GROUPS):
            sl = slice(g * LANES, (g + 1) * LANES)
            qs_ref[g] = _rope128(q[:, sl], c, sn) * (AT_DH ** -0.5)
            ks_ref[g] = _rope128(k[:, sl], c, sn)
            vs_ref[g] = v[:, sl]
        for i, d in enumerate(DILATIONS):
            for src_ref, dst_ref in zip((qs_ref, ks_ref, vs_ref), outs[3 * i:3 * i + 3]):
                _to_view(src_ref, dst_ref, d)

    tab = pl.BlockSpec((tm, LANES), lambda i: (i, 0))
    return pl.pallas_call(
        body, name="inproj_fwd", grid=(s // tm,),
        in_specs=[pl.BlockSpec((tm, D_MODEL), lambda i: (i, 0)),
                  pl.BlockSpec((1, D_MODEL), lambda i: (0, 0)),
                  pl.BlockSpec((N_SEC, D_MODEL, SEC_W), lambda i: (0, 0, 0)), tab, tab],
        out_specs=[pl.BlockSpec((len(PROJ_KEPT), tm, SEC_W), lambda i: (0, i, 0))]
                  + [_view_spec(tm, d) for d in DILATIONS for _ in range(3)],
        out_shape=[jax.ShapeDtypeStruct((len(PROJ_KEPT), s, SEC_W), F32)]
                  + [_view_shape(s, d, MM) for d in DILATIONS for _ in range(3)],
        scratch_shapes=[pltpu.VMEM((LANE_GROUPS, tm, LANES), F32)] * 3,
        compiler_params=_params("parallel"),
    )(x, norm_w, w_all, cos, sin)


def _band_mask(first_ok, second_ok):
    row, col = _iota2((ATT_BLK, 2 * ATT_BLK), 0), _iota2((ATT_BLK, 2 * ATT_BLK), 1)
    return ((col < ATT_BLK) & (col >= row) & first_ok) | ((col >= ATT_BLK) & ((col - ATT_BLK) <= row) & second_ok)


def _own_lanes(rows, h):
    lane = _iota2((rows, LANES), 1)
    return (lane < AT_DH) if h == 0 else (lane >= AT_DH)


def _neg_pieces(rows, h):
    lane = _iota2((rows, LANES), 1) - (AT_DH if h == 0 else 0)
    return jnp.where((lane >= 0) & (lane < 3), -1.0, 0.0).astype(MM)


def _units(qb):
    return [(b, slice(g * LANES, (g + 1) * LANES), h) for b in range(qb) for g in range(AT_COLS // LANES) for h in range(2)]


def _sub(b):
    return slice(b * ATT_BLK, (b + 1) * ATT_BLK)


def _band_before(cur_ref, prev_ref, b, sl):
    if b == 0:
        return jnp.concatenate([prev_ref[:, sl], cur_ref[0:ATT_BLK, sl]], axis=0)
    return cur_ref[(b - 1) * ATT_BLK:(b + 1) * ATT_BLK, sl]


def _band_after(cur_ref, next_ref, b, sl):
    if (b + 1) * ATT_BLK == cur_ref.shape[0]:
        return jnp.concatenate([cur_ref[b * ATT_BLK:(b + 1) * ATT_BLK, sl], next_ref[:, sl]], axis=0)
    return cur_ref[b * ATT_BLK:(b + 2) * ATT_BLK, sl]


def _attn_specs(rows):
    qb = min(AT_QB, rows // ATT_BLK)
    assert rows % (qb * ATT_BLK) == 0
    last = rows // ATT_BLK - 1
    cur = pl.BlockSpec((qb * ATT_BLK, AT_COLS), lambda c, n: (n, c))
    prev = pl.BlockSpec((ATT_BLK, AT_COLS), lambda c, n: (jnp.maximum(qb * n - 1, 0), c))
    nxt = pl.BlockSpec((ATT_BLK, AT_COLS), lambda c, n: (jnp.minimum(qb * (n + 1), last), c))
    return qb, cur, prev, nxt


def _stack_heads(a):
    h0 = _own_lanes(a.shape[0], 0)
    zero = jnp.zeros_like(a)
    return jnp.concatenate([jnp.where(h0, a, zero), jnp.where(h0, zero, a)], axis=0)


def _unstack_heads(a2):
    return jnp.where(_own_lanes(ATT_BLK, 0), a2[:ATT_BLK], a2[ATT_BLK:])


def _attn_fwd(qr, kr, vr, d):
    rows, cols = qr.shape
    qb, cur, prev, nxt = _attn_specs(rows)
    nb = rows // (qb * ATT_BLK)

    def body(q_ref, kc_ref, kp_ref, vc_ref, vp_ref, o_ref, lse_ref):
        twice = lambda m: jnp.concatenate([m, m], axis=0)
        masks = {True: twice(_band_mask(pl.program_id(1) > 0, True)), False: twice(_band_mask(True, True))}
        ones = jnp.ones((2 * ATT_BLK, LANES), MM)
        units = [(b, sl) for b, sl, h in _units(qb) if h == 0]
        scs = [jnp.where(masks[b == 0], _dot_nt(_stack_heads(q_ref[_sub(b), sl]), _band_before(kc_ref, kp_ref, b, sl)),
                         NEG) for b, sl in units]
        ms = [jnp.max(sc, axis=-1, keepdims=True) for sc in scs]
        ps = [jnp.exp(sc - m).astype(MM) for sc, m in zip(scs, ms)]
        ols = [jnp.dot(p, jnp.concatenate([_band_before(vc_ref, vp_ref, b, sl), ones], axis=1),
                       preferred_element_type=F32) for p, (b, sl) in zip(ps, units)]
        for (b, sl), m, ol in zip(units, ms, ols):
            l = _unstack_heads(ol[:, LANES:])
            o_ref[_sub(b), sl] = _unstack_heads(ol[:, :LANES]) / l
            lse_ref[_sub(b), sl] = _unstack_heads(jnp.broadcast_to(m, (2 * ATT_BLK, LANES))) + jnp.log(l)

    o, lse = pl.pallas_call(
        body, name=f"attn_fwd_d{d}", grid=(cols // AT_COLS, nb),
        in_specs=[cur, cur, prev, cur, prev], out_specs=[cur, cur],
        out_shape=[jax.ShapeDtypeStruct((rows, cols), F32)] * 2,
        compiler_params=_params("parallel", "parallel"),
    )(qr, kr, kr, vr, vr)
    return o, lse


def _attn_bwd_dq(qr, kr, vr, do, lse, delta, d):
    rows, cols = qr.shape
    qb, cur, prev, nxt = _attn_specs(rows)
    nb = rows // (qb * ATT_BLK)

    def body(q_ref, kc_ref, kp_ref, vc_ref, vp_ref, do_ref, lse_ref, dl_ref, dq_ref):
        masks = {True: _band_mask(pl.program_id(1) > 0, True), False: _band_mask(True, True)}
        units = _units(qb)
        sms, dps = [], []
        for b, sl, h in units:
            own, own_b, neg = _own_lanes(ATT_BLK, h), _own_lanes(2 * ATT_BLK, h), _neg_pieces(2 * ATT_BLK, h)
            sms.append(_dot_nt(jnp.where(own, q_ref[_sub(b), sl], lse_ref[_sub(b), sl]),
                               jnp.where(own_b, _band_before(kc_ref, kp_ref, b, sl), neg)))
            dps.append(_dot_nt(jnp.where(own, do_ref[_sub(b), sl], dl_ref[_sub(b), sl]),
                               jnp.where(own_b, _band_before(vc_ref, vp_ref, b, sl), neg)))
        dss = [(jnp.exp(jnp.where(masks[b == 0], sm, NEG)) * dp).astype(MM)
               for sm, dp, (b, _, _) in zip(sms, dps, units)]
        dqs = [jnp.dot(ds, _band_before(kc_ref, kp_ref, b, sl), preferred_element_type=F32) * (AT_DH ** -0.5)
               for ds, (b, sl, _) in zip(dss, units)]
        for i in range(0, len(units), 2):
            b, sl, _ = units[i]
            dq_ref[_sub(b), sl] = jnp.where(_own_lanes(ATT_BLK, 0), dqs[i], dqs[i + 1]).astype(dq_ref.dtype)

    dq = pl.pallas_call(
        body, name=f"attn_bwd_dq_d{d}", grid=(cols // AT_COLS, nb),
        in_specs=[cur, cur, prev, cur, prev, cur, cur, cur], out_specs=cur,
        out_shape=jax.ShapeDtypeStruct((rows, cols), MM),
        compiler_params=_params("parallel", "parallel"),
    )(qr, kr, kr, vr, vr, do, lse, delta)
    return dq


def _attn_bwd_dkv(qr, kr, vr, do, lse, delta, d):
    rows, cols = qr.shape
    qb, cur, prev, nxt = _attn_specs(rows)
    nb = rows // (qb * ATT_BLK)

    def body(k_ref, v_ref, qc_ref, qn_ref, doc_ref, don_ref, lsec_ref, lsen_ref, dlc_ref, dln_ref,
             dk_ref, dv_ref):
        masks = {True: _band_mask(True, pl.program_id(1) < nb - 1), False: _band_mask(True, True)}
        units = _units(qb)
        sms, dps = [], []
        for b, sl, h in units:
            own, own_b, neg = _own_lanes(ATT_BLK, h), _own_lanes(2 * ATT_BLK, h), _neg_pieces(ATT_BLK, h)
            sms.append(_dot_nt(jnp.where(own, k_ref[_sub(b), sl], neg),
                               jnp.where(own_b, _band_after(qc_ref, qn_ref, b, sl),
                                         _band_after(lsec_ref, lsen_ref, b, sl))))
            dps.append(_dot_nt(jnp.where(own, v_ref[_sub(b), sl], neg),
                               jnp.where(own_b, _band_after(doc_ref, don_ref, b, sl),
                                         _band_after(dlc_ref, dln_ref, b, sl))))
        ps = [jnp.exp(jnp.where(masks[b == qb - 1], sm, NEG)) for sm, (b, _, _) in zip(sms, units)]
        dss = [(p * dp).astype(MM) for p, dp in zip(ps, dps)]
        dvs = [jnp.dot(p.astype(MM), _band_after(doc_ref, don_ref, b, sl), preferred_element_type=F32)
               for p, (b, sl, _) in zip(ps, units)]
        dks = [jnp.dot(ds, _band_after(qc_ref, qn_ref, b, sl), preferred_element_type=F32)
               for ds, (b, sl, _) in zip(dss, units)]
        head0 = _own_lanes(ATT_BLK, 0)
        for i in range(0, len(units), 2):
            b, sl, _ = units[i]
            dk_ref[_sub(b), sl] = jnp.where(head0, dks[i], dks[i + 1]).astype(dk_ref.dtype)
            dv_ref[_sub(b), sl] = jnp.where(head0, dvs[i], dvs[i + 1]).astype(dv_ref.dtype)

    dk, dv = pl.pallas_call(
        body, name=f"attn_bwd_dkv_d{d}", grid=(cols // AT_COLS, nb),
        in_specs=[cur, cur, cur, nxt, cur, nxt, cur, nxt, cur, nxt], out_specs=[cur, cur],
        out_shape=[jax.ShapeDtypeStruct((rows, cols), MM)] * 2,
        compiler_params=_params("parallel", "parallel"),
    )(kr, vr, qr, qr, do, do, lse, lse, delta, delta)
    return dk, dv


def _head_sum(a, width):
    parts = []
    for j in range(a.shape[1] // width):
        sm = jnp.sum(a[:, j * width:(j + 1) * width], axis=-1, keepdims=True)
        parts.append(jnp.broadcast_to(sm, (a.shape[0], width)))
    return jnp.concatenate(parts, axis=-1)


def _partner_sum(a):
    swap = (_iota2((LANES, LANES), 0) // AT_DH != _iota2((LANES, LANES), 1) // AT_DH).astype(jnp.bfloat16)
    d = functools.partial(jnp.dot, preferred_element_type=F32)
    parts = _split3(a)
    return jnp.concatenate([d(parts[0][:, sl], swap) + d(parts[1][:, sl], swap) + d(parts[2][:, sl], swap)
                            for sl in (slice(j * LANES, (j + 1) * LANES) for j in range(a.shape[1] // LANES))], axis=-1)


def _partner_value(x):
    return jnp.concatenate([pltpu.roll(x[:, j * LANES:(j + 1) * LANES], AT_DH, 1) for j in range(x.shape[1] // LANES)],
                           axis=-1)


def _pieces(xs):
    hi = xs.astype(jnp.bfloat16).astype(F32)
    mid = (xs - hi).astype(jnp.bfloat16).astype(F32)
    lo = (xs - hi - mid).astype(jnp.bfloat16).astype(F32)
    lane = _iota2(xs.shape, 1) % AT_DH
    return jnp.where(lane == 0, hi, jnp.where(lane == 1, mid, jnp.where(lane == 2, lo, 0.0)))


def _mid(x, tgt, proj, o_hg, o_at, lse_at, hg_norm_w, final_norm_w, wo_all, tm=256):
    s = x.shape[0]
    nb = s // tm

    def body(x_ref, t_ref, hgz_ref, atz_ref, ohg_ref, o1_ref, o2_ref, o3_ref, l1_ref, l2_ref, l3_ref,
             g_ref, fw_ref, wo_ref,
             dh_ref, dohg_ref, dhgz_ref, datz_ref, do1_ref, do2_ref, do3_ref, dl1_ref, dl2_ref, dl3_ref,
             lp1_ref, lp2_ref, lp3_ref,
             gwo_ref, gfw_ref, ghg_ref, loss_ref, nat_ref, stage_ref, gwo_acc):
        @pl.when(pl.program_id(0) == 0)
        def _():
            gwo_acc[...] = jnp.zeros_like(gwo_acc)
            gfw_ref[...] = jnp.zeros_like(gfw_ref)
            ghg_ref[...] = jnp.zeros_like(ghg_ref)
            loss_ref[...] = jnp.zeros_like(loss_ref)

        ohg, g = ohg_ref[...], g_ref[...]
        rs = lax.rsqrt(_head_sum(ohg * ohg, HG_D) * (1.0 / HG_D) + NORM_EPS)
        on = ohg * rs
        hgz = hgz_ref[...]
        sz = _sigmoid(hgz)
        gate_hg = hgz * sz
        lses, outs = [l1_ref[...]], [o1_ref[...]]
        for k, (d, l_ref, o_ref) in enumerate(zip(DILATIONS[1:], (l2_ref, l3_ref), (o2_ref, o3_ref))):
            _from_view(l_ref, nat_ref.at[2 * k], d)
            _from_view(o_ref, nat_ref.at[2 * k + 1], d)
            lses.append(_get_lanes(nat_ref.at[2 * k]))
            outs.append(_get_lanes(nat_ref.at[2 * k + 1]))
        mx = jnp.maximum(jnp.maximum(lses[0], lses[1]), lses[2])
        es = [jnp.exp(l - mx) for l in lses]
        den = es[0] + es[1] + es[2]
        ws = [e / den for e in es]
        oat = ws[0] * outs[0] + ws[1] * outs[1] + ws[2] * outs[2]
        atz = atz_ref[...]
        sa = _sigmoid(atz)
        gate_at = atz * sa
        mixed = jnp.concatenate([on * g * gate_hg, oat * gate_at], axis=-1).astype(MM)
        h = x_ref[...] + jnp.dot(mixed, wo_ref[...], preferred_element_type=F32)
        rstd = lax.rsqrt(jnp.mean(h * h, axis=-1, keepdims=True) + NORM_EPS)
        hn = h * rstd
        fw = fw_ref[...]
        err = hn * fw - t_ref[...]
        loss_ref[...] += 0.5 * jnp.sum(jnp.mean(err * err, axis=-1, keepdims=True), axis=0, keepdims=True)
        dout = err * (1.0 / D_MODEL)
        gfw_ref[...] += jnp.sum(dout * hn, axis=0, keepdims=True)
        dhn = dout * fw
        dh = rstd * (dhn - hn * jnp.mean(dhn * hn, axis=-1, keepdims=True))
        dh_ref[...] = dh
        dh_mm = dh.astype(MM)
        gwo_acc[...] += _dot_tn(mixed, dh_mm)

        @pl.when(pl.program_id(0) == nb - 1)
        def _():
            gwo_ref[...] = gwo_acc[...].astype(gwo_ref.dtype)

        dmixed = _dot_nt(dh_mm, wo_ref[...])
        dm_hg = dmixed[:, :SEC_W]
        d_ong = dm_hg * gate_hg
        dhgz_ref[...] = (dm_hg * (on * g) * (sz * (1.0 + hgz * (1.0 - sz)))).astype(MM)
        ghg_ref[...] += jnp.sum(d_ong * on, axis=0, keepdims=True)
        d_on = d_ong * g
        dohg_ref[...] = rs * (d_on - on * (_head_sum(d_on * on, HG_D) * (1.0 / HG_D)))
        dm_at = dmixed[:, SEC_W:]
        d_oat = dm_at * gate_at
        datz_ref[...] = (dm_at * oat * (sa * (1.0 + atz * (1.0 - sa)))).astype(MM)
        lse_all = mx + jnp.log(den)
        for val, dst_refs in ((d_oat, (do1_ref, do2_ref, do3_ref)),
                              (_pieces(_partner_sum(d_oat * oat)), (dl1_ref, dl2_ref, dl3_ref)),
                              (_pieces(_partner_value(lse_all)), (lp1_ref, lp2_ref, lp3_ref))):
            _set_lanes(stage_ref, val)
            for d, dst_ref in zip(DILATIONS, dst_refs):
                _to_view(stage_ref, dst_ref, d)

    row = lambda w: pl.BlockSpec((tm, w), lambda i: (i, 0))
    sec = lambda j: pl.BlockSpec((None, tm, SEC_W), lambda i, j=j: (j, i, 0))
    const = lambda shp: pl.BlockSpec(shp, lambda i: (0,) * len(shp))
    half = row(SEC_W)
    views = [_view_spec(tm, d) for d in DILATIONS]
    return pl.pallas_call(
        body, name="mid", grid=(nb,),
        in_specs=[row(D_MODEL), row(D_MODEL), sec(PROJ_KEPT.index(3)), sec(PROJ_KEPT.index(7)), half] + views * 2
                 + [const((1, SEC_W)), const((1, D_MODEL)), const((D_MODEL, D_MODEL))],
        out_specs=[row(D_MODEL)] + [half] * 3 + views * 3
                  + [const((D_MODEL, D_MODEL)), const((1, D_MODEL)), const((1, SEC_W)), const((1, 1))],
        out_shape=[jax.ShapeDtypeStruct((s, D_MODEL), F32), jax.ShapeDtypeStruct((s, SEC_W), F32)]
                  + [jax.ShapeDtypeStruct((s, SEC_W), MM)] * 2
                  + [_view_shape(s, d, MM) for d in DILATIONS] * 3
                  + [jax.ShapeDtypeStruct((D_MODEL, D_MODEL), XCH), jax.ShapeDtypeStruct((1, D_MODEL), F32),
                     jax.ShapeDtypeStruct((1, SEC_W), F32), jax.ShapeDtypeStruct((1, 1), F32)],
        scratch_shapes=[pltpu.VMEM((4, LANE_GROUPS, tm, LANES), F32), pltpu.VMEM((LANE_GROUPS, tm, LANES), F32),
                        pltpu.VMEM((D_MODEL, D_MODEL), F32)],
        compiler_params=_params("arbitrary"),
    )(x, tgt, proj, proj, o_hg, *o_at, *lse_at, hg_norm_w, final_norm_w, wo_all)


def _section_specs(dsecs, tm):
    return [pl.BlockSpec((tm, SEC_W), lambda i: (i, 0)) if k is None
            else pl.BlockSpec((None, tm, SEC_W), lambda i, k=k: (k, i, 0)) for _, k in dsecs]


def _inproj_bwd_x(x, norm_w, w_all, dh, dsecs, token, tm=512):
    s = x.shape[0]

    def body(x_ref, nw_ref, w_ref, dh_ref, tok_ref, *refs):
        sec_refs, (gx_ref, gnw_ref) = refs[:N_SEC], refs[N_SEC:]

        @pl.when(pl.program_id(0) == 0)
        def _():
            gnw_ref[...] = jnp.zeros_like(gnw_ref)

        du = jnp.zeros((tm, D_MODEL), F32)
        for j in range(N_SEC):
            du = du + _dot_nt(sec_refs[j][...], w_ref[j])
        xv, nw = x_ref[...], nw_ref[...]
        rstd = lax.rsqrt(jnp.mean(xv * xv, axis=-1, keepdims=True) + NORM_EPS)
        xn = xv * rstd
        gnw_ref[...] += jnp.sum(du * xn, axis=0, keepdims=True)
        dxn = du * nw
        dx = rstd * (dxn - xn * jnp.mean(dxn * xn, axis=-1, keepdims=True))
        gx_ref[...] = (dh_ref[...] + tok_ref[0:1, 0:1]) + dx

    row = lambda w: pl.BlockSpec((tm, w), lambda i: (i, 0))
    const = lambda shp: pl.BlockSpec(shp, lambda i: (0,) * len(shp))
    return pl.pallas_call(
        body, name="inproj_bwd_x", grid=(s // tm,),
        in_specs=[row(D_MODEL), const((1, D_MODEL)), const((N_SEC, D_MODEL, SEC_W)), row(D_MODEL), const((8, 128))]
                 + _section_specs(dsecs, tm),
        out_specs=[row(D_MODEL), const((1, D_MODEL))],
        out_shape=[jax.ShapeDtypeStruct((s, D_MODEL), F32), jax.ShapeDtypeStruct((1, D_MODEL), F32)],
        compiler_params=_params("arbitrary"),
    )(x, norm_w, w_all, dh, token, *[a for a, _ in dsecs])


def _inproj_bwd_w(x, norm_w, dsec, dq_r, dk_r, dv, cos, sin, tm=512):
    s = x.shape[0]
    nb = s // tm

    def body(x_ref, nw_ref, s0, s1, s2, s3, s7, q1, q2, q3, k1, k2, k3, v1, v2, v3, cos_ref, sin_ref,
             gw_hbm, datt_ref, acc_ref, stage_ref, nat_ref):
        @pl.when(pl.program_id(0) == 0)
        def _():
            acc_ref[...] = jnp.zeros_like(acc_ref)

        def total(refs):
            acc = refs[0][...].astype(F32)
            for d, ref in zip(DILATIONS[1:], refs[1:]):
                _from_view(ref, nat_ref, d)
                acc = acc + _get_lanes(nat_ref)
            return acc

        c, sn = cos_ref[...], -sin_ref[...]
        unrot = lambda a: jnp.concatenate(
            [_rope128(a[:, j * LANES:(j + 1) * LANES], c, sn) for j in range(LANE_GROUPS)], axis=-1)
        att = [a.astype(MM) for a in (unrot(total((q1, q2, q3))), unrot(total((k1, k2, k3))), total((v1, v2, v3)))]
        for j, a in enumerate(att):
            datt_ref[j] = a
        xv = x_ref[...]
        rstd = lax.rsqrt(jnp.mean(xv * xv, axis=-1, keepdims=True) + NORM_EPS)
        u_t = (xv * rstd * nw_ref[...]).T.astype(MM)
        for j, dsj in enumerate((s0[...], s1[...], s2[...], s3[...], *att, s7[...])):
            acc_ref[j] += jnp.dot(u_t, dsj, preferred_element_type=F32)

        @pl.when(pl.program_id(0) == nb - 1)
        def _():
            for j in range(N_SEC):
                stage_ref[...] = acc_ref[j].astype(stage_ref.dtype)
                pltpu.sync_copy(stage_ref, gw_hbm.at[j])

    row = lambda w: pl.BlockSpec((tm, w), lambda i: (i, 0))
    return pl.pallas_call(
        body, name="inproj_bwd_w", grid=(nb,),
        in_specs=[row(D_MODEL), pl.BlockSpec((1, D_MODEL), lambda i: (0, 0))] + [row(SEC_W)] * 5
                 + [_view_spec(tm, d) for d in DILATIONS] * 3 + [row(LANES), row(LANES)],
        out_specs=[pl.BlockSpec(memory_space=pl.ANY), pl.BlockSpec((3, tm, SEC_W), lambda i: (0, i, 0))],
        out_shape=[jax.ShapeDtypeStruct((N_SEC, D_MODEL, SEC_W), XCH), jax.ShapeDtypeStruct((3, s, SEC_W), MM)],
        scratch_shapes=[pltpu.VMEM((N_SEC, D_MODEL, SEC_W), F32), pltpu.VMEM((D_MODEL, SEC_W), XCH),
                        pltpu.VMEM((LANE_GROUPS, tm, LANES), F32)],
        compiler_params=_params("arbitrary"),
    )(x, norm_w, *dsec, *dq_r, *dk_r, *dv, cos, sin)


def _local_step(x, tgt, norm_w, w_all, lb_logits, hg_norm_w, wo_all, final_norm_w, on_weight_grads):
    s = x.shape[0]
    cos, sin = _rope_tables(s)
    proj, *qkv = _inproj_fwd(x, norm_w, w_all, cos, sin)
    o_hg, sst, a_hg = _hgrn_fwd(proj, lb_logits)
    qkv = [qkv[3 * i:3 * i + 3] for i in range(len(DILATIONS))]
    att = [_attn_fwd(*qkv_d, d) for qkv_d, d in zip(qkv, DILATIONS)]
    (dh, d_ohg, d_hgz, d_atz, do1, do2, do3, dl1, dl2, dl3, lp1, lp2, lp3, gwo, gfw, ghg, loss) = _mid(
        x, tgt, proj, o_hg, [a[0] for a in att], [a[1] for a in att], hg_norm_w, final_norm_w[None, :], wo_all)
    dxq, dxf, dxi, dlb = _hgrn_bwd(proj, lb_logits, d_ohg, sst, a_hg)
    dq_r, dk_r, dv = [], [], []
    for d, qkv_d, do, lp, dl in zip(DILATIONS, qkv, (do1, do2, do3), (lp1, lp2, lp3), (dl1, dl2, dl3)):
        dq_r.append(_attn_bwd_dq(*qkv_d, do, lp, dl, d))
        dk_d, dv_d = _attn_bwd_dkv(*qkv_d, do, lp, dl, d)
        dk_r.append(dk_d)
        dv.append(dv_d)
    gwi, d_att = _inproj_bwd_w(x, norm_w, (dxq, dxf, dxi, d_hgz, d_atz), dq_r, dk_r, dv, cos, sin)
    dsecs = [(dxq, None), (dxf, None), (dxi, None), (d_hgz, None), (d_att, 0), (d_att, 1), (d_att, 2), (d_atz, None)]
    token = on_weight_grads(gwi, gwo)
    gx, gnw = _inproj_bwd_x(x, norm_w, w_all, dh, dsecs, token)
    small = jnp.concatenate([gnw, jnp.concatenate([dlb, ghg], axis=-1), gfw,
                             jnp.pad(loss, ((0, 0), (0, D_MODEL - 1)))], axis=0)
    return gx, gwi, gwo, small


def _coords():
    return lax.axis_index("x"), lax.axis_index("y"), lax.axis_index("c")


def _gather_weights(w_in, w_out):
    wo_rows = w_out.shape[0]

    def body(wi_ref, wo_ref, wi_all, wo_all, send_sems, recv_sems):
        x, y, c = _coords()
        me, sibling = (x, y, c), (x, y, 1 - c)
        chips = [(1 - x, y), (x, 1 - y), (1 - x, 1 - y)]
        slot = lambda p: 4 * p[0] + 2 * p[1] + p[2]

        def copies(k, block, to):
            return [pltpu.make_async_remote_copy(
                src_ref=ref.at[slot(block)], dst_ref=ref.at[slot(block)], send_sem=send_sems.at[a, k],
                recv_sem=recv_sems.at[a, k], device_id=to, device_id_type=MESH)
                for a, ref in enumerate((wi_all, wo_all))]

        wi_all[slot(me)] = wi_ref[...].astype(MM)
        wo_all[slot(me)] = wo_ref[...].astype(MM)
        first = copies(0, me, sibling)
        for j, chip in enumerate(chips):
            first += copies(1 + j, me, (*chip, c))
        for cp in first:
            cp.start()
        passed = []
        for j, chip in enumerate(chips):
            for cp in copies(1 + j, (*chip, c), me):
                cp.wait_recv()
            fwd = copies(4 + j, (*chip, c), sibling)
            for cp in fwd:
                cp.start()
            passed += fwd
        for cp in copies(0, sibling, me):
            cp.wait_recv()
        for j, chip in enumerate(chips):
            for cp in copies(4 + j, (*chip, 1 - c), me):
                cp.wait_recv()
        for cp in first + passed:
            cp.wait_send()

    vmem = pl.BlockSpec(memory_space=pltpu.VMEM)
    return pl.pallas_call(
        body, name="gather_weights",
        in_specs=[vmem, vmem], out_specs=[vmem, vmem],
        out_shape=[jax.ShapeDtypeStruct((N_DEV, D_MODEL, SEC_W), MM),
                   jax.ShapeDtypeStruct((N_DEV, wo_rows, D_MODEL), MM)],
        scratch_shapes=[pltpu.SemaphoreType.DMA((2, 7)), pltpu.SemaphoreType.DMA((2, 7))],
        compiler_params=pltpu.CompilerParams(vmem_limit_bytes=VMEM_LIMIT),
    )(w_in, w_out)


def _me():
    x, y, c = _coords()
    return 4 * x + 2 * y + c


def _grad_copies(srcs, lands, send_sems, recv_sems):
    x, y, c = _coords()
    me = 4 * x + 2 * y + c
    copies = []
    for k in range(1, N_DEV):
        px, py, pc = x ^ (k >> 2), y ^ ((k >> 1) & 1), c ^ (k & 1)
        peer = 4 * px + 2 * py + pc
        for a, (src, dst) in enumerate(zip(srcs, lands)):
            copies.append(pltpu.make_async_remote_copy(
                src_ref=src.at[peer], dst_ref=dst.at[me], send_sem=send_sems.at[a * (N_DEV - 1) + k - 1],
                recv_sem=recv_sems.at[a * (N_DEV - 1) + k - 1], device_id=(px, py, pc), device_id_type=MESH))
    return copies


HBM_SPEC = pl.BlockSpec(memory_space=pltpu.HBM)
SEM_SPEC = pl.BlockSpec(memory_space=pltpu.SEMAPHORE)
SPLIT_COPY_EFFECT = pltpu.SideEffectType.DATAFLOW_SIDE_EFFECTING


def _exchange_start(gwi, gwo):
    def body(gwi_ref, gwo_ref, li_ref, lo_ref, send_sems, recv_sems, gwi_thru, gwo_thru, li_thru, lo_thru, token):
        for cp in _grad_copies((gwi_ref, gwo_ref), (li_ref, lo_ref), send_sems, recv_sems):
            cp.start()
        token[...] = jnp.zeros_like(token)

    hbm = lambda a: pltpu.with_memory_space_constraint(a, pltpu.HBM)
    bufs = (gwi, gwo, lax.empty(gwi.shape, gwi.dtype), lax.empty(gwo.shape, gwo.dtype))
    return pl.pallas_call(
        body, name="exchange_start",
        out_shape=(pltpu.SemaphoreType.DMA((2 * (N_DEV - 1),)), pltpu.SemaphoreType.DMA((2 * (N_DEV - 1),)),
                   *[pltpu.HBM(a.shape, a.dtype) for a in bufs], jax.ShapeDtypeStruct((8, 128), F32)),
        in_specs=[HBM_SPEC] * 4,
        out_specs=(SEM_SPEC, SEM_SPEC, HBM_SPEC, HBM_SPEC, HBM_SPEC, HBM_SPEC, pl.BlockSpec(memory_space=pltpu.VMEM)),
        input_output_aliases={0: 2, 1: 3, 2: 4, 3: 5},
        compiler_params=pltpu.CompilerParams(has_side_effects=SPLIT_COPY_EFFECT),
    )(*[hbm(a) for a in bufs])


def _exchange_wait(send_sems, recv_sems, gwi, gwo, li, lo, after):
    def body(gwi_ref, gwo_ref, li_ref, lo_ref, send_sems, recv_sems, after_ref, gwi_out, gwo_out, li_out, lo_out):
        for cp in _grad_copies((gwi_ref, gwo_ref), (li_ref, lo_ref), send_sems, recv_sems):
            cp.wait_send()
            cp.wait_recv()

    return pl.pallas_call(
        body, name="exchange_wait",
        out_shape=tuple(pltpu.HBM(a.shape, a.dtype) for a in (gwi, gwo, li, lo)),
        in_specs=[HBM_SPEC] * 4 + [SEM_SPEC, SEM_SPEC, pl.BlockSpec(memory_space=pl.ANY)],
        out_specs=(HBM_SPEC,) * 4,
        input_output_aliases={0: 0, 1: 1, 2: 2, 3: 3},
        compiler_params=pltpu.CompilerParams(has_side_effects=SPLIT_COPY_EFFECT),
    )(gwi, gwo, li, lo, send_sems, recv_sems, after)


def _gather_small(small):
    def body(sm_ref, ls_ref, send_sems, recv_sems, local_sem):
        x, y, c = _coords()
        me = 4 * x + 2 * y + c
        own = pltpu.make_async_copy(sm_ref, ls_ref.at[me], local_sem)
        own.start()
        sends = []
        for k in range(1, N_DEV):
            peer = (x ^ (k >> 2), y ^ ((k >> 1) & 1), c ^ (k & 1))
            sends.append(pltpu.make_async_remote_copy(
                src_ref=sm_ref, dst_ref=ls_ref.at[me], send_sem=send_sems.at[k - 1], recv_sem=recv_sems.at[k - 1],
                device_id=peer, device_id_type=MESH))
        for cp in sends:
            cp.start()
        for cp in sends:
            cp.wait_recv()
        for cp in sends:
            cp.wait_send()
        own.wait()

    vmem = pl.BlockSpec(memory_space=pltpu.VMEM)
    return pl.pallas_call(
        body, name="gather_small", in_specs=[vmem], out_specs=vmem,
        out_shape=jax.ShapeDtypeStruct((N_DEV,) + small.shape, F32),
        scratch_shapes=[pltpu.SemaphoreType.DMA((N_DEV - 1,)), pltpu.SemaphoreType.DMA((N_DEV - 1,)),
                        pltpu.SemaphoreType.DMA],
    )(small)


def _adamw(w, g, m, v):
    m = ADAM_B1 * m + (1.0 - ADAM_B1) * g
    v = ADAM_B2 * v + (1.0 - ADAM_B2) * (g * g)
    m_hat = m / (1.0 - ADAM_B1 ** ADAM_STEP)
    v_hat = v / (1.0 - ADAM_B2 ** ADAM_STEP)
    return -ADAM_LR * (m_hat / (jnp.sqrt(v_hat) + ADAM_EPS) + ADAM_WD * w), m, v


def _slot_sum(ref, own=None, me=None):
    g = None
    for i in range(N_DEV):
        term = ref[i].astype(F32)
        if own is not None:
            term = jnp.where(i == me, own, term)
        g = term if g is None else g + term
    return g


def _update_matrix(name, me, landed, own, w, m, v, rows):
    r, c = w.shape

    def body(me_ref, l_ref, own_ref, w_ref, m_ref, v_ref, g_ref, d_ref, nm_ref, nv_ref):
        g = _slot_sum(l_ref, own_ref[...].astype(F32), me_ref[0])
        g_ref[...] = g
        d_ref[...], nm_ref[...], nv_ref[...] = _adamw(w_ref[...], g, m_ref[...], v_ref[...])

    blk = pl.BlockSpec((rows, c), lambda i, me_ref: (i, 0))
    return pl.pallas_call(
        body, name=name,
        grid_spec=pltpu.PrefetchScalarGridSpec(
            num_scalar_prefetch=1, grid=(r // rows,),
            in_specs=[pl.BlockSpec((N_DEV, rows, c), lambda i, me_ref: (0, i, 0)),
                      pl.BlockSpec((None, rows, c), lambda i, me_ref: (me_ref[0], i, 0)), blk, blk, blk],
            out_specs=[blk] * 4),
        out_shape=[jax.ShapeDtypeStruct((r, c), F32)] * 4,
        compiler_params=_params("parallel"),
    )(me, landed, own, w, m, v)


def _update_small(landed, lb_logits, ws, ms, vs):
    def body(l_ref, lbl_ref, w_ref, m_ref, v_ref, g_ref, d_ref, nm_ref, nv_ref, loss_ref):
        tot = _slot_sum(l_ref)
        _, dlb = _lower_bound(lbl_ref[...])
        g_lb = tot[1:2, :SEC_W] * dlb
        g = jnp.concatenate([tot[0:1], jnp.concatenate([g_lb, -g_lb], axis=-1),
                             jnp.pad(tot[1:2, SEC_W:], ((0, 0), (0, SEC_W))), tot[2:3]], axis=0)
        g_ref[...] = g
        d_ref[...], nm_ref[...], nv_ref[...] = _adamw(w_ref[...], g, m_ref[...], v_ref[...])
        loss_ref[...] = tot[3:4, 0:1]

    vmem = pl.BlockSpec(memory_space=pltpu.VMEM)
    return pl.pallas_call(
        body, name="update_small", in_specs=[vmem] * 5, out_specs=[vmem] * 5,
        out_shape=[jax.ShapeDtypeStruct((4, D_MODEL), F32)] * 4 + [jax.ShapeDtypeStruct((1, 1), F32)],
    )(landed, lb_logits, ws, ms, vs)


def _pack_small(norm_w, lb_logits, hg_norm_w, final_norm_w):
    return jnp.concatenate([norm_w, lb_logits.reshape(1, D_MODEL),
                            jnp.pad(hg_norm_w, ((0, 0), (0, D_MODEL - SEC_W))), final_norm_w[None, :]], axis=0)


def _unpack_small(a):
    return a[0:1], a[1].reshape(2, SEC_W), a[2:3, :SEC_W], a[3]


def kernel(x, norm_w, w_in, hgrn_lb_logits, hg_norm_w, w_out, final_norm_w, loss_target, m_norm_w, m_w_in, m_hgrn_lb_logits, m_hg_norm_w, m_w_out, m_final_norm_w, v_norm_w, v_w_in, v_hgrn_lb_logits, v_hg_norm_w, v_w_out, v_final_norm_w):
    w_all, wo_all = _gather_weights(w_in[0], w_out[0])
    in_flight = []

    def start_exchange(gwi, gwo):
        *handles, token = _exchange_start(gwi, gwo.reshape(N_DEV, D_MODEL // N_DEV, D_MODEL))
        in_flight.extend(handles)
        return token

    gx, _, _, small = _local_step(x[0], loss_target[0], norm_w, w_all, hgrn_lb_logits, hg_norm_w,
                                  wo_all.reshape(D_MODEL, D_MODEL), final_norm_w, start_exchange)
    ls = _gather_small(small)
    gwi, gwo, li, lo = _exchange_wait(*in_flight, gx)
    me = _me().astype(jnp.int32).reshape(1)
    g_wi, d_wi, nm_wi, nv_wi = _update_matrix("update_w_in", me, li, gwi, w_in[0], m_w_in[0], v_w_in[0], 256)
    g_wo, d_wo, nm_wo, nv_wo = _update_matrix("update_w_out", me, lo, gwo, w_out[0], m_w_out[0], v_w_out[0], 128)
    g_s, d_s, nm_s, nv_s, loss = _update_small(
        ls, hgrn_lb_logits, _pack_small(norm_w, hgrn_lb_logits, hg_norm_w, final_norm_w),
        _pack_small(m_norm_w, m_hgrn_lb_logits, m_hg_norm_w, m_final_norm_w),
        _pack_small(v_norm_w, v_hgrn_lb_logits, v_hg_norm_w, v_final_norm_w))
    outs = []
    for small_out, wi, wo in ((g_s, g_wi, g_wo), (d_s, d_wi, d_wo), (nm_s, nm_wi, nm_wo), (nv_s, nv_wi, nv_wo)):
        nw, lb, hg, fw = _unpack_small(small_out)
        outs += [nw, wi[None], lb, hg, wo[None], fw]
    return (loss[0, 0], gx[None], *outs)
```

```python
import functools

import jax
import jax.numpy as jnp
import numpy as np
from jax import lax
from jax.experimental import pallas as pl
from jax.experimental.pallas import tpu as pltpu

F32 = jnp.float32
MM = jnp.bfloat16
XCH = jnp.bfloat16
NORM_EPS = 1e-6
NEG = -1e30
N_DEV = 8
D_MODEL = 1024
N_SEC = 8
SEC_W = 512
HG_HEADS = 4
HG_D = 128
HG_GROUP = 4
AT_DH = 64
LANES = 128
ATT_BLK = 128
AT_COLS = 512
AT_QB = 8
DILATIONS = (1, 4, 16)
ROPE_THETA = 10000.0
CH = 16
LB_LO, LB_HI = 1e-6, 1.0 - 1e-6
ADAM_LR, ADAM_B1, ADAM_B2, ADAM_EPS, ADAM_WD, ADAM_STEP = 0.001, 0.9, 0.999, 1e-08, 0.01, 10
VMEM_LIMIT = 56 * 1024 * 1024
MESH = pl.DeviceIdType.MESH


def _params(*sem):
    return pltpu.CompilerParams(dimension_semantics=sem, vmem_limit_bytes=VMEM_LIMIT)


def _sigmoid(x):
    return 1.0 / (1.0 + jnp.exp(-x))


def _dot(a, b):
    return jnp.dot(a.astype(MM), b.astype(MM), preferred_element_type=F32)


def _dot_nt(a, b):
    return lax.dot_general(a.astype(MM), b.astype(MM), (((1,), (1,)), ((), ())), preferred_element_type=F32)


def _dot_tn(a, b):
    return lax.dot_general(a.astype(MM), b.astype(MM), (((0,), (0,)), ((), ())), preferred_element_type=F32)


def _split3(g):
    g1 = g.astype(jnp.bfloat16)
    r1 = g - g1.astype(F32)
    g2 = r1.astype(jnp.bfloat16)
    return g1, g2, (r1 - g2.astype(F32)).astype(jnp.bfloat16)


def _tri_dot(tri, g):
    t = tri.astype(jnp.bfloat16)
    g1, g2, g3 = _split3(g)
    d = functools.partial(jnp.dot, preferred_element_type=F32)
    return d(t, g1) + d(t, g2) + d(t, g3)


def _lower_bound(lbl):
    l0, l1 = lbl[0:1, :], lbl[1:2, :]
    m = jnp.maximum(l0, l1)
    e0, e1 = jnp.exp(l0 - m), jnp.exp(l1 - m)
    p = e0 / (e0 + e1)
    inside = (p >= LB_LO) & (p <= LB_HI)
    return jnp.clip(p, LB_LO, LB_HI), jnp.where(inside, p * (e1 / (e0 + e1)), 0.0)


def _iota2(shape, dim):
    return lax.broadcasted_iota(jnp.int32, shape, dim)


def _hgrn_gates(xq, xf, lb):
    sgq = _sigmoid(xq)
    sg = _sigmoid(xf)
    sn = _sigmoid(-xf)
    f = lb + (1.0 - lb) * sg
    return sgq, xq * sgq, sg, sn, f, (1.0 - lb) * sn


def _bdot(a, b, ca, cb):
    return lax.dot_general(a.astype(MM), b.astype(MM), (((ca,), (cb,)), ((0,), (0,))), preferred_element_type=F32)


def _chunk_masks(rb):
    row, col = _iota2((rb, rb), 0), _iota2((rb, rb), 1)
    same = (row // CH) == (col // CH)
    return same & (row >= col), same & (row <= col)


HALF = CH // 2
SLAB_ROWS = HALF * CH + (HALF // 2) * CH


def _write_slabs(slab_ref, g, q3, b3):
    slab = lambda t, rows: q3[:, rows, :] * jnp.exp(jnp.minimum(b3[:, rows, :] - b3[:, t:t + 1, :], 0.0))
    late = slice(HALF, CH)
    for t in range(HALF):
        slab_ref[g, :, t * CH:(t + 1) * CH, :] = slab(t, slice(0, CH)).astype(MM)
    for p in range(HALF // 2):
        t = HALF + 2 * p
        two = jnp.concatenate([slab(t, late), slab(t + 1, late)], axis=1)
        slab_ref[g, :, (HALF + p) * CH:(HALF + p + 1) * CH, :] = two.astype(MM)


def _read_diag(r):
    nc = r.shape[0]
    col, col_late = _iota2((nc, CH, CH), 2), _iota2((nc, HALF, CH), 2)
    a, a_late = jnp.zeros((nc, CH, CH), F32), jnp.zeros((nc, HALF, CH), F32)
    for t in range(HALF):
        a = a + jnp.where(col == t, r[:, t * CH:(t + 1) * CH, :], 0.0)
    for p in range(HALF // 2):
        t, two = HALF + 2 * p, r[:, (HALF + p) * CH:(HALF + p + 1) * CH, :]
        a_late = a_late + jnp.where(col_late == t, two[:, :HALF, :], 0.0) + jnp.where(col_late == t + 1, two[:, HALF:, :], 0.0)
    return a + jnp.concatenate([jnp.zeros_like(a_late), a_late], axis=1)


def _hgrn_fwd(proj, lb_logits, rb=256):
    s = proj.shape[1]
    nb, nc = s // rb, rb // CH

    def body(q_ref, f_ref, i_ref, lbl_ref, o_ref, sst_ref, a_ref, st_ref, slab_ref, states_ref):
        @pl.when(pl.program_id(1) == 0)
        def _():
            st_ref[...] = jnp.zeros_like(st_ref)

        sst_ref[...] = st_ref[...]
        prefix, _ = _chunk_masks(rb)
        c3 = lambda a: a.reshape(nc, CH, HG_D)
        row, col = _iota2((nc, CH, CH), 1), _iota2((nc, CH, CH), 2)
        heads = []
        for g in range(HG_GROUP):
            hs = slice(g * HG_D, (g + 1) * HG_D)
            lb, _ = _lower_bound(lbl_ref[:, hs])
            _, q, _, _, f, kk = _hgrn_gates(q_ref[:, hs], f_ref[:, hs], lb)
            b3 = c3(_tri_dot(prefix, jnp.log(f)))
            q3, kk3, v3 = c3(q), c3(kk), c3(i_ref[:, hs])
            bl3 = b3[:, CH - 1:CH, :]
            _write_slabs(slab_ref, g, q3, b3)
            x_upd = _bdot(v3, kk3 * jnp.exp(bl3 - b3), 1, 1)
            heads.append(dict(hs=hs, kk3=kk3, v3=v3, qe3=q3 * jnp.exp(b3), ebl3=jnp.exp(bl3), x_upd=x_upd))
        for g, hd in enumerate(heads):
            st = st_ref[g]
            for c in range(nc):
                states_ref[g, c] = st
                st = st * hd["ebl3"][c] + hd["x_upd"][c]
            st_ref[g] = st
        for g, hd in enumerate(heads):
            a = _read_diag(_bdot(slab_ref[g], hd["kk3"], 2, 2))
            a = jnp.where(row >= col, a, 0.0)
            a_ref[:, g * CH:(g + 1) * CH] = a.reshape(rb, CH)
            o3 = _bdot(hd["qe3"], states_ref[g], 2, 2) + _bdot(a, hd["v3"], 2, 1)
            o_ref[:, hd["hs"]] = o3.reshape(rb, HG_D)

    wide = HG_GROUP * HG_D
    sec = lambda j: pl.BlockSpec((None, rb, wide), lambda h, i, j=j: (j, i, h))
    return pl.pallas_call(
        body, name="hgrn_fwd", grid=(HG_HEADS // HG_GROUP, nb),
        in_specs=[sec(0), sec(1), sec(2), pl.BlockSpec((2, wide), lambda h, i: (0, h))],
        out_specs=[pl.BlockSpec((rb, wide), lambda h, i: (i, h)),
                   pl.BlockSpec((None, HG_GROUP, HG_D, HG_D), lambda h, i: (i, h, 0, 0)),
                   pl.BlockSpec((rb, HG_GROUP * CH), lambda h, i: (i, h))],
        out_shape=[jax.ShapeDtypeStruct((s, SEC_W), F32),
                   jax.ShapeDtypeStruct((nb, HG_HEADS, HG_D, HG_D), F32),
                   jax.ShapeDtypeStruct((s, HG_HEADS * CH), F32)],
        scratch_shapes=[pltpu.VMEM((HG_GROUP, HG_D, HG_D), F32), pltpu.VMEM((HG_GROUP, nc, SLAB_ROWS, HG_D), MM),
                        pltpu.VMEM((HG_GROUP, nc, HG_D, HG_D), F32)],
        compiler_params=_params("parallel", "arbitrary"),
    )(proj, proj, proj, lb_logits)


def _hgrn_bwd(proj, lb_logits, d_o, sst, a_in, rb=256):
    s = proj.shape[1]
    nb, nc = s // rb, rb // CH

    def body(q_ref, f_ref, i_ref, lbl_ref, do_ref, sst_ref, a_ref, dxq_ref, dxf_ref, dxi_ref, dlb_ref,
             dst_ref, states_ref, dstates_ref, lslab_ref, kslab_ref):
        @pl.when(pl.program_id(1) == 0)
        def _():
            dst_ref[...] = jnp.zeros_like(dst_ref)
            dlb_ref[...] = jnp.zeros_like(dlb_ref)

        prefix, suffix = _chunk_masks(rb)
        c3 = lambda a: a.reshape(nc, CH, HG_D)
        flat = lambda a: a.reshape(rb, HG_D)
        row, col = _iota2((nc, CH, CH), 1), _iota2((nc, CH, CH), 2)
        tril, triu = row >= col, row <= col
        sel = (_iota2((CH, CH * CH), 1) % CH == _iota2((CH, CH * CH), 0)).astype(MM)
        blockdiag = _iota2((nc, CH, CH * CH), 2) // CH == _iota2((nc, CH, CH * CH), 1)
        tile = lambda m: jnp.where(blockdiag, _dot(m.reshape(rb, CH), sel).reshape(nc, CH, CH * CH), 0.0)
        last = _iota2((nc, CH, HG_D), 1) == CH - 1
        heads = []
        for g in range(HG_GROUP):
            hs = slice(g * HG_D, (g + 1) * HG_D)
            lb, _ = _lower_bound(lbl_ref[:, hs])
            xq = q_ref[:, hs]
            sgq, q, sg, sn, f, kk = _hgrn_gates(xq, f_ref[:, hs], lb)
            b3 = c3(_tri_dot(prefix, jnp.log(f)))
            q3, kk3, v3, do3 = c3(q), c3(kk), c3(i_ref[:, hs]), c3(do_ref[:, hs])
            bl3 = b3[:, CH - 1:CH, :]
            eb3, ebl3, dec3 = jnp.exp(b3), jnp.exp(bl3), jnp.exp(bl3 - b3)
            qe3, kd3 = q3 * eb3, kk3 * dec3
            x_upd, y_upd = _bdot(v3, kd3, 1, 1), _bdot(do3, qe3, 1, 1)
            zero, every, early, late = jnp.zeros((nc, HALF, HG_D), F32), slice(0, CH), slice(0, HALF), slice(HALF, CH)
            for t in range(CH):
                bt = b3[:, t:t + 1, :]
                since = lambda rows: q3[:, rows, :] * jnp.exp(jnp.minimum(b3[:, rows, :] - bt, 0.0))
                until = lambda rows: kk3[:, rows, :] * jnp.exp(jnp.minimum(bt - b3[:, rows, :], 0.0))
                if t < HALF:
                    lv, kv = since(every), jnp.concatenate([until(early), zero], axis=1)
                else:
                    lv, kv = jnp.concatenate([zero, since(late)], axis=1), until(every)
                lslab_ref[g, :, t * CH:(t + 1) * CH, :] = lv.astype(MM)
                kslab_ref[g, :, t * CH:(t + 1) * CH, :] = kv.astype(MM)
            d_a = jnp.where(tril, _bdot(do3, v3, 2, 2), 0.0)
            d_at = jnp.where(triu, _bdot(v3, do3, 2, 2), 0.0)
            heads.append(dict(hs=hs, lb=lb, xq=xq, sgq=sgq, sg=sg, sn=sn, f=f, q3=q3, kk3=kk3, v3=v3, do3=do3,
                              eb3=eb3, ebl3=ebl3, dec3=dec3, qe3=qe3, kd3=kd3, x_upd=x_upd, y_upd=y_upd,
                              d_a=d_a, d_at=d_at))
        for g, hd in enumerate(heads):
            st = sst_ref[g]
            for c in range(nc):
                states_ref[g, c] = st
                st = st * hd["ebl3"][c] + hd["x_upd"][c]
            dst = dst_ref[g]
            for c in reversed(range(nc)):
                dstates_ref[g, c] = dst
                dst = dst * hd["ebl3"][c] + hd["y_upd"][c]
            dst_ref[g] = dst
        for g, hd in enumerate(heads):
            q3, v3, do3, kd3 = hd["q3"], hd["v3"], hd["do3"], hd["kd3"]
            states, dstates = states_ref[g], dstates_ref[g]
            hd["dqe"] = _bdot(do3, states, 2, 1)
            hd["dkd"] = _bdot(v3, dstates, 2, 1)
            a = a_ref[:, g * CH:(g + 1) * CH].reshape(nc, CH, CH)
            hd["dv"] = _bdot(kd3, dstates, 2, 2) + _bdot(a, do3, 1, 1)
            hd["dq_in"] = _bdot(tile(hd["d_a"]), kslab_ref[g], 2, 1)
            hd["dk_in"] = _bdot(tile(hd["d_at"]), lslab_ref[g], 2, 1)
            hd["ss"] = jnp.sum(dstates * states, axis=1, keepdims=True)
        for g, hd in enumerate(heads):
            q3, kk3, eb3, ebl3, dec3, qe3, kd3 = (hd[k] for k in ("q3", "kk3", "eb3", "ebl3", "dec3", "qe3", "kd3"))
            dqe, dkd, dq_in, dk_in = hd["dqe"], hd["dkd"], hd["dq_in"], hd["dk_in"]
            dkd_kd = dkd * kd3
            db = dqe * qe3 - dkd_kd + q3 * dq_in - kk3 * dk_in
            dbl = jnp.sum(dkd_kd, axis=1, keepdims=True) + hd["ss"] * ebl3
            dg = _tri_dot(suffix, flat(db + jnp.where(last, dbl, 0.0)))
            df = dg / hd["f"] - flat(dkd * dec3 + dk_in)
            xq, sgq, hs = hd["xq"], hd["sgq"], hd["hs"]
            dxq_ref[:, hs] = (flat(dqe * eb3 + dq_in) * (sgq * (1.0 + xq * (1.0 - sgq)))).astype(MM)
            dxf_ref[:, hs] = (df * (1.0 - hd["lb"]) * hd["sg"] * hd["sn"]).astype(MM)
            dxi_ref[:, hs] = flat(hd["dv"]).astype(MM)
            dlb_ref[:, hs] += jnp.sum(df * hd["sn"], axis=0, keepdims=True)

    wide = HG_GROUP * HG_D
    rev = lambda i: nb - 1 - i
    sec = lambda j: pl.BlockSpec((None, rb, wide), lambda h, i, j=j: (j, rev(i), h))
    blk = pl.BlockSpec((rb, wide), lambda h, i: (rev(i), h))
    state = (pltpu.VMEM((HG_GROUP, nc, HG_D, HG_D), F32), pltpu.VMEM((HG_GROUP, nc, CH * CH, HG_D), MM))
    return pl.pallas_call(
        body, name="hgrn_bwd", grid=(HG_HEADS // HG_GROUP, nb),
        in_specs=[sec(0), sec(1), sec(2), pl.BlockSpec((2, wide), lambda h, i: (0, h)), blk,
                  pl.BlockSpec((None, HG_GROUP, HG_D, HG_D), lambda h, i: (rev(i), h, 0, 0)),
                  pl.BlockSpec((rb, HG_GROUP * CH), lambda h, i: (rev(i), h))],
        out_specs=[blk, blk, blk, pl.BlockSpec((1, wide), lambda h, i: (0, h))],
        out_shape=[jax.ShapeDtypeStruct((s, SEC_W), MM)] * 3 + [jax.ShapeDtypeStruct((1, SEC_W), F32)],
        scratch_shapes=[pltpu.VMEM((HG_GROUP, HG_D, HG_D), F32), state[0], state[0], state[1], state[1]],
        compiler_params=_params("parallel", "arbitrary"),
    )(proj, proj, proj, lb_logits, d_o, sst, a_in)


def _rope_tables(s):
    half = AT_DH // 2
    inv_freq = np.float32(1.0) / (np.float32(ROPE_THETA) ** (np.arange(half, dtype=np.float32) / np.float32(half)))
    ang = np.arange(s, dtype=np.float32)[:, None] * inv_freq[None, :]
    cos, sin = np.cos(ang), np.sin(ang)
    return np.concatenate([cos] * 4, axis=-1), np.concatenate([-sin, sin] * 2, axis=-1)


def _rope128(x, cos, sin):
    lo = (_iota2(x.shape, 1) % AT_DH) < AT_DH // 2
    rot = jnp.where(lo, pltpu.roll(x, LANES - AT_DH // 2, 1), pltpu.roll(x, AT_DH // 2, 1))
    return x * cos + rot * sin


LANE_GROUPS = SEC_W // LANES


def _set_lanes(ref, val):
    for j in range(LANE_GROUPS):
        ref[j] = val[:, j * LANES:(j + 1) * LANES]


def _get_lanes(ref):
    return jnp.concatenate([ref[j] for j in range(LANE_GROUPS)], axis=-1)


def _to_view(src_ref, dst_ref, d):
    n = src_ref.shape[1] // d
    for r in range(d):
        rows = pl.ds(r, n, stride=d) if d > 1 else slice(None)
        for j in range(LANE_GROUPS):
            c0 = r * SEC_W + j * LANES
            dst_ref[:, c0:c0 + LANES] = src_ref.at[j][rows, :].astype(dst_ref.dtype)


def _from_view(src_ref, dst_ref, d):
    n = dst_ref.shape[1] // d
    for r in range(d):
        for j in range(LANE_GROUPS):
            c0 = r * SEC_W + j * LANES
            dst_ref.at[j][pl.ds(r, n, stride=d), :] = src_ref[:, c0:c0 + LANES].astype(dst_ref.dtype)


def _view_spec(tm, d):
    return pl.BlockSpec((tm // d, d * SEC_W), lambda i: (i, 0))


def _view_shape(s, d, dtype):
    return jax.ShapeDtypeStruct((s // d, d * SEC_W), dtype)


PROJ_KEPT = (0, 1, 2, 3, 7)


def _inproj_fwd(x, norm_w, w_all, cos, sin, tm=512):
    s = x.shape[0]

    def body(x_ref, nw_ref, w_ref, cos_ref, sin_ref, proj_ref, *refs):
        outs, (qs_ref, ks_ref, vs_ref) = refs[:-3], refs[-3:]
        xv = x_ref[...]
        rstd = lax.rsqrt(jnp.mean(xv * xv, axis=-1, keepdims=True) + NORM_EPS)
        u = (xv * rstd * nw_ref[...]).astype(MM)
        for slot, j in enumerate(PROJ_KEPT):
            proj_ref[slot] = jnp.dot(u, w_ref[j], preferred_element_type=F32)
        q, k, v = [jnp.dot(u, w_ref[j], preferred_element_type=F32) for j in (4, 5, 6)]
        c, sn = cos_ref[...], sin_ref[...]
        for g in range(LANE_GROUPS):
            sl = slice(g * LANES, (g + 1) * LANES)
            qs_ref[g] = _rope128(q[:, sl], c, sn) * (AT_DH ** -0.5)
            ks_ref[g] = _rope128(k[:, sl], c, sn)
            vs_ref[g] = v[:, sl]
        for i, d in enumerate(DILATIONS):
            for src_ref, dst_ref in zip((qs_ref, ks_ref, vs_ref), outs[3 * i:3 * i + 3]):
                _to_view(src_ref, dst_ref, d)

    tab = pl.BlockSpec((tm, LANES), lambda i: (i, 0))
    return pl.pallas_call(
        body, name="inproj_fwd", grid=(s // tm,),
        in_specs=[pl.BlockSpec((tm, D_MODEL), lambda i: (i, 0)),
                  pl.BlockSpec((1, D_MODEL), lambda i: (0, 0)),
                  pl.BlockSpec((N_SEC, D_MODEL, SEC_W), lambda i: (0, 0, 0)), tab, tab],
        out_specs=[pl.BlockSpec((len(PROJ_KEPT), tm, SEC_W), lambda i: (0, i, 0))]
                  + [_view_spec(tm, d) for d in DILATIONS for _ in range(3)],
        out_shape=[jax.ShapeDtypeStruct((len(PROJ_KEPT), s, SEC_W), F32)]
                  + [_view_shape(s, d, MM) for d in DILATIONS for _ in range(3)],
        scratch_shapes=[pltpu.VMEM((LANE_GROUPS, tm, LANES), F32)] * 3,
        compiler_params=_params("parallel"),
    )(x, norm_w, w_all, cos, sin)


def _band_mask(first_ok, second_ok):
    row, col = _iota2((ATT_BLK, 2 * ATT_BLK), 0), _iota2((ATT_BLK, 2 * ATT_BLK), 1)
    return ((col < ATT_BLK) & (col >= row) & first_ok) | ((col >= ATT_BLK) & ((col - ATT_BLK) <= row) & second_ok)


def _own_lanes(rows, h):
    lane = _iota2((rows, LANES), 1)
    return (lane < AT_DH) if h == 0 else (lane >= AT_DH)


def _neg_pieces(rows, h):
    lane = _iota2((rows, LANES), 1) - (AT_DH if h == 0 else 0)
    return jnp.where((lane >= 0) & (lane < 3), -1.0, 0.0).astype(MM)


def _units(qb):
    return [(b, slice(g * LANES, (g + 1) * LANES), h) for b in range(qb) for g in range(AT_COLS // LANES) for h in range(2)]


def _sub(b):
    return slice(b * ATT_BLK, (b + 1) * ATT_BLK)


def _band_before(cur_ref, prev_ref, b, sl):
    if b == 0:
        return jnp.concatenate([prev_ref[:, sl], cur_ref[0:ATT_BLK, sl]], axis=0)
    return cur_ref[(b - 1) * ATT_BLK:(b + 1) * ATT_BLK, sl]


def _band_after(cur_ref, next_ref, b, sl):
    if (b + 1) * ATT_BLK == cur_ref.shape[0]:
        return jnp.concatenate([cur_ref[b * ATT_BLK:(b + 1) * ATT_BLK, sl], next_ref[:, sl]], axis=0)
    return cur_ref[b * ATT_BLK:(b + 2) * ATT_BLK, sl]


def _attn_specs(rows):
    qb = min(AT_QB, rows // ATT_BLK)
    assert rows % (qb * ATT_BLK) == 0
    last = rows // ATT_BLK - 1
    cur = pl.BlockSpec((qb * ATT_BLK, AT_COLS), lambda c, n: (n, c))
    prev = pl.BlockSpec((ATT_BLK, AT_COLS), lambda c, n: (jnp.maximum(qb * n - 1, 0), c))
    nxt = pl.BlockSpec((ATT_BLK, AT_COLS), lambda c, n: (jnp.minimum(qb * (n + 1), last), c))
    return qb, cur, prev, nxt


def _stack_heads(a):
    h0 = _own_lanes(a.shape[0], 0)
    zero = jnp.zeros_like(a)
    return jnp.concatenate([jnp.where(h0, a, zero), jnp.where(h0, zero, a)], axis=0)


def _unstack_heads(a2):
    return jnp.where(_own_lanes(ATT_BLK, 0), a2[:ATT_BLK], a2[ATT_BLK:])


def _attn_fwd(qr, kr, vr, d):
    rows, cols = qr.shape
    qb, cur, prev, nxt = _attn_specs(rows)
    nb = rows // (qb * ATT_BLK)

    def body(q_ref, kc_ref, kp_ref, vc_ref, vp_ref, o_ref, lse_ref):
        twice = lambda m: jnp.concatenate([m, m], axis=0)
        masks = {True: twice(_band_mask(pl.program_id(1) > 0, True)), False: twice(_band_mask(True, True))}
        ones = jnp.ones((2 * ATT_BLK, LANES), MM)
        units = [(b, sl) for b, sl, h in _units(qb) if h == 0]
        scs = [jnp.where(masks[b == 0], _dot_nt(_stack_heads(q_ref[_sub(b), sl]), _band_before(kc_ref, kp_ref, b, sl)),
                         NEG) for b, sl in units]
        ms = [jnp.max(sc, axis=-1, keepdims=True) for sc in scs]
        ps = [jnp.exp(sc - m).astype(MM) for sc, m in zip(scs, ms)]
        ols = [jnp.dot(p, jnp.concatenate([_band_before(vc_ref, vp_ref, b, sl), ones], axis=1),
                       preferred_element_type=F32) for p, (b, sl) in zip(ps, units)]
        for (b, sl), m, ol in zip(units, ms, ols):
            l = _unstack_heads(ol[:, LANES:])
            o_ref[_sub(b), sl] = _unstack_heads(ol[:, :LANES]) / l
            lse_ref[_sub(b), sl] = _unstack_heads(jnp.broadcast_to(m, (2 * ATT_BLK, LANES))) + jnp.log(l)

    o, lse = pl.pallas_call(
        body, name=f"attn_fwd_d{d}", grid=(cols // AT_COLS, nb),
        in_specs=[cur, cur, prev, cur, prev], out_specs=[cur, cur],
        out_shape=[jax.ShapeDtypeStruct((rows, cols), F32)] * 2,
        compiler_params=_params("parallel", "parallel"),
    )(qr, kr, kr, vr, vr)
    return o, lse


def _attn_bwd_dq(qr, kr, vr, do, lse, delta, d):
    rows, cols = qr.shape
    qb, cur, prev, nxt = _attn_specs(rows)
    nb = rows // (qb * ATT_BLK)

    def body(q_ref, kc_ref, kp_ref, vc_ref, vp_ref, do_ref, lse_ref, dl_ref, dq_ref):
        masks = {True: _band_mask(pl.program_id(1) > 0, True), False: _band_mask(True, True)}
        units = _units(qb)
        sms, dps = [], []
        for b, sl, h in units:
            own, own_b, neg = _own_lanes(ATT_BLK, h), _own_lanes(2 * ATT_BLK, h), _neg_pieces(2 * ATT_BLK, h)
            sms.append(_dot_nt(jnp.where(own, q_ref[_sub(b), sl], lse_ref[_sub(b), sl]),
                               jnp.where(own_b, _band_before(kc_ref, kp_ref, b, sl), neg)))
            dps.append(_dot_nt(jnp.where(own, do_ref[_sub(b), sl], dl_ref[_sub(b), sl]),
                               jnp.where(own_b, _band_before(vc_ref, vp_ref, b, sl), neg)))
        dss = [(jnp.exp(jnp.where(masks[b == 0], sm, NEG)) * dp).astype(MM)
               for sm, dp, (b, _, _) in zip(sms, dps, units)]
        dqs = [jnp.dot(ds, _band_before(kc_ref, kp_ref, b, sl), preferred_element_type=F32) * (AT_DH ** -0.5)
               for ds, (b, sl, _) in zip(dss, units)]
        for i in range(0, len(units), 2):
            b, sl, _ = units[i]
            dq_ref[_sub(b), sl] = jnp.where(_own_lanes(ATT_BLK, 0), dqs[i], dqs[i + 1]).astype(dq_ref.dtype)

    dq = pl.pallas_call(
        body, name=f"attn_bwd_dq_d{d}", grid=(cols // AT_COLS, nb),
        in_specs=[cur, cur, prev, cur, prev, cur, cur, cur], out_specs=cur,
        out_shape=jax.ShapeDtypeStruct((rows, cols), MM),
        compiler_params=_params("parallel", "parallel"),
    )(qr, kr, kr, vr, vr, do, lse, delta)
    return dq


def _attn_bwd_dkv(qr, kr, vr, do, lse, delta, d):
    rows, cols = qr.shape
    qb, cur, prev, nxt = _attn_specs(rows)
    nb = rows // (qb * ATT_BLK)

    def body(k_ref, v_ref, qc_ref, qn_ref, doc_ref, don_ref, lsec_ref, lsen_ref, dlc_ref, dln_ref,
             dk_ref, dv_ref):
        masks = {True: _band_mask(True, pl.program_id(1) < nb - 1), False: _band_mask(True, True)}
        units = _units(qb)
        sms, dps = [], []
        for b, sl, h in units:
            own, own_b, neg = _own_lanes(ATT_BLK, h), _own_lanes(2 * ATT_BLK, h), _neg_pieces(ATT_BLK, h)
            sms.append(_dot_nt(jnp.where(own, k_ref[_sub(b), sl], neg),
                               jnp.where(own_b, _band_after(qc_ref, qn_ref, b, sl),
                                         _band_after(lsec_ref, lsen_ref, b, sl))))
            dps.append(_dot_nt(jnp.where(own, v_ref[_sub(b), sl], neg),
                               jnp.where(own_b, _band_after(doc_ref, don_ref, b, sl),
                                         _band_after(dlc_ref, dln_ref, b, sl))))
        ps = [jnp.exp(jnp.where(masks[b == qb - 1], sm, NEG)) for sm, (b, _, _) in zip(sms, units)]
        dss = [(p * dp).astype(MM) for p, dp in zip(ps, dps)]
        dvs = [jnp.dot(p.astype(MM), _band_after(doc_ref, don_ref, b, sl), preferred_element_type=F32)
               for p, (b, sl, _) in zip(ps, units)]
        dks = [jnp.dot(ds, _band_after(qc_ref, qn_ref, b, sl), preferred_element_type=F32)
               for ds, (b, sl, _) in zip(dss, units)]
        head0 = _own_lanes(ATT_BLK, 0)
        for i in range(0, len(units), 2):
            b, sl, _ = units[i]
            dk_ref[_sub(b), sl] = jnp.where(head0, dks[i], dks[i + 1]).astype(dk_ref.dtype)
            dv_ref[_sub(b), sl] = jnp.where(head0, dvs[i], dvs[i + 1]).astype(dv_ref.dtype)

    dk, dv = pl.pallas_call(
        body, name=f"attn_bwd_dkv_d{d}", grid=(cols // AT_COLS, nb),
        in_specs=[cur, cur, cur, nxt, cur, nxt, cur, nxt, cur, nxt], out_specs=[cur, cur],
        out_shape=[jax.ShapeDtypeStruct((rows, cols), MM)] * 2,
        compiler_params=_params("parallel", "parallel"),
    )(kr, vr, qr, qr, do, do, lse, lse, delta, delta)
    return dk, dv


def _head_sum(a, width):
    parts = []
    for j in range(a.shape[1] // width):
        sm = jnp.sum(a[:, j * width:(j + 1) * width], axis=-1, keepdims=True)
        parts.append(jnp.broadcast_to(sm, (a.shape[0], width)))
    return jnp.concatenate(parts, axis=-1)


def _partner_sum(a):
    swap = (_iota2((LANES, LANES), 0) // AT_DH != _iota2((LANES, LANES), 1) // AT_DH).astype(jnp.bfloat16)
    d = functools.partial(jnp.dot, preferred_element_type=F32)
    parts = _split3(a)
    return jnp.concatenate([d(parts[0][:, sl], swap) + d(parts[1][:, sl], swap) + d(parts[2][:, sl], swap)
                            for sl in (slice(j * LANES, (j + 1) * LANES) for j in range(a.shape[1] // LANES))], axis=-1)


def _partner_value(x):
    return jnp.concatenate([pltpu.roll(x[:, j * LANES:(j + 1) * LANES], AT_DH, 1) for j in range(x.shape[1] // LANES)],
                           axis=-1)


def _pieces(xs):
    hi = xs.astype(jnp.bfloat16).astype(F32)
    mid = (xs - hi).astype(jnp.bfloat16).astype(F32)
    lo = (xs - hi - mid).astype(jnp.bfloat16).astype(F32)
    lane = _iota2(xs.shape, 1) % AT_DH
    return jnp.where(lane == 0, hi, jnp.where(lane == 1, mid, jnp.where(lane == 2, lo, 0.0)))


def _mid(x, tgt, proj, o_hg, o_at, lse_at, hg_norm_w, final_norm_w, wo_all, tm=256):
    s = x.shape[0]
    nb = s // tm

    def body(x_ref, t_ref, hgz_ref, atz_ref, ohg_ref, o1_ref, o2_ref, o3_ref, l1_ref, l2_ref, l3_ref,
             g_ref, fw_ref, wo_ref,
             dh_ref, dohg_ref, dhgz_ref, datz_ref, do1_ref, do2_ref, do3_ref, dl1_ref, dl2_ref, dl3_ref,
             lp1_ref, lp2_ref, lp3_ref,
             gwo_ref, gfw_ref, ghg_ref, loss_ref, nat_ref, stage_ref, gwo_acc):
        @pl.when(pl.program_id(0) == 0)
        def _():
            gwo_acc[...] = jnp.zeros_like(gwo_acc)
            gfw_ref[...] = jnp.zeros_like(gfw_ref)
            ghg_ref[...] = jnp.zeros_like(ghg_ref)
            loss_ref[...] = jnp.zeros_like(loss_ref)

        ohg, g = ohg_ref[...], g_ref[...]
        rs = lax.rsqrt(_head_sum(ohg * ohg, HG_D) * (1.0 / HG_D) + NORM_EPS)
        on = ohg * rs
        hgz = hgz_ref[...]
        sz = _sigmoid(hgz)
        gate_hg = hgz * sz
        lses, outs = [l1_ref[...]], [o1_ref[...]]
        for k, (d, l_ref, o_ref) in enumerate(zip(DILATIONS[1:], (l2_ref, l3_ref), (o2_ref, o3_ref))):
            _from_view(l_ref, nat_ref.at[2 * k], d)
            _from_view(o_ref, nat_ref.at[2 * k + 1], d)
            lses.append(_get_lanes(nat_ref.at[2 * k]))
            outs.append(_get_lanes(nat_ref.at[2 * k + 1]))
        mx = jnp.maximum(jnp.maximum(lses[0], lses[1]), lses[2])
        es = [jnp.exp(l - mx) for l in lses]
        den = es[0] + es[1] + es[2]
        ws = [e / den for e in es]
        oat = ws[0] * outs[0] + ws[1] * outs[1] + ws[2] * outs[2]
        atz = atz_ref[...]
        sa = _sigmoid(atz)
        gate_at = atz * sa
        mixed = jnp.concatenate([on * g * gate_hg, oat * gate_at], axis=-1).astype(MM)
        h = x_ref[...] + jnp.dot(mixed, wo_ref[...], preferred_element_type=F32)
        rstd = lax.rsqrt(jnp.mean(h * h, axis=-1, keepdims=True) + NORM_EPS)
        hn = h * rstd
        fw = fw_ref[...]
        err = hn * fw - t_ref[...]
        loss_ref[...] += 0.5 * jnp.sum(jnp.mean(err * err, axis=-1, keepdims=True), axis=0, keepdims=True)
        dout = err * (1.0 / D_MODEL)
        gfw_ref[...] += jnp.sum(dout * hn, axis=0, keepdims=True)
        dhn = dout * fw
        dh = rstd * (dhn - hn * jnp.mean(dhn * hn, axis=-1, keepdims=True))
        dh_ref[...] = dh
        dh_mm = dh.astype(MM)
        gwo_acc[...] += _dot_tn(mixed, dh_mm)

        @pl.when(pl.program_id(0) == nb - 1)
        def _():
            gwo_ref[...] = gwo_acc[...].astype(gwo_ref.dtype)

        dmixed = _dot_nt(dh_mm, wo_ref[...])
        dm_hg = dmixed[:, :SEC_W]
        d_ong = dm_hg * gate_hg
        dhgz_ref[...] = (dm_hg * (on * g) * (sz * (1.0 + hgz * (1.0 - sz)))).astype(MM)
        ghg_ref[...] += jnp.sum(d_ong * on, axis=0, keepdims=True)
        d_on = d_ong * g
        dohg_ref[...] = rs * (d_on - on * (_head_sum(d_on * on, HG_D) * (1.0 / HG_D)))
        dm_at = dmixed[:, SEC_W:]
        d_oat = dm_at * gate_at
        datz_ref[...] = (dm_at * oat * (sa * (1.0 + atz * (1.0 - sa)))).astype(MM)
        lse_all = mx + jnp.log(den)
        for val, dst_refs in ((d_oat, (do1_ref, do2_ref, do3_ref)),
                              (_pieces(_partner_sum(d_oat * oat)), (dl1_ref, dl2_ref, dl3_ref)),
                              (_pieces(_partner_value(lse_all)), (lp1_ref, lp2_ref, lp3_ref))):
            _set_lanes(stage_ref, val)
            for d, dst_ref in zip(DILATIONS, dst_refs):
                _to_view(stage_ref, dst_ref, d)

    row = lambda w: pl.BlockSpec((tm, w), lambda i: (i, 0))
    sec = lambda j: pl.BlockSpec((None, tm, SEC_W), lambda i, j=j: (j, i, 0))
    const = lambda shp: pl.BlockSpec(shp, lambda i: (0,) * len(shp))
    half = row(SEC_W)
    views = [_view_spec(tm, d) for d in DILATIONS]
    return pl.pallas_call(
        body, name="mid", grid=(nb,),
        in_specs=[row(D_MODEL), row(D_MODEL), sec(PROJ_KEPT.index(3)), sec(PROJ_KEPT.index(7)), half] + views * 2
                 + [const((1, SEC_W)), const((1, D_MODEL)), const((D_MODEL, D_MODEL))],
        out_specs=[row(D_MODEL)] + [half] * 3 + views * 3
                  + [const((D_MODEL, D_MODEL)), const((1, D_MODEL)), const((1, SEC_W)), const((1, 1))],
        out_shape=[jax.ShapeDtypeStruct((s, D_MODEL), F32), jax.ShapeDtypeStruct((s, SEC_W), F32)]
                  + [jax.ShapeDtypeStruct((s, SEC_W), MM)] * 2
                  + [_view_shape(s, d, MM) for d in DILATIONS] * 3
                  + [jax.ShapeDtypeStruct((D_MODEL, D_MODEL), XCH), jax.ShapeDtypeStruct((1, D_MODEL), F32),
                     jax.ShapeDtypeStruct((1, SEC_W), F32), jax.ShapeDtypeStruct((1, 1), F32)],
        scratch_shapes=[pltpu.VMEM((4, LANE_GROUPS, tm, LANES), F32), pltpu.VMEM((LANE_GROUPS, tm, LANES), F32),
                        pltpu.VMEM((D_MODEL, D_MODEL), F32)],
        compiler_params=_params("arbitrary"),
    )(x, tgt, proj, proj, o_hg, *o_at, *lse_at, hg_norm_w, final_norm_w, wo_all)


def _section_specs(dsecs, tm):
    return [pl.BlockSpec((tm, SEC_W), lambda i: (i, 0)) if k is None
            else pl.BlockSpec((None, tm, SEC_W), lambda i, k=k: (k, i, 0)) for _, k in dsecs]


def _inproj_bwd_x(x, norm_w, w_all, dh, dsecs, token, tm=512):
    s = x.shape[0]

    def body(x_ref, nw_ref, w_ref, dh_ref, tok_ref, *refs):
        sec_refs, (gx_ref, gnw_ref) = refs[:N_SEC], refs[N_SEC:]

        @pl.when(pl.program_id(0) == 0)
        def _():
            gnw_ref[...] = jnp.zeros_like(gnw_ref)

        du = jnp.zeros((tm, D_MODEL), F32)
        for j in range(N_SEC):
            du = du + _dot_nt(sec_refs[j][...], w_ref[j])
        xv, nw = x_ref[...], nw_ref[...]
        rstd = lax.rsqrt(jnp.mean(xv * xv, axis=-1, keepdims=True) + NORM_EPS)
        xn = xv * rstd
        gnw_ref[...] += jnp.sum(du * xn, axis=0, keepdims=True)
        dxn = du * nw
        dx = rstd * (dxn - xn * jnp.mean(dxn * xn, axis=-1, keepdims=True))
        gx_ref[...] = (dh_ref[...] + tok_ref[0:1, 0:1]) + dx

    row = lambda w: pl.BlockSpec((tm, w), lambda i: (i, 0))
    const = lambda shp: pl.BlockSpec(shp, lambda i: (0,) * len(shp))
    return pl.pallas_call(
        body, name="inproj_bwd_x", grid=(s // tm,),
        in_specs=[row(D_MODEL), const((1, D_MODEL)), const((N_SEC, D_MODEL, SEC_W)), row(D_MODEL), const((8, 128))]
                 + _section_specs(dsecs, tm),
        out_specs=[row(D_MODEL), const((1, D_MODEL))],
        out_shape=[jax.ShapeDtypeStruct((s, D_MODEL), F32), jax.ShapeDtypeStruct((1, D_MODEL), F32)],
        compiler_params=_params("arbitrary"),
    )(x, norm_w, w_all, dh, token, *[a for a, _ in dsecs])


def _inproj_bwd_w(x, norm_w, dsec, dq_r, dk_r, dv, cos, sin, tm=512):
    s = x.shape[0]
    nb = s // tm

    def body(x_ref, nw_ref, s0, s1, s2, s3, s7, q1, q2, q3, k1, k2, k3, v1, v2, v3, cos_ref, sin_ref,
             gw_hbm, datt_ref, acc_ref, stage_ref, nat_ref):
        @pl.when(pl.program_id(0) == 0)
        def _():
            acc_ref[...] = jnp.zeros_like(acc_ref)

        def total(refs):
            acc = refs[0][...].astype(F32)
            for d, ref in zip(DILATIONS[1:], refs[1:]):
                _from_view(ref, nat_ref, d)
                acc = acc + _get_lanes(nat_ref)
            return acc

        c, sn = cos_ref[...], -sin_ref[...]
        unrot = lambda a: jnp.concatenate(
            [_rope128(a[:, j * LANES:(j + 1) * LANES], c, sn) for j in range(LANE_GROUPS)], axis=-1)
        att = [a.astype(MM) for a in (unrot(total((q1, q2, q3))), unrot(total((k1, k2, k3))), total((v1, v2, v3)))]
        for j, a in enumerate(att):
            datt_ref[j] = a
        xv = x_ref[...]
        rstd = lax.rsqrt(jnp.mean(xv * xv, axis=-1, keepdims=True) + NORM_EPS)
        u_t = (xv * rstd * nw_ref[...]).T.astype(MM)
        for j, dsj in enumerate((s0[...], s1[...], s2[...], s3[...], *att, s7[...])):
            acc_ref[j] += jnp.dot(u_t, dsj, preferred_element_type=F32)

        @pl.when(pl.program_id(0) == nb - 1)
        def _():
            for j in range(N_SEC):
                stage_ref[...] = acc_ref[j].astype(stage_ref.dtype)
                pltpu.sync_copy(stage_ref, gw_hbm.at[j])

    row = lambda w: pl.BlockSpec((tm, w), lambda i: (i, 0))
    return pl.pallas_call(
        body, name="inproj_bwd_w", grid=(nb,),
        in_specs=[row(D_MODEL), pl.BlockSpec((1, D_MODEL), lambda i: (0, 0))] + [row(SEC_W)] * 5
                 + [_view_spec(tm, d) for d in DILATIONS] * 3 + [row(LANES), row(LANES)],
        out_specs=[pl.BlockSpec(memory_space=pl.ANY), pl.BlockSpec((3, tm, SEC_W), lambda i: (0, i, 0))],
        out_shape=[jax.ShapeDtypeStruct((N_SEC, D_MODEL, SEC_W), XCH), jax.ShapeDtypeStruct((3, s, SEC_W), MM)],
        scratch_shapes=[pltpu.VMEM((N_SEC, D_MODEL, SEC_W), F32), pltpu.VMEM((D_MODEL, SEC_W), XCH),
                        pltpu.VMEM((LANE_GROUPS, tm, LANES), F32)],
        compiler_params=_params("arbitrary"),
    )(x, norm_w, *dsec, *dq_r, *dk_r, *dv, cos, sin)


def _local_step(x, tgt, norm_w, w_all, lb_logits, hg_norm_w, wo_all, final_norm_w, on_weight_grads):
    s = x.shape[0]
    cos, sin = _rope_tables(s)
    proj, *qkv = _inproj_fwd(x, norm_w, w_all, cos, sin)
    o_hg, sst, a_hg = _hgrn_fwd(proj, lb_logits)
    qkv = [qkv[3 * i:3 * i + 3] for i in range(len(DILATIONS))]
    att = [_attn_fwd(*qkv_d, d) for qkv_d, d in zip(qkv, DILATIONS)]
    (dh, d_ohg, d_hgz, d_atz, do1, do2, do3, dl1, dl2, dl3, lp1, lp2, lp3, gwo, gfw, ghg, loss) = _mid(
        x, tgt, proj, o_hg, [a[0] for a in att], [a[1] for a in att], hg_norm_w, final_norm_w[None, :], wo_all)
    dxq, dxf, dxi, dlb = _hgrn_bwd(proj, lb_logits, d_ohg, sst, a_hg)
    dq_r, dk_r, dv = [], [], []
    for d, qkv_d, do, lp, dl in zip(DILATIONS, qkv, (do1, do2, do3), (lp1, lp2, lp3), (dl1, dl2, dl3)):
        dq_r.append(_attn_bwd_dq(*qkv_d, do, lp, dl, d))
        dk_d, dv_d = _attn_bwd_dkv(*qkv_d, do, lp, dl, d)
        dk_r.append(dk_d)
        dv.append(dv_d)
    gwi, d_att = _inproj_bwd_w(x, norm_w, (dxq, dxf, dxi, d_hgz, d_atz), dq_r, dk_r, dv, cos, sin)
    dsecs = [(dxq, None), (dxf, None), (dxi, None), (d_hgz, None), (d_att, 0), (d_att, 1), (d_att, 2), (d_atz, None)]
    token = on_weight_grads(gwi, gwo)
    gx, gnw = _inproj_bwd_x(x, norm_w, w_all, dh, dsecs, token)
    small = jnp.concatenate([gnw, jnp.concatenate([dlb, ghg], axis=-1), gfw,
                             jnp.pad(loss, ((0, 0), (0, D_MODEL - 1)))], axis=0)
    return gx, gwi, gwo, small


def _coords():
    return lax.axis_index("x"), lax.axis_index("y"), lax.axis_index("c")


def _gather_weights(w_in, w_out):
    wo_rows = w_out.shape[0]

    def body(wi_ref, wo_ref, wi_all, wo_all, send_sems, recv_sems):
        x, y, c = _coords()
        me, sibling = (x, y, c), (x, y, 1 - c)
        chips = [(1 - x, y), (x, 1 - y), (1 - x, 1 - y)]
        slot = lambda p: 4 * p[0] + 2 * p[1] + p[2]

        def copies(k, block, to):
            return [pltpu.make_async_remote_copy(
                src_ref=ref.at[slot(block)], dst_ref=ref.at[slot(block)], send_sem=send_sems.at[a, k],
                recv_sem=recv_sems.at[a, k], device_id=to, device_id_type=MESH)
                for a, ref in enumerate((wi_all, wo_all))]

        wi_all[slot(me)] = wi_ref[...].astype(MM)
        wo_all[slot(me)] = wo_ref[...].astype(MM)
        first = copies(0, me, sibling)
        for j, chip in enumerate(chips):
            first += copies(1 + j, me, (*chip, c))
        for cp in first:
            cp.start()
        passed = []
        for j, chip in enumerate(chips):
            for cp in copies(1 + j, (*chip, c), me):
                cp.wait_recv()
            fwd = copies(4 + j, (*chip, c), sibling)
            for cp in fwd:
                cp.start()
            passed += fwd
        for cp in copies(0, sibling, me):
            cp.wait_recv()
        for j, chip in enumerate(chips):
            for cp in copies(4 + j, (*chip, 1 - c), me):
                cp.wait_recv()
        for cp in first + passed:
            cp.wait_send()

    vmem = pl.BlockSpec(memory_space=pltpu.VMEM)
    return pl.pallas_call(
        body, name="gather_weights",
        in_specs=[vmem, vmem], out_specs=[vmem, vmem],
        out_shape=[jax.ShapeDtypeStruct((N_DEV, D_MODEL, SEC_W), MM),
                   jax.ShapeDtypeStruct((N_DEV, wo_rows, D_MODEL), MM)],
        scratch_shapes=[pltpu.SemaphoreType.DMA((2, 7)), pltpu.SemaphoreType.DMA((2, 7))],
        compiler_params=pltpu.CompilerParams(vmem_limit_bytes=VMEM_LIMIT),
    )(w_in, w_out)


def _me():
    x, y, c = _coords()
    return 4 * x + 2 * y + c


def _grad_copies(srcs, lands, send_sems, recv_sems):
    x, y, c = _coords()
    me = 4 * x + 2 * y + c
    copies = []
    for k in range(1, N_DEV):
        px, py, pc = x ^ (k >> 2), y ^ ((k >> 1) & 1), c ^ (k & 1)
        peer = 4 * px + 2 * py + pc
        for a, (src, dst) in enumerate(zip(srcs, lands)):
            copies.append(pltpu.make_async_remote_copy(
                src_ref=src.at[peer], dst_ref=dst.at[me], send_sem=send_sems.at[a * (N_DEV - 1) + k - 1],
                recv_sem=recv_sems.at[a * (N_DEV - 1) + k - 1], device_id=(px, py, pc), device_id_type=MESH))
    return copies


HBM_SPEC = pl.BlockSpec(memory_space=pltpu.HBM)
SEM_SPEC = pl.BlockSpec(memory_space=pltpu.SEMAPHORE)
SPLIT_COPY_EFFECT = pltpu.SideEffectType.DATAFLOW_SIDE_EFFECTING


def _exchange_start(gwi, gwo):
    def body(gwi_ref, gwo_ref, li_ref, lo_ref, send_sems, recv_sems, gwi_thru, gwo_thru, li_thru, lo_thru, token):
        for cp in _grad_copies((gwi_ref, gwo_ref), (li_ref, lo_ref), send_sems, recv_sems):
            cp.start()
        token[...] = jnp.zeros_like(token)

    hbm = lambda a: pltpu.with_memory_space_constraint(a, pltpu.HBM)
    bufs = (gwi, gwo, lax.empty(gwi.shape, gwi.dtype), lax.empty(gwo.shape, gwo.dtype))
    return pl.pallas_call(
        body, name="exchange_start",
        out_shape=(pltpu.SemaphoreType.DMA((2 * (N_DEV - 1),)), pltpu.SemaphoreType.DMA((2 * (N_DEV - 1),)),
                   *[pltpu.HBM(a.shape, a.dtype) for a in bufs], jax.ShapeDtypeStruct((8, 128), F32)),
        in_specs=[HBM_SPEC] * 4,
        out_specs=(SEM_SPEC, SEM_SPEC, HBM_SPEC, HBM_SPEC, HBM_SPEC, HBM_SPEC, pl.BlockSpec(memory_space=pltpu.VMEM)),
        input_output_aliases={0: 2, 1: 3, 2: 4, 3: 5},
        compiler_params=pltpu.CompilerParams(has_side_effects=SPLIT_COPY_EFFECT),
    )(*[hbm(a) for a in bufs])


def _exchange_wait(send_sems, recv_sems, gwi, gwo, li, lo, after):
    def body(gwi_ref, gwo_ref, li_ref, lo_ref, send_sems, recv_sems, after_ref, gwi_out, gwo_out, li_out, lo_out):
        for cp in _grad_copies((gwi_ref, gwo_ref), (li_ref, lo_ref), send_sems, recv_sems):
            cp.wait_send()
            cp.wait_recv()

    return pl.pallas_call(
        body, name="exchange_wait",
        out_shape=tuple(pltpu.HBM(a.shape, a.dtype) for a in (gwi, gwo, li, lo)),
        in_specs=[HBM_SPEC] * 4 + [SEM_SPEC, SEM_SPEC, pl.BlockSpec(memory_space=pl.ANY)],
        out_specs=(HBM_SPEC,) * 4,
        input_output_aliases={0: 0, 1: 1, 2: 2, 3: 3},
        compiler_params=pltpu.CompilerParams(has_side_effects=SPLIT_COPY_EFFECT),
    )(gwi, gwo, li, lo, send_sems, recv_sems, after)


def _gather_small(small):
    def body(sm_ref, ls_ref, send_sems, recv_sems, local_sem):
        x, y, c = _coords()
        me = 4 * x + 2 * y + c
        own = pltpu.make_async_copy(sm_ref, ls_ref.at[me], local_sem)
        own.start()
        sends = []
        for k in range(1, N_DEV):
            peer = (x ^ (k >> 2), y ^ ((k >> 1) & 1), c ^ (k & 1))
            sends.append(pltpu.make_async_remote_copy(
                src_ref=sm_ref, dst_ref=ls_ref.at[me], send_sem=send_sems.at[k - 1], recv_sem=recv_sems.at[k - 1],
                device_id=peer, device_id_type=MESH))
        for cp in sends:
            cp.start()
        for cp in sends:
            cp.wait_recv()
        for cp in sends:
            cp.wait_send()
        own.wait()

    vmem = pl.BlockSpec(memory_space=pltpu.VMEM)
    return pl.pallas_call(
        body, name="gather_small", in_specs=[vmem], out_specs=vmem,
        out_shape=jax.ShapeDtypeStruct((N_DEV,) + small.shape, F32),
        scratch_shapes=[pltpu.SemaphoreType.DMA((N_DEV - 1,)), pltpu.SemaphoreType.DMA((N_DEV - 1,)),
                        pltpu.SemaphoreType.DMA],
    )(small)


def _adamw(w, g, m, v):
    m = ADAM_B1 * m + (1.0 - ADAM_B1) * g
    v = ADAM_B2 * v + (1.0 - ADAM_B2) * (g * g)
    m_hat = m / (1.0 - ADAM_B1 ** ADAM_STEP)
    v_hat = v / (1.0 - ADAM_B2 ** ADAM_STEP)
    return -ADAM_LR * (m_hat / (jnp.sqrt(v_hat) + ADAM_EPS) + ADAM_WD * w), m, v


def _slot_sum(ref, own=None, me=None):
    g = None
    for i in range(N_DEV):
        term = ref[i].astype(F32)
        if own is not None:
            term = jnp.where(i == me, own, term)
        g = term if g is None else g + term
    return g


def _update_matrix(name, me, landed, own, w, m, v, rows):
    r, c = w.shape

    def body(me_ref, l_ref, own_ref, w_ref, m_ref, v_ref, g_ref, d_ref, nm_ref, nv_ref):
        g = _slot_sum(l_ref, own_ref[...].astype(F32), me_ref[0])
        g_ref[...] = g
        d_ref[...], nm_ref[...], nv_ref[...] = _adamw(w_ref[...], g, m_ref[...], v_ref[...])

    blk = pl.BlockSpec((rows, c), lambda i, me_ref: (i, 0))
    return pl.pallas_call(
        body, name=name,
        grid_spec=pltpu.PrefetchScalarGridSpec(
            num_scalar_prefetch=1, grid=(r // rows,),
            in_specs=[pl.BlockSpec((N_DEV, rows, c), lambda i, me_ref: (0, i, 0)),
                      pl.BlockSpec((None, rows, c), lambda i, me_ref: (me_ref[0], i, 0)), blk, blk, blk],
            out_specs=[blk] * 4),
        out_shape=[jax.ShapeDtypeStruct((r, c), F32)] * 4,
        compiler_params=_params("parallel"),
    )(me, landed, own, w, m, v)


def _update_small(landed, lb_logits, ws, ms, vs):
    def body(l_ref, lbl_ref, w_ref, m_ref, v_ref, g_ref, d_ref, nm_ref, nv_ref, loss_ref):
        tot = _slot_sum(l_ref)
        _, dlb = _lower_bound(lbl_ref[...])
        g_lb = tot[1:2, :SEC_W] * dlb
        g = jnp.concatenate([tot[0:1], jnp.concatenate([g_lb, -g_lb], axis=-1),
                             jnp.pad(tot[1:2, SEC_W:], ((0, 0), (0, SEC_W))), tot[2:3]], axis=0)
        g_ref[...] = g
        d_ref[...], nm_ref[...], nv_ref[...] = _adamw(w_ref[...], g, m_ref[...], v_ref[...])
        loss_ref[...] = tot[3:4, 0:1]

    vmem = pl.BlockSpec(memory_space=pltpu.VMEM)
    return pl.pallas_call(
        body, name="update_small", in_specs=[vmem] * 5, out_specs=[vmem] * 5,
        out_shape=[jax.ShapeDtypeStruct((4, D_MODEL), F32)] * 4 + [jax.ShapeDtypeStruct((1, 1), F32)],
    )(landed, lb_logits, ws, ms, vs)


def _pack_small(norm_w, lb_logits, hg_norm_w, final_norm_w):
    return jnp.concatenate([norm_w, lb_logits.reshape(1, D_MODEL),
                            jnp.pad(hg_norm_w, ((0, 0), (0, D_MODEL - SEC_W))), final_norm_w[None, :]], axis=0)


def _unpack_small(a):
    return a[0:1], a[1].reshape(2, SEC_W), a[2:3, :SEC_W], a[3]


def kernel(x, norm_w, w_in, hgrn_lb_logits, hg_norm_w, w_out, final_norm_w, loss_target, m_norm_w, m_w_in, m_hgrn_lb_logits, m_hg_norm_w, m_w_out, m_final_norm_w, v_norm_w, v_w_in, v_hgrn_lb_logits, v_hg_norm_w, v_w_out, v_final_norm_w):
    w_all, wo_all = _gather_weights(w_in[0], w_out[0])
    in_flight = []

    def start_exchange(gwi, gwo):
        *handles, token = _exchange_start(gwi, gwo.reshape(N_DEV, D_MODEL // N_DEV, D_MODEL))
        in_flight.extend(handles)
        return token

    gx, _, _, small = _local_step(x[0], loss_target[0], norm_w, w_all, hgrn_lb_logits, hg_norm_w,
                                  wo_all.reshape(D_MODEL, D_MODEL), final_norm_w, start_exchange)
    ls = _gather_small(small)
    gwi, gwo, li, lo = _exchange_wait(*in_flight, gx)
    me = _me().astype(jnp.int32).reshape(1)
    g_wi, d_wi, nm_wi, nv_wi = _update_matrix("update_w_in", me, li, gwi, w_in[0], m_w_in[0], v_w_in[0], 256)
    g_wo, d_wo, nm_wo, nv_wo = _update_matrix("update_w_out", me, lo, gwo, w_out[0], m_w_out[0], v_w_out[0], 128)
    g_s, d_s, nm_s, nv_s, loss = _update_small(
        ls, hgrn_lb_logits, _pack_small(norm_w, hgrn_lb_logits, hg_norm_w, final_norm_w),
        _pack_small(m_norm_w, m_hgrn_lb_logits, m_hg_norm_w, m_final_norm_w),
        _pack_small(v_norm_w, v_hgrn_lb_logits, v_hg_norm_w, v_final_norm_w))
    outs = []
    for small_out, wi, wo in ((g_s, g_wi, g_wo), (d_s, d_wi, d_wo), (nm_s, nm_wi, nm_wo), (nv_s, nv_wi, nv_wo)):
        nw, lb, hg, fw = _unpack_small(small_out)
        outs += [nw, wi[None], lb, hg, wo[None], fw]
    return (loss[0, 0], gx[None], *outs)
```

```python
import functools

import jax
import jax.numpy as jnp
import numpy as np
from jax import lax
from jax.experimental import pallas as pl
from jax.experimental.pallas import tpu as pltpu

F32 = jnp.float32
MM = jnp.bfloat16
XCH = jnp.bfloat16
NORM_EPS = 1e-6
NEG = -1e30
N_DEV = 8
D_MODEL = 1024
N_SEC = 8
SEC_W = 512
HG_HEADS = 4
HG_D = 128
HG_GROUP = 4
AT_DH = 64
LANES = 128
ATT_BLK = 128
AT_COLS = 512
AT_QB = 8
DILATIONS = (1, 4, 16)
ROPE_THETA = 10000.0
CH = 16
LB_LO, LB_HI = 1e-6, 1.0 - 1e-6
ADAM_LR, ADAM_B1, ADAM_B2, ADAM_EPS, ADAM_WD, ADAM_STEP = 0.001, 0.9, 0.999, 1e-08, 0.01, 10
VMEM_LIMIT = 56 * 1024 * 1024
MESH = pl.DeviceIdType.MESH


def _params(*sem):
    return pltpu.CompilerParams(dimension_semantics=sem, vmem_limit_bytes=VMEM_LIMIT)


def _sigmoid(x):
    return 1.0 / (1.0 + jnp.exp(-x))


def _dot(a, b):
    return jnp.dot(a.astype(MM), b.astype(MM), preferred_element_type=F32)


def _dot_nt(a, b):
    return lax.dot_general(a.astype(MM), b.astype(MM), (((1,), (1,)), ((), ())), preferred_element_type=F32)


def _dot_tn(a, b):
    return lax.dot_general(a.astype(MM), b.astype(MM), (((0,), (0,)), ((), ())), preferred_element_type=F32)


def _split3(g):
    g1 = g.astype(jnp.bfloat16)
    r1 = g - g1.astype(F32)
    g2 = r1.astype(jnp.bfloat16)
    return g1, g2, (r1 - g2.astype(F32)).astype(jnp.bfloat16)


def _tri_dot(tri, g):
    t = tri.astype(jnp.bfloat16)
    g1, g2, g3 = _split3(g)
    d = functools.partial(jnp.dot, preferred_element_type=F32)
    return d(t, g1) + d(t, g2) + d(t, g3)


def _lower_bound(lbl):
    l0, l1 = lbl[0:1, :], lbl[1:2, :]
    m = jnp.maximum(l0, l1)
    e0, e1 = jnp.exp(l0 - m), jnp.exp(l1 - m)
    p = e0 / (e0 + e1)
    inside = (p >= LB_LO) & (p <= LB_HI)
    return jnp.clip(p, LB_LO, LB_HI), jnp.where(inside, p * (e1 / (e0 + e1)), 0.0)


def _iota2(shape, dim):
    return lax.broadcasted_iota(jnp.int32, shape, dim)


def _hgrn_gates(xq, xf, lb):
    sgq = _sigmoid(xq)
    sg = _sigmoid(xf)
    sn = _sigmoid(-xf)
    f = lb + (1.0 - lb) * sg
    return sgq, xq * sgq, sg, sn, f, (1.0 - lb) * sn


def _bdot(a, b, ca, cb):
    return lax.dot_general(a.astype(MM), b.astype(MM), (((ca,), (cb,)), ((0,), (0,))), preferred_element_type=F32)


def _chunk_masks(rb):
    row, col = _iota2((rb, rb), 0), _iota2((rb, rb), 1)
    same = (row // CH) == (col // CH)
    return same & (row >= col), same & (row <= col)


HALF = CH // 2
SLAB_ROWS = HALF * CH + (HALF // 2) * CH


def _write_slabs(slab_ref, g, q3, b3):
    slab = lambda t, rows: q3[:, rows, :] * jnp.exp(jnp.minimum(b3[:, rows, :] - b3[:, t:t + 1, :], 0.0))
    late = slice(HALF, CH)
    for t in range(HALF):
        slab_ref[g, :, t * CH:(t + 1) * CH, :] = slab(t, slice(0, CH)).astype(MM)
    for p in range(HALF // 2):
        t = HALF + 2 * p
        two = jnp.concatenate([slab(t, late), slab(t + 1, late)], axis=1)
        slab_ref[g, :, (HALF + p) * CH:(HALF + p + 1) * CH, :] = two.astype(MM)


def _read_diag(r):
    nc = r.shape[0]
    col, col_late = _iota2((nc, CH, CH), 2), _iota2((nc, HALF, CH), 2)
    a, a_late = jnp.zeros((nc, CH, CH), F32), jnp.zeros((nc, HALF, CH), F32)
    for t in range(HALF):
        a = a + jnp.where(col == t, r[:, t * CH:(t + 1) * CH, :], 0.0)
    for p in range(HALF // 2):
        t, two = HALF + 2 * p, r[:, (HALF + p) * CH:(HALF + p + 1) * CH, :]
        a_late = a_late + jnp.where(col_late == t, two[:, :HALF, :], 0.0) + jnp.where(col_late == t + 1, two[:, HALF:, :], 0.0)
    return a + jnp.concatenate([jnp.zeros_like(a_late), a_late], axis=1)


def _hgrn_fwd(proj, lb_logits, rb=256, nsub=4):
    s = proj.shape[1]
    nc, rows = rb // CH, rb * nsub
    nb = s // rows

    def body(q_ref, f_ref, i_ref, lbl_ref, o_ref, sst_ref, a_ref, st_ref, slab_ref, states_ref):
        @pl.when(pl.program_id(1) == 0)
        def _():
            st_ref[...] = jnp.zeros_like(st_ref)

        prefix, _ = _chunk_masks(rb)
        c3 = lambda a: a.reshape(nc, CH, HG_D)
        row, col = _iota2((nc, CH, CH), 1), _iota2((nc, CH, CH), 2)
        units = []
        for u in range(nsub):
            rs = slice(u * rb, (u + 1) * rb)
            for g in range(HG_GROUP):
                hs = slice(g * HG_D, (g + 1) * HG_D)
                lb, _ = _lower_bound(lbl_ref[:, hs])
                _, q, _, _, f, kk = _hgrn_gates(q_ref[rs, hs], f_ref[rs, hs], lb)
                b3 = c3(_tri_dot(prefix, jnp.log(f)))
                q3, kk3, v3 = c3(q), c3(kk), c3(i_ref[rs, hs])
                bl3 = b3[:, CH - 1:CH, :]
                _write_slabs(slab_ref, u * HG_GROUP + g, q3, b3)
                x_upd = _bdot(v3, kk3 * jnp.exp(bl3 - b3), 1, 1)
                units.append(dict(g=g, rs=rs, hs=hs, kk3=kk3, v3=v3, qe3=q3 * jnp.exp(b3), ebl3=jnp.exp(bl3),
                                  x_upd=x_upd))
        for g in range(HG_GROUP):
            st = st_ref[g]
            for u in range(nsub):
                hd = units[u * HG_GROUP + g]
                sst_ref[u, g] = st
                for c in range(nc):
                    states_ref[u * HG_GROUP + g, c] = st
                    st = st * hd["ebl3"][c] + hd["x_upd"][c]
            st_ref[g] = st
        for k, hd in enumerate(units):
            a = _read_diag(_bdot(slab_ref[k], hd["kk3"], 2, 2))
            a = jnp.where(row >= col, a, 0.0)
            a_ref[hd["rs"], hd["g"] * CH:(hd["g"] + 1) * CH] = a.reshape(rb, CH)
            o3 = _bdot(hd["qe3"], states_ref[k], 2, 2) + _bdot(a, hd["v3"], 2, 1)
            o_ref[hd["rs"], hd["hs"]] = o3.reshape(rb, HG_D)

    wide = HG_GROUP * HG_D
    sec = lambda j: pl.BlockSpec((None, rows, wide), lambda h, i, j=j: (j, i, h))
    return pl.pallas_call(
        body, name="hgrn_fwd", grid=(HG_HEADS // HG_GROUP, nb),
        in_specs=[sec(0), sec(1), sec(2), pl.BlockSpec((2, wide), lambda h, i: (0, h))],
        out_specs=[pl.BlockSpec((rows, wide), lambda h, i: (i, h)),
                   pl.BlockSpec((nsub, HG_GROUP, HG_D, HG_D), lambda h, i: (i, h, 0, 0)),
                   pl.BlockSpec((rows, HG_GROUP * CH), lambda h, i: (i, h))],
        out_shape=[jax.ShapeDtypeStruct((s, SEC_W), F32),
                   jax.ShapeDtypeStruct((s // rb, HG_HEADS, HG_D, HG_D), F32),
                   jax.ShapeDtypeStruct((s, HG_HEADS * CH), F32)],
        scratch_shapes=[pltpu.VMEM((HG_GROUP, HG_D, HG_D), F32),
                        pltpu.VMEM((nsub * HG_GROUP, nc, SLAB_ROWS, HG_D), MM),
                        pltpu.VMEM((nsub * HG_GROUP, nc, HG_D, HG_D), F32)],
        compiler_params=_params("parallel", "arbitrary"),
    )(proj, proj, proj, lb_logits)


def _hgrn_bwd(proj, lb_logits, d_o, sst, a_in, rb=256, nsub=1):
    s = proj.shape[1]
    nc, rows = rb // CH, rb * nsub
    nb = s // rows

    def body(q_ref, f_ref, i_ref, lbl_ref, do_ref, sst_ref, a_ref, dxq_ref, dxf_ref, dxi_ref, dlb_ref,
             dst_ref, states_ref, dstates_ref, lslab_ref, kslab_ref):
        @pl.when(pl.program_id(1) == 0)
        def _():
            dst_ref[...] = jnp.zeros_like(dst_ref)
            dlb_ref[...] = jnp.zeros_like(dlb_ref)

        prefix, suffix = _chunk_masks(rb)
        c3 = lambda a: a.reshape(nc, CH, HG_D)
        flat = lambda a: a.reshape(rb, HG_D)
        row, col = _iota2((nc, CH, CH), 1), _iota2((nc, CH, CH), 2)
        tril, triu = row >= col, row <= col
        sel = (_iota2((CH, CH * CH), 1) % CH == _iota2((CH, CH * CH), 0)).astype(MM)
        blockdiag = _iota2((nc, CH, CH * CH), 2) // CH == _iota2((nc, CH, CH * CH), 1)
        tile = lambda m: jnp.where(blockdiag, _dot(m.reshape(rb, CH), sel).reshape(nc, CH, CH * CH), 0.0)
        last = _iota2((nc, CH, HG_D), 1) == CH - 1
        units = []
        for u in range(nsub):
            rs = slice(u * rb, (u + 1) * rb)
            for g in range(HG_GROUP):
                k = u * HG_GROUP + g
                hs = slice(g * HG_D, (g + 1) * HG_D)
                lb, _ = _lower_bound(lbl_ref[:, hs])
                xq = q_ref[rs, hs]
                sgq, q, sg, sn, f, kk = _hgrn_gates(xq, f_ref[rs, hs], lb)
                b3 = c3(_tri_dot(prefix, jnp.log(f)))
                q3, kk3, v3, do3 = c3(q), c3(kk), c3(i_ref[rs, hs]), c3(do_ref[rs, hs])
                bl3 = b3[:, CH - 1:CH, :]
                eb3, ebl3, dec3 = jnp.exp(b3), jnp.exp(bl3), jnp.exp(bl3 - b3)
                qe3, kd3 = q3 * eb3, kk3 * dec3
                x_upd, y_upd = _bdot(v3, kd3, 1, 1), _bdot(do3, qe3, 1, 1)
                zero, every, early, late = jnp.zeros((nc, HALF, HG_D), F32), slice(0, CH), slice(0, HALF), slice(HALF, CH)
                for t in range(CH):
                    bt = b3[:, t:t + 1, :]
                    since = lambda part: q3[:, part, :] * jnp.exp(jnp.minimum(b3[:, part, :] - bt, 0.0))
                    until = lambda part: kk3[:, part, :] * jnp.exp(jnp.minimum(bt - b3[:, part, :], 0.0))
                    if t < HALF:
                        lv, kv = since(every), jnp.concatenate([until(early), zero], axis=1)
                    else:
                        lv, kv = jnp.concatenate([zero, since(late)], axis=1), until(every)
                    lslab_ref[k, :, t * CH:(t + 1) * CH, :] = lv.astype(MM)
                    kslab_ref[k, :, t * CH:(t + 1) * CH, :] = kv.astype(MM)
                d_a = jnp.where(tril, _bdot(do3, v3, 2, 2), 0.0)
                d_at = jnp.where(triu, _bdot(v3, do3, 2, 2), 0.0)
                units.append(dict(k=k, u=u, g=g, rs=rs, hs=hs, lb=lb, xq=xq, sgq=sgq, sg=sg, sn=sn, f=f, q3=q3, kk3=kk3,
                                  v3=v3, do3=do3, eb3=eb3, ebl3=ebl3, dec3=dec3, qe3=qe3, kd3=kd3, x_upd=x_upd,
                                  y_upd=y_upd, d_a=d_a, d_at=d_at))
        for hd in units:
            st = sst_ref[hd["u"], hd["g"]]
            for c in range(nc):
                states_ref[hd["k"], c] = st
                st = st * hd["ebl3"][c] + hd["x_upd"][c]
        for g in range(HG_GROUP):
            dst = dst_ref[g]
            for u in reversed(range(nsub)):
                hd = units[u * HG_GROUP + g]
                for c in reversed(range(nc)):
                    dstates_ref[hd["k"], c] = dst
                    dst = dst * hd["ebl3"][c] + hd["y_upd"][c]
            dst_ref[g] = dst
        for hd in units:
            q3, v3, do3, kd3, k, g = hd["q3"], hd["v3"], hd["do3"], hd["kd3"], hd["k"], hd["g"]
            states, dstates = states_ref[k], dstates_ref[k]
            hd["dqe"] = _bdot(do3, states, 2, 1)
            hd["dkd"] = _bdot(v3, dstates, 2, 1)
            a = a_ref[hd["rs"], g * CH:(g + 1) * CH].reshape(nc, CH, CH)
            hd["dv"] = _bdot(kd3, dstates, 2, 2) + _bdot(a, do3, 1, 1)
            hd["dq_in"] = _bdot(tile(hd["d_a"]), kslab_ref[k], 2, 1)
            hd["dk_in"] = _bdot(tile(hd["d_at"]), lslab_ref[k], 2, 1)
            hd["ss"] = jnp.sum(dstates * states, axis=1, keepdims=True)
        for hd in units:
            q3, kk3, eb3, ebl3, dec3, qe3, kd3 = (hd[k] for k in ("q3", "kk3", "eb3", "ebl3", "dec3", "qe3", "kd3"))
            dqe, dkd, dq_in, dk_in = hd["dqe"], hd["dkd"], hd["dq_in"], hd["dk_in"]
            dkd_kd = dkd * kd3
            db = dqe * qe3 - dkd_kd + q3 * dq_in - kk3 * dk_in
            dbl = jnp.sum(dkd_kd, axis=1, keepdims=True) + hd["ss"] * ebl3
            dg = _tri_dot(suffix, flat(db + jnp.where(last, dbl, 0.0)))
            df = dg / hd["f"] - flat(dkd * dec3 + dk_in)
            xq, sgq, hs, rs = hd["xq"], hd["sgq"], hd["hs"], hd["rs"]
            dxq_ref[rs, hs] = (flat(dqe * eb3 + dq_in) * (sgq * (1.0 + xq * (1.0 - sgq)))).astype(MM)
            dxf_ref[rs, hs] = (df * (1.0 - hd["lb"]) * hd["sg"] * hd["sn"]).astype(MM)
            dxi_ref[rs, hs] = flat(hd["dv"]).astype(MM)
            dlb_ref[:, hs] += jnp.sum(df * hd["sn"], axis=0, keepdims=True)

    wide = HG_GROUP * HG_D
    rev = lambda i: nb - 1 - i
    sec = lambda j: pl.BlockSpec((None, rows, wide), lambda h, i, j=j: (j, rev(i), h))
    blk = pl.BlockSpec((rows, wide), lambda h, i: (rev(i), h))
    state = (pltpu.VMEM((nsub * HG_GROUP, nc, HG_D, HG_D), F32), pltpu.VMEM((nsub * HG_GROUP, nc, CH * CH, HG_D), MM))
    return pl.pallas_call(
        body, name="hgrn_bwd", grid=(HG_HEADS // HG_GROUP, nb),
        in_specs=[sec(0), sec(1), sec(2), pl.BlockSpec((2, wide), lambda h, i: (0, h)), blk,
                  pl.BlockSpec((nsub, HG_GROUP, HG_D, HG_D), lambda h, i: (rev(i), h, 0, 0)),
                  pl.BlockSpec((rows, HG_GROUP * CH), lambda h, i: (rev(i), h))],
        out_specs=[blk, blk, blk, pl.BlockSpec((1, wide), lambda h, i: (0, h))],
        out_shape=[jax.ShapeDtypeStruct((s, SEC_W), MM)] * 3 + [jax.ShapeDtypeStruct((1, SEC_W), F32)],
        scratch_shapes=[pltpu.VMEM((HG_GROUP, HG_D, HG_D), F32), state[0], state[0], state[1], state[1]],
        compiler_params=_params("parallel", "arbitrary"),
    )(proj, proj, proj, lb_logits, d_o, sst, a_in)


def _rope_tables(s):
    half = AT_DH // 2
    inv_freq = np.float32(1.0) / (np.float32(ROPE_THETA) ** (np.arange(half, dtype=np.float32) / np.float32(half)))
    ang = np.arange(s, dtype=np.float32)[:, None] * inv_freq[None, :]
    cos, sin = np.cos(ang), np.sin(ang)
    return np.concatenate([cos] * 4, axis=-1), np.concatenate([-sin, sin] * 2, axis=-1)


def _rope128(x, cos, sin):
    lo = (_iota2(x.shape, 1) % AT_DH) < AT_DH // 2
    rot = jnp.where(lo, pltpu.roll(x, LANES - AT_DH // 2, 1), pltpu.roll(x, AT_DH // 2, 1))
    return x * cos + rot * sin


LANE_GROUPS = SEC_W // LANES


def _set_lanes(ref, val):
    for j in range(LANE_GROUPS):
        ref[j] = val[:, j * LANES:(j + 1) * LANES]


def _get_lanes(ref):
    return jnp.concatenate([ref[j] for j in range(LANE_GROUPS)], axis=-1)


def _to_view(src_ref, dst_ref, d):
    n = src_ref.shape[1] // d
    for r in range(d):
        rows = pl.ds(r, n, stride=d) if d > 1 else slice(None)
        for j in range(LANE_GROUPS):
            c0 = r * SEC_W + j * LANES
            dst_ref[:, c0:c0 + LANES] = src_ref.at[j][rows, :].astype(dst_ref.dtype)


def _from_view(src_ref, dst_ref, d):
    n = dst_ref.shape[1] // d
    for r in range(d):
        for j in range(LANE_GROUPS):
            c0 = r * SEC_W + j * LANES
            dst_ref.at[j][pl.ds(r, n, stride=d), :] = src_ref[:, c0:c0 + LANES].astype(dst_ref.dtype)


def _view_spec(tm, d):
    return pl.BlockSpec((tm // d, d * SEC_W), lambda i: (i, 0))


def _view_shape(s, d, dtype):
    return jax.ShapeDtypeStruct((s // d, d * SEC_W), dtype)


PROJ_KEPT = (0, 1, 2, 3, 7)


def _inproj_fwd(x, norm_w, w_all, cos, sin, tm=512):
    s = x.shape[0]

    def body(x_ref, nw_ref, w_ref, cos_ref, sin_ref, proj_ref, *refs):
        outs, (qs_ref, ks_ref, vs_ref) = refs[:-3], refs[-3:]
        xv = x_ref[...]
        rstd = lax.rsqrt(jnp.mean(xv * xv, axis=-1, keepdims=True) + NORM_EPS)
        u = (xv * rstd * nw_ref[...]).astype(MM)
        for slot, j in enumerate(PROJ_KEPT):
            proj_ref[slot] = jnp.dot(u, w_ref[j], preferred_element_type=F32)
        q, k, v = [jnp.dot(u, w_ref[j], preferred_element_type=F32) for j in (4, 5, 6)]
        c, sn = cos_ref[...], sin_ref[...]
        for g in range(LANE_GROUPS):
            sl = slice(g * LANES, (g + 1) * LANES)
            qs_ref[g] = _rope128(q[:, sl], c, sn) * (AT_DH ** -0.5)
            ks_ref[g] = _rope128(k[:, sl], c, sn)
            vs_ref[g] = v[:, sl]
        for i, d in enumerate(DILATIONS):
            for src_ref, dst_ref in zip((qs_ref, ks_ref, vs_ref), outs[3 * i:3 * i + 3]):
                _to_view(src_ref, dst_ref, d)

    tab = pl.BlockSpec((tm, LANES), lambda i: (i, 0))
    return pl.pallas_call(
        body, name="inproj_fwd", grid=(s // tm,),
        in_specs=[pl.BlockSpec((tm, D_MODEL), lambda i: (i, 0)),
                  pl.BlockSpec((1, D_MODEL), lambda i: (0, 0)),
                  pl.BlockSpec((N_SEC, D_MODEL, SEC_W), lambda i: (0, 0, 0)), tab, tab],
        out_specs=[pl.BlockSpec((len(PROJ_KEPT), tm, SEC_W), lambda i: (0, i, 0))]
                  + [_view_spec(tm, d) for d in DILATIONS for _ in range(3)],
        out_shape=[jax.ShapeDtypeStruct((len(PROJ_KEPT), s, SEC_W), F32)]
                  + [_view_shape(s, d, MM) for d in DILATIONS for _ in range(3)],
        scratch_shapes=[pltpu.VMEM((LANE_GROUPS, tm, LANES), F32)] * 3,
        compiler_params=_params("parallel"),
    )(x, norm_w, w_all, cos, sin)


def _band_mask(first_ok, second_ok):
    row, col = _iota2((ATT_BLK, 2 * ATT_BLK), 0), _iota2((ATT_BLK, 2 * ATT_BLK), 1)
    return ((col < ATT_BLK) & (col >= row) & first_ok) | ((col >= ATT_BLK) & ((col - ATT_BLK) <= row) & second_ok)


def _own_lanes(rows, h):
    lane = _iota2((rows, LANES), 1)
    return (lane < AT_DH) if h == 0 else (lane >= AT_DH)


def _neg_pieces(rows, h):
    lane = _iota2((rows, LANES), 1) - (AT_DH if h == 0 else 0)
    return jnp.where((lane >= 0) & (lane < 3), -1.0, 0.0).astype(MM)


def _units(qb):
    return [(b, slice(g * LANES, (g + 1) * LANES), h) for b in range(qb) for g in range(AT_COLS // LANES) for h in range(2)]


def _sub(b):
    return slice(b * ATT_BLK, (b + 1) * ATT_BLK)


def _band_before(cur_ref, prev_ref, b, sl):
    if b == 0:
        return jnp.concatenate([prev_ref[:, sl], cur_ref[0:ATT_BLK, sl]], axis=0)
    return cur_ref[(b - 1) * ATT_BLK:(b + 1) * ATT_BLK, sl]


def _band_after(cur_ref, next_ref, b, sl):
    if (b + 1) * ATT_BLK == cur_ref.shape[0]:
        return jnp.concatenate([cur_ref[b * ATT_BLK:(b + 1) * ATT_BLK, sl], next_ref[:, sl]], axis=0)
    return cur_ref[b * ATT_BLK:(b + 2) * ATT_BLK, sl]


def _attn_specs(rows):
    qb = min(AT_QB, rows // ATT_BLK)
    assert rows % (qb * ATT_BLK) == 0
    last = rows // ATT_BLK - 1
    cur = pl.BlockSpec((qb * ATT_BLK, AT_COLS), lambda c, n: (n, c))
    prev = pl.BlockSpec((ATT_BLK, AT_COLS), lambda c, n: (jnp.maximum(qb * n - 1, 0), c))
    nxt = pl.BlockSpec((ATT_BLK, AT_COLS), lambda c, n: (jnp.minimum(qb * (n + 1), last), c))
    return qb, cur, prev, nxt


def _stack_heads(a):
    h0 = _own_lanes(a.shape[0], 0)
    zero = jnp.zeros_like(a)
    return jnp.concatenate([jnp.where(h0, a, zero), jnp.where(h0, zero, a)], axis=0)


def _unstack_heads(a2):
    return jnp.where(_own_lanes(ATT_BLK, 0), a2[:ATT_BLK], a2[ATT_BLK:])


def _attn_fwd(qr, kr, vr, d):
    rows, cols = qr.shape
    qb, cur, prev, nxt = _attn_specs(rows)
    nb = rows // (qb * ATT_BLK)

    def body(q_ref, kc_ref, kp_ref, vc_ref, vp_ref, o_ref, lse_ref):
        twice = lambda m: jnp.concatenate([m, m], axis=0)
        masks = {True: twice(_band_mask(pl.program_id(1) > 0, True)), False: twice(_band_mask(True, True))}
        ones = jnp.ones((2 * ATT_BLK, LANES), MM)
        units = [(b, sl) for b, sl, h in _units(qb) if h == 0]
        scs = [jnp.where(masks[b == 0], _dot_nt(_stack_heads(q_ref[_sub(b), sl]), _band_before(kc_ref, kp_ref, b, sl)),
                         NEG) for b, sl in units]
        ms = [jnp.max(sc, axis=-1, keepdims=True) for sc in scs]
        ps = [jnp.exp(sc - m).astype(MM) for sc, m in zip(scs, ms)]
        ols = [jnp.dot(p, jnp.concatenate([_band_before(vc_ref, vp_ref, b, sl), ones], axis=1),
                       preferred_element_type=F32) for p, (b, sl) in zip(ps, units)]
        for (b, sl), m, ol in zip(units, ms, ols):
            l = _unstack_heads(ol[:, LANES:])
            o_ref[_sub(b), sl] = _unstack_heads(ol[:, :LANES]) / l
            lse_ref[_sub(b), sl] = _unstack_heads(jnp.broadcast_to(m, (2 * ATT_BLK, LANES))) + jnp.log(l)

    o, lse = pl.pallas_call(
        body, name=f"attn_fwd_d{d}", grid=(cols // AT_COLS, nb),
        in_specs=[cur, cur, prev, cur, prev], out_specs=[cur, cur],
        out_shape=[jax.ShapeDtypeStruct((rows, cols), F32)] * 2,
        compiler_params=_params("parallel", "parallel"),
    )(qr, kr, kr, vr, vr)
    return o, lse


def _attn_bwd_dq(qr, kr, vr, do, lse, delta, d):
    rows, cols = qr.shape
    qb, cur, prev, nxt = _attn_specs(rows)
    nb = rows // (qb * ATT_BLK)

    def body(q_ref, kc_ref, kp_ref, vc_ref, vp_ref, do_ref, lse_ref, dl_ref, dq_ref):
        masks = {True: _band_mask(pl.program_id(1) > 0, True), False: _band_mask(True, True)}
        units = _units(qb)
        sms, dps = [], []
        for b, sl, h in units:
            own, own_b, neg = _own_lanes(ATT_BLK, h), _own_lanes(2 * ATT_BLK, h), _neg_pieces(2 * ATT_BLK, h)
            sms.append(_dot_nt(jnp.where(own, q_ref[_sub(b), sl], lse_ref[_sub(b), sl]),
                               jnp.where(own_b, _band_before(kc_ref, kp_ref, b, sl), neg)))
            dps.append(_dot_nt(jnp.where(own, do_ref[_sub(b), sl], dl_ref[_sub(b), sl]),
                               jnp.where(own_b, _band_before(vc_ref, vp_ref, b, sl), neg)))
        dss = [(jnp.exp(jnp.where(masks[b == 0], sm, NEG)) * dp).astype(MM)
               for sm, dp, (b, _, _) in zip(sms, dps, units)]
        dqs = [jnp.dot(ds, _band_before(kc_ref, kp_ref, b, sl), preferred_element_type=F32) * (AT_DH ** -0.5)
               for ds, (b, sl, _) in zip(dss, units)]
        for i in range(0, len(units), 2):
            b, sl, _ = units[i]
            dq_ref[_sub(b), sl] = jnp.where(_own_lanes(ATT_BLK, 0), dqs[i], dqs[i + 1]).astype(dq_ref.dtype)

    dq = pl.pallas_call(
        body, name=f"attn_bwd_dq_d{d}", grid=(cols // AT_COLS, nb),
        in_specs=[cur, cur, prev, cur, prev, cur, cur, cur], out_specs=cur,
        out_shape=jax.ShapeDtypeStruct((rows, cols), MM),
        compiler_params=_params("parallel", "parallel"),
    )(qr, kr, kr, vr, vr, do, lse, delta)
    return dq


def _attn_bwd_dkv(qr, kr, vr, do, lse, delta, d):
    rows, cols = qr.shape
    qb, cur, prev, nxt = _attn_specs(rows)
    nb = rows // (qb * ATT_BLK)

    def body(k_ref, v_ref, qc_ref, qn_ref, doc_ref, don_ref, lsec_ref, lsen_ref, dlc_ref, dln_ref,
             dk_ref, dv_ref):
        masks = {True: _band_mask(True, pl.program_id(1) < nb - 1), False: _band_mask(True, True)}
        units = _units(qb)
        sms, dps = [], []
        for b, sl, h in units:
            own, own_b, neg = _own_lanes(ATT_BLK, h), _own_lanes(2 * ATT_BLK, h), _neg_pieces(ATT_BLK, h)
            sms.append(_dot_nt(jnp.where(own, k_ref[_sub(b), sl], neg),
                               jnp.where(own_b, _band_after(qc_ref, qn_ref, b, sl),
                                         _band_after(lsec_ref, lsen_ref, b, sl))))
            dps.append(_dot_nt(jnp.where(own, v_ref[_sub(b), sl], neg),
                               jnp.where(own_b, _band_after(doc_ref, don_ref, b, sl),
                                         _band_after(dlc_ref, dln_ref, b, sl))))
        ps = [jnp.exp(jnp.where(masks[b == qb - 1], sm, NEG)) for sm, (b, _, _) in zip(sms, units)]
        dss = [(p * dp).astype(MM) for p, dp in zip(ps, dps)]
        dvs = [jnp.dot(p.astype(MM), _band_after(doc_ref, don_ref, b, sl), preferred_element_type=F32)
               for p, (b, sl, _) in zip(ps, units)]
        dks = [jnp.dot(ds, _band_after(qc_ref, qn_ref, b, sl), preferred_element_type=F32)
               for ds, (b, sl, _) in zip(dss, units)]
        head0 = _own_lanes(ATT_BLK, 0)
        for i in range(0, len(units), 2):
            b, sl, _ = units[i]
            dk_ref[_sub(b), sl] = jnp.where(head0, dks[i], dks[i + 1]).astype(dk_ref.dtype)
            dv_ref[_sub(b), sl] = jnp.where(head0, dvs[i], dvs[i + 1]).astype(dv_ref.dtype)

    dk, dv = pl.pallas_call(
        body, name=f"attn_bwd_dkv_d{d}", grid=(cols // AT_COLS, nb),
        in_specs=[cur, cur, cur, nxt, cur, nxt, cur, nxt, cur, nxt], out_specs=[cur, cur],
        out_shape=[jax.ShapeDtypeStruct((rows, cols), MM)] * 2,
        compiler_params=_params("parallel", "parallel"),
    )(kr, vr, qr, qr, do, do, lse, lse, delta, delta)
    return dk, dv


def _head_sum(a, width):
    parts = []
    for j in range(a.shape[1] // width):
        sm = jnp.sum(a[:, j * width:(j + 1) * width], axis=-1, keepdims=True)
        parts.append(jnp.broadcast_to(sm, (a.shape[0], width)))
    return jnp.concatenate(parts, axis=-1)


def _partner_sum(a):
    swap = (_iota2((LANES, LANES), 0) // AT_DH != _iota2((LANES, LANES), 1) // AT_DH).astype(jnp.bfloat16)
    d = functools.partial(jnp.dot, preferred_element_type=F32)
    parts = _split3(a)
    return jnp.concatenate([d(parts[0][:, sl], swap) + d(parts[1][:, sl], swap) + d(parts[2][:, sl], swap)
                            for sl in (slice(j * LANES, (j + 1) * LANES) for j in range(a.shape[1] // LANES))], axis=-1)


def _partner_value(x):
    return jnp.concatenate([pltpu.roll(x[:, j * LANES:(j + 1) * LANES], AT_DH, 1) for j in range(x.shape[1] // LANES)],
                           axis=-1)


def _pieces(xs):
    hi = xs.astype(jnp.bfloat16).astype(F32)
    mid = (xs - hi).astype(jnp.bfloat16).astype(F32)
    lo = (xs - hi - mid).astype(jnp.bfloat16).astype(F32)
    lane = _iota2(xs.shape, 1) % AT_DH
    return jnp.where(lane == 0, hi, jnp.where(lane == 1, mid, jnp.where(lane == 2, lo, 0.0)))


def _mid(x, tgt, proj, o_hg, o_at, lse_at, hg_norm_w, final_norm_w, wo_all, tm=256):
    s = x.shape[0]
    nb = s // tm

    def body(x_ref, t_ref, hgz_ref, atz_ref, ohg_ref, o1_ref, o2_ref, o3_ref, l1_ref, l2_ref, l3_ref,
             g_ref, fw_ref, wo_ref,
             dh_ref, dohg_ref, dhgz_ref, datz_ref, do1_ref, do2_ref, do3_ref, dl1_ref, dl2_ref, dl3_ref,
             lp1_ref, lp2_ref, lp3_ref,
             gwo_ref, gfw_ref, ghg_ref, loss_ref, nat_ref, stage_ref, gwo_acc):
        @pl.when(pl.program_id(0) == 0)
        def _():
            gwo_acc[...] = jnp.zeros_like(gwo_acc)
            gfw_ref[...] = jnp.zeros_like(gfw_ref)
            ghg_ref[...] = jnp.zeros_like(ghg_ref)
            loss_ref[...] = jnp.zeros_like(loss_ref)

        ohg, g = ohg_ref[...], g_ref[...]
        rs = lax.rsqrt(_head_sum(ohg * ohg, HG_D) * (1.0 / HG_D) + NORM_EPS)
        on = ohg * rs
        hgz = hgz_ref[...]
        sz = _sigmoid(hgz)
        gate_hg = hgz * sz
        lses, outs = [l1_ref[...]], [o1_ref[...]]
        for k, (d, l_ref, o_ref) in enumerate(zip(DILATIONS[1:], (l2_ref, l3_ref), (o2_ref, o3_ref))):
            _from_view(l_ref, nat_ref.at[2 * k], d)
            _from_view(o_ref, nat_ref.at[2 * k + 1], d)
            lses.append(_get_lanes(nat_ref.at[2 * k]))
            outs.append(_get_lanes(nat_ref.at[2 * k + 1]))
        mx = jnp.maximum(jnp.maximum(lses[0], lses[1]), lses[2])
        es = [jnp.exp(l - mx) for l in lses]
        den = es[0] + es[1] + es[2]
        ws = [e / den for e in es]
        oat = ws[0] * outs[0] + ws[1] * outs[1] + ws[2] * outs[2]
        atz = atz_ref[...]
        sa = _sigmoid(atz)
        gate_at = atz * sa
        mixed = jnp.concatenate([on * g * gate_hg, oat * gate_at], axis=-1).astype(MM)
        h = x_ref[...] + jnp.dot(mixed, wo_ref[...], preferred_element_type=F32)
        rstd = lax.rsqrt(jnp.mean(h * h, axis=-1, keepdims=True) + NORM_EPS)
        hn = h * rstd
        fw = fw_ref[...]
        err = hn * fw - t_ref[...]
        loss_ref[...] += 0.5 * jnp.sum(jnp.mean(err * err, axis=-1, keepdims=True), axis=0, keepdims=True)
        dout = err * (1.0 / D_MODEL)
        gfw_ref[...] += jnp.sum(dout * hn, axis=0, keepdims=True)
        dhn = dout * fw
        dh = rstd * (dhn - hn * jnp.mean(dhn * hn, axis=-1, keepdims=True))
        dh_ref[...] = dh
        dh_mm = dh.astype(MM)
        gwo_acc[...] += _dot_tn(mixed, dh_mm)

        @pl.when(pl.program_id(0) == nb - 1)
        def _():
            gwo_ref[...] = gwo_acc[...].astype(gwo_ref.dtype)

        dmixed = _dot_nt(dh_mm, wo_ref[...])
        dm_hg = dmixed[:, :SEC_W]
        d_ong = dm_hg * gate_hg
        dhgz_ref[...] = (dm_hg * (on * g) * (sz * (1.0 + hgz * (1.0 - sz)))).astype(MM)
        ghg_ref[...] += jnp.sum(d_ong * on, axis=0, keepdims=True)
        d_on = d_ong * g
        dohg_ref[...] = rs * (d_on - on * (_head_sum(d_on * on, HG_D) * (1.0 / HG_D)))
        dm_at = dmixed[:, SEC_W:]
        d_oat = dm_at * gate_at
        datz_ref[...] = (dm_at * oat * (sa * (1.0 + atz * (1.0 - sa)))).astype(MM)
        lse_all = mx + jnp.log(den)
        for val, dst_refs in ((d_oat, (do1_ref, do2_ref, do3_ref)),
                              (_pieces(_partner_sum(d_oat * oat)), (dl1_ref, dl2_ref, dl3_ref)),
                              (_pieces(_partner_value(lse_all)), (lp1_ref, lp2_ref, lp3_ref))):
            _set_lanes(stage_ref, val)
            for d, dst_ref in zip(DILATIONS, dst_refs):
                _to_view(stage_ref, dst_ref, d)

    row = lambda w: pl.BlockSpec((tm, w), lambda i: (i, 0))
    sec = lambda j: pl.BlockSpec((None, tm, SEC_W), lambda i, j=j: (j, i, 0))
    const = lambda shp: pl.BlockSpec(shp, lambda i: (0,) * len(shp))
    half = row(SEC_W)
    views = [_view_spec(tm, d) for d in DILATIONS]
    return pl.pallas_call(
        body, name="mid", grid=(nb,),
        in_specs=[row(D_MODEL), row(D_MODEL), sec(PROJ_KEPT.index(3)), sec(PROJ_KEPT.index(7)), half] + views * 2
                 + [const((1, SEC_W)), const((1, D_MODEL)), const((D_MODEL, D_MODEL))],
        out_specs=[row(D_MODEL)] + [half] * 3 + views * 3
                  + [const((D_MODEL, D_MODEL)), const((1, D_MODEL)), const((1, SEC_W)), const((1, 1))],
        out_shape=[jax.ShapeDtypeStruct((s, D_MODEL), F32), jax.ShapeDtypeStruct((s, SEC_W), F32)]
                  + [jax.ShapeDtypeStruct((s, SEC_W), MM)] * 2
                  + [_view_shape(s, d, MM) for d in DILATIONS] * 3
                  + [jax.ShapeDtypeStruct((D_MODEL, D_MODEL), XCH), jax.ShapeDtypeStruct((1, D_MODEL), F32),
                     jax.ShapeDtypeStruct((1, SEC_W), F32), jax.ShapeDtypeStruct((1, 1), F32)],
        scratch_shapes=[pltpu.VMEM((4, LANE_GROUPS, tm, LANES), F32), pltpu.VMEM((LANE_GROUPS, tm, LANES), F32),
                        pltpu.VMEM((D_MODEL, D_MODEL), F32)],
        compiler_params=_params("arbitrary"),
    )(x, tgt, proj, proj, o_hg, *o_at, *lse_at, hg_norm_w, final_norm_w, wo_all)


def _section_specs(dsecs, tm):
    return [pl.BlockSpec((tm, SEC_W), lambda i: (i, 0)) if k is None
            else pl.BlockSpec((None, tm, SEC_W), lambda i, k=k: (k, i, 0)) for _, k in dsecs]


def _inproj_bwd_x(x, norm_w, w_all, dh, dsecs, token, tm=512):
    s = x.shape[0]

    def body(x_ref, nw_ref, w_ref, dh_ref, tok_ref, *refs):
        sec_refs, (gx_ref, gnw_ref) = refs[:N_SEC], refs[N_SEC:]

        @pl.when(pl.program_id(0) == 0)
        def _():
            gnw_ref[...] = jnp.zeros_like(gnw_ref)

        du = jnp.zeros((tm, D_MODEL), F32)
        for j in range(N_SEC):
            du = du + _dot_nt(sec_refs[j][...], w_ref[j])
        xv, nw = x_ref[...], nw_ref[...]
        rstd = lax.rsqrt(jnp.mean(xv * xv, axis=-1, keepdims=True) + NORM_EPS)
        xn = xv * rstd
        gnw_ref[...] += jnp.sum(du * xn, axis=0, keepdims=True)
        dxn = du * nw
        dx = rstd * (dxn - xn * jnp.mean(dxn * xn, axis=-1, keepdims=True))
        gx_ref[...] = (dh_ref[...] + tok_ref[0:1, 0:1]) + dx

    row = lambda w: pl.BlockSpec((tm, w), lambda i: (i, 0))
    const = lambda shp: pl.BlockSpec(shp, lambda i: (0,) * len(shp))
    return pl.pallas_call(
        body, name="inproj_bwd_x", grid=(s // tm,),
        in_specs=[row(D_MODEL), const((1, D_MODEL)), const((N_SEC, D_MODEL, SEC_W)), row(D_MODEL), const((8, 128))]
                 + _section_specs(dsecs, tm),
        out_specs=[row(D_MODEL), const((1, D_MODEL))],
        out_shape=[jax.ShapeDtypeStruct((s, D_MODEL), F32), jax.ShapeDtypeStruct((1, D_MODEL), F32)],
        compiler_params=_params("arbitrary"),
    )(x, norm_w, w_all, dh, token, *[a for a, _ in dsecs])


def _inproj_bwd_w(x, norm_w, dsec, dq_r, dk_r, dv, cos, sin, tm=512):
    s = x.shape[0]
    nb = s // tm

    def body(x_ref, nw_ref, s0, s1, s2, s3, s7, q1, q2, q3, k1, k2, k3, v1, v2, v3, cos_ref, sin_ref,
             gw_hbm, datt_ref, acc_ref, stage_ref, nat_ref):
        @pl.when(pl.program_id(0) == 0)
        def _():
            acc_ref[...] = jnp.zeros_like(acc_ref)

        def total(refs):
            acc = refs[0][...].astype(F32)
            for d, ref in zip(DILATIONS[1:], refs[1:]):
                _from_view(ref, nat_ref, d)
                acc = acc + _get_lanes(nat_ref)
            return acc

        c, sn = cos_ref[...], -sin_ref[...]
        unrot = lambda a: jnp.concatenate(
            [_rope128(a[:, j * LANES:(j + 1) * LANES], c, sn) for j in range(LANE_GROUPS)], axis=-1)
        att = [a.astype(MM) for a in (unrot(total((q1, q2, q3))), unrot(total((k1, k2, k3))), total((v1, v2, v3)))]
        for j, a in enumerate(att):
            datt_ref[j] = a
        xv = x_ref[...]
        rstd = lax.rsqrt(jnp.mean(xv * xv, axis=-1, keepdims=True) + NORM_EPS)
        u_t = (xv * rstd * nw_ref[...]).T.astype(MM)
        for j, dsj in enumerate((s0[...], s1[...], s2[...], s3[...], *att, s7[...])):
            acc_ref[j] += jnp.dot(u_t, dsj, preferred_element_type=F32)

        @pl.when(pl.program_id(0) == nb - 1)
        def _():
            for j in range(N_SEC):
                stage_ref[...] = acc_ref[j].astype(stage_ref.dtype)
                pltpu.sync_copy(stage_ref, gw_hbm.at[j])

    row = lambda w: pl.BlockSpec((tm, w), lambda i: (i, 0))
    return pl.pallas_call(
        body, name="inproj_bwd_w", grid=(nb,),
        in_specs=[row(D_MODEL), pl.BlockSpec((1, D_MODEL), lambda i: (0, 0))] + [row(SEC_W)] * 5
                 + [_view_spec(tm, d) for d in DILATIONS] * 3 + [row(LANES), row(LANES)],
        out_specs=[pl.BlockSpec(memory_space=pl.ANY), pl.BlockSpec((3, tm, SEC_W), lambda i: (0, i, 0))],
        out_shape=[jax.ShapeDtypeStruct((N_SEC, D_MODEL, SEC_W), XCH), jax.ShapeDtypeStruct((3, s, SEC_W), MM)],
        scratch_shapes=[pltpu.VMEM((N_SEC, D_MODEL, SEC_W), F32), pltpu.VMEM((D_MODEL, SEC_W), XCH),
                        pltpu.VMEM((LANE_GROUPS, tm, LANES), F32)],
        compiler_params=_params("arbitrary"),
    )(x, norm_w, *dsec, *dq_r, *dk_r, *dv, cos, sin)


def _local_step(x, tgt, norm_w, w_all, lb_logits, hg_norm_w, wo_all, final_norm_w, on_weight_grads):
    s = x.shape[0]
    cos, sin = _rope_tables(s)
    proj, *qkv = _inproj_fwd(x, norm_w, w_all, cos, sin)
    o_hg, sst, a_hg = _hgrn_fwd(proj, lb_logits)
    qkv = [qkv[3 * i:3 * i + 3] for i in range(len(DILATIONS))]
    att = [_attn_fwd(*qkv_d, d) for qkv_d, d in zip(qkv, DILATIONS)]
    (dh, d_ohg, d_hgz, d_atz, do1, do2, do3, dl1, dl2, dl3, lp1, lp2, lp3, gwo, gfw, ghg, loss) = _mid(
        x, tgt, proj, o_hg, [a[0] for a in att], [a[1] for a in att], hg_norm_w, final_norm_w[None, :], wo_all)
    dxq, dxf, dxi, dlb = _hgrn_bwd(proj, lb_logits, d_ohg, sst, a_hg)
    dq_r, dk_r, dv = [], [], []
    for d, qkv_d, do, lp, dl in zip(DILATIONS, qkv, (do1, do2, do3), (lp1, lp2, lp3), (dl1, dl2, dl3)):
        dq_r.append(_attn_bwd_dq(*qkv_d, do, lp, dl, d))
        dk_d, dv_d = _attn_bwd_dkv(*qkv_d, do, lp, dl, d)
        dk_r.append(dk_d)
        dv.append(dv_d)
    gwi, d_att = _inproj_bwd_w(x, norm_w, (dxq, dxf, dxi, d_hgz, d_atz), dq_r, dk_r, dv, cos, sin)
    dsecs = [(dxq, None), (dxf, None), (dxi, None), (d_hgz, None), (d_att, 0), (d_att, 1), (d_att, 2), (d_atz, None)]
    token = on_weight_grads(gwi, gwo)
    gx, gnw = _inproj_bwd_x(x, norm_w, w_all, dh, dsecs, token)
    small = jnp.concatenate([gnw, jnp.concatenate([dlb, ghg], axis=-1), gfw,
                             jnp.pad(loss, ((0, 0), (0, D_MODEL - 1)))], axis=0)
    return gx, gwi, gwo, small


def _coords():
    return lax.axis_index("x"), lax.axis_index("y"), lax.axis_index("c")


def _gather_weights(w_in, w_out):
    wo_rows = w_out.shape[0]

    def body(wi_ref, wo_ref, wi_all, wo_all, send_sems, recv_sems):
        x, y, c = _coords()
        me, sibling = (x, y, c), (x, y, 1 - c)
        chips = [(1 - x, y), (x, 1 - y), (1 - x, 1 - y)]
        slot = lambda p: 4 * p[0] + 2 * p[1] + p[2]

        def copies(k, block, to):
            return [pltpu.make_async_remote_copy(
                src_ref=ref.at[slot(block)], dst_ref=ref.at[slot(block)], send_sem=send_sems.at[a, k],
                recv_sem=recv_sems.at[a, k], device_id=to, device_id_type=MESH)
                for a, ref in enumerate((wi_all, wo_all))]

        wi_all[slot(me)] = wi_ref[...].astype(MM)
        wo_all[slot(me)] = wo_ref[...].astype(MM)
        first = copies(0, me, sibling)
        for j, chip in enumerate(chips):
            first += copies(1 + j, me, (*chip, c))
        for cp in first:
            cp.start()
        passed = []
        for j, chip in enumerate(chips):
            for cp in copies(1 + j, (*chip, c), me):
                cp.wait_recv()
            fwd = copies(4 + j, (*chip, c), sibling)
            for cp in fwd:
                cp.start()
            passed += fwd
        for cp in copies(0, sibling, me):
            cp.wait_recv()
        for j, chip in enumerate(chips):
            for cp in copies(4 + j, (*chip, 1 - c), me):
                cp.wait_recv()
        for cp in first + passed:
            cp.wait_send()

    vmem = pl.BlockSpec(memory_space=pltpu.VMEM)
    return pl.pallas_call(
        body, name="gather_weights",
        in_specs=[vmem, vmem], out_specs=[vmem, vmem],
        out_shape=[jax.ShapeDtypeStruct((N_DEV, D_MODEL, SEC_W), MM),
                   jax.ShapeDtypeStruct((N_DEV, wo_rows, D_MODEL), MM)],
        scratch_shapes=[pltpu.SemaphoreType.DMA((2, 7)), pltpu.SemaphoreType.DMA((2, 7))],
        compiler_params=pltpu.CompilerParams(vmem_limit_bytes=VMEM_LIMIT),
    )(w_in, w_out)


def _me():
    x, y, c = _coords()
    return 4 * x + 2 * y + c


def _grad_copies(srcs, lands, send_sems, recv_sems):
    x, y, c = _coords()
    me = 4 * x + 2 * y + c
    copies = []
    for k in range(1, N_DEV):
        px, py, pc = x ^ (k >> 2), y ^ ((k >> 1) & 1), c ^ (k & 1)
        peer = 4 * px + 2 * py + pc
        for a, (src, dst) in enumerate(zip(srcs, lands)):
            copies.append(pltpu.make_async_remote_copy(
                src_ref=src.at[peer], dst_ref=dst.at[me], send_sem=send_sems.at[a * (N_DEV - 1) + k - 1],
                recv_sem=recv_sems.at[a * (N_DEV - 1) + k - 1], device_id=(px, py, pc), device_id_type=MESH))
    return copies


HBM_SPEC = pl.BlockSpec(memory_space=pltpu.HBM)
SEM_SPEC = pl.BlockSpec(memory_space=pltpu.SEMAPHORE)
SPLIT_COPY_EFFECT = pltpu.SideEffectType.DATAFLOW_SIDE_EFFECTING


def _exchange_start(gwi, gwo):
    def body(gwi_ref, gwo_ref, li_ref, lo_ref, send_sems, recv_sems, gwi_thru, gwo_thru, li_thru, lo_thru, token):
        for cp in _grad_copies((gwi_ref, gwo_ref), (li_ref, lo_ref), send_sems, recv_sems):
            cp.start()
        token[...] = jnp.zeros_like(token)

    hbm = lambda a: pltpu.with_memory_space_constraint(a, pltpu.HBM)
    bufs = (gwi, gwo, lax.empty(gwi.shape, gwi.dtype), lax.empty(gwo.shape, gwo.dtype))
    return pl.pallas_call(
        body, name="exchange_start",
        out_shape=(pltpu.SemaphoreType.DMA((2 * (N_DEV - 1),)), pltpu.SemaphoreType.DMA((2 * (N_DEV - 1),)),
                   *[pltpu.HBM(a.shape, a.dtype) for a in bufs], jax.ShapeDtypeStruct((8, 128), F32)),
        in_specs=[HBM_SPEC] * 4,
        out_specs=(SEM_SPEC, SEM_SPEC, HBM_SPEC, HBM_SPEC, HBM_SPEC, HBM_SPEC, pl.BlockSpec(memory_space=pltpu.VMEM)),
        input_output_aliases={0: 2, 1: 3, 2: 4, 3: 5},
        compiler_params=pltpu.CompilerParams(has_side_effects=SPLIT_COPY_EFFECT),
    )(*[hbm(a) for a in bufs])


def _exchange_wait(send_sems, recv_sems, gwi, gwo, li, lo, after):
    def body(gwi_ref, gwo_ref, li_ref, lo_ref, send_sems, recv_sems, after_ref, gwi_out, gwo_out, li_out, lo_out):
        for cp in _grad_copies((gwi_ref, gwo_ref), (li_ref, lo_ref), send_sems, recv_sems):
            cp.wait_send()
            cp.wait_recv()

    return pl.pallas_call(
        body, name="exchange_wait",
        out_shape=tuple(pltpu.HBM(a.shape, a.dtype) for a in (gwi, gwo, li, lo)),
        in_specs=[HBM_SPEC] * 4 + [SEM_SPEC, SEM_SPEC, pl.BlockSpec(memory_space=pl.ANY)],
        out_specs=(HBM_SPEC,) * 4,
        input_output_aliases={0: 0, 1: 1, 2: 2, 3: 3},
        compiler_params=pltpu.CompilerParams(has_side_effects=SPLIT_COPY_EFFECT),
    )(gwi, gwo, li, lo, send_sems, recv_sems, after)


def _gather_small(small):
    def body(sm_ref, ls_ref, send_sems, recv_sems, local_sem):
        x, y, c = _coords()
        me = 4 * x + 2 * y + c
        own = pltpu.make_async_copy(sm_ref, ls_ref.at[me], local_sem)
        own.start()
        sends = []
        for k in range(1, N_DEV):
            peer = (x ^ (k >> 2), y ^ ((k >> 1) & 1), c ^ (k & 1))
            sends.append(pltpu.make_async_remote_copy(
                src_ref=sm_ref, dst_ref=ls_ref.at[me], send_sem=send_sems.at[k - 1], recv_sem=recv_sems.at[k - 1],
                device_id=peer, device_id_type=MESH))
        for cp in sends:
            cp.start()
        for cp in sends:
            cp.wait_recv()
        for cp in sends:
            cp.wait_send()
        own.wait()

    vmem = pl.BlockSpec(memory_space=pltpu.VMEM)
    return pl.pallas_call(
        body, name="gather_small", in_specs=[vmem], out_specs=vmem,
        out_shape=jax.ShapeDtypeStruct((N_DEV,) + small.shape, F32),
        scratch_shapes=[pltpu.SemaphoreType.DMA((N_DEV - 1,)), pltpu.SemaphoreType.DMA((N_DEV - 1,)),
                        pltpu.SemaphoreType.DMA],
    )(small)


def _adamw(w, g, m, v):
    m = ADAM_B1 * m + (1.0 - ADAM_B1) * g
    v = ADAM_B2 * v + (1.0 - ADAM_B2) * (g * g)
    m_hat = m / (1.0 - ADAM_B1 ** ADAM_STEP)
    v_hat = v / (1.0 - ADAM_B2 ** ADAM_STEP)
    return -ADAM_LR * (m_hat / (jnp.sqrt(v_hat) + ADAM_EPS) + ADAM_WD * w), m, v


def _slot_sum(ref, own=None, me=None):
    g = None
    for i in range(N_DEV):
        term = ref[i].astype(F32)
        if own is not None:
            term = jnp.where(i == me, own, term)
        g = term if g is None else g + term
    return g


def _update_matrix(name, me, landed, own, w, m, v, rows):
    r, c = w.shape

    def body(me_ref, l_ref, own_ref, w_ref, m_ref, v_ref, g_ref, d_ref, nm_ref, nv_ref):
        g = _slot_sum(l_ref, own_ref[...].astype(F32), me_ref[0])
        g_ref[...] = g
        d_ref[...], nm_ref[...], nv_ref[...] = _adamw(w_ref[...], g, m_ref[...], v_ref[...])

    blk = pl.BlockSpec((rows, c), lambda i, me_ref: (i, 0))
    return pl.pallas_call(
        body, name=name,
        grid_spec=pltpu.PrefetchScalarGridSpec(
            num_scalar_prefetch=1, grid=(r // rows,),
            in_specs=[pl.BlockSpec((N_DEV, rows, c), lambda i, me_ref: (0, i, 0)),
                      pl.BlockSpec((None, rows, c), lambda i, me_ref: (me_ref[0], i, 0)), blk, blk, blk],
            out_specs=[blk] * 4),
        out_shape=[jax.ShapeDtypeStruct((r, c), F32)] * 4,
        compiler_params=_params("parallel"),
    )(me, landed, own, w, m, v)


def _update_small(landed, lb_logits, ws, ms, vs):
    def body(l_ref, lbl_ref, w_ref, m_ref, v_ref, g_ref, d_ref, nm_ref, nv_ref, loss_ref):
        tot = _slot_sum(l_ref)
        _, dlb = _lower_bound(lbl_ref[...])
        g_lb = tot[1:2, :SEC_W] * dlb
        g = jnp.concatenate([tot[0:1], jnp.concatenate([g_lb, -g_lb], axis=-1),
                             jnp.pad(tot[1:2, SEC_W:], ((0, 0), (0, SEC_W))), tot[2:3]], axis=0)
        g_ref[...] = g
        d_ref[...], nm_ref[...], nv_ref[...] = _adamw(w_ref[...], g, m_ref[...], v_ref[...])
        loss_ref[...] = tot[3:4, 0:1]

    vmem = pl.BlockSpec(memory_space=pltpu.VMEM)
    return pl.pallas_call(
        body, name="update_small", in_specs=[vmem] * 5, out_specs=[vmem] * 5,
        out_shape=[jax.ShapeDtypeStruct((4, D_MODEL), F32)] * 4 + [jax.ShapeDtypeStruct((1, 1), F32)],
    )(landed, lb_logits, ws, ms, vs)


def _pack_small(norm_w, lb_logits, hg_norm_w, final_norm_w):
    return jnp.concatenate([norm_w, lb_logits.reshape(1, D_MODEL),
                            jnp.pad(hg_norm_w, ((0, 0), (0, D_MODEL - SEC_W))), final_norm_w[None, :]], axis=0)


def _unpack_small(a):
    return a[0:1], a[1].reshape(2, SEC_W), a[2:3, :SEC_W], a[3]


def kernel(x, norm_w, w_in, hgrn_lb_logits, hg_norm_w, w_out, final_norm_w, loss_target, m_norm_w, m_w_in, m_hgrn_lb_logits, m_hg_norm_w, m_w_out, m_final_norm_w, v_norm_w, v_w_in, v_hgrn_lb_logits, v_hg_norm_w, v_w_out, v_final_norm_w):
    w_all, wo_all = _gather_weights(w_in[0], w_out[0])
    in_flight = []

    def start_exchange(gwi, gwo):
        *handles, token = _exchange_start(gwi, gwo.reshape(N_DEV, D_MODEL // N_DEV, D_MODEL))
        in_flight.extend(handles)
        return token

    gx, _, _, small = _local_step(x[0], loss_target[0], norm_w, w_all, hgrn_lb_logits, hg_norm_w,
                                  wo_all.reshape(D_MODEL, D_MODEL), final_norm_w, start_exchange)
    ls = _gather_small(small)
    gwi, gwo, li, lo = _exchange_wait(*in_flight, gx)
    me = _me().astype(jnp.int32).reshape(1)
    g_wi, d_wi, nm_wi, nv_wi = _update_matrix("update_w_in", me, li, gwi, w_in[0], m_w_in[0], v_w_in[0], 256)
    g_wo, d_wo, nm_wo, nv_wo = _update_matrix("update_w_out", me, lo, gwo, w_out[0], m_w_out[0], v_w_out[0], 128)
    g_s, d_s, nm_s, nv_s, loss = _update_small(
        ls, hgrn_lb_logits, _pack_small(norm_w, hgrn_lb_logits, hg_norm_w, final_norm_w),
        _pack_small(m_norm_w, m_hgrn_lb_logits, m_hg_norm_w, m_final_norm_w),
        _pack_small(v_norm_w, v_hgrn_lb_logits, v_hg_norm_w, v_final_norm_w))
    outs = []
    for small_out, wi, wo in ((g_s, g_wi, g_wo), (d_s, d_wi, d_wo), (nm_s, nm_wi, nm_wo), (nv_s, nv_wi, nv_wo)):
        nw, lb, hg, fw = _unpack_small(small_out)
        outs += [nw, wi[None], lb, hg, wo[None], fw]
    return (loss[0, 0], gx[None], *outs)
```

```python
import functools

import jax
import jax.numpy as jnp
import numpy as np
from jax import lax
from jax.experimental import pallas as pl
from jax.experimental.pallas import tpu as pltpu

F32 = jnp.float32
MM = jnp.bfloat16
XCH = jnp.bfloat16
NORM_EPS = 1e-6
NEG = -1e30
N_DEV = 8
D_MODEL = 1024
N_SEC = 8
SEC_W = 512
HG_HEADS = 4
HG_D = 128
HG_GROUP = 4
AT_DH = 64
LANES = 128
ATT_BLK = 128
AT_COLS = 512
AT_QB = 8
DILATIONS = (1, 4, 16)
ROPE_THETA = 10000.0
CH = 16
LB_LO, LB_HI = 1e-6, 1.0 - 1e-6
ADAM_LR, ADAM_B1, ADAM_B2, ADAM_EPS, ADAM_WD, ADAM_STEP = 0.001, 0.9, 0.999, 1e-08, 0.01, 10
VMEM_LIMIT = 56 * 1024 * 1024
MESH = pl.DeviceIdType.MESH


def _params(*sem):
    return pltpu.CompilerParams(dimension_semantics=sem, vmem_limit_bytes=VMEM_LIMIT)


def _sigmoid(x):
    return 1.0 / (1.0 + jnp.exp(-x))


def _dot(a, b):
    return jnp.dot(a.astype(MM), b.astype(MM), preferred_element_type=F32)


def _dot_nt(a, b):
    return lax.dot_general(a.astype(MM), b.astype(MM), (((1,), (1,)), ((), ())), preferred_element_type=F32)


def _dot_tn(a, b):
    return lax.dot_general(a.astype(MM), b.astype(MM), (((0,), (0,)), ((), ())), preferred_element_type=F32)


def _split3(g):
    g1 = g.astype(jnp.bfloat16)
    r1 = g - g1.astype(F32)
    g2 = r1.astype(jnp.bfloat16)
    return g1, g2, (r1 - g2.astype(F32)).astype(jnp.bfloat16)


def _tri_dot(tri, g):
    t = tri.astype(jnp.bfloat16)
    g1, g2, g3 = _split3(g)
    d = functools.partial(jnp.dot, preferred_element_type=F32)
    return d(t, g1) + d(t, g2) + d(t, g3)


def _lower_bound(lbl):
    l0, l1 = lbl[0:1, :], lbl[1:2, :]
    m = jnp.maximum(l0, l1)
    e0, e1 = jnp.exp(l0 - m), jnp.exp(l1 - m)
    p = e0 / (e0 + e1)
    inside = (p >= LB_LO) & (p <= LB_HI)
    return jnp.clip(p, LB_LO, LB_HI), jnp.where(inside, p * (e1 / (e0 + e1)), 0.0)


def _iota2(shape, dim):
    return lax.broadcasted_iota(jnp.int32, shape, dim)


def _hgrn_gates(xq, xf, lb):
    sgq = _sigmoid(xq)
    sg = _sigmoid(xf)
    sn = _sigmoid(-xf)
    f = lb + (1.0 - lb) * sg
    return sgq, xq * sgq, sg, sn, f, (1.0 - lb) * sn


def _bdot(a, b, ca, cb):
    return lax.dot_general(a.astype(MM), b.astype(MM), (((ca,), (cb,)), ((0,), (0,))), preferred_element_type=F32)


def _chunk_masks(rb):
    row, col = _iota2((rb, rb), 0), _iota2((rb, rb), 1)
    same = (row // CH) == (col // CH)
    return same & (row >= col), same & (row <= col)


HALF = CH // 2
SLAB_ROWS = HALF * CH + (HALF // 2) * CH


def _write_slabs(slab_ref, g, q3, b3):
    slab = lambda t, rows: q3[:, rows, :] * jnp.exp(jnp.minimum(b3[:, rows, :] - b3[:, t:t + 1, :], 0.0))
    late = slice(HALF, CH)
    for t in range(HALF):
        slab_ref[g, :, t * CH:(t + 1) * CH, :] = slab(t, slice(0, CH)).astype(MM)
    for p in range(HALF // 2):
        t = HALF + 2 * p
        two = jnp.concatenate([slab(t, late), slab(t + 1, late)], axis=1)
        slab_ref[g, :, (HALF + p) * CH:(HALF + p + 1) * CH, :] = two.astype(MM)


def _read_diag(r):
    nc = r.shape[0]
    col, col_late = _iota2((nc, CH, CH), 2), _iota2((nc, HALF, CH), 2)
    a, a_late = jnp.zeros((nc, CH, CH), F32), jnp.zeros((nc, HALF, CH), F32)
    for t in range(HALF):
        a = a + jnp.where(col == t, r[:, t * CH:(t + 1) * CH, :], 0.0)
    for p in range(HALF // 2):
        t, two = HALF + 2 * p, r[:, (HALF + p) * CH:(HALF + p + 1) * CH, :]
        a_late = a_late + jnp.where(col_late == t, two[:, :HALF, :], 0.0) + jnp.where(col_late == t + 1, two[:, HALF:, :], 0.0)
    return a + jnp.concatenate([jnp.zeros_like(a_late), a_late], axis=1)


def _hgrn_fwd(proj, lb_logits, rb=256):
    s = proj.shape[1]
    nb, nc = s // rb, rb // CH

    def body(q_ref, f_ref, i_ref, lbl_ref, o_ref, sst_ref, a_ref, st_ref, slab_ref, states_ref):
        @pl.when(pl.program_id(1) == 0)
        def _():
            st_ref[...] = jnp.zeros_like(st_ref)

        sst_ref[...] = st_ref[...]
        prefix, _ = _chunk_masks(rb)
        c3 = lambda a: a.reshape(nc, CH, HG_D)
        row, col = _iota2((nc, CH, CH), 1), _iota2((nc, CH, CH), 2)
        heads = []
        for g in range(HG_GROUP):
            hs = slice(g * HG_D, (g + 1) * HG_D)
            lb, _ = _lower_bound(lbl_ref[:, hs])
            _, q, _, _, f, kk = _hgrn_gates(q_ref[:, hs], f_ref[:, hs], lb)
            b3 = c3(_tri_dot(prefix, jnp.log(f)))
            q3, kk3, v3 = c3(q), c3(kk), c3(i_ref[:, hs])
            bl3 = b3[:, CH - 1:CH, :]
            _write_slabs(slab_ref, g, q3, b3)
            x_upd = _bdot(v3, kk3 * jnp.exp(bl3 - b3), 1, 1)
            heads.append(dict(hs=hs, kk3=kk3, v3=v3, qe3=q3 * jnp.exp(b3), ebl3=jnp.exp(bl3), x_upd=x_upd))
        for g, hd in enumerate(heads):
            st = st_ref[g]
            for c in range(nc):
                states_ref[g, c] = st
                st = st * hd["ebl3"][c] + hd["x_upd"][c]
            st_ref[g] = st
        for g, hd in enumerate(heads):
            a = _read_diag(_bdot(slab_ref[g], hd["kk3"], 2, 2))
            a = jnp.where(row >= col, a, 0.0)
            a_ref[:, g * CH:(g + 1) * CH] = a.reshape(rb, CH)
            o3 = _bdot(hd["qe3"], states_ref[g], 2, 2) + _bdot(a, hd["v3"], 2, 1)
            o_ref[:, hd["hs"]] = o3.reshape(rb, HG_D)

    wide = HG_GROUP * HG_D
    sec = lambda j: pl.BlockSpec((None, rb, wide), lambda h, i, j=j: (j, i, h))
    return pl.pallas_call(
        body, name="hgrn_fwd", grid=(HG_HEADS // HG_GROUP, nb),
        in_specs=[sec(0), sec(1), sec(2), pl.BlockSpec((2, wide), lambda h, i: (0, h))],
        out_specs=[pl.BlockSpec((rb, wide), lambda h, i: (i, h)),
                   pl.BlockSpec((None, HG_GROUP, HG_D, HG_D), lambda h, i: (i, h, 0, 0)),
                   pl.BlockSpec((rb, HG_GROUP * CH), lambda h, i: (i, h))],
        out_shape=[jax.ShapeDtypeStruct((s, SEC_W), F32),
                   jax.ShapeDtypeStruct((nb, HG_HEADS, HG_D, HG_D), F32),
                   jax.ShapeDtypeStruct((s, HG_HEADS * CH), F32)],
        scratch_shapes=[pltpu.VMEM((HG_GROUP, HG_D, HG_D), F32), pltpu.VMEM((HG_GROUP, nc, SLAB_ROWS, HG_D), MM),
                        pltpu.VMEM((HG_GROUP, nc, HG_D, HG_D), F32)],
        compiler_params=_params("parallel", "arbitrary"),
    )(proj, proj, proj, lb_logits)


def _hgrn_bwd(proj, lb_logits, d_o, sst, a_in, token, rb=256):
    s = proj.shape[1]
    nb, nc = s // rb, rb // CH

    def body(q_ref, f_ref, i_ref, lbl_ref, do_ref, sst_ref, a_ref, tok_ref, dxq_ref, dxf_ref, dxi_ref, dlb_ref,
             dst_ref, states_ref, dstates_ref, lslab_ref, kslab_ref):
        @pl.when(pl.program_id(1) == 0)
        def _():
            dst_ref[...] = jnp.zeros_like(dst_ref)
            dlb_ref[...] = jnp.zeros_like(dlb_ref) + tok_ref[0:1, 0:1]

        prefix, suffix = _chunk_masks(rb)
        c3 = lambda a: a.reshape(nc, CH, HG_D)
        flat = lambda a: a.reshape(rb, HG_D)
        row, col = _iota2((nc, CH, CH), 1), _iota2((nc, CH, CH), 2)
        tril, triu = row >= col, row <= col
        sel = (_iota2((CH, CH * CH), 1) % CH == _iota2((CH, CH * CH), 0)).astype(MM)
        blockdiag = _iota2((nc, CH, CH * CH), 2) // CH == _iota2((nc, CH, CH * CH), 1)
        tile = lambda m: jnp.where(blockdiag, _dot(m.reshape(rb, CH), sel).reshape(nc, CH, CH * CH), 0.0)
        last = _iota2((nc, CH, HG_D), 1) == CH - 1
        heads = []
        for g in range(HG_GROUP):
            hs = slice(g * HG_D, (g + 1) * HG_D)
            lb, _ = _lower_bound(lbl_ref[:, hs])
            xq = q_ref[:, hs]
            sgq, q, sg, sn, f, kk = _hgrn_gates(xq, f_ref[:, hs], lb)
            b3 = c3(_tri_dot(prefix, jnp.log(f)))
            q3, kk3, v3, do3 = c3(q), c3(kk), c3(i_ref[:, hs]), c3(do_ref[:, hs])
            bl3 = b3[:, CH - 1:CH, :]
            eb3, ebl3, dec3 = jnp.exp(b3), jnp.exp(bl3), jnp.exp(bl3 - b3)
            qe3, kd3 = q3 * eb3, kk3 * dec3
            x_upd, y_upd = _bdot(v3, kd3, 1, 1), _bdot(do3, qe3, 1, 1)
            zero, every, early, late = jnp.zeros((nc, HALF, HG_D), F32), slice(0, CH), slice(0, HALF), slice(HALF, CH)
            for t in range(CH):
                bt = b3[:, t:t + 1, :]
                since = lambda rows: q3[:, rows, :] * jnp.exp(jnp.minimum(b3[:, rows, :] - bt, 0.0))
                until = lambda rows: kk3[:, rows, :] * jnp.exp(jnp.minimum(bt - b3[:, rows, :], 0.0))
                if t < HALF:
                    lv, kv = since(every), jnp.concatenate([until(early), zero], axis=1)
                else:
                    lv, kv = jnp.concatenate([zero, since(late)], axis=1), until(every)
                lslab_ref[g, :, t * CH:(t + 1) * CH, :] = lv.astype(MM)
                kslab_ref[g, :, t * CH:(t + 1) * CH, :] = kv.astype(MM)
            d_a = jnp.where(tril, _bdot(do3, v3, 2, 2), 0.0)
            d_at = jnp.where(triu, _bdot(v3, do3, 2, 2), 0.0)
            heads.append(dict(hs=hs, lb=lb, xq=xq, sgq=sgq, sg=sg, sn=sn, f=f, q3=q3, kk3=kk3, v3=v3, do3=do3,
                              eb3=eb3, ebl3=ebl3, dec3=dec3, qe3=qe3, kd3=kd3, x_upd=x_upd, y_upd=y_upd,
                              d_a=d_a, d_at=d_at))
        for g, hd in enumerate(heads):
            st = sst_ref[g]
            for c in range(nc):
                states_ref[g, c] = st
                st = st * hd["ebl3"][c] + hd["x_upd"][c]
            dst = dst_ref[g]
            for c in reversed(range(nc)):
                dstates_ref[g, c] = dst
                dst = dst * hd["ebl3"][c] + hd["y_upd"][c]
            dst_ref[g] = dst
        for g, hd in enumerate(heads):
            q3, v3, do3, kd3 = hd["q3"], hd["v3"], hd["do3"], hd["kd3"]
            states, dstates = states_ref[g], dstates_ref[g]
            hd["dqe"] = _bdot(do3, states, 2, 1)
            hd["dkd"] = _bdot(v3, dstates, 2, 1)
            a = a_ref[:, g * CH:(g + 1) * CH].reshape(nc, CH, CH)
            hd["dv"] = _bdot(kd3, dstates, 2, 2) + _bdot(a, do3, 1, 1)
            hd["dq_in"] = _bdot(tile(hd["d_a"]), kslab_ref[g], 2, 1)
            hd["dk_in"] = _bdot(tile(hd["d_at"]), lslab_ref[g], 2, 1)
            hd["ss"] = jnp.sum(dstates * states, axis=1, keepdims=True)
        for g, hd in enumerate(heads):
            q3, kk3, eb3, ebl3, dec3, qe3, kd3 = (hd[k] for k in ("q3", "kk3", "eb3", "ebl3", "dec3", "qe3", "kd3"))
            dqe, dkd, dq_in, dk_in = hd["dqe"], hd["dkd"], hd["dq_in"], hd["dk_in"]
            dkd_kd = dkd * kd3
            db = dqe * qe3 - dkd_kd + q3 * dq_in - kk3 * dk_in
            dbl = jnp.sum(dkd_kd, axis=1, keepdims=True) + hd["ss"] * ebl3
            dg = _tri_dot(suffix, flat(db + jnp.where(last, dbl, 0.0)))
            df = dg / hd["f"] - flat(dkd * dec3 + dk_in)
            xq, sgq, hs = hd["xq"], hd["sgq"], hd["hs"]
            dxq_ref[:, hs] = (flat(dqe * eb3 + dq_in) * (sgq * (1.0 + xq * (1.0 - sgq)))).astype(MM)
            dxf_ref[:, hs] = (df * (1.0 - hd["lb"]) * hd["sg"] * hd["sn"]).astype(MM)
            dxi_ref[:, hs] = flat(hd["dv"]).astype(MM)
            dlb_ref[:, hs] += jnp.sum(df * hd["sn"], axis=0, keepdims=True)

    wide = HG_GROUP * HG_D
    rev = lambda i: nb - 1 - i
    sec = lambda j: pl.BlockSpec((None, rb, wide), lambda h, i, j=j: (j, rev(i), h))
    blk = pl.BlockSpec((rb, wide), lambda h, i: (rev(i), h))
    state = (pltpu.VMEM((HG_GROUP, nc, HG_D, HG_D), F32), pltpu.VMEM((HG_GROUP, nc, CH * CH, HG_D), MM))
    return pl.pallas_call(
        body, name="hgrn_bwd", grid=(HG_HEADS // HG_GROUP, nb),
        in_specs=[sec(0), sec(1), sec(2), pl.BlockSpec((2, wide), lambda h, i: (0, h)), blk,
                  pl.BlockSpec((None, HG_GROUP, HG_D, HG_D), lambda h, i: (rev(i), h, 0, 0)),
                  pl.BlockSpec((rb, HG_GROUP * CH), lambda h, i: (rev(i), h)),
                  pl.BlockSpec((8, 128), lambda h, i: (0, 0))],
        out_specs=[blk, blk, blk, pl.BlockSpec((1, wide), lambda h, i: (0, h))],
        out_shape=[jax.ShapeDtypeStruct((s, SEC_W), MM)] * 3 + [jax.ShapeDtypeStruct((1, SEC_W), F32)],
        scratch_shapes=[pltpu.VMEM((HG_GROUP, HG_D, HG_D), F32), state[0], state[0], state[1], state[1]],
        compiler_params=_params("parallel", "arbitrary"),
    )(proj, proj, proj, lb_logits, d_o, sst, a_in, token)


def _rope_tables(s):
    half = AT_DH // 2
    inv_freq = np.float32(1.0) / (np.float32(ROPE_THETA) ** (np.arange(half, dtype=np.float32) / np.float32(half)))
    ang = np.arange(s, dtype=np.float32)[:, None] * inv_freq[None, :]
    cos, sin = np.cos(ang), np.sin(ang)
    return np.concatenate([cos] * 4, axis=-1), np.concatenate([-sin, sin] * 2, axis=-1)


def _rope128(x, cos, sin):
    lo = (_iota2(x.shape, 1) % AT_DH) < AT_DH // 2
    rot = jnp.where(lo, pltpu.roll(x, LANES - AT_DH // 2, 1), pltpu.roll(x, AT_DH // 2, 1))
    return x * cos + rot * sin


LANE_GROUPS = SEC_W // LANES


def _set_lanes(ref, val):
    for j in range(LANE_GROUPS):
        ref[j] = val[:, j * LANES:(j + 1) * LANES]


def _get_lanes(ref):
    return jnp.concatenate([ref[j] for j in range(LANE_GROUPS)], axis=-1)


def _to_view(src_ref, dst_ref, d):
    n = src_ref.shape[1] // d
    for r in range(d):
        rows = pl.ds(r, n, stride=d) if d > 1 else slice(None)
        for j in range(LANE_GROUPS):
            c0 = r * SEC_W + j * LANES
            dst_ref[:, c0:c0 + LANES] = src_ref.at[j][rows, :].astype(dst_ref.dtype)


def _from_view(src_ref, dst_ref, d):
    n = dst_ref.shape[1] // d
    for r in range(d):
        for j in range(LANE_GROUPS):
            c0 = r * SEC_W + j * LANES
            dst_ref.at[j][pl.ds(r, n, stride=d), :] = src_ref[:, c0:c0 + LANES].astype(dst_ref.dtype)


def _view_spec(tm, d):
    return pl.BlockSpec((tm // d, d * SEC_W), lambda i: (i, 0))


def _view_shape(s, d, dtype):
    return jax.ShapeDtypeStruct((s // d, d * SEC_W), dtype)


PROJ_KEPT = (0, 1, 2, 3, 7)


def _inproj_fwd(x, norm_w, w_all, cos, sin, tm=512):
    s = x.shape[0]

    def body(x_ref, nw_ref, w_ref, cos_ref, sin_ref, proj_ref, *refs):
        outs, (qs_ref, ks_ref, vs_ref) = refs[:-3], refs[-3:]
        xv = x_ref[...]
        rstd = lax.rsqrt(jnp.mean(xv * xv, axis=-1, keepdims=True) + NORM_EPS)
        u = (xv * rstd * nw_ref[...]).astype(MM)
        for slot, j in enumerate(PROJ_KEPT):
            proj_ref[slot] = jnp.dot(u, w_ref[j], preferred_element_type=F32)
        q, k, v = [jnp.dot(u, w_ref[j], preferred_element_type=F32) for j in (4, 5, 6)]
        c, sn = cos_ref[...], sin_ref[...]
        for g in range(LANE_GROUPS):
            sl = slice(g * LANES, (g + 1) * LANES)
            qs_ref[g] = _rope128(q[:, sl], c, sn) * (AT_DH ** -0.5)
            ks_ref[g] = _rope128(k[:, sl], c, sn)
            vs_ref[g] = v[:, sl]
        for i, d in enumerate(DILATIONS):
            for src_ref, dst_ref in zip((qs_ref, ks_ref, vs_ref), outs[3 * i:3 * i + 3]):
                _to_view(src_ref, dst_ref, d)

    tab = pl.BlockSpec((tm, LANES), lambda i: (i, 0))
    return pl.pallas_call(
        body, name="inproj_fwd", grid=(s // tm,),
        in_specs=[pl.BlockSpec((tm, D_MODEL), lambda i: (i, 0)),
                  pl.BlockSpec((1, D_MODEL), lambda i: (0, 0)),
                  pl.BlockSpec((N_SEC, D_MODEL, SEC_W), lambda i: (0, 0, 0)), tab, tab],
        out_specs=[pl.BlockSpec((len(PROJ_KEPT), tm, SEC_W), lambda i: (0, i, 0))]
                  + [_view_spec(tm, d) for d in DILATIONS for _ in range(3)],
        out_shape=[jax.ShapeDtypeStruct((len(PROJ_KEPT), s, SEC_W), F32)]
                  + [_view_shape(s, d, MM) for d in DILATIONS for _ in range(3)],
        scratch_shapes=[pltpu.VMEM((LANE_GROUPS, tm, LANES), F32)] * 3,
        compiler_params=_params("parallel"),
    )(x, norm_w, w_all, cos, sin)


def _band_mask(first_ok, second_ok):
    row, col = _iota2((ATT_BLK, 2 * ATT_BLK), 0), _iota2((ATT_BLK, 2 * ATT_BLK), 1)
    return ((col < ATT_BLK) & (col >= row) & first_ok) | ((col >= ATT_BLK) & ((col - ATT_BLK) <= row) & second_ok)


def _own_lanes(rows, h):
    lane = _iota2((rows, LANES), 1)
    return (lane < AT_DH) if h == 0 else (lane >= AT_DH)


def _neg_pieces(rows, h):
    lane = _iota2((rows, LANES), 1) - (AT_DH if h == 0 else 0)
    return jnp.where((lane >= 0) & (lane < 3), -1.0, 0.0).astype(MM)


def _units(qb):
    return [(b, slice(g * LANES, (g + 1) * LANES), h) for b in range(qb) for g in range(AT_COLS // LANES) for h in range(2)]


def _sub(b):
    return slice(b * ATT_BLK, (b + 1) * ATT_BLK)


def _band_before(cur_ref, prev_ref, b, sl):
    if b == 0:
        return jnp.concatenate([prev_ref[:, sl], cur_ref[0:ATT_BLK, sl]], axis=0)
    return cur_ref[(b - 1) * ATT_BLK:(b + 1) * ATT_BLK, sl]


def _band_after(cur_ref, next_ref, b, sl):
    if (b + 1) * ATT_BLK == cur_ref.shape[0]:
        return jnp.concatenate([cur_ref[b * ATT_BLK:(b + 1) * ATT_BLK, sl], next_ref[:, sl]], axis=0)
    return cur_ref[b * ATT_BLK:(b + 2) * ATT_BLK, sl]


def _attn_specs(rows):
    qb = min(AT_QB, rows // ATT_BLK)
    assert rows % (qb * ATT_BLK) == 0
    last = rows // ATT_BLK - 1
    cur = pl.BlockSpec((qb * ATT_BLK, AT_COLS), lambda c, n: (n, c))
    prev = pl.BlockSpec((ATT_BLK, AT_COLS), lambda c, n: (jnp.maximum(qb * n - 1, 0), c))
    nxt = pl.BlockSpec((ATT_BLK, AT_COLS), lambda c, n: (jnp.minimum(qb * (n + 1), last), c))
    return qb, cur, prev, nxt


def _stack_heads(a):
    h0 = _own_lanes(a.shape[0], 0)
    zero = jnp.zeros_like(a)
    return jnp.concatenate([jnp.where(h0, a, zero), jnp.where(h0, zero, a)], axis=0)


def _unstack_heads(a2):
    return jnp.where(_own_lanes(ATT_BLK, 0), a2[:ATT_BLK], a2[ATT_BLK:])


def _attn_fwd(qr, kr, vr, d):
    rows, cols = qr.shape
    qb, cur, prev, nxt = _attn_specs(rows)
    nb = rows // (qb * ATT_BLK)

    def body(q_ref, kc_ref, kp_ref, vc_ref, vp_ref, o_ref, lse_ref):
        twice = lambda m: jnp.concatenate([m, m], axis=0)
        masks = {True: twice(_band_mask(pl.program_id(1) > 0, True)), False: twice(_band_mask(True, True))}
        ones = jnp.ones((2 * ATT_BLK, LANES), MM)
        units = [(b, sl) for b, sl, h in _units(qb) if h == 0]
        scs = [jnp.where(masks[b == 0], _dot_nt(_stack_heads(q_ref[_sub(b), sl]), _band_before(kc_ref, kp_ref, b, sl)),
                         NEG) for b, sl in units]
        ms = [jnp.max(sc, axis=-1, keepdims=True) for sc in scs]
        ps = [jnp.exp(sc - m).astype(MM) for sc, m in zip(scs, ms)]
        ols = [jnp.dot(p, jnp.concatenate([_band_before(vc_ref, vp_ref, b, sl), ones], axis=1),
                       preferred_element_type=F32) for p, (b, sl) in zip(ps, units)]
        for (b, sl), m, ol in zip(units, ms, ols):
            l = _unstack_heads(ol[:, LANES:])
            o_ref[_sub(b), sl] = _unstack_heads(ol[:, :LANES]) / l
            lse_ref[_sub(b), sl] = _unstack_heads(jnp.broadcast_to(m, (2 * ATT_BLK, LANES))) + jnp.log(l)

    o, lse = pl.pallas_call(
        body, name=f"attn_fwd_d{d}", grid=(cols // AT_COLS, nb),
        in_specs=[cur, cur, prev, cur, prev], out_specs=[cur, cur],
        out_shape=[jax.ShapeDtypeStruct((rows, cols), F32)] * 2,
        compiler_params=_params("parallel", "parallel"),
    )(qr, kr, kr, vr, vr)
    return o, lse


def _attn_bwd_dq(qr, kr, vr, do, lse, delta, d):
    rows, cols = qr.shape
    qb, cur, prev, nxt = _attn_specs(rows)
    nb = rows // (qb * ATT_BLK)

    def body(q_ref, kc_ref, kp_ref, vc_ref, vp_ref, do_ref, lse_ref, dl_ref, dq_ref):
        masks = {True: _band_mask(pl.program_id(1) > 0, True), False: _band_mask(True, True)}
        units = _units(qb)
        sms, dps = [], []
        for b, sl, h in units:
            own, own_b, neg = _own_lanes(ATT_BLK, h), _own_lanes(2 * ATT_BLK, h), _neg_pieces(2 * ATT_BLK, h)
            sms.append(_dot_nt(jnp.where(own, q_ref[_sub(b), sl], lse_ref[_sub(b), sl]),
                               jnp.where(own_b, _band_before(kc_ref, kp_ref, b, sl), neg)))
            dps.append(_dot_nt(jnp.where(own, do_ref[_sub(b), sl], dl_ref[_sub(b), sl]),
                               jnp.where(own_b, _band_before(vc_ref, vp_ref, b, sl), neg)))
        dss = [(jnp.exp(jnp.where(masks[b == 0], sm, NEG)) * dp).astype(MM)
               for sm, dp, (b, _, _) in zip(sms, dps, units)]
        dqs = [jnp.dot(ds, _band_before(kc_ref, kp_ref, b, sl), preferred_element_type=F32) * (AT_DH ** -0.5)
               for ds, (b, sl, _) in zip(dss, units)]
        for i in range(0, len(units), 2):
            b, sl, _ = units[i]
            dq_ref[_sub(b), sl] = jnp.where(_own_lanes(ATT_BLK, 0), dqs[i], dqs[i + 1]).astype(dq_ref.dtype)

    dq = pl.pallas_call(
        body, name=f"attn_bwd_dq_d{d}", grid=(cols // AT_COLS, nb),
        in_specs=[cur, cur, prev, cur, prev, cur, cur, cur], out_specs=cur,
        out_shape=jax.ShapeDtypeStruct((rows, cols), MM),
        compiler_params=_params("parallel", "parallel"),
    )(qr, kr, kr, vr, vr, do, lse, delta)
    return dq


def _attn_bwd_dkv(qr, kr, vr, do, lse, delta, d):
    rows, cols = qr.shape
    qb, cur, prev, nxt = _attn_specs(rows)
    nb = rows // (qb * ATT_BLK)

    def body(k_ref, v_ref, qc_ref, qn_ref, doc_ref, don_ref, lsec_ref, lsen_ref, dlc_ref, dln_ref,
             dk_ref, dv_ref):
        masks = {True: _band_mask(True, pl.program_id(1) < nb - 1), False: _band_mask(True, True)}
        units = _units(qb)
        sms, dps = [], []
        for b, sl, h in units:
            own, own_b, neg = _own_lanes(ATT_BLK, h), _own_lanes(2 * ATT_BLK, h), _neg_pieces(ATT_BLK, h)
            sms.append(_dot_nt(jnp.where(own, k_ref[_sub(b), sl], neg),
                               jnp.where(own_b, _band_after(qc_ref, qn_ref, b, sl),
                                         _band_after(lsec_ref, lsen_ref, b, sl))))
            dps.append(_dot_nt(jnp.where(own, v_ref[_sub(b), sl], neg),
                               jnp.where(own_b, _band_after(doc_ref, don_ref, b, sl),
                                         _band_after(dlc_ref, dln_ref, b, sl))))
        ps = [jnp.exp(jnp.where(masks[b == qb - 1], sm, NEG)) for sm, (b, _, _) in zip(sms, units)]
        dss = [(p * dp).astype(MM) for p, dp in zip(ps, dps)]
        dvs = [jnp.dot(p.astype(MM), _band_after(doc_ref, don_ref, b, sl), preferred_element_type=F32)
               for p, (b, sl, _) in zip(ps, units)]
        dks = [jnp.dot(ds, _band_after(qc_ref, qn_ref, b, sl), preferred_element_type=F32)
               for ds, (b, sl, _) in zip(dss, units)]
        head0 = _own_lanes(ATT_BLK, 0)
        for i in range(0, len(units), 2):
            b, sl, _ = units[i]
            dk_ref[_sub(b), sl] = jnp.where(head0, dks[i], dks[i + 1]).astype(dk_ref.dtype)
            dv_ref[_sub(b), sl] = jnp.where(head0, dvs[i], dvs[i + 1]).astype(dv_ref.dtype)

    dk, dv = pl.pallas_call(
        body, name=f"attn_bwd_dkv_d{d}", grid=(cols // AT_COLS, nb),
        in_specs=[cur, cur, cur, nxt, cur, nxt, cur, nxt, cur, nxt], out_specs=[cur, cur],
        out_shape=[jax.ShapeDtypeStruct((rows, cols), MM)] * 2,
        compiler_params=_params("parallel", "parallel"),
    )(kr, vr, qr, qr, do, do, lse, lse, delta, delta)
    return dk, dv


def _head_sum(a, width):
    parts = []
    for j in range(a.shape[1] // width):
        sm = jnp.sum(a[:, j * width:(j + 1) * width], axis=-1, keepdims=True)
        parts.append(jnp.broadcast_to(sm, (a.shape[0], width)))
    return jnp.concatenate(parts, axis=-1)


def _partner_sum(a):
    swap = (_iota2((LANES, LANES), 0) // AT_DH != _iota2((LANES, LANES), 1) // AT_DH).astype(jnp.bfloat16)
    d = functools.partial(jnp.dot, preferred_element_type=F32)
    parts = _split3(a)
    return jnp.concatenate([d(parts[0][:, sl], swap) + d(parts[1][:, sl], swap) + d(parts[2][:, sl], swap)
                            for sl in (slice(j * LANES, (j + 1) * LANES) for j in range(a.shape[1] // LANES))], axis=-1)


def _partner_value(x):
    return jnp.concatenate([pltpu.roll(x[:, j * LANES:(j + 1) * LANES], AT_DH, 1) for j in range(x.shape[1] // LANES)],
                           axis=-1)


def _pieces(xs):
    hi = xs.astype(jnp.bfloat16).astype(F32)
    mid = (xs - hi).astype(jnp.bfloat16).astype(F32)
    lo = (xs - hi - mid).astype(jnp.bfloat16).astype(F32)
    lane = _iota2(xs.shape, 1) % AT_DH
    return jnp.where(lane == 0, hi, jnp.where(lane == 1, mid, jnp.where(lane == 2, lo, 0.0)))


def _mid(x, tgt, proj, o_hg, o_at, lse_at, hg_norm_w, final_norm_w, wo_all, tm=256):
    s = x.shape[0]
    nb = s // tm

    def body(x_ref, t_ref, hgz_ref, atz_ref, ohg_ref, o1_ref, o2_ref, o3_ref, l1_ref, l2_ref, l3_ref,
             g_ref, fw_ref, wo_ref,
             dh_ref, dohg_ref, dhgz_ref, datz_ref, do1_ref, do2_ref, do3_ref, dl1_ref, dl2_ref, dl3_ref,
             lp1_ref, lp2_ref, lp3_ref,
             gwo_ref, gfw_ref, ghg_ref, loss_ref, nat_ref, stage_ref, gwo_acc):
        @pl.when(pl.program_id(0) == 0)
        def _():
            gwo_acc[...] = jnp.zeros_like(gwo_acc)
            gfw_ref[...] = jnp.zeros_like(gfw_ref)
            ghg_ref[...] = jnp.zeros_like(ghg_ref)
            loss_ref[...] = jnp.zeros_like(loss_ref)

        ohg, g = ohg_ref[...], g_ref[...]
        rs = lax.rsqrt(_head_sum(ohg * ohg, HG_D) * (1.0 / HG_D) + NORM_EPS)
        on = ohg * rs
        hgz = hgz_ref[...]
        sz = _sigmoid(hgz)
        gate_hg = hgz * sz
        lses, outs = [l1_ref[...]], [o1_ref[...]]
        for k, (d, l_ref, o_ref) in enumerate(zip(DILATIONS[1:], (l2_ref, l3_ref), (o2_ref, o3_ref))):
            _from_view(l_ref, nat_ref.at[2 * k], d)
            _from_view(o_ref, nat_ref.at[2 * k + 1], d)
            lses.append(_get_lanes(nat_ref.at[2 * k]))
            outs.append(_get_lanes(nat_ref.at[2 * k + 1]))
        mx = jnp.maximum(jnp.maximum(lses[0], lses[1]), lses[2])
        es = [jnp.exp(l - mx) for l in lses]
        den = es[0] + es[1] + es[2]
        ws = [e / den for e in es]
        oat = ws[0] * outs[0] + ws[1] * outs[1] + ws[2] * outs[2]
        atz = atz_ref[...]
        sa = _sigmoid(atz)
        gate_at = atz * sa
        mixed = jnp.concatenate([on * g * gate_hg, oat * gate_at], axis=-1).astype(MM)
        h = x_ref[...] + jnp.dot(mixed, wo_ref[...], preferred_element_type=F32)
        rstd = lax.rsqrt(jnp.mean(h * h, axis=-1, keepdims=True) + NORM_EPS)
        hn = h * rstd
        fw = fw_ref[...]
        err = hn * fw - t_ref[...]
        loss_ref[...] += 0.5 * jnp.sum(jnp.mean(err * err, axis=-1, keepdims=True), axis=0, keepdims=True)
        dout = err * (1.0 / D_MODEL)
        gfw_ref[...] += jnp.sum(dout * hn, axis=0, keepdims=True)
        dhn = dout * fw
        dh = rstd * (dhn - hn * jnp.mean(dhn * hn, axis=-1, keepdims=True))
        dh_ref[...] = dh
        dh_mm = dh.astype(MM)
        gwo_acc[...] += _dot_tn(mixed, dh_mm)

        @pl.when(pl.program_id(0) == nb - 1)
        def _():
            gwo_ref[...] = gwo_acc[...].astype(gwo_ref.dtype)

        dmixed = _dot_nt(dh_mm, wo_ref[...])
        dm_hg = dmixed[:, :SEC_W]
        d_ong = dm_hg * gate_hg
        dhgz_ref[...] = (dm_hg * (on * g) * (sz * (1.0 + hgz * (1.0 - sz)))).astype(MM)
        ghg_ref[...] += jnp.sum(d_ong * on, axis=0, keepdims=True)
        d_on = d_ong * g
        dohg_ref[...] = rs * (d_on - on * (_head_sum(d_on * on, HG_D) * (1.0 / HG_D)))
        dm_at = dmixed[:, SEC_W:]
        d_oat = dm_at * gate_at
        datz_ref[...] = (dm_at * oat * (sa * (1.0 + atz * (1.0 - sa)))).astype(MM)
        lse_all = mx + jnp.log(den)
        for val, dst_refs in ((d_oat, (do1_ref, do2_ref, do3_ref)),
                              (_pieces(_partner_sum(d_oat * oat)), (dl1_ref, dl2_ref, dl3_ref)),
                              (_pieces(_partner_value(lse_all)), (lp1_ref, lp2_ref, lp3_ref))):
            _set_lanes(stage_ref, val)
            for d, dst_ref in zip(DILATIONS, dst_refs):
                _to_view(stage_ref, dst_ref, d)

    row = lambda w: pl.BlockSpec((tm, w), lambda i: (i, 0))
    sec = lambda j: pl.BlockSpec((None, tm, SEC_W), lambda i, j=j: (j, i, 0))
    const = lambda shp: pl.BlockSpec(shp, lambda i: (0,) * len(shp))
    half = row(SEC_W)
    views = [_view_spec(tm, d) for d in DILATIONS]
    return pl.pallas_call(
        body, name="mid", grid=(nb,),
        in_specs=[row(D_MODEL), row(D_MODEL), sec(PROJ_KEPT.index(3)), sec(PROJ_KEPT.index(7)), half] + views * 2
                 + [const((1, SEC_W)), const((1, D_MODEL)), const((D_MODEL, D_MODEL))],
        out_specs=[row(D_MODEL)] + [half] * 3 + views * 3
                  + [const((D_MODEL, D_MODEL)), const((1, D_MODEL)), const((1, SEC_W)), const((1, 1))],
        out_shape=[jax.ShapeDtypeStruct((s, D_MODEL), F32), jax.ShapeDtypeStruct((s, SEC_W), F32)]
                  + [jax.ShapeDtypeStruct((s, SEC_W), MM)] * 2
                  + [_view_shape(s, d, MM) for d in DILATIONS] * 3
                  + [jax.ShapeDtypeStruct((D_MODEL, D_MODEL), XCH), jax.ShapeDtypeStruct((1, D_MODEL), F32),
                     jax.ShapeDtypeStruct((1, SEC_W), F32), jax.ShapeDtypeStruct((1, 1), F32)],
        scratch_shapes=[pltpu.VMEM((4, LANE_GROUPS, tm, LANES), F32), pltpu.VMEM((LANE_GROUPS, tm, LANES), F32),
                        pltpu.VMEM((D_MODEL, D_MODEL), F32)],
        compiler_params=_params("arbitrary"),
    )(x, tgt, proj, proj, o_hg, *o_at, *lse_at, hg_norm_w, final_norm_w, wo_all)


def _section_specs(dsecs, tm):
    return [pl.BlockSpec((tm, SEC_W), lambda i: (i, 0)) if k is None
            else pl.BlockSpec((None, tm, SEC_W), lambda i, k=k: (k, i, 0)) for _, k in dsecs]


def _inproj_bwd_x(x, norm_w, w_all, dh, dsecs, token, tm=512):
    s = x.shape[0]

    def body(x_ref, nw_ref, w_ref, dh_ref, tok_ref, *refs):
        sec_refs, (gx_ref, gnw_ref) = refs[:N_SEC], refs[N_SEC:]

        @pl.when(pl.program_id(0) == 0)
        def _():
            gnw_ref[...] = jnp.zeros_like(gnw_ref)

        du = jnp.zeros((tm, D_MODEL), F32)
        for j in range(N_SEC):
            du = du + _dot_nt(sec_refs[j][...], w_ref[j])
        xv, nw = x_ref[...], nw_ref[...]
        rstd = lax.rsqrt(jnp.mean(xv * xv, axis=-1, keepdims=True) + NORM_EPS)
        xn = xv * rstd
        gnw_ref[...] += jnp.sum(du * xn, axis=0, keepdims=True)
        dxn = du * nw
        dx = rstd * (dxn - xn * jnp.mean(dxn * xn, axis=-1, keepdims=True))
        gx_ref[...] = (dh_ref[...] + tok_ref[0:1, 0:1]) + dx

    row = lambda w: pl.BlockSpec((tm, w), lambda i: (i, 0))
    const = lambda shp: pl.BlockSpec(shp, lambda i: (0,) * len(shp))
    return pl.pallas_call(
        body, name="inproj_bwd_x", grid=(s // tm,),
        in_specs=[row(D_MODEL), const((1, D_MODEL)), const((N_SEC, D_MODEL, SEC_W)), row(D_MODEL), const((8, 128))]
                 + _section_specs(dsecs, tm),
        out_specs=[row(D_MODEL), const((1, D_MODEL))],
        out_shape=[jax.ShapeDtypeStruct((s, D_MODEL), F32), jax.ShapeDtypeStruct((1, D_MODEL), F32)],
        compiler_params=_params("arbitrary"),
    )(x, norm_w, w_all, dh, token, *[a for a, _ in dsecs])


def _inproj_bwd_w(x, norm_w, dsec, dq_r, dk_r, dv, cos, sin, tm=512):
    s = x.shape[0]
    nb = s // tm

    def body(x_ref, nw_ref, s0, s1, s2, s3, s7, q1, q2, q3, k1, k2, k3, v1, v2, v3, cos_ref, sin_ref,
             gw_hbm, datt_ref, acc_ref, stage_ref, nat_ref):
        @pl.when(pl.program_id(0) == 0)
        def _():
            acc_ref[...] = jnp.zeros_like(acc_ref)

        def total(refs):
            acc = refs[0][...].astype(F32)
            for d, ref in zip(DILATIONS[1:], refs[1:]):
                _from_view(ref, nat_ref, d)
                acc = acc + _get_lanes(nat_ref)
            return acc

        c, sn = cos_ref[...], -sin_ref[...]
        unrot = lambda a: jnp.concatenate(
            [_rope128(a[:, j * LANES:(j + 1) * LANES], c, sn) for j in range(LANE_GROUPS)], axis=-1)
        att = [a.astype(MM) for a in (unrot(total((q1, q2, q3))), unrot(total((k1, k2, k3))), total((v1, v2, v3)))]
        for j, a in enumerate(att):
            datt_ref[j] = a
        xv = x_ref[...]
        rstd = lax.rsqrt(jnp.mean(xv * xv, axis=-1, keepdims=True) + NORM_EPS)
        u_t = (xv * rstd * nw_ref[...]).T.astype(MM)
        for j, dsj in enumerate((s0[...], s1[...], s2[...], s3[...], *att, s7[...])):
            acc_ref[j] += jnp.dot(u_t, dsj, preferred_element_type=F32)

        @pl.when(pl.program_id(0) == nb - 1)
        def _():
            for j in range(N_SEC):
                stage_ref[...] = acc_ref[j].astype(stage_ref.dtype)
                pltpu.sync_copy(stage_ref, gw_hbm.at[j])

    row = lambda w: pl.BlockSpec((tm, w), lambda i: (i, 0))
    return pl.pallas_call(
        body, name="inproj_bwd_w", grid=(nb,),
        in_specs=[row(D_MODEL), pl.BlockSpec((1, D_MODEL), lambda i: (0, 0))] + [row(SEC_W)] * 5
                 + [_view_spec(tm, d) for d in DILATIONS] * 3 + [row(LANES), row(LANES)],
        out_specs=[pl.BlockSpec(memory_space=pl.ANY), pl.BlockSpec((3, tm, SEC_W), lambda i: (0, i, 0))],
        out_shape=[jax.ShapeDtypeStruct((N_SEC, D_MODEL, SEC_W), XCH), jax.ShapeDtypeStruct((3, s, SEC_W), MM)],
        scratch_shapes=[pltpu.VMEM((N_SEC, D_MODEL, SEC_W), F32), pltpu.VMEM((D_MODEL, SEC_W), XCH),
                        pltpu.VMEM((LANE_GROUPS, tm, LANES), F32)],
        compiler_params=_params("arbitrary"),
    )(x, norm_w, *dsec, *dq_r, *dk_r, *dv, cos, sin)


def _local_step(x, tgt, norm_w, w_all, lb_logits, hg_norm_w, wo_all, final_norm_w, on_w_out_grad, on_w_in_grad):
    s = x.shape[0]
    cos, sin = _rope_tables(s)
    proj, *qkv = _inproj_fwd(x, norm_w, w_all, cos, sin)
    o_hg, sst, a_hg = _hgrn_fwd(proj, lb_logits)
    qkv = [qkv[3 * i:3 * i + 3] for i in range(len(DILATIONS))]
    att = [_attn_fwd(*qkv_d, d) for qkv_d, d in zip(qkv, DILATIONS)]
    (dh, d_ohg, d_hgz, d_atz, do1, do2, do3, dl1, dl2, dl3, lp1, lp2, lp3, gwo, gfw, ghg, loss) = _mid(
        x, tgt, proj, o_hg, [a[0] for a in att], [a[1] for a in att], hg_norm_w, final_norm_w[None, :], wo_all)
    dxq, dxf, dxi, dlb = _hgrn_bwd(proj, lb_logits, d_ohg, sst, a_hg, on_w_out_grad(gwo))
    dq_r, dk_r, dv = [], [], []
    for d, qkv_d, do, lp, dl in zip(DILATIONS, qkv, (do1, do2, do3), (lp1, lp2, lp3), (dl1, dl2, dl3)):
        dq_r.append(_attn_bwd_dq(*qkv_d, do, lp, dl, d))
        dk_d, dv_d = _attn_bwd_dkv(*qkv_d, do, lp, dl, d)
        dk_r.append(dk_d)
        dv.append(dv_d)
    gwi, d_att = _inproj_bwd_w(x, norm_w, (dxq, dxf, dxi, d_hgz, d_atz), dq_r, dk_r, dv, cos, sin)
    dsecs = [(dxq, None), (dxf, None), (dxi, None), (d_hgz, None), (d_att, 0), (d_att, 1), (d_att, 2), (d_atz, None)]
    token = on_w_in_grad(gwi)
    gx, gnw = _inproj_bwd_x(x, norm_w, w_all, dh, dsecs, token)
    small = jnp.concatenate([gnw, jnp.concatenate([dlb, ghg], axis=-1), gfw,
                             jnp.pad(loss, ((0, 0), (0, D_MODEL - 1)))], axis=0)
    return gx, small


def _coords():
    return lax.axis_index("x"), lax.axis_index("y"), lax.axis_index("c")


def _gather_weights(w_in, w_out):
    wo_rows = w_out.shape[0]

    def body(wi_ref, wo_ref, wi_all, wo_all, send_sems, recv_sems):
        x, y, c = _coords()
        me, sibling = (x, y, c), (x, y, 1 - c)
        chips = [(1 - x, y), (x, 1 - y), (1 - x, 1 - y)]
        slot = lambda p: 4 * p[0] + 2 * p[1] + p[2]

        def copies(k, block, to):
            return [pltpu.make_async_remote_copy(
                src_ref=ref.at[slot(block)], dst_ref=ref.at[slot(block)], send_sem=send_sems.at[a, k],
                recv_sem=recv_sems.at[a, k], device_id=to, device_id_type=MESH)
                for a, ref in enumerate((wi_all, wo_all))]

        wi_all[slot(me)] = wi_ref[...].astype(MM)
        wo_all[slot(me)] = wo_ref[...].astype(MM)
        first = copies(0, me, sibling)
        for j, chip in enumerate(chips):
            first += copies(1 + j, me, (*chip, c))
        for cp in first:
            cp.start()
        passed = []
        for j, chip in enumerate(chips):
            for cp in copies(1 + j, (*chip, c), me):
                cp.wait_recv()
            fwd = copies(4 + j, (*chip, c), sibling)
            for cp in fwd:
                cp.start()
            passed += fwd
        for cp in copies(0, sibling, me):
            cp.wait_recv()
        for j, chip in enumerate(chips):
            for cp in copies(4 + j, (*chip, 1 - c), me):
                cp.wait_recv()
        for cp in first + passed:
            cp.wait_send()

    vmem = pl.BlockSpec(memory_space=pltpu.VMEM)
    return pl.pallas_call(
        body, name="gather_weights",
        in_specs=[vmem, vmem], out_specs=[vmem, vmem],
        out_shape=[jax.ShapeDtypeStruct((N_DEV, D_MODEL, SEC_W), MM),
                   jax.ShapeDtypeStruct((N_DEV, wo_rows, D_MODEL), MM)],
        scratch_shapes=[pltpu.SemaphoreType.DMA((2, 7)), pltpu.SemaphoreType.DMA((2, 7))],
        compiler_params=pltpu.CompilerParams(vmem_limit_bytes=VMEM_LIMIT),
    )(w_in, w_out)


def _me():
    x, y, c = _coords()
    return 4 * x + 2 * y + c


def _grad_copies(srcs, lands, send_sems, recv_sems):
    x, y, c = _coords()
    me = 4 * x + 2 * y + c
    copies = []
    for k in range(1, N_DEV):
        px, py, pc = x ^ (k >> 2), y ^ ((k >> 1) & 1), c ^ (k & 1)
        peer = 4 * px + 2 * py + pc
        for a, (src, dst) in enumerate(zip(srcs, lands)):
            copies.append(pltpu.make_async_remote_copy(
                src_ref=src.at[peer], dst_ref=dst.at[me], send_sem=send_sems.at[a * (N_DEV - 1) + k - 1],
                recv_sem=recv_sems.at[a * (N_DEV - 1) + k - 1], device_id=(px, py, pc), device_id_type=MESH))
    return copies


HBM_SPEC = pl.BlockSpec(memory_space=pltpu.HBM)
SEM_SPEC = pl.BlockSpec(memory_space=pltpu.SEMAPHORE)
SPLIT_COPY_EFFECT = pltpu.SideEffectType.DATAFLOW_SIDE_EFFECTING


def _exchange_start(name, *arrs):
    n = len(arrs)

    def body(*refs):
        for cp in _grad_copies(refs[:n], refs[n:2 * n], refs[2 * n], refs[2 * n + 1]):
            cp.start()
        refs[-1][...] = jnp.zeros_like(refs[-1])

    hbm = lambda a: pltpu.with_memory_space_constraint(a, pltpu.HBM)
    bufs = (*arrs, *[lax.empty(a.shape, a.dtype) for a in arrs])
    return pl.pallas_call(
        body, name=name,
        out_shape=(pltpu.SemaphoreType.DMA((n * (N_DEV - 1),)), pltpu.SemaphoreType.DMA((n * (N_DEV - 1),)),
                   *[pltpu.HBM(a.shape, a.dtype) for a in bufs], jax.ShapeDtypeStruct((8, 128), F32)),
        in_specs=[HBM_SPEC] * (2 * n),
        out_specs=(SEM_SPEC, SEM_SPEC, *[HBM_SPEC] * (2 * n), pl.BlockSpec(memory_space=pltpu.VMEM)),
        input_output_aliases={i: i + 2 for i in range(2 * n)},
        compiler_params=pltpu.CompilerParams(has_side_effects=SPLIT_COPY_EFFECT),
    )(*[hbm(a) for a in bufs])


def _exchange_wait(name, send_sems, recv_sems, *bufs_after):
    *bufs, after = bufs_after
    n = len(bufs) // 2

    def body(*refs):
        for cp in _grad_copies(refs[:n], refs[n:2 * n], refs[2 * n], refs[2 * n + 1]):
            cp.wait_send()
            cp.wait_recv()

    return pl.pallas_call(
        body, name=name,
        out_shape=tuple(pltpu.HBM(a.shape, a.dtype) for a in bufs),
        in_specs=[HBM_SPEC] * (2 * n) + [SEM_SPEC, SEM_SPEC, pl.BlockSpec(memory_space=pl.ANY)],
        out_specs=(HBM_SPEC,) * (2 * n),
        input_output_aliases={i: i for i in range(2 * n)},
        compiler_params=pltpu.CompilerParams(has_side_effects=SPLIT_COPY_EFFECT),
    )(*bufs, send_sems, recv_sems, after)


def _gather_small(small):
    def body(sm_ref, ls_ref, send_sems, recv_sems, local_sem):
        x, y, c = _coords()
        me = 4 * x + 2 * y + c
        own = pltpu.make_async_copy(sm_ref, ls_ref.at[me], local_sem)
        own.start()
        sends = []
        for k in range(1, N_DEV):
            peer = (x ^ (k >> 2), y ^ ((k >> 1) & 1), c ^ (k & 1))
            sends.append(pltpu.make_async_remote_copy(
                src_ref=sm_ref, dst_ref=ls_ref.at[me], send_sem=send_sems.at[k - 1], recv_sem=recv_sems.at[k - 1],
                device_id=peer, device_id_type=MESH))
        for cp in sends:
            cp.start()
        for cp in sends:
            cp.wait_recv()
        for cp in sends:
            cp.wait_send()
        own.wait()

    vmem = pl.BlockSpec(memory_space=pltpu.VMEM)
    return pl.pallas_call(
        body, name="gather_small", in_specs=[vmem], out_specs=vmem,
        out_shape=jax.ShapeDtypeStruct((N_DEV,) + small.shape, F32),
        scratch_shapes=[pltpu.SemaphoreType.DMA((N_DEV - 1,)), pltpu.SemaphoreType.DMA((N_DEV - 1,)),
                        pltpu.SemaphoreType.DMA],
    )(small)


def _adamw(w, g, m, v):
    m = ADAM_B1 * m + (1.0 - ADAM_B1) * g
    v = ADAM_B2 * v + (1.0 - ADAM_B2) * (g * g)
    m_hat = m / (1.0 - ADAM_B1 ** ADAM_STEP)
    v_hat = v / (1.0 - ADAM_B2 ** ADAM_STEP)
    return -ADAM_LR * (m_hat / (jnp.sqrt(v_hat) + ADAM_EPS) + ADAM_WD * w), m, v


def _slot_sum(ref, own=None, me=None):
    g = None
    for i in range(N_DEV):
        term = ref[i].astype(F32)
        if own is not None:
            term = jnp.where(i == me, own, term)
        g = term if g is None else g + term
    return g


def _update_matrix(name, me, landed, own, w, m, v, rows):
    r, c = w.shape

    def body(me_ref, l_ref, own_ref, w_ref, m_ref, v_ref, g_ref, d_ref, nm_ref, nv_ref):
        g = _slot_sum(l_ref, own_ref[...].astype(F32), me_ref[0])
        g_ref[...] = g
        d_ref[...], nm_ref[...], nv_ref[...] = _adamw(w_ref[...], g, m_ref[...], v_ref[...])

    blk = pl.BlockSpec((rows, c), lambda i, me_ref: (i, 0))
    return pl.pallas_call(
        body, name=name,
        grid_spec=pltpu.PrefetchScalarGridSpec(
            num_scalar_prefetch=1, grid=(r // rows,),
            in_specs=[pl.BlockSpec((N_DEV, rows, c), lambda i, me_ref: (0, i, 0)),
                      pl.BlockSpec((None, rows, c), lambda i, me_ref: (me_ref[0], i, 0)), blk, blk, blk],
            out_specs=[blk] * 4),
        out_shape=[jax.ShapeDtypeStruct((r, c), F32)] * 4,
        compiler_params=_params("parallel"),
    )(me, landed, own, w, m, v)


def _update_small(landed, lb_logits, ws, ms, vs):
    def body(l_ref, lbl_ref, w_ref, m_ref, v_ref, g_ref, d_ref, nm_ref, nv_ref, loss_ref):
        tot = _slot_sum(l_ref)
        _, dlb = _lower_bound(lbl_ref[...])
        g_lb = tot[1:2, :SEC_W] * dlb
        g = jnp.concatenate([tot[0:1], jnp.concatenate([g_lb, -g_lb], axis=-1),
                             jnp.pad(tot[1:2, SEC_W:], ((0, 0), (0, SEC_W))), tot[2:3]], axis=0)
        g_ref[...] = g
        d_ref[...], nm_ref[...], nv_ref[...] = _adamw(w_ref[...], g, m_ref[...], v_ref[...])
        loss_ref[...] = tot[3:4, 0:1]

    vmem = pl.BlockSpec(memory_space=pltpu.VMEM)
    return pl.pallas_call(
        body, name="update_small", in_specs=[vmem] * 5, out_specs=[vmem] * 5,
        out_shape=[jax.ShapeDtypeStruct((4, D_MODEL), F32)] * 4 + [jax.ShapeDtypeStruct((1, 1), F32)],
    )(landed, lb_logits, ws, ms, vs)


def _pack_small(norm_w, lb_logits, hg_norm_w, final_norm_w):
    return jnp.concatenate([norm_w, lb_logits.reshape(1, D_MODEL),
                            jnp.pad(hg_norm_w, ((0, 0), (0, D_MODEL - SEC_W))), final_norm_w[None, :]], axis=0)


def _unpack_small(a):
    return a[0:1], a[1].reshape(2, SEC_W), a[2:3, :SEC_W], a[3]


def kernel(x, norm_w, w_in, hgrn_lb_logits, hg_norm_w, w_out, final_norm_w, loss_target, m_norm_w, m_w_in, m_hgrn_lb_logits, m_hg_norm_w, m_w_out, m_final_norm_w, v_norm_w, v_w_in, v_hgrn_lb_logits, v_hg_norm_w, v_w_out, v_final_norm_w):
    w_all, wo_all = _gather_weights(w_in[0], w_out[0])
    flying_wo, flying_wi = [], []

    def start_w_out(gwo):
        *handles, token = _exchange_start("exchange_start_w_out", gwo.reshape(N_DEV, D_MODEL // N_DEV, D_MODEL))
        flying_wo.extend(handles)
        return token

    def start_w_in(gwi):
        *handles, token = _exchange_start("exchange_start_w_in", gwi)
        flying_wi.extend(handles)
        return token

    gx, small = _local_step(x[0], loss_target[0], norm_w, w_all, hgrn_lb_logits, hg_norm_w,
                            wo_all.reshape(D_MODEL, D_MODEL), final_norm_w, start_w_out, start_w_in)
    ls = _gather_small(small)
    gwo, lo = _exchange_wait("exchange_wait_w_out", *flying_wo, gx)
    gwi, li = _exchange_wait("exchange_wait_w_in", *flying_wi, gx)
    me = _me().astype(jnp.int32).reshape(1)
    g_wi, d_wi, nm_wi, nv_wi = _update_matrix("update_w_in", me, li, gwi, w_in[0], m_w_in[0], v_w_in[0], 256)
    g_wo, d_wo, nm_wo, nv_wo = _update_matrix("update_w_out", me, lo, gwo, w_out[0], m_w_out[0], v_w_out[0], 128)
    g_s, d_s, nm_s, nv_s, loss = _update_small(
        ls, hgrn_lb_logits, _pack_small(norm_w, hgrn_lb_logits, hg_norm_w, final_norm_w),
        _pack_small(m_norm_w, m_hgrn_lb_logits, m_hg_norm_w, m_final_norm_w),
        _pack_small(v_norm_w, v_hgrn_lb_logits, v_hg_norm_w, v_final_norm_w))
    outs = []
    for small_out, wi, wo in ((g_s, g_wi, g_wo), (d_s, d_wi, d_wo), (nm_s, nm_wi, nm_wo), (nv_s, nv_wi, nv_wo)):
        nw, lb, hg, fw = _unpack_small(small_out)
        outs += [nw, wi[None], lb, hg, wo[None], fw]
    return (loss[0, 0], gx[None], *outs)
```

```python
import functools

import jax
import jax.numpy as jnp
import numpy as np
from jax import lax
from jax.experimental import pallas as pl
from jax.experimental.pallas import tpu as pltpu

F32 = jnp.float32
MM = jnp.bfloat16
XCH = jnp.bfloat16
NORM_EPS = 1e-6
NEG = -1e30
N_DEV = 8
D_MODEL = 1024
N_SEC = 8
SEC_W = 512
HG_HEADS = 4
HG_D = 128
HG_GROUP = 4
AT_DH = 64
LANES = 128
ATT_BLK = 128
AT_COLS = 512
AT_QB = 8
DILATIONS = (1, 4, 16)
ROPE_THETA = 10000.0
CH = 16
LB_LO, LB_HI = 1e-6, 1.0 - 1e-6
ADAM_LR, ADAM_B1, ADAM_B2, ADAM_EPS, ADAM_WD, ADAM_STEP = 0.001, 0.9, 0.999, 1e-08, 0.01, 10
VMEM_LIMIT = 56 * 1024 * 1024
MESH = pl.DeviceIdType.MESH


def _params(*sem):
    return pltpu.CompilerParams(dimension_semantics=sem, vmem_limit_bytes=VMEM_LIMIT)


def _sigmoid(x):
    return 1.0 / (1.0 + jnp.exp(-x))


def _dot(a, b):
    return jnp.dot(a.astype(MM), b.astype(MM), preferred_element_type=F32)


def _dot_nt(a, b):
    return lax.dot_general(a.astype(MM), b.astype(MM), (((1,), (1,)), ((), ())), preferred_element_type=F32)


def _dot_tn(a, b):
    return lax.dot_general(a.astype(MM), b.astype(MM), (((0,), (0,)), ((), ())), preferred_element_type=F32)


def _split3(g):
    g1 = g.astype(jnp.bfloat16)
    r1 = g - g1.astype(F32)
    g2 = r1.astype(jnp.bfloat16)
    return g1, g2, (r1 - g2.astype(F32)).astype(jnp.bfloat16)


def _tri_dot(tri, g):
    t = tri.astype(jnp.bfloat16)
    g1, g2, g3 = _split3(g)
    d = functools.partial(jnp.dot, preferred_element_type=F32)
    return d(t, g1) + d(t, g2) + d(t, g3)


def _lower_bound(lbl):
    l0, l1 = lbl[0:1, :], lbl[1:2, :]
    m = jnp.maximum(l0, l1)
    e0, e1 = jnp.exp(l0 - m), jnp.exp(l1 - m)
    p = e0 / (e0 + e1)
    inside = (p >= LB_LO) & (p <= LB_HI)
    return jnp.clip(p, LB_LO, LB_HI), jnp.where(inside, p * (e1 / (e0 + e1)), 0.0)


def _iota2(shape, dim):
    return lax.broadcasted_iota(jnp.int32, shape, dim)


def _hgrn_gates(xq, xf, lb):
    sgq = _sigmoid(xq)
    sg = _sigmoid(xf)
    sn = _sigmoid(-xf)
    f = lb + (1.0 - lb) * sg
    return sgq, xq * sgq, sg, sn, f, (1.0 - lb) * sn


def _bdot(a, b, ca, cb):
    return lax.dot_general(a.astype(MM), b.astype(MM), (((ca,), (cb,)), ((0,), (0,))), preferred_element_type=F32)


def _chunk_masks(rb):
    row, col = _iota2((rb, rb), 0), _iota2((rb, rb), 1)
    same = (row // CH) == (col // CH)
    return same & (row >= col), same & (row <= col)


HALF = CH // 2
SLAB_ROWS = HALF * CH + (HALF // 2) * CH


def _write_slabs(slab_ref, g, q3, b3):
    slab = lambda t, rows: q3[:, rows, :] * jnp.exp(jnp.minimum(b3[:, rows, :] - b3[:, t:t + 1, :], 0.0))
    late = slice(HALF, CH)
    for t in range(HALF):
        slab_ref[g, :, t * CH:(t + 1) * CH, :] = slab(t, slice(0, CH)).astype(MM)
    for p in range(HALF // 2):
        t = HALF + 2 * p
        two = jnp.concatenate([slab(t, late), slab(t + 1, late)], axis=1)
        slab_ref[g, :, (HALF + p) * CH:(HALF + p + 1) * CH, :] = two.astype(MM)


def _read_diag(r):
    nc = r.shape[0]
    col, col_late = _iota2((nc, CH, CH), 2), _iota2((nc, HALF, CH), 2)
    a, a_late = jnp.zeros((nc, CH, CH), F32), jnp.zeros((nc, HALF, CH), F32)
    for t in range(HALF):
        a = a + jnp.where(col == t, r[:, t * CH:(t + 1) * CH, :], 0.0)
    for p in range(HALF // 2):
        t, two = HALF + 2 * p, r[:, (HALF + p) * CH:(HALF + p + 1) * CH, :]
        a_late = a_late + jnp.where(col_late == t, two[:, :HALF, :], 0.0) + jnp.where(col_late == t + 1, two[:, HALF:, :], 0.0)
    return a + jnp.concatenate([jnp.zeros_like(a_late), a_late], axis=1)


def _hgrn_fwd(proj, lb_logits, rb=256):
    s = proj.shape[1]
    nb, nc = s // rb, rb // CH

    def body(q_ref, f_ref, i_ref, lbl_ref, o_ref, sst_ref, a_ref, st_ref, slab_ref, states_ref):
        @pl.when(pl.program_id(1) == 0)
        def _():
            st_ref[...] = jnp.zeros_like(st_ref)

        sst_ref[...] = st_ref[...]
        prefix, _ = _chunk_masks(rb)
        c3 = lambda a: a.reshape(nc, CH, HG_D)
        row, col = _iota2((nc, CH, CH), 1), _iota2((nc, CH, CH), 2)
        heads = []
        for g in range(HG_GROUP):
            hs = slice(g * HG_D, (g + 1) * HG_D)
            lb, _ = _lower_bound(lbl_ref[:, hs])
            _, q, _, _, f, kk = _hgrn_gates(q_ref[:, hs], f_ref[:, hs], lb)
            b3 = c3(_tri_dot(prefix, jnp.log(f)))
            q3, kk3, v3 = c3(q), c3(kk), c3(i_ref[:, hs])
            bl3 = b3[:, CH - 1:CH, :]
            _write_slabs(slab_ref, g, q3, b3)
            x_upd = _bdot(v3, kk3 * jnp.exp(bl3 - b3), 1, 1)
            heads.append(dict(hs=hs, kk3=kk3, v3=v3, qe3=q3 * jnp.exp(b3), ebl3=jnp.exp(bl3), x_upd=x_upd))
        for g, hd in enumerate(heads):
            st = st_ref[g]
            for c in range(nc):
                states_ref[g, c] = st
                st = st * hd["ebl3"][c] + hd["x_upd"][c]
            st_ref[g] = st
        for g, hd in enumerate(heads):
            a = _read_diag(_bdot(slab_ref[g], hd["kk3"], 2, 2))
            a = jnp.where(row >= col, a, 0.0)
            a_ref[:, g * CH:(g + 1) * CH] = a.reshape(rb, CH)
            o3 = _bdot(hd["qe3"], states_ref[g], 2, 2) + _bdot(a, hd["v3"], 2, 1)
            o_ref[:, hd["hs"]] = o3.reshape(rb, HG_D)

    wide = HG_GROUP * HG_D
    sec = lambda j: pl.BlockSpec((None, rb, wide), lambda h, i, j=j: (j, i, h))
    return pl.pallas_call(
        body, name="hgrn_fwd", grid=(HG_HEADS // HG_GROUP, nb),
        in_specs=[sec(0), sec(1), sec(2), pl.BlockSpec((2, wide), lambda h, i: (0, h))],
        out_specs=[pl.BlockSpec((rb, wide), lambda h, i: (i, h)),
                   pl.BlockSpec((None, HG_GROUP, HG_D, HG_D), lambda h, i: (i, h, 0, 0)),
                   pl.BlockSpec((rb, HG_GROUP * CH), lambda h, i: (i, h))],
        out_shape=[jax.ShapeDtypeStruct((s, SEC_W), F32),
                   jax.ShapeDtypeStruct((nb, HG_HEADS, HG_D, HG_D), F32),
                   jax.ShapeDtypeStruct((s, HG_HEADS * CH), F32)],
        scratch_shapes=[pltpu.VMEM((HG_GROUP, HG_D, HG_D), F32), pltpu.VMEM((HG_GROUP, nc, SLAB_ROWS, HG_D), MM),
                        pltpu.VMEM((HG_GROUP, nc, HG_D, HG_D), F32)],
        compiler_params=_params("parallel", "arbitrary"),
    )(proj, proj, proj, lb_logits)


def _hgrn_bwd(proj, lb_logits, d_o, sst, a_in, token, rb=256):
    s = proj.shape[1]
    nb, nc = s // rb, rb // CH

    def body(q_ref, f_ref, i_ref, lbl_ref, do_ref, sst_ref, a_ref, tok_ref, dxq_ref, dxf_ref, dxi_ref, dlb_ref,
             dst_ref, states_ref, dstates_ref, lslab_ref, kslab_ref):
        @pl.when(pl.program_id(1) == 0)
        def _():
            dst_ref[...] = jnp.zeros_like(dst_ref)
            dlb_ref[...] = jnp.zeros_like(dlb_ref) + tok_ref[0:1, 0:1]

        prefix, suffix = _chunk_masks(rb)
        c3 = lambda a: a.reshape(nc, CH, HG_D)
        flat = lambda a: a.reshape(rb, HG_D)
        row, col = _iota2((nc, CH, CH), 1), _iota2((nc, CH, CH), 2)
        tril, triu = row >= col, row <= col
        sel = (_iota2((CH, CH * CH), 1) % CH == _iota2((CH, CH * CH), 0)).astype(MM)
        blockdiag = _iota2((nc, CH, CH * CH), 2) // CH == _iota2((nc, CH, CH * CH), 1)
        tile = lambda m: jnp.where(blockdiag, _dot(m.reshape(rb, CH), sel).reshape(nc, CH, CH * CH), 0.0)
        last = _iota2((nc, CH, HG_D), 1) == CH - 1
        heads = []
        for g in range(HG_GROUP):
            hs = slice(g * HG_D, (g + 1) * HG_D)
            lb, _ = _lower_bound(lbl_ref[:, hs])
            xq = q_ref[:, hs]
            sgq, q, sg, sn, f, kk = _hgrn_gates(xq, f_ref[:, hs], lb)
            b3 = c3(_tri_dot(prefix, jnp.log(f)))
            q3, kk3, v3, do3 = c3(q), c3(kk), c3(i_ref[:, hs]), c3(do_ref[:, hs])
            bl3 = b3[:, CH - 1:CH, :]
            eb3, ebl3, dec3 = jnp.exp(b3), jnp.exp(bl3), jnp.exp(bl3 - b3)
            qe3, kd3 = q3 * eb3, kk3 * dec3
            x_upd, y_upd = _bdot(v3, kd3, 1, 1), _bdot(do3, qe3, 1, 1)
            zero, every, early, late = jnp.zeros((nc, HALF, HG_D), F32), slice(0, CH), slice(0, HALF), slice(HALF, CH)
            for t in range(CH):
                bt = b3[:, t:t + 1, :]
                since = lambda rows: q3[:, rows, :] * jnp.exp(jnp.minimum(b3[:, rows, :] - bt, 0.0))
                until = lambda rows: kk3[:, rows, :] * jnp.exp(jnp.minimum(bt - b3[:, rows, :], 0.0))
                if t < HALF:
                    lv, kv = since(every), jnp.concatenate([until(early), zero], axis=1)
                else:
                    lv, kv = jnp.concatenate([zero, since(late)], axis=1), until(every)
                lslab_ref[g, :, t * CH:(t + 1) * CH, :] = lv.astype(MM)
                kslab_ref[g, :, t * CH:(t + 1) * CH, :] = kv.astype(MM)
            d_a = jnp.where(tril, _bdot(do3, v3, 2, 2), 0.0)
            d_at = jnp.where(triu, _bdot(v3, do3, 2, 2), 0.0)
            heads.append(dict(hs=hs, lb=lb, xq=xq, sgq=sgq, sg=sg, sn=sn, f=f, q3=q3, kk3=kk3, v3=v3, do3=do3,
                              eb3=eb3, ebl3=ebl3, dec3=dec3, qe3=qe3, kd3=kd3, x_upd=x_upd, y_upd=y_upd,
                              d_a=d_a, d_at=d_at))
        for g, hd in enumerate(heads):
            st = sst_ref[g]
            for c in range(nc):
                states_ref[g, c] = st
                st = st * hd["ebl3"][c] + hd["x_upd"][c]
            dst = dst_ref[g]
            for c in reversed(range(nc)):
                dstates_ref[g, c] = dst
                dst = dst * hd["ebl3"][c] + hd["y_upd"][c]
            dst_ref[g] = dst
        for g, hd in enumerate(heads):
            q3, v3, do3, kd3 = hd["q3"], hd["v3"], hd["do3"], hd["kd3"]
            states, dstates = states_ref[g], dstates_ref[g]
            hd["dqe"] = _bdot(do3, states, 2, 1)
            hd["dkd"] = _bdot(v3, dstates, 2, 1)
            a = a_ref[:, g * CH:(g + 1) * CH].reshape(nc, CH, CH)
            hd["dv"] = _bdot(kd3, dstates, 2, 2) + _bdot(a, do3, 1, 1)
            hd["dq_in"] = _bdot(tile(hd["d_a"]), kslab_ref[g], 2, 1)
            hd["dk_in"] = _bdot(tile(hd["d_at"]), lslab_ref[g], 2, 1)
            hd["ss"] = jnp.sum(dstates * states, axis=1, keepdims=True)
        for g, hd in enumerate(heads):
            q3, kk3, eb3, ebl3, dec3, qe3, kd3 = (hd[k] for k in ("q3", "kk3", "eb3", "ebl3", "dec3", "qe3", "kd3"))
            dqe, dkd, dq_in, dk_in = hd["dqe"], hd["dkd"], hd["dq_in"], hd["dk_in"]
            dkd_kd = dkd * kd3
            db = dqe * qe3 - dkd_kd + q3 * dq_in - kk3 * dk_in
            dbl = jnp.sum(dkd_kd, axis=1, keepdims=True) + hd["ss"] * ebl3
            dg = _tri_dot(suffix, flat(db + jnp.where(last, dbl, 0.0)))
            df = dg / hd["f"] - flat(dkd * dec3 + dk_in)
            xq, sgq, hs = hd["xq"], hd["sgq"], hd["hs"]
            dxq_ref[:, hs] = (flat(dqe * eb3 + dq_in) * (sgq * (1.0 + xq * (1.0 - sgq)))).astype(MM)
            dxf_ref[:, hs] = (df * (1.0 - hd["lb"]) * hd["sg"] * hd["sn"]).astype(MM)
            dxi_ref[:, hs] = flat(hd["dv"]).astype(MM)
            dlb_ref[:, hs] += jnp.sum(df * hd["sn"], axis=0, keepdims=True)

    wide = HG_GROUP * HG_D
    rev = lambda i: nb - 1 - i
    sec = lambda j: pl.BlockSpec((None, rb, wide), lambda h, i, j=j: (j, rev(i), h))
    blk = pl.BlockSpec((rb, wide), lambda h, i: (rev(i), h))
    state = (pltpu.VMEM((HG_GROUP, nc, HG_D, HG_D), F32), pltpu.VMEM((HG_GROUP, nc, CH * CH, HG_D), MM))
    return pl.pallas_call(
        body, name="hgrn_bwd", grid=(HG_HEADS // HG_GROUP, nb),
        in_specs=[sec(0), sec(1), sec(2), pl.BlockSpec((2, wide), lambda h, i: (0, h)), blk,
                  pl.BlockSpec((None, HG_GROUP, HG_D, HG_D), lambda h, i: (rev(i), h, 0, 0)),
                  pl.BlockSpec((rb, HG_GROUP * CH), lambda h, i: (rev(i), h)),
                  pl.BlockSpec((8, 128), lambda h, i: (0, 0))],
        out_specs=[blk, blk, blk, pl.BlockSpec((1, wide), lambda h, i: (0, h))],
        out_shape=[jax.ShapeDtypeStruct((s, SEC_W), MM)] * 3 + [jax.ShapeDtypeStruct((1, SEC_W), F32)],
        scratch_shapes=[pltpu.VMEM((HG_GROUP, HG_D, HG_D), F32), state[0], state[0], state[1], state[1]],
        compiler_params=_params("parallel", "arbitrary"),
    )(proj, proj, proj, lb_logits, d_o, sst, a_in, token)


def _rope_tables(s):
    half = AT_DH // 2
    inv_freq = np.float32(1.0) / (np.float32(ROPE_THETA) ** (np.arange(half, dtype=np.float32) / np.float32(half)))
    ang = np.arange(s, dtype=np.float32)[:, None] * inv_freq[None, :]
    cos, sin = np.cos(ang), np.sin(ang)
    return np.concatenate([cos] * 4, axis=-1), np.concatenate([-sin, sin] * 2, axis=-1)


def _rope128(x, cos, sin):
    lo = (_iota2(x.shape, 1) % AT_DH) < AT_DH // 2
    rot = jnp.where(lo, pltpu.roll(x, LANES - AT_DH // 2, 1), pltpu.roll(x, AT_DH // 2, 1))
    return x * cos + rot * sin


LANE_GROUPS = SEC_W // LANES


def _set_lanes(ref, val):
    for j in range(LANE_GROUPS):
        ref[j] = val[:, j * LANES:(j + 1) * LANES]


def _get_lanes(ref):
    return jnp.concatenate([ref[j] for j in range(LANE_GROUPS)], axis=-1)


def _to_view(src_ref, dst_ref, d):
    n = src_ref.shape[1] // d
    for r in range(d):
        rows = pl.ds(r, n, stride=d) if d > 1 else slice(None)
        for j in range(LANE_GROUPS):
            c0 = r * SEC_W + j * LANES
            dst_ref[:, c0:c0 + LANES] = src_ref.at[j][rows, :].astype(dst_ref.dtype)


def _from_view(src_ref, dst_ref, d):
    n = dst_ref.shape[1] // d
    for r in range(d):
        for j in range(LANE_GROUPS):
            c0 = r * SEC_W + j * LANES
            dst_ref.at[j][pl.ds(r, n, stride=d), :] = src_ref[:, c0:c0 + LANES].astype(dst_ref.dtype)


def _view_spec(tm, d):
    return pl.BlockSpec((tm // d, d * SEC_W), lambda i: (i, 0))


def _view_shape(s, d, dtype):
    return jax.ShapeDtypeStruct((s // d, d * SEC_W), dtype)


PROJ_KEPT = (0, 1, 2, 3, 7)


def _inproj_fwd(x, norm_w, w_all, cos, sin, tm=512):
    s = x.shape[0]

    def body(x_ref, nw_ref, w_ref, cos_ref, sin_ref, proj_ref, *refs):
        outs, (qs_ref, ks_ref, vs_ref) = refs[:-3], refs[-3:]
        xv = x_ref[...]
        rstd = lax.rsqrt(jnp.mean(xv * xv, axis=-1, keepdims=True) + NORM_EPS)
        u = (xv * rstd * nw_ref[...]).astype(MM)
        for slot, j in enumerate(PROJ_KEPT):
            proj_ref[slot] = jnp.dot(u, w_ref[j], preferred_element_type=F32)
        q, k, v = [jnp.dot(u, w_ref[j], preferred_element_type=F32) for j in (4, 5, 6)]
        c, sn = cos_ref[...], sin_ref[...]
        for g in range(LANE_GROUPS):
            sl = slice(g * LANES, (g + 1) * LANES)
            qs_ref[g] = _rope128(q[:, sl], c, sn) * (AT_DH ** -0.5)
            ks_ref[g] = _rope128(k[:, sl], c, sn)
            vs_ref[g] = v[:, sl]
        for i, d in enumerate(DILATIONS):
            for src_ref, dst_ref in zip((qs_ref, ks_ref, vs_ref), outs[3 * i:3 * i + 3]):
                _to_view(src_ref, dst_ref, d)

    tab = pl.BlockSpec((tm, LANES), lambda i: (i, 0))
    return pl.pallas_call(
        body, name="inproj_fwd", grid=(s // tm,),
        in_specs=[pl.BlockSpec((tm, D_MODEL), lambda i: (i, 0)),
                  pl.BlockSpec((1, D_MODEL), lambda i: (0, 0)),
                  pl.BlockSpec((N_SEC, D_MODEL, SEC_W), lambda i: (0, 0, 0)), tab, tab],
        out_specs=[pl.BlockSpec((len(PROJ_KEPT), tm, SEC_W), lambda i: (0, i, 0))]
                  + [_view_spec(tm, d) for d in DILATIONS for _ in range(3)],
        out_shape=[jax.ShapeDtypeStruct((len(PROJ_KEPT), s, SEC_W), F32)]
                  + [_view_shape(s, d, MM) for d in DILATIONS for _ in range(3)],
        scratch_shapes=[pltpu.VMEM((LANE_GROUPS, tm, LANES), F32)] * 3,
        compiler_params=_params("parallel"),
    )(x, norm_w, w_all, cos, sin)


def _band_mask(first_ok, second_ok):
    row, col = _iota2((ATT_BLK, 2 * ATT_BLK), 0), _iota2((ATT_BLK, 2 * ATT_BLK), 1)
    return ((col < ATT_BLK) & (col >= row) & first_ok) | ((col >= ATT_BLK) & ((col - ATT_BLK) <= row) & second_ok)


def _own_lanes(rows, h):
    lane = _iota2((rows, LANES), 1)
    return (lane < AT_DH) if h == 0 else (lane >= AT_DH)


def _neg_pieces(rows, h):
    lane = _iota2((rows, LANES), 1) - (AT_DH if h == 0 else 0)
    return jnp.where((lane >= 0) & (lane < 3), -1.0, 0.0).astype(MM)


def _units(qb):
    return [(b, slice(g * LANES, (g + 1) * LANES), h) for b in range(qb) for g in range(AT_COLS // LANES) for h in range(2)]


def _sub(b):
    return slice(b * ATT_BLK, (b + 1) * ATT_BLK)


def _band_before(cur_ref, prev_ref, b, sl):
    if b == 0:
        return jnp.concatenate([prev_ref[:, sl], cur_ref[0:ATT_BLK, sl]], axis=0)
    return cur_ref[(b - 1) * ATT_BLK:(b + 1) * ATT_BLK, sl]


def _band_after(cur_ref, next_ref, b, sl):
    if (b + 1) * ATT_BLK == cur_ref.shape[0]:
        return jnp.concatenate([cur_ref[b * ATT_BLK:(b + 1) * ATT_BLK, sl], next_ref[:, sl]], axis=0)
    return cur_ref[b * ATT_BLK:(b + 2) * ATT_BLK, sl]


def _attn_specs(rows):
    qb = min(AT_QB, rows // ATT_BLK)
    assert rows % (qb * ATT_BLK) == 0
    last = rows // ATT_BLK - 1
    cur = pl.BlockSpec((qb * ATT_BLK, AT_COLS), lambda c, n: (n, c))
    prev = pl.BlockSpec((ATT_BLK, AT_COLS), lambda c, n: (jnp.maximum(qb * n - 1, 0), c))
    nxt = pl.BlockSpec((ATT_BLK, AT_COLS), lambda c, n: (jnp.minimum(qb * (n + 1), last), c))
    return qb, cur, prev, nxt


def _stack_heads(a):
    h0 = _own_lanes(a.shape[0], 0)
    zero = jnp.zeros_like(a)
    return jnp.concatenate([jnp.where(h0, a, zero), jnp.where(h0, zero, a)], axis=0)


def _unstack_heads(a2):
    return jnp.where(_own_lanes(ATT_BLK, 0), a2[:ATT_BLK], a2[ATT_BLK:])


def _attn_fwd(qr, kr, vr, d):
    rows, cols = qr.shape
    qb, cur, prev, nxt = _attn_specs(rows)
    nb = rows // (qb * ATT_BLK)

    def body(q_ref, kc_ref, kp_ref, vc_ref, vp_ref, o_ref, lse_ref):
        twice = lambda m: jnp.concatenate([m, m], axis=0)
        masks = {True: twice(_band_mask(pl.program_id(1) > 0, True)), False: twice(_band_mask(True, True))}
        ones = jnp.ones((2 * ATT_BLK, LANES), MM)
        units = [(b, sl) for b, sl, h in _units(qb) if h == 0]
        scs = [jnp.where(masks[b == 0], _dot_nt(_stack_heads(q_ref[_sub(b), sl]), _band_before(kc_ref, kp_ref, b, sl)),
                         NEG) for b, sl in units]
        ms = [jnp.max(sc, axis=-1, keepdims=True) for sc in scs]
        ps = [jnp.exp(sc - m).astype(MM) for sc, m in zip(scs, ms)]
        ols = [jnp.dot(p, jnp.concatenate([_band_before(vc_ref, vp_ref, b, sl), ones], axis=1),
                       preferred_element_type=F32) for p, (b, sl) in zip(ps, units)]
        for (b, sl), m, ol in zip(units, ms, ols):
            l = _unstack_heads(ol[:, LANES:])
            o_ref[_sub(b), sl] = _unstack_heads(ol[:, :LANES]) / l
            lse_ref[_sub(b), sl] = _unstack_heads(jnp.broadcast_to(m, (2 * ATT_BLK, LANES))) + jnp.log(l)

    o, lse = pl.pallas_call(
        body, name=f"attn_fwd_d{d}", grid=(cols // AT_COLS, nb),
        in_specs=[cur, cur, prev, cur, prev], out_specs=[cur, cur],
        out_shape=[jax.ShapeDtypeStruct((rows, cols), F32)] * 2,
        compiler_params=_params("parallel", "parallel"),
    )(qr, kr, kr, vr, vr)
    return o, lse


def _attn_bwd_dq(qr, kr, vr, do, lse, delta, d):
    rows, cols = qr.shape
    qb, cur, prev, nxt = _attn_specs(rows)
    nb = rows // (qb * ATT_BLK)

    def body(q_ref, kc_ref, kp_ref, vc_ref, vp_ref, do_ref, lse_ref, dl_ref, dq_ref):
        masks = {True: _band_mask(pl.program_id(1) > 0, True), False: _band_mask(True, True)}
        units = _units(qb)
        sms, dps = [], []
        for b, sl, h in units:
            own, own_b, neg = _own_lanes(ATT_BLK, h), _own_lanes(2 * ATT_BLK, h), _neg_pieces(2 * ATT_BLK, h)
            sms.append(_dot_nt(jnp.where(own, q_ref[_sub(b), sl], lse_ref[_sub(b), sl]),
                               jnp.where(own_b, _band_before(kc_ref, kp_ref, b, sl), neg)))
            dps.append(_dot_nt(jnp.where(own, do_ref[_sub(b), sl], dl_ref[_sub(b), sl]),
                               jnp.where(own_b, _band_before(vc_ref, vp_ref, b, sl), neg)))
        dss = [(jnp.exp(jnp.where(masks[b == 0], sm, NEG)) * dp).astype(MM)
               for sm, dp, (b, _, _) in zip(sms, dps, units)]
        dqs = [jnp.dot(ds, _band_before(kc_ref, kp_ref, b, sl), preferred_element_type=F32) * (AT_DH ** -0.5)
               for ds, (b, sl, _) in zip(dss, units)]
        for i in range(0, len(units), 2):
            b, sl, _ = units[i]
            dq_ref[_sub(b), sl] = jnp.where(_own_lanes(ATT_BLK, 0), dqs[i], dqs[i + 1]).astype(dq_ref.dtype)

    dq = pl.pallas_call(
        body, name=f"attn_bwd_dq_d{d}", grid=(cols // AT_COLS, nb),
        in_specs=[cur, cur, prev, cur, prev, cur, cur, cur], out_specs=cur,
        out_shape=jax.ShapeDtypeStruct((rows, cols), MM),
        compiler_params=_params("parallel", "parallel"),
    )(qr, kr, kr, vr, vr, do, lse, delta)
    return dq


def _attn_bwd_dkv(qr, kr, vr, do, lse, delta, d):
    rows, cols = qr.shape
    qb, cur, prev, nxt = _attn_specs(rows)
    nb = rows // (qb * ATT_BLK)

    def body(k_ref, v_ref, qc_ref, qn_ref, doc_ref, don_ref, lsec_ref, lsen_ref, dlc_ref, dln_ref,
             dk_ref, dv_ref):
        masks = {True: _band_mask(True, pl.program_id(1) < nb - 1), False: _band_mask(True, True)}
        units = _units(qb)
        sms, dps = [], []
        for b, sl, h in units:
            own, own_b, neg = _own_lanes(ATT_BLK, h), _own_lanes(2 * ATT_BLK, h), _neg_pieces(ATT_BLK, h)
            sms.append(_dot_nt(jnp.where(own, k_ref[_sub(b), sl], neg),
                               jnp.where(own_b, _band_after(qc_ref, qn_ref, b, sl),
                                         _band_after(lsec_ref, lsen_ref, b, sl))))
            dps.append(_dot_nt(jnp.where(own, v_ref[_sub(b), sl], neg),
                               jnp.where(own_b, _band_after(doc_ref, don_ref, b, sl),
                                         _band_after(dlc_ref, dln_ref, b, sl))))
        ps = [jnp.exp(jnp.where(masks[b == qb - 1], sm, NEG)) for sm, (b, _, _) in zip(sms, units)]
        dss = [(p * dp).astype(MM) for p, dp in zip(ps, dps)]
        dvs = [jnp.dot(p.astype(MM), _band_after(doc_ref, don_ref, b, sl), preferred_element_type=F32)
               for p, (b, sl, _) in zip(ps, units)]
        dks = [jnp.dot(ds, _band_after(qc_ref, qn_ref, b, sl), preferred_element_type=F32)
               for ds, (b, sl, _) in zip(dss, units)]
        head0 = _own_lanes(ATT_BLK, 0)
        for i in range(0, len(units), 2):
            b, sl, _ = units[i]
            dk_ref[_sub(b), sl] = jnp.where(head0, dks[i], dks[i + 1]).astype(dk_ref.dtype)
            dv_ref[_sub(b), sl] = jnp.where(head0, dvs[i], dvs[i + 1]).astype(dv_ref.dtype)

    dk, dv = pl.pallas_call(
        body, name=f"attn_bwd_dkv_d{d}", grid=(cols // AT_COLS, nb),
        in_specs=[cur, cur, cur, nxt, cur, nxt, cur, nxt, cur, nxt], out_specs=[cur, cur],
        out_shape=[jax.ShapeDtypeStruct((rows, cols), MM)] * 2,
        compiler_params=_params("parallel", "parallel"),
    )(kr, vr, qr, qr, do, do, lse, lse, delta, delta)
    return dk, dv


def _head_sum(a, width):
    parts = []
    for j in range(a.shape[1] // width):
        sm = jnp.sum(a[:, j * width:(j + 1) * width], axis=-1, keepdims=True)
        parts.append(jnp.broadcast_to(sm, (a.shape[0], width)))
    return jnp.concatenate(parts, axis=-1)


def _partner_sum(a):
    swap = (_iota2((LANES, LANES), 0) // AT_DH != _iota2((LANES, LANES), 1) // AT_DH).astype(jnp.bfloat16)
    d = functools.partial(jnp.dot, preferred_element_type=F32)
    parts = _split3(a)
    return jnp.concatenate([d(parts[0][:, sl], swap) + d(parts[1][:, sl], swap) + d(parts[2][:, sl], swap)
                            for sl in (slice(j * LANES, (j + 1) * LANES) for j in range(a.shape[1] // LANES))], axis=-1)


def _partner_value(x):
    return jnp.concatenate([pltpu.roll(x[:, j * LANES:(j + 1) * LANES], AT_DH, 1) for j in range(x.shape[1] // LANES)],
                           axis=-1)


def _pieces(xs):
    hi = xs.astype(jnp.bfloat16).astype(F32)
    mid = (xs - hi).astype(jnp.bfloat16).astype(F32)
    lo = (xs - hi - mid).astype(jnp.bfloat16).astype(F32)
    lane = _iota2(xs.shape, 1) % AT_DH
    return jnp.where(lane == 0, hi, jnp.where(lane == 1, mid, jnp.where(lane == 2, lo, 0.0)))


def _mid(x, tgt, proj, o_hg, o_at, lse_at, hg_norm_w, final_norm_w, wo_all, tm=256):
    s = x.shape[0]
    nb = s // tm

    def body(x_ref, t_ref, hgz_ref, atz_ref, ohg_ref, o1_ref, o2_ref, o3_ref, l1_ref, l2_ref, l3_ref,
             g_ref, fw_ref, wo_ref,
             dh_ref, dohg_ref, dhgz_ref, datz_ref, do1_ref, do2_ref, do3_ref, dl1_ref, dl2_ref, dl3_ref,
             lp1_ref, lp2_ref, lp3_ref,
             gwo_ref, gfw_ref, ghg_ref, loss_ref, nat_ref, stage_ref, gwo_acc):
        @pl.when(pl.program_id(0) == 0)
        def _():
            gwo_acc[...] = jnp.zeros_like(gwo_acc)
            gfw_ref[...] = jnp.zeros_like(gfw_ref)
            ghg_ref[...] = jnp.zeros_like(ghg_ref)
            loss_ref[...] = jnp.zeros_like(loss_ref)

        ohg, g = ohg_ref[...], g_ref[...]
        rs = lax.rsqrt(_head_sum(ohg * ohg, HG_D) * (1.0 / HG_D) + NORM_EPS)
        on = ohg * rs
        hgz = hgz_ref[...]
        sz = _sigmoid(hgz)
        gate_hg = hgz * sz
        lses, outs = [l1_ref[...]], [o1_ref[...]]
        for k, (d, l_ref, o_ref) in enumerate(zip(DILATIONS[1:], (l2_ref, l3_ref), (o2_ref, o3_ref))):
            _from_view(l_ref, nat_ref.at[2 * k], d)
            _from_view(o_ref, nat_ref.at[2 * k + 1], d)
            lses.append(_get_lanes(nat_ref.at[2 * k]))
            outs.append(_get_lanes(nat_ref.at[2 * k + 1]))
        mx = jnp.maximum(jnp.maximum(lses[0], lses[1]), lses[2])
        es = [jnp.exp(l - mx) for l in lses]
        den = es[0] + es[1] + es[2]
        ws = [e / den for e in es]
        oat = ws[0] * outs[0] + ws[1] * outs[1] + ws[2] * outs[2]
        atz = atz_ref[...]
        sa = _sigmoid(atz)
        gate_at = atz * sa
        mixed = jnp.concatenate([on * g * gate_hg, oat * gate_at], axis=-1).astype(MM)
        h = x_ref[...] + jnp.dot(mixed, wo_ref[...], preferred_element_type=F32)
        rstd = lax.rsqrt(jnp.mean(h * h, axis=-1, keepdims=True) + NORM_EPS)
        hn = h * rstd
        fw = fw_ref[...]
        err = hn * fw - t_ref[...]
        loss_ref[...] += 0.5 * jnp.sum(jnp.mean(err * err, axis=-1, keepdims=True), axis=0, keepdims=True)
        dout = err * (1.0 / D_MODEL)
        gfw_ref[...] += jnp.sum(dout * hn, axis=0, keepdims=True)
        dhn = dout * fw
        dh = rstd * (dhn - hn * jnp.mean(dhn * hn, axis=-1, keepdims=True))
        dh_ref[...] = dh
        dh_mm = dh.astype(MM)
        gwo_acc[...] += _dot_tn(mixed, dh_mm)

        @pl.when(pl.program_id(0) == nb - 1)
        def _():
            gwo_ref[...] = gwo_acc[...].astype(gwo_ref.dtype)

        dmixed = _dot_nt(dh_mm, wo_ref[...])
        dm_hg = dmixed[:, :SEC_W]
        d_ong = dm_hg * gate_hg
        dhgz_ref[...] = (dm_hg * (on * g) * (sz * (1.0 + hgz * (1.0 - sz)))).astype(MM)
        ghg_ref[...] += jnp.sum(d_ong * on, axis=0, keepdims=True)
        d_on = d_ong * g
        dohg_ref[...] = rs * (d_on - on * (_head_sum(d_on * on, HG_D) * (1.0 / HG_D)))
        dm_at = dmixed[:, SEC_W:]
        d_oat = dm_at * gate_at
        datz_ref[...] = (dm_at * oat * (sa * (1.0 + atz * (1.0 - sa)))).astype(MM)
        lse_all = mx + jnp.log(den)
        for val, dst_refs in ((d_oat, (do1_ref, do2_ref, do3_ref)),
                              (_pieces(_partner_sum(d_oat * oat)), (dl1_ref, dl2_ref, dl3_ref)),
                              (_pieces(_partner_value(lse_all)), (lp1_ref, lp2_ref, lp3_ref))):
            _set_lanes(stage_ref, val)
            for d, dst_ref in zip(DILATIONS, dst_refs):
                _to_view(stage_ref, dst_ref, d)

    row = lambda w: pl.BlockSpec((tm, w), lambda i: (i, 0))
    sec = lambda j: pl.BlockSpec((None, tm, SEC_W), lambda i, j=j: (j, i, 0))
    const = lambda shp: pl.BlockSpec(shp, lambda i: (0,) * len(shp))
    half = row(SEC_W)
    views = [_view_spec(tm, d) for d in DILATIONS]
    return pl.pallas_call(
        body, name="mid", grid=(nb,),
        in_specs=[row(D_MODEL), row(D_MODEL), sec(PROJ_KEPT.index(3)), sec(PROJ_KEPT.index(7)), half] + views * 2
                 + [const((1, SEC_W)), const((1, D_MODEL)), const((D_MODEL, D_MODEL))],
        out_specs=[row(D_MODEL)] + [half] * 3 + views * 3
                  + [const((D_MODEL, D_MODEL)), const((1, D_MODEL)), const((1, SEC_W)), const((1, 1))],
        out_shape=[jax.ShapeDtypeStruct((s, D_MODEL), F32), jax.ShapeDtypeStruct((s, SEC_W), F32)]
                  + [jax.ShapeDtypeStruct((s, SEC_W), MM)] * 2
                  + [_view_shape(s, d, MM) for d in DILATIONS] * 3
                  + [jax.ShapeDtypeStruct((D_MODEL, D_MODEL), XCH), jax.ShapeDtypeStruct((1, D_MODEL), F32),
                     jax.ShapeDtypeStruct((1, SEC_W), F32), jax.ShapeDtypeStruct((1, 1), F32)],
        scratch_shapes=[pltpu.VMEM((4, LANE_GROUPS, tm, LANES), F32), pltpu.VMEM((LANE_GROUPS, tm, LANES), F32),
                        pltpu.VMEM((D_MODEL, D_MODEL), F32)],
        compiler_params=_params("arbitrary"),
    )(x, tgt, proj, proj, o_hg, *o_at, *lse_at, hg_norm_w, final_norm_w, wo_all)


def _section_specs(dsecs, tm):
    return [pl.BlockSpec((tm, SEC_W), lambda i: (i, 0)) if k is None
            else pl.BlockSpec((None, tm, SEC_W), lambda i, k=k: (k, i, 0)) for _, k in dsecs]


def _inproj_bwd_x(x, norm_w, w_all, dh, dsecs, token, tm=512):
    s = x.shape[0]

    def body(x_ref, nw_ref, w_ref, dh_ref, tok_ref, *refs):
        sec_refs, (gx_ref, gnw_ref) = refs[:N_SEC], refs[N_SEC:]

        @pl.when(pl.program_id(0) == 0)
        def _():
            gnw_ref[...] = jnp.zeros_like(gnw_ref)

        du = jnp.zeros((tm, D_MODEL), F32)
        for j in range(N_SEC):
            du = du + _dot_nt(sec_refs[j][...], w_ref[j])
        xv, nw = x_ref[...], nw_ref[...]
        rstd = lax.rsqrt(jnp.mean(xv * xv, axis=-1, keepdims=True) + NORM_EPS)
        xn = xv * rstd
        gnw_ref[...] += jnp.sum(du * xn, axis=0, keepdims=True)
        dxn = du * nw
        dx = rstd * (dxn - xn * jnp.mean(dxn * xn, axis=-1, keepdims=True))
        gx_ref[...] = (dh_ref[...] + tok_ref[0:1, 0:1]) + dx

    row = lambda w: pl.BlockSpec((tm, w), lambda i: (i, 0))
    const = lambda shp: pl.BlockSpec(shp, lambda i: (0,) * len(shp))
    return pl.pallas_call(
        body, name="inproj_bwd_x", grid=(s // tm,),
        in_specs=[row(D_MODEL), const((1, D_MODEL)), const((N_SEC, D_MODEL, SEC_W)), row(D_MODEL), const((8, 128))]
                 + _section_specs(dsecs, tm),
        out_specs=[row(D_MODEL), const((1, D_MODEL))],
        out_shape=[jax.ShapeDtypeStruct((s, D_MODEL), F32), jax.ShapeDtypeStruct((1, D_MODEL), F32)],
        compiler_params=_params("arbitrary"),
    )(x, norm_w, w_all, dh, token, *[a for a, _ in dsecs])


def _inproj_bwd_w(x, norm_w, dsec, dq_r, dk_r, dv, cos, sin, tm=512):
    s = x.shape[0]
    nb = s // tm

    def body(x_ref, nw_ref, s0, s1, s2, s3, s7, q1, q2, q3, k1, k2, k3, v1, v2, v3, cos_ref, sin_ref,
             gw_hbm, datt_ref, acc_ref, stage_ref, nat_ref):
        @pl.when(pl.program_id(0) == 0)
        def _():
            acc_ref[...] = jnp.zeros_like(acc_ref)

        def total(refs):
            acc = refs[0][...].astype(F32)
            for d, ref in zip(DILATIONS[1:], refs[1:]):
                _from_view(ref, nat_ref, d)
                acc = acc + _get_lanes(nat_ref)
            return acc

        c, sn = cos_ref[...], -sin_ref[...]
        unrot = lambda a: jnp.concatenate(
            [_rope128(a[:, j * LANES:(j + 1) * LANES], c, sn) for j in range(LANE_GROUPS)], axis=-1)
        att = [a.astype(MM) for a in (unrot(total((q1, q2, q3))), unrot(total((k1, k2, k3))), total((v1, v2, v3)))]
        for j, a in enumerate(att):
            datt_ref[j] = a
        xv = x_ref[...]
        rstd = lax.rsqrt(jnp.mean(xv * xv, axis=-1, keepdims=True) + NORM_EPS)
        u_t = (xv * rstd * nw_ref[...]).T.astype(MM)
        for j, dsj in enumerate((s0[...], s1[...], s2[...], s3[...], *att, s7[...])):
            acc_ref[j] += jnp.dot(u_t, dsj, preferred_element_type=F32)

        @pl.when(pl.program_id(0) == nb - 1)
        def _():
            for j in range(N_SEC):
                stage_ref[...] = acc_ref[j].astype(stage_ref.dtype)
                pltpu.sync_copy(stage_ref, gw_hbm.at[j])

    row = lambda w: pl.BlockSpec((tm, w), lambda i: (i, 0))
    return pl.pallas_call(
        body, name="inproj_bwd_w", grid=(nb,),
        in_specs=[row(D_MODEL), pl.BlockSpec((1, D_MODEL), lambda i: (0, 0))] + [row(SEC_W)] * 5
                 + [_view_spec(tm, d) for d in DILATIONS] * 3 + [row(LANES), row(LANES)],
        out_specs=[pl.BlockSpec(memory_space=pl.ANY), pl.BlockSpec((3, tm, SEC_W), lambda i: (0, i, 0))],
        out_shape=[jax.ShapeDtypeStruct((N_SEC, D_MODEL, SEC_W), XCH), jax.ShapeDtypeStruct((3, s, SEC_W), MM)],
        scratch_shapes=[pltpu.VMEM((N_SEC, D_MODEL, SEC_W), F32), pltpu.VMEM((D_MODEL, SEC_W), XCH),
                        pltpu.VMEM((LANE_GROUPS, tm, LANES), F32)],
        compiler_params=_params("arbitrary"),
    )(x, norm_w, *dsec, *dq_r, *dk_r, *dv, cos, sin)


def _local_step(x, tgt, norm_w, w_all, lb_logits, hg_norm_w, wo_all, final_norm_w, on_w_out_grad, on_w_in_grad):
    s = x.shape[0]
    cos, sin = _rope_tables(s)
    proj, *qkv = _inproj_fwd(x, norm_w, w_all, cos, sin)
    o_hg, sst, a_hg = _hgrn_fwd(proj, lb_logits)
    qkv = [qkv[3 * i:3 * i + 3] for i in range(len(DILATIONS))]
    att = [_attn_fwd(*qkv_d, d) for qkv_d, d in zip(qkv, DILATIONS)]
    (dh, d_ohg, d_hgz, d_atz, do1, do2, do3, dl1, dl2, dl3, lp1, lp2, lp3, gwo, gfw, ghg, loss) = _mid(
        x, tgt, proj, o_hg, [a[0] for a in att], [a[1] for a in att], hg_norm_w, final_norm_w[None, :], wo_all(o_hg))
    dxq, dxf, dxi, dlb = _hgrn_bwd(proj, lb_logits, d_ohg, sst, a_hg, on_w_out_grad(gwo))
    dq_r, dk_r, dv = [], [], []
    for d, qkv_d, do, lp, dl in zip(DILATIONS, qkv, (do1, do2, do3), (lp1, lp2, lp3), (dl1, dl2, dl3)):
        dq_r.append(_attn_bwd_dq(*qkv_d, do, lp, dl, d))
        dk_d, dv_d = _attn_bwd_dkv(*qkv_d, do, lp, dl, d)
        dk_r.append(dk_d)
        dv.append(dv_d)
    gwi, d_att = _inproj_bwd_w(x, norm_w, (dxq, dxf, dxi, d_hgz, d_atz), dq_r, dk_r, dv, cos, sin)
    dsecs = [(dxq, None), (dxf, None), (dxi, None), (d_hgz, None), (d_att, 0), (d_att, 1), (d_att, 2), (d_atz, None)]
    token = on_w_in_grad(gwi)
    gx, gnw = _inproj_bwd_x(x, norm_w, w_all, dh, dsecs, token)
    small = jnp.concatenate([gnw, jnp.concatenate([dlb, ghg], axis=-1), gfw,
                             jnp.pad(loss, ((0, 0), (0, D_MODEL - 1)))], axis=0)
    return gx, small


def _coords():
    return lax.axis_index("x"), lax.axis_index("y"), lax.axis_index("c")


def _gather_weights(w_in):
    def body(wi_ref, wi_all, send_sems, recv_sems):
        x, y, c = _coords()
        me, sibling = (x, y, c), (x, y, 1 - c)
        chips = [(1 - x, y), (x, 1 - y), (1 - x, 1 - y)]
        slot = lambda p: 4 * p[0] + 2 * p[1] + p[2]

        def copies(k, block, to):
            return [pltpu.make_async_remote_copy(
                src_ref=wi_all.at[slot(block)], dst_ref=wi_all.at[slot(block)], send_sem=send_sems.at[k],
                recv_sem=recv_sems.at[k], device_id=to, device_id_type=MESH)]

        wi_all[slot(me)] = wi_ref[...].astype(MM)
        first = copies(0, me, sibling)
        for j, chip in enumerate(chips):
            first += copies(1 + j, me, (*chip, c))
        for cp in first:
            cp.start()
        passed = []
        for j, chip in enumerate(chips):
            for cp in copies(1 + j, (*chip, c), me):
                cp.wait_recv()
            fwd = copies(4 + j, (*chip, c), sibling)
            for cp in fwd:
                cp.start()
            passed += fwd
        for cp in copies(0, sibling, me):
            cp.wait_recv()
        for j, chip in enumerate(chips):
            for cp in copies(4 + j, (*chip, 1 - c), me):
                cp.wait_recv()
        for cp in first + passed:
            cp.wait_send()

    vmem = pl.BlockSpec(memory_space=pltpu.VMEM)
    return pl.pallas_call(
        body, name="gather_weights",
        in_specs=[vmem], out_specs=vmem,
        out_shape=jax.ShapeDtypeStruct((N_DEV, D_MODEL, SEC_W), MM),
        scratch_shapes=[pltpu.SemaphoreType.DMA((7,)), pltpu.SemaphoreType.DMA((7,))],
        compiler_params=pltpu.CompilerParams(vmem_limit_bytes=VMEM_LIMIT),
    )(w_in)


def _me():
    x, y, c = _coords()
    return 4 * x + 2 * y + c


def _grad_copies(srcs, lands, send_sems, recv_sems):
    x, y, c = _coords()
    me = 4 * x + 2 * y + c
    copies = []
    for k in range(1, N_DEV):
        px, py, pc = x ^ (k >> 2), y ^ ((k >> 1) & 1), c ^ (k & 1)
        peer = 4 * px + 2 * py + pc
        for a, (src, dst) in enumerate(zip(srcs, lands)):
            copies.append(pltpu.make_async_remote_copy(
                src_ref=src.at[peer], dst_ref=dst.at[me], send_sem=send_sems.at[a * (N_DEV - 1) + k - 1],
                recv_sem=recv_sems.at[a * (N_DEV - 1) + k - 1], device_id=(px, py, pc), device_id_type=MESH))
    return copies


HBM_SPEC = pl.BlockSpec(memory_space=pltpu.HBM)
SEM_SPEC = pl.BlockSpec(memory_space=pltpu.SEMAPHORE)
SPLIT_COPY_EFFECT = pltpu.SideEffectType.DATAFLOW_SIDE_EFFECTING


def _exchange_start(name, *arrs):
    n = len(arrs)

    def body(*refs):
        for cp in _grad_copies(refs[:n], refs[n:2 * n], refs[2 * n], refs[2 * n + 1]):
            cp.start()
        refs[-1][...] = jnp.zeros_like(refs[-1])

    hbm = lambda a: pltpu.with_memory_space_constraint(a, pltpu.HBM)
    bufs = (*arrs, *[lax.empty(a.shape, a.dtype) for a in arrs])
    return pl.pallas_call(
        body, name=name,
        out_shape=(pltpu.SemaphoreType.DMA((n * (N_DEV - 1),)), pltpu.SemaphoreType.DMA((n * (N_DEV - 1),)),
                   *[pltpu.HBM(a.shape, a.dtype) for a in bufs], jax.ShapeDtypeStruct((8, 128), F32)),
        in_specs=[HBM_SPEC] * (2 * n),
        out_specs=(SEM_SPEC, SEM_SPEC, *[HBM_SPEC] * (2 * n), pl.BlockSpec(memory_space=pltpu.VMEM)),
        input_output_aliases={i: i + 2 for i in range(2 * n)},
        compiler_params=pltpu.CompilerParams(has_side_effects=SPLIT_COPY_EFFECT),
    )(*[hbm(a) for a in bufs])


def _exchange_wait(name, send_sems, recv_sems, *bufs_after):
    *bufs, after = bufs_after
    n = len(bufs) // 2

    def body(*refs):
        for cp in _grad_copies(refs[:n], refs[n:2 * n], refs[2 * n], refs[2 * n + 1]):
            cp.wait_send()
            cp.wait_recv()

    return pl.pallas_call(
        body, name=name,
        out_shape=tuple(pltpu.HBM(a.shape, a.dtype) for a in bufs),
        in_specs=[HBM_SPEC] * (2 * n) + [SEM_SPEC, SEM_SPEC, pl.BlockSpec(memory_space=pl.ANY)],
        out_specs=(HBM_SPEC,) * (2 * n),
        input_output_aliases={i: i for i in range(2 * n)},
        compiler_params=pltpu.CompilerParams(has_side_effects=SPLIT_COPY_EFFECT),
    )(*bufs, send_sems, recv_sems, after)


def _gather_small(small):
    def body(sm_ref, ls_ref, send_sems, recv_sems, local_sem):
        x, y, c = _coords()
        me = 4 * x + 2 * y + c
        own = pltpu.make_async_copy(sm_ref, ls_ref.at[me], local_sem)
        own.start()
        sends = []
        for k in range(1, N_DEV):
            peer = (x ^ (k >> 2), y ^ ((k >> 1) & 1), c ^ (k & 1))
            sends.append(pltpu.make_async_remote_copy(
                src_ref=sm_ref, dst_ref=ls_ref.at[me], send_sem=send_sems.at[k - 1], recv_sem=recv_sems.at[k - 1],
                device_id=peer, device_id_type=MESH))
        for cp in sends:
            cp.start()
        for cp in sends:
            cp.wait_recv()
        for cp in sends:
            cp.wait_send()
        own.wait()

    vmem = pl.BlockSpec(memory_space=pltpu.VMEM)
    return pl.pallas_call(
        body, name="gather_small", in_specs=[vmem], out_specs=vmem,
        out_shape=jax.ShapeDtypeStruct((N_DEV,) + small.shape, F32),
        scratch_shapes=[pltpu.SemaphoreType.DMA((N_DEV - 1,)), pltpu.SemaphoreType.DMA((N_DEV - 1,)),
                        pltpu.SemaphoreType.DMA],
    )(small)


def _adamw(w, g, m, v):
    m = ADAM_B1 * m + (1.0 - ADAM_B1) * g
    v = ADAM_B2 * v + (1.0 - ADAM_B2) * (g * g)
    m_hat = m / (1.0 - ADAM_B1 ** ADAM_STEP)
    v_hat = v / (1.0 - ADAM_B2 ** ADAM_STEP)
    return -ADAM_LR * (m_hat / (jnp.sqrt(v_hat) + ADAM_EPS) + ADAM_WD * w), m, v


def _slot_sum(ref, own=None, me=None):
    g = None
    for i in range(N_DEV):
        term = ref[i].astype(F32)
        if own is not None:
            term = jnp.where(i == me, own, term)
        g = term if g is None else g + term
    return g


def _update_matrix(name, me, landed, own, w, m, v, rows):
    r, c = w.shape

    def body(me_ref, l_ref, own_ref, w_ref, m_ref, v_ref, g_ref, d_ref, nm_ref, nv_ref):
        g = _slot_sum(l_ref, own_ref[...].astype(F32), me_ref[0])
        g_ref[...] = g
        d_ref[...], nm_ref[...], nv_ref[...] = _adamw(w_ref[...], g, m_ref[...], v_ref[...])

    blk = pl.BlockSpec((rows, c), lambda i, me_ref: (i, 0))
    return pl.pallas_call(
        body, name=name,
        grid_spec=pltpu.PrefetchScalarGridSpec(
            num_scalar_prefetch=1, grid=(r // rows,),
            in_specs=[pl.BlockSpec((N_DEV, rows, c), lambda i, me_ref: (0, i, 0)),
                      pl.BlockSpec((None, rows, c), lambda i, me_ref: (me_ref[0], i, 0)), blk, blk, blk],
            out_specs=[blk] * 4),
        out_shape=[jax.ShapeDtypeStruct((r, c), F32)] * 4,
        compiler_params=_params("parallel"),
    )(me, landed, own, w, m, v)


def _update_small(landed, lb_logits, ws, ms, vs):
    def body(l_ref, lbl_ref, w_ref, m_ref, v_ref, g_ref, d_ref, nm_ref, nv_ref, loss_ref):
        tot = _slot_sum(l_ref)
        _, dlb = _lower_bound(lbl_ref[...])
        g_lb = tot[1:2, :SEC_W] * dlb
        g = jnp.concatenate([tot[0:1], jnp.concatenate([g_lb, -g_lb], axis=-1),
                             jnp.pad(tot[1:2, SEC_W:], ((0, 0), (0, SEC_W))), tot[2:3]], axis=0)
        g_ref[...] = g
        d_ref[...], nm_ref[...], nv_ref[...] = _adamw(w_ref[...], g, m_ref[...], v_ref[...])
        loss_ref[...] = tot[3:4, 0:1]

    vmem = pl.BlockSpec(memory_space=pltpu.VMEM)
    return pl.pallas_call(
        body, name="update_small", in_specs=[vmem] * 5, out_specs=[vmem] * 5,
        out_shape=[jax.ShapeDtypeStruct((4, D_MODEL), F32)] * 4 + [jax.ShapeDtypeStruct((1, 1), F32)],
    )(landed, lb_logits, ws, ms, vs)


def _pack_small(norm_w, lb_logits, hg_norm_w, final_norm_w):
    return jnp.concatenate([norm_w, lb_logits.reshape(1, D_MODEL),
                            jnp.pad(hg_norm_w, ((0, 0), (0, D_MODEL - SEC_W))), final_norm_w[None, :]], axis=0)


def _unpack_small(a):
    return a[0:1], a[1].reshape(2, SEC_W), a[2:3, :SEC_W], a[3]


def kernel(x, norm_w, w_in, hgrn_lb_logits, hg_norm_w, w_out, final_norm_w, loss_target, m_norm_w, m_w_in, m_hgrn_lb_logits, m_hg_norm_w, m_w_out, m_final_norm_w, v_norm_w, v_w_in, v_hgrn_lb_logits, v_hg_norm_w, v_w_out, v_final_norm_w):
    w_all = _gather_weights(w_in[0])
    wo_own = w_out[0].astype(MM)
    *gathering_wo, _ = _exchange_start("gather_start_w_out", jnp.broadcast_to(wo_own[None], (N_DEV,) + wo_own.shape))

    def wo_all_after(after):
        _, landed = _exchange_wait("gather_wait_w_out", *gathering_wo, after)
        return lax.dynamic_update_slice(landed, wo_own[None], (_me(), 0, 0)).reshape(D_MODEL, D_MODEL)

    flying_wo, flying_wi = [], []

    def start_w_out(gwo):
        *handles, token = _exchange_start("exchange_start_w_out", gwo.reshape(N_DEV, D_MODEL // N_DEV, D_MODEL))
        flying_wo.extend(handles)
        return token

    def start_w_in(gwi):
        *handles, token = _exchange_start("exchange_start_w_in", gwi)
        flying_wi.extend(handles)
        return token

    gx, small = _local_step(x[0], loss_target[0], norm_w, w_all, hgrn_lb_logits, hg_norm_w,
                            wo_all_after, final_norm_w, start_w_out, start_w_in)
    ls = _gather_small(small)
    gwo, lo = _exchange_wait("exchange_wait_w_out", *flying_wo, gx)
    gwi, li = _exchange_wait("exchange_wait_w_in", *flying_wi, gx)
    me = _me().astype(jnp.int32).reshape(1)
    g_wi, d_wi, nm_wi, nv_wi = _update_matrix("update_w_in", me, li, gwi, w_in[0], m_w_in[0], v_w_in[0], 256)
    g_wo, d_wo, nm_wo, nv_wo = _update_matrix("update_w_out", me, lo, gwo, w_out[0], m_w_out[0], v_w_out[0], 128)
    g_s, d_s, nm_s, nv_s, loss = _update_small(
        ls, hgrn_lb_logits, _pack_small(norm_w, hgrn_lb_logits, hg_norm_w, final_norm_w),
        _pack_small(m_norm_w, m_hgrn_lb_logits, m_hg_norm_w, m_final_norm_w),
        _pack_small(v_norm_w, v_hgrn_lb_logits, v_hg_norm_w, v_final_norm_w))
    outs = []
    for small_out, wi, wo in ((g_s, g_wi, g_wo), (d_s, d_wi, d_wo), (nm_s, nm_wi, nm_wo), (nv_s, nv_wi, nv_wo)):
        nw, lb, hg, fw = _unpack_small(small_out)
        outs += [nw, wi[None], lb, hg, wo[None], fw]
    return (loss[0, 0], gx[None], *outs)
```

```python
import functools

import jax
import jax.numpy as jnp
import numpy as np
from jax import lax
from jax.experimental import pallas as pl
from jax.experimental.pallas import tpu as pltpu

F32 = jnp.float32
MM = jnp.bfloat16
XCH = jnp.bfloat16
NORM_EPS = 1e-6
NEG = -1e30
N_DEV = 8
D_MODEL = 1024
N_SEC = 8
SEC_W = 512
HG_HEADS = 4
HG_D = 128
HG_GROUP = 4
AT_DH = 64
LANES = 128
ATT_BLK = 128
AT_COLS = 512
AT_QB = 8
DILATIONS = (1, 4, 16)
ROPE_THETA = 10000.0
CH = 16
LB_LO, LB_HI = 1e-6, 1.0 - 1e-6
ADAM_LR, ADAM_B1, ADAM_B2, ADAM_EPS, ADAM_WD, ADAM_STEP = 0.001, 0.9, 0.999, 1e-08, 0.01, 10
VMEM_LIMIT = 56 * 1024 * 1024
MESH = pl.DeviceIdType.MESH


def _params(*sem):
    return pltpu.CompilerParams(dimension_semantics=sem, vmem_limit_bytes=VMEM_LIMIT)


def _sigmoid(x):
    return 1.0 / (1.0 + jnp.exp(-x))


def _dot(a, b):
    return jnp.dot(a.astype(MM), b.astype(MM), preferred_element_type=F32)


def _dot_nt(a, b):
    return lax.dot_general(a.astype(MM), b.astype(MM), (((1,), (1,)), ((), ())), preferred_element_type=F32)


def _dot_tn(a, b):
    return lax.dot_general(a.astype(MM), b.astype(MM), (((0,), (0,)), ((), ())), preferred_element_type=F32)


def _split3(g):
    g1 = g.astype(jnp.bfloat16)
    r1 = g - g1.astype(F32)
    g2 = r1.astype(jnp.bfloat16)
    return g1, g2, (r1 - g2.astype(F32)).astype(jnp.bfloat16)


def _tri_dot(tri, g):
    t = tri.astype(jnp.bfloat16)
    g1, g2, g3 = _split3(g)
    d = functools.partial(jnp.dot, preferred_element_type=F32)
    return d(t, g1) + d(t, g2) + d(t, g3)


def _lower_bound(lbl):
    l0, l1 = lbl[0:1, :], lbl[1:2, :]
    m = jnp.maximum(l0, l1)
    e0, e1 = jnp.exp(l0 - m), jnp.exp(l1 - m)
    p = e0 / (e0 + e1)
    inside = (p >= LB_LO) & (p <= LB_HI)
    return jnp.clip(p, LB_LO, LB_HI), jnp.where(inside, p * (e1 / (e0 + e1)), 0.0)


def _iota2(shape, dim):
    return lax.broadcasted_iota(jnp.int32, shape, dim)


def _hgrn_gates(xq, xf, lb):
    sgq = _sigmoid(xq)
    sg = _sigmoid(xf)
    sn = _sigmoid(-xf)
    f = lb + (1.0 - lb) * sg
    return sgq, xq * sgq, sg, sn, f, (1.0 - lb) * sn


def _bdot(a, b, ca, cb):
    return lax.dot_general(a.astype(MM), b.astype(MM), (((ca,), (cb,)), ((0,), (0,))), preferred_element_type=F32)


def _chunk_masks(rb):
    row, col = _iota2((rb, rb), 0), _iota2((rb, rb), 1)
    same = (row // CH) == (col // CH)
    return same & (row >= col), same & (row <= col)


HALF = CH // 2
SLAB_ROWS = HALF * CH + (HALF // 2) * CH


def _write_slabs(slab_ref, g, q3, b3):
    slab = lambda t, rows: q3[:, rows, :] * jnp.exp(jnp.minimum(b3[:, rows, :] - b3[:, t:t + 1, :], 0.0))
    late = slice(HALF, CH)
    for t in range(HALF):
        slab_ref[g, :, t * CH:(t + 1) * CH, :] = slab(t, slice(0, CH)).astype(MM)
    for p in range(HALF // 2):
        t = HALF + 2 * p
        two = jnp.concatenate([slab(t, late), slab(t + 1, late)], axis=1)
        slab_ref[g, :, (HALF + p) * CH:(HALF + p + 1) * CH, :] = two.astype(MM)


def _read_diag(r):
    nc = r.shape[0]
    col, col_late = _iota2((nc, CH, CH), 2), _iota2((nc, HALF, CH), 2)
    a, a_late = jnp.zeros((nc, CH, CH), F32), jnp.zeros((nc, HALF, CH), F32)
    for t in range(HALF):
        a = a + jnp.where(col == t, r[:, t * CH:(t + 1) * CH, :], 0.0)
    for p in range(HALF // 2):
        t, two = HALF + 2 * p, r[:, (HALF + p) * CH:(HALF + p + 1) * CH, :]
        a_late = a_late + jnp.where(col_late == t, two[:, :HALF, :], 0.0) + jnp.where(col_late == t + 1, two[:, HALF:, :], 0.0)
    return a + jnp.concatenate([jnp.zeros_like(a_late), a_late], axis=1)


def _hgrn_fwd(proj, lb_logits, rb=256):
    s = proj.shape[1]
    nb, nc = s // rb, rb // CH

    def body(q_ref, f_ref, i_ref, lbl_ref, o_ref, sst_ref, a_ref, st_ref, slab_ref, states_ref):
        @pl.when(pl.program_id(1) == 0)
        def _():
            st_ref[...] = jnp.zeros_like(st_ref)

        sst_ref[...] = st_ref[...]
        prefix, _ = _chunk_masks(rb)
        c3 = lambda a: a.reshape(nc, CH, HG_D)
        row, col = _iota2((nc, CH, CH), 1), _iota2((nc, CH, CH), 2)
        heads = []
        for g in range(HG_GROUP):
            hs = slice(g * HG_D, (g + 1) * HG_D)
            lb, _ = _lower_bound(lbl_ref[:, hs])
            _, q, _, _, f, kk = _hgrn_gates(q_ref[:, hs], f_ref[:, hs], lb)
            b3 = c3(_tri_dot(prefix, jnp.log(f)))
            q3, kk3, v3 = c3(q), c3(kk), c3(i_ref[:, hs])
            bl3 = b3[:, CH - 1:CH, :]
            _write_slabs(slab_ref, g, q3, b3)
            x_upd = _bdot(v3, kk3 * jnp.exp(bl3 - b3), 1, 1)
            heads.append(dict(hs=hs, kk3=kk3, v3=v3, qe3=q3 * jnp.exp(b3), ebl3=jnp.exp(bl3), x_upd=x_upd))
        for g, hd in enumerate(heads):
            st = st_ref[g]
            for c in range(nc):
                states_ref[g, c] = st
                st = st * hd["ebl3"][c] + hd["x_upd"][c]
            st_ref[g] = st
        for g, hd in enumerate(heads):
            a = _read_diag(_bdot(slab_ref[g], hd["kk3"], 2, 2))
            a = jnp.where(row >= col, a, 0.0)
            a_ref[:, g * CH:(g + 1) * CH] = a.reshape(rb, CH)
            o3 = _bdot(hd["qe3"], states_ref[g], 2, 2) + _bdot(a, hd["v3"], 2, 1)
            o_ref[:, hd["hs"]] = o3.reshape(rb, HG_D)

    wide = HG_GROUP * HG_D
    sec = lambda j: pl.BlockSpec((None, rb, wide), lambda h, i, j=j: (j, i, h))
    return pl.pallas_call(
        body, name="hgrn_fwd", grid=(HG_HEADS // HG_GROUP, nb),
        in_specs=[sec(0), sec(1), sec(2), pl.BlockSpec((2, wide), lambda h, i: (0, h))],
        out_specs=[pl.BlockSpec((rb, wide), lambda h, i: (i, h)),
                   pl.BlockSpec((None, HG_GROUP, HG_D, HG_D), lambda h, i: (i, h, 0, 0)),
                   pl.BlockSpec((rb, HG_GROUP * CH), lambda h, i: (i, h))],
        out_shape=[jax.ShapeDtypeStruct((s, SEC_W), F32),
                   jax.ShapeDtypeStruct((nb, HG_HEADS, HG_D, HG_D), F32),
                   jax.ShapeDtypeStruct((s, HG_HEADS * CH), F32)],
        scratch_shapes=[pltpu.VMEM((HG_GROUP, HG_D, HG_D), F32), pltpu.VMEM((HG_GROUP, nc, SLAB_ROWS, HG_D), MM),
                        pltpu.VMEM((HG_GROUP, nc, HG_D, HG_D), F32)],
        compiler_params=_params("parallel", "arbitrary"),
    )(proj, proj, proj, lb_logits)


def _hgrn_bwd(proj, lb_logits, d_o, sst, a_in, token, rb=256):
    s = proj.shape[1]
    nb, nc = s // rb, rb // CH

    def body(q_ref, f_ref, i_ref, lbl_ref, do_ref, sst_ref, a_ref, tok_ref, dxq_ref, dxf_ref, dxi_ref, dlb_ref,
             dst_ref, states_ref, dstates_ref, lslab_ref, kslab_ref):
        @pl.when(pl.program_id(1) == 0)
        def _():
            dst_ref[...] = jnp.zeros_like(dst_ref)
            dlb_ref[...] = jnp.zeros_like(dlb_ref) + tok_ref[0:1, 0:1]

        prefix, suffix = _chunk_masks(rb)
        c3 = lambda a: a.reshape(nc, CH, HG_D)
        flat = lambda a: a.reshape(rb, HG_D)
        row, col = _iota2((nc, CH, CH), 1), _iota2((nc, CH, CH), 2)
        tril, triu = row >= col, row <= col
        sel = (_iota2((CH, CH * CH), 1) % CH == _iota2((CH, CH * CH), 0)).astype(MM)
        blockdiag = _iota2((nc, CH, CH * CH), 2) // CH == _iota2((nc, CH, CH * CH), 1)
        tile = lambda m: jnp.where(blockdiag, _dot(m.reshape(rb, CH), sel).reshape(nc, CH, CH * CH), 0.0)
        last = _iota2((nc, CH, HG_D), 1) == CH - 1
        heads = []
        for g in range(HG_GROUP):
            hs = slice(g * HG_D, (g + 1) * HG_D)
            lb, _ = _lower_bound(lbl_ref[:, hs])
            xq = q_ref[:, hs]
            sgq, q, sg, sn, f, kk = _hgrn_gates(xq, f_ref[:, hs], lb)
            b3 = c3(_tri_dot(prefix, jnp.log(f)))
            q3, kk3, v3, do3 = c3(q), c3(kk), c3(i_ref[:, hs]), c3(do_ref[:, hs])
            bl3 = b3[:, CH - 1:CH, :]
            eb3, ebl3, dec3 = jnp.exp(b3), jnp.exp(bl3), jnp.exp(bl3 - b3)
            qe3, kd3 = q3 * eb3, kk3 * dec3
            x_upd, y_upd = _bdot(v3, kd3, 1, 1), _bdot(do3, qe3, 1, 1)
            zero, every, early, late = jnp.zeros((nc, HALF, HG_D), F32), slice(0, CH), slice(0, HALF), slice(HALF, CH)
            for t in range(CH):
                bt = b3[:, t:t + 1, :]
                since = lambda rows: q3[:, rows, :] * jnp.exp(jnp.minimum(b3[:, rows, :] - bt, 0.0))
                until = lambda rows: kk3[:, rows, :] * jnp.exp(jnp.minimum(bt - b3[:, rows, :], 0.0))
                if t < HALF:
                    lv, kv = since(every), jnp.concatenate([until(early), zero], axis=1)
                else:
                    lv, kv = jnp.concatenate([zero, since(late)], axis=1), until(every)
                lslab_ref[g, :, t * CH:(t + 1) * CH, :] = lv.astype(MM)
                kslab_ref[g, :, t * CH:(t + 1) * CH, :] = kv.astype(MM)
            d_a = jnp.where(tril, _bdot(do3, v3, 2, 2), 0.0)
            d_at = jnp.where(triu, _bdot(v3, do3, 2, 2), 0.0)
            heads.append(dict(hs=hs, lb=lb, xq=xq, sgq=sgq, sg=sg, sn=sn, f=f, q3=q3, kk3=kk3, v3=v3, do3=do3,
                              eb3=eb3, ebl3=ebl3, dec3=dec3, qe3=qe3, kd3=kd3, x_upd=x_upd, y_upd=y_upd,
                              d_a=d_a, d_at=d_at))
        for g, hd in enumerate(heads):
            st = sst_ref[g]
            for c in range(nc):
                states_ref[g, c] = st
                st = st * hd["ebl3"][c] + hd["x_upd"][c]
            dst = dst_ref[g]
            for c in reversed(range(nc)):
                dstates_ref[g, c] = dst
                dst = dst * hd["ebl3"][c] + hd["y_upd"][c]
            dst_ref[g] = dst
        for g, hd in enumerate(heads):
            q3, v3, do3, kd3 = hd["q3"], hd["v3"], hd["do3"], hd["kd3"]
            states, dstates = states_ref[g], dstates_ref[g]
            hd["dqe"] = _bdot(do3, states, 2, 1)
            hd["dkd"] = _bdot(v3, dstates, 2, 1)
            a = a_ref[:, g * CH:(g + 1) * CH].reshape(nc, CH, CH)
            hd["dv"] = _bdot(kd3, dstates, 2, 2) + _bdot(a, do3, 1, 1)
            hd["dq_in"] = _bdot(tile(hd["d_a"]), kslab_ref[g], 2, 1)
            hd["dk_in"] = _bdot(tile(hd["d_at"]), lslab_ref[g], 2, 1)
            hd["ss"] = jnp.sum(dstates * states, axis=1, keepdims=True)
        for g, hd in enumerate(heads):
            q3, kk3, eb3, ebl3, dec3, qe3, kd3 = (hd[k] for k in ("q3", "kk3", "eb3", "ebl3", "dec3", "qe3", "kd3"))
            dqe, dkd, dq_in, dk_in = hd["dqe"], hd["dkd"], hd["dq_in"], hd["dk_in"]
            dkd_kd = dkd * kd3
            db = dqe * qe3 - dkd_kd + q3 * dq_in - kk3 * dk_in
            dbl = jnp.sum(dkd_kd, axis=1, keepdims=True) + hd["ss"] * ebl3
            dg = _tri_dot(suffix, flat(db + jnp.where(last, dbl, 0.0)))
            df = dg / hd["f"] - flat(dkd * dec3 + dk_in)
            xq, sgq, hs = hd["xq"], hd["sgq"], hd["hs"]
            dxq_ref[:, hs] = (flat(dqe * eb3 + dq_in) * (sgq * (1.0 + xq * (1.0 - sgq)))).astype(MM)
            dxf_ref[:, hs] = (df * (1.0 - hd["lb"]) * hd["sg"] * hd["sn"]).astype(MM)
            dxi_ref[:, hs] = flat(hd["dv"]).astype(MM)
            dlb_ref[:, hs] += jnp.sum(df * hd["sn"], axis=0, keepdims=True)

    wide = HG_GROUP * HG_D
    rev = lambda i: nb - 1 - i
    sec = lambda j: pl.BlockSpec((None, rb, wide), lambda h, i, j=j: (j, rev(i), h))
    blk = pl.BlockSpec((rb, wide), lambda h, i: (rev(i), h))
    state = (pltpu.VMEM((HG_GROUP, nc, HG_D, HG_D), F32), pltpu.VMEM((HG_GROUP, nc, CH * CH, HG_D), MM))
    return pl.pallas_call(
        body, name="hgrn_bwd", grid=(HG_HEADS // HG_GROUP, nb),
        in_specs=[sec(0), sec(1), sec(2), pl.BlockSpec((2, wide), lambda h, i: (0, h)), blk,
                  pl.BlockSpec((None, HG_GROUP, HG_D, HG_D), lambda h, i: (rev(i), h, 0, 0)),
                  pl.BlockSpec((rb, HG_GROUP * CH), lambda h, i: (rev(i), h)),
                  pl.BlockSpec((8, 128), lambda h, i: (0, 0))],
        out_specs=[blk, blk, blk, pl.BlockSpec((1, wide), lambda h, i: (0, h))],
        out_shape=[jax.ShapeDtypeStruct((s, SEC_W), MM)] * 3 + [jax.ShapeDtypeStruct((1, SEC_W), F32)],
        scratch_shapes=[pltpu.VMEM((HG_GROUP, HG_D, HG_D), F32), state[0], state[0], state[1], state[1]],
        compiler_params=_params("parallel", "arbitrary"),
    )(proj, proj, proj, lb_logits, d_o, sst, a_in, token)


def _rope_tables(s):
    half = AT_DH // 2
    inv_freq = np.float32(1.0) / (np.float32(ROPE_THETA) ** (np.arange(half, dtype=np.float32) / np.float32(half)))
    ang = np.arange(s, dtype=np.float32)[:, None] * inv_freq[None, :]
    cos, sin = np.cos(ang), np.sin(ang)
    return np.concatenate([cos] * 4, axis=-1), np.concatenate([-sin, sin] * 2, axis=-1)


def _rope128(x, cos, sin):
    lo = (_iota2(x.shape, 1) % AT_DH) < AT_DH // 2
    rot = jnp.where(lo, pltpu.roll(x, LANES - AT_DH // 2, 1), pltpu.roll(x, AT_DH // 2, 1))
    return x * cos + rot * sin


LANE_GROUPS = SEC_W // LANES


def _set_lanes(ref, val):
    for j in range(LANE_GROUPS):
        ref[j] = val[:, j * LANES:(j + 1) * LANES]


def _get_lanes(ref):
    return jnp.concatenate([ref[j] for j in range(LANE_GROUPS)], axis=-1)


def _to_view(src_ref, dst_ref, d):
    n = src_ref.shape[1] // d
    for r in range(d):
        rows = pl.ds(r, n, stride=d) if d > 1 else slice(None)
        for j in range(LANE_GROUPS):
            c0 = r * SEC_W + j * LANES
            dst_ref[:, c0:c0 + LANES] = src_ref.at[j][rows, :].astype(dst_ref.dtype)


def _from_view(src_ref, dst_ref, d):
    n = dst_ref.shape[1] // d
    for r in range(d):
        for j in range(LANE_GROUPS):
            c0 = r * SEC_W + j * LANES
            dst_ref.at[j][pl.ds(r, n, stride=d), :] = src_ref[:, c0:c0 + LANES].astype(dst_ref.dtype)


def _view_spec(tm, d):
    return pl.BlockSpec((tm // d, d * SEC_W), lambda i: (i, 0))


def _view_shape(s, d, dtype):
    return jax.ShapeDtypeStruct((s // d, d * SEC_W), dtype)


PROJ_KEPT = (0, 1, 2, 3, 7)


def _inproj_fwd(x, norm_w, w_all, cos, sin, tm=512):
    s = x.shape[0]

    def body(x_ref, nw_ref, w_ref, cos_ref, sin_ref, proj_ref, *refs):
        outs, (qs_ref, ks_ref, vs_ref) = refs[:-3], refs[-3:]
        xv = x_ref[...]
        rstd = lax.rsqrt(jnp.mean(xv * xv, axis=-1, keepdims=True) + NORM_EPS)
        u = (xv * rstd * nw_ref[...]).astype(MM)
        for slot, j in enumerate(PROJ_KEPT):
            proj_ref[slot] = jnp.dot(u, w_ref[j], preferred_element_type=F32)
        q, k, v = [jnp.dot(u, w_ref[j], preferred_element_type=F32) for j in (4, 5, 6)]
        c, sn = cos_ref[...], sin_ref[...]
        for g in range(LANE_GROUPS):
            sl = slice(g * LANES, (g + 1) * LANES)
            qs_ref[g] = _rope128(q[:, sl], c, sn) * (AT_DH ** -0.5)
            ks_ref[g] = _rope128(k[:, sl], c, sn)
            vs_ref[g] = v[:, sl]
        for i, d in enumerate(DILATIONS):
            for src_ref, dst_ref in zip((qs_ref, ks_ref, vs_ref), outs[3 * i:3 * i + 3]):
                _to_view(src_ref, dst_ref, d)

    tab = pl.BlockSpec((tm, LANES), lambda i: (i, 0))
    return pl.pallas_call(
        body, name="inproj_fwd", grid=(s // tm,),
        in_specs=[pl.BlockSpec((tm, D_MODEL), lambda i: (i, 0)),
                  pl.BlockSpec((1, D_MODEL), lambda i: (0, 0)),
                  pl.BlockSpec((N_SEC, D_MODEL, SEC_W), lambda i: (0, 0, 0)), tab, tab],
        out_specs=[pl.BlockSpec((len(PROJ_KEPT), tm, SEC_W), lambda i: (0, i, 0))]
                  + [_view_spec(tm, d) for d in DILATIONS for _ in range(3)],
        out_shape=[jax.ShapeDtypeStruct((len(PROJ_KEPT), s, SEC_W), F32)]
                  + [_view_shape(s, d, MM) for d in DILATIONS for _ in range(3)],
        scratch_shapes=[pltpu.VMEM((LANE_GROUPS, tm, LANES), F32)] * 3,
        compiler_params=_params("parallel"),
    )(x, norm_w, w_all, cos, sin)


def _band_mask(first_ok, second_ok):
    row, col = _iota2((ATT_BLK, 2 * ATT_BLK), 0), _iota2((ATT_BLK, 2 * ATT_BLK), 1)
    return ((col < ATT_BLK) & (col >= row) & first_ok) | ((col >= ATT_BLK) & ((col - ATT_BLK) <= row) & second_ok)


def _own_lanes(rows, h):
    lane = _iota2((rows, LANES), 1)
    return (lane < AT_DH) if h == 0 else (lane >= AT_DH)


def _neg_pieces(rows, h):
    lane = _iota2((rows, LANES), 1) - (AT_DH if h == 0 else 0)
    return jnp.where((lane >= 0) & (lane < 3), -1.0, 0.0).astype(MM)


def _units(qb):
    return [(b, slice(g * LANES, (g + 1) * LANES), h) for b in range(qb) for g in range(AT_COLS // LANES) for h in range(2)]


def _sub(b):
    return slice(b * ATT_BLK, (b + 1) * ATT_BLK)


def _band_before(cur_ref, prev_ref, b, sl):
    if b == 0:
        return jnp.concatenate([prev_ref[:, sl], cur_ref[0:ATT_BLK, sl]], axis=0)
    return cur_ref[(b - 1) * ATT_BLK:(b + 1) * ATT_BLK, sl]


def _band_after(cur_ref, next_ref, b, sl):
    if (b + 1) * ATT_BLK == cur_ref.shape[0]:
        return jnp.concatenate([cur_ref[b * ATT_BLK:(b + 1) * ATT_BLK, sl], next_ref[:, sl]], axis=0)
    return cur_ref[b * ATT_BLK:(b + 2) * ATT_BLK, sl]


def _attn_specs(rows):
    qb = min(AT_QB, rows // ATT_BLK)
    assert rows % (qb * ATT_BLK) == 0
    last = rows // ATT_BLK - 1
    cur = pl.BlockSpec((qb * ATT_BLK, AT_COLS), lambda c, n: (n, c))
    prev = pl.BlockSpec((ATT_BLK, AT_COLS), lambda c, n: (jnp.maximum(qb * n - 1, 0), c))
    nxt = pl.BlockSpec((ATT_BLK, AT_COLS), lambda c, n: (jnp.minimum(qb * (n + 1), last), c))
    return qb, cur, prev, nxt


def _stack_heads(a):
    h0 = _own_lanes(a.shape[0], 0)
    zero = jnp.zeros_like(a)
    return jnp.concatenate([jnp.where(h0, a, zero), jnp.where(h0, zero, a)], axis=0)


def _unstack_heads(a2):
    return jnp.where(_own_lanes(ATT_BLK, 0), a2[:ATT_BLK], a2[ATT_BLK:])


def _attn_fwd(qr, kr, vr, d):
    rows, cols = qr.shape
    qb, cur, prev, nxt = _attn_specs(rows)
    nb = rows // (qb * ATT_BLK)

    def body(q_ref, kc_ref, kp_ref, vc_ref, vp_ref, o_ref, lse_ref):
        twice = lambda m: jnp.concatenate([m, m], axis=0)
        masks = {True: twice(_band_mask(pl.program_id(1) > 0, True)), False: twice(_band_mask(True, True))}
        ones = jnp.ones((2 * ATT_BLK, LANES), MM)
        units = [(b, sl) for b, sl, h in _units(qb) if h == 0]
        scs = [jnp.where(masks[b == 0], _dot_nt(_stack_heads(q_ref[_sub(b), sl]), _band_before(kc_ref, kp_ref, b, sl)),
                         NEG) for b, sl in units]
        ms = [jnp.max(sc, axis=-1, keepdims=True) for sc in scs]
        ps = [jnp.exp(sc - m).astype(MM) for sc, m in zip(scs, ms)]
        ols = [jnp.dot(p, jnp.concatenate([_band_before(vc_ref, vp_ref, b, sl), ones], axis=1),
                       preferred_element_type=F32) for p, (b, sl) in zip(ps, units)]
        for (b, sl), m, ol in zip(units, ms, ols):
            l = _unstack_heads(ol[:, LANES:])
            o_ref[_sub(b), sl] = _unstack_heads(ol[:, :LANES]) / l
            lse_ref[_sub(b), sl] = _unstack_heads(jnp.broadcast_to(m, (2 * ATT_BLK, LANES))) + jnp.log(l)

    o, lse = pl.pallas_call(
        body, name=f"attn_fwd_d{d}", grid=(cols // AT_COLS, nb),
        in_specs=[cur, cur, prev, cur, prev], out_specs=[cur, cur],
        out_shape=[jax.ShapeDtypeStruct((rows, cols), F32)] * 2,
        compiler_params=_params("parallel", "parallel"),
    )(qr, kr, kr, vr, vr)
    return o, lse


def _attn_bwd_dq(qr, kr, vr, do, lse, delta, d):
    rows, cols = qr.shape
    qb, cur, prev, nxt = _attn_specs(rows)
    nb = rows // (qb * ATT_BLK)

    def body(q_ref, kc_ref, kp_ref, vc_ref, vp_ref, do_ref, lse_ref, dl_ref, dq_ref):
        masks = {True: _band_mask(pl.program_id(1) > 0, True), False: _band_mask(True, True)}
        units = _units(qb)
        sms, dps = [], []
        for b, sl, h in units:
            own, own_b, neg = _own_lanes(ATT_BLK, h), _own_lanes(2 * ATT_BLK, h), _neg_pieces(2 * ATT_BLK, h)
            sms.append(_dot_nt(jnp.where(own, q_ref[_sub(b), sl], lse_ref[_sub(b), sl]),
                               jnp.where(own_b, _band_before(kc_ref, kp_ref, b, sl), neg)))
            dps.append(_dot_nt(jnp.where(own, do_ref[_sub(b), sl], dl_ref[_sub(b), sl]),
                               jnp.where(own_b, _band_before(vc_ref, vp_ref, b, sl), neg)))
        dss = [(jnp.exp(jnp.where(masks[b == 0], sm, NEG)) * dp).astype(MM)
               for sm, dp, (b, _, _) in zip(sms, dps, units)]
        dqs = [jnp.dot(ds, _band_before(kc_ref, kp_ref, b, sl), preferred_element_type=F32) * (AT_DH ** -0.5)
               for ds, (b, sl, _) in zip(dss, units)]
        for i in range(0, len(units), 2):
            b, sl, _ = units[i]
            dq_ref[_sub(b), sl] = jnp.where(_own_lanes(ATT_BLK, 0), dqs[i], dqs[i + 1]).astype(dq_ref.dtype)

    dq = pl.pallas_call(
        body, name=f"attn_bwd_dq_d{d}", grid=(cols // AT_COLS, nb),
        in_specs=[cur, cur, prev, cur, prev, cur, cur, cur], out_specs=cur,
        out_shape=jax.ShapeDtypeStruct((rows, cols), MM),
        compiler_params=_params("parallel", "parallel"),
    )(qr, kr, kr, vr, vr, do, lse, delta)
    return dq


def _attn_bwd_dkv(qr, kr, vr, do, lse, delta, d):
    rows, cols = qr.shape
    qb, cur, prev, nxt = _attn_specs(rows)
    nb = rows // (qb * ATT_BLK)

    def body(k_ref, v_ref, qc_ref, qn_ref, doc_ref, don_ref, lsec_ref, lsen_ref, dlc_ref, dln_ref,
             dk_ref, dv_ref):
        masks = {True: _band_mask(True, pl.program_id(1) < nb - 1), False: _band_mask(True, True)}
        units = _units(qb)
        sms, dps = [], []
        for b, sl, h in units:
            own, own_b, neg = _own_lanes(ATT_BLK, h), _own_lanes(2 * ATT_BLK, h), _neg_pieces(ATT_BLK, h)
            sms.append(_dot_nt(jnp.where(own, k_ref[_sub(b), sl], neg),
                               jnp.where(own_b, _band_after(qc_ref, qn_ref, b, sl),
                                         _band_after(lsec_ref, lsen_ref, b, sl))))
            dps.append(_dot_nt(jnp.where(own, v_ref[_sub(b), sl], neg),
                               jnp.where(own_b, _band_after(doc_ref, don_ref, b, sl),
                                         _band_after(dlc_ref, dln_ref, b, sl))))
        ps = [jnp.exp(jnp.where(masks[b == qb - 1], sm, NEG)) for sm, (b, _, _) in zip(sms, units)]
        dss = [(p * dp).astype(MM) for p, dp in zip(ps, dps)]
        dvs = [jnp.dot(p.astype(MM), _band_after(doc_ref, don_ref, b, sl), preferred_element_type=F32)
               for p, (b, sl, _) in zip(ps, units)]
        dks = [jnp.dot(ds, _band_after(qc_ref, qn_ref, b, sl), preferred_element_type=F32)
               for ds, (b, sl, _) in zip(dss, units)]
        head0 = _own_lanes(ATT_BLK, 0)
        for i in range(0, len(units), 2):
            b, sl, _ = units[i]
            dk_ref[_sub(b), sl] = jnp.where(head0, dks[i], dks[i + 1]).astype(dk_ref.dtype)
            dv_ref[_sub(b), sl] = jnp.where(head0, dvs[i], dvs[i + 1]).astype(dv_ref.dtype)

    dk, dv = pl.pallas_call(
        body, name=f"attn_bwd_dkv_d{d}", grid=(cols // AT_COLS, nb),
        in_specs=[cur, cur, cur, nxt, cur, nxt, cur, nxt, cur, nxt], out_specs=[cur, cur],
        out_shape=[jax.ShapeDtypeStruct((rows, cols), MM)] * 2,
        compiler_params=_params("parallel", "parallel"),
    )(kr, vr, qr, qr, do, do, lse, lse, delta, delta)
    return dk, dv


def _head_sum(a, width):
    parts = []
    for j in range(a.shape[1] // width):
        sm = jnp.sum(a[:, j * width:(j + 1) * width], axis=-1, keepdims=True)
        parts.append(jnp.broadcast_to(sm, (a.shape[0], width)))
    return jnp.concatenate(parts, axis=-1)


def _partner_sum(a):
    swap = (_iota2((LANES, LANES), 0) // AT_DH != _iota2((LANES, LANES), 1) // AT_DH).astype(jnp.bfloat16)
    d = functools.partial(jnp.dot, preferred_element_type=F32)
    parts = _split3(a)
    return jnp.concatenate([d(parts[0][:, sl], swap) + d(parts[1][:, sl], swap) + d(parts[2][:, sl], swap)
                            for sl in (slice(j * LANES, (j + 1) * LANES) for j in range(a.shape[1] // LANES))], axis=-1)


def _partner_value(x):
    return jnp.concatenate([pltpu.roll(x[:, j * LANES:(j + 1) * LANES], AT_DH, 1) for j in range(x.shape[1] // LANES)],
                           axis=-1)


def _pieces(xs):
    hi = xs.astype(jnp.bfloat16).astype(F32)
    mid = (xs - hi).astype(jnp.bfloat16).astype(F32)
    lo = (xs - hi - mid).astype(jnp.bfloat16).astype(F32)
    lane = _iota2(xs.shape, 1) % AT_DH
    return jnp.where(lane == 0, hi, jnp.where(lane == 1, mid, jnp.where(lane == 2, lo, 0.0)))


def _mid(x, tgt, proj, o_hg, o_at, lse_at, hg_norm_w, final_norm_w, wo_all, tm=256):
    s = x.shape[0]
    nb = s // tm

    def body(x_ref, t_ref, hgz_ref, atz_ref, ohg_ref, o1_ref, o2_ref, o3_ref, l1_ref, l2_ref, l3_ref,
             g_ref, fw_ref, wo_ref,
             dh_ref, dohg_ref, dhgz_ref, datz_ref, do1_ref, do2_ref, do3_ref, dl1_ref, dl2_ref, dl3_ref,
             lp1_ref, lp2_ref, lp3_ref,
             gwo_ref, gfw_ref, ghg_ref, loss_ref, nat_ref, stage_ref, gwo_acc):
        @pl.when(pl.program_id(0) == 0)
        def _():
            gwo_acc[...] = jnp.zeros_like(gwo_acc)
            gfw_ref[...] = jnp.zeros_like(gfw_ref)
            ghg_ref[...] = jnp.zeros_like(ghg_ref)
            loss_ref[...] = jnp.zeros_like(loss_ref)

        ohg, g = ohg_ref[...], g_ref[...]
        rs = lax.rsqrt(_head_sum(ohg * ohg, HG_D) * (1.0 / HG_D) + NORM_EPS)
        on = ohg * rs
        hgz = hgz_ref[...]
        sz = _sigmoid(hgz)
        gate_hg = hgz * sz
        lses, outs = [l1_ref[...]], [o1_ref[...]]
        for k, (d, l_ref, o_ref) in enumerate(zip(DILATIONS[1:], (l2_ref, l3_ref), (o2_ref, o3_ref))):
            _from_view(l_ref, nat_ref.at[2 * k], d)
            _from_view(o_ref, nat_ref.at[2 * k + 1], d)
            lses.append(_get_lanes(nat_ref.at[2 * k]))
            outs.append(_get_lanes(nat_ref.at[2 * k + 1]))
        mx = jnp.maximum(jnp.maximum(lses[0], lses[1]), lses[2])
        es = [jnp.exp(l - mx) for l in lses]
        den = es[0] + es[1] + es[2]
        ws = [e / den for e in es]
        oat = ws[0] * outs[0] + ws[1] * outs[1] + ws[2] * outs[2]
        atz = atz_ref[...]
        sa = _sigmoid(atz)
        gate_at = atz * sa
        mixed = jnp.concatenate([on * g * gate_hg, oat * gate_at], axis=-1).astype(MM)
        h = x_ref[...] + jnp.dot(mixed, wo_ref[...], preferred_element_type=F32)
        rstd = lax.rsqrt(jnp.mean(h * h, axis=-1, keepdims=True) + NORM_EPS)
        hn = h * rstd
        fw = fw_ref[...]
        err = hn * fw - t_ref[...]
        loss_ref[...] += 0.5 * jnp.sum(jnp.mean(err * err, axis=-1, keepdims=True), axis=0, keepdims=True)
        dout = err * (1.0 / D_MODEL)
        gfw_ref[...] += jnp.sum(dout * hn, axis=0, keepdims=True)
        dhn = dout * fw
        dh = rstd * (dhn - hn * jnp.mean(dhn * hn, axis=-1, keepdims=True))
        dh_ref[...] = dh
        dh_mm = dh.astype(MM)
        gwo_acc[...] += _dot_tn(mixed, dh_mm)

        @pl.when(pl.program_id(0) == nb - 1)
        def _():
            gwo_ref[...] = gwo_acc[...].astype(gwo_ref.dtype)

        dmixed = _dot_nt(dh_mm, wo_ref[...])
        dm_hg = dmixed[:, :SEC_W]
        d_ong = dm_hg * gate_hg
        dhgz_ref[...] = (dm_hg * (on * g) * (sz * (1.0 + hgz * (1.0 - sz)))).astype(MM)
        ghg_ref[...] += jnp.sum(d_ong * on, axis=0, keepdims=True)
        d_on = d_ong * g
        dohg_ref[...] = rs * (d_on - on * (_head_sum(d_on * on, HG_D) * (1.0 / HG_D)))
        dm_at = dmixed[:, SEC_W:]
        d_oat = dm_at * gate_at
        datz_ref[...] = (dm_at * oat * (sa * (1.0 + atz * (1.0 - sa)))).astype(MM)
        lse_all = mx + jnp.log(den)
        for val, dst_refs in ((d_oat, (do1_ref, do2_ref, do3_ref)),
                              (_pieces(_partner_sum(d_oat * oat)), (dl1_ref, dl2_ref, dl3_ref)),
                              (_pieces(_partner_value(lse_all)), (lp1_ref, lp2_ref, lp3_ref))):
            _set_lanes(stage_ref, val)
            for d, dst_ref in zip(DILATIONS, dst_refs):
                _to_view(stage_ref, dst_ref, d)

    row = lambda w: pl.BlockSpec((tm, w), lambda i: (i, 0))
    sec = lambda j: pl.BlockSpec((None, tm, SEC_W), lambda i, j=j: (j, i, 0))
    const = lambda shp: pl.BlockSpec(shp, lambda i: (0,) * len(shp))
    half = row(SEC_W)
    views = [_view_spec(tm, d) for d in DILATIONS]
    return pl.pallas_call(
        body, name="mid", grid=(nb,),
        in_specs=[row(D_MODEL), row(D_MODEL), sec(PROJ_KEPT.index(3)), sec(PROJ_KEPT.index(7)), half] + views * 2
                 + [const((1, SEC_W)), const((1, D_MODEL)), const((D_MODEL, D_MODEL))],
        out_specs=[row(D_MODEL)] + [half] * 3 + views * 3
                  + [const((D_MODEL, D_MODEL)), const((1, D_MODEL)), const((1, SEC_W)), const((1, 1))],
        out_shape=[jax.ShapeDtypeStruct((s, D_MODEL), F32), jax.ShapeDtypeStruct((s, SEC_W), F32)]
                  + [jax.ShapeDtypeStruct((s, SEC_W), MM)] * 2
                  + [_view_shape(s, d, MM) for d in DILATIONS] * 3
                  + [jax.ShapeDtypeStruct((D_MODEL, D_MODEL), XCH), jax.ShapeDtypeStruct((1, D_MODEL), F32),
                     jax.ShapeDtypeStruct((1, SEC_W), F32), jax.ShapeDtypeStruct((1, 1), F32)],
        scratch_shapes=[pltpu.VMEM((4, LANE_GROUPS, tm, LANES), F32), pltpu.VMEM((LANE_GROUPS, tm, LANES), F32),
                        pltpu.VMEM((D_MODEL, D_MODEL), F32)],
        compiler_params=_params("arbitrary"),
    )(x, tgt, proj, proj, o_hg, *o_at, *lse_at, hg_norm_w, final_norm_w, wo_all)


def _section_specs(dsecs, tm):
    return [pl.BlockSpec((tm, SEC_W), lambda i: (i, 0)) if k is None
            else pl.BlockSpec((None, tm, SEC_W), lambda i, k=k: (k, i, 0)) for _, k in dsecs]


def _inproj_bwd_x(x, norm_w, w_all, dh, dsecs, token, tm=512):
    s = x.shape[0]

    def body(x_ref, nw_ref, w_ref, dh_ref, tok_ref, *refs):
        sec_refs, (gx_ref, gnw_ref) = refs[:N_SEC], refs[N_SEC:]

        @pl.when(pl.program_id(0) == 0)
        def _():
            gnw_ref[...] = jnp.zeros_like(gnw_ref)

        du = jnp.zeros((tm, D_MODEL), F32)
        for j in range(N_SEC):
            du = du + _dot_nt(sec_refs[j][...], w_ref[j])
        xv, nw = x_ref[...], nw_ref[...]
        rstd = lax.rsqrt(jnp.mean(xv * xv, axis=-1, keepdims=True) + NORM_EPS)
        xn = xv * rstd
        gnw_ref[...] += jnp.sum(du * xn, axis=0, keepdims=True)
        dxn = du * nw
        dx = rstd * (dxn - xn * jnp.mean(dxn * xn, axis=-1, keepdims=True))
        gx_ref[...] = (dh_ref[...] + tok_ref[0:1, 0:1]) + dx

    row = lambda w: pl.BlockSpec((tm, w), lambda i: (i, 0))
    const = lambda shp: pl.BlockSpec(shp, lambda i: (0,) * len(shp))
    return pl.pallas_call(
        body, name="inproj_bwd_x", grid=(s // tm,),
        in_specs=[row(D_MODEL), const((1, D_MODEL)), const((N_SEC, D_MODEL, SEC_W)), row(D_MODEL), const((8, 128))]
                 + _section_specs(dsecs, tm),
        out_specs=[row(D_MODEL), const((1, D_MODEL))],
        out_shape=[jax.ShapeDtypeStruct((s, D_MODEL), F32), jax.ShapeDtypeStruct((1, D_MODEL), F32)],
        compiler_params=_params("arbitrary"),
    )(x, norm_w, w_all, dh, token, *[a for a, _ in dsecs])


def _inproj_bwd_w(x, norm_w, dsec, dq_r, dk_r, dv, cos, sin, tm=512):
    s = x.shape[0]
    nb = s // tm

    def body(x_ref, nw_ref, s0, s1, s2, s3, s7, q1, q2, q3, k1, k2, k3, v1, v2, v3, cos_ref, sin_ref,
             gw_hbm, datt_ref, acc_ref, stage_ref, nat_ref):
        @pl.when(pl.program_id(0) == 0)
        def _():
            acc_ref[...] = jnp.zeros_like(acc_ref)

        def total(refs):
            acc = refs[0][...].astype(F32)
            for d, ref in zip(DILATIONS[1:], refs[1:]):
                _from_view(ref, nat_ref, d)
                acc = acc + _get_lanes(nat_ref)
            return acc

        c, sn = cos_ref[...], -sin_ref[...]
        unrot = lambda a: jnp.concatenate(
            [_rope128(a[:, j * LANES:(j + 1) * LANES], c, sn) for j in range(LANE_GROUPS)], axis=-1)
        att = [a.astype(MM) for a in (unrot(total((q1, q2, q3))), unrot(total((k1, k2, k3))), total((v1, v2, v3)))]
        for j, a in enumerate(att):
            datt_ref[j] = a
        xv = x_ref[...]
        rstd = lax.rsqrt(jnp.mean(xv * xv, axis=-1, keepdims=True) + NORM_EPS)
        u_t = (xv * rstd * nw_ref[...]).T.astype(MM)
        for j, dsj in enumerate((s0[...], s1[...], s2[...], s3[...], *att, s7[...])):
            acc_ref[j] += jnp.dot(u_t, dsj, preferred_element_type=F32)

        @pl.when(pl.program_id(0) == nb - 1)
        def _():
            for j in range(N_SEC):
                stage_ref[...] = acc_ref[j].astype(stage_ref.dtype)
                pltpu.sync_copy(stage_ref, gw_hbm.at[j])

    row = lambda w: pl.BlockSpec((tm, w), lambda i: (i, 0))
    return pl.pallas_call(
        body, name="inproj_bwd_w", grid=(nb,),
        in_specs=[row(D_MODEL), pl.BlockSpec((1, D_MODEL), lambda i: (0, 0))] + [row(SEC_W)] * 5
                 + [_view_spec(tm, d) for d in DILATIONS] * 3 + [row(LANES), row(LANES)],
        out_specs=[pl.BlockSpec(memory_space=pl.ANY), pl.BlockSpec((3, tm, SEC_W), lambda i: (0, i, 0))],
        out_shape=[jax.ShapeDtypeStruct((N_SEC, D_MODEL, SEC_W), XCH), jax.ShapeDtypeStruct((3, s, SEC_W), MM)],
        scratch_shapes=[pltpu.VMEM((N_SEC, D_MODEL, SEC_W), F32), pltpu.VMEM((D_MODEL, SEC_W), XCH),
                        pltpu.VMEM((LANE_GROUPS, tm, LANES), F32)],
        compiler_params=_params("arbitrary"),
    )(x, norm_w, *dsec, *dq_r, *dk_r, *dv, cos, sin)


def _local_step(x, tgt, norm_w, w_all, lb_logits, hg_norm_w, wo_all, final_norm_w, on_w_out_grad, on_w_in_grad):
    s = x.shape[0]
    cos, sin = _rope_tables(s)
    proj, *qkv = _inproj_fwd(x, norm_w, w_all, cos, sin)
    o_hg, sst, a_hg = _hgrn_fwd(proj, lb_logits)
    qkv = [qkv[3 * i:3 * i + 3] for i in range(len(DILATIONS))]
    att = [_attn_fwd(*qkv_d, d) for qkv_d, d in zip(qkv, DILATIONS)]
    (dh, d_ohg, d_hgz, d_atz, do1, do2, do3, dl1, dl2, dl3, lp1, lp2, lp3, gwo, gfw, ghg, loss) = _mid(
        x, tgt, proj, o_hg, [a[0] for a in att], [a[1] for a in att], hg_norm_w, final_norm_w[None, :], wo_all(o_hg))
    dxq, dxf, dxi, dlb = _hgrn_bwd(proj, lb_logits, d_ohg, sst, a_hg, on_w_out_grad(gwo))
    dq_r, dk_r, dv = [], [], []
    for d, qkv_d, do, lp, dl in zip(DILATIONS, qkv, (do1, do2, do3), (lp1, lp2, lp3), (dl1, dl2, dl3)):
        dq_r.append(_attn_bwd_dq(*qkv_d, do, lp, dl, d))
        dk_d, dv_d = _attn_bwd_dkv(*qkv_d, do, lp, dl, d)
        dk_r.append(dk_d)
        dv.append(dv_d)
    gwi, d_att = _inproj_bwd_w(x, norm_w, (dxq, dxf, dxi, d_hgz, d_atz), dq_r, dk_r, dv, cos, sin)
    dsecs = [(dxq, None), (dxf, None), (dxi, None), (d_hgz, None), (d_att, 0), (d_att, 1), (d_att, 2), (d_atz, None)]
    token = on_w_in_grad(gwi)
    gx, gnw = _inproj_bwd_x(x, norm_w, w_all, dh, dsecs, token)
    small = jnp.concatenate([gnw, jnp.concatenate([dlb, ghg], axis=-1), gfw,
                             jnp.pad(loss, ((0, 0), (0, D_MODEL - 1)))], axis=0)
    return gx, small


def _coords():
    return lax.axis_index("x"), lax.axis_index("y"), lax.axis_index("c")


def _gather_weights(w_in):
    def body(wi_ref, wi_all, send_sems, recv_sems):
        x, y, c = _coords()
        me, sibling = (x, y, c), (x, y, 1 - c)
        chips = [(1 - x, y), (x, 1 - y), (1 - x, 1 - y)]
        slot = lambda p: 4 * p[0] + 2 * p[1] + p[2]

        def copies(k, block, to):
            return [pltpu.make_async_remote_copy(
                src_ref=wi_all.at[slot(block)], dst_ref=wi_all.at[slot(block)], send_sem=send_sems.at[k],
                recv_sem=recv_sems.at[k], device_id=to, device_id_type=MESH)]

        wi_all[slot(me)] = wi_ref[...].astype(MM)
        first = copies(0, me, sibling)
        for j, chip in enumerate(chips):
            first += copies(1 + j, me, (*chip, c))
        for cp in first:
            cp.start()
        passed = []
        for j, chip in enumerate(chips):
            for cp in copies(1 + j, (*chip, c), me):
                cp.wait_recv()
            fwd = copies(4 + j, (*chip, c), sibling)
            for cp in fwd:
                cp.start()
            passed += fwd
        for cp in copies(0, sibling, me):
            cp.wait_recv()
        for j, chip in enumerate(chips):
            for cp in copies(4 + j, (*chip, 1 - c), me):
                cp.wait_recv()
        for cp in first + passed:
            cp.wait_send()

    vmem = pl.BlockSpec(memory_space=pltpu.VMEM)
    return pl.pallas_call(
        body, name="gather_weights",
        in_specs=[vmem], out_specs=vmem,
        out_shape=jax.ShapeDtypeStruct((N_DEV, D_MODEL, SEC_W), MM),
        scratch_shapes=[pltpu.SemaphoreType.DMA((7,)), pltpu.SemaphoreType.DMA((7,))],
        compiler_params=pltpu.CompilerParams(vmem_limit_bytes=VMEM_LIMIT),
    )(w_in)


def _me():
    x, y, c = _coords()
    return 4 * x + 2 * y + c


def _grad_copies(srcs, lands, send_sems, recv_sems):
    x, y, c = _coords()
    me = 4 * x + 2 * y + c
    copies = []
    for k in range(1, N_DEV):
        px, py, pc = x ^ (k >> 2), y ^ ((k >> 1) & 1), c ^ (k & 1)
        peer = 4 * px + 2 * py + pc
        for a, (src, dst) in enumerate(zip(srcs, lands)):
            copies.append(pltpu.make_async_remote_copy(
                src_ref=src.at[peer], dst_ref=dst.at[me], send_sem=send_sems.at[a * (N_DEV - 1) + k - 1],
                recv_sem=recv_sems.at[a * (N_DEV - 1) + k - 1], device_id=(px, py, pc), device_id_type=MESH))
    return copies


HBM_SPEC = pl.BlockSpec(memory_space=pltpu.HBM)
SEM_SPEC = pl.BlockSpec(memory_space=pltpu.SEMAPHORE)
SPLIT_COPY_EFFECT = pltpu.SideEffectType.DATAFLOW_SIDE_EFFECTING


def _exchange_start(name, *arrs):
    n = len(arrs)

    def body(*refs):
        for cp in _grad_copies(refs[:n], refs[n:2 * n], refs[2 * n], refs[2 * n + 1]):
            cp.start()
        refs[-1][...] = jnp.zeros_like(refs[-1])

    hbm = lambda a: pltpu.with_memory_space_constraint(a, pltpu.HBM)
    bufs = (*arrs, *[lax.empty(a.shape, a.dtype) for a in arrs])
    return pl.pallas_call(
        body, name=name,
        out_shape=(pltpu.SemaphoreType.DMA((n * (N_DEV - 1),)), pltpu.SemaphoreType.DMA((n * (N_DEV - 1),)),
                   *[pltpu.HBM(a.shape, a.dtype) for a in bufs], jax.ShapeDtypeStruct((8, 128), F32)),
        in_specs=[HBM_SPEC] * (2 * n),
        out_specs=(SEM_SPEC, SEM_SPEC, *[HBM_SPEC] * (2 * n), pl.BlockSpec(memory_space=pltpu.VMEM)),
        input_output_aliases={i: i + 2 for i in range(2 * n)},
        compiler_params=pltpu.CompilerParams(has_side_effects=SPLIT_COPY_EFFECT),
    )(*[hbm(a) for a in bufs])


def _exchange_wait(name, send_sems, recv_sems, *bufs_after):
    *bufs, after = bufs_after
    n = len(bufs) // 2

    def body(*refs):
        for cp in _grad_copies(refs[:n], refs[n:2 * n], refs[2 * n], refs[2 * n + 1]):
            cp.wait_send()
            cp.wait_recv()

    return pl.pallas_call(
        body, name=name,
        out_shape=tuple(pltpu.HBM(a.shape, a.dtype) for a in bufs),
        in_specs=[HBM_SPEC] * (2 * n) + [SEM_SPEC, SEM_SPEC, pl.BlockSpec(memory_space=pl.ANY)],
        out_specs=(HBM_SPEC,) * (2 * n),
        input_output_aliases={i: i for i in range(2 * n)},
        compiler_params=pltpu.CompilerParams(has_side_effects=SPLIT_COPY_EFFECT),
    )(*bufs, send_sems, recv_sems, after)


def _gather_small(small):
    def body(sm_ref, ls_ref, send_sems, recv_sems, local_sem):
        x, y, c = _coords()
        me = 4 * x + 2 * y + c
        own = pltpu.make_async_copy(sm_ref, ls_ref.at[me], local_sem)
        own.start()
        sends = []
        for k in range(1, N_DEV):
            peer = (x ^ (k >> 2), y ^ ((k >> 1) & 1), c ^ (k & 1))
            sends.append(pltpu.make_async_remote_copy(
                src_ref=sm_ref, dst_ref=ls_ref.at[me], send_sem=send_sems.at[k - 1], recv_sem=recv_sems.at[k - 1],
                device_id=peer, device_id_type=MESH))
        for cp in sends:
            cp.start()
        for cp in sends:
            cp.wait_recv()
        for cp in sends:
            cp.wait_send()
        own.wait()

    vmem = pl.BlockSpec(memory_space=pltpu.VMEM)
    return pl.pallas_call(
        body, name="gather_small", in_specs=[vmem], out_specs=vmem,
        out_shape=jax.ShapeDtypeStruct((N_DEV,) + small.shape, F32),
        scratch_shapes=[pltpu.SemaphoreType.DMA((N_DEV - 1,)), pltpu.SemaphoreType.DMA((N_DEV - 1,)),
                        pltpu.SemaphoreType.DMA],
    )(small)


def _adamw(w, g, m, v):
    m = ADAM_B1 * m + (1.0 - ADAM_B1) * g
    v = ADAM_B2 * v + (1.0 - ADAM_B2) * (g * g)
    m_hat = m / (1.0 - ADAM_B1 ** ADAM_STEP)
    v_hat = v / (1.0 - ADAM_B2 ** ADAM_STEP)
    return -ADAM_LR * (m_hat / (jnp.sqrt(v_hat) + ADAM_EPS) + ADAM_WD * w), m, v


def _slot_sum(ref, own=None, me=None):
    g = None
    for i in range(N_DEV):
        term = ref[i].astype(F32)
        if own is not None:
            term = jnp.where(i == me, own, term)
        g = term if g is None else g + term
    return g


def _update_matrix(name, me, landed, own, w, m, v, rows):
    r, c = w.shape

    def body(me_ref, l_ref, own_ref, w_ref, m_ref, v_ref, g_ref, d_ref, nm_ref, nv_ref):
        g = _slot_sum(l_ref, own_ref[...].astype(F32), me_ref[0])
        g_ref[...] = g
        d_ref[...], nm_ref[...], nv_ref[...] = _adamw(w_ref[...], g, m_ref[...], v_ref[...])

    blk = pl.BlockSpec((rows, c), lambda i, me_ref: (i, 0))
    return pl.pallas_call(
        body, name=name,
        grid_spec=pltpu.PrefetchScalarGridSpec(
            num_scalar_prefetch=1, grid=(r // rows,),
            in_specs=[pl.BlockSpec((N_DEV, rows, c), lambda i, me_ref: (0, i, 0)),
                      pl.BlockSpec((None, rows, c), lambda i, me_ref: (me_ref[0], i, 0)), blk, blk, blk],
            out_specs=[blk] * 4),
        out_shape=[jax.ShapeDtypeStruct((r, c), F32)] * 4,
        compiler_params=_params("parallel"),
    )(me, landed, own, w, m, v)


def _update_small(landed, lb_logits, ws, ms, vs):
    def body(l_ref, lbl_ref, w_ref, m_ref, v_ref, g_ref, d_ref, nm_ref, nv_ref, loss_ref):
        tot = _slot_sum(l_ref)
        _, dlb = _lower_bound(lbl_ref[...])
        g_lb = tot[1:2, :SEC_W] * dlb
        g = jnp.concatenate([tot[0:1], jnp.concatenate([g_lb, -g_lb], axis=-1),
                             jnp.pad(tot[1:2, SEC_W:], ((0, 0), (0, SEC_W))), tot[2:3]], axis=0)
        g_ref[...] = g
        d_ref[...], nm_ref[...], nv_ref[...] = _adamw(w_ref[...], g, m_ref[...], v_ref[...])
        loss_ref[...] = tot[3:4, 0:1]

    vmem = pl.BlockSpec(memory_space=pltpu.VMEM)
    return pl.pallas_call(
        body, name="update_small", in_specs=[vmem] * 5, out_specs=[vmem] * 5,
        out_shape=[jax.ShapeDtypeStruct((4, D_MODEL), F32)] * 4 + [jax.ShapeDtypeStruct((1, 1), F32)],
    )(landed, lb_logits, ws, ms, vs)


def _pack_small(norm_w, lb_logits, hg_norm_w, final_norm_w):
    return jnp.concatenate([norm_w, lb_logits.reshape(1, D_MODEL),
                            jnp.pad(hg_norm_w, ((0, 0), (0, D_MODEL - SEC_W))), final_norm_w[None, :]], axis=0)


def _unpack_small(a):
    return a[0:1], a[1].reshape(2, SEC_W), a[2:3, :SEC_W], a[3]


def kernel(x, norm_w, w_in, hgrn_lb_logits, hg_norm_w, w_out, final_norm_w, loss_target, m_norm_w, m_w_in, m_hgrn_lb_logits, m_hg_norm_w, m_w_out, m_final_norm_w, v_norm_w, v_w_in, v_hgrn_lb_logits, v_hg_norm_w, v_w_out, v_final_norm_w):
    w_all = _gather_weights(w_in[0])
    wo_own = w_out[0].astype(MM)
    *gathering_wo, wo_started = _exchange_start(
        "gather_start_w_out", jnp.broadcast_to(wo_own[None], (N_DEV,) + wo_own.shape))
    norm_w_then = norm_w + wo_started[:1, :1]

    def wo_all_after(after):
        _, landed = _exchange_wait("gather_wait_w_out", *gathering_wo, after)
        return lax.dynamic_update_slice(landed, wo_own[None], (_me(), 0, 0)).reshape(D_MODEL, D_MODEL)

    flying_wo, flying_wi = [], []

    def start_w_out(gwo):
        *handles, token = _exchange_start("exchange_start_w_out", gwo.reshape(N_DEV, D_MODEL // N_DEV, D_MODEL))
        flying_wo.extend(handles)
        return token

    def start_w_in(gwi):
        *handles, token = _exchange_start("exchange_start_w_in", gwi)
        flying_wi.extend(handles)
        return token

    gx, small = _local_step(x[0], loss_target[0], norm_w_then, w_all, hgrn_lb_logits, hg_norm_w,
                            wo_all_after, final_norm_w, start_w_out, start_w_in)
    ls = _gather_small(small)
    gwo, lo = _exchange_wait("exchange_wait_w_out", *flying_wo, gx)
    gwi, li = _exchange_wait("exchange_wait_w_in", *flying_wi, gx)
    me = _me().astype(jnp.int32).reshape(1)
    g_wi, d_wi, nm_wi, nv_wi = _update_matrix("update_w_in", me, li, gwi, w_in[0], m_w_in[0], v_w_in[0], 256)
    g_wo, d_wo, nm_wo, nv_wo = _update_matrix("update_w_out", me, lo, gwo, w_out[0], m_w_out[0], v_w_out[0], 128)
    g_s, d_s, nm_s, nv_s, loss = _update_small(
        ls, hgrn_lb_logits, _pack_small(norm_w, hgrn_lb_logits, hg_norm_w, final_norm_w),
        _pack_small(m_norm_w, m_hgrn_lb_logits, m_hg_norm_w, m_final_norm_w),
        _pack_small(v_norm_w, v_hgrn_lb_logits, v_hg_norm_w, v_final_norm_w))
    outs = []
    for small_out, wi, wo in ((g_s, g_wi, g_wo), (d_s, d_wi, d_wo), (nm_s, nm_wi, nm_wo), (nv_s, nv_wi, nv_wo)):
        nw, lb, hg, fw = _unpack_small(small_out)
        outs += [nw, wi[None], lb, hg, wo[None], fw]
    return (loss[0, 0], gx[None], *outs)
```

```python
import functools

import jax
import jax.numpy as jnp
import numpy as np
from jax import lax
from jax.experimental import pallas as pl
from jax.experimental.pallas import tpu as pltpu

F32 = jnp.float32
MM = jnp.bfloat16
XCH = jnp.bfloat16
NORM_EPS = 1e-6
NEG = -1e30
N_DEV = 8
D_MODEL = 1024
N_SEC = 8
SEC_W = 512
HG_HEADS = 4
HG_D = 128
HG_GROUP = 4
AT_DH = 64
LANES = 128
ATT_BLK = 128
AT_COLS = 512
AT_QB = 8
DILATIONS = (1, 4, 16)
ROPE_THETA = 10000.0
CH = 16
LB_LO, LB_HI = 1e-6, 1.0 - 1e-6
ADAM_LR, ADAM_B1, ADAM_B2, ADAM_EPS, ADAM_WD, ADAM_STEP = 0.001, 0.9, 0.999, 1e-08, 0.01, 10
VMEM_LIMIT = 56 * 1024 * 1024
MESH = pl.DeviceIdType.MESH


def _params(*sem):
    return pltpu.CompilerParams(dimension_semantics=sem, vmem_limit_bytes=VMEM_LIMIT)


def _sigmoid(x):
    return 1.0 / (1.0 + jnp.exp(-x))


def _dot(a, b):
    return jnp.dot(a.astype(MM), b.astype(MM), preferred_element_type=F32)


def _dot_nt(a, b):
    return lax.dot_general(a.astype(MM), b.astype(MM), (((1,), (1,)), ((), ())), preferred_element_type=F32)


def _dot_tn(a, b):
    return lax.dot_general(a.astype(MM), b.astype(MM), (((0,), (0,)), ((), ())), preferred_element_type=F32)


def _split3(g):
    g1 = g.astype(jnp.bfloat16)
    r1 = g - g1.astype(F32)
    g2 = r1.astype(jnp.bfloat16)
    return g1, g2, (r1 - g2.astype(F32)).astype(jnp.bfloat16)


def _tri_dot(tri, g):
    t = tri.astype(jnp.bfloat16)
    g1, g2, g3 = _split3(g)
    d = functools.partial(jnp.dot, preferred_element_type=F32)
    return d(t, g1) + d(t, g2) + d(t, g3)


def _lower_bound(lbl):
    l0, l1 = lbl[0:1, :], lbl[1:2, :]
    m = jnp.maximum(l0, l1)
    e0, e1 = jnp.exp(l0 - m), jnp.exp(l1 - m)
    p = e0 / (e0 + e1)
    inside = (p >= LB_LO) & (p <= LB_HI)
    return jnp.clip(p, LB_LO, LB_HI), jnp.where(inside, p * (e1 / (e0 + e1)), 0.0)


def _iota2(shape, dim):
    return lax.broadcasted_iota(jnp.int32, shape, dim)


def _hgrn_gates(xq, xf, lb):
    sgq = _sigmoid(xq)
    sg = _sigmoid(xf)
    sn = _sigmoid(-xf)
    f = lb + (1.0 - lb) * sg
    return sgq, xq * sgq, sg, sn, f, (1.0 - lb) * sn


def _bdot(a, b, ca, cb):
    return lax.dot_general(a.astype(MM), b.astype(MM), (((ca,), (cb,)), ((0,), (0,))), preferred_element_type=F32)


def _chunk_masks(rb):
    row, col = _iota2((rb, rb), 0), _iota2((rb, rb), 1)
    same = (row // CH) == (col // CH)
    return same & (row >= col), same & (row <= col)


HALF = CH // 2
SLAB_ROWS = HALF * CH + (HALF // 2) * CH


def _write_slabs(slab_ref, g, q3, b3):
    slab = lambda t, rows: q3[:, rows, :] * jnp.exp(jnp.minimum(b3[:, rows, :] - b3[:, t:t + 1, :], 0.0))
    late = slice(HALF, CH)
    for t in range(HALF):
        slab_ref[g, :, t * CH:(t + 1) * CH, :] = slab(t, slice(0, CH)).astype(MM)
    for p in range(HALF // 2):
        t = HALF + 2 * p
        two = jnp.concatenate([slab(t, late), slab(t + 1, late)], axis=1)
        slab_ref[g, :, (HALF + p) * CH:(HALF + p + 1) * CH, :] = two.astype(MM)


def _read_diag(r):
    nc = r.shape[0]
    col, col_late = _iota2((nc, CH, CH), 2), _iota2((nc, HALF, CH), 2)
    a, a_late = jnp.zeros((nc, CH, CH), F32), jnp.zeros((nc, HALF, CH), F32)
    for t in range(HALF):
        a = a + jnp.where(col == t, r[:, t * CH:(t + 1) * CH, :], 0.0)
    for p in range(HALF // 2):
        t, two = HALF + 2 * p, r[:, (HALF + p) * CH:(HALF + p + 1) * CH, :]
        a_late = a_late + jnp.where(col_late == t, two[:, :HALF, :], 0.0) + jnp.where(col_late == t + 1, two[:, HALF:, :], 0.0)
    return a + jnp.concatenate([jnp.zeros_like(a_late), a_late], axis=1)


def _hgrn_fwd(proj, lb_logits, rb=256):
    s = proj.shape[1]
    nb, nc = s // rb, rb // CH

    def body(q_ref, f_ref, i_ref, lbl_ref, o_ref, sst_ref, a_ref, st_ref, slab_ref, states_ref):
        @pl.when(pl.program_id(1) == 0)
        def _():
            st_ref[...] = jnp.zeros_like(st_ref)

        sst_ref[...] = st_ref[...]
        prefix, _ = _chunk_masks(rb)
        c3 = lambda a: a.reshape(nc, CH, HG_D)
        row, col = _iota2((nc, CH, CH), 1), _iota2((nc, CH, CH), 2)
        heads = []
        for g in range(HG_GROUP):
            hs = slice(g * HG_D, (g + 1) * HG_D)
            lb, _ = _lower_bound(lbl_ref[:, hs])
            _, q, _, _, f, kk = _hgrn_gates(q_ref[:, hs], f_ref[:, hs], lb)
            b3 = c3(_tri_dot(prefix, jnp.log(f)))
            q3, kk3, v3 = c3(q), c3(kk), c3(i_ref[:, hs])
            bl3 = b3[:, CH - 1:CH, :]
            _write_slabs(slab_ref, g, q3, b3)
            x_upd = _bdot(v3, kk3 * jnp.exp(bl3 - b3), 1, 1)
            heads.append(dict(hs=hs, kk3=kk3, v3=v3, qe3=q3 * jnp.exp(b3), ebl3=jnp.exp(bl3), x_upd=x_upd))
        for g, hd in enumerate(heads):
            st = st_ref[g]
            for c in range(nc):
                states_ref[g, c] = st
                st = st * hd["ebl3"][c] + hd["x_upd"][c]
            st_ref[g] = st
        for g, hd in enumerate(heads):
            a = _read_diag(_bdot(slab_ref[g], hd["kk3"], 2, 2))
            a = jnp.where(row >= col, a, 0.0)
            a_ref[:, g * CH:(g + 1) * CH] = a.reshape(rb, CH)
            o3 = _bdot(hd["qe3"], states_ref[g], 2, 2) + _bdot(a, hd["v3"], 2, 1)
            o_ref[:, hd["hs"]] = o3.reshape(rb, HG_D)

    wide = HG_GROUP * HG_D
    sec = lambda j: pl.BlockSpec((None, rb, wide), lambda h, i, j=j: (j, i, h))
    return pl.pallas_call(
        body, name="hgrn_fwd", grid=(HG_HEADS // HG_GROUP, nb),
        in_specs=[sec(0), sec(1), sec(2), pl.BlockSpec((2, wide), lambda h, i: (0, h))],
        out_specs=[pl.BlockSpec((rb, wide), lambda h, i: (i, h)),
                   pl.BlockSpec((None, HG_GROUP, HG_D, HG_D), lambda h, i: (i, h, 0, 0)),
                   pl.BlockSpec((rb, HG_GROUP * CH), lambda h, i: (i, h))],
        out_shape=[jax.ShapeDtypeStruct((s, SEC_W), F32),
                   jax.ShapeDtypeStruct((nb, HG_HEADS, HG_D, HG_D), F32),
                   jax.ShapeDtypeStruct((s, HG_HEADS * CH), F32)],
        scratch_shapes=[pltpu.VMEM((HG_GROUP, HG_D, HG_D), F32), pltpu.VMEM((HG_GROUP, nc, SLAB_ROWS, HG_D), MM),
                        pltpu.VMEM((HG_GROUP, nc, HG_D, HG_D), F32)],
        compiler_params=_params("parallel", "arbitrary"),
    )(proj, proj, proj, lb_logits)


def _hgrn_bwd(proj, lb_logits, d_o, sst, a_in, token, rb=256):
    s = proj.shape[1]
    nb, nc = s // rb, rb // CH

    def body(q_ref, f_ref, i_ref, lbl_ref, do_ref, sst_ref, a_ref, tok_ref, dxq_ref, dxf_ref, dxi_ref, dlb_ref,
             dst_ref, states_ref, dstates_ref, lslab_ref, kslab_ref):
        @pl.when(pl.program_id(1) == 0)
        def _():
            dst_ref[...] = jnp.zeros_like(dst_ref)
            dlb_ref[...] = jnp.zeros_like(dlb_ref) + tok_ref[0:1, 0:1]

        prefix, suffix = _chunk_masks(rb)
        c3 = lambda a: a.reshape(nc, CH, HG_D)
        flat = lambda a: a.reshape(rb, HG_D)
        row, col = _iota2((nc, CH, CH), 1), _iota2((nc, CH, CH), 2)
        tril, triu = row >= col, row <= col
        sel = (_iota2((CH, CH * CH), 1) % CH == _iota2((CH, CH * CH), 0)).astype(MM)
        blockdiag = _iota2((nc, CH, CH * CH), 2) // CH == _iota2((nc, CH, CH * CH), 1)
        tile = lambda m: jnp.where(blockdiag, _dot(m.reshape(rb, CH), sel).reshape(nc, CH, CH * CH), 0.0)
        last = _iota2((nc, CH, HG_D), 1) == CH - 1
        heads = []
        for g in range(HG_GROUP):
            hs = slice(g * HG_D, (g + 1) * HG_D)
            lb, _ = _lower_bound(lbl_ref[:, hs])
            xq = q_ref[:, hs]
            sgq, q, sg, sn, f, kk = _hgrn_gates(xq, f_ref[:, hs], lb)
            b3 = c3(_tri_dot(prefix, jnp.log(f)))
            q3, kk3, v3, do3 = c3(q), c3(kk), c3(i_ref[:, hs]), c3(do_ref[:, hs])
            bl3 = b3[:, CH - 1:CH, :]
            eb3, ebl3, dec3 = jnp.exp(b3), jnp.exp(bl3), jnp.exp(bl3 - b3)
            qe3, kd3 = q3 * eb3, kk3 * dec3
            x_upd, y_upd = _bdot(v3, kd3, 1, 1), _bdot(do3, qe3, 1, 1)
            zero, every, early, late = jnp.zeros((nc, HALF, HG_D), F32), slice(0, CH), slice(0, HALF), slice(HALF, CH)
            for t in range(CH):
                bt = b3[:, t:t + 1, :]
                since = lambda rows: q3[:, rows, :] * jnp.exp(jnp.minimum(b3[:, rows, :] - bt, 0.0))
                until = lambda rows: kk3[:, rows, :] * jnp.exp(jnp.minimum(bt - b3[:, rows, :], 0.0))
                if t < HALF:
                    lv, kv = since(every), jnp.concatenate([until(early), zero], axis=1)
                else:
                    lv, kv = jnp.concatenate([zero, since(late)], axis=1), until(every)
                lslab_ref[g, :, t * CH:(t + 1) * CH, :] = lv.astype(MM)
                kslab_ref[g, :, t * CH:(t + 1) * CH, :] = kv.astype(MM)
            d_a = jnp.where(tril, _bdot(do3, v3, 2, 2), 0.0)
            d_at = jnp.where(triu, _bdot(v3, do3, 2, 2), 0.0)
            heads.append(dict(hs=hs, lb=lb, xq=xq, sgq=sgq, sg=sg, sn=sn, f=f, q3=q3, kk3=kk3, v3=v3, do3=do3,
                              eb3=eb3, ebl3=ebl3, dec3=dec3, qe3=qe3, kd3=kd3, x_upd=x_upd, y_upd=y_upd,
                              d_a=d_a, d_at=d_at))
        for g, hd in enumerate(heads):
            st = sst_ref[g]
            for c in range(nc):
                states_ref[g, c] = st
                st = st * hd["ebl3"][c] + hd["x_upd"][c]
            dst = dst_ref[g]
            for c in reversed(range(nc)):
                dstates_ref[g, c] = dst
                dst = dst * hd["ebl3"][c] + hd["y_upd"][c]
            dst_ref[g] = dst
        for g, hd in enumerate(heads):
            q3, v3, do3, kd3 = hd["q3"], hd["v3"], hd["do3"], hd["kd3"]
            states, dstates = states_ref[g], dstates_ref[g]
            hd["dqe"] = _bdot(do3, states, 2, 1)
            hd["dkd"] = _bdot(v3, dstates, 2, 1)
            a = a_ref[:, g * CH:(g + 1) * CH].reshape(nc, CH, CH)
            hd["dv"] = _bdot(kd3, dstates, 2, 2) + _bdot(a, do3, 1, 1)
            hd["dq_in"] = _bdot(tile(hd["d_a"]), kslab_ref[g], 2, 1)
            hd["dk_in"] = _bdot(tile(hd["d_at"]), lslab_ref[g], 2, 1)
            hd["ss"] = jnp.sum(dstates * states, axis=1, keepdims=True)
        for g, hd in enumerate(heads):
            q3, kk3, eb3, ebl3, dec3, qe3, kd3 = (hd[k] for k in ("q3", "kk3", "eb3", "ebl3", "dec3", "qe3", "kd3"))
            dqe, dkd, dq_in, dk_in = hd["dqe"], hd["dkd"], hd["dq_in"], hd["dk_in"]
            dkd_kd = dkd * kd3
            db = dqe * qe3 - dkd_kd + q3 * dq_in - kk3 * dk_in
            dbl = jnp.sum(dkd_kd, axis=1, keepdims=True) + hd["ss"] * ebl3
            dg = _tri_dot(suffix, flat(db + jnp.where(last, dbl, 0.0)))
            df = dg / hd["f"] - flat(dkd * dec3 + dk_in)
            xq, sgq, hs = hd["xq"], hd["sgq"], hd["hs"]
            dxq_ref[:, hs] = (flat(dqe * eb3 + dq_in) * (sgq * (1.0 + xq * (1.0 - sgq)))).astype(MM)
            dxf_ref[:, hs] = (df * (1.0 - hd["lb"]) * hd["sg"] * hd["sn"]).astype(MM)
            dxi_ref[:, hs] = flat(hd["dv"]).astype(MM)
            dlb_ref[:, hs] += jnp.sum(df * hd["sn"], axis=0, keepdims=True)

    wide = HG_GROUP * HG_D
    rev = lambda i: nb - 1 - i
    sec = lambda j: pl.BlockSpec((None, rb, wide), lambda h, i, j=j: (j, rev(i), h))
    blk = pl.BlockSpec((rb, wide), lambda h, i: (rev(i), h))
    state = (pltpu.VMEM((HG_GROUP, nc, HG_D, HG_D), F32), pltpu.VMEM((HG_GROUP, nc, CH * CH, HG_D), MM))
    return pl.pallas_call(
        body, name="hgrn_bwd", grid=(HG_HEADS // HG_GROUP, nb),
        in_specs=[sec(0), sec(1), sec(2), pl.BlockSpec((2, wide), lambda h, i: (0, h)), blk,
                  pl.BlockSpec((None, HG_GROUP, HG_D, HG_D), lambda h, i: (rev(i), h, 0, 0)),
                  pl.BlockSpec((rb, HG_GROUP * CH), lambda h, i: (rev(i), h)),
                  pl.BlockSpec((8, 128), lambda h, i: (0, 0))],
        out_specs=[blk, blk, blk, pl.BlockSpec((1, wide), lambda h, i: (0, h))],
        out_shape=[jax.ShapeDtypeStruct((s, SEC_W), MM)] * 3 + [jax.ShapeDtypeStruct((1, SEC_W), F32)],
        scratch_shapes=[pltpu.VMEM((HG_GROUP, HG_D, HG_D), F32), state[0], state[0], state[1], state[1]],
        compiler_params=_params("parallel", "arbitrary"),
    )(proj, proj, proj, lb_logits, d_o, sst, a_in, token)


def _rope_tables(s):
    half = AT_DH // 2
    inv_freq = np.float32(1.0) / (np.float32(ROPE_THETA) ** (np.arange(half, dtype=np.float32) / np.float32(half)))
    ang = np.arange(s, dtype=np.float32)[:, None] * inv_freq[None, :]
    cos, sin = np.cos(ang), np.sin(ang)
    return np.concatenate([cos] * 4, axis=-1), np.concatenate([-sin, sin] * 2, axis=-1)


def _rope128(x, cos, sin):
    lo = (_iota2(x.shape, 1) % AT_DH) < AT_DH // 2
    rot = jnp.where(lo, pltpu.roll(x, LANES - AT_DH // 2, 1), pltpu.roll(x, AT_DH // 2, 1))
    return x * cos + rot * sin


LANE_GROUPS = SEC_W // LANES


def _set_lanes(ref, val):
    for j in range(LANE_GROUPS):
        ref[j] = val[:, j * LANES:(j + 1) * LANES]


def _get_lanes(ref):
    return jnp.concatenate([ref[j] for j in range(LANE_GROUPS)], axis=-1)


def _to_view(src_ref, dst_ref, d):
    n = src_ref.shape[1] // d
    for r in range(d):
        rows = pl.ds(r, n, stride=d) if d > 1 else slice(None)
        for j in range(LANE_GROUPS):
            c0 = r * SEC_W + j * LANES
            dst_ref[:, c0:c0 + LANES] = src_ref.at[j][rows, :].astype(dst_ref.dtype)


def _from_view(src_ref, dst_ref, d):
    n = dst_ref.shape[1] // d
    for r in range(d):
        for j in range(LANE_GROUPS):
            c0 = r * SEC_W + j * LANES
            dst_ref.at[j][pl.ds(r, n, stride=d), :] = src_ref[:, c0:c0 + LANES].astype(dst_ref.dtype)


def _view_spec(tm, d):
    return pl.BlockSpec((tm // d, d * SEC_W), lambda i: (i, 0))


def _view_shape(s, d, dtype):
    return jax.ShapeDtypeStruct((s // d, d * SEC_W), dtype)


PROJ_KEPT = (0, 1, 2, 3, 7)


def _inproj_fwd(x, norm_w, w_all, cos, sin, tm=512):
    s = x.shape[0]

    def body(x_ref, nw_ref, w_ref, cos_ref, sin_ref, proj_ref, *refs):
        outs, (qs_ref, ks_ref, vs_ref) = refs[:-3], refs[-3:]
        xv = x_ref[...]
        rstd = lax.rsqrt(jnp.mean(xv * xv, axis=-1, keepdims=True) + NORM_EPS)
        u = (xv * rstd * nw_ref[...]).astype(MM)
        for slot, j in enumerate(PROJ_KEPT):
            proj_ref[slot] = jnp.dot(u, w_ref[j], preferred_element_type=F32)
        q, k, v = [jnp.dot(u, w_ref[j], preferred_element_type=F32) for j in (4, 5, 6)]
        c, sn = cos_ref[...], sin_ref[...]
        for g in range(LANE_GROUPS):
            sl = slice(g * LANES, (g + 1) * LANES)
            qs_ref[g] = _rope128(q[:, sl], c, sn) * (AT_DH ** -0.5)
            ks_ref[g] = _rope128(k[:, sl], c, sn)
            vs_ref[g] = v[:, sl]
        for i, d in enumerate(DILATIONS):
            for src_ref, dst_ref in zip((qs_ref, ks_ref, vs_ref), outs[3 * i:3 * i + 3]):
                _to_view(src_ref, dst_ref, d)

    tab = pl.BlockSpec((tm, LANES), lambda i: (i, 0))
    return pl.pallas_call(
        body, name="inproj_fwd", grid=(s // tm,),
        in_specs=[pl.BlockSpec((tm, D_MODEL), lambda i: (i, 0)),
                  pl.BlockSpec((1, D_MODEL), lambda i: (0, 0)),
                  pl.BlockSpec((N_SEC, D_MODEL, SEC_W), lambda i: (0, 0, 0)), tab, tab],
        out_specs=[pl.BlockSpec((len(PROJ_KEPT), tm, SEC_W), lambda i: (0, i, 0))]
                  + [_view_spec(tm, d) for d in DILATIONS for _ in range(3)],
        out_shape=[jax.ShapeDtypeStruct((len(PROJ_KEPT), s, SEC_W), F32)]
                  + [_view_shape(s, d, MM) for d in DILATIONS for _ in range(3)],
        scratch_shapes=[pltpu.VMEM((LANE_GROUPS, tm, LANES), F32)] * 3,
        compiler_params=_params("parallel"),
    )(x, norm_w, w_all, cos, sin)


def _band_mask(first_ok, second_ok):
    row, col = _iota2((ATT_BLK, 2 * ATT_BLK), 0), _iota2((ATT_BLK, 2 * ATT_BLK), 1)
    return ((col < ATT_BLK) & (col >= row) & first_ok) | ((col >= ATT_BLK) & ((col - ATT_BLK) <= row) & second_ok)


def _own_lanes(rows, h):
    lane = _iota2((rows, LANES), 1)
    return (lane < AT_DH) if h == 0 else (lane >= AT_DH)


def _neg_pieces(rows, h):
    lane = _iota2((rows, LANES), 1) - (AT_DH if h == 0 else 0)
    return jnp.where((lane >= 0) & (lane < 3), -1.0, 0.0).astype(MM)


def _units(qb):
    return [(b, slice(g * LANES, (g + 1) * LANES), h) for b in range(qb) for g in range(AT_COLS // LANES) for h in range(2)]


def _sub(b):
    return slice(b * ATT_BLK, (b + 1) * ATT_BLK)


def _band_before(cur_ref, prev_ref, b, sl):
    if b == 0:
        return jnp.concatenate([prev_ref[:, sl], cur_ref[0:ATT_BLK, sl]], axis=0)
    return cur_ref[(b - 1) * ATT_BLK:(b + 1) * ATT_BLK, sl]


def _band_after(cur_ref, next_ref, b, sl):
    if (b + 1) * ATT_BLK == cur_ref.shape[0]:
        return jnp.concatenate([cur_ref[b * ATT_BLK:(b + 1) * ATT_BLK, sl], next_ref[:, sl]], axis=0)
    return cur_ref[b * ATT_BLK:(b + 2) * ATT_BLK, sl]


def _attn_specs(rows):
    qb = min(AT_QB, rows // ATT_BLK)
    assert rows % (qb * ATT_BLK) == 0
    last = rows // ATT_BLK - 1
    cur = pl.BlockSpec((qb * ATT_BLK, AT_COLS), lambda c, n: (n, c))
    prev = pl.BlockSpec((ATT_BLK, AT_COLS), lambda c, n: (jnp.maximum(qb * n - 1, 0), c))
    nxt = pl.BlockSpec((ATT_BLK, AT_COLS), lambda c, n: (jnp.minimum(qb * (n + 1), last), c))
    return qb, cur, prev, nxt


def _stack_heads(a):
    h0 = _own_lanes(a.shape[0], 0)
    zero = jnp.zeros_like(a)
    return jnp.concatenate([jnp.where(h0, a, zero), jnp.where(h0, zero, a)], axis=0)


def _unstack_heads(a2):
    return jnp.where(_own_lanes(ATT_BLK, 0), a2[:ATT_BLK], a2[ATT_BLK:])


def _attn_fwd(qr, kr, vr, d):
    rows, cols = qr.shape
    qb, cur, prev, nxt = _attn_specs(rows)
    nb = rows // (qb * ATT_BLK)

    def body(q_ref, kc_ref, kp_ref, vc_ref, vp_ref, o_ref, lse_ref):
        twice = lambda m: jnp.concatenate([m, m], axis=0)
        masks = {True: twice(_band_mask(pl.program_id(1) > 0, True)), False: twice(_band_mask(True, True))}
        ones = jnp.ones((2 * ATT_BLK, LANES), MM)
        units = [(b, sl) for b, sl, h in _units(qb) if h == 0]
        scs = [jnp.where(masks[b == 0], _dot_nt(_stack_heads(q_ref[_sub(b), sl]), _band_before(kc_ref, kp_ref, b, sl)),
                         NEG) for b, sl in units]
        ms = [jnp.max(sc, axis=-1, keepdims=True) for sc in scs]
        ps = [jnp.exp(sc - m).astype(MM) for sc, m in zip(scs, ms)]
        ols = [jnp.dot(p, jnp.concatenate([_band_before(vc_ref, vp_ref, b, sl), ones], axis=1),
                       preferred_element_type=F32) for p, (b, sl) in zip(ps, units)]
        for (b, sl), m, ol in zip(units, ms, ols):
            l = _unstack_heads(ol[:, LANES:])
            o_ref[_sub(b), sl] = _unstack_heads(ol[:, :LANES]) / l
            lse_ref[_sub(b), sl] = _unstack_heads(jnp.broadcast_to(m, (2 * ATT_BLK, LANES))) + jnp.log(l)

    o, lse = pl.pallas_call(
        body, name=f"attn_fwd_d{d}", grid=(cols // AT_COLS, nb),
        in_specs=[cur, cur, prev, cur, prev], out_specs=[cur, cur],
        out_shape=[jax.ShapeDtypeStruct((rows, cols), F32)] * 2,
        compiler_params=_params("parallel", "parallel"),
    )(qr, kr, kr, vr, vr)
    return o, lse


def _attn_bwd_dq(qr, kr, vr, do, lse, delta, d):
    rows, cols = qr.shape
    qb, cur, prev, nxt = _attn_specs(rows)
    nb = rows // (qb * ATT_BLK)

    def body(q_ref, kc_ref, kp_ref, vc_ref, vp_ref, do_ref, lse_ref, dl_ref, dq_ref):
        masks = {True: _band_mask(pl.program_id(1) > 0, True), False: _band_mask(True, True)}
        units = _units(qb)
        sms, dps = [], []
        for b, sl, h in units:
            own, own_b, neg = _own_lanes(ATT_BLK, h), _own_lanes(2 * ATT_BLK, h), _neg_pieces(2 * ATT_BLK, h)
            sms.append(_dot_nt(jnp.where(own, q_ref[_sub(b), sl], lse_ref[_sub(b), sl]),
                               jnp.where(own_b, _band_before(kc_ref, kp_ref, b, sl), neg)))
            dps.append(_dot_nt(jnp.where(own, do_ref[_sub(b), sl], dl_ref[_sub(b), sl]),
                               jnp.where(own_b, _band_before(vc_ref, vp_ref, b, sl), neg)))
        dss = [(jnp.exp(jnp.where(masks[b == 0], sm, NEG)) * dp).astype(MM)
               for sm, dp, (b, _, _) in zip(sms, dps, units)]
        dqs = [jnp.dot(ds, _band_before(kc_ref, kp_ref, b, sl), preferred_element_type=F32) * (AT_DH ** -0.5)
               for ds, (b, sl, _) in zip(dss, units)]
        for i in range(0, len(units), 2):
            b, sl, _ = units[i]
            dq_ref[_sub(b), sl] = jnp.where(_own_lanes(ATT_BLK, 0), dqs[i], dqs[i + 1]).astype(dq_ref.dtype)

    dq = pl.pallas_call(
        body, name=f"attn_bwd_dq_d{d}", grid=(cols // AT_COLS, nb),
        in_specs=[cur, cur, prev, cur, prev, cur, cur, cur], out_specs=cur,
        out_shape=jax.ShapeDtypeStruct((rows, cols), MM),
        compiler_params=_params("parallel", "parallel"),
    )(qr, kr, kr, vr, vr, do, lse, delta)
    return dq


def _attn_bwd_dkv(qr, kr, vr, do, lse, delta, d):
    rows, cols = qr.shape
    qb, cur, prev, nxt = _attn_specs(rows)
    nb = rows // (qb * ATT_BLK)

    def body(k_ref, v_ref, qc_ref, qn_ref, doc_ref, don_ref, lsec_ref, lsen_ref, dlc_ref, dln_ref,
             dk_ref, dv_ref):
        masks = {True: _band_mask(True, pl.program_id(1) < nb - 1), False: _band_mask(True, True)}
        units = _units(qb)
        sms, dps = [], []
        for b, sl, h in units:
            own, own_b, neg = _own_lanes(ATT_BLK, h), _own_lanes(2 * ATT_BLK, h), _neg_pieces(ATT_BLK, h)
            sms.append(_dot_nt(jnp.where(own, k_ref[_sub(b), sl], neg),
                               jnp.where(own_b, _band_after(qc_ref, qn_ref, b, sl),
                                         _band_after(lsec_ref, lsen_ref, b, sl))))
            dps.append(_dot_nt(jnp.where(own, v_ref[_sub(b), sl], neg),
                               jnp.where(own_b, _band_after(doc_ref, don_ref, b, sl),
                                         _band_after(dlc_ref, dln_ref, b, sl))))
        ps = [jnp.exp(jnp.where(masks[b == qb - 1], sm, NEG)) for sm, (b, _, _) in zip(sms, units)]
        dss = [(p * dp).astype(MM) for p, dp in zip(ps, dps)]
        dvs = [jnp.dot(p.astype(MM), _band_after(doc_ref, don_ref, b, sl), preferred_element_type=F32)
               for p, (b, sl, _) in zip(ps, units)]
        dks = [jnp.dot(ds, _band_after(qc_ref, qn_ref, b, sl), preferred_element_type=F32)
               for ds, (b, sl, _) in zip(dss, units)]
        head0 = _own_lanes(ATT_BLK, 0)
        for i in range(0, len(units), 2):
            b, sl, _ = units[i]
            dk_ref[_sub(b), sl] = jnp.where(head0, dks[i], dks[i + 1]).astype(dk_ref.dtype)
            dv_ref[_sub(b), sl] = jnp.where(head0, dvs[i], dvs[i + 1]).astype(dv_ref.dtype)

    dk, dv = pl.pallas_call(
        body, name=f"attn_bwd_dkv_d{d}", grid=(cols // AT_COLS, nb),
        in_specs=[cur, cur, cur, nxt, cur, nxt, cur, nxt, cur, nxt], out_specs=[cur, cur],
        out_shape=[jax.ShapeDtypeStruct((rows, cols), MM)] * 2,
        compiler_params=_params("parallel", "parallel"),
    )(kr, vr, qr, qr, do, do, lse, lse, delta, delta)
    return dk, dv


def _head_sum(a, width):
    parts = []
    for j in range(a.shape[1] // width):
        sm = jnp.sum(a[:, j * width:(j + 1) * width], axis=-1, keepdims=True)
        parts.append(jnp.broadcast_to(sm, (a.shape[0], width)))
    return jnp.concatenate(parts, axis=-1)


def _partner_sum(a):
    swap = (_iota2((LANES, LANES), 0) // AT_DH != _iota2((LANES, LANES), 1) // AT_DH).astype(jnp.bfloat16)
    d = functools.partial(jnp.dot, preferred_element_type=F32)
    parts = _split3(a)
    return jnp.concatenate([d(parts[0][:, sl], swap) + d(parts[1][:, sl], swap) + d(parts[2][:, sl], swap)
                            for sl in (slice(j * LANES, (j + 1) * LANES) for j in range(a.shape[1] // LANES))], axis=-1)


def _partner_value(x):
    return jnp.concatenate([pltpu.roll(x[:, j * LANES:(j + 1) * LANES], AT_DH, 1) for j in range(x.shape[1] // LANES)],
                           axis=-1)


def _pieces(xs):
    hi = xs.astype(jnp.bfloat16).astype(F32)
    mid = (xs - hi).astype(jnp.bfloat16).astype(F32)
    lo = (xs - hi - mid).astype(jnp.bfloat16).astype(F32)
    lane = _iota2(xs.shape, 1) % AT_DH
    return jnp.where(lane == 0, hi, jnp.where(lane == 1, mid, jnp.where(lane == 2, lo, 0.0)))


def _mid(x, tgt, proj, o_hg, o_at, lse_at, hg_norm_w, final_norm_w, wo_all, tm=256):
    s = x.shape[0]
    nb = s // tm

    def body(x_ref, t_ref, hgz_ref, atz_ref, ohg_ref, o1_ref, o2_ref, o3_ref, l1_ref, l2_ref, l3_ref,
             g_ref, fw_ref, wo_ref,
             dh_ref, dohg_ref, dhgz_ref, datz_ref, do1_ref, do2_ref, do3_ref, dl1_ref, dl2_ref, dl3_ref,
             lp1_ref, lp2_ref, lp3_ref,
             gwo_ref, gfw_ref, ghg_ref, loss_ref, nat_ref, stage_ref, gwo_acc):
        @pl.when(pl.program_id(0) == 0)
        def _():
            gwo_acc[...] = jnp.zeros_like(gwo_acc)
            gfw_ref[...] = jnp.zeros_like(gfw_ref)
            ghg_ref[...] = jnp.zeros_like(ghg_ref)
            loss_ref[...] = jnp.zeros_like(loss_ref)

        ohg, g = ohg_ref[...], g_ref[...]
        rs = lax.rsqrt(_head_sum(ohg * ohg, HG_D) * (1.0 / HG_D) + NORM_EPS)
        on = ohg * rs
        hgz = hgz_ref[...]
        sz = _sigmoid(hgz)
        gate_hg = hgz * sz
        lses, outs = [l1_ref[...]], [o1_ref[...]]
        for k, (d, l_ref, o_ref) in enumerate(zip(DILATIONS[1:], (l2_ref, l3_ref), (o2_ref, o3_ref))):
            _from_view(l_ref, nat_ref.at[2 * k], d)
            _from_view(o_ref, nat_ref.at[2 * k + 1], d)
            lses.append(_get_lanes(nat_ref.at[2 * k]))
            outs.append(_get_lanes(nat_ref.at[2 * k + 1]))
        mx = jnp.maximum(jnp.maximum(lses[0], lses[1]), lses[2])
        es = [jnp.exp(l - mx) for l in lses]
        den = es[0] + es[1] + es[2]
        ws = [e / den for e in es]
        oat = ws[0] * outs[0] + ws[1] * outs[1] + ws[2] * outs[2]
        atz = atz_ref[...]
        sa = _sigmoid(atz)
        gate_at = atz * sa
        mixed = jnp.concatenate([on * g * gate_hg, oat * gate_at], axis=-1).astype(MM)
        h = x_ref[...] + jnp.dot(mixed, wo_ref[...], preferred_element_type=F32)
        rstd = lax.rsqrt(jnp.mean(h * h, axis=-1, keepdims=True) + NORM_EPS)
        hn = h * rstd
        fw = fw_ref[...]
        err = hn * fw - t_ref[...]
        loss_ref[...] += 0.5 * jnp.sum(jnp.mean(err * err, axis=-1, keepdims=True), axis=0, keepdims=True)
        dout = err * (1.0 / D_MODEL)
        gfw_ref[...] += jnp.sum(dout * hn, axis=0, keepdims=True)
        dhn = dout * fw
        dh = rstd * (dhn - hn * jnp.mean(dhn * hn, axis=-1, keepdims=True))
        dh_ref[...] = dh
        dh_mm = dh.astype(MM)
        gwo_acc[...] += _dot_tn(mixed, dh_mm)

        @pl.when(pl.program_id(0) == nb - 1)
        def _():
            gwo_ref[...] = gwo_acc[...].astype(gwo_ref.dtype)

        dmixed = _dot_nt(dh_mm, wo_ref[...])
        dm_hg = dmixed[:, :SEC_W]
        d_ong = dm_hg * gate_hg
        dhgz_ref[...] = (dm_hg * (on * g) * (sz * (1.0 + hgz * (1.0 - sz)))).astype(MM)
        ghg_ref[...] += jnp.sum(d_ong * on, axis=0, keepdims=True)
        d_on = d_ong * g
        dohg_ref[...] = rs * (d_on - on * (_head_sum(d_on * on, HG_D) * (1.0 / HG_D)))
        dm_at = dmixed[:, SEC_W:]
        d_oat = dm_at * gate_at
        datz_ref[...] = (dm_at * oat * (sa * (1.0 + atz * (1.0 - sa)))).astype(MM)
        lse_all = mx + jnp.log(den)
        for val, dst_refs in ((d_oat, (do1_ref, do2_ref, do3_ref)),
                              (_pieces(_partner_sum(d_oat * oat)), (dl1_ref, dl2_ref, dl3_ref)),
                              (_pieces(_partner_value(lse_all)), (lp1_ref, lp2_ref, lp3_ref))):
            _set_lanes(stage_ref, val)
            for d, dst_ref in zip(DILATIONS, dst_refs):
                _to_view(stage_ref, dst_ref, d)

    row = lambda w: pl.BlockSpec((tm, w), lambda i: (i, 0))
    sec = lambda j: pl.BlockSpec((None, tm, SEC_W), lambda i, j=j: (j, i, 0))
    const = lambda shp: pl.BlockSpec(shp, lambda i: (0,) * len(shp))
    half = row(SEC_W)
    views = [_view_spec(tm, d) for d in DILATIONS]
    return pl.pallas_call(
        body, name="mid", grid=(nb,),
        in_specs=[row(D_MODEL), row(D_MODEL), sec(PROJ_KEPT.index(3)), sec(PROJ_KEPT.index(7)), half] + views * 2
                 + [const((1, SEC_W)), const((1, D_MODEL)), const((D_MODEL, D_MODEL))],
        out_specs=[row(D_MODEL)] + [half] * 3 + views * 3
                  + [const((D_MODEL, D_MODEL)), const((1, D_MODEL)), const((1, SEC_W)), const((1, 1))],
        out_shape=[jax.ShapeDtypeStruct((s, D_MODEL), F32), jax.ShapeDtypeStruct((s, SEC_W), F32)]
                  + [jax.ShapeDtypeStruct((s, SEC_W), MM)] * 2
                  + [_view_shape(s, d, MM) for d in DILATIONS] * 3
                  + [jax.ShapeDtypeStruct((D_MODEL, D_MODEL), XCH), jax.ShapeDtypeStruct((1, D_MODEL), F32),
                     jax.ShapeDtypeStruct((1, SEC_W), F32), jax.ShapeDtypeStruct((1, 1), F32)],
        scratch_shapes=[pltpu.VMEM((4, LANE_GROUPS, tm, LANES), F32), pltpu.VMEM((LANE_GROUPS, tm, LANES), F32),
                        pltpu.VMEM((D_MODEL, D_MODEL), F32)],
        compiler_params=_params("arbitrary"),
    )(x, tgt, proj, proj, o_hg, *o_at, *lse_at, hg_norm_w, final_norm_w, wo_all)


def _section_specs(dsecs, tm):
    return [pl.BlockSpec((tm, SEC_W), lambda i: (i, 0)) if k is None
            else pl.BlockSpec((None, tm, SEC_W), lambda i, k=k: (k, i, 0)) for _, k in dsecs]


def _inproj_bwd_x(x, norm_w, w_all, dh, dsecs, token, tm=512):
    s = x.shape[0]

    def body(x_ref, nw_ref, w_ref, dh_ref, tok_ref, *refs):
        sec_refs, (gx_ref, gnw_ref) = refs[:N_SEC], refs[N_SEC:]

        @pl.when(pl.program_id(0) == 0)
        def _():
            gnw_ref[...] = jnp.zeros_like(gnw_ref)

        du = jnp.zeros((tm, D_MODEL), F32)
        for j in range(N_SEC):
            du = du + _dot_nt(sec_refs[j][...], w_ref[j])
        xv, nw = x_ref[...], nw_ref[...]
        rstd = lax.rsqrt(jnp.mean(xv * xv, axis=-1, keepdims=True) + NORM_EPS)
        xn = xv * rstd
        gnw_ref[...] += jnp.sum(du * xn, axis=0, keepdims=True)
        dxn = du * nw
        dx = rstd * (dxn - xn * jnp.mean(dxn * xn, axis=-1, keepdims=True))
        gx_ref[...] = (dh_ref[...] + tok_ref[0:1, 0:1]) + dx

    row = lambda w: pl.BlockSpec((tm, w), lambda i: (i, 0))
    const = lambda shp: pl.BlockSpec(shp, lambda i: (0,) * len(shp))
    return pl.pallas_call(
        body, name="inproj_bwd_x", grid=(s // tm,),
        in_specs=[row(D_MODEL), const((1, D_MODEL)), const((N_SEC, D_MODEL, SEC_W)), row(D_MODEL), const((8, 128))]
                 + _section_specs(dsecs, tm),
        out_specs=[row(D_MODEL), const((1, D_MODEL))],
        out_shape=[jax.ShapeDtypeStruct((s, D_MODEL), F32), jax.ShapeDtypeStruct((1, D_MODEL), F32)],
        compiler_params=_params("arbitrary"),
    )(x, norm_w, w_all, dh, token, *[a for a, _ in dsecs])


def _inproj_bwd_w(x, norm_w, dsec, dq_r, dk_r, dv, cos, sin, tm=512):
    s = x.shape[0]
    nb = s // tm

    def body(x_ref, nw_ref, s0, s1, s2, s3, s7, q1, q2, q3, k1, k2, k3, v1, v2, v3, cos_ref, sin_ref,
             gw_hbm, datt_ref, acc_ref, stage_ref, nat_ref):
        @pl.when(pl.program_id(0) == 0)
        def _():
            acc_ref[...] = jnp.zeros_like(acc_ref)

        def total(refs):
            acc = refs[0][...].astype(F32)
            for d, ref in zip(DILATIONS[1:], refs[1:]):
                _from_view(ref, nat_ref, d)
                acc = acc + _get_lanes(nat_ref)
            return acc

        c, sn = cos_ref[...], -sin_ref[...]
        unrot = lambda a: jnp.concatenate(
            [_rope128(a[:, j * LANES:(j + 1) * LANES], c, sn) for j in range(LANE_GROUPS)], axis=-1)
        att = [a.astype(MM) for a in (unrot(total((q1, q2, q3))), unrot(total((k1, k2, k3))), total((v1, v2, v3)))]
        for j, a in enumerate(att):
            datt_ref[j] = a
        xv = x_ref[...]
        rstd = lax.rsqrt(jnp.mean(xv * xv, axis=-1, keepdims=True) + NORM_EPS)
        u_t = (xv * rstd * nw_ref[...]).T.astype(MM)
        for j, dsj in enumerate((s0[...], s1[...], s2[...], s3[...], *att, s7[...])):
            acc_ref[j] += jnp.dot(u_t, dsj, preferred_element_type=F32)

        @pl.when(pl.program_id(0) == nb - 1)
        def _():
            for j in range(N_SEC):
                stage_ref[...] = acc_ref[j].astype(stage_ref.dtype)
                pltpu.sync_copy(stage_ref, gw_hbm.at[j])

    row = lambda w: pl.BlockSpec((tm, w), lambda i: (i, 0))
    return pl.pallas_call(
        body, name="inproj_bwd_w", grid=(nb,),
        in_specs=[row(D_MODEL), pl.BlockSpec((1, D_MODEL), lambda i: (0, 0))] + [row(SEC_W)] * 5
                 + [_view_spec(tm, d) for d in DILATIONS] * 3 + [row(LANES), row(LANES)],
        out_specs=[pl.BlockSpec(memory_space=pl.ANY), pl.BlockSpec((3, tm, SEC_W), lambda i: (0, i, 0))],
        out_shape=[jax.ShapeDtypeStruct((N_SEC, D_MODEL, SEC_W), XCH), jax.ShapeDtypeStruct((3, s, SEC_W), MM)],
        scratch_shapes=[pltpu.VMEM((N_SEC, D_MODEL, SEC_W), F32), pltpu.VMEM((D_MODEL, SEC_W), XCH),
                        pltpu.VMEM((LANE_GROUPS, tm, LANES), F32)],
        compiler_params=_params("arbitrary"),
    )(x, norm_w, *dsec, *dq_r, *dk_r, *dv, cos, sin)


def _local_step(x, tgt, norm_w, w_all, lb_logits, hg_norm_w, wo_all, final_norm_w, on_w_out_grad, on_w_in_grad):
    s = x.shape[0]
    cos, sin = _rope_tables(s)
    proj, *qkv = _inproj_fwd(x, norm_w, w_all, cos, sin)
    o_hg, sst, a_hg = _hgrn_fwd(proj, lb_logits)
    qkv = [qkv[3 * i:3 * i + 3] for i in range(len(DILATIONS))]
    att = [_attn_fwd(*qkv_d, d) for qkv_d, d in zip(qkv, DILATIONS)]
    (dh, d_ohg, d_hgz, d_atz, do1, do2, do3, dl1, dl2, dl3, lp1, lp2, lp3, gwo, gfw, ghg, loss) = _mid(
        x, tgt, proj, o_hg, [a[0] for a in att], [a[1] for a in att], hg_norm_w, final_norm_w[None, :], wo_all(o_hg))
    dxq, dxf, dxi, dlb = _hgrn_bwd(proj, lb_logits, d_ohg, sst, a_hg, on_w_out_grad(gwo))
    dq_r, dk_r, dv = [], [], []
    for d, qkv_d, do, lp, dl in zip(DILATIONS, qkv, (do1, do2, do3), (lp1, lp2, lp3), (dl1, dl2, dl3)):
        dq_r.append(_attn_bwd_dq(*qkv_d, do, lp, dl, d))
        dk_d, dv_d = _attn_bwd_dkv(*qkv_d, do, lp, dl, d)
        dk_r.append(dk_d)
        dv.append(dv_d)
    gwi, d_att = _inproj_bwd_w(x, norm_w, (dxq, dxf, dxi, d_hgz, d_atz), dq_r, dk_r, dv, cos, sin)
    dsecs = [(dxq, None), (dxf, None), (dxi, None), (d_hgz, None), (d_att, 0), (d_att, 1), (d_att, 2), (d_atz, None)]
    token = on_w_in_grad(gwi)
    gx, gnw = _inproj_bwd_x(x, norm_w, w_all, dh, dsecs, token)
    small = jnp.concatenate([gnw, jnp.concatenate([dlb, ghg], axis=-1), gfw,
                             jnp.pad(loss, ((0, 0), (0, D_MODEL - 1)))], axis=0)
    return gx, small


def _coords():
    return lax.axis_index("x"), lax.axis_index("y"), lax.axis_index("c")


def _gather_weights(w_in):
    def body(wi_ref, wi_all, send_sems, recv_sems):
        x, y, c = _coords()
        me, sibling = (x, y, c), (x, y, 1 - c)
        chips = [(1 - x, y), (x, 1 - y), (1 - x, 1 - y)]
        slot = lambda p: 4 * p[0] + 2 * p[1] + p[2]

        def copies(k, block, to):
            return [pltpu.make_async_remote_copy(
                src_ref=wi_all.at[slot(block)], dst_ref=wi_all.at[slot(block)], send_sem=send_sems.at[k],
                recv_sem=recv_sems.at[k], device_id=to, device_id_type=MESH)]

        wi_all[slot(me)] = wi_ref[...].astype(MM)
        first = copies(0, me, sibling)
        for j, chip in enumerate(chips):
            first += copies(1 + j, me, (*chip, c))
        for cp in first:
            cp.start()
        passed = []
        for j, chip in enumerate(chips):
            for cp in copies(1 + j, (*chip, c), me):
                cp.wait_recv()
            fwd = copies(4 + j, (*chip, c), sibling)
            for cp in fwd:
                cp.start()
            passed += fwd
        for cp in copies(0, sibling, me):
            cp.wait_recv()
        for j, chip in enumerate(chips):
            for cp in copies(4 + j, (*chip, 1 - c), me):
                cp.wait_recv()
        for cp in first + passed:
            cp.wait_send()

    vmem = pl.BlockSpec(memory_space=pltpu.VMEM)
    return pl.pallas_call(
        body, name="gather_weights",
        in_specs=[vmem], out_specs=vmem,
        out_shape=jax.ShapeDtypeStruct((N_DEV, D_MODEL, SEC_W), MM),
        scratch_shapes=[pltpu.SemaphoreType.DMA((7,)), pltpu.SemaphoreType.DMA((7,))],
        compiler_params=pltpu.CompilerParams(vmem_limit_bytes=VMEM_LIMIT),
    )(w_in)


def _me():
    x, y, c = _coords()
    return 4 * x + 2 * y + c


def _grad_copies(srcs, lands, send_sems, recv_sems):
    x, y, c = _coords()
    me = 4 * x + 2 * y + c
    copies = []
    for k in range(1, N_DEV):
        px, py, pc = x ^ (k >> 2), y ^ ((k >> 1) & 1), c ^ (k & 1)
        peer = 4 * px + 2 * py + pc
        for a, (src, dst) in enumerate(zip(srcs, lands)):
            copies.append(pltpu.make_async_remote_copy(
                src_ref=src.at[peer], dst_ref=dst.at[me], send_sem=send_sems.at[a * (N_DEV - 1) + k - 1],
                recv_sem=recv_sems.at[a * (N_DEV - 1) + k - 1], device_id=(px, py, pc), device_id_type=MESH))
    return copies


HBM_SPEC = pl.BlockSpec(memory_space=pltpu.HBM)
SEM_SPEC = pl.BlockSpec(memory_space=pltpu.SEMAPHORE)
SPLIT_COPY_EFFECT = pltpu.SideEffectType.DATAFLOW_SIDE_EFFECTING


def _exchange_start(name, *arrs):
    n = len(arrs)

    def body(*refs):
        for cp in _grad_copies(refs[:n], refs[n:2 * n], refs[2 * n], refs[2 * n + 1]):
            cp.start()
        refs[-1][...] = jnp.zeros_like(refs[-1])

    hbm = lambda a: pltpu.with_memory_space_constraint(a, pltpu.HBM)
    bufs = (*arrs, *[lax.empty(a.shape, a.dtype) for a in arrs])
    return pl.pallas_call(
        body, name=name,
        out_shape=(pltpu.SemaphoreType.DMA((n * (N_DEV - 1),)), pltpu.SemaphoreType.DMA((n * (N_DEV - 1),)),
                   *[pltpu.HBM(a.shape, a.dtype) for a in bufs], jax.ShapeDtypeStruct((8, 128), F32)),
        in_specs=[HBM_SPEC] * (2 * n),
        out_specs=(SEM_SPEC, SEM_SPEC, *[HBM_SPEC] * (2 * n), pl.BlockSpec(memory_space=pltpu.VMEM)),
        input_output_aliases={i: i + 2 for i in range(2 * n)},
        compiler_params=pltpu.CompilerParams(has_side_effects=SPLIT_COPY_EFFECT),
    )(*[hbm(a) for a in bufs])


def _exchange_wait(name, send_sems, recv_sems, *bufs_after):
    *bufs, after = bufs_after
    n = len(bufs) // 2

    def body(*refs):
        for cp in _grad_copies(refs[:n], refs[n:2 * n], refs[2 * n], refs[2 * n + 1]):
            cp.wait_send()
            cp.wait_recv()

    return pl.pallas_call(
        body, name=name,
        out_shape=tuple(pltpu.HBM(a.shape, a.dtype) for a in bufs),
        in_specs=[HBM_SPEC] * (2 * n) + [SEM_SPEC, SEM_SPEC, pl.BlockSpec(memory_space=pl.ANY)],
        out_specs=(HBM_SPEC,) * (2 * n),
        input_output_aliases={i: i for i in range(2 * n)},
        compiler_params=pltpu.CompilerParams(has_side_effects=SPLIT_COPY_EFFECT),
    )(*bufs, send_sems, recv_sems, after)


def _gather_small(small):
    def body(sm_ref, ls_ref, send_sems, recv_sems, local_sem):
        x, y, c = _coords()
        me = 4 * x + 2 * y + c
        own = pltpu.make_async_copy(sm_ref, ls_ref.at[me], local_sem)
        own.start()
        sends = []
        for k in range(1, N_DEV):
            peer = (x ^ (k >> 2), y ^ ((k >> 1) & 1), c ^ (k & 1))
            sends.append(pltpu.make_async_remote_copy(
                src_ref=sm_ref, dst_ref=ls_ref.at[me], send_sem=send_sems.at[k - 1], recv_sem=recv_sems.at[k - 1],
                device_id=peer, device_id_type=MESH))
        for cp in sends:
            cp.start()
        for cp in sends:
            cp.wait_recv()
        for cp in sends:
            cp.wait_send()
        own.wait()

    vmem = pl.BlockSpec(memory_space=pltpu.VMEM)
    return pl.pallas_call(
        body, name="gather_small", in_specs=[vmem], out_specs=vmem,
        out_shape=jax.ShapeDtypeStruct((N_DEV,) + small.shape, F32),
        scratch_shapes=[pltpu.SemaphoreType.DMA((N_DEV - 1,)), pltpu.SemaphoreType.DMA((N_DEV - 1,)),
                        pltpu.SemaphoreType.DMA],
    )(small)


def _adamw(w, g, m, v):
    m = ADAM_B1 * m + (1.0 - ADAM_B1) * g
    v = ADAM_B2 * v + (1.0 - ADAM_B2) * (g * g)
    m_hat = m / (1.0 - ADAM_B1 ** ADAM_STEP)
    v_hat = v / (1.0 - ADAM_B2 ** ADAM_STEP)
    return -ADAM_LR * (m_hat / (jnp.sqrt(v_hat) + ADAM_EPS) + ADAM_WD * w), m, v


def _slot_sum(ref, own=None, me=None):
    g = None
    for i in range(N_DEV):
        term = ref[i].astype(F32)
        if own is not None:
            term = jnp.where(i == me, own, term)
        g = term if g is None else g + term
    return g


def _update_matrix(name, me, landed, own, w, m, v, rows):
    r, c = w.shape

    def body(me_ref, l_ref, own_ref, w_ref, m_ref, v_ref, g_ref, d_ref, nm_ref, nv_ref):
        g = _slot_sum(l_ref, own_ref[...].astype(F32), me_ref[0])
        g_ref[...] = g
        d_ref[...], nm_ref[...], nv_ref[...] = _adamw(w_ref[...], g, m_ref[...], v_ref[...])

    blk = pl.BlockSpec((rows, c), lambda i, me_ref: (i, 0))
    return pl.pallas_call(
        body, name=name,
        grid_spec=pltpu.PrefetchScalarGridSpec(
            num_scalar_prefetch=1, grid=(r // rows,),
            in_specs=[pl.BlockSpec((N_DEV, rows, c), lambda i, me_ref: (0, i, 0)),
                      pl.BlockSpec((None, rows, c), lambda i, me_ref: (me_ref[0], i, 0)), blk, blk, blk],
            out_specs=[blk] * 4),
        out_shape=[jax.ShapeDtypeStruct((r, c), F32)] * 4,
        compiler_params=_params("parallel"),
    )(me, landed, own, w, m, v)


def _update_small(landed, lb_logits, ws, ms, vs):
    def body(l_ref, lbl_ref, w_ref, m_ref, v_ref, g_ref, d_ref, nm_ref, nv_ref, loss_ref):
        tot = _slot_sum(l_ref)
        _, dlb = _lower_bound(lbl_ref[...])
        g_lb = tot[1:2, :SEC_W] * dlb
        g = jnp.concatenate([tot[0:1], jnp.concatenate([g_lb, -g_lb], axis=-1),
                             jnp.pad(tot[1:2, SEC_W:], ((0, 0), (0, SEC_W))), tot[2:3]], axis=0)
        g_ref[...] = g
        d_ref[...], nm_ref[...], nv_ref[...] = _adamw(w_ref[...], g, m_ref[...], v_ref[...])
        loss_ref[...] = tot[3:4, 0:1]

    vmem = pl.BlockSpec(memory_space=pltpu.VMEM)
    return pl.pallas_call(
        body, name="update_small", in_specs=[vmem] * 5, out_specs=[vmem] * 5,
        out_shape=[jax.ShapeDtypeStruct((4, D_MODEL), F32)] * 4 + [jax.ShapeDtypeStruct((1, 1), F32)],
    )(landed, lb_logits, ws, ms, vs)


def _pack_small(norm_w, lb_logits, hg_norm_w, final_norm_w):
    return jnp.concatenate([norm_w, lb_logits.reshape(1, D_MODEL),
                            jnp.pad(hg_norm_w, ((0, 0), (0, D_MODEL - SEC_W))), final_norm_w[None, :]], axis=0)


def _unpack_small(a):
    return a[0:1], a[1].reshape(2, SEC_W), a[2:3, :SEC_W], a[3]


def kernel(x, norm_w, w_in, hgrn_lb_logits, hg_norm_w, w_out, final_norm_w, loss_target, m_norm_w, m_w_in, m_hgrn_lb_logits, m_hg_norm_w, m_w_out, m_final_norm_w, v_norm_w, v_w_in, v_hgrn_lb_logits, v_hg_norm_w, v_w_out, v_final_norm_w):
    w_all = _gather_weights(w_in[0])
    wo_own = w_out[0].astype(MM)
    wo_blocks, w_all = lax.optimization_barrier((jnp.broadcast_to(wo_own[None], (N_DEV,) + wo_own.shape), w_all))
    *gathering_wo, wo_started = _exchange_start("gather_start_w_out", wo_blocks)
    norm_w_then = norm_w + wo_started[:1, :1]

    def wo_all_after(after):
        _, landed = _exchange_wait("gather_wait_w_out", *gathering_wo, after)
        return lax.dynamic_update_slice(landed, wo_own[None], (_me(), 0, 0)).reshape(D_MODEL, D_MODEL)

    flying_wo, flying_wi = [], []

    def start_w_out(gwo):
        *handles, token = _exchange_start("exchange_start_w_out", gwo.reshape(N_DEV, D_MODEL // N_DEV, D_MODEL))
        flying_wo.extend(handles)
        return token

    def start_w_in(gwi):
        *handles, token = _exchange_start("exchange_start_w_in", gwi)
        flying_wi.extend(handles)
        return token

    gx, small = _local_step(x[0], loss_target[0], norm_w_then, w_all, hgrn_lb_logits, hg_norm_w,
                            wo_all_after, final_norm_w, start_w_out, start_w_in)
    ls = _gather_small(small)
    gwo, lo = _exchange_wait("exchange_wait_w_out", *flying_wo, gx)
    gwi, li = _exchange_wait("exchange_wait_w_in", *flying_wi, gx)
    me = _me().astype(jnp.int32).reshape(1)
    g_wi, d_wi, nm_wi, nv_wi = _update_matrix("update_w_in", me, li, gwi, w_in[0], m_w_in[0], v_w_in[0], 256)
    g_wo, d_wo, nm_wo, nv_wo = _update_matrix("update_w_out", me, lo, gwo, w_out[0], m_w_out[0], v_w_out[0], 128)
    g_s, d_s, nm_s, nv_s, loss = _update_small(
        ls, hgrn_lb_logits, _pack_small(norm_w, hgrn_lb_logits, hg_norm_w, final_norm_w),
        _pack_small(m_norm_w, m_hgrn_lb_logits, m_hg_norm_w, m_final_norm_w),
        _pack_small(v_norm_w, v_hgrn_lb_logits, v_hg_norm_w, v_final_norm_w))
    outs = []
    for small_out, wi, wo in ((g_s, g_wi, g_wo), (d_s, d_wi, d_wo), (nm_s, nm_wi, nm_wo), (nv_s, nv_wi, nv_wo)):
        nw, lb, hg, fw = _unpack_small(small_out)
        outs += [nw, wi[None], lb, hg, wo[None], fw]
    return (loss[0, 0], gx[None], *outs)
```

```python
import functools

import jax
import jax.numpy as jnp
import numpy as np
from jax import lax
from jax.experimental import pallas as pl
from jax.experimental.pallas import tpu as pltpu

F32 = jnp.float32
MM = jnp.bfloat16
XCH = jnp.bfloat16
NORM_EPS = 1e-6
NEG = -1e30
N_DEV = 8
D_MODEL = 1024
N_SEC = 8
SEC_W = 512
HG_HEADS = 4
HG_D = 128
HG_GROUP = 4
AT_DH = 64
LANES = 128
ATT_BLK = 128
AT_COLS = 512
AT_QB = 8
DILATIONS = (1, 4, 16)
ROPE_THETA = 10000.0
CH = 16
LB_LO, LB_HI = 1e-6, 1.0 - 1e-6
ADAM_LR, ADAM_B1, ADAM_B2, ADAM_EPS, ADAM_WD, ADAM_STEP = 0.001, 0.9, 0.999, 1e-08, 0.01, 10
VMEM_LIMIT = 56 * 1024 * 1024
MESH = pl.DeviceIdType.MESH


def _params(*sem):
    return pltpu.CompilerParams(dimension_semantics=sem, vmem_limit_bytes=VMEM_LIMIT)


def _sigmoid(x):
    return 1.0 / (1.0 + jnp.exp(-x))


def _dot(a, b):
    return jnp.dot(a.astype(MM), b.astype(MM), preferred_element_type=F32)


def _dot_nt(a, b):
    return lax.dot_general(a.astype(MM), b.astype(MM), (((1,), (1,)), ((), ())), preferred_element_type=F32)


def _dot_tn(a, b):
    return lax.dot_general(a.astype(MM), b.astype(MM), (((0,), (0,)), ((), ())), preferred_element_type=F32)


def _split3(g):
    g1 = g.astype(jnp.bfloat16)
    r1 = g - g1.astype(F32)
    g2 = r1.astype(jnp.bfloat16)
    return g1, g2, (r1 - g2.astype(F32)).astype(jnp.bfloat16)


def _tri_dot(tri, g):
    t = tri.astype(jnp.bfloat16)
    g1, g2, g3 = _split3(g)
    d = functools.partial(jnp.dot, preferred_element_type=F32)
    return d(t, g1) + d(t, g2) + d(t, g3)


def _lower_bound(lbl):
    l0, l1 = lbl[0:1, :], lbl[1:2, :]
    m = jnp.maximum(l0, l1)
    e0, e1 = jnp.exp(l0 - m), jnp.exp(l1 - m)
    p = e0 / (e0 + e1)
    inside = (p >= LB_LO) & (p <= LB_HI)
    return jnp.clip(p, LB_LO, LB_HI), jnp.where(inside, p * (e1 / (e0 + e1)), 0.0)


def _iota2(shape, dim):
    return lax.broadcasted_iota(jnp.int32, shape, dim)


def _hgrn_gates(xq, xf, lb):
    sgq = _sigmoid(xq)
    sg = _sigmoid(xf)
    sn = _sigmoid(-xf)
    f = lb + (1.0 - lb) * sg
    return sgq, xq * sgq, sg, sn, f, (1.0 - lb) * sn


def _bdot(a, b, ca, cb):
    return lax.dot_general(a.astype(MM), b.astype(MM), (((ca,), (cb,)), ((0,), (0,))), preferred_element_type=F32)


def _chunk_masks(rb):
    row, col = _iota2((rb, rb), 0), _iota2((rb, rb), 1)
    same = (row // CH) == (col // CH)
    return same & (row >= col), same & (row <= col)


HALF = CH // 2
SLAB_ROWS = HALF * CH + (HALF // 2) * CH


def _write_slabs(slab_ref, g, q3, b3):
    slab = lambda t, rows: q3[:, rows, :] * jnp.exp(jnp.minimum(b3[:, rows, :] - b3[:, t:t + 1, :], 0.0))
    late = slice(HALF, CH)
    for t in range(HALF):
        slab_ref[g, :, t * CH:(t + 1) * CH, :] = slab(t, slice(0, CH)).astype(MM)
    for p in range(HALF // 2):
        t = HALF + 2 * p
        two = jnp.concatenate([slab(t, late), slab(t + 1, late)], axis=1)
        slab_ref[g, :, (HALF + p) * CH:(HALF + p + 1) * CH, :] = two.astype(MM)


def _read_diag(r):
    nc = r.shape[0]
    col, col_late = _iota2((nc, CH, CH), 2), _iota2((nc, HALF, CH), 2)
    a, a_late = jnp.zeros((nc, CH, CH), F32), jnp.zeros((nc, HALF, CH), F32)
    for t in range(HALF):
        a = a + jnp.where(col == t, r[:, t * CH:(t + 1) * CH, :], 0.0)
    for p in range(HALF // 2):
        t, two = HALF + 2 * p, r[:, (HALF + p) * CH:(HALF + p + 1) * CH, :]
        a_late = a_late + jnp.where(col_late == t, two[:, :HALF, :], 0.0) + jnp.where(col_late == t + 1, two[:, HALF:, :], 0.0)
    return a + jnp.concatenate([jnp.zeros_like(a_late), a_late], axis=1)


def _hgrn_fwd(proj, lb_logits, rb=256):
    s = proj.shape[1]
    nb, nc = s // rb, rb // CH

    def body(q_ref, f_ref, i_ref, lbl_ref, o_ref, sst_ref, a_ref, st_ref, slab_ref, states_ref):
        @pl.when(pl.program_id(1) == 0)
        def _():
            st_ref[...] = jnp.zeros_like(st_ref)

        sst_ref[...] = st_ref[...]
        prefix, _ = _chunk_masks(rb)
        c3 = lambda a: a.reshape(nc, CH, HG_D)
        row, col = _iota2((nc, CH, CH), 1), _iota2((nc, CH, CH), 2)
        heads = []
        for g in range(HG_GROUP):
            hs = slice(g * HG_D, (g + 1) * HG_D)
            lb, _ = _lower_bound(lbl_ref[:, hs])
            _, q, _, _, f, kk = _hgrn_gates(q_ref[:, hs], f_ref[:, hs], lb)
            b3 = c3(_tri_dot(prefix, jnp.log(f)))
            q3, kk3, v3 = c3(q), c3(kk), c3(i_ref[:, hs])
            bl3 = b3[:, CH - 1:CH, :]
            _write_slabs(slab_ref, g, q3, b3)
            x_upd = _bdot(v3, kk3 * jnp.exp(bl3 - b3), 1, 1)
            heads.append(dict(hs=hs, kk3=kk3, v3=v3, qe3=q3 * jnp.exp(b3), ebl3=jnp.exp(bl3), x_upd=x_upd))
        for g, hd in enumerate(heads):
            st = st_ref[g]
            for c in range(nc):
                states_ref[g, c] = st
                st = st * hd["ebl3"][c] + hd["x_upd"][c]
            st_ref[g] = st
        for g, hd in enumerate(heads):
            a = _read_diag(_bdot(slab_ref[g], hd["kk3"], 2, 2))
            a = jnp.where(row >= col, a, 0.0)
            a_ref[:, g * CH:(g + 1) * CH] = a.reshape(rb, CH)
            o3 = _bdot(hd["qe3"], states_ref[g], 2, 2) + _bdot(a, hd["v3"], 2, 1)
            o_ref[:, hd["hs"]] = o3.reshape(rb, HG_D)

    wide = HG_GROUP * HG_D
    sec = lambda j: pl.BlockSpec((None, rb, wide), lambda h, i, j=j: (j, i, h))
    return pl.pallas_call(
        body, name="hgrn_fwd", grid=(HG_HEADS // HG_GROUP, nb),
        in_specs=[sec(0), sec(1), sec(2), pl.BlockSpec((2, wide), lambda h, i: (0, h))],
        out_specs=[pl.BlockSpec((rb, wide), lambda h, i: (i, h)),
                   pl.BlockSpec((None, HG_GROUP, HG_D, HG_D), lambda h, i: (i, h, 0, 0)),
                   pl.BlockSpec((rb, HG_GROUP * CH), lambda h, i: (i, h))],
        out_shape=[jax.ShapeDtypeStruct((s, SEC_W), F32),
                   jax.ShapeDtypeStruct((nb, HG_HEADS, HG_D, HG_D), F32),
                   jax.ShapeDtypeStruct((s, HG_HEADS * CH), F32)],
        scratch_shapes=[pltpu.VMEM((HG_GROUP, HG_D, HG_D), F32), pltpu.VMEM((HG_GROUP, nc, SLAB_ROWS, HG_D), MM),
                        pltpu.VMEM((HG_GROUP, nc, HG_D, HG_D), F32)],
        compiler_params=_params("parallel", "arbitrary"),
    )(proj, proj, proj, lb_logits)


def _hgrn_bwd(proj, lb_logits, d_o, sst, a_in, token, rb=256):
    s = proj.shape[1]
    nb, nc = s // rb, rb // CH

    def body(q_ref, f_ref, i_ref, lbl_ref, do_ref, sst_ref, a_ref, tok_ref, dxq_ref, dxf_ref, dxi_ref, dlb_ref,
             dst_ref, states_ref, dstates_ref, lslab_ref, kslab_ref):
        @pl.when(pl.program_id(1) == 0)
        def _():
            dst_ref[...] = jnp.zeros_like(dst_ref)
            dlb_ref[...] = jnp.zeros_like(dlb_ref) + tok_ref[0:1, 0:1]

        prefix, suffix = _chunk_masks(rb)
        c3 = lambda a: a.reshape(nc, CH, HG_D)
        flat = lambda a: a.reshape(rb, HG_D)
        row, col = _iota2((nc, CH, CH), 1), _iota2((nc, CH, CH), 2)
        tril, triu = row >= col, row <= col
        sel = (_iota2((CH, CH * CH), 1) % CH == _iota2((CH, CH * CH), 0)).astype(MM)
        blockdiag = _iota2((nc, CH, CH * CH), 2) // CH == _iota2((nc, CH, CH * CH), 1)
        tile = lambda m: jnp.where(blockdiag, _dot(m.reshape(rb, CH), sel).reshape(nc, CH, CH * CH), 0.0)
        last = _iota2((nc, CH, HG_D), 1) == CH - 1
        heads = []
        for g in range(HG_GROUP):
            hs = slice(g * HG_D, (g + 1) * HG_D)
            lb, _ = _lower_bound(lbl_ref[:, hs])
            xq = q_ref[:, hs]
            sgq, q, sg, sn, f, kk = _hgrn_gates(xq, f_ref[:, hs], lb)
            b3 = c3(_tri_dot(prefix, jnp.log(f)))
            q3, kk3, v3, do3 = c3(q), c3(kk), c3(i_ref[:, hs]), c3(do_ref[:, hs])
            bl3 = b3[:, CH - 1:CH, :]
            eb3, ebl3, dec3 = jnp.exp(b3), jnp.exp(bl3), jnp.exp(bl3 - b3)
            qe3, kd3 = q3 * eb3, kk3 * dec3
            x_upd, y_upd = _bdot(v3, kd3, 1, 1), _bdot(do3, qe3, 1, 1)
            zero, every, early, late = jnp.zeros((nc, HALF, HG_D), F32), slice(0, CH), slice(0, HALF), slice(HALF, CH)
            for t in range(CH):
                bt = b3[:, t:t + 1, :]
                since = lambda rows: q3[:, rows, :] * jnp.exp(jnp.minimum(b3[:, rows, :] - bt, 0.0))
                until = lambda rows: kk3[:, rows, :] * jnp.exp(jnp.minimum(bt - b3[:, rows, :], 0.0))
                if t < HALF:
                    lv, kv = since(every), jnp.concatenate([until(early), zero], axis=1)
                else:
                    lv, kv = jnp.concatenate([zero, since(late)], axis=1), until(every)
                lslab_ref[g, :, t * CH:(t + 1) * CH, :] = lv.astype(MM)
                kslab_ref[g, :, t * CH:(t + 1) * CH, :] = kv.astype(MM)
            d_a = jnp.where(tril, _bdot(do3, v3, 2, 2), 0.0)
            d_at = jnp.where(triu, _bdot(v3, do3, 2, 2), 0.0)
            heads.append(dict(hs=hs, lb=lb, xq=xq, sgq=sgq, sg=sg, sn=sn, f=f, q3=q3, kk3=kk3, v3=v3, do3=do3,
                              eb3=eb3, ebl3=ebl3, dec3=dec3, qe3=qe3, kd3=kd3, x_upd=x_upd, y_upd=y_upd,
                              d_a=d_a, d_at=d_at))
        for g, hd in enumerate(heads):
            st = sst_ref[g]
            for c in range(nc):
                states_ref[g, c] = st
                st = st * hd["ebl3"][c] + hd["x_upd"][c]
            dst = dst_ref[g]
            for c in reversed(range(nc)):
                dstates_ref[g, c] = dst
                dst = dst * hd["ebl3"][c] + hd["y_upd"][c]
            dst_ref[g] = dst
        for g, hd in enumerate(heads):
            q3, v3, do3, kd3 = hd["q3"], hd["v3"], hd["do3"], hd["kd3"]
            states, dstates = states_ref[g], dstates_ref[g]
            hd["dqe"] = _bdot(do3, states, 2, 1)
            hd["dkd"] = _bdot(v3, dstates, 2, 1)
            a = a_ref[:, g * CH:(g + 1) * CH].reshape(nc, CH, CH)
            hd["dv"] = _bdot(kd3, dstates, 2, 2) + _bdot(a, do3, 1, 1)
            hd["dq_in"] = _bdot(tile(hd["d_a"]), kslab_ref[g], 2, 1)
            hd["dk_in"] = _bdot(tile(hd["d_at"]), lslab_ref[g], 2, 1)
            hd["ss"] = jnp.sum(dstates * states, axis=1, keepdims=True)
        for g, hd in enumerate(heads):
            q3, kk3, eb3, ebl3, dec3, qe3, kd3 = (hd[k] for k in ("q3", "kk3", "eb3", "ebl3", "dec3", "qe3", "kd3"))
            dqe, dkd, dq_in, dk_in = hd["dqe"], hd["dkd"], hd["dq_in"], hd["dk_in"]
            dkd_kd = dkd * kd3
            db = dqe * qe3 - dkd_kd + q3 * dq_in - kk3 * dk_in
            dbl = jnp.sum(dkd_kd, axis=1, keepdims=True) + hd["ss"] * ebl3
            dg = _tri_dot(suffix, flat(db + jnp.where(last, dbl, 0.0)))
            df = dg / hd["f"] - flat(dkd * dec3 + dk_in)
            xq, sgq, hs = hd["xq"], hd["sgq"], hd["hs"]
            dxq_ref[:, hs] = (flat(dqe * eb3 + dq_in) * (sgq * (1.0 + xq * (1.0 - sgq)))).astype(MM)
            dxf_ref[:, hs] = (df * (1.0 - hd["lb"]) * hd["sg"] * hd["sn"]).astype(MM)
            dxi_ref[:, hs] = flat(hd["dv"]).astype(MM)
            dlb_ref[:, hs] += jnp.sum(df * hd["sn"], axis=0, keepdims=True)

    wide = HG_GROUP * HG_D
    rev = lambda i: nb - 1 - i
    sec = lambda j: pl.BlockSpec((None, rb, wide), lambda h, i, j=j: (j, rev(i), h))
    blk = pl.BlockSpec((rb, wide), lambda h, i: (rev(i), h))
    state = (pltpu.VMEM((HG_GROUP, nc, HG_D, HG_D), F32), pltpu.VMEM((HG_GROUP, nc, CH * CH, HG_D), MM))
    return pl.pallas_call(
        body, name="hgrn_bwd", grid=(HG_HEADS // HG_GROUP, nb),
        in_specs=[sec(0), sec(1), sec(2), pl.BlockSpec((2, wide), lambda h, i: (0, h)), blk,
                  pl.BlockSpec((None, HG_GROUP, HG_D, HG_D), lambda h, i: (rev(i), h, 0, 0)),
                  pl.BlockSpec((rb, HG_GROUP * CH), lambda h, i: (rev(i), h)),
                  pl.BlockSpec((8, 128), lambda h, i: (0, 0))],
        out_specs=[blk, blk, blk, pl.BlockSpec((1, wide), lambda h, i: (0, h))],
        out_shape=[jax.ShapeDtypeStruct((s, SEC_W), MM)] * 3 + [jax.ShapeDtypeStruct((1, SEC_W), F32)],
        scratch_shapes=[pltpu.VMEM((HG_GROUP, HG_D, HG_D), F32), state[0], state[0], state[1], state[1]],
        compiler_params=_params("parallel", "arbitrary"),
    )(proj, proj, proj, lb_logits, d_o, sst, a_in, token)


def _rope_tables(s):
    half = AT_DH // 2
    inv_freq = np.float32(1.0) / (np.float32(ROPE_THETA) ** (np.arange(half, dtype=np.float32) / np.float32(half)))
    ang = np.arange(s, dtype=np.float32)[:, None] * inv_freq[None, :]
    cos, sin = np.cos(ang), np.sin(ang)
    return np.concatenate([cos] * 4, axis=-1), np.concatenate([-sin, sin] * 2, axis=-1)


def _rope128(x, cos, sin):
    lo = (_iota2(x.shape, 1) % AT_DH) < AT_DH // 2
    rot = jnp.where(lo, pltpu.roll(x, LANES - AT_DH // 2, 1), pltpu.roll(x, AT_DH // 2, 1))
    return x * cos + rot * sin


LANE_GROUPS = SEC_W // LANES


def _set_lanes(ref, val):
    for j in range(LANE_GROUPS):
        ref[j] = val[:, j * LANES:(j + 1) * LANES]


def _get_lanes(ref):
    return jnp.concatenate([ref[j] for j in range(LANE_GROUPS)], axis=-1)


def _to_view(src_ref, dst_ref, d):
    n = src_ref.shape[1] // d
    for r in range(d):
        rows = pl.ds(r, n, stride=d) if d > 1 else slice(None)
        for j in range(LANE_GROUPS):
            c0 = r * SEC_W + j * LANES
            dst_ref[:, c0:c0 + LANES] = src_ref.at[j][rows, :].astype(dst_ref.dtype)


def _from_view(src_ref, dst_ref, d):
    n = dst_ref.shape[1] // d
    for r in range(d):
        for j in range(LANE_GROUPS):
            c0 = r * SEC_W + j * LANES
            dst_ref.at[j][pl.ds(r, n, stride=d), :] = src_ref[:, c0:c0 + LANES].astype(dst_ref.dtype)


def _view_spec(tm, d):
    return pl.BlockSpec((tm // d, d * SEC_W), lambda i: (i, 0))


def _view_shape(s, d, dtype):
    return jax.ShapeDtypeStruct((s // d, d * SEC_W), dtype)


PROJ_KEPT = (0, 1, 2, 3, 7)


def _inproj_fwd(x, norm_w, w_all, cos, sin, tm=512):
    s = x.shape[0]

    def body(x_ref, nw_ref, w_ref, cos_ref, sin_ref, proj_ref, *refs):
        outs, (qs_ref, ks_ref, vs_ref) = refs[:-3], refs[-3:]
        xv = x_ref[...]
        rstd = lax.rsqrt(jnp.mean(xv * xv, axis=-1, keepdims=True) + NORM_EPS)
        u = (xv * rstd * nw_ref[...]).astype(MM)
        for slot, j in enumerate(PROJ_KEPT):
            proj_ref[slot] = jnp.dot(u, w_ref[j], preferred_element_type=F32)
        q, k, v = [jnp.dot(u, w_ref[j], preferred_element_type=F32) for j in (4, 5, 6)]
        c, sn = cos_ref[...], sin_ref[...]
        for g in range(LANE_GROUPS):
            sl = slice(g * LANES, (g + 1) * LANES)
            qs_ref[g] = _rope128(q[:, sl], c, sn) * (AT_DH ** -0.5)
            ks_ref[g] = _rope128(k[:, sl], c, sn)
            vs_ref[g] = v[:, sl]
        for i, d in enumerate(DILATIONS):
            for src_ref, dst_ref in zip((qs_ref, ks_ref, vs_ref), outs[3 * i:3 * i + 3]):
                _to_view(src_ref, dst_ref, d)

    tab = pl.BlockSpec((tm, LANES), lambda i: (i, 0))
    return pl.pallas_call(
        body, name="inproj_fwd", grid=(s // tm,),
        in_specs=[pl.BlockSpec((tm, D_MODEL), lambda i: (i, 0)),
                  pl.BlockSpec((1, D_MODEL), lambda i: (0, 0)),
                  pl.BlockSpec((N_SEC, D_MODEL, SEC_W), lambda i: (0, 0, 0)), tab, tab],
        out_specs=[pl.BlockSpec((len(PROJ_KEPT), tm, SEC_W), lambda i: (0, i, 0))]
                  + [_view_spec(tm, d) for d in DILATIONS for _ in range(3)],
        out_shape=[jax.ShapeDtypeStruct((len(PROJ_KEPT), s, SEC_W), F32)]
                  + [_view_shape(s, d, MM) for d in DILATIONS for _ in range(3)],
        scratch_shapes=[pltpu.VMEM((LANE_GROUPS, tm, LANES), F32)] * 3,
        compiler_params=_params("parallel"),
    )(x, norm_w, w_all, cos, sin)


def _band_mask(first_ok, second_ok):
    row, col = _iota2((ATT_BLK, 2 * ATT_BLK), 0), _iota2((ATT_BLK, 2 * ATT_BLK), 1)
    return ((col < ATT_BLK) & (col >= row) & first_ok) | ((col >= ATT_BLK) & ((col - ATT_BLK) <= row) & second_ok)


def _own_lanes(rows, h):
    lane = _iota2((rows, LANES), 1)
    return (lane < AT_DH) if h == 0 else (lane >= AT_DH)


def _neg_pieces(rows, h):
    lane = _iota2((rows, LANES), 1) - (AT_DH if h == 0 else 0)
    return jnp.where((lane >= 0) & (lane < 3), -1.0, 0.0).astype(MM)


def _units(qb):
    return [(b, slice(g * LANES, (g + 1) * LANES), h) for b in range(qb) for g in range(AT_COLS // LANES) for h in range(2)]


def _sub(b):
    return slice(b * ATT_BLK, (b + 1) * ATT_BLK)


def _band_before(cur_ref, prev_ref, b, sl):
    if b == 0:
        return jnp.concatenate([prev_ref[:, sl], cur_ref[0:ATT_BLK, sl]], axis=0)
    return cur_ref[(b - 1) * ATT_BLK:(b + 1) * ATT_BLK, sl]


def _band_after(cur_ref, next_ref, b, sl):
    if (b + 1) * ATT_BLK == cur_ref.shape[0]:
        return jnp.concatenate([cur_ref[b * ATT_BLK:(b + 1) * ATT_BLK, sl], next_ref[:, sl]], axis=0)
    return cur_ref[b * ATT_BLK:(b + 2) * ATT_BLK, sl]


def _attn_specs(rows):
    qb = min(AT_QB, rows // ATT_BLK)
    assert rows % (qb * ATT_BLK) == 0
    last = rows // ATT_BLK - 1
    cur = pl.BlockSpec((qb * ATT_BLK, AT_COLS), lambda c, n: (n, c))
    prev = pl.BlockSpec((ATT_BLK, AT_COLS), lambda c, n: (jnp.maximum(qb * n - 1, 0), c))
    nxt = pl.BlockSpec((ATT_BLK, AT_COLS), lambda c, n: (jnp.minimum(qb * (n + 1), last), c))
    return qb, cur, prev, nxt


def _stack_heads(a):
    h0 = _own_lanes(a.shape[0], 0)
    zero = jnp.zeros_like(a)
    return jnp.concatenate([jnp.where(h0, a, zero), jnp.where(h0, zero, a)], axis=0)


def _unstack_heads(a2):
    return jnp.where(_own_lanes(ATT_BLK, 0), a2[:ATT_BLK], a2[ATT_BLK:])


def _attn_fwd(qr, kr, vr, d):
    rows, cols = qr.shape
    qb, cur, prev, nxt = _attn_specs(rows)
    nb = rows // (qb * ATT_BLK)

    def body(q_ref, kc_ref, kp_ref, vc_ref, vp_ref, o_ref, lse_ref):
        twice = lambda m: jnp.concatenate([m, m], axis=0)
        masks = {True: twice(_band_mask(pl.program_id(1) > 0, True)), False: twice(_band_mask(True, True))}
        ones = jnp.ones((2 * ATT_BLK, LANES), MM)
        units = [(b, sl) for b, sl, h in _units(qb) if h == 0]
        scs = [jnp.where(masks[b == 0], _dot_nt(_stack_heads(q_ref[_sub(b), sl]), _band_before(kc_ref, kp_ref, b, sl)),
                         NEG) for b, sl in units]
        ms = [jnp.max(sc, axis=-1, keepdims=True) for sc in scs]
        ps = [jnp.exp(sc - m).astype(MM) for sc, m in zip(scs, ms)]
        ols = [jnp.dot(p, jnp.concatenate([_band_before(vc_ref, vp_ref, b, sl), ones], axis=1),
                       preferred_element_type=F32) for p, (b, sl) in zip(ps, units)]
        for (b, sl), m, ol in zip(units, ms, ols):
            l = _unstack_heads(ol[:, LANES:])
            o_ref[_sub(b), sl] = _unstack_heads(ol[:, :LANES]) / l
            lse_ref[_sub(b), sl] = _unstack_heads(jnp.broadcast_to(m, (2 * ATT_BLK, LANES))) + jnp.log(l)

    o, lse = pl.pallas_call(
        body, name=f"attn_fwd_d{d}", grid=(cols // AT_COLS, nb),
        in_specs=[cur, cur, prev, cur, prev], out_specs=[cur, cur],
        out_shape=[jax.ShapeDtypeStruct((rows, cols), F32)] * 2,
        compiler_params=_params("parallel", "parallel"),
    )(qr, kr, kr, vr, vr)
    return o, lse


def _attn_bwd_dq(qr, kr, vr, do, lse, delta, d):
    rows, cols = qr.shape
    qb, cur, prev, nxt = _attn_specs(rows)
    nb = rows // (qb * ATT_BLK)

    def body(q_ref, kc_ref, kp_ref, vc_ref, vp_ref, do_ref, lse_ref, dl_ref, dq_ref):
        masks = {True: _band_mask(pl.program_id(1) > 0, True), False: _band_mask(True, True)}
        units = _units(qb)
        sms, dps = [], []
        for b, sl, h in units:
            own, own_b, neg = _own_lanes(ATT_BLK, h), _own_lanes(2 * ATT_BLK, h), _neg_pieces(2 * ATT_BLK, h)
            sms.append(_dot_nt(jnp.where(own, q_ref[_sub(b), sl], lse_ref[_sub(b), sl]),
                               jnp.where(own_b, _band_before(kc_ref, kp_ref, b, sl), neg)))
            dps.append(_dot_nt(jnp.where(own, do_ref[_sub(b), sl], dl_ref[_sub(b), sl]),
                               jnp.where(own_b, _band_before(vc_ref, vp_ref, b, sl), neg)))
        dss = [(jnp.exp(jnp.where(masks[b == 0], sm, NEG)) * dp).astype(MM)
               for sm, dp, (b, _, _) in zip(sms, dps, units)]
        dqs = [jnp.dot(ds, _band_before(kc_ref, kp_ref, b, sl), preferred_element_type=F32) * (AT_DH ** -0.5)
               for ds, (b, sl, _) in zip(dss, units)]
        for i in range(0, len(units), 2):
            b, sl, _ = units[i]
            dq_ref[_sub(b), sl] = jnp.where(_own_lanes(ATT_BLK, 0), dqs[i], dqs[i + 1]).astype(dq_ref.dtype)

    dq = pl.pallas_call(
        body, name=f"attn_bwd_dq_d{d}", grid=(cols // AT_COLS, nb),
        in_specs=[cur, cur, prev, cur, prev, cur, cur, cur], out_specs=cur,
        out_shape=jax.ShapeDtypeStruct((rows, cols), MM),
        compiler_params=_params("parallel", "parallel"),
    )(qr, kr, kr, vr, vr, do, lse, delta)
    return dq


def _attn_bwd_dkv(qr, kr, vr, do, lse, delta, d):
    rows, cols = qr.shape
    qb, cur, prev, nxt = _attn_specs(rows)
    nb = rows // (qb * ATT_BLK)

    def body(k_ref, v_ref, qc_ref, qn_ref, doc_ref, don_ref, lsec_ref, lsen_ref, dlc_ref, dln_ref,
             dk_ref, dv_ref):
        masks = {True: _band_mask(True, pl.program_id(1) < nb - 1), False: _band_mask(True, True)}
        units = _units(qb)
        sms, dps = [], []
        for b, sl, h in units:
            own, own_b, neg = _own_lanes(ATT_BLK, h), _own_lanes(2 * ATT_BLK, h), _neg_pieces(ATT_BLK, h)
            sms.append(_dot_nt(jnp.where(own, k_ref[_sub(b), sl], neg),
                               jnp.where(own_b, _band_after(qc_ref, qn_ref, b, sl),
                                         _band_after(lsec_ref, lsen_ref, b, sl))))
            dps.append(_dot_nt(jnp.where(own, v_ref[_sub(b), sl], neg),
                               jnp.where(own_b, _band_after(doc_ref, don_ref, b, sl),
                                         _band_after(dlc_ref, dln_ref, b, sl))))
        ps = [jnp.exp(jnp.where(masks[b == qb - 1], sm, NEG)) for sm, (b, _, _) in zip(sms, units)]
        dss = [(p * dp).astype(MM) for p, dp in zip(ps, dps)]
        dvs = [jnp.dot(p.astype(MM), _band_after(doc_ref, don_ref, b, sl), preferred_element_type=F32)
               for p, (b, sl, _) in zip(ps, units)]
        dks = [jnp.dot(ds, _band_after(qc_ref, qn_ref, b, sl), preferred_element_type=F32)
               for ds, (b, sl, _) in zip(dss, units)]
        head0 = _own_lanes(ATT_BLK, 0)
        for i in range(0, len(units), 2):
            b, sl, _ = units[i]
            dk_ref[_sub(b), sl] = jnp.where(head0, dks[i], dks[i + 1]).astype(dk_ref.dtype)
            dv_ref[_sub(b), sl] = jnp.where(head0, dvs[i], dvs[i + 1]).astype(dv_ref.dtype)

    dk, dv = pl.pallas_call(
        body, name=f"attn_bwd_dkv_d{d}", grid=(cols // AT_COLS, nb),
        in_specs=[cur, cur, cur, nxt, cur, nxt, cur, nxt, cur, nxt], out_specs=[cur, cur],
        out_shape=[jax.ShapeDtypeStruct((rows, cols), MM)] * 2,
        compiler_params=_params("parallel", "parallel"),
    )(kr, vr, qr, qr, do, do, lse, lse, delta, delta)
    return dk, dv


def _head_sum(a, width):
    parts = []
    for j in range(a.shape[1] // width):
        sm = jnp.sum(a[:, j * width:(j + 1) * width], axis=-1, keepdims=True)
        parts.append(jnp.broadcast_to(sm, (a.shape[0], width)))
    return jnp.concatenate(parts, axis=-1)


def _partner_sum(a):
    swap = (_iota2((LANES, LANES), 0) // AT_DH != _iota2((LANES, LANES), 1) // AT_DH).astype(jnp.bfloat16)
    d = functools.partial(jnp.dot, preferred_element_type=F32)
    parts = _split3(a)
    return jnp.concatenate([d(parts[0][:, sl], swap) + d(parts[1][:, sl], swap) + d(parts[2][:, sl], swap)
                            for sl in (slice(j * LANES, (j + 1) * LANES) for j in range(a.shape[1] // LANES))], axis=-1)


def _partner_value(x):
    return jnp.concatenate([pltpu.roll(x[:, j * LANES:(j + 1) * LANES], AT_DH, 1) for j in range(x.shape[1] // LANES)],
                           axis=-1)


def _pieces(xs):
    hi = xs.astype(jnp.bfloat16).astype(F32)
    mid = (xs - hi).astype(jnp.bfloat16).astype(F32)
    lo = (xs - hi - mid).astype(jnp.bfloat16).astype(F32)
    lane = _iota2(xs.shape, 1) % AT_DH
    return jnp.where(lane == 0, hi, jnp.where(lane == 1, mid, jnp.where(lane == 2, lo, 0.0)))


def _mid(x, tgt, proj, o_hg, o_at, lse_at, hg_norm_w, final_norm_w, wo_all, tm=256):
    s = x.shape[0]
    nb = s // tm

    def body(x_ref, t_ref, hgz_ref, atz_ref, ohg_ref, o1_ref, o2_ref, o3_ref, l1_ref, l2_ref, l3_ref,
             g_ref, fw_ref, wo_ref,
             dh_ref, dohg_ref, dhgz_ref, datz_ref, do1_ref, do2_ref, do3_ref, dl1_ref, dl2_ref, dl3_ref,
             lp1_ref, lp2_ref, lp3_ref,
             gwo_ref, gfw_ref, ghg_ref, loss_ref, nat_ref, stage_ref, gwo_acc):
        @pl.when(pl.program_id(0) == 0)
        def _():
            gwo_acc[...] = jnp.zeros_like(gwo_acc)
            gfw_ref[...] = jnp.zeros_like(gfw_ref)
            ghg_ref[...] = jnp.zeros_like(ghg_ref)
            loss_ref[...] = jnp.zeros_like(loss_ref)

        ohg, g = ohg_ref[...], g_ref[...]
        rs = lax.rsqrt(_head_sum(ohg * ohg, HG_D) * (1.0 / HG_D) + NORM_EPS)
        on = ohg * rs
        hgz = hgz_ref[...]
        sz = _sigmoid(hgz)
        gate_hg = hgz * sz
        lses, outs = [l1_ref[...]], [o1_ref[...]]
        for k, (d, l_ref, o_ref) in enumerate(zip(DILATIONS[1:], (l2_ref, l3_ref), (o2_ref, o3_ref))):
            _from_view(l_ref, nat_ref.at[2 * k], d)
            _from_view(o_ref, nat_ref.at[2 * k + 1], d)
            lses.append(_get_lanes(nat_ref.at[2 * k]))
            outs.append(_get_lanes(nat_ref.at[2 * k + 1]))
        mx = jnp.maximum(jnp.maximum(lses[0], lses[1]), lses[2])
        es = [jnp.exp(l - mx) for l in lses]
        den = es[0] + es[1] + es[2]
        ws = [e / den for e in es]
        oat = ws[0] * outs[0] + ws[1] * outs[1] + ws[2] * outs[2]
        atz = atz_ref[...]
        sa = _sigmoid(atz)
        gate_at = atz * sa
        mixed = jnp.concatenate([on * g * gate_hg, oat * gate_at], axis=-1).astype(MM)
        h = x_ref[...] + jnp.dot(mixed, wo_ref[...], preferred_element_type=F32)
        rstd = lax.rsqrt(jnp.mean(h * h, axis=-1, keepdims=True) + NORM_EPS)
        hn = h * rstd
        fw = fw_ref[...]
        err = hn * fw - t_ref[...]
        loss_ref[...] += 0.5 * jnp.sum(jnp.mean(err * err, axis=-1, keepdims=True), axis=0, keepdims=True)
        dout = err * (1.0 / D_MODEL)
        gfw_ref[...] += jnp.sum(dout * hn, axis=0, keepdims=True)
        dhn = dout * fw
        dh = rstd * (dhn - hn * jnp.mean(dhn * hn, axis=-1, keepdims=True))
        dh_ref[...] = dh
        dh_mm = dh.astype(MM)
        gwo_acc[...] += _dot_tn(mixed, dh_mm)

        @pl.when(pl.program_id(0) == nb - 1)
        def _():
            gwo_ref[...] = gwo_acc[...].astype(gwo_ref.dtype)

        dmixed = _dot_nt(dh_mm, wo_ref[...])
        dm_hg = dmixed[:, :SEC_W]
        d_ong = dm_hg * gate_hg
        dhgz_ref[...] = (dm_hg * (on * g) * (sz * (1.0 + hgz * (1.0 - sz)))).astype(MM)
        ghg_ref[...] += jnp.sum(d_ong * on, axis=0, keepdims=True)
        d_on = d_ong * g
        dohg_ref[...] = rs * (d_on - on * (_head_sum(d_on * on, HG_D) * (1.0 / HG_D)))
        dm_at = dmixed[:, SEC_W:]
        d_oat = dm_at * gate_at
        datz_ref[...] = (dm_at * oat * (sa * (1.0 + atz * (1.0 - sa)))).astype(MM)
        lse_all = mx + jnp.log(den)
        for val, dst_refs in ((d_oat, (do1_ref, do2_ref, do3_ref)),
                              (_pieces(_partner_sum(d_oat * oat)), (dl1_ref, dl2_ref, dl3_ref)),
                              (_pieces(_partner_value(lse_all)), (lp1_ref, lp2_ref, lp3_ref))):
            _set_lanes(stage_ref, val)
            for d, dst_ref in zip(DILATIONS, dst_refs):
                _to_view(stage_ref, dst_ref, d)

    row = lambda w: pl.BlockSpec((tm, w), lambda i: (i, 0))
    sec = lambda j: pl.BlockSpec((None, tm, SEC_W), lambda i, j=j: (j, i, 0))
    const = lambda shp: pl.BlockSpec(shp, lambda i: (0,) * len(shp))
    half = row(SEC_W)
    views = [_view_spec(tm, d) for d in DILATIONS]
    return pl.pallas_call(
        body, name="mid", grid=(nb,),
        in_specs=[row(D_MODEL), row(D_MODEL), sec(PROJ_KEPT.index(3)), sec(PROJ_KEPT.index(7)), half] + views * 2
                 + [const((1, SEC_W)), const((1, D_MODEL)), const((D_MODEL, D_MODEL))],
        out_specs=[row(D_MODEL)] + [half] * 3 + views * 3
                  + [const((D_MODEL, D_MODEL)), const((1, D_MODEL)), const((1, SEC_W)), const((1, 1))],
        out_shape=[jax.ShapeDtypeStruct((s, D_MODEL), F32), jax.ShapeDtypeStruct((s, SEC_W), F32)]
                  + [jax.ShapeDtypeStruct((s, SEC_W), MM)] * 2
                  + [_view_shape(s, d, MM) for d in DILATIONS] * 3
                  + [jax.ShapeDtypeStruct((D_MODEL, D_MODEL), XCH), jax.ShapeDtypeStruct((1, D_MODEL), F32),
                     jax.ShapeDtypeStruct((1, SEC_W), F32), jax.ShapeDtypeStruct((1, 1), F32)],
        scratch_shapes=[pltpu.VMEM((4, LANE_GROUPS, tm, LANES), F32), pltpu.VMEM((LANE_GROUPS, tm, LANES), F32),
                        pltpu.VMEM((D_MODEL, D_MODEL), F32)],
        compiler_params=_params("arbitrary"),
    )(x, tgt, proj, proj, o_hg, *o_at, *lse_at, hg_norm_w, final_norm_w, wo_all)


def _section_specs(dsecs, tm):
    return [pl.BlockSpec((tm, SEC_W), lambda i: (i, 0)) if k is None
            else pl.BlockSpec((None, tm, SEC_W), lambda i, k=k: (k, i, 0)) for _, k in dsecs]


def _inproj_bwd_x(x, norm_w, w_all, dh, dsecs, token, tm=512):
    s = x.shape[0]

    def body(x_ref, nw_ref, w_ref, dh_ref, tok_ref, *refs):
        sec_refs, (gx_ref, gnw_ref) = refs[:N_SEC], refs[N_SEC:]

        @pl.when(pl.program_id(0) == 0)
        def _():
            gnw_ref[...] = jnp.zeros_like(gnw_ref)

        du = jnp.zeros((tm, D_MODEL), F32)
        for j in range(N_SEC):
            du = du + _dot_nt(sec_refs[j][...], w_ref[j])
        xv, nw = x_ref[...], nw_ref[...]
        rstd = lax.rsqrt(jnp.mean(xv * xv, axis=-1, keepdims=True) + NORM_EPS)
        xn = xv * rstd
        gnw_ref[...] += jnp.sum(du * xn, axis=0, keepdims=True)
        dxn = du * nw
        dx = rstd * (dxn - xn * jnp.mean(dxn * xn, axis=-1, keepdims=True))
        gx_ref[...] = (dh_ref[...] + tok_ref[0:1, 0:1]) + dx

    row = lambda w: pl.BlockSpec((tm, w), lambda i: (i, 0))
    const = lambda shp: pl.BlockSpec(shp, lambda i: (0,) * len(shp))
    return pl.pallas_call(
        body, name="inproj_bwd_x", grid=(s // tm,),
        in_specs=[row(D_MODEL), const((1, D_MODEL)), const((N_SEC, D_MODEL, SEC_W)), row(D_MODEL), const((8, 128))]
                 + _section_specs(dsecs, tm),
        out_specs=[row(D_MODEL), const((1, D_MODEL))],
        out_shape=[jax.ShapeDtypeStruct((s, D_MODEL), F32), jax.ShapeDtypeStruct((1, D_MODEL), F32)],
        compiler_params=_params("arbitrary"),
    )(x, norm_w, w_all, dh, token, *[a for a, _ in dsecs])


def _inproj_bwd_w(x, norm_w, dsec, dq_r, dk_r, dv, cos, sin, tm=512):
    s = x.shape[0]
    nb = s // tm

    def body(x_ref, nw_ref, s0, s1, s2, s3, s7, q1, q2, q3, k1, k2, k3, v1, v2, v3, cos_ref, sin_ref,
             gw_hbm, datt_ref, acc_ref, stage_ref, nat_ref):
        @pl.when(pl.program_id(0) == 0)
        def _():
            acc_ref[...] = jnp.zeros_like(acc_ref)

        def total(refs):
            acc = refs[0][...].astype(F32)
            for d, ref in zip(DILATIONS[1:], refs[1:]):
                _from_view(ref, nat_ref, d)
                acc = acc + _get_lanes(nat_ref)
            return acc

        c, sn = cos_ref[...], -sin_ref[...]
        unrot = lambda a: jnp.concatenate(
            [_rope128(a[:, j * LANES:(j + 1) * LANES], c, sn) for j in range(LANE_GROUPS)], axis=-1)
        att = [a.astype(MM) for a in (unrot(total((q1, q2, q3))), unrot(total((k1, k2, k3))), total((v1, v2, v3)))]
        for j, a in enumerate(att):
            datt_ref[j] = a
        xv = x_ref[...]
        rstd = lax.rsqrt(jnp.mean(xv * xv, axis=-1, keepdims=True) + NORM_EPS)
        u_t = (xv * rstd * nw_ref[...]).T.astype(MM)
        for j, dsj in enumerate((s0[...], s1[...], s2[...], s3[...], *att, s7[...])):
            acc_ref[j] += jnp.dot(u_t, dsj, preferred_element_type=F32)

        @pl.when(pl.program_id(0) == nb - 1)
        def _():
            for j in range(N_SEC):
                stage_ref[...] = acc_ref[j].astype(stage_ref.dtype)
                pltpu.sync_copy(stage_ref, gw_hbm.at[j])

    row = lambda w: pl.BlockSpec((tm, w), lambda i: (i, 0))
    return pl.pallas_call(
        body, name="inproj_bwd_w", grid=(nb,),
        in_specs=[row(D_MODEL), pl.BlockSpec((1, D_MODEL), lambda i: (0, 0))] + [row(SEC_W)] * 5
                 + [_view_spec(tm, d) for d in DILATIONS] * 3 + [row(LANES), row(LANES)],
        out_specs=[pl.BlockSpec(memory_space=pl.ANY), pl.BlockSpec((3, tm, SEC_W), lambda i: (0, i, 0))],
        out_shape=[jax.ShapeDtypeStruct((N_SEC, D_MODEL, SEC_W), XCH), jax.ShapeDtypeStruct((3, s, SEC_W), MM)],
        scratch_shapes=[pltpu.VMEM((N_SEC, D_MODEL, SEC_W), F32), pltpu.VMEM((D_MODEL, SEC_W), XCH),
                        pltpu.VMEM((LANE_GROUPS, tm, LANES), F32)],
        compiler_params=_params("arbitrary"),
    )(x, norm_w, *dsec, *dq_r, *dk_r, *dv, cos, sin)


def _local_step(x, tgt, norm_w, w_all, lb_logits, hg_norm_w, wo_all, final_norm_w, on_w_out_grad, on_w_in_grad):
    s = x.shape[0]
    cos, sin = _rope_tables(s)
    proj, *qkv = _inproj_fwd(x, norm_w, w_all, cos, sin)
    o_hg, sst, a_hg = _hgrn_fwd(proj, lb_logits)
    qkv = [qkv[3 * i:3 * i + 3] for i in range(len(DILATIONS))]
    att = [_attn_fwd(*qkv_d, d) for qkv_d, d in zip(qkv, DILATIONS)]
    (dh, d_ohg, d_hgz, d_atz, do1, do2, do3, dl1, dl2, dl3, lp1, lp2, lp3, gwo, gfw, ghg, loss) = _mid(
        x, tgt, proj, o_hg, [a[0] for a in att], [a[1] for a in att], hg_norm_w, final_norm_w[None, :], wo_all(o_hg))
    dxq, dxf, dxi, dlb = _hgrn_bwd(proj, lb_logits, d_ohg, sst, a_hg, on_w_out_grad(gwo))
    dq_r, dk_r, dv = [], [], []
    for d, qkv_d, do, lp, dl in zip(DILATIONS, qkv, (do1, do2, do3), (lp1, lp2, lp3), (dl1, dl2, dl3)):
        dq_r.append(_attn_bwd_dq(*qkv_d, do, lp, dl, d))
        dk_d, dv_d = _attn_bwd_dkv(*qkv_d, do, lp, dl, d)
        dk_r.append(dk_d)
        dv.append(dv_d)
    gwi, d_att = _inproj_bwd_w(x, norm_w, (dxq, dxf, dxi, d_hgz, d_atz), dq_r, dk_r, dv, cos, sin)
    dsecs = [(dxq, None), (dxf, None), (dxi, None), (d_hgz, None), (d_att, 0), (d_att, 1), (d_att, 2), (d_atz, None)]
    token = on_w_in_grad(gwi)
    gx, gnw = _inproj_bwd_x(x, norm_w, w_all, dh, dsecs, token)
    small = jnp.concatenate([gnw, jnp.concatenate([dlb, ghg], axis=-1), gfw,
                             jnp.pad(loss, ((0, 0), (0, D_MODEL - 1)))], axis=0)
    return gx, small


def _coords():
    return lax.axis_index("x"), lax.axis_index("y"), lax.axis_index("c")


def _gather_weights(w_in):
    def body(wi_ref, wi_all, send_sems, recv_sems):
        x, y, c = _coords()
        me, sibling = (x, y, c), (x, y, 1 - c)
        chips = [(1 - x, y), (x, 1 - y), (1 - x, 1 - y)]
        slot = lambda p: 4 * p[0] + 2 * p[1] + p[2]

        def copies(k, block, to):
            return [pltpu.make_async_remote_copy(
                src_ref=wi_all.at[slot(block)], dst_ref=wi_all.at[slot(block)], send_sem=send_sems.at[k],
                recv_sem=recv_sems.at[k], device_id=to, device_id_type=MESH)]

        wi_all[slot(me)] = wi_ref[...].astype(MM)
        first = copies(0, me, sibling)
        for j, chip in enumerate(chips):
            first += copies(1 + j, me, (*chip, c))
        for cp in first:
            cp.start()
        passed = []
        for j, chip in enumerate(chips):
            for cp in copies(1 + j, (*chip, c), me):
                cp.wait_recv()
            fwd = copies(4 + j, (*chip, c), sibling)
            for cp in fwd:
                cp.start()
            passed += fwd
        for cp in copies(0, sibling, me):
            cp.wait_recv()
        for j, chip in enumerate(chips):
            for cp in copies(4 + j, (*chip, 1 - c), me):
                cp.wait_recv()
        for cp in first + passed:
            cp.wait_send()

    vmem = pl.BlockSpec(memory_space=pltpu.VMEM)
    return pl.pallas_call(
        body, name="gather_weights",
        in_specs=[vmem], out_specs=vmem,
        out_shape=jax.ShapeDtypeStruct((N_DEV, D_MODEL, SEC_W), MM),
        scratch_shapes=[pltpu.SemaphoreType.DMA((7,)), pltpu.SemaphoreType.DMA((7,))],
        compiler_params=pltpu.CompilerParams(vmem_limit_bytes=VMEM_LIMIT),
    )(w_in)


def _me():
    x, y, c = _coords()
    return 4 * x + 2 * y + c


def _grad_copies(srcs, lands, send_sems, recv_sems, whole=False):
    x, y, c = _coords()
    me = 4 * x + 2 * y + c
    copies = []
    for k in range(1, N_DEV):
        px, py, pc = x ^ (k >> 2), y ^ ((k >> 1) & 1), c ^ (k & 1)
        peer = 4 * px + 2 * py + pc
        for a, (src, dst) in enumerate(zip(srcs, lands)):
            copies.append(pltpu.make_async_remote_copy(
                src_ref=src if whole else src.at[peer], dst_ref=dst.at[me], send_sem=send_sems.at[a * (N_DEV - 1) + k - 1],
                recv_sem=recv_sems.at[a * (N_DEV - 1) + k - 1], device_id=(px, py, pc), device_id_type=MESH))
    return copies


HBM_SPEC = pl.BlockSpec(memory_space=pltpu.HBM)
SEM_SPEC = pl.BlockSpec(memory_space=pltpu.SEMAPHORE)
SPLIT_COPY_EFFECT = pltpu.SideEffectType.DATAFLOW_SIDE_EFFECTING


def _exchange_start(name, *arrs, whole=False):
    n = len(arrs)

    def body(*refs):
        for cp in _grad_copies(refs[:n], refs[n:2 * n], refs[2 * n], refs[2 * n + 1], whole):
            cp.start()
        refs[-1][...] = jnp.zeros_like(refs[-1])

    hbm = lambda a: pltpu.with_memory_space_constraint(a, pltpu.HBM)
    bufs = (*arrs, *[lax.empty(((N_DEV,) if whole else ()) + a.shape, a.dtype) for a in arrs])
    return pl.pallas_call(
        body, name=name,
        out_shape=(pltpu.SemaphoreType.DMA((n * (N_DEV - 1),)), pltpu.SemaphoreType.DMA((n * (N_DEV - 1),)),
                   *[pltpu.HBM(a.shape, a.dtype) for a in bufs], jax.ShapeDtypeStruct((8, 128), F32)),
        in_specs=[HBM_SPEC] * (2 * n),
        out_specs=(SEM_SPEC, SEM_SPEC, *[HBM_SPEC] * (2 * n), pl.BlockSpec(memory_space=pltpu.VMEM)),
        input_output_aliases={i: i + 2 for i in range(2 * n)},
        compiler_params=pltpu.CompilerParams(has_side_effects=SPLIT_COPY_EFFECT),
    )(*[hbm(a) for a in bufs])


def _exchange_wait(name, send_sems, recv_sems, *bufs_after, whole=False):
    *bufs, after = bufs_after
    n = len(bufs) // 2

    def body(*refs):
        for cp in _grad_copies(refs[:n], refs[n:2 * n], refs[2 * n], refs[2 * n + 1], whole):
            cp.wait_send()
            cp.wait_recv()

    return pl.pallas_call(
        body, name=name,
        out_shape=tuple(pltpu.HBM(a.shape, a.dtype) for a in bufs),
        in_specs=[HBM_SPEC] * (2 * n) + [SEM_SPEC, SEM_SPEC, pl.BlockSpec(memory_space=pl.ANY)],
        out_specs=(HBM_SPEC,) * (2 * n),
        input_output_aliases={i: i for i in range(2 * n)},
        compiler_params=pltpu.CompilerParams(has_side_effects=SPLIT_COPY_EFFECT),
    )(*bufs, send_sems, recv_sems, after)


def _gather_small(small):
    def body(sm_ref, ls_ref, send_sems, recv_sems, local_sem):
        x, y, c = _coords()
        me = 4 * x + 2 * y + c
        own = pltpu.make_async_copy(sm_ref, ls_ref.at[me], local_sem)
        own.start()
        sends = []
        for k in range(1, N_DEV):
            peer = (x ^ (k >> 2), y ^ ((k >> 1) & 1), c ^ (k & 1))
            sends.append(pltpu.make_async_remote_copy(
                src_ref=sm_ref, dst_ref=ls_ref.at[me], send_sem=send_sems.at[k - 1], recv_sem=recv_sems.at[k - 1],
                device_id=peer, device_id_type=MESH))
        for cp in sends:
            cp.start()
        for cp in sends:
            cp.wait_recv()
        for cp in sends:
            cp.wait_send()
        own.wait()

    vmem = pl.BlockSpec(memory_space=pltpu.VMEM)
    return pl.pallas_call(
        body, name="gather_small", in_specs=[vmem], out_specs=vmem,
        out_shape=jax.ShapeDtypeStruct((N_DEV,) + small.shape, F32),
        scratch_shapes=[pltpu.SemaphoreType.DMA((N_DEV - 1,)), pltpu.SemaphoreType.DMA((N_DEV - 1,)),
                        pltpu.SemaphoreType.DMA],
    )(small)


def _adamw(w, g, m, v):
    m = ADAM_B1 * m + (1.0 - ADAM_B1) * g
    v = ADAM_B2 * v + (1.0 - ADAM_B2) * (g * g)
    m_hat = m / (1.0 - ADAM_B1 ** ADAM_STEP)
    v_hat = v / (1.0 - ADAM_B2 ** ADAM_STEP)
    return -ADAM_LR * (m_hat / (jnp.sqrt(v_hat) + ADAM_EPS) + ADAM_WD * w), m, v


def _slot_sum(ref, own=None, me=None):
    g = None
    for i in range(N_DEV):
        term = ref[i].astype(F32)
        if own is not None:
            term = jnp.where(i == me, own, term)
        g = term if g is None else g + term
    return g


def _update_matrix(name, me, landed, own, w, m, v, rows):
    r, c = w.shape

    def body(me_ref, l_ref, own_ref, w_ref, m_ref, v_ref, g_ref, d_ref, nm_ref, nv_ref):
        g = _slot_sum(l_ref, own_ref[...].astype(F32), me_ref[0])
        g_ref[...] = g
        d_ref[...], nm_ref[...], nv_ref[...] = _adamw(w_ref[...], g, m_ref[...], v_ref[...])

    blk = pl.BlockSpec((rows, c), lambda i, me_ref: (i, 0))
    return pl.pallas_call(
        body, name=name,
        grid_spec=pltpu.PrefetchScalarGridSpec(
            num_scalar_prefetch=1, grid=(r // rows,),
            in_specs=[pl.BlockSpec((N_DEV, rows, c), lambda i, me_ref: (0, i, 0)),
                      pl.BlockSpec((None, rows, c), lambda i, me_ref: (me_ref[0], i, 0)), blk, blk, blk],
            out_specs=[blk] * 4),
        out_shape=[jax.ShapeDtypeStruct((r, c), F32)] * 4,
        compiler_params=_params("parallel"),
    )(me, landed, own, w, m, v)


def _update_small(landed, lb_logits, ws, ms, vs):
    def body(l_ref, lbl_ref, w_ref, m_ref, v_ref, g_ref, d_ref, nm_ref, nv_ref, loss_ref):
        tot = _slot_sum(l_ref)
        _, dlb = _lower_bound(lbl_ref[...])
        g_lb = tot[1:2, :SEC_W] * dlb
        g = jnp.concatenate([tot[0:1], jnp.concatenate([g_lb, -g_lb], axis=-1),
                             jnp.pad(tot[1:2, SEC_W:], ((0, 0), (0, SEC_W))), tot[2:3]], axis=0)
        g_ref[...] = g
        d_ref[...], nm_ref[...], nv_ref[...] = _adamw(w_ref[...], g, m_ref[...], v_ref[...])
        loss_ref[...] = tot[3:4, 0:1]

    vmem = pl.BlockSpec(memory_space=pltpu.VMEM)
    return pl.pallas_call(
        body, name="update_small", in_specs=[vmem] * 5, out_specs=[vmem] * 5,
        out_shape=[jax.ShapeDtypeStruct((4, D_MODEL), F32)] * 4 + [jax.ShapeDtypeStruct((1, 1), F32)],
    )(landed, lb_logits, ws, ms, vs)


def _pack_small(norm_w, lb_logits, hg_norm_w, final_norm_w):
    return jnp.concatenate([norm_w, lb_logits.reshape(1, D_MODEL),
                            jnp.pad(hg_norm_w, ((0, 0), (0, D_MODEL - SEC_W))), final_norm_w[None, :]], axis=0)


def _unpack_small(a):
    return a[0:1], a[1].reshape(2, SEC_W), a[2:3, :SEC_W], a[3]


def kernel(x, norm_w, w_in, hgrn_lb_logits, hg_norm_w, w_out, final_norm_w, loss_target, m_norm_w, m_w_in, m_hgrn_lb_logits, m_hg_norm_w, m_w_out, m_final_norm_w, v_norm_w, v_w_in, v_hgrn_lb_logits, v_hg_norm_w, v_w_out, v_final_norm_w):
    w_all = _gather_weights(w_in[0])
    wo_own = w_out[0].astype(MM)
    wo_own, w_all = lax.optimization_barrier((wo_own, w_all))
    *gathering_wo, wo_started = _exchange_start("gather_start_w_out", wo_own, whole=True)
    norm_w_then = norm_w + wo_started[:1, :1]

    def wo_all_after(after):
        _, landed = _exchange_wait("gather_wait_w_out", *gathering_wo, after, whole=True)
        return lax.dynamic_update_slice(landed, wo_own[None], (_me(), 0, 0)).reshape(D_MODEL, D_MODEL)

    flying_wo, flying_wi = [], []

    def start_w_out(gwo):
        *handles, token = _exchange_start("exchange_start_w_out", gwo.reshape(N_DEV, D_MODEL // N_DEV, D_MODEL))
        flying_wo.extend(handles)
        return token

    def start_w_in(gwi):
        *handles, token = _exchange_start("exchange_start_w_in", gwi)
        flying_wi.extend(handles)
        return token

    gx, small = _local_step(x[0], loss_target[0], norm_w_then, w_all, hgrn_lb_logits, hg_norm_w,
                            wo_all_after, final_norm_w, start_w_out, start_w_in)
    ls = _gather_small(small)
    gwo, lo = _exchange_wait("exchange_wait_w_out", *flying_wo, gx)
    gwi, li = _exchange_wait("exchange_wait_w_in", *flying_wi, gx)
    me = _me().astype(jnp.int32).reshape(1)
    g_wi, d_wi, nm_wi, nv_wi = _update_matrix("update_w_in", me, li, gwi, w_in[0], m_w_in[0], v_w_in[0], 256)
    g_wo, d_wo, nm_wo, nv_wo = _update_matrix("update_w_out", me, lo, gwo, w_out[0], m_w_out[0], v_w_out[0], 128)
    g_s, d_s, nm_s, nv_s, loss = _update_small(
        ls, hgrn_lb_logits, _pack_small(norm_w, hgrn_lb_logits, hg_norm_w, final_norm_w),
        _pack_small(m_norm_w, m_hgrn_lb_logits, m_hg_norm_w, m_final_norm_w),
        _pack_small(v_norm_w, v_hgrn_lb_logits, v_hg_norm_w, v_final_norm_w))
    outs = []
    for small_out, wi, wo in ((g_s, g_wi, g_wo), (d_s, d_wi, d_wo), (nm_s, nm_wi, nm_wo), (nv_s, nv_wi, nv_wo)):
        nw, lb, hg, fw = _unpack_small(small_out)
        outs += [nw, wi[None], lb, hg, wo[None], fw]
    return (loss[0, 0], gx[None], *outs)
```
